```python
import math
import jax, jax.numpy as jnp
from jax import lax
import numpy as np

D_MODEL = 1024
BATCH = 8
SEQ = 4096
DEPTH = 2

N_MOD = 6
EPS = 1e-6
SSM_EXPAND = 2
SSM_D_INNER = SSM_EXPAND * D_MODEL
SSM_HEAD_DIM = 64
SSM_HEADS = SSM_D_INNER // SSM_HEAD_DIM
SSM_GROUPS = 4
SSM_STATE = 128
SSM_CONV = 4
SSM_CHUNK = 128
SSM_CONV_DIM = SSM_D_INNER + 2 * SSM_GROUPS * SSM_STATE
ATT_HEAD_DIM = 64
ATT_KV_HEADS = D_MODEL // ATT_HEAD_DIM
ATT_PATTERNS = ((128, 1), (512, 4), (2048, 16))
ATT_GROUPS = len(ATT_PATTERNS)
ATT_Q_HEADS = ATT_GROUPS * ATT_KV_HEADS
ATT_WIDTH = ATT_KV_HEADS * ATT_HEAD_DIM
REL_BUCKETS = 32
REL_MAX_DIST = 2048
HY_SIZES = (SSM_D_INNER, SSM_CONV_DIM, SSM_HEADS, ATT_Q_HEADS * ATT_HEAD_DIM, ATT_WIDTH, ATT_WIDTH)
HY_SPLITS = tuple(int(v) for v in np.cumsum(HY_SIZES)[:-1])
HY_IN_DIM = sum(HY_SIZES)
HY_OUT_DIM = SSM_D_INNER + ATT_WIDTH
CONV_WIDTH = 31
FFN_HIDDEN = -(-8 * D_MODEL // (3 * 256)) * 256
N_EVEN = (DEPTH + 1) // 2
N_ODD = DEPTH // 2

kernel_name = "hybrid_ssd_dilated_conformer_block"


def rms_norm(x, g):
    xf = x.astype(jnp.float32)
    y = xf * lax.rsqrt(jnp.mean(xf * xf, -1, keepdims=True) + EPS)
    return (y * g.astype(jnp.float32)).astype(x.dtype)


def layer_norm(x, g, b):
    xf = x.astype(jnp.float32)
    mu = jnp.mean(xf, -1, keepdims=True)
    var = jnp.mean(jnp.square(xf - mu), -1, keepdims=True)
    y = (xf - mu) * lax.rsqrt(var + EPS)
    return (y * g.astype(jnp.float32) + b.astype(jnp.float32)).astype(x.dtype)


def causal_depthwise_conv(x, w, b):
    k = w.shape[0]
    out = lax.conv_general_dilated(
        x, w[:, None, :].astype(x.dtype), window_strides=(1,), padding=[(k - 1, 0)],
        dimension_numbers=('NWC', 'WIO', 'NWC'), feature_group_count=x.shape[-1])
    return out + b


def t5_bucket(dist):
    max_exact = REL_BUCKETS // 2
    n = jnp.maximum(dist, 1).astype(jnp.float32)
    large = max_exact + jnp.log(n / max_exact) / math.log(REL_MAX_DIST / max_exact) * (REL_BUCKETS - max_exact)
    large = jnp.minimum(large.astype(jnp.int32), REL_BUCKETS - 1)
    return jnp.where(dist < max_exact, dist, large)


def ssd_chunked(x, dt, A, Bm, Cm):
    f32 = jnp.float32
    b_, s, h, p = x.shape
    g, n = Bm.shape[2], Bm.shape[3]
    r, q = h // g, SSM_CHUNK
    nc = s // q
    x = x.astype(f32).reshape(b_, nc, q, g, r, p)
    dt = dt.astype(f32).reshape(b_, nc, q, g, r)
    Bm = Bm.astype(f32).reshape(b_, nc, q, g, n)
    Cm = Cm.astype(f32).reshape(b_, nc, q, g, n)
    a_cs = jnp.cumsum(dt * A.astype(f32).reshape(g, r), axis=2)
    xdt = x * dt[..., None]
    seg = a_cs[:, :, :, None] - a_cs[:, :, None, :]
    causal = jnp.tril(jnp.ones((q, q), bool))[:, :, None, None]
    decay = jnp.exp(jnp.where(causal, seg, -jnp.inf))
    cb = jnp.einsum('bclgn,bcsgn->bclsg', Cm, Bm)
    y_diag = jnp.einsum('bclsgr,bcsgrp->bclgrp', cb[..., None] * decay, xdt)
    decay_end = jnp.exp(a_cs[:, :, -1:] - a_cs)
    states = jnp.einsum('bcsgn,bcsgrp->bcgrpn', Bm, xdt * decay_end[..., None])
    chunk_decay = jnp.exp(a_cs[:, :, -1])

    def step(hs, inp):
        st, dec = inp
        return dec[..., None, None] * hs + st, hs

    h0 = jnp.zeros((b_, g, r, p, n), f32)
    _, prev = lax.scan(step, h0, (jnp.moveaxis(states, 1, 0), jnp.moveaxis(chunk_decay, 1, 0)))
    prev = jnp.moveaxis(prev, 0, 1)
    y_off = jnp.einsum('bclgn,bcgrpn->bclgrp', Cm, prev) * jnp.exp(a_cs)[..., None]
    return (y_diag + y_off).reshape(b_, s, h, p)


def dilated_branch(q, k, v, bias_tab, window, dil):
    b_, s, h, dh = q.shape
    blk = window // dil
    L = s // dil
    nb = -(-L // blk)
    lp = nb * blk

    def blocks(t):
        t = t.reshape(b_, L, dil, h, dh)
        t = jnp.pad(t, ((0, 0), (0, lp - L), (0, 0), (0, 0), (0, 0)))
        return t.reshape(b_, nb, blk, dil, h, dh)

    def band_keys(t):
        prev = jnp.pad(t, ((0, 0), (1, 0), (0, 0), (0, 0), (0, 0), (0, 0)))[:, :-1]
        return jnp.concatenate([prev, t], axis=2)

    qb = blocks(q)
    kw, vw = band_keys(blocks(k)), band_keys(blocks(v))
    i = jnp.arange(blk)[:, None]
    j = jnp.arange(2 * blk)[None, :]
    delta = blk + i - j
    band = (delta >= 0) & (delta <= blk)
    kpos = jnp.arange(nb)[:, None] * blk + jnp.arange(2 * blk)[None, :] - blk
    mask = band[None] & (kpos >= 0)[:, None, :]
    bias = jnp.transpose(bias_tab[t5_bucket(jnp.maximum(delta, 0) * dil)], (2, 0, 1)).astype(jnp.float32)
    sc = jnp.einsum('bnirhd,bnjrhd->bnrhij', qb, kw).astype(jnp.float32) * (dh ** -0.5) + bias
    sc = jnp.where(mask[None, :, None, None], sc, -jnp.inf)
    m = jnp.max(sc, -1, keepdims=True)
    pr = jnp.exp(sc - m)
    l = jnp.sum(pr, -1, keepdims=True)
    o = jnp.einsum('bnrhij,bnjrhd->bnirhd', (pr / l).astype(v.dtype), vw)
    lse = jnp.transpose((m + jnp.log(l))[..., 0], (0, 1, 4, 2, 3))
    lse = lse.reshape(b_, lp, dil, h)[:, :L].reshape(b_, s, h)
    o = o.reshape(b_, lp, dil, h, dh)[:, :L].reshape(b_, s, h, dh)
    return o, lse


def dilated_attention(q, k, v, rel_table):
    b_, s, _, h, dh = q.shape
    outs, lses = [], []
    for gi, (w, d) in enumerate(ATT_PATTERNS):
        o, lse = dilated_branch(q[:, :, gi], k, v, rel_table[:, gi * h:(gi + 1) * h], w, d)
        outs.append(o)
        lses.append(lse)
    wgt = jax.nn.softmax(jnp.stack(lses, 0), axis=0)
    o = jnp.einsum('gbsh,gbshd->bshd', wgt.astype(outs[0].dtype), jnp.stack(outs, 0))
    return o.reshape(b_, s, h * dh)


def hybrid_mixer(h, w_in, conv_w, conv_b, dt_bias, a_log, d_skip, ssm_norm_g, w_out, rel_table):
    b_, s, _ = h.shape
    z, xbc, dt_raw, q, k, v = jnp.split(h @ w_in, HY_SPLITS, axis=-1)
    xbc = jax.nn.silu(causal_depthwise_conv(xbc, conv_w, conv_b))
    xs, bm, cm = jnp.split(xbc, (SSM_D_INNER, SSM_D_INNER + SSM_GROUPS * SSM_STATE), axis=-1)
    xs = xs.reshape(b_, s, SSM_HEADS, SSM_HEAD_DIM)
    dt = jax.nn.softplus((dt_raw + dt_bias).astype(jnp.float32))
    A = -jnp.exp(a_log.astype(jnp.float32))
    y = ssd_chunked(xs, dt, A, bm.reshape(b_, s, SSM_GROUPS, SSM_STATE), cm.reshape(b_, s, SSM_GROUPS, SSM_STATE))
    y = y + d_skip.astype(jnp.float32)[:, None] * xs.astype(jnp.float32)
    y = y.reshape(b_, s, SSM_D_INNER).astype(h.dtype)
    y = rms_norm(y * jax.nn.silu(z), ssm_norm_g)
    att = dilated_attention(q.reshape(b_, s, ATT_GROUPS, ATT_KV_HEADS, ATT_HEAD_DIM),
                            k.reshape(b_, s, ATT_KV_HEADS, ATT_HEAD_DIM),
                            v.reshape(b_, s, ATT_KV_HEADS, ATT_HEAD_DIM), rel_table)
    return jnp.concatenate([y, att.astype(y.dtype)], axis=-1) @ w_out


def conformer_conv(h, w1, b1, w_dw, b_dw, ln_g, ln_b, w2, b2):
    a, gt = jnp.split(h @ w1 + b1, 2, axis=-1)
    u = a * jax.nn.sigmoid(gt)
    u = causal_depthwise_conv(u, w_dw, b_dw)
    u = jax.nn.silu(layer_norm(u, ln_g, ln_b))
    return u @ w2 + b2


def swiglu(h, wg, wu, wd):
    return (jax.nn.silu(h @ wg) * (h @ wu)) @ wd


def _fwd_setup_inputs(seed: int = 0) -> dict:
    key = jax.random.key(seed)
    ks = iter(jax.random.split(key, 40))
    f32 = jnp.float32

    def nrm(shape, scale=1.0):
        return jax.random.normal(next(ks), shape, f32) * scale

    D = D_MODEL
    dt0 = jnp.exp(jax.random.uniform(next(ks), (N_EVEN, SSM_HEADS), f32, math.log(1e-3), math.log(1e-1)))
    return {
        "x": nrm((BATCH, SEQ, D)),
        "c": nrm((BATCH, D)),
        "ada_w": nrm((DEPTH, D, N_MOD * D), 0.5 * D ** -0.5),
        "ada_b": nrm((DEPTH, N_MOD * D), 0.02),
        "norm_mix_g": 1.0 + nrm((DEPTH, D), 0.05),
        "norm_ffn_g": 1.0 + nrm((DEPTH, D), 0.05),
        "hy_w_in": nrm((N_EVEN, D, HY_IN_DIM), D ** -0.5),
        "hy_conv_w": nrm((N_EVEN, SSM_CONV, SSM_CONV_DIM), SSM_CONV ** -0.5),
        "hy_conv_b": nrm((N_EVEN, SSM_CONV_DIM), 0.02),
        "hy_dt_bias": dt0 + jnp.log(-jnp.expm1(-dt0)),
        "hy_a_log": jnp.log(jax.random.uniform(next(ks), (N_EVEN, SSM_HEADS), f32, 1.0, 16.0)),
        "hy_d_skip": 1.0 + nrm((N_EVEN, SSM_HEADS), 0.1),
        "hy_ssm_norm_g": 1.0 + nrm((N_EVEN, SSM_D_INNER), 0.05),
        "hy_w_out": nrm((N_EVEN, HY_OUT_DIM, D), HY_OUT_DIM ** -0.5),
        "rel_table": nrm((REL_BUCKETS, ATT_Q_HEADS), 0.2),
        "cv_w_pw1": nrm((N_ODD, D, 2 * D), D ** -0.5),
        "cv_b_pw1": nrm((N_ODD, 2 * D), 0.02),
        "cv_w_dw": nrm((N_ODD, CONV_WIDTH, D), CONV_WIDTH ** -0.5),
        "cv_b_dw": nrm((N_ODD, D), 0.02),
        "cv_ln_g": 1.0 + nrm((N_ODD, D), 0.05),
        "cv_ln_b": nrm((N_ODD, D), 0.02),
        "cv_w_pw2": nrm((N_ODD, D, D), D ** -0.5),
        "cv_b_pw2": nrm((N_ODD, D), 0.02),
        "ffn_w_gate": nrm((DEPTH, D, FFN_HIDDEN), D ** -0.5),
        "ffn_w_up": nrm((DEPTH, D, FFN_HIDDEN), D ** -0.5),
        "ffn_w_down": nrm((DEPTH, FFN_HIDDEN, D), FFN_HIDDEN ** -0.5),
        "final_norm_g": 1.0 + nrm((D,), 0.05),
    }


def _fwd_reference(x, c, ada_w, ada_b, norm_mix_g, norm_ffn_g, hy_w_in, hy_conv_w, hy_conv_b, hy_dt_bias,
              hy_a_log, hy_d_skip, hy_ssm_norm_g, hy_w_out, rel_table, cv_w_pw1, cv_b_pw1, cv_w_dw,
              cv_b_dw, cv_ln_g, cv_ln_b, cv_w_pw2, cv_b_pw2, ffn_w_gate, ffn_w_up, ffn_w_down,
              final_norm_g):
    cs = jax.nn.silu(c)
    for i in range(DEPTH):
        mod = cs @ ada_w[i] + ada_b[i]
        sh1, sc1, g1, sh2, sc2, g2 = [m[:, None, :] for m in jnp.split(mod, N_MOD, axis=-1)]
        h = rms_norm(x, norm_mix_g[i]) * (1 + sc1) + sh1
        j = i // 2
        if i % 2 == 0:
            mix = hybrid_mixer(h, hy_w_in[j], hy_conv_w[j], hy_conv_b[j], hy_dt_bias[j], hy_a_log[j],
                               hy_d_skip[j], hy_ssm_norm_g[j], hy_w_out[j], rel_table)
        else:
            mix = conformer_conv(h, cv_w_pw1[j], cv_b_pw1[j], cv_w_dw[j], cv_b_dw[j], cv_ln_g[j],
                                 cv_ln_b[j], cv_w_pw2[j], cv_b_pw2[j])
        x = x + g1 * mix
        h = rms_norm(x, norm_ffn_g[i]) * (1 + sc2) + sh2
        x = x + g2 * swiglu(h, ffn_w_gate[i], ffn_w_up[i], ffn_w_down[i])
    return rms_norm(x, final_norm_g)


import jax as _jax
import jax.numpy as _jnp

TWIN_FORMAT = 'train_step'
FWD_PARAMS = ['x', 'c', 'ada_w', 'ada_b', 'norm_mix_g', 'norm_ffn_g', 'hy_w_in', 'hy_conv_w', 'hy_conv_b', 'hy_dt_bias', 'hy_a_log', 'hy_d_skip', 'hy_ssm_norm_g', 'hy_w_out', 'rel_table', 'cv_w_pw1', 'cv_b_pw1', 'cv_w_dw', 'cv_b_dw', 'cv_ln_g', 'cv_ln_b', 'cv_w_pw2', 'cv_b_pw2', 'ffn_w_gate', 'ffn_w_up', 'ffn_w_down', 'final_norm_g']
TWIN_WEIGHTS = ['ada_w', 'ada_b', 'norm_mix_g', 'norm_ffn_g', 'hy_w_in', 'hy_conv_w', 'hy_conv_b', 'hy_dt_bias', 'hy_a_log', 'hy_d_skip', 'hy_ssm_norm_g', 'hy_w_out', 'rel_table', 'cv_w_pw1', 'cv_b_pw1', 'cv_w_dw', 'cv_b_dw', 'cv_ln_g', 'cv_ln_b', 'cv_w_pw2', 'cv_b_pw2', 'ffn_w_gate', 'ffn_w_up', 'ffn_w_down', 'final_norm_g']
TWIN_DIFF_INPUT = 'x'
TWIN_INPUTS = ['x', 'c', 'ada_w', 'ada_b', 'norm_mix_g', 'norm_ffn_g', 'hy_w_in', 'hy_conv_w', 'hy_conv_b', 'hy_dt_bias', 'hy_a_log', 'hy_d_skip', 'hy_ssm_norm_g', 'hy_w_out', 'rel_table', 'cv_w_pw1', 'cv_b_pw1', 'cv_w_dw', 'cv_b_dw', 'cv_ln_g', 'cv_ln_b', 'cv_w_pw2', 'cv_b_pw2', 'ffn_w_gate', 'ffn_w_up', 'ffn_w_down', 'final_norm_g', 'loss_target', 'm_ada_w', 'm_ada_b', 'm_norm_mix_g', 'm_norm_ffn_g', 'm_hy_w_in', 'm_hy_conv_w', 'm_hy_conv_b', 'm_hy_dt_bias', 'm_hy_a_log', 'm_hy_d_skip', 'm_hy_ssm_norm_g', 'm_hy_w_out', 'm_rel_table', 'm_cv_w_pw1', 'm_cv_b_pw1', 'm_cv_w_dw', 'm_cv_b_dw', 'm_cv_ln_g', 'm_cv_ln_b', 'm_cv_w_pw2', 'm_cv_b_pw2', 'm_ffn_w_gate', 'm_ffn_w_up', 'm_ffn_w_down', 'm_final_norm_g', 'v_ada_w', 'v_ada_b', 'v_norm_mix_g', 'v_norm_ffn_g', 'v_hy_w_in', 'v_hy_conv_w', 'v_hy_conv_b', 'v_hy_dt_bias', 'v_hy_a_log', 'v_hy_d_skip', 'v_hy_ssm_norm_g', 'v_hy_w_out', 'v_rel_table', 'v_cv_w_pw1', 'v_cv_b_pw1', 'v_cv_w_dw', 'v_cv_b_dw', 'v_cv_ln_g', 'v_cv_ln_b', 'v_cv_w_pw2', 'v_cv_b_pw2', 'v_ffn_w_gate', 'v_ffn_w_up', 'v_ffn_w_down', 'v_final_norm_g']
TWIN_OUTPUTS = ['loss', 'grad_x', 'grad_ada_w', 'grad_ada_b', 'grad_norm_mix_g', 'grad_norm_ffn_g', 'grad_hy_w_in', 'grad_hy_conv_w', 'grad_hy_conv_b', 'grad_hy_dt_bias', 'grad_hy_a_log', 'grad_hy_d_skip', 'grad_hy_ssm_norm_g', 'grad_hy_w_out', 'grad_rel_table', 'grad_cv_w_pw1', 'grad_cv_b_pw1', 'grad_cv_w_dw', 'grad_cv_b_dw', 'grad_cv_ln_g', 'grad_cv_ln_b', 'grad_cv_w_pw2', 'grad_cv_b_pw2', 'grad_ffn_w_gate', 'grad_ffn_w_up', 'grad_ffn_w_down', 'grad_final_norm_g', 'delta_ada_w', 'delta_ada_b', 'delta_norm_mix_g', 'delta_norm_ffn_g', 'delta_hy_w_in', 'delta_hy_conv_w', 'delta_hy_conv_b', 'delta_hy_dt_bias', 'delta_hy_a_log', 'delta_hy_d_skip', 'delta_hy_ssm_norm_g', 'delta_hy_w_out', 'delta_rel_table', 'delta_cv_w_pw1', 'delta_cv_b_pw1', 'delta_cv_w_dw', 'delta_cv_b_dw', 'delta_cv_ln_g', 'delta_cv_ln_b', 'delta_cv_w_pw2', 'delta_cv_b_pw2', 'delta_ffn_w_gate', 'delta_ffn_w_up', 'delta_ffn_w_down', 'delta_final_norm_g', 'new_m_ada_w', 'new_m_ada_b', 'new_m_norm_mix_g', 'new_m_norm_ffn_g', 'new_m_hy_w_in', 'new_m_hy_conv_w', 'new_m_hy_conv_b', 'new_m_hy_dt_bias', 'new_m_hy_a_log', 'new_m_hy_d_skip', 'new_m_hy_ssm_norm_g', 'new_m_hy_w_out', 'new_m_rel_table', 'new_m_cv_w_pw1', 'new_m_cv_b_pw1', 'new_m_cv_w_dw', 'new_m_cv_b_dw', 'new_m_cv_ln_g', 'new_m_cv_ln_b', 'new_m_cv_w_pw2', 'new_m_cv_b_pw2', 'new_m_ffn_w_gate', 'new_m_ffn_w_up', 'new_m_ffn_w_down', 'new_m_final_norm_g', 'new_v_ada_w', 'new_v_ada_b', 'new_v_norm_mix_g', 'new_v_norm_ffn_g', 'new_v_hy_w_in', 'new_v_hy_conv_w', 'new_v_hy_conv_b', 'new_v_hy_dt_bias', 'new_v_hy_a_log', 'new_v_hy_d_skip', 'new_v_hy_ssm_norm_g', 'new_v_hy_w_out', 'new_v_rel_table', 'new_v_cv_w_pw1', 'new_v_cv_b_pw1', 'new_v_cv_w_dw', 'new_v_cv_b_dw', 'new_v_cv_ln_g', 'new_v_cv_ln_b', 'new_v_cv_w_pw2', 'new_v_cv_b_pw2', 'new_v_ffn_w_gate', 'new_v_ffn_w_up', 'new_v_ffn_w_down', 'new_v_final_norm_g']
TWIN_LEAF_KINDS = {'loss': 'loss', 'grad_x': 'grad_x', 'grad_ada_w': 'grad_w', 'grad_ada_b': 'grad_w', 'grad_norm_mix_g': 'grad_w', 'grad_norm_ffn_g': 'grad_w', 'grad_hy_w_in': 'grad_w', 'grad_hy_conv_w': 'grad_w', 'grad_hy_conv_b': 'grad_w', 'grad_hy_dt_bias': 'grad_w', 'grad_hy_a_log': 'grad_w', 'grad_hy_d_skip': 'grad_w', 'grad_hy_ssm_norm_g': 'grad_w', 'grad_hy_w_out': 'grad_w', 'grad_rel_table': 'grad_w', 'grad_cv_w_pw1': 'grad_w', 'grad_cv_b_pw1': 'grad_w', 'grad_cv_w_dw': 'grad_w', 'grad_cv_b_dw': 'grad_w', 'grad_cv_ln_g': 'grad_w', 'grad_cv_ln_b': 'grad_w', 'grad_cv_w_pw2': 'grad_w', 'grad_cv_b_pw2': 'grad_w', 'grad_ffn_w_gate': 'grad_w', 'grad_ffn_w_up': 'grad_w', 'grad_ffn_w_down': 'grad_w', 'grad_final_norm_g': 'grad_w', 'delta_ada_w': 'delta_w', 'delta_ada_b': 'delta_w', 'delta_norm_mix_g': 'delta_w', 'delta_norm_ffn_g': 'delta_w', 'delta_hy_w_in': 'delta_w', 'delta_hy_conv_w': 'delta_w', 'delta_hy_conv_b': 'delta_w', 'delta_hy_dt_bias': 'delta_w', 'delta_hy_a_log': 'delta_w', 'delta_hy_d_skip': 'delta_w', 'delta_hy_ssm_norm_g': 'delta_w', 'delta_hy_w_out': 'delta_w', 'delta_rel_table': 'delta_w', 'delta_cv_w_pw1': 'delta_w', 'delta_cv_b_pw1': 'delta_w', 'delta_cv_w_dw': 'delta_w', 'delta_cv_b_dw': 'delta_w', 'delta_cv_ln_g': 'delta_w', 'delta_cv_ln_b': 'delta_w', 'delta_cv_w_pw2': 'delta_w', 'delta_cv_b_pw2': 'delta_w', 'delta_ffn_w_gate': 'delta_w', 'delta_ffn_w_up': 'delta_w', 'delta_ffn_w_down': 'delta_w', 'delta_final_norm_g': 'delta_w', 'new_m_ada_w': 'new_m', 'new_m_ada_b': 'new_m', 'new_m_norm_mix_g': 'new_m', 'new_m_norm_ffn_g': 'new_m', 'new_m_hy_w_in': 'new_m', 'new_m_hy_conv_w': 'new_m', 'new_m_hy_conv_b': 'new_m', 'new_m_hy_dt_bias': 'new_m', 'new_m_hy_a_log': 'new_m', 'new_m_hy_d_skip': 'new_m', 'new_m_hy_ssm_norm_g': 'new_m', 'new_m_hy_w_out': 'new_m', 'new_m_rel_table': 'new_m', 'new_m_cv_w_pw1': 'new_m', 'new_m_cv_b_pw1': 'new_m', 'new_m_cv_w_dw': 'new_m', 'new_m_cv_b_dw': 'new_m', 'new_m_cv_ln_g': 'new_m', 'new_m_cv_ln_b': 'new_m', 'new_m_cv_w_pw2': 'new_m', 'new_m_cv_b_pw2': 'new_m', 'new_m_ffn_w_gate': 'new_m', 'new_m_ffn_w_up': 'new_m', 'new_m_ffn_w_down': 'new_m', 'new_m_final_norm_g': 'new_m', 'new_v_ada_w': 'new_v', 'new_v_ada_b': 'new_v', 'new_v_norm_mix_g': 'new_v', 'new_v_norm_ffn_g': 'new_v', 'new_v_hy_w_in': 'new_v', 'new_v_hy_conv_w': 'new_v', 'new_v_hy_conv_b': 'new_v', 'new_v_hy_dt_bias': 'new_v', 'new_v_hy_a_log': 'new_v', 'new_v_hy_d_skip': 'new_v', 'new_v_hy_ssm_norm_g': 'new_v', 'new_v_hy_w_out': 'new_v', 'new_v_rel_table': 'new_v', 'new_v_cv_w_pw1': 'new_v', 'new_v_cv_b_pw1': 'new_v', 'new_v_cv_w_dw': 'new_v', 'new_v_cv_b_dw': 'new_v', 'new_v_cv_ln_g': 'new_v', 'new_v_cv_ln_b': 'new_v', 'new_v_cv_w_pw2': 'new_v', 'new_v_cv_b_pw2': 'new_v', 'new_v_ffn_w_gate': 'new_v', 'new_v_ffn_w_up': 'new_v', 'new_v_ffn_w_down': 'new_v', 'new_v_final_norm_g': 'new_v'}


def _forward(args):
    return _fwd_reference(*[args[k] for k in FWD_PARAMS])


def _output_shape():
    def fwd():
        inp = _fwd_setup_inputs(0)
        return _fwd_reference(*[inp[k] for k in FWD_PARAMS])
    out = _jax.eval_shape(fwd)
    return out.shape, out.dtype

N_MICROBATCH = 1
ADAM_LR = 0.001
ADAM_B1 = 0.9
ADAM_B2 = 0.999
ADAM_EPS = 1e-08
ADAM_WD = 0.01
ADAM_STEP = 10
PER_EXAMPLE_BATCH_AXIS = {'x': 0, 'c': 0, 'loss_target': 0}
SHARED_INPUTS = []
_WEIGHT_DTYPES = {'ada_w': _jnp.float32, 'ada_b': _jnp.float32, 'norm_mix_g': _jnp.float32, 'norm_ffn_g': _jnp.float32, 'hy_w_in': _jnp.float32, 'hy_conv_w': _jnp.float32, 'hy_conv_b': _jnp.float32, 'hy_dt_bias': _jnp.float32, 'hy_a_log': _jnp.float32, 'hy_d_skip': _jnp.float32, 'hy_ssm_norm_g': _jnp.float32, 'hy_w_out': _jnp.float32, 'rel_table': _jnp.float32, 'cv_w_pw1': _jnp.float32, 'cv_b_pw1': _jnp.float32, 'cv_w_dw': _jnp.float32, 'cv_b_dw': _jnp.float32, 'cv_ln_g': _jnp.float32, 'cv_ln_b': _jnp.float32, 'cv_w_pw2': _jnp.float32, 'cv_b_pw2': _jnp.float32, 'ffn_w_gate': _jnp.float32, 'ffn_w_up': _jnp.float32, 'ffn_w_down': _jnp.float32, 'final_norm_g': _jnp.float32}
MOMENT_SCALE = {'ada_w': 6.296958e-02, 'ada_b': 1.236839e-01, 'norm_mix_g': 5.207803e-02, 'norm_ffn_g': 5.173501e-02, 'hy_w_in': 2.185085e-02, 'hy_conv_w': 2.842607e-02, 'hy_conv_b': 3.810006e-02, 'hy_dt_bias': 8.459290e-02, 'hy_a_log': 1.889323e-01, 'hy_d_skip': 1.533557e-01, 'hy_ssm_norm_g': 3.279742e-02, 'hy_w_out': 4.657274e-02, 'rel_table': 4.407665e-03, 'cv_w_pw1': 2.378937e-02, 'cv_b_pw1': 2.723230e-02, 'cv_w_dw': 3.138937e-02, 'cv_b_dw': 6.088156e-02, 'cv_ln_g': 3.883107e-02, 'cv_ln_b': 3.545952e-02, 'cv_w_pw2': 3.114945e-02, 'cv_b_pw2': 6.099127e-02, 'ffn_w_gate': 2.236312e-02, 'ffn_w_up': 2.170751e-02, 'ffn_w_down': 3.605072e-02, 'final_norm_g': 3.207659e+01}


def _to_microbatches(a, axis):
    t = _jnp.moveaxis(a, axis, 0)
    t = t.reshape((N_MICROBATCH, t.shape[0] // N_MICROBATCH) + t.shape[1:])
    return _jnp.moveaxis(t, 1, axis + 1)


def setup_inputs(seed: int = 0) -> dict:
    inp = _fwd_setup_inputs(seed)
    key = _jax.random.fold_in(_jax.random.key(seed), 7919)
    shape, _ = _output_shape()
    out = dict(inp)
    out["loss_target"] = _jax.random.normal(_jax.random.fold_in(key, 0), shape, _jnp.float32)
    for i, name in enumerate(TWIN_WEIGHTS):
        w = inp[name].astype(_jnp.float32)
        if MOMENT_SCALE is None:
            s = _jnp.sqrt(_jnp.mean(_jnp.square(w)) + 1e-30)
        else:
            s = MOMENT_SCALE[name]
        km, kv = _jax.random.split(_jax.random.fold_in(key, i + 1))
        out[name] = w
        out["m_" + name] = s * _jax.random.normal(km, w.shape, _jnp.float32)
        out["v_" + name] = (s * s) * _jax.random.uniform(kv, w.shape, _jnp.float32, 0.5, 1.5)
    if N_MICROBATCH > 1:
        for name, axis in PER_EXAMPLE_BATCH_AXIS.items():
            out[name] = _to_microbatches(out[name], axis)
    return {'x': out['x'], 'c': out['c'], 'ada_w': out['ada_w'], 'ada_b': out['ada_b'], 'norm_mix_g': out['norm_mix_g'], 'norm_ffn_g': out['norm_ffn_g'], 'hy_w_in': out['hy_w_in'], 'hy_conv_w': out['hy_conv_w'], 'hy_conv_b': out['hy_conv_b'], 'hy_dt_bias': out['hy_dt_bias'], 'hy_a_log': out['hy_a_log'], 'hy_d_skip': out['hy_d_skip'], 'hy_ssm_norm_g': out['hy_ssm_norm_g'], 'hy_w_out': out['hy_w_out'], 'rel_table': out['rel_table'], 'cv_w_pw1': out['cv_w_pw1'], 'cv_b_pw1': out['cv_b_pw1'], 'cv_w_dw': out['cv_w_dw'], 'cv_b_dw': out['cv_b_dw'], 'cv_ln_g': out['cv_ln_g'], 'cv_ln_b': out['cv_ln_b'], 'cv_w_pw2': out['cv_w_pw2'], 'cv_b_pw2': out['cv_b_pw2'], 'ffn_w_gate': out['ffn_w_gate'], 'ffn_w_up': out['ffn_w_up'], 'ffn_w_down': out['ffn_w_down'], 'final_norm_g': out['final_norm_g'], 'loss_target': out['loss_target'], 'm_ada_w': out['m_ada_w'], 'm_ada_b': out['m_ada_b'], 'm_norm_mix_g': out['m_norm_mix_g'], 'm_norm_ffn_g': out['m_norm_ffn_g'], 'm_hy_w_in': out['m_hy_w_in'], 'm_hy_conv_w': out['m_hy_conv_w'], 'm_hy_conv_b': out['m_hy_conv_b'], 'm_hy_dt_bias': out['m_hy_dt_bias'], 'm_hy_a_log': out['m_hy_a_log'], 'm_hy_d_skip': out['m_hy_d_skip'], 'm_hy_ssm_norm_g': out['m_hy_ssm_norm_g'], 'm_hy_w_out': out['m_hy_w_out'], 'm_rel_table': out['m_rel_table'], 'm_cv_w_pw1': out['m_cv_w_pw1'], 'm_cv_b_pw1': out['m_cv_b_pw1'], 'm_cv_w_dw': out['m_cv_w_dw'], 'm_cv_b_dw': out['m_cv_b_dw'], 'm_cv_ln_g': out['m_cv_ln_g'], 'm_cv_ln_b': out['m_cv_ln_b'], 'm_cv_w_pw2': out['m_cv_w_pw2'], 'm_cv_b_pw2': out['m_cv_b_pw2'], 'm_ffn_w_gate': out['m_ffn_w_gate'], 'm_ffn_w_up': out['m_ffn_w_up'], 'm_ffn_w_down': out['m_ffn_w_down'], 'm_final_norm_g': out['m_final_norm_g'], 'v_ada_w': out['v_ada_w'], 'v_ada_b': out['v_ada_b'], 'v_norm_mix_g': out['v_norm_mix_g'], 'v_norm_ffn_g': out['v_norm_ffn_g'], 'v_hy_w_in': out['v_hy_w_in'], 'v_hy_conv_w': out['v_hy_conv_w'], 'v_hy_conv_b': out['v_hy_conv_b'], 'v_hy_dt_bias': out['v_hy_dt_bias'], 'v_hy_a_log': out['v_hy_a_log'], 'v_hy_d_skip': out['v_hy_d_skip'], 'v_hy_ssm_norm_g': out['v_hy_ssm_norm_g'], 'v_hy_w_out': out['v_hy_w_out'], 'v_rel_table': out['v_rel_table'], 'v_cv_w_pw1': out['v_cv_w_pw1'], 'v_cv_b_pw1': out['v_cv_b_pw1'], 'v_cv_w_dw': out['v_cv_w_dw'], 'v_cv_b_dw': out['v_cv_b_dw'], 'v_cv_ln_g': out['v_cv_ln_g'], 'v_cv_ln_b': out['v_cv_ln_b'], 'v_cv_w_pw2': out['v_cv_w_pw2'], 'v_cv_b_pw2': out['v_cv_b_pw2'], 'v_ffn_w_gate': out['v_ffn_w_gate'], 'v_ffn_w_up': out['v_ffn_w_up'], 'v_ffn_w_down': out['v_ffn_w_down'], 'v_final_norm_g': out['v_final_norm_g']}


def _loss(weights, diff, rest, loss_target):
    with _jax.named_scope("forward"):
        args = {**rest, TWIN_DIFF_INPUT: diff, **{k: w.astype(_WEIGHT_DTYPES[k]) for k, w in weights.items()}}
        y = _forward(args)
    with _jax.named_scope("loss_head"):
        err = _jnp.square(y.astype(_jnp.float32) - loss_target)
        return 0.5 * _jnp.sum(_jnp.mean(err, axis=-1)) if err.ndim else 0.5 * err


def _adamw(w, g, m, v):
    m = ADAM_B1 * m + (1.0 - ADAM_B1) * g
    v = ADAM_B2 * v + (1.0 - ADAM_B2) * _jnp.square(g)
    m_hat = m / (1.0 - ADAM_B1 ** ADAM_STEP)
    v_hat = v / (1.0 - ADAM_B2 ** ADAM_STEP)
    delta = -ADAM_LR * (m_hat / (_jnp.sqrt(v_hat) + ADAM_EPS) + ADAM_WD * w)
    return delta, m, v


def reference(x, c, ada_w, ada_b, norm_mix_g, norm_ffn_g, hy_w_in, hy_conv_w, hy_conv_b, hy_dt_bias, hy_a_log, hy_d_skip, hy_ssm_norm_g, hy_w_out, rel_table, cv_w_pw1, cv_b_pw1, cv_w_dw, cv_b_dw, cv_ln_g, cv_ln_b, cv_w_pw2, cv_b_pw2, ffn_w_gate, ffn_w_up, ffn_w_down, final_norm_g, loss_target, m_ada_w, m_ada_b, m_norm_mix_g, m_norm_ffn_g, m_hy_w_in, m_hy_conv_w, m_hy_conv_b, m_hy_dt_bias, m_hy_a_log, m_hy_d_skip, m_hy_ssm_norm_g, m_hy_w_out, m_rel_table, m_cv_w_pw1, m_cv_b_pw1, m_cv_w_dw, m_cv_b_dw, m_cv_ln_g, m_cv_ln_b, m_cv_w_pw2, m_cv_b_pw2, m_ffn_w_gate, m_ffn_w_up, m_ffn_w_down, m_final_norm_g, v_ada_w, v_ada_b, v_norm_mix_g, v_norm_ffn_g, v_hy_w_in, v_hy_conv_w, v_hy_conv_b, v_hy_dt_bias, v_hy_a_log, v_hy_d_skip, v_hy_ssm_norm_g, v_hy_w_out, v_rel_table, v_cv_w_pw1, v_cv_b_pw1, v_cv_w_dw, v_cv_b_dw, v_cv_ln_g, v_cv_ln_b, v_cv_w_pw2, v_cv_b_pw2, v_ffn_w_gate, v_ffn_w_up, v_ffn_w_down, v_final_norm_g):
    given = dict(x=x, c=c, ada_w=ada_w, ada_b=ada_b, norm_mix_g=norm_mix_g, norm_ffn_g=norm_ffn_g, hy_w_in=hy_w_in, hy_conv_w=hy_conv_w, hy_conv_b=hy_conv_b, hy_dt_bias=hy_dt_bias, hy_a_log=hy_a_log, hy_d_skip=hy_d_skip, hy_ssm_norm_g=hy_ssm_norm_g, hy_w_out=hy_w_out, rel_table=rel_table, cv_w_pw1=cv_w_pw1, cv_b_pw1=cv_b_pw1, cv_w_dw=cv_w_dw, cv_b_dw=cv_b_dw, cv_ln_g=cv_ln_g, cv_ln_b=cv_ln_b, cv_w_pw2=cv_w_pw2, cv_b_pw2=cv_b_pw2, ffn_w_gate=ffn_w_gate, ffn_w_up=ffn_w_up, ffn_w_down=ffn_w_down, final_norm_g=final_norm_g, loss_target=loss_target, m_ada_w=m_ada_w, m_ada_b=m_ada_b, m_norm_mix_g=m_norm_mix_g, m_norm_ffn_g=m_norm_ffn_g, m_hy_w_in=m_hy_w_in, m_hy_conv_w=m_hy_conv_w, m_hy_conv_b=m_hy_conv_b, m_hy_dt_bias=m_hy_dt_bias, m_hy_a_log=m_hy_a_log, m_hy_d_skip=m_hy_d_skip, m_hy_ssm_norm_g=m_hy_ssm_norm_g, m_hy_w_out=m_hy_w_out, m_rel_table=m_rel_table, m_cv_w_pw1=m_cv_w_pw1, m_cv_b_pw1=m_cv_b_pw1, m_cv_w_dw=m_cv_w_dw, m_cv_b_dw=m_cv_b_dw, m_cv_ln_g=m_cv_ln_g, m_cv_ln_b=m_cv_ln_b, m_cv_w_pw2=m_cv_w_pw2, m_cv_b_pw2=m_cv_b_pw2, m_ffn_w_gate=m_ffn_w_gate, m_ffn_w_up=m_ffn_w_up, m_ffn_w_down=m_ffn_w_down, m_final_norm_g=m_final_norm_g, v_ada_w=v_ada_w, v_ada_b=v_ada_b, v_norm_mix_g=v_norm_mix_g, v_norm_ffn_g=v_norm_ffn_g, v_hy_w_in=v_hy_w_in, v_hy_conv_w=v_hy_conv_w, v_hy_conv_b=v_hy_conv_b, v_hy_dt_bias=v_hy_dt_bias, v_hy_a_log=v_hy_a_log, v_hy_d_skip=v_hy_d_skip, v_hy_ssm_norm_g=v_hy_ssm_norm_g, v_hy_w_out=v_hy_w_out, v_rel_table=v_rel_table, v_cv_w_pw1=v_cv_w_pw1, v_cv_b_pw1=v_cv_b_pw1, v_cv_w_dw=v_cv_w_dw, v_cv_b_dw=v_cv_b_dw, v_cv_ln_g=v_cv_ln_g, v_cv_ln_b=v_cv_ln_b, v_cv_w_pw2=v_cv_w_pw2, v_cv_b_pw2=v_cv_b_pw2, v_ffn_w_gate=v_ffn_w_gate, v_ffn_w_up=v_ffn_w_up, v_ffn_w_down=v_ffn_w_down, v_final_norm_g=v_final_norm_g)
    weights = {n: given[n] for n in TWIN_WEIGHTS}
    shared = {n: given[n] for n in SHARED_INPUTS}
    per_example = {n: given[n] for n in ['x', 'c']}
    grad_fn = _jax.value_and_grad(_loss, argnums=(0, 1))

    def one_microbatch(ex, loss_target):
        ex = dict(ex)
        diff = ex.pop(TWIN_DIFF_INPUT)
        return grad_fn(weights, diff, {**shared, **ex}, loss_target)

    if N_MICROBATCH == 1:
        loss, (grad_w, grad_x) = one_microbatch(per_example, given["loss_target"])
    else:
        def body(carry, xs):
            loss_sum, grad_sum = carry
            l_k, (gw_k, gx_k) = one_microbatch(xs[0], xs[1])
            with _jax.named_scope("update"):
                return (loss_sum + l_k, _jax.tree.map(_jnp.add, grad_sum, gw_k)), gx_k

        init = (_jnp.zeros((), _jnp.float32), _jax.tree.map(_jnp.zeros_like, weights))
        (loss, grad_w), grad_x = _jax.lax.scan(body, init, (per_example, given["loss_target"]))
    with _jax.named_scope("update"):
        delta_w, new_m, new_v = {}, {}, {}
        for n in TWIN_WEIGHTS:
            delta_w[n], new_m[n], new_v[n] = _adamw(weights[n], grad_w[n], given["m_" + n], given["v_" + n])
    return (loss, grad_x, *[grad_w[n] for n in TWIN_WEIGHTS], *[delta_w[n] for n in TWIN_WEIGHTS],
            *[new_m[n] for n in TWIN_WEIGHTS], *[new_v[n] for n in TWIN_WEIGHTS])
```

```python
import functools
import math

import jax
import jax.numpy as jnp
import numpy as np
from jax import lax
from jax.experimental import pallas as pl
from jax.experimental.pallas import tpu as pltpu

F32 = jnp.float32
BF16 = jnp.bfloat16
MESH = pl.DeviceIdType.MESH

D = 1024
S = 4096
EPS = 1e-6
SSM_INNER = 2048
SSM_HEADS = 32
SSM_HDIM = 64
SSM_GROUPS = 4
SSM_STATE = 128
SSM_CONVK = 4
SSM_CONV_DIM = 3072
CHUNK = 128
N_CHUNKS = S // CHUNK
ATT_HEADS = 16
ATT_HDIM = 64
ATT_PATTERNS = ((128, 1), (512, 4), (2048, 16))
ATT_BLK = 128
REL_BUCKETS = 32
REL_MAX_DIST = 2048
CONV_WIDTH = 31
FFN_HIDDEN = 2816
N_CHIPS = 4
N_DEV = 8
ADAM_LR, ADAM_B1, ADAM_B2, ADAM_EPS, ADAM_WD, ADAM_STEP = 0.001, 0.9, 0.999, 1e-08, 0.01, 10

VMEM_LIMIT_BYTES = 56 * 1024 * 1024
LANES = 128


def _cparams(*sem):
    return pltpu.CompilerParams(dimension_semantics=sem, vmem_limit_bytes=VMEM_LIMIT_BYTES)


def _pick(n, cap, mult=LANES):
    best = None
    for t in range(mult, min(n, cap) + 1, mult):
        if n % t == 0:
            best = t
    return best or n


def _dot(a, b, ca, cb):
    return lax.dot_general(a.astype(BF16), b.astype(BF16), (((ca,), (cb,)), ((), ())), preferred_element_type=F32)


@jax.custom_vjp
def mm(a, b):
    return _dot(a, b, 1, 0)


def _mm_fwd(a, b):
    return _dot(a, b, 1, 0), (a, b)


def _mm_bwd(res, g):
    a, b = res
    return _dot(g, b, 1, 1).astype(a.dtype), _dot(a, g, 0, 0).astype(b.dtype)


mm.defvjp(_mm_fwd, _mm_bwd)


@jax.custom_vjp
def mm_nt(a, b):
    return _dot(a, b, 1, 1)


def _mm_nt_fwd(a, b):
    return _dot(a, b, 1, 1), (a, b)


def _mm_nt_bwd(res, g):
    a, b = res
    return _dot(g, b, 1, 0).astype(a.dtype), _dot(g, a, 0, 0).astype(b.dtype)


mm_nt.defvjp(_mm_nt_fwd, _mm_nt_bwd)


@jax.custom_vjp
def mm_tn(a, b):
    return _dot(a, b, 0, 0)


def _mm_tn_fwd(a, b):
    return _dot(a, b, 0, 0), (a, b)


def _mm_tn_bwd(res, g):
    a, b = res
    return _dot(b, g, 1, 1).astype(a.dtype), _dot(a, g, 1, 0).astype(b.dtype)


mm_tn.defvjp(_mm_tn_fwd, _mm_tn_bwd)


def matmul(a, b, *, mode, out_dtype, name, n=None, b_off=0, tm_cap=512, tn_cap=512, tk_cap=1536):
    if mode == "tn":
        k_dim, m_dim = a.shape
    else:
        m_dim, k_dim = a.shape
    n_dim = n if n is not None else (b.shape[0] if mode == "nt" else b.shape[1])
    tm = m_dim if m_dim < LANES else _pick(m_dim, tm_cap)
    tn = _pick(n_dim, tn_cap)
    tk = k_dim if k_dim < LANES else _pick(k_dim, tk_cap)
    assert m_dim % tm == 0 and n_dim % tn == 0 and k_dim % tk == 0 and b_off % tn == 0
    nk = k_dim // tk
    off = b_off // tn
    if mode == "nn":
        a_spec = pl.BlockSpec((tm, tk), lambda i, j, k: (i, k))
        b_spec = pl.BlockSpec((tk, tn), lambda i, j, k: (k, j))
        ca, cb = 1, 0
    elif mode == "nt":
        a_spec = pl.BlockSpec((tm, tk), lambda i, j, k: (i, k))
        b_spec = pl.BlockSpec((tn, tk), lambda i, j, k: (j + off, k))
        ca, cb = 1, 1
    else:
        a_spec = pl.BlockSpec((tk, tm), lambda i, j, k: (k, i))
        b_spec = pl.BlockSpec((tk, tn), lambda i, j, k: (k, j))
        ca, cb = 0, 0

    def body(a_ref, b_ref, o_ref, acc_ref):
        part = _dot(a_ref[...], b_ref[...], ca, cb)
        if nk == 1:
            o_ref[...] = part.astype(o_ref.dtype)
        else:
            k = pl.program_id(2)

            @pl.when(k == 0)
            def _():
                acc_ref[...] = part

            @pl.when(k > 0)
            def _():
                acc_ref[...] += part

            @pl.when(k == nk - 1)
            def _():
                o_ref[...] = acc_ref[...].astype(o_ref.dtype)

    return pl.pallas_call(
        body, name=name,
        out_shape=jax.ShapeDtypeStruct((m_dim, n_dim), out_dtype),
        grid=(m_dim // tm, n_dim // tn, nk),
        in_specs=[a_spec, b_spec],
        out_specs=pl.BlockSpec((tm, tn), lambda i, j, k: (i, j)),
        scratch_shapes=[pltpu.VMEM((tm, tn), F32)],
        compiler_params=_cparams("parallel", "parallel", "arbitrary"),
    )(a, b)


def _f32(xs):
    return [x.astype(F32) for x in xs]


def rowmap(f, rows, consts, out_dtypes, *, name, tr=256):
    r_dim = rows[0].shape[0]
    nr, nc = len(rows), len(consts)
    outs = jax.eval_shape(lambda *xs: f(*xs), *[jax.ShapeDtypeStruct((tr, x.shape[1]), F32) for x in rows],
                          *[jax.ShapeDtypeStruct(x.shape, F32) for x in consts])

    def body(*refs):
        res = f(*_f32([r[...] for r in refs[:nr + nc]]))
        for o_ref, o in zip(refs[nr + nc:], res, strict=True):
            o_ref[...] = o.astype(o_ref.dtype)

    return pl.pallas_call(
        body, name=name,
        out_shape=[jax.ShapeDtypeStruct((r_dim, o.shape[1]), dt) for o, dt in zip(outs, out_dtypes, strict=True)],
        grid=(r_dim // tr,),
        in_specs=[pl.BlockSpec((tr, x.shape[1]), lambda i: (i, 0)) for x in rows]
        + [pl.BlockSpec(x.shape, lambda i: (0, 0)) for x in consts],
        out_specs=[pl.BlockSpec((tr, o.shape[1]), lambda i: (i, 0)) for o in outs],
        compiler_params=_cparams("parallel"),
    )(*rows, *consts)


def rowmap_bwd(f, rows, consts, cts, *, name, row_grad, row_dtypes=None, tr=256, emit=(), row_add=None):
    r_dim = rows[0].shape[0]
    nr, nc, nct = len(rows), len(consts), len(cts)
    gi = [i for i, flag in enumerate(row_grad) if flag]
    row_dtypes = row_dtypes or [F32] * len(gi)
    row_add = row_add or [None] * len(gi)
    adds = [a for a in row_add if a is not None]
    outs = jax.eval_shape(lambda *xs: f(*xs), *[jax.ShapeDtypeStruct((tr, x.shape[1]), F32) for x in rows],
                          *[jax.ShapeDtypeStruct(x.shape, F32) for x in consts])

    def body(*refs):
        ins = _f32([r[...] for r in refs[:nr + nc]])
        ct = _f32([r[...] for r in refs[nr + nc:nr + nc + nct]])
        add_refs = list(refs[nr + nc + nct:nr + nc + nct + len(adds)])
        o_refs = refs[nr + nc + nct + len(adds):]
        res, vjp = jax.vjp(f, *ins)
        grads = vjp(tuple(ct))
        for o_ref, i, a in zip(o_refs[:len(gi)], gi, row_add):
            g = grads[i] if a is None else grads[i] + add_refs.pop(0)[...].astype(F32)
            o_ref[...] = g.astype(o_ref.dtype)
        first = pl.program_id(0) == 0
        for o_ref, g in zip(o_refs[len(gi):len(gi) + nc], grads[nr:]):
            @pl.when(first)
            def _(o_ref=o_ref, g=g):
                o_ref[...] = g

            @pl.when(jnp.logical_not(first))
            def _(o_ref=o_ref, g=g):
                o_ref[...] += g
        for o_ref, i in zip(o_refs[len(gi) + nc:], emit):
            o_ref[...] = res[i].astype(o_ref.dtype)

    out_shape = ([jax.ShapeDtypeStruct(rows[i].shape, dt) for i, dt in zip(gi, row_dtypes, strict=True)]
                 + [jax.ShapeDtypeStruct(x.shape, F32) for x in consts]
                 + [jax.ShapeDtypeStruct((r_dim, outs[i].shape[1]), F32) for i in emit])
    out_specs = ([pl.BlockSpec((tr, rows[i].shape[1]), lambda i_: (i_, 0)) for i in gi]
                 + [pl.BlockSpec(x.shape, lambda i_: (0, 0)) for x in consts]
                 + [pl.BlockSpec((tr, outs[i].shape[1]), lambda i_: (i_, 0)) for i in emit])
    res = pl.pallas_call(
        body, name=name,
        out_shape=out_shape,
        grid=(r_dim // tr,),
        in_specs=[pl.BlockSpec((tr, x.shape[1]), lambda i: (i, 0)) for x in rows]
        + [pl.BlockSpec(x.shape, lambda i: (0, 0)) for x in consts]
        + [pl.BlockSpec((tr, x.shape[1]), lambda i: (i, 0)) for x in list(cts) + adds],
        out_specs=out_specs,
        compiler_params=_cparams("arbitrary"),
    )(*rows, *consts, *cts, *adds)
    return res[:len(gi)], res[len(gi):len(gi) + nc], res[len(gi) + nc:]


CONV_HALO = 32
CONV_CHUNK = 256


def conv_fwd(x, w, b, *, name, cb=256):
    s_dim, c_dim = x.shape
    taps = w.shape[0]
    assert taps - 1 <= CONV_HALO and s_dim % CONV_CHUNK == 0 and c_dim % cb == 0
    n_chunks = s_dim // CONV_CHUNK
    ext = CONV_CHUNK + CONV_HALO

    def body(x_ref, w_ref, b_ref, o_ref, xp_ref):
        xp_ref[pl.ds(0, CONV_HALO), :] = jnp.zeros((CONV_HALO, cb), F32)
        xp_ref[pl.ds(CONV_HALO, s_dim), :] = x_ref[...].astype(F32)
        wv = w_ref[...].astype(F32)
        bv = b_ref[...].astype(F32)

        def chunk(t, carry):
            base = pl.multiple_of(t * CONV_CHUNK, CONV_CHUNK)
            xe = xp_ref[pl.ds(base, ext), :]
            acc = jnp.broadcast_to(bv, (CONV_CHUNK, cb))
            for j in range(taps):
                sh = xe if j == 0 else pltpu.roll(xe, shift=j, axis=0)
                acc = acc + wv[taps - 1 - j:taps - j, :] * sh[CONV_HALO:, :]
            o_ref[pl.ds(base, CONV_CHUNK), :] = acc
            return carry

        lax.fori_loop(0, n_chunks, chunk, 0)

    return pl.pallas_call(
        body, name=name,
        out_shape=jax.ShapeDtypeStruct((s_dim, c_dim), F32),
        grid=(c_dim // cb,),
        in_specs=[pl.BlockSpec((s_dim, cb), lambda i: (0, i)), pl.BlockSpec((taps, cb), lambda i: (0, i)),
                  pl.BlockSpec((1, cb), lambda i: (0, i))],
        out_specs=pl.BlockSpec((s_dim, cb), lambda i: (0, i)),
        scratch_shapes=[pltpu.VMEM((s_dim + CONV_HALO, cb), F32)],
        compiler_params=_cparams("parallel"),
    )(x, w, b)


def conv_bwd(x, w, g, *, name, cb=256):
    s_dim, c_dim = x.shape
    taps = w.shape[0]
    n_chunks = s_dim // CONV_CHUNK
    ext = CONV_CHUNK + CONV_HALO
    taps_pad = -(-taps // 8) * 8

    def body(x_ref, w_ref, g_ref, dx_ref, dw_ref, db_ref, xp_ref, gp_ref, acc_ref):
        xp_ref[pl.ds(0, CONV_HALO), :] = jnp.zeros((CONV_HALO, cb), F32)
        xp_ref[pl.ds(CONV_HALO, s_dim), :] = x_ref[...].astype(F32)
        gp_ref[pl.ds(0, s_dim), :] = g_ref[...].astype(F32)
        gp_ref[pl.ds(s_dim, CONV_HALO), :] = jnp.zeros((CONV_HALO, cb), F32)
        acc_ref[...] = jnp.zeros_like(acc_ref)
        wv = w_ref[...].astype(F32)

        def chunk(t, carry):
            base = pl.multiple_of(t * CONV_CHUNK, CONV_CHUNK)
            xe = xp_ref[pl.ds(base, ext), :]
            ge = gp_ref[pl.ds(base, ext), :]
            gc = ge[:CONV_CHUNK, :]
            dx = jnp.zeros((CONV_CHUNK, cb), F32)
            for j in range(taps):
                xs = xe if j == 0 else pltpu.roll(xe, shift=j, axis=0)
                gs = ge if j == 0 else pltpu.roll(ge, shift=ext - j, axis=0)
                k = taps - 1 - j
                dx = dx + wv[k:k + 1, :] * gs[:CONV_CHUNK, :]
                acc_ref[k:k + 1, :] += jnp.sum(gc * xs[CONV_HALO:, :], axis=0, keepdims=True)
            acc_ref[taps_pad:taps_pad + 1, :] += jnp.sum(gc, axis=0, keepdims=True)
            dx_ref[pl.ds(base, CONV_CHUNK), :] = dx
            return carry

        lax.fori_loop(0, n_chunks, chunk, 0)
        dw_ref[...] = acc_ref[0:taps, :]
        db_ref[...] = acc_ref[taps_pad:taps_pad + 1, :]

    return pl.pallas_call(
        body, name=name,
        out_shape=[jax.ShapeDtypeStruct((s_dim, c_dim), F32), jax.ShapeDtypeStruct((taps, c_dim), F32),
                   jax.ShapeDtypeStruct((1, c_dim), F32)],
        grid=(c_dim // cb,),
        in_specs=[pl.BlockSpec((s_dim, cb), lambda i: (0, i)), pl.BlockSpec((taps, cb), lambda i: (0, i)),
                  pl.BlockSpec((s_dim, cb), lambda i: (0, i))],
        out_specs=[pl.BlockSpec((s_dim, cb), lambda i: (0, i)), pl.BlockSpec((taps, cb), lambda i: (0, i)),
                   pl.BlockSpec((1, cb), lambda i: (0, i))],
        scratch_shapes=[pltpu.VMEM((s_dim + CONV_HALO, cb), F32), pltpu.VMEM((s_dim + CONV_HALO, cb), F32),
                        pltpu.VMEM((taps_pad + 8, cb), F32)],
        compiler_params=_cparams("parallel"),
    )(x, w, g)


def _iota2(n, axis):
    return lax.broadcasted_iota(jnp.int32, (n, n), axis)


def _to_col(row):
    n = row.shape[1]
    return jnp.sum(jnp.where(_iota2(n, 0) == _iota2(n, 1), jnp.broadcast_to(row, (n, n)), 0.0), axis=1, keepdims=True)


def _softplus(x):
    return jnp.maximum(x, 0.0) + jnp.log(1.0 + jnp.exp(-jnp.abs(x)))


def ssd_head(x, dtraw_row, dt_bias, a_log, dskip, bm, cm, prev):
    q = x.shape[0]
    li, si = _iota2(q, 0), _iota2(q, 1)
    dt_row = _softplus(dtraw_row + dt_bias)
    a_row = dt_row * (-jnp.exp(a_log))
    a_col = _to_col(a_row)
    acs_col = jnp.sum(jnp.where(si <= li, jnp.broadcast_to(a_row, (q, q)), 0.0), axis=1, keepdims=True)
    acs_row = jnp.sum(jnp.where(li <= si, jnp.broadcast_to(a_col, (q, q)), 0.0), axis=0, keepdims=True)
    total = jnp.sum(a_row, axis=1, keepdims=True)
    xdt = x * _to_col(dt_row)
    lmat = jnp.exp(jnp.where(li >= si, acs_col - acs_row, -1e30))
    y = mm(mm_nt(cm, bm) * lmat, xdt)
    y = y + mm_nt(cm, prev) * jnp.exp(acs_col)
    y = y + dskip * x
    state = mm_tn(xdt * jnp.exp(total - acs_col), bm)
    return y, jnp.exp(total) * prev + state


HEADS_PER_GROUP = SSM_HEADS // SSM_GROUPS
BM_COL0 = SSM_INNER // SSM_STATE
CM_COL0 = BM_COL0 + SSM_GROUPS


def ssd_fwd(xs_hm, dtraw_t, dt_bias, a_log, dskip, xbc):
    hg = HEADS_PER_GROUP

    def body(x_ref, dt_ref, dtb_ref, al_ref, dk_ref, bm_ref, cm_ref, y_ref, prev_ref, state_ref):
        @pl.when(pl.program_id(1) == 0)
        def _():
            state_ref[...] = jnp.zeros_like(state_ref)

        bm, cm = bm_ref[...], cm_ref[...]
        for i in range(hg):
            prev = state_ref[i]
            prev_ref[0, i] = prev
            y, nxt = ssd_head(x_ref[i], dt_ref[i:i + 1, :], dtb_ref[i:i + 1, :], al_ref[i:i + 1, :],
                              dk_ref[i:i + 1, :], bm, cm, prev)
            y_ref[i] = y
            state_ref[i] = nxt

    hp = pl.BlockSpec((hg, 1), lambda g, c: (g, 0))
    return pl.pallas_call(
        body, name="ssd_fwd",
        out_shape=[jax.ShapeDtypeStruct((SSM_HEADS, S, SSM_HDIM), F32),
                   jax.ShapeDtypeStruct((N_CHUNKS, SSM_HEADS, SSM_HDIM, SSM_STATE), F32)],
        grid=(SSM_GROUPS, N_CHUNKS),
        in_specs=[pl.BlockSpec((hg, CHUNK, SSM_HDIM), lambda g, c: (g, c, 0)),
                  pl.BlockSpec((hg, CHUNK), lambda g, c: (g, c)), hp, hp, hp,
                  pl.BlockSpec((CHUNK, SSM_STATE), lambda g, c: (c, BM_COL0 + g)),
                  pl.BlockSpec((CHUNK, SSM_STATE), lambda g, c: (c, CM_COL0 + g))],
        out_specs=[pl.BlockSpec((hg, CHUNK, SSM_HDIM), lambda g, c: (g, c, 0)),
                   pl.BlockSpec((1, hg, SSM_HDIM, SSM_STATE), lambda g, c: (c, g, 0, 0))],
        scratch_shapes=[pltpu.VMEM((hg, SSM_HDIM, SSM_STATE), F32)],
        compiler_params=_cparams("parallel", "arbitrary"),
    )(xs_hm, dtraw_t, dt_bias, a_log, dskip, xbc, xbc)


def ssd_bwd(xs_hm, dtraw_t, dt_bias, a_log, dskip, xbc, prev_all, dy_hm):
    hg = HEADS_PER_GROUP
    last = N_CHUNKS - 1

    def body(x_ref, dt_ref, dtb_ref, al_ref, dk_ref, bm_ref, cm_ref, prev_ref, dy_ref,
             dx_ref, ddt_ref, ddtb_ref, dal_ref, ddk_ref, dbm_ref, dcm_ref, dstate_ref):
        @pl.when(pl.program_id(1) == 0)
        def _():
            dstate_ref[...] = jnp.zeros_like(dstate_ref)
            ddtb_ref[...] = jnp.zeros_like(ddtb_ref)
            dal_ref[...] = jnp.zeros_like(dal_ref)
            ddk_ref[...] = jnp.zeros_like(ddk_ref)

        bm, cm = bm_ref[...], cm_ref[...]
        dbm = jnp.zeros_like(bm)
        dcm = jnp.zeros_like(cm)
        for i in range(hg):
            _, vjp = jax.vjp(ssd_head, x_ref[i], dt_ref[i:i + 1, :], dtb_ref[i:i + 1, :], al_ref[i:i + 1, :],
                             dk_ref[i:i + 1, :], bm, cm, prev_ref[0, i])
            dx, ddt, ddtb, dal, ddk, dbm_i, dcm_i, dprev = vjp((dy_ref[i], dstate_ref[i]))
            dx_ref[i] = dx
            ddt_ref[i:i + 1, :] = ddt
            ddtb_ref[i:i + 1, :] += ddtb
            dal_ref[i:i + 1, :] += dal
            ddk_ref[i:i + 1, :] += ddk
            dbm = dbm + dbm_i
            dcm = dcm + dcm_i
            dstate_ref[i] = dprev
        dbm_ref[...] = dbm
        dcm_ref[...] = dcm

    hp = pl.BlockSpec((hg, 1), lambda g, c: (g, 0))
    xspec = pl.BlockSpec((hg, CHUNK, SSM_HDIM), lambda g, c: (g, last - c, 0))
    tspec = pl.BlockSpec((hg, CHUNK), lambda g, c: (g, last - c))
    gspec = pl.BlockSpec((CHUNK, SSM_STATE), lambda g, c: (last - c, g))
    return pl.pallas_call(
        body, name="ssd_bwd",
        out_shape=[jax.ShapeDtypeStruct((SSM_HEADS, S, SSM_HDIM), F32), jax.ShapeDtypeStruct((SSM_HEADS, S), F32),
                   jax.ShapeDtypeStruct((SSM_HEADS, 1), F32), jax.ShapeDtypeStruct((SSM_HEADS, 1), F32),
                   jax.ShapeDtypeStruct((SSM_HEADS, 1), F32),
                   jax.ShapeDtypeStruct((S, SSM_GROUPS * SSM_STATE), F32),
                   jax.ShapeDtypeStruct((S, SSM_GROUPS * SSM_STATE), F32)],
        grid=(SSM_GROUPS, N_CHUNKS),
        in_specs=[xspec, tspec, hp, hp, hp,
                  pl.BlockSpec((CHUNK, SSM_STATE), lambda g, c: (last - c, BM_COL0 + g)),
                  pl.BlockSpec((CHUNK, SSM_STATE), lambda g, c: (last - c, CM_COL0 + g)),
                  pl.BlockSpec((1, hg, SSM_HDIM, SSM_STATE), lambda g, c: (last - c, g, 0, 0)), xspec],
        out_specs=[xspec, tspec, hp, hp, hp, gspec, gspec],
        scratch_shapes=[pltpu.VMEM((hg, SSM_HDIM, SSM_STATE), F32)],
        compiler_params=_cparams("parallel", "arbitrary"),
    )(xs_hm, dtraw_t, dt_bias, a_log, dskip, xbc, xbc, prev_all, dy_hm)


ATT_HB = 4


def att_head(q, kp, kc, vp, vc, bias_p, bias_c, has_prev):
    b, dh = q.shape
    i, j = _iota2(b, 0), _iota2(b, 1)
    scale = dh ** -0.5
    sp = jnp.where(jnp.logical_and(j >= i, has_prev), mm_nt(q, kp) * scale + bias_p, -1e30)
    sc = jnp.where(j <= i, mm_nt(q, kc) * scale + bias_c, -1e30)
    m = lax.stop_gradient(jnp.maximum(jnp.max(sp, axis=1, keepdims=True), jnp.max(sc, axis=1, keepdims=True)))
    pp, pc = jnp.exp(sp - m), jnp.exp(sc - m)
    l = jnp.sum(pp, axis=1, keepdims=True) + jnp.sum(pc, axis=1, keepdims=True)
    o = mm(pp / l, vp) + mm(pc / l, vc)
    return o, jnp.broadcast_to(m + jnp.log(l), (b, dh))


def _att_specs(nb):
    hb, blk = ATT_HB, ATT_BLK
    cur = pl.BlockSpec((hb, blk, ATT_HDIM), lambda h, b: (h, b, 0))
    prv = pl.BlockSpec((hb, blk, ATT_HDIM), lambda h, b: (h, jnp.maximum(b - 1, 0), 0))
    bias = pl.BlockSpec((hb, 2, blk, blk), lambda h, b: (h, 0, 0, 0))
    return cur, prv, bias


def att_fwd(q, k, v, bias, nb, *, name):
    cur, prv, bspec = _att_specs(nb)

    def body(q_ref, kp_ref, kc_ref, vp_ref, vc_ref, b_ref, o_ref, l_ref):
        has_prev = (pl.program_id(1) % nb) != 0
        for i in range(ATT_HB):
            o, lse = att_head(q_ref[i], kp_ref[i], kc_ref[i], vp_ref[i], vc_ref[i], b_ref[i, 0], b_ref[i, 1], has_prev)
            o_ref[i] = o
            l_ref[i] = lse

    shp = jax.ShapeDtypeStruct((ATT_HEADS, S, ATT_HDIM), F32)
    return pl.pallas_call(
        body, name=name, out_shape=[shp, shp],
        grid=(ATT_HEADS // ATT_HB, S // ATT_BLK),
        in_specs=[cur, prv, cur, prv, cur, bspec],
        out_specs=[cur, cur],
        compiler_params=_cparams("parallel", "parallel"),
    )(q, k, k, v, v, bias)


def att_bwd(q, k, v, bias, do, dlse, nb, *, name):
    cur, prv, bspec = _att_specs(nb)

    def body(q_ref, kp_ref, kc_ref, vp_ref, vc_ref, b_ref, do_ref, dl_ref,
             dq_ref, dkc_ref, dkp_ref, dvc_ref, dvp_ref, db_ref):
        has_prev = (pl.program_id(1) % nb) != 0

        @pl.when(pl.program_id(1) == 0)
        def _():
            db_ref[...] = jnp.zeros_like(db_ref)

        for i in range(ATT_HB):
            ins = _f32([q_ref[i], kp_ref[i], kc_ref[i], vp_ref[i], vc_ref[i]]) + [b_ref[i, 0], b_ref[i, 1]]
            _, vjp = jax.vjp(functools.partial(att_head, has_prev=has_prev), *ins)
            dq, dkp, dkc, dvp, dvc, dbp, dbc = vjp((do_ref[i], dl_ref[i]))
            dq_ref[i] = dq
            dkc_ref[i] = dkc
            dkp_ref[i] = dkp
            dvc_ref[i] = dvc
            dvp_ref[i] = dvp
            db_ref[i, 0] += dbp
            db_ref[i, 1] += dbc

    shp = jax.ShapeDtypeStruct((ATT_HEADS, S, ATT_HDIM), F32)
    return pl.pallas_call(
        body, name=name,
        out_shape=[shp] * 5 + [jax.ShapeDtypeStruct((ATT_HEADS, 2, ATT_BLK, ATT_BLK), F32)],
        grid=(ATT_HEADS // ATT_HB, S // ATT_BLK),
        in_specs=[cur, prv, cur, prv, cur, bspec, cur, cur],
        out_specs=[cur] * 5 + [bspec],
        compiler_params=_cparams("parallel", "arbitrary"),
    )(q, k, k, v, v, bias, do, dlse)


def shift_add(cur, prev, nb, *, name):
    n_blocks = S // ATT_BLK

    def body(c_ref, p_ref, o_ref):
        nxt = pl.program_id(0) + 1
        keep = jnp.where((nxt % nb) != 0, 1.0, 0.0)
        o_ref[...] = c_ref[...] + keep * p_ref[...]

    return pl.pallas_call(
        body, name=name, out_shape=jax.ShapeDtypeStruct(cur.shape, F32),
        grid=(n_blocks,),
        in_specs=[pl.BlockSpec((ATT_HEADS, ATT_BLK, ATT_HDIM), lambda b: (0, b, 0)),
                  pl.BlockSpec((ATT_HEADS, ATT_BLK, ATT_HDIM), lambda b: (0, jnp.minimum(b + 1, n_blocks - 1), 0))],
        out_specs=pl.BlockSpec((ATT_HEADS, ATT_BLK, ATT_HDIM), lambda b: (0, b, 0)),
        compiler_params=_cparams("parallel"),
    )(cur, prev)


def _silu(x):
    return x * jax.nn.sigmoid(x)


def _rms(x):
    return x * lax.rsqrt(jnp.mean(x * x, -1, keepdims=True) + EPS)


def f_normmod(x, g, sc, sh):
    return (_rms(x) * g * (1.0 + sc) + sh,)


def f_resid(x, mix, gate):
    return (x + gate * mix,)


def f_resid_bias(x, mix, gate, b):
    return (x + gate * (mix + b),)


def f_swiglu(hgu):
    return (_silu(hgu[:, :FFN_HIDDEN]) * hgu[:, FFN_HIDDEN:],)


def f_silu(x):
    return (_silu(x),)


def f_gated_norm(y, z, g):
    return (_rms(y * _silu(z)) * g,)


def f_glu(y, b):
    y = y + b
    return (y[:, :D] * jax.nn.sigmoid(y[:, D:]),)


def f_ln_silu(u, g, b):
    mu = jnp.mean(u, -1, keepdims=True)
    var = jnp.mean(jnp.square(u - mu), -1, keepdims=True)
    return (_silu((u - mu) * lax.rsqrt(var + EPS) * g + b),)


def f_combine(o1, o2, o3, l1, l2, l3):
    m = lax.stop_gradient(jnp.maximum(jnp.maximum(l1, l2), l3))
    e1, e2, e3 = jnp.exp(l1 - m), jnp.exp(l2 - m), jnp.exp(l3 - m)
    return ((e1 * o1 + e2 * o2 + e3 * o3) / (e1 + e2 + e3),)


def f_head(x, tgt, g):
    return (0.5 * jnp.mean(jnp.square(_rms(x) * g - tgt), -1, keepdims=True),)


def f_sum3(a, b, c):
    return (a + b + c,)


def f_sum4(a, b, c, d):
    return (a + b + c + d,)


def f_add(a, b):
    return (a + b,)


def f_adamw(w, g, m, v):
    m = ADAM_B1 * m + (1.0 - ADAM_B1) * g
    v = ADAM_B2 * v + (1.0 - ADAM_B2) * jnp.square(g)
    m_hat = m / (1.0 - ADAM_B1 ** ADAM_STEP)
    v_hat = v / (1.0 - ADAM_B2 ** ADAM_STEP)
    return -ADAM_LR * (m_hat / (jnp.sqrt(v_hat) + ADAM_EPS) + ADAM_WD * w), m, v


def _rows_tile(r, cap=256):
    return _pick(r, cap, mult=8)


def adamw(w, g, m, v, *, name):
    shape = w.shape
    c_dim = shape[-1] if len(shape) > 1 else shape[0]
    flat = [a.reshape(-1, c_dim) for a in (w, g, m, v)]
    res = rowmap(f_adamw, flat, [], [F32] * 3, name=name, tr=_rows_tile(flat[0].shape[0], cap=128))
    return [r.reshape(shape) for r in res]


def _t5_bucket(dist):
    max_exact = REL_BUCKETS // 2
    n = jnp.maximum(dist, 1).astype(F32)
    large = max_exact + jnp.log(n / max_exact) / math.log(REL_MAX_DIST / max_exact) * (REL_BUCKETS - max_exact)
    large = jnp.minimum(large.astype(jnp.int32), REL_BUCKETS - 1)
    return jnp.where(dist < max_exact, dist, large)


def _att_buckets(dil):
    i = jnp.arange(ATT_BLK)[:, None]
    j = jnp.arange(2 * ATT_BLK)[None, :]
    bkt = _t5_bucket(jnp.maximum(ATT_BLK + i - j, 0) * dil)
    return jnp.transpose(bkt.reshape(ATT_BLK, 2, ATT_BLK), (1, 0, 2))


def att_bias(rel_table, p, dil):
    tab = rel_table[:, p * ATT_HEADS:(p + 1) * ATT_HEADS]
    return jnp.transpose(tab[_att_buckets(dil)], (3, 0, 1, 2)).astype(F32)


def att_bias_grad(dbias, dil, *, name):
    onehot = (_att_buckets(dil).reshape(-1, 1) == jnp.arange(LANES)[None, :]).astype(BF16)
    dtab = matmul(dbias.reshape(ATT_HEADS, -1), onehot, mode="nn", out_dtype=F32, name=name, tk_cap=2048)
    return dtab[:, :REL_BUCKETS].T


def to_heads(a, n_heads, dil=1):
    hd = a.shape[1] // n_heads
    return jnp.transpose(a.reshape(S // dil, dil, n_heads, hd), (2, 1, 0, 3)).reshape(n_heads, S, hd)


def from_heads(a, dil=1):
    n_heads, _, hd = a.shape
    return jnp.transpose(a.reshape(n_heads, dil, S // dil, hd), (2, 1, 0, 3)).reshape(S, n_heads * hd)


def regroup_heads(a, dil, inverse=False):
    n_heads, _, hd = a.shape
    if dil == 1:
        return a
    if inverse:
        return jnp.transpose(a.reshape(n_heads, dil, S // dil, hd), (0, 2, 1, 3)).reshape(n_heads, S, hd)
    return jnp.transpose(a.reshape(n_heads, S // dil, dil, hd), (0, 2, 1, 3)).reshape(n_heads, S, hd)


HY_Z, HY_XBC, HY_DT, HY_Q, HY_K, HY_V = 2048, 3072, 32, 3072, 1024, 1024
HY_IN = HY_Z + HY_XBC + HY_DT + HY_Q + HY_K + HY_V
OFF_Z, OFF_XBC, OFF_Q, OFF_KV, OFF_DT = 0, 2048, 5120, 8192, 10240
HY_CAT = OFF_DT + LANES
DT_PAD = LANES


def hy_to_cat(w):
    z, xbc, dt, qkv = w[:2048], w[2048:5120], w[5120:5152], w[5152:]
    return jnp.concatenate([z, xbc, qkv, dt, jnp.zeros((DT_PAD - HY_DT,) + w.shape[1:], w.dtype)], axis=0)


def hy_from_cat(w):
    return jnp.concatenate([w[:5120], w[OFF_DT:OFF_DT + HY_DT], w[5120:OFF_DT]], axis=0)


def device_step(x, tgt, mods, wts, sp):
    g = {}
    dmods = [[None] * 6 for _ in range(2)]

    def normmod(xi, gain, sc, sh, nm):
        return rowmap(f_normmod, [xi], [gain, sc, sh], [BF16], name=nm)[0]

    def ffn_fwd(xi, i, gate, nm):
        h = normmod(xi, sp["norm_ffn_g"][i], mods[i][4], mods[i][3], nm + "_norm")
        hgu = matmul(h, wts["gu_t"][i], mode="nt", out_dtype=F32, name=nm + "_gu")
        act = rowmap(f_swiglu, [hgu], [], [BF16], name=nm + "_act", tr=128)[0]
        out = matmul(act, wts["down"][i], mode="nn", out_dtype=F32, name=nm + "_down")
        xo = rowmap(f_resid, [xi, out], [gate], [F32], name=nm + "_res")[0]
        return xo, (h, hgu, act, out)

    def ffn_bwd(dres, xi, i, saved, nm):
        h, hgu, act, out = saved
        (dout,), (dgate,), _ = rowmap_bwd(f_resid, [xi, out], [mods[i][5]], [dres], name=nm + "_res_b",
                                          row_grad=[False, True], row_dtypes=[BF16])
        dmods[i][5] = dgate
        dact = matmul(dout, wts["down"][i], mode="nt", out_dtype=F32, name=nm + "_down_dx")
        g[f"down{i}"] = matmul(act, dout, mode="tn", out_dtype=F32, name=nm + "_down_dw")
        (dhgu,), _, _ = rowmap_bwd(f_swiglu, [hgu], [], [dact], name=nm + "_act_b", row_grad=[True],
                                   row_dtypes=[BF16], tr=128)
        g[f"gu_t{i}"] = matmul(dhgu, h, mode="tn", out_dtype=F32, name=nm + "_gu_dw")
        dh = matmul(dhgu, wts["gu_t"][i], mode="nn", out_dtype=F32, name=nm + "_gu_dx")
        (dres,), (dg_, dsc, dsh), _ = rowmap_bwd(f_normmod, [xi], [sp["norm_ffn_g"][i], mods[i][4], mods[i][3]], [dh],
                                                 name=nm + "_norm_b", row_grad=[True], row_add=[dres])
        g[f"norm_ffn_g{i}"] = dg_
        dmods[i][4], dmods[i][3] = dsc, dsh
        return dres

    h0 = normmod(x, sp["norm_mix_g"][0], mods[0][1], mods[0][0], "l0_norm")
    w_in = wts["hy_in_t"]
    z = matmul(h0, w_in, mode="nt", out_dtype=F32, name="hy_z", n=HY_Z, b_off=OFF_Z)
    xbc_raw = matmul(h0, w_in, mode="nt", out_dtype=F32, name="hy_xbc", n=HY_XBC, b_off=OFF_XBC)
    q = matmul(h0, w_in, mode="nt", out_dtype=BF16, name="hy_q", n=HY_Q, b_off=OFF_Q)
    kv = matmul(h0, w_in, mode="nt", out_dtype=BF16, name="hy_kv", n=HY_K + HY_V, b_off=OFF_KV)
    dtr = matmul(h0, w_in, mode="nt", out_dtype=F32, name="hy_dt", n=DT_PAD, b_off=OFF_DT)
    xbc_pre = conv_fwd(xbc_raw, sp["hy_conv_w"], sp["hy_conv_b"], name="hy_conv")
    xbc = rowmap(f_silu, [xbc_pre], [], [F32], name="hy_conv_act", tr=128)[0]
    xs_hm = to_heads(xbc[:, :SSM_INNER], SSM_HEADS)
    dtraw_t = dtr[:, :HY_DT].T
    y_hm, prev_all = ssd_fwd(xs_hm, dtraw_t, sp["hy_dt_bias"], sp["hy_a_log"], sp["hy_d_skip"], xbc)
    y = from_heads(y_hm)
    ysn = rowmap(f_gated_norm, [y, z], [sp["hy_ssm_norm_g"]], [BF16], name="hy_gnorm", tr=128)[0]
    k_all, v_all = kv[:, :HY_K], kv[:, HY_K:]
    att_in, att_o, att_l = [], [], []
    for p, (win, dil) in enumerate(ATT_PATTERNS):
        qp = to_heads(q[:, p * D:(p + 1) * D], ATT_HEADS, dil)
        kp, vp = to_heads(k_all, ATT_HEADS, dil), to_heads(v_all, ATT_HEADS, dil)
        bias = att_bias(sp["rel_table"], p, dil)
        nb = S // dil // ATT_BLK
        o, lse = att_fwd(qp, kp, vp, bias, nb, name=f"att_fwd{p}")
        att_in.append((qp, kp, vp, bias, nb))
        att_o.append(regroup_heads(o, dil, inverse=True).reshape(ATT_HEADS * S, ATT_HDIM))
        att_l.append(regroup_heads(lse, dil, inverse=True).reshape(ATT_HEADS * S, ATT_HDIM))
    att_hm = rowmap(f_combine, att_o + att_l, [], [BF16], name="att_combine", tr=2048)[0]
    att = from_heads(att_hm.reshape(ATT_HEADS, S, ATT_HDIM))
    cat = jnp.concatenate([ysn, att], axis=-1)
    mix0 = matmul(cat, wts["hy_out"], mode="nn", out_dtype=F32, name="hy_out")
    x1 = rowmap(f_resid, [x, mix0], [mods[0][2]], [F32], name="l0_res")[0]
    x2, ffn0 = ffn_fwd(x1, 0, mods[0][5], "ffn0")

    h1 = normmod(x2, sp["norm_mix_g"][1], mods[1][1], mods[1][0], "l1_norm")
    p1 = matmul(h1, wts["pw1_t"], mode="nt", out_dtype=F32, name="cv_pw1")
    u = rowmap(f_glu, [p1], [sp["cv_b_pw1"]], [F32], name="cv_glu")[0]
    uc = conv_fwd(u, sp["cv_w_dw"], sp["cv_b_dw"], name="cv_conv")
    ul = rowmap(f_ln_silu, [uc], [sp["cv_ln_g"], sp["cv_ln_b"]], [BF16], name="cv_ln")[0]
    mix1 = matmul(ul, wts["pw2"], mode="nn", out_dtype=F32, name="cv_pw2")
    x3 = rowmap(f_resid_bias, [x2, mix1], [mods[1][2], sp["cv_b_pw2"]], [F32], name="l1_res")[0]
    x4, ffn1 = ffn_fwd(x3, 1, mods[1][5], "ffn1")

    ones = jnp.ones((S, 1), F32)
    (dres,), (dfinal,), (loss_rows,) = rowmap_bwd(f_head, [x4, tgt], [sp["final_norm_g"]], [ones], name="head",
                                                  row_grad=[True, False], emit=(0,))
    g["final_norm_g"] = dfinal

    dres = ffn_bwd(dres, x3, 1, ffn1, "ffn1")
    (dmix1,), (dg1, db2), _ = rowmap_bwd(f_resid_bias, [x2, mix1], [mods[1][2], sp["cv_b_pw2"]], [dres], name="l1_res_b",
                                         row_grad=[False, True], row_dtypes=[BF16])
    dmods[1][2] = dg1
    g["cv_b_pw2"] = db2
    dul = matmul(dmix1, wts["pw2"], mode="nt", out_dtype=F32, name="cv_pw2_dx")
    g["pw2"] = matmul(ul, dmix1, mode="tn", out_dtype=F32, name="cv_pw2_dw")
    (duc,), (g["cv_ln_g"], g["cv_ln_b"]), _ = rowmap_bwd(f_ln_silu, [uc], [sp["cv_ln_g"], sp["cv_ln_b"]], [dul],
                                                         name="cv_ln_b", row_grad=[True])
    du, g["cv_w_dw"], g["cv_b_dw"] = conv_bwd(u, sp["cv_w_dw"], duc, name="cv_conv_b")
    (dp1,), (g["cv_b_pw1"],), _ = rowmap_bwd(f_glu, [p1], [sp["cv_b_pw1"]], [du], name="cv_glu_b", row_grad=[True],
                                             row_dtypes=[BF16])
    g["pw1_t"] = matmul(dp1, h1, mode="tn", out_dtype=F32, name="cv_pw1_dw")
    dh1 = matmul(dp1, wts["pw1_t"], mode="nn", out_dtype=F32, name="cv_pw1_dx")
    (dres,), (dg_, dsc, dsh), _ = rowmap_bwd(f_normmod, [x2], [sp["norm_mix_g"][1], mods[1][1], mods[1][0]], [dh1],
                                             name="l1_norm_b", row_grad=[True], row_add=[dres])
    g["norm_mix_g1"] = dg_
    dmods[1][1], dmods[1][0] = dsc, dsh

    dres = ffn_bwd(dres, x1, 0, ffn0, "ffn0")
    (dmix0,), (dg1,), _ = rowmap_bwd(f_resid, [x, mix0], [mods[0][2]], [dres], name="l0_res_b",
                                     row_grad=[False, True], row_dtypes=[BF16])
    dmods[0][2] = dg1
    dysn = matmul(dmix0, wts["hy_out"], mode="nt", out_dtype=F32, name="hy_out_dy", n=SSM_INNER, b_off=0)
    datt = matmul(dmix0, wts["hy_out"], mode="nt", out_dtype=F32, name="hy_out_da", n=D, b_off=SSM_INNER)
    g["hy_out"] = matmul(cat, dmix0, mode="tn", out_dtype=F32, name="hy_out_dw")
    (dy, dz), (g["hy_ssm_norm_g"],), _ = rowmap_bwd(f_gated_norm, [y, z], [sp["hy_ssm_norm_g"]], [dysn], name="hy_gnorm_b",
                                                    row_grad=[True, True], tr=128)
    dxs_hm, ddtraw_t, g["hy_dt_bias"], g["hy_a_log"], g["hy_d_skip"], dbm, dcm = ssd_bwd(
        xs_hm, dtraw_t, sp["hy_dt_bias"], sp["hy_a_log"], sp["hy_d_skip"], xbc, prev_all, to_heads(dy, SSM_HEADS))
    dxbc = jnp.concatenate([from_heads(dxs_hm), dbm, dcm], axis=-1)
    (dxbc_pre,), _, _ = rowmap_bwd(f_silu, [xbc_pre], [], [dxbc], name="hy_conv_act_b", row_grad=[True], tr=128)
    dxbc_raw, g["hy_conv_w"], g["hy_conv_b"] = conv_bwd(xbc_raw, sp["hy_conv_w"], dxbc_pre, name="hy_conv_b")
    datt_hm = to_heads(datt, ATT_HEADS).reshape(ATT_HEADS * S, ATT_HDIM)
    dol, _, _ = rowmap_bwd(f_combine, att_o + att_l, [], [datt_hm], name="att_combine_b", row_grad=[True] * 6, tr=2048)
    dqs, dks, dvs, dtabs = [], [], [], []
    for p, (win, dil) in enumerate(ATT_PATTERNS):
        qp, kp, vp, bias, nb = att_in[p]
        do = regroup_heads(dol[p].reshape(ATT_HEADS, S, ATT_HDIM), dil)
        dl = regroup_heads(dol[3 + p].reshape(ATT_HEADS, S, ATT_HDIM), dil)
        dq, dkc, dkp, dvc, dvp, dbias = att_bwd(qp, kp, vp, bias, do, dl, nb, name=f"att_bwd{p}")
        dqs.append(from_heads(dq, dil))
        dks.append(from_heads(shift_add(dkc, dkp, nb, name=f"att_dk{p}"), dil))
        dvs.append(from_heads(shift_add(dvc, dvp, nb, name=f"att_dv{p}"), dil))
        dtabs.append(att_bias_grad(dbias, dil, name=f"att_dtab{p}"))
    g["rel_table"] = jnp.concatenate(dtabs, axis=1)
    dk = rowmap(f_sum3, dks, [], [F32], name="att_dk_sum")[0]
    dv = rowmap(f_sum3, dvs, [], [F32], name="att_dv_sum")[0]
    ddt = jnp.pad(ddtraw_t.T, ((0, 0), (0, DT_PAD - HY_DT)))
    dproj = jnp.concatenate([dz, dxbc_raw] + dqs + [dk, dv, ddt], axis=-1).astype(BF16)
    g["hy_in_t"] = matmul(dproj, h0, mode="tn", out_dtype=F32, name="hy_in_dw")
    dh0 = matmul(dproj, w_in, mode="nn", out_dtype=F32, name="hy_in_dx")
    (dres,), (dg_, dsc, dsh), _ = rowmap_bwd(f_normmod, [x], [sp["norm_mix_g"][0], mods[0][1], mods[0][0]], [dh0],
                                             name="l0_norm_b", row_grad=[True], row_add=[dres])
    g["norm_mix_g0"] = dg_
    dmods[0][1], dmods[0][0] = dsc, dsh
    return loss_rows, dres, g, dmods


ANY = pl.BlockSpec(memory_space=pl.ANY)
WHOLE_VMEM = pl.BlockSpec(memory_space=pltpu.VMEM)


def _place():
    return lax.axis_index("x"), lax.axis_index("y"), lax.axis_index("c")


def _other_chips(x, y):
    return [(1 - x, y), (x, 1 - y), (1 - x, 1 - y)]


def allgather_small(v, *, name):
    m_per = v.shape[0]

    def body(x_ref, out_ref, send_sems, recv_sems, local_sem):
        x, y, c = _place()
        me, sibling = (x, y, c), (x, y, 1 - c)
        chips = _other_chips(x, y)

        def rows(px, py, pc):
            return out_ref.at[pl.ds((4 * px + 2 * py + pc) * m_per, m_per), :]

        def copy(k, block, to, src=None):
            return pltpu.make_async_remote_copy(
                src_ref=rows(*block) if src is None else src, dst_ref=rows(*block),
                send_sem=send_sems.at[k], recv_sem=recv_sems.at[k], device_id=to, device_id_type=MESH)

        mine = pltpu.make_async_copy(x_ref, rows(*me), local_sem)
        mine.start()
        first = [copy(0, me, sibling, src=x_ref)]
        first += [copy(1 + j, me, (*chip, c), src=x_ref) for j, chip in enumerate(chips)]
        for cp in first:
            cp.start()
        passed = [copy(4 + j, (*chip, c), sibling) for j, chip in enumerate(chips)]
        for j, chip in enumerate(chips):
            copy(1 + j, (*chip, c), me).wait_recv()
            passed[j].start()
        copy(0, sibling, me).wait_recv()
        for j, chip in enumerate(chips):
            copy(4 + j, (*chip, 1 - c), me).wait_recv()
        for cp in first + passed:
            cp.wait_send()
        mine.wait()

    return pl.pallas_call(
        body, name=name,
        out_shape=jax.ShapeDtypeStruct((N_DEV * m_per, LANES), v.dtype),
        in_specs=[WHOLE_VMEM], out_specs=WHOLE_VMEM,
        scratch_shapes=[pltpu.SemaphoreType.DMA((7,)), pltpu.SemaphoreType.DMA((7,)), pltpu.SemaphoreType.DMA],
    )(v)


def allgather_chips(pack, *, name):
    def body(p_ref, o_ref, send_sems, recv_sems, local_sem):
        x, y, c = _place()
        chips = _other_chips(x, y)
        mine = pltpu.make_async_copy(p_ref, o_ref.at[2 * x + y], local_sem)
        mine.start()
        sends = [pltpu.make_async_remote_copy(
            src_ref=p_ref, dst_ref=o_ref.at[2 * x + y], send_sem=send_sems.at[k], recv_sem=recv_sems.at[k],
            device_id=(cx, cy, c), device_id_type=MESH) for k, (cx, cy) in enumerate(chips)]
        for cp in sends:
            cp.start()
        for k, (cx, cy) in enumerate(chips):
            pltpu.make_async_remote_copy(
                src_ref=p_ref, dst_ref=o_ref.at[2 * cx + cy], send_sem=send_sems.at[k], recv_sem=recv_sems.at[k],
                device_id=(cx, cy, c), device_id_type=MESH).wait_recv()
        for cp in sends:
            cp.wait_send()
        mine.wait()

    return pl.pallas_call(
        body, name=name,
        out_shape=jax.ShapeDtypeStruct((N_CHIPS,) + pack.shape, pack.dtype),
        in_specs=[ANY], out_specs=ANY,
        scratch_shapes=[pltpu.SemaphoreType.DMA((3,)), pltpu.SemaphoreType.DMA((3,)), pltpu.SemaphoreType.DMA],
    )(pack)


def scatter_chips(gpack, *, name):
    def body(g_ref, own_ref, recv_ref, send_sems, recv_sems, local_sem):
        x, y, c = _place()
        chips = _other_chips(x, y)
        mine = pltpu.make_async_copy(g_ref.at[2 * x + y], own_ref, local_sem)
        mine.start()
        sends = [pltpu.make_async_remote_copy(
            src_ref=g_ref.at[2 * cx + cy], dst_ref=recv_ref.at[k], send_sem=send_sems.at[k], recv_sem=recv_sems.at[k],
            device_id=(cx, cy, c), device_id_type=MESH) for k, (cx, cy) in enumerate(chips)]
        for cp in sends:
            cp.start()
        for cp in sends:
            cp.wait_recv()
        for cp in sends:
            cp.wait_send()
        mine.wait()

    slot = jax.ShapeDtypeStruct(gpack.shape[1:], gpack.dtype)
    return pl.pallas_call(
        body, name=name,
        out_shape=[slot, jax.ShapeDtypeStruct((3,) + gpack.shape[1:], gpack.dtype)],
        in_specs=[ANY], out_specs=[ANY, ANY],
        scratch_shapes=[pltpu.SemaphoreType.DMA((3,)), pltpu.SemaphoreType.DMA((3,)), pltpu.SemaphoreType.DMA],
    )(gpack)


def sibling_swap(p, *, name):
    def body(p_ref, r_ref, send_sem, recv_sem):
        x, y, c = _place()
        cp = pltpu.make_async_remote_copy(src_ref=p_ref, dst_ref=r_ref, send_sem=send_sem, recv_sem=recv_sem,
                                          device_id=(x, y, 1 - c), device_id_type=MESH)
        cp.start()
        cp.wait()

    return pl.pallas_call(
        body, name=name, out_shape=jax.ShapeDtypeStruct(p.shape, p.dtype),
        in_specs=[ANY], out_specs=ANY,
        scratch_shapes=[pltpu.SemaphoreType.DMA, pltpu.SemaphoreType.DMA],
    )(p)


def sum_devices(v_all, *, name):
    m_per = v_all.shape[0] // N_DEV

    def body(v_ref, o_ref):
        acc = v_ref[pl.ds(0, m_per), :]
        for d in range(1, N_DEV):
            acc = acc + v_ref[pl.ds(d * m_per, m_per), :]
        o_ref[...] = acc

    return pl.pallas_call(
        body, name=name, out_shape=jax.ShapeDtypeStruct((m_per, LANES), F32),
        in_specs=[WHOLE_VMEM], out_specs=WHOLE_VMEM,
    )(v_all)


WEIGHTS = ['ada_w', 'ada_b', 'norm_mix_g', 'norm_ffn_g', 'hy_w_in', 'hy_conv_w', 'hy_conv_b', 'hy_dt_bias', 'hy_a_log',
           'hy_d_skip', 'hy_ssm_norm_g', 'hy_w_out', 'rel_table', 'cv_w_pw1', 'cv_b_pw1', 'cv_w_dw', 'cv_b_dw', 'cv_ln_g',
           'cv_ln_b', 'cv_w_pw2', 'cv_b_pw2', 'ffn_w_gate', 'ffn_w_up', 'ffn_w_down', 'final_norm_g']
BIG = ('ada_w', 'hy_w_in', 'hy_w_out', 'cv_w_pw1', 'cv_w_pw2', 'ffn_w_gate', 'ffn_w_up', 'ffn_w_down')
SMALL_SHARDED = {'hy_conv_w': (1, 4, 3072), 'cv_b_pw1': (1, 2048), 'cv_w_dw': (1, 31, 1024), 'cv_b_dw': (1, 1024),
                 'cv_ln_g': (1, 1024), 'cv_ln_b': (1, 1024), 'cv_b_pw2': (1, 1024)}
SMALL_GRADS = {'ada_b': (2, 6144), 'norm_mix_g': (2, 1024), 'norm_ffn_g': (2, 1024), 'hy_conv_w': (1, 4, 3072),
               'hy_conv_b': (1, 3072), 'hy_dt_bias': (1, 32), 'hy_a_log': (1, 32), 'hy_d_skip': (1, 32),
               'hy_ssm_norm_g': (1, 2048), 'rel_table': (32, 48), 'cv_b_pw1': (1, 2048), 'cv_w_dw': (1, 31, 1024),
               'cv_b_dw': (1, 1024), 'cv_ln_g': (1, 1024), 'cv_ln_b': (1, 1024), 'cv_b_pw2': (1, 1024),
               'final_norm_g': (1024,), 'loss': (1,)}

PACK_LAYOUT = (('hy_in_t', 2568), ('hy_out', 768), ('pw1_t', 512), ('pw2', 256),
               ('gate_t0', 704), ('up_t0', 704), ('down0', 704), ('gate_t1', 704), ('up_t1', 704), ('down1', 704))
PACK_ROWS = 8448


def _pack_offsets():
    off, out = 0, {}
    for nm, r in PACK_LAYOUT:
        out[nm] = (off, r)
        off += r
    return out


PACK_OFF = _pack_offsets()


def _to_lanes(flat):
    n = flat.shape[0]
    m = -(-n // (8 * LANES)) * 8
    return jnp.pad(flat, (0, m * LANES - n)).reshape(m, LANES)


def _split(flat, shapes):
    out, off = {}, 0
    for nm, shp in shapes.items():
        n = int(np.prod(shp))
        out[nm] = flat[off:off + n].reshape(shp)
        off += n
    return out


def kernel(x, c, ada_w, ada_b, norm_mix_g, norm_ffn_g, hy_w_in, hy_conv_w, hy_conv_b, hy_dt_bias, hy_a_log, hy_d_skip, hy_ssm_norm_g, hy_w_out, rel_table, cv_w_pw1, cv_b_pw1, cv_w_dw, cv_b_dw, cv_ln_g, cv_ln_b, cv_w_pw2, cv_b_pw2, ffn_w_gate, ffn_w_up, ffn_w_down, final_norm_g, loss_target, m_ada_w, m_ada_b, m_norm_mix_g, m_norm_ffn_g, m_hy_w_in, m_hy_conv_w, m_hy_conv_b, m_hy_dt_bias, m_hy_a_log, m_hy_d_skip, m_hy_ssm_norm_g, m_hy_w_out, m_rel_table, m_cv_w_pw1, m_cv_b_pw1, m_cv_w_dw, m_cv_b_dw, m_cv_ln_g, m_cv_ln_b, m_cv_w_pw2, m_cv_b_pw2, m_ffn_w_gate, m_ffn_w_up, m_ffn_w_down, m_final_norm_g, v_ada_w, v_ada_b, v_norm_mix_g, v_norm_ffn_g, v_hy_w_in, v_hy_conv_w, v_hy_conv_b, v_hy_dt_bias, v_hy_a_log, v_hy_d_skip, v_hy_ssm_norm_g, v_hy_w_out, v_rel_table, v_cv_w_pw1, v_cv_b_pw1, v_cv_w_dw, v_cv_b_dw, v_cv_ln_g, v_cv_ln_b, v_cv_w_pw2, v_cv_b_pw2, v_ffn_w_gate, v_ffn_w_up, v_ffn_w_down, v_final_norm_g):
    args = (x, c, ada_w, ada_b, norm_mix_g, norm_ffn_g, hy_w_in, hy_conv_w, hy_conv_b, hy_dt_bias, hy_a_log, hy_d_skip, hy_ssm_norm_g, hy_w_out, rel_table, cv_w_pw1, cv_b_pw1, cv_w_dw, cv_b_dw, cv_ln_g, cv_ln_b, cv_w_pw2, cv_b_pw2, ffn_w_gate, ffn_w_up, ffn_w_down, final_norm_g, loss_target, m_ada_w, m_ada_b, m_norm_mix_g, m_norm_ffn_g, m_hy_w_in, m_hy_conv_w, m_hy_conv_b, m_hy_dt_bias, m_hy_a_log, m_hy_d_skip, m_hy_ssm_norm_g, m_hy_w_out, m_rel_table, m_cv_w_pw1, m_cv_b_pw1, m_cv_w_dw, m_cv_b_dw, m_cv_ln_g, m_cv_ln_b, m_cv_w_pw2, m_cv_b_pw2, m_ffn_w_gate, m_ffn_w_up, m_ffn_w_down, m_final_norm_g, v_ada_w, v_ada_b, v_norm_mix_g, v_norm_ffn_g, v_hy_w_in, v_hy_conv_w, v_hy_conv_b, v_hy_dt_bias, v_hy_a_log, v_hy_d_skip, v_hy_ssm_norm_g, v_hy_w_out, v_rel_table, v_cv_w_pw1, v_cv_b_pw1, v_cv_w_dw, v_cv_b_dw, v_cv_ln_g, v_cv_ln_b, v_cv_w_pw2, v_cv_b_pw2, v_ffn_w_gate, v_ffn_w_up, v_ffn_w_down, v_final_norm_g)
    x_in, c_in = args[0], args[1]
    w = dict(zip(WEIGHTS, args[2:27], strict=True))
    tgt = args[27]
    m_in = dict(zip(WEIGHTS, args[28:53], strict=True))
    v_in = dict(zip(WEIGHTS, args[53:78], strict=True))
    xi, yi, ci = _place()
    chip = 2 * xi + yi
    dev = 2 * chip + ci

    cs = rowmap(f_silu, [c_in.reshape(8, LANES)], [], [F32], name="cond_silu", tr=8)[0]
    cs_all = allgather_small(cs, name="gather_cond").reshape(N_DEV, D)
    cs16 = jnp.pad(cs_all, ((0, 8), (0, 0)))
    modpart = jnp.stack([matmul(cs16, w['ada_w'][i], mode="nn", out_dtype=F32, name=f"ada_fwd{i}")[:N_DEV]
                         for i in range(2)], axis=1)
    shard_names = list(SMALL_SHARDED)
    payload = jnp.concatenate([modpart.reshape(-1)] + [w[nm].reshape(-1) for nm in shard_names])
    got = allgather_small(_to_lanes(payload), name="gather_mod").reshape(N_DEV, -1)[0::2]
    modparts = got[:, :modpart.size].reshape(N_CHIPS, N_DEV, 2, 1536)
    mine = lax.dynamic_index_in_dim(modparts, dev, axis=1, keepdims=False)
    mod = jnp.transpose(mine, (1, 0, 2)).reshape(2, 6 * D) + w['ada_b']
    mods = [[mod[i, j * D:(j + 1) * D].reshape(1, D) for j in range(6)] for i in range(2)]
    sp = {}
    off = modpart.size
    for nm in shard_names:
        shp = w[nm].shape
        n = int(np.prod(shp))
        parts = got[:, off:off + n].reshape((N_CHIPS,) + shp)
        sp[nm] = jnp.concatenate([parts[s] for s in range(N_CHIPS)], axis=-1)
        off += n

    def rows_of(nm, i=None):
        a = w[nm][0 if i is None else i]
        return (a.T if nm in ('hy_w_in', 'cv_w_pw1', 'ffn_w_gate', 'ffn_w_up') else a).astype(BF16)

    pieces = [rows_of('hy_w_in'), rows_of('hy_w_out'), rows_of('cv_w_pw1'), rows_of('cv_w_pw2')]
    for i in range(2):
        pieces += [rows_of('ffn_w_gate', i), rows_of('ffn_w_up', i), rows_of('ffn_w_down', i)]
    n_rows = sum(p.shape[0] for p in pieces)
    pack = jnp.concatenate(pieces + [jnp.zeros((PACK_ROWS - n_rows, D), BF16)], axis=0)
    full = allgather_chips(pack, name="gather_weights")

    def whole(nm):
        o, r = PACK_OFF[nm]
        return full[:, o:o + r].reshape(N_CHIPS * r, D)

    wts = {"hy_in_t": hy_to_cat(whole('hy_in_t')), "hy_out": whole('hy_out'), "pw1_t": whole('pw1_t'), "pw2": whole('pw2'),
           "gu_t": [jnp.concatenate([whole(f'gate_t{i}'), whole(f'up_t{i}')], axis=0) for i in range(2)],
           "down": [whole(f'down{i}') for i in range(2)]}

    sp = {"norm_mix_g": [w['norm_mix_g'][i].reshape(1, D) for i in range(2)],
          "norm_ffn_g": [w['norm_ffn_g'][i].reshape(1, D) for i in range(2)],
          "hy_conv_w": sp['hy_conv_w'][0], "hy_conv_b": w['hy_conv_b'],
          "hy_dt_bias": w['hy_dt_bias'].reshape(SSM_HEADS, 1), "hy_a_log": w['hy_a_log'].reshape(SSM_HEADS, 1),
          "hy_d_skip": w['hy_d_skip'].reshape(SSM_HEADS, 1), "hy_ssm_norm_g": w['hy_ssm_norm_g'],
          "rel_table": w['rel_table'], "cv_b_pw1": sp['cv_b_pw1'], "cv_w_dw": sp['cv_w_dw'][0], "cv_b_dw": sp['cv_b_dw'],
          "cv_ln_g": sp['cv_ln_g'], "cv_ln_b": sp['cv_ln_b'], "cv_b_pw2": sp['cv_b_pw2'],
          "final_norm_g": w['final_norm_g'].reshape(1, D)}

    loss_rows, grad_x, g, dmods = device_step(x_in[0], tgt[0], mods, wts, sp)

    dmod = jnp.stack([jnp.concatenate([d.reshape(-1) for d in dmods[i]]) for i in range(2)])
    small = {'ada_b': dmod, 'norm_mix_g': jnp.stack([g[f'norm_mix_g{i}'].reshape(-1) for i in range(2)]),
             'norm_ffn_g': jnp.stack([g[f'norm_ffn_g{i}'].reshape(-1) for i in range(2)]),
             'loss': jnp.sum(loss_rows).reshape(1)}
    for nm in SMALL_GRADS:
        if nm not in small:
            small[nm] = g[nm]
    vec = _to_lanes(jnp.concatenate([small[nm].reshape(-1) for nm in SMALL_GRADS]))
    vec_all = allgather_small(vec, name="gather_small_grads")
    tot = _split(sum_devices(vec_all, name="sum_small_grads").reshape(-1), SMALL_GRADS)
    dmod_all = vec_all.reshape(N_DEV, -1)[:, :2 * 6 * D].reshape(N_DEV, 2, 6 * D)

    gp = [hy_from_cat(g['hy_in_t']), g['hy_out'], g['pw1_t'], g['pw2']]
    for i in range(2):
        gp += [g[f'gu_t{i}'][:FFN_HIDDEN], g[f'gu_t{i}'][FFN_HIDDEN:], g[f'down{i}']]
    gp = [a.reshape(N_CHIPS, a.shape[0] // N_CHIPS, D) for a in gp]
    gpack = jnp.concatenate(gp + [jnp.zeros((N_CHIPS, PACK_ROWS - n_rows, D), F32)], axis=1)
    own, recv = scatter_chips(gpack, name="scatter_grads")
    part = rowmap(f_sum4, [own, recv[0], recv[1], recv[2]], [], [F32], name="sum_chip_grads")[0]
    other = sibling_swap(part, name="swap_grads")
    red = rowmap(f_add, [part, other], [], [F32], name="sum_core_grads")[0]

    def shard_grad(nm, i=None):
        key = {'hy_w_in': 'hy_in_t', 'hy_w_out': 'hy_out', 'cv_w_pw1': 'pw1_t', 'cv_w_pw2': 'pw2'}.get(nm)
        if key is None:
            key = {'ffn_w_gate': 'gate_t', 'ffn_w_up': 'up_t', 'ffn_w_down': 'down'}[nm] + str(i)
        o, r = PACK_OFF[key]
        a = red[o:o + r]
        return a.T if key.endswith('_t') or key[:-1].endswith('_t') else a

    grads = {}
    grads['hy_w_in'] = shard_grad('hy_w_in')[None]
    grads['hy_w_out'] = shard_grad('hy_w_out')[None]
    grads['cv_w_pw1'] = shard_grad('cv_w_pw1')[None]
    grads['cv_w_pw2'] = shard_grad('cv_w_pw2')[None]
    for nm in ('ffn_w_gate', 'ffn_w_up', 'ffn_w_down'):
        grads[nm] = jnp.stack([shard_grad(nm, i) for i in range(2)])
    cs16 = jnp.pad(cs_all, ((0, 8), (0, 0)))
    dm_mine = lax.dynamic_slice_in_dim(dmod_all, chip * 1536, 1536, axis=2)
    dm16 = jnp.pad(dm_mine, ((0, 8), (0, 0), (0, 0)))
    grads['ada_w'] = jnp.stack([matmul(cs16, dm16[:, i], mode="tn", out_dtype=F32, name=f"ada_dw{i}") for i in range(2)])
    for nm, shp in SMALL_GRADS.items():
        if nm == 'loss':
            continue
        if nm in SMALL_SHARDED:
            n = w[nm].shape[-1]
            grads[nm] = lax.dynamic_slice_in_dim(tot[nm], chip * n, n, axis=len(shp) - 1)
        else:
            grads[nm] = tot[nm].reshape(w[nm].shape)

    delta, new_m, new_v = {}, {}, {}
    for nm in BIG:
        delta[nm], new_m[nm], new_v[nm] = adamw(w[nm], grads[nm], m_in[nm], v_in[nm], name="adamw_" + nm)
    smalls = [nm for nm in WEIGHTS if nm not in BIG]
    packed = [_to_lanes(jnp.concatenate([d[nm].reshape(-1) for nm in smalls])) for d in (w, grads, m_in, v_in)]
    res = rowmap(f_adamw, packed, [], [F32] * 3, name="adamw_small", tr=_rows_tile(packed[0].shape[0]))
    for d, r in zip((delta, new_m, new_v), res, strict=True):
        d.update(_split(r.reshape(-1), {nm: w[nm].shape for nm in smalls}))

    loss = tot['loss'].reshape(())
    return (loss, grad_x[None], *[grads[nm] for nm in WEIGHTS], *[delta[nm] for nm in WEIGHTS],
            *[new_m[nm] for nm in WEIGHTS], *[new_v[nm] for nm in WEIGHTS])
```

```python
import functools
import math

import jax
import jax.numpy as jnp
import numpy as np
from jax import lax
from jax.experimental import pallas as pl
from jax.experimental.pallas import tpu as pltpu

F32 = jnp.float32
BF16 = jnp.bfloat16
MESH = pl.DeviceIdType.MESH

D = 1024
S = 4096
EPS = 1e-6
SSM_INNER = 2048
SSM_HEADS = 32
SSM_HDIM = 64
SSM_GROUPS = 4
SSM_STATE = 128
SSM_CONVK = 4
SSM_CONV_DIM = 3072
CHUNK = 128
N_CHUNKS = S // CHUNK
ATT_HEADS = 16
ATT_HDIM = 64
ATT_PATTERNS = ((128, 1), (512, 4), (2048, 16))
ATT_BLK = 128
REL_BUCKETS = 32
REL_MAX_DIST = 2048
CONV_WIDTH = 31
FFN_HIDDEN = 2816
N_CHIPS = 4
N_DEV = 8
ADAM_LR, ADAM_B1, ADAM_B2, ADAM_EPS, ADAM_WD, ADAM_STEP = 0.001, 0.9, 0.999, 1e-08, 0.01, 10

VMEM_LIMIT_BYTES = 56 * 1024 * 1024
LANES = 128


def _cparams(*sem):
    return pltpu.CompilerParams(dimension_semantics=sem, vmem_limit_bytes=VMEM_LIMIT_BYTES)


def _pick(n, cap, mult=LANES):
    best = None
    for t in range(mult, min(n, cap) + 1, mult):
        if n % t == 0:
            best = t
    return best or n


def _dot(a, b, ca, cb):
    return lax.dot_general(a.astype(BF16), b.astype(BF16), (((ca,), (cb,)), ((), ())), preferred_element_type=F32)


@jax.custom_vjp
def mm(a, b):
    return _dot(a, b, 1, 0)


def _mm_fwd(a, b):
    return _dot(a, b, 1, 0), (a, b)


def _mm_bwd(res, g):
    a, b = res
    return _dot(g, b, 1, 1).astype(a.dtype), _dot(a, g, 0, 0).astype(b.dtype)


mm.defvjp(_mm_fwd, _mm_bwd)


@jax.custom_vjp
def mm_nt(a, b):
    return _dot(a, b, 1, 1)


def _mm_nt_fwd(a, b):
    return _dot(a, b, 1, 1), (a, b)


def _mm_nt_bwd(res, g):
    a, b = res
    return _dot(g, b, 1, 0).astype(a.dtype), _dot(g, a, 0, 0).astype(b.dtype)


mm_nt.defvjp(_mm_nt_fwd, _mm_nt_bwd)


@jax.custom_vjp
def mm_tn(a, b):
    return _dot(a, b, 0, 0)


def _mm_tn_fwd(a, b):
    return _dot(a, b, 0, 0), (a, b)


def _mm_tn_bwd(res, g):
    a, b = res
    return _dot(b, g, 1, 1).astype(a.dtype), _dot(a, g, 1, 0).astype(b.dtype)


mm_tn.defvjp(_mm_tn_fwd, _mm_tn_bwd)


def matmul(a, b, *, mode, out_dtype, name, n=None, b_off=0, tm_cap=1024, tn_cap=512, tk_cap=1536):
    if mode == "tn":
        k_dim, m_dim = a.shape
    else:
        m_dim, k_dim = a.shape
    n_dim = n if n is not None else (b.shape[0] if mode == "nt" else b.shape[1])
    tm = m_dim if m_dim < LANES else _pick(m_dim, tm_cap)
    tn = _pick(n_dim, tn_cap)
    tk = k_dim if k_dim < LANES else _pick(k_dim, tk_cap)
    assert m_dim % tm == 0 and n_dim % tn == 0 and k_dim % tk == 0 and b_off % tn == 0
    nk = k_dim // tk
    off = b_off // tn
    if mode == "nn":
        a_spec = pl.BlockSpec((tm, tk), lambda i, j, k: (i, k))
        b_spec = pl.BlockSpec((tk, tn), lambda i, j, k: (k, j))
        ca, cb = 1, 0
    elif mode == "nt":
        a_spec = pl.BlockSpec((tm, tk), lambda i, j, k: (i, k))
        b_spec = pl.BlockSpec((tn, tk), lambda i, j, k: (j + off, k))
        ca, cb = 1, 1
    else:
        a_spec = pl.BlockSpec((tk, tm), lambda i, j, k: (k, i))
        b_spec = pl.BlockSpec((tk, tn), lambda i, j, k: (k, j))
        ca, cb = 0, 0

    def body(a_ref, b_ref, o_ref, acc_ref):
        part = _dot(a_ref[...], b_ref[...], ca, cb)
        if nk == 1:
            o_ref[...] = part.astype(o_ref.dtype)
        else:
            k = pl.program_id(2)

            @pl.when(k == 0)
            def _():
                acc_ref[...] = part

            @pl.when(k > 0)
            def _():
                acc_ref[...] += part

            @pl.when(k == nk - 1)
            def _():
                o_ref[...] = acc_ref[...].astype(o_ref.dtype)

    return pl.pallas_call(
        body, name=name,
        out_shape=jax.ShapeDtypeStruct((m_dim, n_dim), out_dtype),
        grid=(m_dim // tm, n_dim // tn, nk),
        in_specs=[a_spec, b_spec],
        out_specs=pl.BlockSpec((tm, tn), lambda i, j, k: (i, j)),
        scratch_shapes=[pltpu.VMEM((tm, tn), F32)],
        compiler_params=_cparams("parallel", "parallel", "arbitrary"),
    )(a, b)


def _f32(xs):
    return [x.astype(F32) for x in xs]


def rowmap(f, rows, consts, out_dtypes, *, name, tr=256):
    r_dim = rows[0].shape[0]
    tr = _pick(r_dim, tr, mult=8)
    assert r_dim % tr == 0
    nr, nc = len(rows), len(consts)
    outs = jax.eval_shape(lambda *xs: f(*xs), *[jax.ShapeDtypeStruct((tr, x.shape[1]), F32) for x in rows],
                          *[jax.ShapeDtypeStruct(x.shape, F32) for x in consts])

    def body(*refs):
        res = f(*_f32([r[...] for r in refs[:nr + nc]]))
        for o_ref, o in zip(refs[nr + nc:], res, strict=True):
            o_ref[...] = o.astype(o_ref.dtype)

    return pl.pallas_call(
        body, name=name,
        out_shape=[jax.ShapeDtypeStruct((r_dim, o.shape[1]), dt) for o, dt in zip(outs, out_dtypes, strict=True)],
        grid=(r_dim // tr,),
        in_specs=[pl.BlockSpec((tr, x.shape[1]), lambda i: (i, 0)) for x in rows]
        + [pl.BlockSpec(x.shape, lambda i: (0, 0)) for x in consts],
        out_specs=[pl.BlockSpec((tr, o.shape[1]), lambda i: (i, 0)) for o in outs],
        compiler_params=_cparams("parallel"),
    )(*rows, *consts)


def rowmap_bwd(f, rows, consts, cts, *, name, row_grad, row_dtypes=None, tr=256, emit=(), row_add=None):
    r_dim = rows[0].shape[0]
    tr = _pick(r_dim, tr, mult=8)
    assert r_dim % tr == 0
    nr, nc, nct = len(rows), len(consts), len(cts)
    gi = [i for i, flag in enumerate(row_grad) if flag]
    row_dtypes = row_dtypes or [F32] * len(gi)
    row_add = row_add or [None] * len(gi)
    adds = [a for a in row_add if a is not None]
    outs = jax.eval_shape(lambda *xs: f(*xs), *[jax.ShapeDtypeStruct((tr, x.shape[1]), F32) for x in rows],
                          *[jax.ShapeDtypeStruct(x.shape, F32) for x in consts])

    def body(*refs):
        ins = _f32([r[...] for r in refs[:nr + nc]])
        ct = _f32([r[...] for r in refs[nr + nc:nr + nc + nct]])
        add_refs = list(refs[nr + nc + nct:nr + nc + nct + len(adds)])
        o_refs = refs[nr + nc + nct + len(adds):]
        res, vjp = jax.vjp(f, *ins)
        grads = vjp(tuple(ct))
        for o_ref, i, a in zip(o_refs[:len(gi)], gi, row_add):
            g = grads[i] if a is None else grads[i] + add_refs.pop(0)[...].astype(F32)
            o_ref[...] = g.astype(o_ref.dtype)
        first = pl.program_id(0) == 0
        for o_ref, g in zip(o_refs[len(gi):len(gi) + nc], grads[nr:]):
            @pl.when(first)
            def _(o_ref=o_ref, g=g):
                o_ref[...] = g

            @pl.when(jnp.logical_not(first))
            def _(o_ref=o_ref, g=g):
                o_ref[...] += g
        for o_ref, i in zip(o_refs[len(gi) + nc:], emit):
            o_ref[...] = res[i].astype(o_ref.dtype)

    out_shape = ([jax.ShapeDtypeStruct(rows[i].shape, dt) for i, dt in zip(gi, row_dtypes, strict=True)]
                 + [jax.ShapeDtypeStruct(x.shape, F32) for x in consts]
                 + [jax.ShapeDtypeStruct((r_dim, outs[i].shape[1]), F32) for i in emit])
    out_specs = ([pl.BlockSpec((tr, rows[i].shape[1]), lambda i_: (i_, 0)) for i in gi]
                 + [pl.BlockSpec(x.shape, lambda i_: (0, 0)) for x in consts]
                 + [pl.BlockSpec((tr, outs[i].shape[1]), lambda i_: (i_, 0)) for i in emit])
    res = pl.pallas_call(
        body, name=name,
        out_shape=out_shape,
        grid=(r_dim // tr,),
        in_specs=[pl.BlockSpec((tr, x.shape[1]), lambda i: (i, 0)) for x in rows]
        + [pl.BlockSpec(x.shape, lambda i: (0, 0)) for x in consts]
        + [pl.BlockSpec((tr, x.shape[1]), lambda i: (i, 0)) for x in list(cts) + adds],
        out_specs=out_specs,
        compiler_params=_cparams("arbitrary"),
    )(*rows, *consts, *cts, *adds)
    return res[:len(gi)], res[len(gi):len(gi) + nc], res[len(gi) + nc:]


CONV_HALO = 32
CONV_CHUNK = 256


def conv_fwd(x, w, b, *, name, cb=256):
    s_dim, c_dim = x.shape
    taps = w.shape[0]
    assert taps - 1 <= CONV_HALO and s_dim % CONV_CHUNK == 0 and c_dim % cb == 0
    n_chunks = s_dim // CONV_CHUNK
    ext = CONV_CHUNK + CONV_HALO

    def body(x_ref, w_ref, b_ref, o_ref, xp_ref):
        xp_ref[pl.ds(0, CONV_HALO), :] = jnp.zeros((CONV_HALO, cb), F32)
        xp_ref[pl.ds(CONV_HALO, s_dim), :] = x_ref[...].astype(F32)
        wv = w_ref[...].astype(F32)
        bv = b_ref[...].astype(F32)

        def chunk(t, carry):
            base = pl.multiple_of(t * CONV_CHUNK, CONV_CHUNK)
            xe = xp_ref[pl.ds(base, ext), :]
            acc = jnp.broadcast_to(bv, (CONV_CHUNK, cb))
            for j in range(taps):
                sh = xe if j == 0 else pltpu.roll(xe, shift=j, axis=0)
                acc = acc + wv[taps - 1 - j:taps - j, :] * sh[CONV_HALO:, :]
            o_ref[pl.ds(base, CONV_CHUNK), :] = acc
            return carry

        lax.fori_loop(0, n_chunks, chunk, 0)

    return pl.pallas_call(
        body, name=name,
        out_shape=jax.ShapeDtypeStruct((s_dim, c_dim), F32),
        grid=(c_dim // cb,),
        in_specs=[pl.BlockSpec((s_dim, cb), lambda i: (0, i)), pl.BlockSpec((taps, cb), lambda i: (0, i)),
                  pl.BlockSpec((1, cb), lambda i: (0, i))],
        out_specs=pl.BlockSpec((s_dim, cb), lambda i: (0, i)),
        scratch_shapes=[pltpu.VMEM((s_dim + CONV_HALO, cb), F32)],
        compiler_params=_cparams("parallel"),
    )(x, w, b)


def conv_bwd(x, w, g, *, name, cb=256):
    s_dim, c_dim = x.shape
    taps = w.shape[0]
    n_chunks = s_dim // CONV_CHUNK
    ext = CONV_CHUNK + CONV_HALO
    taps_pad = -(-taps // 8) * 8

    def body(x_ref, w_ref, g_ref, dx_ref, dw_ref, db_ref, xp_ref, gp_ref, acc_ref):
        xp_ref[pl.ds(0, CONV_HALO), :] = jnp.zeros((CONV_HALO, cb), F32)
        xp_ref[pl.ds(CONV_HALO, s_dim), :] = x_ref[...].astype(F32)
        gp_ref[pl.ds(0, s_dim), :] = g_ref[...].astype(F32)
        gp_ref[pl.ds(s_dim, CONV_HALO), :] = jnp.zeros((CONV_HALO, cb), F32)
        acc_ref[...] = jnp.zeros_like(acc_ref)
        wv = w_ref[...].astype(F32)

        def chunk(t, carry):
            base = pl.multiple_of(t * CONV_CHUNK, CONV_CHUNK)
            xe = xp_ref[pl.ds(base, ext), :]
            ge = gp_ref[pl.ds(base, ext), :]
            gc = ge[:CONV_CHUNK, :]
            dx = jnp.zeros((CONV_CHUNK, cb), F32)
            for j in range(taps):
                xs = xe if j == 0 else pltpu.roll(xe, shift=j, axis=0)
                gs = ge if j == 0 else pltpu.roll(ge, shift=ext - j, axis=0)
                k = taps - 1 - j
                dx = dx + wv[k:k + 1, :] * gs[:CONV_CHUNK, :]
                acc_ref[k:k + 1, :] += jnp.sum(gc * xs[CONV_HALO:, :], axis=0, keepdims=True)
            acc_ref[taps_pad:taps_pad + 1, :] += jnp.sum(gc, axis=0, keepdims=True)
            dx_ref[pl.ds(base, CONV_CHUNK), :] = dx
            return carry

        lax.fori_loop(0, n_chunks, chunk, 0)
        dw_ref[...] = acc_ref[0:taps, :]
        db_ref[...] = acc_ref[taps_pad:taps_pad + 1, :]

    return pl.pallas_call(
        body, name=name,
        out_shape=[jax.ShapeDtypeStruct((s_dim, c_dim), F32), jax.ShapeDtypeStruct((taps, c_dim), F32),
                   jax.ShapeDtypeStruct((1, c_dim), F32)],
        grid=(c_dim // cb,),
        in_specs=[pl.BlockSpec((s_dim, cb), lambda i: (0, i)), pl.BlockSpec((taps, cb), lambda i: (0, i)),
                  pl.BlockSpec((s_dim, cb), lambda i: (0, i))],
        out_specs=[pl.BlockSpec((s_dim, cb), lambda i: (0, i)), pl.BlockSpec((taps, cb), lambda i: (0, i)),
                   pl.BlockSpec((1, cb), lambda i: (0, i))],
        scratch_shapes=[pltpu.VMEM((s_dim + CONV_HALO, cb), F32), pltpu.VMEM((s_dim + CONV_HALO, cb), F32),
                        pltpu.VMEM((taps_pad + 8, cb), F32)],
        compiler_params=_cparams("parallel"),
    )(x, w, g)


def _iota2(n, axis):
    return lax.broadcasted_iota(jnp.int32, (n, n), axis)


def _to_col(row):
    n = row.shape[1]
    return jnp.sum(jnp.where(_iota2(n, 0) == _iota2(n, 1), jnp.broadcast_to(row, (n, n)), 0.0), axis=1, keepdims=True)


def _softplus(x):
    return jnp.maximum(x, 0.0) + jnp.log(1.0 + jnp.exp(-jnp.abs(x)))


def ssd_heads(x, dtraw, dt_bias, a_log, dskip, bm, cm, prev):
    h, q, _ = x.shape
    n = bm.shape[1]
    li = lax.broadcasted_iota(jnp.int32, (1, q, q), 1)
    si = lax.broadcasted_iota(jnp.int32, (1, q, q), 2)

    def to_col(row):
        return jnp.sum(jnp.where(li == si, jnp.broadcast_to(row, (h, q, q)), 0.0), axis=2, keepdims=True)

    dt_row = _softplus(dtraw + dt_bias)
    a_row = dt_row * (-jnp.exp(a_log))
    a_col = to_col(a_row)
    acs_col = jnp.sum(jnp.where(si <= li, jnp.broadcast_to(a_row, (h, q, q)), 0.0), axis=2, keepdims=True)
    acs_row = jnp.sum(jnp.where(li <= si, jnp.broadcast_to(a_col, (h, q, q)), 0.0), axis=1, keepdims=True)
    total = jnp.sum(a_row, axis=2, keepdims=True)
    xdt = x * to_col(dt_row)
    lmat = jnp.exp(jnp.where(li >= si, acs_col - acs_row, -1e30))
    bmb = jnp.broadcast_to(bm[None], (h, q, n))
    cmb = jnp.broadcast_to(cm[None], (h, q, n))
    y = bmm(mm_nt(cm, bm)[None] * lmat, xdt)
    y = y + bmm_nt(cmb, prev) * jnp.exp(acs_col)
    y = y + dskip * x
    state = bmm_tn(xdt * jnp.exp(total - acs_col), bmb)
    return y, jnp.exp(total) * prev + state


HEADS_PER_GROUP = SSM_HEADS // SSM_GROUPS
BM_COL0 = SSM_INNER // SSM_STATE
CM_COL0 = BM_COL0 + SSM_GROUPS


def ssd_fwd(xs_hm, dtraw_t, dt_bias, a_log, dskip, xbc):
    hg = HEADS_PER_GROUP

    def body(x_ref, dt_ref, dtb_ref, al_ref, dk_ref, bm_ref, cm_ref, y_ref, prev_ref, state_ref):
        @pl.when(pl.program_id(1) == 0)
        def _():
            state_ref[...] = jnp.zeros_like(state_ref)

        prev = state_ref[...]
        prev_ref[0] = prev
        y, nxt = ssd_heads(x_ref[...], dt_ref[...], dtb_ref[...], al_ref[...], dk_ref[...], bm_ref[...], cm_ref[...], prev)
        y_ref[...] = y
        state_ref[...] = nxt

    hp = pl.BlockSpec((hg, 1, 1), lambda g, c: (g, 0, 0))
    dtraw_t, dt_bias, a_log, dskip = [a.reshape(SSM_HEADS, 1, -1) for a in (dtraw_t, dt_bias, a_log, dskip)]
    return pl.pallas_call(
        body, name="ssd_fwd",
        out_shape=[jax.ShapeDtypeStruct((SSM_HEADS, S, SSM_HDIM), F32),
                   jax.ShapeDtypeStruct((N_CHUNKS, SSM_HEADS, SSM_HDIM, SSM_STATE), F32)],
        grid=(SSM_GROUPS, N_CHUNKS),
        in_specs=[pl.BlockSpec((hg, CHUNK, SSM_HDIM), lambda g, c: (g, c, 0)),
                  pl.BlockSpec((hg, 1, CHUNK), lambda g, c: (g, 0, c)), hp, hp, hp,
                  pl.BlockSpec((CHUNK, SSM_STATE), lambda g, c: (c, BM_COL0 + g)),
                  pl.BlockSpec((CHUNK, SSM_STATE), lambda g, c: (c, CM_COL0 + g))],
        out_specs=[pl.BlockSpec((hg, CHUNK, SSM_HDIM), lambda g, c: (g, c, 0)),
                   pl.BlockSpec((1, hg, SSM_HDIM, SSM_STATE), lambda g, c: (c, g, 0, 0))],
        scratch_shapes=[pltpu.VMEM((hg, SSM_HDIM, SSM_STATE), F32)],
        compiler_params=_cparams("parallel", "arbitrary"),
    )(xs_hm, dtraw_t, dt_bias, a_log, dskip, xbc, xbc)


def ssd_bwd(xs_hm, dtraw_t, dt_bias, a_log, dskip, xbc, prev_all, dy_hm):
    hg = HEADS_PER_GROUP
    last = N_CHUNKS - 1

    def body(x_ref, dt_ref, dtb_ref, al_ref, dk_ref, bm_ref, cm_ref, prev_ref, dy_ref,
             dx_ref, ddt_ref, ddtb_ref, dal_ref, ddk_ref, dbm_ref, dcm_ref, dstate_ref):
        @pl.when(pl.program_id(1) == 0)
        def _():
            dstate_ref[...] = jnp.zeros_like(dstate_ref)
            ddtb_ref[...] = jnp.zeros_like(ddtb_ref)
            dal_ref[...] = jnp.zeros_like(dal_ref)
            ddk_ref[...] = jnp.zeros_like(ddk_ref)

        _, vjp = jax.vjp(ssd_heads, x_ref[...], dt_ref[...], dtb_ref[...], al_ref[...], dk_ref[...], bm_ref[...],
                         cm_ref[...], prev_ref[0])
        dx, ddt, ddtb, dal, ddk, dbm, dcm, dprev = vjp((dy_ref[...], dstate_ref[...]))
        dx_ref[...] = dx
        ddt_ref[...] = ddt
        ddtb_ref[...] += ddtb
        dal_ref[...] += dal
        ddk_ref[...] += ddk
        dbm_ref[...] = dbm
        dcm_ref[...] = dcm
        dstate_ref[...] = dprev

    hp = pl.BlockSpec((hg, 1, 1), lambda g, c: (g, 0, 0))
    xspec = pl.BlockSpec((hg, CHUNK, SSM_HDIM), lambda g, c: (g, last - c, 0))
    tspec = pl.BlockSpec((hg, 1, CHUNK), lambda g, c: (g, 0, last - c))
    gspec = pl.BlockSpec((CHUNK, SSM_STATE), lambda g, c: (last - c, g))
    dtraw_t, dt_bias, a_log, dskip = [a.reshape(SSM_HEADS, 1, -1) for a in (dtraw_t, dt_bias, a_log, dskip)]
    res = pl.pallas_call(
        body, name="ssd_bwd",
        out_shape=[jax.ShapeDtypeStruct((SSM_HEADS, S, SSM_HDIM), F32), jax.ShapeDtypeStruct((SSM_HEADS, 1, S), F32),
                   jax.ShapeDtypeStruct((SSM_HEADS, 1, 1), F32), jax.ShapeDtypeStruct((SSM_HEADS, 1, 1), F32),
                   jax.ShapeDtypeStruct((SSM_HEADS, 1, 1), F32),
                   jax.ShapeDtypeStruct((S, SSM_GROUPS * SSM_STATE), F32),
                   jax.ShapeDtypeStruct((S, SSM_GROUPS * SSM_STATE), F32)],
        grid=(SSM_GROUPS, N_CHUNKS),
        in_specs=[xspec, tspec, hp, hp, hp,
                  pl.BlockSpec((CHUNK, SSM_STATE), lambda g, c: (last - c, BM_COL0 + g)),
                  pl.BlockSpec((CHUNK, SSM_STATE), lambda g, c: (last - c, CM_COL0 + g)),
                  pl.BlockSpec((1, hg, SSM_HDIM, SSM_STATE), lambda g, c: (last - c, g, 0, 0)), xspec],
        out_specs=[xspec, tspec, hp, hp, hp, gspec, gspec],
        scratch_shapes=[pltpu.VMEM((hg, SSM_HDIM, SSM_STATE), F32)],
        compiler_params=_cparams("parallel", "arbitrary"),
    )(xs_hm, dtraw_t, dt_bias, a_log, dskip, xbc, xbc, prev_all, dy_hm)
    return [res[0]] + [r.reshape(SSM_HEADS, -1) for r in res[1:5]] + list(res[5:])


ATT_HB = 8


def _bdot(a, b, ca, cb):
    return lax.dot_general(a.astype(BF16), b.astype(BF16), (((ca,), (cb,)), ((0,), (0,))), preferred_element_type=F32)


@jax.custom_vjp
def bmm(a, b):
    return _bdot(a, b, 2, 1)


def _bmm_fwd(a, b):
    return _bdot(a, b, 2, 1), (a, b)


def _bmm_bwd(res, g):
    a, b = res
    return _bdot(g, b, 2, 2).astype(a.dtype), _bdot(a, g, 1, 1).astype(b.dtype)


bmm.defvjp(_bmm_fwd, _bmm_bwd)


@jax.custom_vjp
def bmm_nt(a, b):
    return _bdot(a, b, 2, 2)


def _bmm_nt_fwd(a, b):
    return _bdot(a, b, 2, 2), (a, b)


def _bmm_nt_bwd(res, g):
    a, b = res
    return _bdot(g, b, 2, 1).astype(a.dtype), _bdot(g, a, 1, 1).astype(b.dtype)


bmm_nt.defvjp(_bmm_nt_fwd, _bmm_nt_bwd)


@jax.custom_vjp
def bmm_tn(a, b):
    return _bdot(a, b, 1, 1)


def _bmm_tn_fwd(a, b):
    return _bdot(a, b, 1, 1), (a, b)


def _bmm_tn_bwd(res, g):
    a, b = res
    return _bdot(b, g, 2, 2).astype(a.dtype), _bdot(a, g, 2, 1).astype(b.dtype)


bmm_tn.defvjp(_bmm_tn_fwd, _bmm_tn_bwd)


def att_heads(q, kp, kc, vp, vc, bias_p, bias_c, has_prev):
    h, b, dh = q.shape
    i = lax.broadcasted_iota(jnp.int32, (1, b, b), 1)
    j = lax.broadcasted_iota(jnp.int32, (1, b, b), 2)
    scale = dh ** -0.5
    sp = jnp.where(jnp.logical_and(j >= i, has_prev), bmm_nt(q, kp) * scale + bias_p, -1e30)
    sc = jnp.where(j <= i, bmm_nt(q, kc) * scale + bias_c, -1e30)
    m = lax.stop_gradient(jnp.maximum(jnp.max(sp, axis=2, keepdims=True), jnp.max(sc, axis=2, keepdims=True)))
    pp, pc = jnp.exp(sp - m), jnp.exp(sc - m)
    l = jnp.sum(pp, axis=2, keepdims=True) + jnp.sum(pc, axis=2, keepdims=True)
    o = bmm(pp / l, vp) + bmm(pc / l, vc)
    return o, jnp.broadcast_to(m + jnp.log(l), (h, b, dh))


def _att_specs(nb):
    hb, blk = ATT_HB, ATT_BLK
    cur = pl.BlockSpec((hb, blk, ATT_HDIM), lambda h, b: (h, b, 0))
    prv = pl.BlockSpec((hb, blk, ATT_HDIM), lambda h, b: (h, jnp.maximum(b - 1, 0), 0))
    bias = pl.BlockSpec((hb, 2, blk, blk), lambda h, b: (h, 0, 0, 0))
    return cur, prv, bias


def att_fwd(q, k, v, bias, nb, *, name):
    cur, prv, bspec = _att_specs(nb)

    def body(q_ref, kp_ref, kc_ref, vp_ref, vc_ref, b_ref, o_ref, l_ref):
        has_prev = (pl.program_id(1) % nb) != 0
        o, lse = att_heads(q_ref[...], kp_ref[...], kc_ref[...], vp_ref[...], vc_ref[...], b_ref[:, 0], b_ref[:, 1],
                           has_prev)
        o_ref[...] = o
        l_ref[...] = lse

    shp = jax.ShapeDtypeStruct((ATT_HEADS, S, ATT_HDIM), F32)
    return pl.pallas_call(
        body, name=name, out_shape=[shp, shp],
        grid=(ATT_HEADS // ATT_HB, S // ATT_BLK),
        in_specs=[cur, prv, cur, prv, cur, bspec],
        out_specs=[cur, cur],
        compiler_params=_cparams("parallel", "parallel"),
    )(q, k, k, v, v, bias)


def att_bwd(q, k, v, bias, do, dlse, nb, *, name):
    cur, prv, bspec = _att_specs(nb)

    def body(q_ref, kp_ref, kc_ref, vp_ref, vc_ref, b_ref, do_ref, dl_ref,
             dq_ref, dkc_ref, dkp_ref, dvc_ref, dvp_ref, db_ref):
        has_prev = (pl.program_id(1) % nb) != 0

        @pl.when(pl.program_id(1) == 0)
        def _():
            db_ref[...] = jnp.zeros_like(db_ref)

        ins = _f32([q_ref[...], kp_ref[...], kc_ref[...], vp_ref[...], vc_ref[...]]) + [b_ref[:, 0], b_ref[:, 1]]
        _, vjp = jax.vjp(functools.partial(att_heads, has_prev=has_prev), *ins)
        dq, dkp, dkc, dvp, dvc, dbp, dbc = vjp((do_ref[...], dl_ref[...]))
        dq_ref[...] = dq
        dkc_ref[...] = dkc
        dkp_ref[...] = dkp
        dvc_ref[...] = dvc
        dvp_ref[...] = dvp
        db_ref[:, 0] += dbp
        db_ref[:, 1] += dbc

    shp = jax.ShapeDtypeStruct((ATT_HEADS, S, ATT_HDIM), F32)
    return pl.pallas_call(
        body, name=name,
        out_shape=[shp] * 5 + [jax.ShapeDtypeStruct((ATT_HEADS, 2, ATT_BLK, ATT_BLK), F32)],
        grid=(ATT_HEADS // ATT_HB, S // ATT_BLK),
        in_specs=[cur, prv, cur, prv, cur, bspec, cur, cur],
        out_specs=[cur] * 5 + [bspec],
        compiler_params=_cparams("parallel", "arbitrary"),
    )(q, k, k, v, v, bias, do, dlse)


def shift_add(cur, prev, nb, *, name):
    n_blocks = S // ATT_BLK

    def body(c_ref, p_ref, o_ref):
        nxt = pl.program_id(0) + 1
        keep = jnp.where((nxt % nb) != 0, 1.0, 0.0)
        o_ref[...] = c_ref[...] + keep * p_ref[...]

    return pl.pallas_call(
        body, name=name, out_shape=jax.ShapeDtypeStruct(cur.shape, F32),
        grid=(n_blocks,),
        in_specs=[pl.BlockSpec((ATT_HEADS, ATT_BLK, ATT_HDIM), lambda b: (0, b, 0)),
                  pl.BlockSpec((ATT_HEADS, ATT_BLK, ATT_HDIM), lambda b: (0, jnp.minimum(b + 1, n_blocks - 1), 0))],
        out_specs=pl.BlockSpec((ATT_HEADS, ATT_BLK, ATT_HDIM), lambda b: (0, b, 0)),
        compiler_params=_cparams("parallel"),
    )(cur, prev)


def _silu(x):
    return x * jax.nn.sigmoid(x)


def _rms(x):
    return x * lax.rsqrt(jnp.mean(x * x, -1, keepdims=True) + EPS)


def f_normmod(x, g, sc, sh):
    return (_rms(x) * g * (1.0 + sc) + sh,)


def f_resid(x, mix, gate):
    return (x + gate * mix,)


def f_resid_bias(x, mix, gate, b):
    return (x + gate * (mix + b),)


def f_swiglu(hgu):
    return (_silu(hgu[:, :FFN_HIDDEN]) * hgu[:, FFN_HIDDEN:],)


def f_silu(x):
    return (_silu(x),)


def f_gated_norm(y, z, g):
    return (_rms(y * _silu(z)) * g,)


def f_glu(y, b):
    y = y + b
    return (y[:, :D] * jax.nn.sigmoid(y[:, D:]),)


def f_ln_silu(u, g, b):
    mu = jnp.mean(u, -1, keepdims=True)
    var = jnp.mean(jnp.square(u - mu), -1, keepdims=True)
    return (_silu((u - mu) * lax.rsqrt(var + EPS) * g + b),)


def f_combine(o1, o2, o3, l1, l2, l3):
    m = lax.stop_gradient(jnp.maximum(jnp.maximum(l1, l2), l3))
    e1, e2, e3 = jnp.exp(l1 - m), jnp.exp(l2 - m), jnp.exp(l3 - m)
    return ((e1 * o1 + e2 * o2 + e3 * o3) / (e1 + e2 + e3),)


def f_head(x, tgt, g):
    return (0.5 * jnp.mean(jnp.square(_rms(x) * g - tgt), -1, keepdims=True),)


def f_sum3(a, b, c):
    return (a + b + c,)


def f_sum4(a, b, c, d):
    return (a + b + c + d,)


def f_add(a, b):
    return (a + b,)


def f_adamw(w, g, m, v):
    m = ADAM_B1 * m + (1.0 - ADAM_B1) * g
    v = ADAM_B2 * v + (1.0 - ADAM_B2) * jnp.square(g)
    m_hat = m / (1.0 - ADAM_B1 ** ADAM_STEP)
    v_hat = v / (1.0 - ADAM_B2 ** ADAM_STEP)
    return -ADAM_LR * (m_hat / (jnp.sqrt(v_hat) + ADAM_EPS) + ADAM_WD * w), m, v


def _rows_tile(r, cap=256):
    return _pick(r, cap, mult=8)


def adamw(w, g, m, v, *, name):
    shape = w.shape
    c_dim = shape[-1] if len(shape) > 1 else shape[0]
    flat = [a.reshape(-1, c_dim) for a in (w, g, m, v)]
    res = rowmap(f_adamw, flat, [], [F32] * 3, name=name, tr=_rows_tile(flat[0].shape[0], cap=128))
    return [r.reshape(shape) for r in res]


def _t5_bucket(dist):
    max_exact = REL_BUCKETS // 2
    n = jnp.maximum(dist, 1).astype(F32)
    large = max_exact + jnp.log(n / max_exact) / math.log(REL_MAX_DIST / max_exact) * (REL_BUCKETS - max_exact)
    large = jnp.minimum(large.astype(jnp.int32), REL_BUCKETS - 1)
    return jnp.where(dist < max_exact, dist, large)


def _att_buckets(dil):
    i = jnp.arange(ATT_BLK)[:, None]
    j = jnp.arange(2 * ATT_BLK)[None, :]
    bkt = _t5_bucket(jnp.maximum(ATT_BLK + i - j, 0) * dil)
    return jnp.transpose(bkt.reshape(ATT_BLK, 2, ATT_BLK), (1, 0, 2))


def att_bias(rel_table, p, dil):
    tab = rel_table[:, p * ATT_HEADS:(p + 1) * ATT_HEADS]
    onehot = (jnp.arange(REL_BUCKETS)[:, None] == _att_buckets(dil).reshape(1, -1)).astype(F32)
    bias = lax.dot_general(tab, onehot, (((0,), (0,)), ((), ())), precision=lax.Precision.HIGHEST)
    return bias.reshape(ATT_HEADS, 2, ATT_BLK, ATT_BLK)


def att_bias_grad(dbias, dil, *, name):
    onehot = (_att_buckets(dil).reshape(-1, 1) == jnp.arange(LANES)[None, :]).astype(BF16)
    dtab = matmul(dbias.reshape(ATT_HEADS, -1), onehot, mode="nn", out_dtype=F32, name=name, tk_cap=2048)
    return dtab[:, :REL_BUCKETS].T


def to_heads(a, n_heads, dil=1):
    hd = a.shape[1] // n_heads
    return jnp.transpose(a.reshape(S // dil, dil, n_heads, hd), (2, 1, 0, 3)).reshape(n_heads, S, hd)


def from_heads(a, dil=1):
    n_heads, _, hd = a.shape
    return jnp.transpose(a.reshape(n_heads, dil, S // dil, hd), (2, 1, 0, 3)).reshape(S, n_heads * hd)


def regroup_heads(a, dil, inverse=False):
    n_heads, _, hd = a.shape
    if dil == 1:
        return a
    if inverse:
        return jnp.transpose(a.reshape(n_heads, dil, S // dil, hd), (0, 2, 1, 3)).reshape(n_heads, S, hd)
    return jnp.transpose(a.reshape(n_heads, S // dil, dil, hd), (0, 2, 1, 3)).reshape(n_heads, S, hd)


HY_Z, HY_XBC, HY_DT, HY_Q, HY_K, HY_V = 2048, 3072, 32, 3072, 1024, 1024
HY_IN = HY_Z + HY_XBC + HY_DT + HY_Q + HY_K + HY_V
OFF_Z, OFF_XBC, OFF_Q, OFF_KV, OFF_DT = 0, 2048, 5120, 8192, 10240
HY_CAT = OFF_DT + LANES
DT_PAD = LANES


def hy_to_cat(w):
    z, xbc, dt, qkv = w[:2048], w[2048:5120], w[5120:5152], w[5152:]
    return jnp.concatenate([z, xbc, qkv, dt, jnp.zeros((DT_PAD - HY_DT,) + w.shape[1:], w.dtype)], axis=0)


def hy_from_cat(w):
    return jnp.concatenate([w[:5120], w[OFF_DT:OFF_DT + HY_DT], w[5120:OFF_DT]], axis=0)


def device_step(x, tgt, mods, wts, sp):
    g = {}
    dmods = [[None] * 6 for _ in range(2)]

    def normmod(xi, gain, sc, sh, nm):
        return rowmap(f_normmod, [xi], [gain, sc, sh], [BF16], name=nm)[0]

    def ffn_fwd(xi, i, gate, nm):
        h = normmod(xi, sp["norm_ffn_g"][i], mods[i][4], mods[i][3], nm + "_norm")
        hgu = matmul(h, wts["gu_t"][i], mode="nt", out_dtype=F32, name=nm + "_gu")
        act = rowmap(f_swiglu, [hgu], [], [BF16], name=nm + "_act", tr=128)[0]
        out = matmul(act, wts["down"][i], mode="nn", out_dtype=F32, name=nm + "_down")
        xo = rowmap(f_resid, [xi, out], [gate], [F32], name=nm + "_res")[0]
        return xo, (h, hgu, act, out)

    def ffn_bwd(dres, xi, i, saved, nm):
        h, hgu, act, out = saved
        (dout,), (dgate,), _ = rowmap_bwd(f_resid, [xi, out], [mods[i][5]], [dres], name=nm + "_res_b",
                                          row_grad=[False, True], row_dtypes=[BF16])
        dmods[i][5] = dgate
        dact = matmul(dout, wts["down"][i], mode="nt", out_dtype=F32, name=nm + "_down_dx")
        g[f"down{i}"] = matmul(act, dout, mode="tn", out_dtype=F32, name=nm + "_down_dw")
        (dhgu,), _, _ = rowmap_bwd(f_swiglu, [hgu], [], [dact], name=nm + "_act_b", row_grad=[True],
                                   row_dtypes=[BF16], tr=128)
        g[f"gu_t{i}"] = matmul(dhgu, h, mode="tn", out_dtype=F32, name=nm + "_gu_dw")
        dh = matmul(dhgu, wts["gu_t"][i], mode="nn", out_dtype=F32, name=nm + "_gu_dx")
        (dres,), (dg_, dsc, dsh), _ = rowmap_bwd(f_normmod, [xi], [sp["norm_ffn_g"][i], mods[i][4], mods[i][3]], [dh],
                                                 name=nm + "_norm_b", row_grad=[True], row_add=[dres])
        g[f"norm_ffn_g{i}"] = dg_
        dmods[i][4], dmods[i][3] = dsc, dsh
        return dres

    h0 = normmod(x, sp["norm_mix_g"][0], mods[0][1], mods[0][0], "l0_norm")
    w_in = wts["hy_in_t"]
    z = matmul(h0, w_in, mode="nt", out_dtype=F32, name="hy_z", n=HY_Z, b_off=OFF_Z)
    xbc_raw = matmul(h0, w_in, mode="nt", out_dtype=F32, name="hy_xbc", n=HY_XBC, b_off=OFF_XBC)
    q = matmul(h0, w_in, mode="nt", out_dtype=BF16, name="hy_q", n=HY_Q, b_off=OFF_Q)
    kv = matmul(h0, w_in, mode="nt", out_dtype=BF16, name="hy_kv", n=HY_K + HY_V, b_off=OFF_KV)
    dtr = matmul(h0, w_in, mode="nt", out_dtype=F32, name="hy_dt", n=DT_PAD, b_off=OFF_DT)
    xbc_pre = conv_fwd(xbc_raw, sp["hy_conv_w"], sp["hy_conv_b"], name="hy_conv")
    xbc = rowmap(f_silu, [xbc_pre], [], [F32], name="hy_conv_act", tr=128)[0]
    xs_hm = to_heads(xbc[:, :SSM_INNER], SSM_HEADS)
    dtraw_t = dtr[:, :HY_DT].T
    y_hm, prev_all = ssd_fwd(xs_hm, dtraw_t, sp["hy_dt_bias"], sp["hy_a_log"], sp["hy_d_skip"], xbc)
    y = from_heads(y_hm)
    ysn = rowmap(f_gated_norm, [y, z], [sp["hy_ssm_norm_g"]], [BF16], name="hy_gnorm", tr=128)[0]
    k_all, v_all = kv[:, :HY_K], kv[:, HY_K:]
    att_in, att_o, att_l = [], [], []
    for p, (win, dil) in enumerate(ATT_PATTERNS):
        qp = to_heads(q[:, p * D:(p + 1) * D], ATT_HEADS, dil)
        kp, vp = to_heads(k_all, ATT_HEADS, dil), to_heads(v_all, ATT_HEADS, dil)
        bias = att_bias(sp["rel_table"], p, dil)
        nb = S // dil // ATT_BLK
        o, lse = att_fwd(qp, kp, vp, bias, nb, name=f"att_fwd{p}")
        att_in.append((qp, kp, vp, bias, nb))
        att_o.append(regroup_heads(o, dil, inverse=True).reshape(ATT_HEADS * S, ATT_HDIM))
        att_l.append(regroup_heads(lse, dil, inverse=True).reshape(ATT_HEADS * S, ATT_HDIM))
    att_hm = rowmap(f_combine, att_o + att_l, [], [BF16], name="att_combine", tr=2048)[0]
    att = from_heads(att_hm.reshape(ATT_HEADS, S, ATT_HDIM))
    cat = jnp.concatenate([ysn, att], axis=-1)
    mix0 = matmul(cat, wts["hy_out"], mode="nn", out_dtype=F32, name="hy_out")
    x1 = rowmap(f_resid, [x, mix0], [mods[0][2]], [F32], name="l0_res")[0]
    x2, ffn0 = ffn_fwd(x1, 0, mods[0][5], "ffn0")

    h1 = normmod(x2, sp["norm_mix_g"][1], mods[1][1], mods[1][0], "l1_norm")
    p1 = matmul(h1, wts["pw1_t"], mode="nt", out_dtype=F32, name="cv_pw1")
    u = rowmap(f_glu, [p1], [sp["cv_b_pw1"]], [F32], name="cv_glu")[0]
    uc = conv_fwd(u, sp["cv_w_dw"], sp["cv_b_dw"], name="cv_conv")
    ul = rowmap(f_ln_silu, [uc], [sp["cv_ln_g"], sp["cv_ln_b"]], [BF16], name="cv_ln")[0]
    mix1 = matmul(ul, wts["pw2"], mode="nn", out_dtype=F32, name="cv_pw2")
    x3 = rowmap(f_resid_bias, [x2, mix1], [mods[1][2], sp["cv_b_pw2"]], [F32], name="l1_res")[0]
    x4, ffn1 = ffn_fwd(x3, 1, mods[1][5], "ffn1")

    ones = jnp.ones((S, 1), F32)
    (dres,), (dfinal,), (loss_rows,) = rowmap_bwd(f_head, [x4, tgt], [sp["final_norm_g"]], [ones], name="head",
                                                  row_grad=[True, False], emit=(0,))
    g["final_norm_g"] = dfinal

    dres = ffn_bwd(dres, x3, 1, ffn1, "ffn1")
    (dmix1,), (dg1, db2), _ = rowmap_bwd(f_resid_bias, [x2, mix1], [mods[1][2], sp["cv_b_pw2"]], [dres], name="l1_res_b",
                                         row_grad=[False, True], row_dtypes=[BF16])
    dmods[1][2] = dg1
    g["cv_b_pw2"] = db2
    dul = matmul(dmix1, wts["pw2"], mode="nt", out_dtype=F32, name="cv_pw2_dx")
    g["pw2"] = matmul(ul, dmix1, mode="tn", out_dtype=F32, name="cv_pw2_dw")
    (duc,), (g["cv_ln_g"], g["cv_ln_b"]), _ = rowmap_bwd(f_ln_silu, [uc], [sp["cv_ln_g"], sp["cv_ln_b"]], [dul],
                                                         name="cv_ln_b", row_grad=[True])
    du, g["cv_w_dw"], g["cv_b_dw"] = conv_bwd(u, sp["cv_w_dw"], duc, name="cv_conv_b")
    (dp1,), (g["cv_b_pw1"],), _ = rowmap_bwd(f_glu, [p1], [sp["cv_b_pw1"]], [du], name="cv_glu_b", row_grad=[True],
                                             row_dtypes=[BF16])
    g["pw1_t"] = matmul(dp1, h1, mode="tn", out_dtype=F32, name="cv_pw1_dw")
    dh1 = matmul(dp1, wts["pw1_t"], mode="nn", out_dtype=F32, name="cv_pw1_dx")
    (dres,), (dg_, dsc, dsh), _ = rowmap_bwd(f_normmod, [x2], [sp["norm_mix_g"][1], mods[1][1], mods[1][0]], [dh1],
                                             name="l1_norm_b", row_grad=[True], row_add=[dres])
    g["norm_mix_g1"] = dg_
    dmods[1][1], dmods[1][0] = dsc, dsh

    dres = ffn_bwd(dres, x1, 0, ffn0, "ffn0")
    (dmix0,), (dg1,), _ = rowmap_bwd(f_resid, [x, mix0], [mods[0][2]], [dres], name="l0_res_b",
                                     row_grad=[False, True], row_dtypes=[BF16])
    dmods[0][2] = dg1
    dysn = matmul(dmix0, wts["hy_out"], mode="nt", out_dtype=F32, name="hy_out_dy", n=SSM_INNER, b_off=0)
    datt = matmul(dmix0, wts["hy_out"], mode="nt", out_dtype=F32, name="hy_out_da", n=D, b_off=SSM_INNER)
    g["hy_out"] = matmul(cat, dmix0, mode="tn", out_dtype=F32, name="hy_out_dw")
    (dy, dz), (g["hy_ssm_norm_g"],), _ = rowmap_bwd(f_gated_norm, [y, z], [sp["hy_ssm_norm_g"]], [dysn], name="hy_gnorm_b",
                                                    row_grad=[True, True], tr=128)
    dxs_hm, ddtraw_t, g["hy_dt_bias"], g["hy_a_log"], g["hy_d_skip"], dbm, dcm = ssd_bwd(
        xs_hm, dtraw_t, sp["hy_dt_bias"], sp["hy_a_log"], sp["hy_d_skip"], xbc, prev_all, to_heads(dy, SSM_HEADS))
    dxbc = jnp.concatenate([from_heads(dxs_hm), dbm, dcm], axis=-1)
    (dxbc_pre,), _, _ = rowmap_bwd(f_silu, [xbc_pre], [], [dxbc], name="hy_conv_act_b", row_grad=[True], tr=128)
    dxbc_raw, g["hy_conv_w"], g["hy_conv_b"] = conv_bwd(xbc_raw, sp["hy_conv_w"], dxbc_pre, name="hy_conv_b")
    datt_hm = to_heads(datt, ATT_HEADS).reshape(ATT_HEADS * S, ATT_HDIM)
    dol, _, _ = rowmap_bwd(f_combine, att_o + att_l, [], [datt_hm], name="att_combine_b", row_grad=[True] * 6, tr=2048)
    dqs, dks, dvs, dtabs = [], [], [], []
    for p, (win, dil) in enumerate(ATT_PATTERNS):
        qp, kp, vp, bias, nb = att_in[p]
        do = regroup_heads(dol[p].reshape(ATT_HEADS, S, ATT_HDIM), dil)
        dl = regroup_heads(dol[3 + p].reshape(ATT_HEADS, S, ATT_HDIM), dil)
        dq, dkc, dkp, dvc, dvp, dbias = att_bwd(qp, kp, vp, bias, do, dl, nb, name=f"att_bwd{p}")
        dqs.append(from_heads(dq, dil))
        dks.append(from_heads(shift_add(dkc, dkp, nb, name=f"att_dk{p}"), dil))
        dvs.append(from_heads(shift_add(dvc, dvp, nb, name=f"att_dv{p}"), dil))
        dtabs.append(att_bias_grad(dbias, dil, name=f"att_dtab{p}"))
    g["rel_table"] = jnp.concatenate(dtabs, axis=1)
    dk = rowmap(f_sum3, dks, [], [F32], name="att_dk_sum")[0]
    dv = rowmap(f_sum3, dvs, [], [F32], name="att_dv_sum")[0]
    ddt = jnp.pad(ddtraw_t.T, ((0, 0), (0, DT_PAD - HY_DT)))
    dproj = jnp.concatenate([dz, dxbc_raw] + dqs + [dk, dv, ddt], axis=-1).astype(BF16)
    g["hy_in_t"] = matmul(dproj, h0, mode="tn", out_dtype=F32, name="hy_in_dw")
    dh0 = matmul(dproj, w_in, mode="nn", out_dtype=F32, name="hy_in_dx")
    (dres,), (dg_, dsc, dsh), _ = rowmap_bwd(f_normmod, [x], [sp["norm_mix_g"][0], mods[0][1], mods[0][0]], [dh0],
                                             name="l0_norm_b", row_grad=[True], row_add=[dres])
    g["norm_mix_g0"] = dg_
    dmods[0][1], dmods[0][0] = dsc, dsh
    return loss_rows, dres, g, dmods


ANY = pl.BlockSpec(memory_space=pl.ANY)
WHOLE_VMEM = pl.BlockSpec(memory_space=pltpu.VMEM)


def _place():
    return lax.axis_index("x"), lax.axis_index("y"), lax.axis_index("c")


def _other_chips(x, y):
    return [(1 - x, y), (x, 1 - y), (1 - x, 1 - y)]


def allgather_small(v, *, name):
    m_per = v.shape[0]

    def body(x_ref, out_ref, send_sems, recv_sems, local_sem):
        x, y, c = _place()
        me, sibling = (x, y, c), (x, y, 1 - c)
        chips = _other_chips(x, y)

        def rows(px, py, pc):
            return out_ref.at[pl.ds((4 * px + 2 * py + pc) * m_per, m_per), :]

        def copy(k, block, to, src=None):
            return pltpu.make_async_remote_copy(
                src_ref=rows(*block) if src is None else src, dst_ref=rows(*block),
                send_sem=send_sems.at[k], recv_sem=recv_sems.at[k], device_id=to, device_id_type=MESH)

        mine = pltpu.make_async_copy(x_ref, rows(*me), local_sem)
        mine.start()
        first = [copy(0, me, sibling, src=x_ref)]
        first += [copy(1 + j, me, (*chip, c), src=x_ref) for j, chip in enumerate(chips)]
        for cp in first:
            cp.start()
        passed = [copy(4 + j, (*chip, c), sibling) for j, chip in enumerate(chips)]
        for j, chip in enumerate(chips):
            copy(1 + j, (*chip, c), me).wait_recv()
            passed[j].start()
        copy(0, sibling, me).wait_recv()
        for j, chip in enumerate(chips):
            copy(4 + j, (*chip, 1 - c), me).wait_recv()
        for cp in first + passed:
            cp.wait_send()
        mine.wait()

    return pl.pallas_call(
        body, name=name,
        out_shape=jax.ShapeDtypeStruct((N_DEV * m_per, LANES), v.dtype),
        in_specs=[WHOLE_VMEM], out_specs=WHOLE_VMEM,
        scratch_shapes=[pltpu.SemaphoreType.DMA((7,)), pltpu.SemaphoreType.DMA((7,)), pltpu.SemaphoreType.DMA],
    )(v)


def allgather_chips(pack, *, name):
    half_rows = pack.shape[0] // 2

    def body(p_ref, o_ref, send_sems, recv_sems, local_sem):
        x, y, c = _place()
        chips = _other_chips(x, y)
        sibling = (x, y, 1 - c)
        my_half = pl.ds(c * half_rows, half_rows)
        its_half = pl.ds((1 - c) * half_rows, half_rows)
        mine = pltpu.make_async_copy(p_ref, o_ref.at[2 * x + y], local_sem)
        mine.start()
        sends = [pltpu.make_async_remote_copy(
            src_ref=p_ref.at[my_half], dst_ref=o_ref.at[2 * x + y, my_half],
            send_sem=send_sems.at[k], recv_sem=recv_sems.at[k],
            device_id=(cx, cy, c), device_id_type=MESH) for k, (cx, cy) in enumerate(chips)]
        for cp in sends:
            cp.start()
        passed = []
        for k, (cx, cy) in enumerate(chips):
            landed = o_ref.at[2 * cx + cy, my_half]
            pltpu.make_async_remote_copy(
                src_ref=p_ref.at[my_half], dst_ref=landed, send_sem=send_sems.at[k], recv_sem=recv_sems.at[k],
                device_id=(cx, cy, c), device_id_type=MESH).wait_recv()
            cp = pltpu.make_async_remote_copy(
                src_ref=landed, dst_ref=landed, send_sem=send_sems.at[3 + k], recv_sem=recv_sems.at[3 + k],
                device_id=sibling, device_id_type=MESH)
            cp.start()
            passed.append(cp)
        for k, (cx, cy) in enumerate(chips):
            from_sibling = o_ref.at[2 * cx + cy, its_half]
            pltpu.make_async_remote_copy(
                src_ref=from_sibling, dst_ref=from_sibling, send_sem=send_sems.at[3 + k], recv_sem=recv_sems.at[3 + k],
                device_id=sibling, device_id_type=MESH).wait_recv()
        for cp in sends + passed:
            cp.wait_send()
        mine.wait()

    return pl.pallas_call(
        body, name=name,
        out_shape=jax.ShapeDtypeStruct((N_CHIPS,) + pack.shape, pack.dtype),
        in_specs=[ANY], out_specs=ANY,
        scratch_shapes=[pltpu.SemaphoreType.DMA((6,)), pltpu.SemaphoreType.DMA((6,)), pltpu.SemaphoreType.DMA],
    )(pack)


def swap_halves(gpack, *, name):
    half_rows = gpack.shape[1] // 2

    def body(g_ref, r_ref, send_sems, recv_sems):
        x, y, c = _place()
        its_half = pl.ds((1 - c) * half_rows, half_rows)
        copies = [pltpu.make_async_remote_copy(
            src_ref=g_ref.at[s, its_half], dst_ref=r_ref.at[s], send_sem=send_sems.at[s], recv_sem=recv_sems.at[s],
            device_id=(x, y, 1 - c), device_id_type=MESH) for s in range(N_CHIPS)]
        for cp in copies:
            cp.start()
        for cp in copies:
            cp.wait()

    return pl.pallas_call(
        body, name=name,
        out_shape=jax.ShapeDtypeStruct((N_CHIPS, half_rows) + gpack.shape[2:], gpack.dtype),
        in_specs=[ANY], out_specs=ANY,
        scratch_shapes=[pltpu.SemaphoreType.DMA((N_CHIPS,)), pltpu.SemaphoreType.DMA((N_CHIPS,))],
    )(gpack)


def scatter_chips(gpack, *, name):
    def body(g_ref, own_ref, recv_ref, send_sems, recv_sems, local_sem):
        x, y, c = _place()
        chips = _other_chips(x, y)
        mine = pltpu.make_async_copy(g_ref.at[2 * x + y], own_ref, local_sem)
        mine.start()
        sends = [pltpu.make_async_remote_copy(
            src_ref=g_ref.at[2 * cx + cy], dst_ref=recv_ref.at[k], send_sem=send_sems.at[k], recv_sem=recv_sems.at[k],
            device_id=(cx, cy, c), device_id_type=MESH) for k, (cx, cy) in enumerate(chips)]
        for cp in sends:
            cp.start()
        for cp in sends:
            cp.wait_recv()
        for cp in sends:
            cp.wait_send()
        mine.wait()

    slot = jax.ShapeDtypeStruct(gpack.shape[1:], gpack.dtype)
    return pl.pallas_call(
        body, name=name,
        out_shape=[slot, jax.ShapeDtypeStruct((3,) + gpack.shape[1:], gpack.dtype)],
        in_specs=[ANY], out_specs=[ANY, ANY],
        scratch_shapes=[pltpu.SemaphoreType.DMA((3,)), pltpu.SemaphoreType.DMA((3,)), pltpu.SemaphoreType.DMA],
    )(gpack)


def sibling_swap(p, *, name):
    def body(p_ref, r_ref, send_sem, recv_sem):
        x, y, c = _place()
        cp = pltpu.make_async_remote_copy(src_ref=p_ref, dst_ref=r_ref, send_sem=send_sem, recv_sem=recv_sem,
                                          device_id=(x, y, 1 - c), device_id_type=MESH)
        cp.start()
        cp.wait()

    return pl.pallas_call(
        body, name=name, out_shape=jax.ShapeDtypeStruct(p.shape, p.dtype),
        in_specs=[ANY], out_specs=ANY,
        scratch_shapes=[pltpu.SemaphoreType.DMA, pltpu.SemaphoreType.DMA],
    )(p)


def sum_devices(v_all, *, name):
    m_per = v_all.shape[0] // N_DEV

    def body(v_ref, o_ref):
        acc = v_ref[pl.ds(0, m_per), :]
        for d in range(1, N_DEV):
            acc = acc + v_ref[pl.ds(d * m_per, m_per), :]
        o_ref[...] = acc

    return pl.pallas_call(
        body, name=name, out_shape=jax.ShapeDtypeStruct((m_per, LANES), F32),
        in_specs=[WHOLE_VMEM], out_specs=WHOLE_VMEM,
    )(v_all)


WEIGHTS = ['ada_w', 'ada_b', 'norm_mix_g', 'norm_ffn_g', 'hy_w_in', 'hy_conv_w', 'hy_conv_b', 'hy_dt_bias', 'hy_a_log',
           'hy_d_skip', 'hy_ssm_norm_g', 'hy_w_out', 'rel_table', 'cv_w_pw1', 'cv_b_pw1', 'cv_w_dw', 'cv_b_dw', 'cv_ln_g',
           'cv_ln_b', 'cv_w_pw2', 'cv_b_pw2', 'ffn_w_gate', 'ffn_w_up', 'ffn_w_down', 'final_norm_g']
BIG = ('ada_w', 'hy_w_in', 'hy_w_out', 'cv_w_pw1', 'cv_w_pw2', 'ffn_w_gate', 'ffn_w_up', 'ffn_w_down')
SMALL_SHARDED = {'hy_conv_w': (1, 4, 3072), 'cv_b_pw1': (1, 2048), 'cv_w_dw': (1, 31, 1024), 'cv_b_dw': (1, 1024),
                 'cv_ln_g': (1, 1024), 'cv_ln_b': (1, 1024), 'cv_b_pw2': (1, 1024)}
SMALL_GRADS = {'ada_b': (2, 6144), 'norm_mix_g': (2, 1024), 'norm_ffn_g': (2, 1024), 'hy_conv_w': (1, 4, 3072),
               'hy_conv_b': (1, 3072), 'hy_dt_bias': (1, 32), 'hy_a_log': (1, 32), 'hy_d_skip': (1, 32),
               'hy_ssm_norm_g': (1, 2048), 'rel_table': (32, 48), 'cv_b_pw1': (1, 2048), 'cv_w_dw': (1, 31, 1024),
               'cv_b_dw': (1, 1024), 'cv_ln_g': (1, 1024), 'cv_ln_b': (1, 1024), 'cv_b_pw2': (1, 1024),
               'final_norm_g': (1024,), 'loss': (1,)}

PACK_LAYOUT = (('hy_in_t', 2568), ('hy_out', 768), ('pw1_t', 512), ('pw2', 256),
               ('gate_t0', 704), ('up_t0', 704), ('down0', 704), ('gate_t1', 704), ('up_t1', 704), ('down1', 704))
PACK_ROWS = 8448


def _pack_offsets():
    off, out = 0, {}
    for nm, r in PACK_LAYOUT:
        out[nm] = (off, r)
        off += r
    return out


PACK_OFF = _pack_offsets()


def _to_lanes(flat):
    n = flat.shape[0]
    m = -(-n // (8 * LANES)) * 8
    return jnp.pad(flat, (0, m * LANES - n)).reshape(m, LANES)


def _split(flat, shapes):
    out, off = {}, 0
    for nm, shp in shapes.items():
        n = int(np.prod(shp))
        out[nm] = flat[off:off + n].reshape(shp)
        off += n
    return out


def kernel(x, c, ada_w, ada_b, norm_mix_g, norm_ffn_g, hy_w_in, hy_conv_w, hy_conv_b, hy_dt_bias, hy_a_log, hy_d_skip, hy_ssm_norm_g, hy_w_out, rel_table, cv_w_pw1, cv_b_pw1, cv_w_dw, cv_b_dw, cv_ln_g, cv_ln_b, cv_w_pw2, cv_b_pw2, ffn_w_gate, ffn_w_up, ffn_w_down, final_norm_g, loss_target, m_ada_w, m_ada_b, m_norm_mix_g, m_norm_ffn_g, m_hy_w_in, m_hy_conv_w, m_hy_conv_b, m_hy_dt_bias, m_hy_a_log, m_hy_d_skip, m_hy_ssm_norm_g, m_hy_w_out, m_rel_table, m_cv_w_pw1, m_cv_b_pw1, m_cv_w_dw, m_cv_b_dw, m_cv_ln_g, m_cv_ln_b, m_cv_w_pw2, m_cv_b_pw2, m_ffn_w_gate, m_ffn_w_up, m_ffn_w_down, m_final_norm_g, v_ada_w, v_ada_b, v_norm_mix_g, v_norm_ffn_g, v_hy_w_in, v_hy_conv_w, v_hy_conv_b, v_hy_dt_bias, v_hy_a_log, v_hy_d_skip, v_hy_ssm_norm_g, v_hy_w_out, v_rel_table, v_cv_w_pw1, v_cv_b_pw1, v_cv_w_dw, v_cv_b_dw, v_cv_ln_g, v_cv_ln_b, v_cv_w_pw2, v_cv_b_pw2, v_ffn_w_gate, v_ffn_w_up, v_ffn_w_down, v_final_norm_g):
    args = (x, c, ada_w, ada_b, norm_mix_g, norm_ffn_g, hy_w_in, hy_conv_w, hy_conv_b, hy_dt_bias, hy_a_log, hy_d_skip, hy_ssm_norm_g, hy_w_out, rel_table, cv_w_pw1, cv_b_pw1, cv_w_dw, cv_b_dw, cv_ln_g, cv_ln_b, cv_w_pw2, cv_b_pw2, ffn_w_gate, ffn_w_up, ffn_w_down, final_norm_g, loss_target, m_ada_w, m_ada_b, m_norm_mix_g, m_norm_ffn_g, m_hy_w_in, m_hy_conv_w, m_hy_conv_b, m_hy_dt_bias, m_hy_a_log, m_hy_d_skip, m_hy_ssm_norm_g, m_hy_w_out, m_rel_table, m_cv_w_pw1, m_cv_b_pw1, m_cv_w_dw, m_cv_b_dw, m_cv_ln_g, m_cv_ln_b, m_cv_w_pw2, m_cv_b_pw2, m_ffn_w_gate, m_ffn_w_up, m_ffn_w_down, m_final_norm_g, v_ada_w, v_ada_b, v_norm_mix_g, v_norm_ffn_g, v_hy_w_in, v_hy_conv_w, v_hy_conv_b, v_hy_dt_bias, v_hy_a_log, v_hy_d_skip, v_hy_ssm_norm_g, v_hy_w_out, v_rel_table, v_cv_w_pw1, v_cv_b_pw1, v_cv_w_dw, v_cv_b_dw, v_cv_ln_g, v_cv_ln_b, v_cv_w_pw2, v_cv_b_pw2, v_ffn_w_gate, v_ffn_w_up, v_ffn_w_down, v_final_norm_g)
    x_in, c_in = args[0], args[1]
    w = dict(zip(WEIGHTS, args[2:27], strict=True))
    tgt = args[27]
    m_in = dict(zip(WEIGHTS, args[28:53], strict=True))
    v_in = dict(zip(WEIGHTS, args[53:78], strict=True))
    xi, yi, ci = _place()
    chip = 2 * xi + yi
    dev = 2 * chip + ci

    cs = rowmap(f_silu, [c_in.reshape(8, LANES)], [], [F32], name="cond_silu", tr=8)[0]
    cs_all = allgather_small(cs, name="gather_cond").reshape(N_DEV, D)
    cs16 = jnp.pad(cs_all, ((0, 8), (0, 0)))
    modpart = jnp.stack([matmul(cs16, w['ada_w'][i], mode="nn", out_dtype=F32, name=f"ada_fwd{i}")[:N_DEV]
                         for i in range(2)], axis=1)
    shard_names = list(SMALL_SHARDED)
    payload = jnp.concatenate([modpart.reshape(-1)] + [w[nm].reshape(-1) for nm in shard_names])
    got = allgather_small(_to_lanes(payload), name="gather_mod").reshape(N_DEV, -1)[0::2]
    modparts = got[:, :modpart.size].reshape(N_CHIPS, N_DEV, 2, 1536)
    mine = lax.dynamic_index_in_dim(modparts, dev, axis=1, keepdims=False)
    mod = jnp.transpose(mine, (1, 0, 2)).reshape(2, 6 * D) + w['ada_b']
    mods = [[mod[i, j * D:(j + 1) * D].reshape(1, D) for j in range(6)] for i in range(2)]
    sp = {}
    off = modpart.size
    for nm in shard_names:
        shp = w[nm].shape
        n = int(np.prod(shp))
        parts = got[:, off:off + n].reshape((N_CHIPS,) + shp)
        sp[nm] = jnp.concatenate([parts[s] for s in range(N_CHIPS)], axis=-1)
        off += n

    def rows_of(nm, i=None):
        a = w[nm][0 if i is None else i]
        return (a.T if nm in ('hy_w_in', 'cv_w_pw1', 'ffn_w_gate', 'ffn_w_up') else a).astype(BF16)

    pieces = [rows_of('hy_w_in'), rows_of('hy_w_out'), rows_of('cv_w_pw1'), rows_of('cv_w_pw2')]
    for i in range(2):
        pieces += [rows_of('ffn_w_gate', i), rows_of('ffn_w_up', i), rows_of('ffn_w_down', i)]
    n_rows = sum(p.shape[0] for p in pieces)
    pack = jnp.concatenate(pieces + [jnp.zeros((PACK_ROWS - n_rows, D), BF16)], axis=0)
    full = allgather_chips(pack, name="gather_weights")

    def whole(nm):
        o, r = PACK_OFF[nm]
        return full[:, o:o + r].reshape(N_CHIPS * r, D)

    wts = {"hy_in_t": hy_to_cat(whole('hy_in_t')), "hy_out": whole('hy_out'), "pw1_t": whole('pw1_t'), "pw2": whole('pw2'),
           "gu_t": [jnp.concatenate([whole(f'gate_t{i}'), whole(f'up_t{i}')], axis=0) for i in range(2)],
           "down": [whole(f'down{i}') for i in range(2)]}

    sp = {"norm_mix_g": [w['norm_mix_g'][i].reshape(1, D) for i in range(2)],
          "norm_ffn_g": [w['norm_ffn_g'][i].reshape(1, D) for i in range(2)],
          "hy_conv_w": sp['hy_conv_w'][0], "hy_conv_b": w['hy_conv_b'],
          "hy_dt_bias": w['hy_dt_bias'].reshape(SSM_HEADS, 1), "hy_a_log": w['hy_a_log'].reshape(SSM_HEADS, 1),
          "hy_d_skip": w['hy_d_skip'].reshape(SSM_HEADS, 1), "hy_ssm_norm_g": w['hy_ssm_norm_g'],
          "rel_table": w['rel_table'], "cv_b_pw1": sp['cv_b_pw1'], "cv_w_dw": sp['cv_w_dw'][0], "cv_b_dw": sp['cv_b_dw'],
          "cv_ln_g": sp['cv_ln_g'], "cv_ln_b": sp['cv_ln_b'], "cv_b_pw2": sp['cv_b_pw2'],
          "final_norm_g": w['final_norm_g'].reshape(1, D)}

    loss_rows, grad_x, g, dmods = device_step(x_in[0], tgt[0], mods, wts, sp)

    dmod = jnp.stack([jnp.concatenate([d.reshape(-1) for d in dmods[i]]) for i in range(2)])
    small = {'ada_b': dmod, 'norm_mix_g': jnp.stack([g[f'norm_mix_g{i}'].reshape(-1) for i in range(2)]),
             'norm_ffn_g': jnp.stack([g[f'norm_ffn_g{i}'].reshape(-1) for i in range(2)]),
             'loss': jnp.sum(loss_rows).reshape(1)}
    for nm in SMALL_GRADS:
        if nm not in small:
            small[nm] = g[nm]
    vec = _to_lanes(jnp.concatenate([small[nm].reshape(-1) for nm in SMALL_GRADS]))
    vec_all = allgather_small(vec, name="gather_small_grads")
    tot = _split(sum_devices(vec_all, name="sum_small_grads").reshape(-1), SMALL_GRADS)
    dmod_all = vec_all.reshape(N_DEV, -1)[:, :2 * 6 * D].reshape(N_DEV, 2, 6 * D)

    gp = [hy_from_cat(g['hy_in_t']), g['hy_out'], g['pw1_t'], g['pw2']]
    for i in range(2):
        gp += [g[f'gu_t{i}'][:FFN_HIDDEN], g[f'gu_t{i}'][FFN_HIDDEN:], g[f'down{i}']]
    gp = [a.reshape(N_CHIPS, a.shape[0] // N_CHIPS, D) for a in gp]
    gpack = jnp.concatenate(gp + [jnp.zeros((N_CHIPS, PACK_ROWS - n_rows, D), F32)], axis=1).astype(BF16)
    half = PACK_ROWS // 2
    theirs = swap_halves(gpack, name="swap_grad_halves")
    ours = lax.dynamic_slice_in_dim(gpack, ci * half, half, axis=1)
    chip_sum = rowmap(f_add, [ours.reshape(N_CHIPS * half, D), theirs.reshape(N_CHIPS * half, D)], [], [BF16],
                      name="sum_core_grads")[0].reshape(N_CHIPS, half, D)
    own, recv = scatter_chips(chip_sum, name="scatter_grads")
    mine_half = rowmap(f_sum4, [own, recv[0], recv[1], recv[2]], [], [F32], name="sum_chip_grads")[0]
    its_half = sibling_swap(mine_half, name="swap_grads")
    red = jnp.concatenate([jnp.where(ci == 0, mine_half, its_half), jnp.where(ci == 0, its_half, mine_half)], axis=0)

    def shard_grad(nm, i=None):
        key = {'hy_w_in': 'hy_in_t', 'hy_w_out': 'hy_out', 'cv_w_pw1': 'pw1_t', 'cv_w_pw2': 'pw2'}.get(nm)
        if key is None:
            key = {'ffn_w_gate': 'gate_t', 'ffn_w_up': 'up_t', 'ffn_w_down': 'down'}[nm] + str(i)
        o, r = PACK_OFF[key]
        a = red[o:o + r]
        return a.T if key.endswith('_t') or key[:-1].endswith('_t') else a

    grads = {}
    grads['hy_w_in'] = shard_grad('hy_w_in')[None]
    grads['hy_w_out'] = shard_grad('hy_w_out')[None]
    grads['cv_w_pw1'] = shard_grad('cv_w_pw1')[None]
    grads['cv_w_pw2'] = shard_grad('cv_w_pw2')[None]
    for nm in ('ffn_w_gate', 'ffn_w_up', 'ffn_w_down'):
        grads[nm] = jnp.stack([shard_grad(nm, i) for i in range(2)])
    cs16 = jnp.pad(cs_all, ((0, 8), (0, 0)))
    dm_mine = lax.dynamic_slice_in_dim(dmod_all, chip * 1536, 1536, axis=2)
    dm16 = jnp.pad(dm_mine, ((0, 8), (0, 0), (0, 0)))
    grads['ada_w'] = jnp.stack([matmul(cs16, dm16[:, i], mode="tn", out_dtype=F32, name=f"ada_dw{i}") for i in range(2)])
    for nm, shp in SMALL_GRADS.items():
        if nm == 'loss':
            continue
        if nm in SMALL_SHARDED:
            n = w[nm].shape[-1]
            grads[nm] = lax.dynamic_slice_in_dim(tot[nm], chip * n, n, axis=len(shp) - 1)
        else:
            grads[nm] = tot[nm].reshape(w[nm].shape)

    delta, new_m, new_v = {}, {}, {}
    for nm in BIG:
        delta[nm], new_m[nm], new_v[nm] = adamw(w[nm], grads[nm], m_in[nm], v_in[nm], name="adamw_" + nm)
    smalls = [nm for nm in WEIGHTS if nm not in BIG]
    packed = [_to_lanes(jnp.concatenate([d[nm].reshape(-1) for nm in smalls])) for d in (w, grads, m_in, v_in)]
    res = rowmap(f_adamw, packed, [], [F32] * 3, name="adamw_small", tr=_rows_tile(packed[0].shape[0]))
    for d, r in zip((delta, new_m, new_v), res, strict=True):
        d.update(_split(r.reshape(-1), {nm: w[nm].shape for nm in smalls}))

    loss = tot['loss'].reshape(())
    return (loss, grad_x[None], *[grads[nm] for nm in WEIGHTS], *[delta[nm] for nm in WEIGHTS],
            *[new_m[nm] for nm in WEIGHTS], *[new_v[nm] for nm in WEIGHTS])
```

```python
import functools
import math

import jax
import jax.numpy as jnp
import numpy as np
from jax import lax
from jax.experimental import pallas as pl
from jax.experimental.pallas import tpu as pltpu

F32 = jnp.float32
BF16 = jnp.bfloat16
MESH = pl.DeviceIdType.MESH

D = 1024
S = 4096
EPS = 1e-6
SSM_INNER = 2048
SSM_HEADS = 32
SSM_HDIM = 64
SSM_GROUPS = 4
SSM_STATE = 128
SSM_CONVK = 4
SSM_CONV_DIM = 3072
CHUNK = 128
N_CHUNKS = S // CHUNK
ATT_HEADS = 16
ATT_HDIM = 64
ATT_PATTERNS = ((128, 1), (512, 4), (2048, 16))
ATT_BLK = 128
REL_BUCKETS = 32
REL_MAX_DIST = 2048
CONV_WIDTH = 31
FFN_HIDDEN = 2816
N_CHIPS = 4
N_DEV = 8
ADAM_LR, ADAM_B1, ADAM_B2, ADAM_EPS, ADAM_WD, ADAM_STEP = 0.001, 0.9, 0.999, 1e-08, 0.01, 10

VMEM_LIMIT_BYTES = 56 * 1024 * 1024
LANES = 128


def _cparams(*sem):
    return pltpu.CompilerParams(dimension_semantics=sem, vmem_limit_bytes=VMEM_LIMIT_BYTES)


def _pick(n, cap, mult=LANES):
    best = None
    for t in range(mult, min(n, cap) + 1, mult):
        if n % t == 0:
            best = t
    return best or n


def _dot(a, b, ca, cb):
    return lax.dot_general(a.astype(BF16), b.astype(BF16), (((ca,), (cb,)), ((), ())), preferred_element_type=F32)


@jax.custom_vjp
def mm(a, b):
    return _dot(a, b, 1, 0)


def _mm_fwd(a, b):
    return _dot(a, b, 1, 0), (a, b)


def _mm_bwd(res, g):
    a, b = res
    return _dot(g, b, 1, 1).astype(a.dtype), _dot(a, g, 0, 0).astype(b.dtype)


mm.defvjp(_mm_fwd, _mm_bwd)


@jax.custom_vjp
def mm_nt(a, b):
    return _dot(a, b, 1, 1)


def _mm_nt_fwd(a, b):
    return _dot(a, b, 1, 1), (a, b)


def _mm_nt_bwd(res, g):
    a, b = res
    return _dot(g, b, 1, 0).astype(a.dtype), _dot(g, a, 0, 0).astype(b.dtype)


mm_nt.defvjp(_mm_nt_fwd, _mm_nt_bwd)


@jax.custom_vjp
def mm_tn(a, b):
    return _dot(a, b, 0, 0)


def _mm_tn_fwd(a, b):
    return _dot(a, b, 0, 0), (a, b)


def _mm_tn_bwd(res, g):
    a, b = res
    return _dot(b, g, 1, 1).astype(a.dtype), _dot(a, g, 1, 0).astype(b.dtype)


mm_tn.defvjp(_mm_tn_fwd, _mm_tn_bwd)


def matmul(a, b, *, mode, out_dtype, name, n=None, b_off=0, tm_cap=1024, tn_cap=512, tk_cap=1536):
    if mode == "tn":
        k_dim, m_dim = a.shape
    else:
        m_dim, k_dim = a.shape
    n_dim = n if n is not None else (b.shape[0] if mode == "nt" else b.shape[1])
    tm = m_dim if m_dim < LANES else _pick(m_dim, tm_cap)
    tn = _pick(n_dim, tn_cap)
    tk = k_dim if k_dim < LANES else _pick(k_dim, tk_cap)
    assert m_dim % tm == 0 and n_dim % tn == 0 and k_dim % tk == 0 and b_off % tn == 0
    nk = k_dim // tk
    off = b_off // tn
    if mode == "nn":
        a_spec = pl.BlockSpec((tm, tk), lambda i, j, k: (i, k))
        b_spec = pl.BlockSpec((tk, tn), lambda i, j, k: (k, j))
        ca, cb = 1, 0
    elif mode == "nt":
        a_spec = pl.BlockSpec((tm, tk), lambda i, j, k: (i, k))
        b_spec = pl.BlockSpec((tn, tk), lambda i, j, k: (j + off, k))
        ca, cb = 1, 1
    else:
        a_spec = pl.BlockSpec((tk, tm), lambda i, j, k: (k, i))
        b_spec = pl.BlockSpec((tk, tn), lambda i, j, k: (k, j))
        ca, cb = 0, 0

    def body(a_ref, b_ref, o_ref, acc_ref):
        part = _dot(a_ref[...], b_ref[...], ca, cb)
        if nk == 1:
            o_ref[...] = part.astype(o_ref.dtype)
        else:
            k = pl.program_id(2)

            @pl.when(k == 0)
            def _():
                acc_ref[...] = part

            @pl.when(k > 0)
            def _():
                acc_ref[...] += part

            @pl.when(k == nk - 1)
            def _():
                o_ref[...] = acc_ref[...].astype(o_ref.dtype)

    return pl.pallas_call(
        body, name=name,
        out_shape=jax.ShapeDtypeStruct((m_dim, n_dim), out_dtype),
        grid=(m_dim // tm, n_dim // tn, nk),
        in_specs=[a_spec, b_spec],
        out_specs=pl.BlockSpec((tm, tn), lambda i, j, k: (i, j)),
        scratch_shapes=[pltpu.VMEM((tm, tn), F32)],
        compiler_params=_cparams("parallel", "parallel", "arbitrary"),
    )(a, b)


def _f32(xs):
    return [x.astype(F32) for x in xs]


def rowmap(f, rows, consts, out_dtypes, *, name, tr=256):
    r_dim = rows[0].shape[0]
    tr = _pick(r_dim, tr, mult=8)
    assert r_dim % tr == 0
    nr, nc = len(rows), len(consts)
    outs = jax.eval_shape(lambda *xs: f(*xs), *[jax.ShapeDtypeStruct((tr, x.shape[1]), F32) for x in rows],
                          *[jax.ShapeDtypeStruct(x.shape, F32) for x in consts])

    def body(*refs):
        res = f(*_f32([r[...] for r in refs[:nr + nc]]))
        for o_ref, o in zip(refs[nr + nc:], res, strict=True):
            o_ref[...] = o.astype(o_ref.dtype)

    return pl.pallas_call(
        body, name=name,
        out_shape=[jax.ShapeDtypeStruct((r_dim, o.shape[1]), dt) for o, dt in zip(outs, out_dtypes, strict=True)],
        grid=(r_dim // tr,),
        in_specs=[pl.BlockSpec((tr, x.shape[1]), lambda i: (i, 0)) for x in rows]
        + [pl.BlockSpec(x.shape, lambda i: (0, 0)) for x in consts],
        out_specs=[pl.BlockSpec((tr, o.shape[1]), lambda i: (i, 0)) for o in outs],
        compiler_params=_cparams("parallel"),
    )(*rows, *consts)


def rowmap_bwd(f, rows, consts, cts, *, name, row_grad, row_dtypes=None, tr=256, emit=(), row_add=None):
    r_dim = rows[0].shape[0]
    tr = _pick(r_dim, tr, mult=8)
    assert r_dim % tr == 0
    nr, nc, nct = len(rows), len(consts), len(cts)
    gi = [i for i, flag in enumerate(row_grad) if flag]
    row_dtypes = row_dtypes or [F32] * len(gi)
    row_add = row_add or [None] * len(gi)
    adds = [a for a in row_add if a is not None]
    outs = jax.eval_shape(lambda *xs: f(*xs), *[jax.ShapeDtypeStruct((tr, x.shape[1]), F32) for x in rows],
                          *[jax.ShapeDtypeStruct(x.shape, F32) for x in consts])

    def body(*refs):
        ins = _f32([r[...] for r in refs[:nr + nc]])
        ct = _f32([r[...] for r in refs[nr + nc:nr + nc + nct]])
        add_refs = list(refs[nr + nc + nct:nr + nc + nct + len(adds)])
        o_refs = refs[nr + nc + nct + len(adds):]
        res, vjp = jax.vjp(f, *ins)
        grads = vjp(tuple(ct))
        for o_ref, i, a in zip(o_refs[:len(gi)], gi, row_add):
            g = grads[i] if a is None else grads[i] + add_refs.pop(0)[...].astype(F32)
            o_ref[...] = g.astype(o_ref.dtype)
        first = pl.program_id(0) == 0
        for o_ref, g in zip(o_refs[len(gi):len(gi) + nc], grads[nr:]):
            @pl.when(first)
            def _(o_ref=o_ref, g=g):
                o_ref[...] = g

            @pl.when(jnp.logical_not(first))
            def _(o_ref=o_ref, g=g):
                o_ref[...] += g
        for o_ref, i in zip(o_refs[len(gi) + nc:], emit):
            o_ref[...] = res[i].astype(o_ref.dtype)

    out_shape = ([jax.ShapeDtypeStruct(rows[i].shape, dt) for i, dt in zip(gi, row_dtypes, strict=True)]
                 + [jax.ShapeDtypeStruct(x.shape, F32) for x in consts]
                 + [jax.ShapeDtypeStruct((r_dim, outs[i].shape[1]), F32) for i in emit])
    out_specs = ([pl.BlockSpec((tr, rows[i].shape[1]), lambda i_: (i_, 0)) for i in gi]
                 + [pl.BlockSpec(x.shape, lambda i_: (0, 0)) for x in consts]
                 + [pl.BlockSpec((tr, outs[i].shape[1]), lambda i_: (i_, 0)) for i in emit])
    res = pl.pallas_call(
        body, name=name,
        out_shape=out_shape,
        grid=(r_dim // tr,),
        in_specs=[pl.BlockSpec((tr, x.shape[1]), lambda i: (i, 0)) for x in rows]
        + [pl.BlockSpec(x.shape, lambda i: (0, 0)) for x in consts]
        + [pl.BlockSpec((tr, x.shape[1]), lambda i: (i, 0)) for x in list(cts) + adds],
        out_specs=out_specs,
        compiler_params=_cparams("arbitrary"),
    )(*rows, *consts, *cts, *adds)
    return res[:len(gi)], res[len(gi):len(gi) + nc], res[len(gi) + nc:]


def transpose(a, *, name, out_dtype=BF16, tr=512, tc=512):
    r_dim, c_dim = a.shape
    tr, tc = _pick(r_dim, tr), _pick(c_dim, tc)

    def body(a_ref, o_ref):
        o_ref[...] = a_ref[...].astype(F32).T.astype(o_ref.dtype)

    return pl.pallas_call(
        body, name=name, out_shape=jax.ShapeDtypeStruct((c_dim, r_dim), out_dtype),
        grid=(r_dim // tr, c_dim // tc),
        in_specs=[pl.BlockSpec((tr, tc), lambda i, j: (i, j))],
        out_specs=pl.BlockSpec((tc, tr), lambda i, j: (j, i)),
        compiler_params=_cparams("parallel", "parallel"),
    )(a)


CONV_HALO = 32
CONV_CHUNK = 256


def conv_fwd(x, w, b, *, name, cb=256):
    s_dim, c_dim = x.shape
    taps = w.shape[0]
    assert taps - 1 <= CONV_HALO and s_dim % CONV_CHUNK == 0 and c_dim % cb == 0
    n_chunks = s_dim // CONV_CHUNK
    ext = CONV_CHUNK + CONV_HALO

    def body(x_ref, w_ref, b_ref, o_ref, xp_ref):
        xp_ref[pl.ds(0, CONV_HALO), :] = jnp.zeros((CONV_HALO, cb), F32)
        xp_ref[pl.ds(CONV_HALO, s_dim), :] = x_ref[...].astype(F32)
        wv = w_ref[...].astype(F32)
        bv = b_ref[...].astype(F32)

        def chunk(t, carry):
            base = pl.multiple_of(t * CONV_CHUNK, CONV_CHUNK)
            xe = xp_ref[pl.ds(base, ext), :]
            acc = jnp.broadcast_to(bv, (CONV_CHUNK, cb))
            for j in range(taps):
                sh = xe if j == 0 else pltpu.roll(xe, shift=j, axis=0)
                acc = acc + wv[taps - 1 - j:taps - j, :] * sh[CONV_HALO:, :]
            o_ref[pl.ds(base, CONV_CHUNK), :] = acc
            return carry

        lax.fori_loop(0, n_chunks, chunk, 0)

    return pl.pallas_call(
        body, name=name,
        out_shape=jax.ShapeDtypeStruct((s_dim, c_dim), F32),
        grid=(c_dim // cb,),
        in_specs=[pl.BlockSpec((s_dim, cb), lambda i: (0, i)), pl.BlockSpec((taps, cb), lambda i: (0, i)),
                  pl.BlockSpec((1, cb), lambda i: (0, i))],
        out_specs=pl.BlockSpec((s_dim, cb), lambda i: (0, i)),
        scratch_shapes=[pltpu.VMEM((s_dim + CONV_HALO, cb), F32)],
        compiler_params=_cparams("parallel"),
    )(x, w, b)


def conv_bwd(x, w, g, *, name, cb=256):
    s_dim, c_dim = x.shape
    taps = w.shape[0]
    n_chunks = s_dim // CONV_CHUNK
    ext = CONV_CHUNK + CONV_HALO
    taps_pad = -(-taps // 8) * 8

    def body(x_ref, w_ref, g_ref, dx_ref, dw_ref, db_ref, xp_ref, gp_ref, acc_ref):
        xp_ref[pl.ds(0, CONV_HALO), :] = jnp.zeros((CONV_HALO, cb), F32)
        xp_ref[pl.ds(CONV_HALO, s_dim), :] = x_ref[...].astype(F32)
        gp_ref[pl.ds(0, s_dim), :] = g_ref[...].astype(F32)
        gp_ref[pl.ds(s_dim, CONV_HALO), :] = jnp.zeros((CONV_HALO, cb), F32)
        acc_ref[...] = jnp.zeros_like(acc_ref)
        wv = w_ref[...].astype(F32)

        def chunk(t, carry):
            base = pl.multiple_of(t * CONV_CHUNK, CONV_CHUNK)
            xe = xp_ref[pl.ds(base, ext), :]
            ge = gp_ref[pl.ds(base, ext), :]
            gc = ge[:CONV_CHUNK, :]
            dx = jnp.zeros((CONV_CHUNK, cb), F32)
            for j in range(taps):
                xs = xe if j == 0 else pltpu.roll(xe, shift=j, axis=0)
                gs = ge if j == 0 else pltpu.roll(ge, shift=ext - j, axis=0)
                k = taps - 1 - j
                dx = dx + wv[k:k + 1, :] * gs[:CONV_CHUNK, :]
                acc_ref[k:k + 1, :] += jnp.sum(gc * xs[CONV_HALO:, :], axis=0, keepdims=True)
            acc_ref[taps_pad:taps_pad + 1, :] += jnp.sum(gc, axis=0, keepdims=True)
            dx_ref[pl.ds(base, CONV_CHUNK), :] = dx
            return carry

        lax.fori_loop(0, n_chunks, chunk, 0)
        dw_ref[...] = acc_ref[0:taps, :]
        db_ref[...] = acc_ref[taps_pad:taps_pad + 1, :]

    return pl.pallas_call(
        body, name=name,
        out_shape=[jax.ShapeDtypeStruct((s_dim, c_dim), F32), jax.ShapeDtypeStruct((taps, c_dim), F32),
                   jax.ShapeDtypeStruct((1, c_dim), F32)],
        grid=(c_dim // cb,),
        in_specs=[pl.BlockSpec((s_dim, cb), lambda i: (0, i)), pl.BlockSpec((taps, cb), lambda i: (0, i)),
                  pl.BlockSpec((s_dim, cb), lambda i: (0, i))],
        out_specs=[pl.BlockSpec((s_dim, cb), lambda i: (0, i)), pl.BlockSpec((taps, cb), lambda i: (0, i)),
                   pl.BlockSpec((1, cb), lambda i: (0, i))],
        scratch_shapes=[pltpu.VMEM((s_dim + CONV_HALO, cb), F32), pltpu.VMEM((s_dim + CONV_HALO, cb), F32),
                        pltpu.VMEM((taps_pad + 8, cb), F32)],
        compiler_params=_cparams("parallel"),
    )(x, w, g)


def _iota2(n, axis):
    return lax.broadcasted_iota(jnp.int32, (n, n), axis)


def _to_col(row):
    n = row.shape[1]
    return jnp.sum(jnp.where(_iota2(n, 0) == _iota2(n, 1), jnp.broadcast_to(row, (n, n)), 0.0), axis=1, keepdims=True)


def _softplus(x):
    return jnp.maximum(x, 0.0) + jnp.log(1.0 + jnp.exp(-jnp.abs(x)))


def ssd_heads(x, dtraw, dt_bias, a_log, dskip, bm, cm, prev):
    h, q, _ = x.shape
    n = bm.shape[1]
    li = lax.broadcasted_iota(jnp.int32, (1, q, q), 1)
    si = lax.broadcasted_iota(jnp.int32, (1, q, q), 2)

    def to_col(row):
        return jnp.sum(jnp.where(li == si, jnp.broadcast_to(row, (h, q, q)), 0.0), axis=2, keepdims=True)

    dt_row = _softplus(dtraw + dt_bias)
    a_row = dt_row * (-jnp.exp(a_log))
    a_col = to_col(a_row)
    acs_col = jnp.sum(jnp.where(si <= li, jnp.broadcast_to(a_row, (h, q, q)), 0.0), axis=2, keepdims=True)
    acs_row = jnp.sum(jnp.where(li <= si, jnp.broadcast_to(a_col, (h, q, q)), 0.0), axis=1, keepdims=True)
    total = jnp.sum(a_row, axis=2, keepdims=True)
    xdt = x * to_col(dt_row)
    lmat = jnp.exp(jnp.where(li >= si, acs_col - acs_row, -1e30))
    bmb = jnp.broadcast_to(bm[None], (h, q, n))
    cmb = jnp.broadcast_to(cm[None], (h, q, n))
    y = bmm(mm_nt(cm, bm)[None] * lmat, xdt)
    y = y + bmm_nt(cmb, prev) * jnp.exp(acs_col)
    y = y + dskip * x
    state = bmm_tn(xdt * jnp.exp(total - acs_col), bmb)
    return y, jnp.exp(total) * prev + state


HEADS_PER_GROUP = SSM_HEADS // SSM_GROUPS
BM_COL0 = SSM_INNER // SSM_STATE
CM_COL0 = BM_COL0 + SSM_GROUPS


def ssd_fwd(xs_hm, dtraw_t, dt_bias, a_log, dskip, xbc):
    hg = HEADS_PER_GROUP

    def body(x_ref, dt_ref, dtb_ref, al_ref, dk_ref, bm_ref, cm_ref, y_ref, prev_ref, state_ref):
        @pl.when(pl.program_id(1) == 0)
        def _():
            state_ref[...] = jnp.zeros_like(state_ref)

        prev = state_ref[...]
        prev_ref[0] = prev
        y, nxt = ssd_heads(x_ref[...], dt_ref[...], dtb_ref[...], al_ref[...], dk_ref[...], bm_ref[...], cm_ref[...], prev)
        y_ref[...] = y
        state_ref[...] = nxt

    hp = pl.BlockSpec((hg, 1, 1), lambda g, c: (g, 0, 0))
    dtraw_t, dt_bias, a_log, dskip = [a.reshape(SSM_HEADS, 1, -1) for a in (dtraw_t, dt_bias, a_log, dskip)]
    return pl.pallas_call(
        body, name="ssd_fwd",
        out_shape=[jax.ShapeDtypeStruct((SSM_HEADS, S, SSM_HDIM), F32),
                   jax.ShapeDtypeStruct((N_CHUNKS, SSM_HEADS, SSM_HDIM, SSM_STATE), F32)],
        grid=(SSM_GROUPS, N_CHUNKS),
        in_specs=[pl.BlockSpec((hg, CHUNK, SSM_HDIM), lambda g, c: (g, c, 0)),
                  pl.BlockSpec((hg, 1, CHUNK), lambda g, c: (g, 0, c)), hp, hp, hp,
                  pl.BlockSpec((CHUNK, SSM_STATE), lambda g, c: (c, BM_COL0 + g)),
                  pl.BlockSpec((CHUNK, SSM_STATE), lambda g, c: (c, CM_COL0 + g))],
        out_specs=[pl.BlockSpec((hg, CHUNK, SSM_HDIM), lambda g, c: (g, c, 0)),
                   pl.BlockSpec((1, hg, SSM_HDIM, SSM_STATE), lambda g, c: (c, g, 0, 0))],
        scratch_shapes=[pltpu.VMEM((hg, SSM_HDIM, SSM_STATE), F32)],
        compiler_params=_cparams("parallel", "arbitrary"),
    )(xs_hm, dtraw_t, dt_bias, a_log, dskip, xbc, xbc)


def ssd_bwd(xs_hm, dtraw_t, dt_bias, a_log, dskip, xbc, prev_all, dy_hm):
    hg = HEADS_PER_GROUP
    last = N_CHUNKS - 1

    def body(x_ref, dt_ref, dtb_ref, al_ref, dk_ref, bm_ref, cm_ref, prev_ref, dy_ref,
             dx_ref, ddt_ref, ddtb_ref, dal_ref, ddk_ref, dbm_ref, dcm_ref, dstate_ref):
        @pl.when(pl.program_id(1) == 0)
        def _():
            dstate_ref[...] = jnp.zeros_like(dstate_ref)
            ddtb_ref[...] = jnp.zeros_like(ddtb_ref)
            dal_ref[...] = jnp.zeros_like(dal_ref)
            ddk_ref[...] = jnp.zeros_like(ddk_ref)

        _, vjp = jax.vjp(ssd_heads, x_ref[...], dt_ref[...], dtb_ref[...], al_ref[...], dk_ref[...], bm_ref[...],
                         cm_ref[...], prev_ref[0])
        dx, ddt, ddtb, dal, ddk, dbm, dcm, dprev = vjp((dy_ref[...], dstate_ref[...]))
        dx_ref[...] = dx
        ddt_ref[...] = ddt
        ddtb_ref[...] += ddtb
        dal_ref[...] += dal
        ddk_ref[...] += ddk
        dbm_ref[...] = dbm
        dcm_ref[...] = dcm
        dstate_ref[...] = dprev

    hp = pl.BlockSpec((hg, 1, 1), lambda g, c: (g, 0, 0))
    xspec = pl.BlockSpec((hg, CHUNK, SSM_HDIM), lambda g, c: (g, last - c, 0))
    tspec = pl.BlockSpec((hg, 1, CHUNK), lambda g, c: (g, 0, last - c))
    gspec = pl.BlockSpec((CHUNK, SSM_STATE), lambda g, c: (last - c, g))
    dtraw_t, dt_bias, a_log, dskip = [a.reshape(SSM_HEADS, 1, -1) for a in (dtraw_t, dt_bias, a_log, dskip)]
    res = pl.pallas_call(
        body, name="ssd_bwd",
        out_shape=[jax.ShapeDtypeStruct((SSM_HEADS, S, SSM_HDIM), F32), jax.ShapeDtypeStruct((SSM_HEADS, 1, S), F32),
                   jax.ShapeDtypeStruct((SSM_HEADS, 1, 1), F32), jax.ShapeDtypeStruct((SSM_HEADS, 1, 1), F32),
                   jax.ShapeDtypeStruct((SSM_HEADS, 1, 1), F32),
                   jax.ShapeDtypeStruct((S, SSM_GROUPS * SSM_STATE), F32),
                   jax.ShapeDtypeStruct((S, SSM_GROUPS * SSM_STATE), F32)],
        grid=(SSM_GROUPS, N_CHUNKS),
        in_specs=[xspec, tspec, hp, hp, hp,
                  pl.BlockSpec((CHUNK, SSM_STATE), lambda g, c: (last - c, BM_COL0 + g)),
                  pl.BlockSpec((CHUNK, SSM_STATE), lambda g, c: (last - c, CM_COL0 + g)),
                  pl.BlockSpec((1, hg, SSM_HDIM, SSM_STATE), lambda g, c: (last - c, g, 0, 0)), xspec],
        out_specs=[xspec, tspec, hp, hp, hp, gspec, gspec],
        scratch_shapes=[pltpu.VMEM((hg, SSM_HDIM, SSM_STATE), F32)],
        compiler_params=_cparams("parallel", "arbitrary"),
    )(xs_hm, dtraw_t, dt_bias, a_log, dskip, xbc, xbc, prev_all, dy_hm)
    return [res[0]] + [r.reshape(SSM_HEADS, -1) for r in res[1:5]] + list(res[5:])


ATT_HB = 8


def _bdot(a, b, ca, cb):
    return lax.dot_general(a.astype(BF16), b.astype(BF16), (((ca,), (cb,)), ((0,), (0,))), preferred_element_type=F32)


@jax.custom_vjp
def bmm(a, b):
    return _bdot(a, b, 2, 1)


def _bmm_fwd(a, b):
    return _bdot(a, b, 2, 1), (a, b)


def _bmm_bwd(res, g):
    a, b = res
    return _bdot(g, b, 2, 2).astype(a.dtype), _bdot(a, g, 1, 1).astype(b.dtype)


bmm.defvjp(_bmm_fwd, _bmm_bwd)


@jax.custom_vjp
def bmm_nt(a, b):
    return _bdot(a, b, 2, 2)


def _bmm_nt_fwd(a, b):
    return _bdot(a, b, 2, 2), (a, b)


def _bmm_nt_bwd(res, g):
    a, b = res
    return _bdot(g, b, 2, 1).astype(a.dtype), _bdot(g, a, 1, 1).astype(b.dtype)


bmm_nt.defvjp(_bmm_nt_fwd, _bmm_nt_bwd)


@jax.custom_vjp
def bmm_tn(a, b):
    return _bdot(a, b, 1, 1)


def _bmm_tn_fwd(a, b):
    return _bdot(a, b, 1, 1), (a, b)


def _bmm_tn_bwd(res, g):
    a, b = res
    return _bdot(b, g, 2, 2).astype(a.dtype), _bdot(a, g, 2, 1).astype(b.dtype)


bmm_tn.defvjp(_bmm_tn_fwd, _bmm_tn_bwd)


def att_heads(q, kp, kc, vp, vc, bias_p, bias_c, has_prev):
    h, b, dh = q.shape
    i = lax.broadcasted_iota(jnp.int32, (1, b, b), 1)
    j = lax.broadcasted_iota(jnp.int32, (1, b, b), 2)
    scale = dh ** -0.5
    sp = jnp.where(jnp.logical_and(j >= i, has_prev), bmm_nt(q, kp) * scale + bias_p, -1e30)
    sc = jnp.where(j <= i, bmm_nt(q, kc) * scale + bias_c, -1e30)
    m = lax.stop_gradient(jnp.maximum(jnp.max(sp, axis=2, keepdims=True), jnp.max(sc, axis=2, keepdims=True)))
    pp, pc = jnp.exp(sp - m), jnp.exp(sc - m)
    l = jnp.sum(pp, axis=2, keepdims=True) + jnp.sum(pc, axis=2, keepdims=True)
    o = bmm(pp / l, vp) + bmm(pc / l, vc)
    return o, jnp.broadcast_to(m + jnp.log(l), (h, b, dh))


def _att_specs(nb):
    hb, blk = ATT_HB, ATT_BLK
    cur = pl.BlockSpec((hb, blk, ATT_HDIM), lambda h, b: (h, b, 0))
    prv = pl.BlockSpec((hb, blk, ATT_HDIM), lambda h, b: (h, jnp.maximum(b - 1, 0), 0))
    bias = pl.BlockSpec((hb, 2, blk, blk), lambda h, b: (h, 0, 0, 0))
    return cur, prv, bias


def att_fwd(q, k, v, bias, nb, *, name):
    cur, prv, bspec = _att_specs(nb)

    def body(q_ref, kp_ref, kc_ref, vp_ref, vc_ref, b_ref, o_ref, l_ref):
        has_prev = (pl.program_id(1) % nb) != 0
        o, lse = att_heads(q_ref[...], kp_ref[...], kc_ref[...], vp_ref[...], vc_ref[...], b_ref[:, 0], b_ref[:, 1],
                           has_prev)
        o_ref[...] = o
        l_ref[...] = lse

    shp = jax.ShapeDtypeStruct((ATT_HEADS, S, ATT_HDIM), F32)
    return pl.pallas_call(
        body, name=name, out_shape=[shp, shp],
        grid=(ATT_HEADS // ATT_HB, S // ATT_BLK),
        in_specs=[cur, prv, cur, prv, cur, bspec],
        out_specs=[cur, cur],
        compiler_params=_cparams("parallel", "parallel"),
    )(q, k, k, v, v, bias)


def att_bwd(q, k, v, bias, do, dlse, nb, *, name):
    cur, prv, bspec = _att_specs(nb)

    def body(q_ref, kp_ref, kc_ref, vp_ref, vc_ref, b_ref, do_ref, dl_ref,
             dq_ref, dkc_ref, dkp_ref, dvc_ref, dvp_ref, db_ref):
        has_prev = (pl.program_id(1) % nb) != 0

        @pl.when(pl.program_id(1) == 0)
        def _():
            db_ref[...] = jnp.zeros_like(db_ref)

        ins = _f32([q_ref[...], kp_ref[...], kc_ref[...], vp_ref[...], vc_ref[...]]) + [b_ref[:, 0], b_ref[:, 1]]
        _, vjp = jax.vjp(functools.partial(att_heads, has_prev=has_prev), *ins)
        dq, dkp, dkc, dvp, dvc, dbp, dbc = vjp((do_ref[...], dl_ref[...]))
        dq_ref[...] = dq
        dkc_ref[...] = dkc
        dkp_ref[...] = dkp
        dvc_ref[...] = dvc
        dvp_ref[...] = dvp
        db_ref[:, 0] += dbp
        db_ref[:, 1] += dbc

    shp = jax.ShapeDtypeStruct((ATT_HEADS, S, ATT_HDIM), F32)
    return pl.pallas_call(
        body, name=name,
        out_shape=[shp] * 5 + [jax.ShapeDtypeStruct((ATT_HEADS, 2, ATT_BLK, ATT_BLK), F32)],
        grid=(ATT_HEADS // ATT_HB, S // ATT_BLK),
        in_specs=[cur, prv, cur, prv, cur, bspec, cur, cur],
        out_specs=[cur] * 5 + [bspec],
        compiler_params=_cparams("parallel", "arbitrary"),
    )(q, k, k, v, v, bias, do, dlse)


def shift_add(cur, prev, nb, *, name):
    n_blocks = S // ATT_BLK

    def body(c_ref, p_ref, o_ref):
        nxt = pl.program_id(0) + 1
        keep = jnp.where((nxt % nb) != 0, 1.0, 0.0)
        o_ref[...] = c_ref[...] + keep * p_ref[...]

    return pl.pallas_call(
        body, name=name, out_shape=jax.ShapeDtypeStruct(cur.shape, F32),
        grid=(n_blocks,),
        in_specs=[pl.BlockSpec((ATT_HEADS, ATT_BLK, ATT_HDIM), lambda b: (0, b, 0)),
                  pl.BlockSpec((ATT_HEADS, ATT_BLK, ATT_HDIM), lambda b: (0, jnp.minimum(b + 1, n_blocks - 1), 0))],
        out_specs=pl.BlockSpec((ATT_HEADS, ATT_BLK, ATT_HDIM), lambda b: (0, b, 0)),
        compiler_params=_cparams("parallel"),
    )(cur, prev)


def _silu(x):
    return x * jax.nn.sigmoid(x)


def _rms(x):
    return x * lax.rsqrt(jnp.mean(x * x, -1, keepdims=True) + EPS)


def f_normmod(x, g, sc, sh):
    return (_rms(x) * g * (1.0 + sc) + sh,)


def f_resid(x, mix, gate):
    return (x + gate * mix,)


def f_resid_bias(x, mix, gate, b):
    return (x + gate * (mix + b),)


def f_swiglu(hgu):
    return (_silu(hgu[:, :FFN_HIDDEN]) * hgu[:, FFN_HIDDEN:],)


def f_silu(x):
    return (_silu(x),)


def f_gated_norm(y, z, g):
    return (_rms(y * _silu(z)) * g,)


def f_glu(y, b):
    y = y + b
    return (y[:, :D] * jax.nn.sigmoid(y[:, D:]),)


def f_ln_silu(u, g, b):
    mu = jnp.mean(u, -1, keepdims=True)
    var = jnp.mean(jnp.square(u - mu), -1, keepdims=True)
    return (_silu((u - mu) * lax.rsqrt(var + EPS) * g + b),)


def f_combine(o1, o2, o3, l1, l2, l3):
    m = lax.stop_gradient(jnp.maximum(jnp.maximum(l1, l2), l3))
    e1, e2, e3 = jnp.exp(l1 - m), jnp.exp(l2 - m), jnp.exp(l3 - m)
    return ((e1 * o1 + e2 * o2 + e3 * o3) / (e1 + e2 + e3),)


def f_head(x, tgt, g):
    return (0.5 * jnp.mean(jnp.square(_rms(x) * g - tgt), -1, keepdims=True),)


def f_sum3(a, b, c):
    return (a + b + c,)


def f_sum4(a, b, c, d):
    return (a + b + c + d,)


def f_add(a, b):
    return (a + b,)


def f_adamw(w, g, m, v):
    m = ADAM_B1 * m + (1.0 - ADAM_B1) * g
    v = ADAM_B2 * v + (1.0 - ADAM_B2) * jnp.square(g)
    m_hat = m / (1.0 - ADAM_B1 ** ADAM_STEP)
    v_hat = v / (1.0 - ADAM_B2 ** ADAM_STEP)
    return -ADAM_LR * (m_hat / (jnp.sqrt(v_hat) + ADAM_EPS) + ADAM_WD * w), m, v


def _rows_tile(r, cap=256):
    return _pick(r, cap, mult=8)


def adamw(w, g, m, v, *, name):
    shape = w.shape
    c_dim = shape[-1] if len(shape) > 1 else shape[0]
    flat = [a.reshape(-1, c_dim) for a in (w, g, m, v)]
    res = rowmap(f_adamw, flat, [], [F32] * 3, name=name, tr=_rows_tile(flat[0].shape[0], cap=128))
    return [r.reshape(shape) for r in res]


def _t5_bucket(dist):
    max_exact = REL_BUCKETS // 2
    n = jnp.maximum(dist, 1).astype(F32)
    large = max_exact + jnp.log(n / max_exact) / math.log(REL_MAX_DIST / max_exact) * (REL_BUCKETS - max_exact)
    large = jnp.minimum(large.astype(jnp.int32), REL_BUCKETS - 1)
    return jnp.where(dist < max_exact, dist, large)


def _att_buckets(dil):
    i = jnp.arange(ATT_BLK)[:, None]
    j = jnp.arange(2 * ATT_BLK)[None, :]
    bkt = _t5_bucket(jnp.maximum(ATT_BLK + i - j, 0) * dil)
    return jnp.transpose(bkt.reshape(ATT_BLK, 2, ATT_BLK), (1, 0, 2))


def att_bias(rel_table, p, dil):
    tab = rel_table[:, p * ATT_HEADS:(p + 1) * ATT_HEADS]
    onehot = (jnp.arange(REL_BUCKETS)[:, None] == _att_buckets(dil).reshape(1, -1)).astype(F32)
    bias = lax.dot_general(tab, onehot, (((0,), (0,)), ((), ())), precision=lax.Precision.HIGHEST)
    return bias.reshape(ATT_HEADS, 2, ATT_BLK, ATT_BLK)


def att_bias_grad(dbias, dil, *, name):
    onehot = (_att_buckets(dil).reshape(-1, 1) == jnp.arange(LANES)[None, :]).astype(BF16)
    dtab = matmul(dbias.reshape(ATT_HEADS, -1), onehot, mode="nn", out_dtype=F32, name=name, tk_cap=2048)
    return dtab[:, :REL_BUCKETS].T


def to_heads(a, n_heads, dil=1):
    hd = a.shape[1] // n_heads
    return jnp.transpose(a.reshape(S // dil, dil, n_heads, hd), (2, 1, 0, 3)).reshape(n_heads, S, hd)


def from_heads(a, dil=1):
    n_heads, _, hd = a.shape
    return jnp.transpose(a.reshape(n_heads, dil, S // dil, hd), (2, 1, 0, 3)).reshape(S, n_heads * hd)


def regroup_heads(a, dil, inverse=False):
    n_heads, _, hd = a.shape
    if dil == 1:
        return a
    if inverse:
        return jnp.transpose(a.reshape(n_heads, dil, S // dil, hd), (0, 2, 1, 3)).reshape(n_heads, S, hd)
    return jnp.transpose(a.reshape(n_heads, S // dil, dil, hd), (0, 2, 1, 3)).reshape(n_heads, S, hd)


HY_Z, HY_XBC, HY_DT, HY_Q, HY_K, HY_V = 2048, 3072, 32, 3072, 1024, 1024
HY_IN = HY_Z + HY_XBC + HY_DT + HY_Q + HY_K + HY_V
OFF_Z, OFF_XBC, OFF_Q, OFF_KV, OFF_DT = 0, 2048, 5120, 8192, 10240
HY_CAT = OFF_DT + LANES
DT_PAD = LANES


def hy_to_cat(w):
    z, xbc, dt, qkv = w[:2048], w[2048:5120], w[5120:5152], w[5152:]
    return jnp.concatenate([z, xbc, qkv, dt, jnp.zeros((DT_PAD - HY_DT,) + w.shape[1:], w.dtype)], axis=0)


def hy_from_cat(w, axis=0):
    part = lambda a, b: lax.slice_in_dim(w, a, b, axis=axis)
    return jnp.concatenate([part(0, 5120), part(OFF_DT, OFF_DT + HY_DT), part(5120, OFF_DT)], axis=axis)


def device_step(x, tgt, mods, wts, sp):
    g = {}
    dmods = [[None] * 6 for _ in range(2)]

    def normmod(xi, gain, sc, sh, nm):
        return rowmap(f_normmod, [xi], [gain, sc, sh], [BF16], name=nm)[0]

    def ffn_fwd(xi, i, gate, nm):
        h = normmod(xi, sp["norm_ffn_g"][i], mods[i][4], mods[i][3], nm + "_norm")
        hgu = matmul(h, wts["gu_t"][i], mode="nt", out_dtype=F32, name=nm + "_gu")
        act = rowmap(f_swiglu, [hgu], [], [BF16], name=nm + "_act", tr=128)[0]
        out = matmul(act, wts["down"][i], mode="nn", out_dtype=F32, name=nm + "_down")
        xo = rowmap(f_resid, [xi, out], [gate], [F32], name=nm + "_res")[0]
        return xo, (h, hgu, act, out)

    def ffn_bwd(dres, xi, i, saved, nm):
        h, hgu, act, out = saved
        (dout,), (dgate,), _ = rowmap_bwd(f_resid, [xi, out], [mods[i][5]], [dres], name=nm + "_res_b",
                                          row_grad=[False, True], row_dtypes=[BF16])
        dmods[i][5] = dgate
        dact = matmul(dout, wts["down"][i], mode="nt", out_dtype=F32, name=nm + "_down_dx")
        g[f"down{i}"] = matmul(transpose(dout, name=nm + "_dout_t"), act, mode="nn", out_dtype=F32, name=nm + "_down_dw")
        (dhgu,), _, _ = rowmap_bwd(f_swiglu, [hgu], [], [dact], name=nm + "_act_b", row_grad=[True],
                                   row_dtypes=[BF16], tr=128)
        g[f"gu_t{i}"] = matmul(transpose(h, name=nm + "_h_t"), dhgu, mode="nn", out_dtype=F32, name=nm + "_gu_dw")
        dh = matmul(dhgu, wts["gu_t"][i], mode="nn", out_dtype=F32, name=nm + "_gu_dx")
        (dres,), (dg_, dsc, dsh), _ = rowmap_bwd(f_normmod, [xi], [sp["norm_ffn_g"][i], mods[i][4], mods[i][3]], [dh],
                                                 name=nm + "_norm_b", row_grad=[True], row_add=[dres])
        g[f"norm_ffn_g{i}"] = dg_
        dmods[i][4], dmods[i][3] = dsc, dsh
        return dres

    h0 = normmod(x, sp["norm_mix_g"][0], mods[0][1], mods[0][0], "l0_norm")
    w_in = wts["hy_in_t"]
    z = matmul(h0, w_in, mode="nt", out_dtype=F32, name="hy_z", n=HY_Z, b_off=OFF_Z)
    xbc_raw = matmul(h0, w_in, mode="nt", out_dtype=F32, name="hy_xbc", n=HY_XBC, b_off=OFF_XBC)
    q = matmul(h0, w_in, mode="nt", out_dtype=BF16, name="hy_q", n=HY_Q, b_off=OFF_Q)
    kv = matmul(h0, w_in, mode="nt", out_dtype=BF16, name="hy_kv", n=HY_K + HY_V, b_off=OFF_KV)
    dtr = matmul(h0, w_in, mode="nt", out_dtype=F32, name="hy_dt", n=DT_PAD, b_off=OFF_DT)
    xbc_pre = conv_fwd(xbc_raw, sp["hy_conv_w"], sp["hy_conv_b"], name="hy_conv")
    xbc = rowmap(f_silu, [xbc_pre], [], [F32], name="hy_conv_act", tr=128)[0]
    xs_hm = to_heads(xbc[:, :SSM_INNER], SSM_HEADS)
    dtraw_t = dtr[:, :HY_DT].T
    y_hm, prev_all = ssd_fwd(xs_hm, dtraw_t, sp["hy_dt_bias"], sp["hy_a_log"], sp["hy_d_skip"], xbc)
    y = from_heads(y_hm)
    ysn = rowmap(f_gated_norm, [y, z], [sp["hy_ssm_norm_g"]], [BF16], name="hy_gnorm", tr=128)[0]
    k_all, v_all = kv[:, :HY_K], kv[:, HY_K:]
    att_in, att_o, att_l = [], [], []
    for p, (win, dil) in enumerate(ATT_PATTERNS):
        qp = to_heads(q[:, p * D:(p + 1) * D], ATT_HEADS, dil)
        kp, vp = to_heads(k_all, ATT_HEADS, dil), to_heads(v_all, ATT_HEADS, dil)
        bias = att_bias(sp["rel_table"], p, dil)
        nb = S // dil // ATT_BLK
        o, lse = att_fwd(qp, kp, vp, bias, nb, name=f"att_fwd{p}")
        att_in.append((qp, kp, vp, bias, nb))
        att_o.append(regroup_heads(o, dil, inverse=True).reshape(ATT_HEADS * S, ATT_HDIM))
        att_l.append(regroup_heads(lse, dil, inverse=True).reshape(ATT_HEADS * S, ATT_HDIM))
    att_hm = rowmap(f_combine, att_o + att_l, [], [BF16], name="att_combine", tr=2048)[0]
    att = from_heads(att_hm.reshape(ATT_HEADS, S, ATT_HDIM))
    cat = jnp.concatenate([ysn, att], axis=-1)
    mix0 = matmul(cat, wts["hy_out"], mode="nn", out_dtype=F32, name="hy_out")
    x1 = rowmap(f_resid, [x, mix0], [mods[0][2]], [F32], name="l0_res")[0]
    x2, ffn0 = ffn_fwd(x1, 0, mods[0][5], "ffn0")

    h1 = normmod(x2, sp["norm_mix_g"][1], mods[1][1], mods[1][0], "l1_norm")
    p1 = matmul(h1, wts["pw1_t"], mode="nt", out_dtype=F32, name="cv_pw1")
    u = rowmap(f_glu, [p1], [sp["cv_b_pw1"]], [F32], name="cv_glu")[0]
    uc = conv_fwd(u, sp["cv_w_dw"], sp["cv_b_dw"], name="cv_conv")
    ul = rowmap(f_ln_silu, [uc], [sp["cv_ln_g"], sp["cv_ln_b"]], [BF16], name="cv_ln")[0]
    mix1 = matmul(ul, wts["pw2"], mode="nn", out_dtype=F32, name="cv_pw2")
    x3 = rowmap(f_resid_bias, [x2, mix1], [mods[1][2], sp["cv_b_pw2"]], [F32], name="l1_res")[0]
    x4, ffn1 = ffn_fwd(x3, 1, mods[1][5], "ffn1")

    ones = jnp.ones((S, 1), F32)
    (dres,), (dfinal,), (loss_rows,) = rowmap_bwd(f_head, [x4, tgt], [sp["final_norm_g"]], [ones], name="head",
                                                  row_grad=[True, False], emit=(0,))
    g["final_norm_g"] = dfinal

    dres = ffn_bwd(dres, x3, 1, ffn1, "ffn1")
    (dmix1,), (dg1, db2), _ = rowmap_bwd(f_resid_bias, [x2, mix1], [mods[1][2], sp["cv_b_pw2"]], [dres], name="l1_res_b",
                                         row_grad=[False, True], row_dtypes=[BF16])
    dmods[1][2] = dg1
    g["cv_b_pw2"] = db2
    dul = matmul(dmix1, wts["pw2"], mode="nt", out_dtype=F32, name="cv_pw2_dx")
    g["pw2"] = matmul(transpose(dmix1, name="cv_dmix_t"), ul, mode="nn", out_dtype=F32, name="cv_pw2_dw")
    (duc,), (g["cv_ln_g"], g["cv_ln_b"]), _ = rowmap_bwd(f_ln_silu, [uc], [sp["cv_ln_g"], sp["cv_ln_b"]], [dul],
                                                         name="cv_ln_b", row_grad=[True])
    du, g["cv_w_dw"], g["cv_b_dw"] = conv_bwd(u, sp["cv_w_dw"], duc, name="cv_conv_b")
    (dp1,), (g["cv_b_pw1"],), _ = rowmap_bwd(f_glu, [p1], [sp["cv_b_pw1"]], [du], name="cv_glu_b", row_grad=[True],
                                             row_dtypes=[BF16])
    g["pw1_t"] = matmul(transpose(h1, name="cv_h_t"), dp1, mode="nn", out_dtype=F32, name="cv_pw1_dw")
    dh1 = matmul(dp1, wts["pw1_t"], mode="nn", out_dtype=F32, name="cv_pw1_dx")
    (dres,), (dg_, dsc, dsh), _ = rowmap_bwd(f_normmod, [x2], [sp["norm_mix_g"][1], mods[1][1], mods[1][0]], [dh1],
                                             name="l1_norm_b", row_grad=[True], row_add=[dres])
    g["norm_mix_g1"] = dg_
    dmods[1][1], dmods[1][0] = dsc, dsh

    dres = ffn_bwd(dres, x1, 0, ffn0, "ffn0")
    (dmix0,), (dg1,), _ = rowmap_bwd(f_resid, [x, mix0], [mods[0][2]], [dres], name="l0_res_b",
                                     row_grad=[False, True], row_dtypes=[BF16])
    dmods[0][2] = dg1
    dysn = matmul(dmix0, wts["hy_out"], mode="nt", out_dtype=F32, name="hy_out_dy", n=SSM_INNER, b_off=0)
    datt = matmul(dmix0, wts["hy_out"], mode="nt", out_dtype=F32, name="hy_out_da", n=D, b_off=SSM_INNER)
    g["hy_out"] = matmul(transpose(dmix0, name="hy_dmix_t"), cat, mode="nn", out_dtype=F32, name="hy_out_dw")
    (dy, dz), (g["hy_ssm_norm_g"],), _ = rowmap_bwd(f_gated_norm, [y, z], [sp["hy_ssm_norm_g"]], [dysn], name="hy_gnorm_b",
                                                    row_grad=[True, True], tr=128)
    dxs_hm, ddtraw_t, g["hy_dt_bias"], g["hy_a_log"], g["hy_d_skip"], dbm, dcm = ssd_bwd(
        xs_hm, dtraw_t, sp["hy_dt_bias"], sp["hy_a_log"], sp["hy_d_skip"], xbc, prev_all, to_heads(dy, SSM_HEADS))
    dxbc = jnp.concatenate([from_heads(dxs_hm), dbm, dcm], axis=-1)
    (dxbc_pre,), _, _ = rowmap_bwd(f_silu, [xbc_pre], [], [dxbc], name="hy_conv_act_b", row_grad=[True], tr=128)
    dxbc_raw, g["hy_conv_w"], g["hy_conv_b"] = conv_bwd(xbc_raw, sp["hy_conv_w"], dxbc_pre, name="hy_conv_b")
    datt_hm = to_heads(datt, ATT_HEADS).reshape(ATT_HEADS * S, ATT_HDIM)
    dol, _, _ = rowmap_bwd(f_combine, att_o + att_l, [], [datt_hm], name="att_combine_b", row_grad=[True] * 6, tr=2048)
    dqs, dks, dvs, dtabs = [], [], [], []
    for p, (win, dil) in enumerate(ATT_PATTERNS):
        qp, kp, vp, bias, nb = att_in[p]
        do = regroup_heads(dol[p].reshape(ATT_HEADS, S, ATT_HDIM), dil)
        dl = regroup_heads(dol[3 + p].reshape(ATT_HEADS, S, ATT_HDIM), dil)
        dq, dkc, dkp, dvc, dvp, dbias = att_bwd(qp, kp, vp, bias, do, dl, nb, name=f"att_bwd{p}")
        dqs.append(from_heads(dq, dil))
        dks.append(from_heads(shift_add(dkc, dkp, nb, name=f"att_dk{p}"), dil))
        dvs.append(from_heads(shift_add(dvc, dvp, nb, name=f"att_dv{p}"), dil))
        dtabs.append(att_bias_grad(dbias, dil, name=f"att_dtab{p}"))
    g["rel_table"] = jnp.concatenate(dtabs, axis=1)
    dk = rowmap(f_sum3, dks, [], [F32], name="att_dk_sum")[0]
    dv = rowmap(f_sum3, dvs, [], [F32], name="att_dv_sum")[0]
    ddt = jnp.pad(ddtraw_t.T, ((0, 0), (0, DT_PAD - HY_DT)))
    dproj = jnp.concatenate([dz, dxbc_raw] + dqs + [dk, dv, ddt], axis=-1).astype(BF16)
    g["hy_in_t"] = matmul(transpose(h0, name="hy_h_t"), dproj, mode="nn", out_dtype=F32, name="hy_in_dw")
    dh0 = matmul(dproj, w_in, mode="nn", out_dtype=F32, name="hy_in_dx")
    (dres,), (dg_, dsc, dsh), _ = rowmap_bwd(f_normmod, [x], [sp["norm_mix_g"][0], mods[0][1], mods[0][0]], [dh0],
                                             name="l0_norm_b", row_grad=[True], row_add=[dres])
    g["norm_mix_g0"] = dg_
    dmods[0][1], dmods[0][0] = dsc, dsh
    return loss_rows, dres, g, dmods


ANY = pl.BlockSpec(memory_space=pl.ANY)
WHOLE_VMEM = pl.BlockSpec(memory_space=pltpu.VMEM)


def _place():
    return lax.axis_index("x"), lax.axis_index("y"), lax.axis_index("c")


def _other_chips(x, y):
    return [(1 - x, y), (x, 1 - y), (1 - x, 1 - y)]


def allgather_small(v, *, name):
    m_per = v.shape[0]

    def body(x_ref, out_ref, send_sems, recv_sems, local_sem):
        x, y, c = _place()
        me, sibling = (x, y, c), (x, y, 1 - c)
        chips = _other_chips(x, y)

        def rows(px, py, pc):
            return out_ref.at[pl.ds((4 * px + 2 * py + pc) * m_per, m_per), :]

        def copy(k, block, to, src=None):
            return pltpu.make_async_remote_copy(
                src_ref=rows(*block) if src is None else src, dst_ref=rows(*block),
                send_sem=send_sems.at[k], recv_sem=recv_sems.at[k], device_id=to, device_id_type=MESH)

        mine = pltpu.make_async_copy(x_ref, rows(*me), local_sem)
        mine.start()
        first = [copy(0, me, sibling, src=x_ref)]
        first += [copy(1 + j, me, (*chip, c), src=x_ref) for j, chip in enumerate(chips)]
        for cp in first:
            cp.start()
        passed = [copy(4 + j, (*chip, c), sibling) for j, chip in enumerate(chips)]
        for j, chip in enumerate(chips):
            copy(1 + j, (*chip, c), me).wait_recv()
            passed[j].start()
        copy(0, sibling, me).wait_recv()
        for j, chip in enumerate(chips):
            copy(4 + j, (*chip, 1 - c), me).wait_recv()
        for cp in first + passed:
            cp.wait_send()
        mine.wait()

    return pl.pallas_call(
        body, name=name,
        out_shape=jax.ShapeDtypeStruct((N_DEV * m_per, LANES), v.dtype),
        in_specs=[WHOLE_VMEM], out_specs=WHOLE_VMEM,
        scratch_shapes=[pltpu.SemaphoreType.DMA((7,)), pltpu.SemaphoreType.DMA((7,)), pltpu.SemaphoreType.DMA],
    )(v)


def allgather_chips(pack, *, name):
    half_rows = pack.shape[0] // 2

    def body(p_ref, o_ref, send_sems, recv_sems, local_sem):
        x, y, c = _place()
        chips = _other_chips(x, y)
        sibling = (x, y, 1 - c)
        my_half = pl.ds(c * half_rows, half_rows)
        its_half = pl.ds((1 - c) * half_rows, half_rows)
        mine = pltpu.make_async_copy(p_ref, o_ref.at[2 * x + y], local_sem)
        mine.start()
        sends = [pltpu.make_async_remote_copy(
            src_ref=p_ref.at[my_half], dst_ref=o_ref.at[2 * x + y, my_half],
            send_sem=send_sems.at[k], recv_sem=recv_sems.at[k],
            device_id=(cx, cy, c), device_id_type=MESH) for k, (cx, cy) in enumerate(chips)]
        for cp in sends:
            cp.start()
        passed = []
        for k, (cx, cy) in enumerate(chips):
            landed = o_ref.at[2 * cx + cy, my_half]
            pltpu.make_async_remote_copy(
                src_ref=p_ref.at[my_half], dst_ref=landed, send_sem=send_sems.at[k], recv_sem=recv_sems.at[k],
                device_id=(cx, cy, c), device_id_type=MESH).wait_recv()
            cp = pltpu.make_async_remote_copy(
                src_ref=landed, dst_ref=landed, send_sem=send_sems.at[3 + k], recv_sem=recv_sems.at[3 + k],
                device_id=sibling, device_id_type=MESH)
            cp.start()
            passed.append(cp)
        for k, (cx, cy) in enumerate(chips):
            from_sibling = o_ref.at[2 * cx + cy, its_half]
            pltpu.make_async_remote_copy(
                src_ref=from_sibling, dst_ref=from_sibling, send_sem=send_sems.at[3 + k], recv_sem=recv_sems.at[3 + k],
                device_id=sibling, device_id_type=MESH).wait_recv()
        for cp in sends + passed:
            cp.wait_send()
        mine.wait()

    return pl.pallas_call(
        body, name=name,
        out_shape=jax.ShapeDtypeStruct((N_CHIPS,) + pack.shape, pack.dtype),
        in_specs=[ANY], out_specs=ANY,
        scratch_shapes=[pltpu.SemaphoreType.DMA((6,)), pltpu.SemaphoreType.DMA((6,)), pltpu.SemaphoreType.DMA],
    )(pack)


def swap_halves(gpack, *, name):
    half_rows = gpack.shape[1] // 2

    def body(g_ref, r_ref, send_sems, recv_sems):
        x, y, c = _place()
        its_half = pl.ds((1 - c) * half_rows, half_rows)
        copies = [pltpu.make_async_remote_copy(
            src_ref=g_ref.at[s, its_half], dst_ref=r_ref.at[s], send_sem=send_sems.at[s], recv_sem=recv_sems.at[s],
            device_id=(x, y, 1 - c), device_id_type=MESH) for s in range(N_CHIPS)]
        for cp in copies:
            cp.start()
        for cp in copies:
            cp.wait()

    return pl.pallas_call(
        body, name=name,
        out_shape=jax.ShapeDtypeStruct((N_CHIPS, half_rows) + gpack.shape[2:], gpack.dtype),
        in_specs=[ANY], out_specs=ANY,
        scratch_shapes=[pltpu.SemaphoreType.DMA((N_CHIPS,)), pltpu.SemaphoreType.DMA((N_CHIPS,))],
    )(gpack)


def scatter_chips(gpack, *, name):
    def body(g_ref, own_ref, recv_ref, send_sems, recv_sems, local_sem):
        x, y, c = _place()
        chips = _other_chips(x, y)
        mine = pltpu.make_async_copy(g_ref.at[2 * x + y], own_ref, local_sem)
        mine.start()
        sends = [pltpu.make_async_remote_copy(
            src_ref=g_ref.at[2 * cx + cy], dst_ref=recv_ref.at[k], send_sem=send_sems.at[k], recv_sem=recv_sems.at[k],
            device_id=(cx, cy, c), device_id_type=MESH) for k, (cx, cy) in enumerate(chips)]
        for cp in sends:
            cp.start()
        for cp in sends:
            cp.wait_recv()
        for cp in sends:
            cp.wait_send()
        mine.wait()

    slot = jax.ShapeDtypeStruct(gpack.shape[1:], gpack.dtype)
    return pl.pallas_call(
        body, name=name,
        out_shape=[slot, jax.ShapeDtypeStruct((3,) + gpack.shape[1:], gpack.dtype)],
        in_specs=[ANY], out_specs=[ANY, ANY],
        scratch_shapes=[pltpu.SemaphoreType.DMA((3,)), pltpu.SemaphoreType.DMA((3,)), pltpu.SemaphoreType.DMA],
    )(gpack)


def sibling_swap(p, *, name):
    def body(p_ref, r_ref, send_sem, recv_sem):
        x, y, c = _place()
        cp = pltpu.make_async_remote_copy(src_ref=p_ref, dst_ref=r_ref, send_sem=send_sem, recv_sem=recv_sem,
                                          device_id=(x, y, 1 - c), device_id_type=MESH)
        cp.start()
        cp.wait()

    return pl.pallas_call(
        body, name=name, out_shape=jax.ShapeDtypeStruct(p.shape, p.dtype),
        in_specs=[ANY], out_specs=ANY,
        scratch_shapes=[pltpu.SemaphoreType.DMA, pltpu.SemaphoreType.DMA],
    )(p)


def sum_devices(v_all, *, name):
    m_per = v_all.shape[0] // N_DEV

    def body(v_ref, o_ref):
        acc = v_ref[pl.ds(0, m_per), :]
        for d in range(1, N_DEV):
            acc = acc + v_ref[pl.ds(d * m_per, m_per), :]
        o_ref[...] = acc

    return pl.pallas_call(
        body, name=name, out_shape=jax.ShapeDtypeStruct((m_per, LANES), F32),
        in_specs=[WHOLE_VMEM], out_specs=WHOLE_VMEM,
    )(v_all)


WEIGHTS = ['ada_w', 'ada_b', 'norm_mix_g', 'norm_ffn_g', 'hy_w_in', 'hy_conv_w', 'hy_conv_b', 'hy_dt_bias', 'hy_a_log',
           'hy_d_skip', 'hy_ssm_norm_g', 'hy_w_out', 'rel_table', 'cv_w_pw1', 'cv_b_pw1', 'cv_w_dw', 'cv_b_dw', 'cv_ln_g',
           'cv_ln_b', 'cv_w_pw2', 'cv_b_pw2', 'ffn_w_gate', 'ffn_w_up', 'ffn_w_down', 'final_norm_g']
BIG = ('ada_w', 'hy_w_in', 'hy_w_out', 'cv_w_pw1', 'cv_w_pw2', 'ffn_w_gate', 'ffn_w_up', 'ffn_w_down')
SMALL_SHARDED = {'hy_conv_w': (1, 4, 3072), 'cv_b_pw1': (1, 2048), 'cv_w_dw': (1, 31, 1024), 'cv_b_dw': (1, 1024),
                 'cv_ln_g': (1, 1024), 'cv_ln_b': (1, 1024), 'cv_b_pw2': (1, 1024)}
SMALL_GRADS = {'ada_b': (2, 6144), 'norm_mix_g': (2, 1024), 'norm_ffn_g': (2, 1024), 'hy_conv_w': (1, 4, 3072),
               'hy_conv_b': (1, 3072), 'hy_dt_bias': (1, 32), 'hy_a_log': (1, 32), 'hy_d_skip': (1, 32),
               'hy_ssm_norm_g': (1, 2048), 'rel_table': (32, 48), 'cv_b_pw1': (1, 2048), 'cv_w_dw': (1, 31, 1024),
               'cv_b_dw': (1, 1024), 'cv_ln_g': (1, 1024), 'cv_ln_b': (1, 1024), 'cv_b_pw2': (1, 1024),
               'final_norm_g': (1024,), 'loss': (1,)}

PACK_LAYOUT = (('hy_in_t', 2568), ('hy_out', 768), ('pw1_t', 512), ('pw2', 256),
               ('gate_t0', 704), ('up_t0', 704), ('down0', 704), ('gate_t1', 704), ('up_t1', 704), ('down1', 704))
PACK_ROWS = 8448


def _pack_offsets():
    off, out = 0, {}
    for nm, r in PACK_LAYOUT:
        out[nm] = (off, r)
        off += r
    return out


PACK_OFF = _pack_offsets()


def _to_lanes(flat):
    n = flat.shape[0]
    m = -(-n // (8 * LANES)) * 8
    return jnp.pad(flat, (0, m * LANES - n)).reshape(m, LANES)


def _split(flat, shapes):
    out, off = {}, 0
    for nm, shp in shapes.items():
        n = int(np.prod(shp))
        out[nm] = flat[off:off + n].reshape(shp)
        off += n
    return out


def kernel(x, c, ada_w, ada_b, norm_mix_g, norm_ffn_g, hy_w_in, hy_conv_w, hy_conv_b, hy_dt_bias, hy_a_log, hy_d_skip, hy_ssm_norm_g, hy_w_out, rel_table, cv_w_pw1, cv_b_pw1, cv_w_dw, cv_b_dw, cv_ln_g, cv_ln_b, cv_w_pw2, cv_b_pw2, ffn_w_gate, ffn_w_up, ffn_w_down, final_norm_g, loss_target, m_ada_w, m_ada_b, m_norm_mix_g, m_norm_ffn_g, m_hy_w_in, m_hy_conv_w, m_hy_conv_b, m_hy_dt_bias, m_hy_a_log, m_hy_d_skip, m_hy_ssm_norm_g, m_hy_w_out, m_rel_table, m_cv_w_pw1, m_cv_b_pw1, m_cv_w_dw, m_cv_b_dw, m_cv_ln_g, m_cv_ln_b, m_cv_w_pw2, m_cv_b_pw2, m_ffn_w_gate, m_ffn_w_up, m_ffn_w_down, m_final_norm_g, v_ada_w, v_ada_b, v_norm_mix_g, v_norm_ffn_g, v_hy_w_in, v_hy_conv_w, v_hy_conv_b, v_hy_dt_bias, v_hy_a_log, v_hy_d_skip, v_hy_ssm_norm_g, v_hy_w_out, v_rel_table, v_cv_w_pw1, v_cv_b_pw1, v_cv_w_dw, v_cv_b_dw, v_cv_ln_g, v_cv_ln_b, v_cv_w_pw2, v_cv_b_pw2, v_ffn_w_gate, v_ffn_w_up, v_ffn_w_down, v_final_norm_g):
    args = (x, c, ada_w, ada_b, norm_mix_g, norm_ffn_g, hy_w_in, hy_conv_w, hy_conv_b, hy_dt_bias, hy_a_log, hy_d_skip, hy_ssm_norm_g, hy_w_out, rel_table, cv_w_pw1, cv_b_pw1, cv_w_dw, cv_b_dw, cv_ln_g, cv_ln_b, cv_w_pw2, cv_b_pw2, ffn_w_gate, ffn_w_up, ffn_w_down, final_norm_g, loss_target, m_ada_w, m_ada_b, m_norm_mix_g, m_norm_ffn_g, m_hy_w_in, m_hy_conv_w, m_hy_conv_b, m_hy_dt_bias, m_hy_a_log, m_hy_d_skip, m_hy_ssm_norm_g, m_hy_w_out, m_rel_table, m_cv_w_pw1, m_cv_b_pw1, m_cv_w_dw, m_cv_b_dw, m_cv_ln_g, m_cv_ln_b, m_cv_w_pw2, m_cv_b_pw2, m_ffn_w_gate, m_ffn_w_up, m_ffn_w_down, m_final_norm_g, v_ada_w, v_ada_b, v_norm_mix_g, v_norm_ffn_g, v_hy_w_in, v_hy_conv_w, v_hy_conv_b, v_hy_dt_bias, v_hy_a_log, v_hy_d_skip, v_hy_ssm_norm_g, v_hy_w_out, v_rel_table, v_cv_w_pw1, v_cv_b_pw1, v_cv_w_dw, v_cv_b_dw, v_cv_ln_g, v_cv_ln_b, v_cv_w_pw2, v_cv_b_pw2, v_ffn_w_gate, v_ffn_w_up, v_ffn_w_down, v_final_norm_g)
    x_in, c_in = args[0], args[1]
    w = dict(zip(WEIGHTS, args[2:27], strict=True))
    tgt = args[27]
    m_in = dict(zip(WEIGHTS, args[28:53], strict=True))
    v_in = dict(zip(WEIGHTS, args[53:78], strict=True))
    xi, yi, ci = _place()
    chip = 2 * xi + yi
    dev = 2 * chip + ci

    cs = rowmap(f_silu, [c_in.reshape(8, LANES)], [], [F32], name="cond_silu", tr=8)[0]
    cs_all = allgather_small(cs, name="gather_cond").reshape(N_DEV, D)
    cs16 = jnp.pad(cs_all, ((0, 8), (0, 0)))
    modpart = jnp.stack([matmul(cs16, w['ada_w'][i], mode="nn", out_dtype=F32, name=f"ada_fwd{i}")[:N_DEV]
                         for i in range(2)], axis=1)
    shard_names = list(SMALL_SHARDED)
    payload = jnp.concatenate([modpart.reshape(-1)] + [w[nm].reshape(-1) for nm in shard_names])
    got = allgather_small(_to_lanes(payload), name="gather_mod").reshape(N_DEV, -1)[0::2]
    modparts = got[:, :modpart.size].reshape(N_CHIPS, N_DEV, 2, 1536)
    mine = lax.dynamic_index_in_dim(modparts, dev, axis=1, keepdims=False)
    mod = jnp.transpose(mine, (1, 0, 2)).reshape(2, 6 * D) + w['ada_b']
    mods = [[mod[i, j * D:(j + 1) * D].reshape(1, D) for j in range(6)] for i in range(2)]
    sp = {}
    off = modpart.size
    for nm in shard_names:
        shp = w[nm].shape
        n = int(np.prod(shp))
        parts = got[:, off:off + n].reshape((N_CHIPS,) + shp)
        sp[nm] = jnp.concatenate([parts[s] for s in range(N_CHIPS)], axis=-1)
        off += n

    def rows_of(nm, i=None):
        a = w[nm][0 if i is None else i]
        return (a.T if nm in ('hy_w_in', 'cv_w_pw1', 'ffn_w_gate', 'ffn_w_up') else a).astype(BF16)

    pieces = [rows_of('hy_w_in'), rows_of('hy_w_out'), rows_of('cv_w_pw1'), rows_of('cv_w_pw2')]
    for i in range(2):
        pieces += [rows_of('ffn_w_gate', i), rows_of('ffn_w_up', i), rows_of('ffn_w_down', i)]
    n_rows = sum(p.shape[0] for p in pieces)
    pack = jnp.concatenate(pieces + [jnp.zeros((PACK_ROWS - n_rows, D), BF16)], axis=0)
    full = allgather_chips(pack, name="gather_weights")

    def whole(nm):
        o, r = PACK_OFF[nm]
        return full[:, o:o + r].reshape(N_CHIPS * r, D)

    wts = {"hy_in_t": hy_to_cat(whole('hy_in_t')), "hy_out": whole('hy_out'), "pw1_t": whole('pw1_t'), "pw2": whole('pw2'),
           "gu_t": [jnp.concatenate([whole(f'gate_t{i}'), whole(f'up_t{i}')], axis=0) for i in range(2)],
           "down": [whole(f'down{i}') for i in range(2)]}

    sp = {"norm_mix_g": [w['norm_mix_g'][i].reshape(1, D) for i in range(2)],
          "norm_ffn_g": [w['norm_ffn_g'][i].reshape(1, D) for i in range(2)],
          "hy_conv_w": sp['hy_conv_w'][0], "hy_conv_b": w['hy_conv_b'],
          "hy_dt_bias": w['hy_dt_bias'].reshape(SSM_HEADS, 1), "hy_a_log": w['hy_a_log'].reshape(SSM_HEADS, 1),
          "hy_d_skip": w['hy_d_skip'].reshape(SSM_HEADS, 1), "hy_ssm_norm_g": w['hy_ssm_norm_g'],
          "rel_table": w['rel_table'], "cv_b_pw1": sp['cv_b_pw1'], "cv_w_dw": sp['cv_w_dw'][0], "cv_b_dw": sp['cv_b_dw'],
          "cv_ln_g": sp['cv_ln_g'], "cv_ln_b": sp['cv_ln_b'], "cv_b_pw2": sp['cv_b_pw2'],
          "final_norm_g": w['final_norm_g'].reshape(1, D)}

    loss_rows, grad_x, g, dmods = device_step(x_in[0], tgt[0], mods, wts, sp)

    dmod = jnp.stack([jnp.concatenate([d.reshape(-1) for d in dmods[i]]) for i in range(2)])
    small = {'ada_b': dmod, 'norm_mix_g': jnp.stack([g[f'norm_mix_g{i}'].reshape(-1) for i in range(2)]),
             'norm_ffn_g': jnp.stack([g[f'norm_ffn_g{i}'].reshape(-1) for i in range(2)]),
             'loss': jnp.sum(loss_rows).reshape(1)}
    for nm in SMALL_GRADS:
        if nm not in small:
            small[nm] = g[nm]
    vec = _to_lanes(jnp.concatenate([small[nm].reshape(-1) for nm in SMALL_GRADS]))
    vec_all = allgather_small(vec, name="gather_small_grads")
    tot = _split(sum_devices(vec_all, name="sum_small_grads").reshape(-1), SMALL_GRADS)
    dmod_all = vec_all.reshape(N_DEV, -1)[:, :2 * 6 * D].reshape(N_DEV, 2, 6 * D)

    def rows_bf16(nm):
        return transpose(g[nm], name="grad_t_" + nm)

    gp = [hy_from_cat(rows_bf16('hy_in_t')), rows_bf16('hy_out'), rows_bf16('pw1_t'), rows_bf16('pw2')]
    for i in range(2):
        gu = rows_bf16(f'gu_t{i}')
        gp += [gu[:FFN_HIDDEN], gu[FFN_HIDDEN:], rows_bf16(f'down{i}')]
    gp = [a.reshape(N_CHIPS, a.shape[0] // N_CHIPS, D) for a in gp]
    gpack = jnp.concatenate(gp + [jnp.zeros((N_CHIPS, PACK_ROWS - n_rows, D), BF16)], axis=1)
    half = PACK_ROWS // 2
    theirs = swap_halves(gpack, name="swap_grad_halves")
    ours = lax.dynamic_slice_in_dim(gpack, ci * half, half, axis=1)
    chip_sum = rowmap(f_add, [ours.reshape(N_CHIPS * half, D), theirs.reshape(N_CHIPS * half, D)], [], [BF16],
                      name="sum_core_grads")[0].reshape(N_CHIPS, half, D)
    own, recv = scatter_chips(chip_sum, name="scatter_grads")
    mine_half = rowmap(f_sum4, [own, recv[0], recv[1], recv[2]], [], [F32], name="sum_chip_grads")[0]
    its_half = sibling_swap(mine_half, name="swap_grads")
    red = jnp.concatenate([jnp.where(ci == 0, mine_half, its_half), jnp.where(ci == 0, its_half, mine_half)], axis=0)

    def shard_grad(nm, i=None):
        key = {'hy_w_in': 'hy_in_t', 'hy_w_out': 'hy_out', 'cv_w_pw1': 'pw1_t', 'cv_w_pw2': 'pw2'}.get(nm)
        if key is None:
            key = {'ffn_w_gate': 'gate_t', 'ffn_w_up': 'up_t', 'ffn_w_down': 'down'}[nm] + str(i)
        o, r = PACK_OFF[key]
        a = red[o:o + r]
        return a.T if key.endswith('_t') or key[:-1].endswith('_t') else a

    grads = {}
    grads['hy_w_in'] = shard_grad('hy_w_in')[None]
    grads['hy_w_out'] = shard_grad('hy_w_out')[None]
    grads['cv_w_pw1'] = shard_grad('cv_w_pw1')[None]
    grads['cv_w_pw2'] = shard_grad('cv_w_pw2')[None]
    for nm in ('ffn_w_gate', 'ffn_w_up', 'ffn_w_down'):
        grads[nm] = jnp.stack([shard_grad(nm, i) for i in range(2)])
    cs16 = jnp.pad(cs_all, ((0, 8), (0, 0)))
    dm_mine = lax.dynamic_slice_in_dim(dmod_all, chip * 1536, 1536, axis=2)
    dm16 = jnp.pad(dm_mine, ((0, 8), (0, 0), (0, 0)))
    grads['ada_w'] = jnp.stack([matmul(cs16, dm16[:, i], mode="tn", out_dtype=F32, name=f"ada_dw{i}") for i in range(2)])
    for nm, shp in SMALL_GRADS.items():
        if nm == 'loss':
            continue
        if nm in SMALL_SHARDED:
            n = w[nm].shape[-1]
            grads[nm] = lax.dynamic_slice_in_dim(tot[nm], chip * n, n, axis=len(shp) - 1)
        else:
            grads[nm] = tot[nm].reshape(w[nm].shape)

    delta, new_m, new_v = {}, {}, {}
    for nm in BIG:
        delta[nm], new_m[nm], new_v[nm] = adamw(w[nm], grads[nm], m_in[nm], v_in[nm], name="adamw_" + nm)
    smalls = [nm for nm in WEIGHTS if nm not in BIG]
    packed = [_to_lanes(jnp.concatenate([d[nm].reshape(-1) for nm in smalls])) for d in (w, grads, m_in, v_in)]
    res = rowmap(f_adamw, packed, [], [F32] * 3, name="adamw_small", tr=_rows_tile(packed[0].shape[0]))
    for d, r in zip((delta, new_m, new_v), res, strict=True):
        d.update(_split(r.reshape(-1), {nm: w[nm].shape for nm in smalls}))

    loss = tot['loss'].reshape(())
    return (loss, grad_x[None], *[grads[nm] for nm in WEIGHTS], *[delta[nm] for nm in WEIGHTS],
            *[new_m[nm] for nm in WEIGHTS], *[new_v[nm] for nm in WEIGHTS])
```

```python
import functools
import math

import jax
import jax.numpy as jnp
import numpy as np
from jax import lax
from jax.experimental import pallas as pl
from jax.experimental.pallas import tpu as pltpu

F32 = jnp.float32
BF16 = jnp.bfloat16
MESH = pl.DeviceIdType.MESH

D = 1024
S = 4096
EPS = 1e-6
SSM_INNER = 2048
SSM_HEADS = 32
SSM_HDIM = 64
SSM_GROUPS = 4
SSM_STATE = 128
SSM_CONVK = 4
SSM_CONV_DIM = 3072
CHUNK = 128
N_CHUNKS = S // CHUNK
ATT_HEADS = 16
ATT_HDIM = 64
ATT_PATTERNS = ((128, 1), (512, 4), (2048, 16))
ATT_BLK = 128
REL_BUCKETS = 32
REL_MAX_DIST = 2048
CONV_WIDTH = 31
FFN_HIDDEN = 2816
N_CHIPS = 4
N_DEV = 8
ADAM_LR, ADAM_B1, ADAM_B2, ADAM_EPS, ADAM_WD, ADAM_STEP = 0.001, 0.9, 0.999, 1e-08, 0.01, 10

VMEM_LIMIT_BYTES = 56 * 1024 * 1024
LANES = 128


def _cparams(*sem):
    return pltpu.CompilerParams(dimension_semantics=sem, vmem_limit_bytes=VMEM_LIMIT_BYTES)


def _pick(n, cap, mult=LANES):
    best = None
    for t in range(mult, min(n, cap) + 1, mult):
        if n % t == 0:
            best = t
    return best or n


def _dot(a, b, ca, cb):
    return lax.dot_general(a.astype(BF16), b.astype(BF16), (((ca,), (cb,)), ((), ())), preferred_element_type=F32)


@jax.custom_vjp
def mm(a, b):
    return _dot(a, b, 1, 0)


def _mm_fwd(a, b):
    return _dot(a, b, 1, 0), (a, b)


def _mm_bwd(res, g):
    a, b = res
    return _dot(g, b, 1, 1).astype(a.dtype), _dot(a, g, 0, 0).astype(b.dtype)


mm.defvjp(_mm_fwd, _mm_bwd)


@jax.custom_vjp
def mm_nt(a, b):
    return _dot(a, b, 1, 1)


def _mm_nt_fwd(a, b):
    return _dot(a, b, 1, 1), (a, b)


def _mm_nt_bwd(res, g):
    a, b = res
    return _dot(g, b, 1, 0).astype(a.dtype), _dot(g, a, 0, 0).astype(b.dtype)


mm_nt.defvjp(_mm_nt_fwd, _mm_nt_bwd)


@jax.custom_vjp
def mm_tn(a, b):
    return _dot(a, b, 0, 0)


def _mm_tn_fwd(a, b):
    return _dot(a, b, 0, 0), (a, b)


def _mm_tn_bwd(res, g):
    a, b = res
    return _dot(b, g, 1, 1).astype(a.dtype), _dot(a, g, 1, 0).astype(b.dtype)


mm_tn.defvjp(_mm_tn_fwd, _mm_tn_bwd)


def matmul(a, b, *, mode, out_dtype, name, n=None, b_off=0, tm_cap=1024, tn_cap=512, tk_cap=1536):
    if mode == "tn":
        k_dim, m_dim = a.shape
    else:
        m_dim, k_dim = a.shape
    n_dim = n if n is not None else (b.shape[0] if mode == "nt" else b.shape[1])
    tm = m_dim if m_dim < LANES else _pick(m_dim, tm_cap)
    tn = _pick(n_dim, tn_cap)
    tk = k_dim if k_dim < LANES else _pick(k_dim, tk_cap)
    assert m_dim % tm == 0 and n_dim % tn == 0 and k_dim % tk == 0 and b_off % tn == 0
    nk = k_dim // tk
    off = b_off // tn
    if mode == "nn":
        a_spec = pl.BlockSpec((tm, tk), lambda i, j, k: (i, k))
        b_spec = pl.BlockSpec((tk, tn), lambda i, j, k: (k, j))
        ca, cb = 1, 0
    elif mode == "nt":
        a_spec = pl.BlockSpec((tm, tk), lambda i, j, k: (i, k))
        b_spec = pl.BlockSpec((tn, tk), lambda i, j, k: (j + off, k))
        ca, cb = 1, 1
    else:
        a_spec = pl.BlockSpec((tk, tm), lambda i, j, k: (k, i))
        b_spec = pl.BlockSpec((tk, tn), lambda i, j, k: (k, j))
        ca, cb = 0, 0

    def body(a_ref, b_ref, o_ref, acc_ref):
        part = _dot(a_ref[...], b_ref[...], ca, cb)
        if nk == 1:
            o_ref[...] = part.astype(o_ref.dtype)
        else:
            k = pl.program_id(2)

            @pl.when(k == 0)
            def _():
                acc_ref[...] = part

            @pl.when(k > 0)
            def _():
                acc_ref[...] += part

            @pl.when(k == nk - 1)
            def _():
                o_ref[...] = acc_ref[...].astype(o_ref.dtype)

    return pl.pallas_call(
        body, name=name,
        out_shape=jax.ShapeDtypeStruct((m_dim, n_dim), out_dtype),
        grid=(m_dim // tm, n_dim // tn, nk),
        in_specs=[a_spec, b_spec],
        out_specs=pl.BlockSpec((tm, tn), lambda i, j, k: (i, j)),
        scratch_shapes=[pltpu.VMEM((tm, tn), F32)],
        compiler_params=_cparams("parallel", "parallel", "arbitrary"),
    )(a, b)


def _f32(xs):
    return [x.astype(F32) for x in xs]


def rowmap(f, rows, consts, out_dtypes, *, name, tr=256):
    r_dim = rows[0].shape[0]
    tr = _pick(r_dim, tr, mult=8)
    assert r_dim % tr == 0
    nr, nc = len(rows), len(consts)
    outs = jax.eval_shape(lambda *xs: f(*xs), *[jax.ShapeDtypeStruct((tr, x.shape[1]), F32) for x in rows],
                          *[jax.ShapeDtypeStruct(x.shape, F32) for x in consts])

    def body(*refs):
        res = f(*_f32([r[...] for r in refs[:nr + nc]]))
        for o_ref, o in zip(refs[nr + nc:], res, strict=True):
            o_ref[...] = o.astype(o_ref.dtype)

    return pl.pallas_call(
        body, name=name,
        out_shape=[jax.ShapeDtypeStruct((r_dim, o.shape[1]), dt) for o, dt in zip(outs, out_dtypes, strict=True)],
        grid=(r_dim // tr,),
        in_specs=[pl.BlockSpec((tr, x.shape[1]), lambda i: (i, 0)) for x in rows]
        + [pl.BlockSpec(x.shape, lambda i: (0, 0)) for x in consts],
        out_specs=[pl.BlockSpec((tr, o.shape[1]), lambda i: (i, 0)) for o in outs],
        compiler_params=_cparams("parallel"),
    )(*rows, *consts)


def rowmap_bwd(f, rows, consts, cts, *, name, row_grad, row_dtypes=None, tr=256, emit=(), row_add=None):
    r_dim = rows[0].shape[0]
    tr = _pick(r_dim, tr, mult=8)
    assert r_dim % tr == 0
    nr, nc, nct = len(rows), len(consts), len(cts)
    gi = [i for i, flag in enumerate(row_grad) if flag]
    row_dtypes = row_dtypes or [F32] * len(gi)
    row_add = row_add or [None] * len(gi)
    adds = [a for a in row_add if a is not None]
    outs = jax.eval_shape(lambda *xs: f(*xs), *[jax.ShapeDtypeStruct((tr, x.shape[1]), F32) for x in rows],
                          *[jax.ShapeDtypeStruct(x.shape, F32) for x in consts])

    def body(*refs):
        ins = _f32([r[...] for r in refs[:nr + nc]])
        ct = _f32([r[...] for r in refs[nr + nc:nr + nc + nct]])
        add_refs = list(refs[nr + nc + nct:nr + nc + nct + len(adds)])
        o_refs = refs[nr + nc + nct + len(adds):]
        res, vjp = jax.vjp(f, *ins)
        grads = vjp(tuple(ct))
        for o_ref, i, a in zip(o_refs[:len(gi)], gi, row_add):
            g = grads[i] if a is None else grads[i] + add_refs.pop(0)[...].astype(F32)
            o_ref[...] = g.astype(o_ref.dtype)
        first = pl.program_id(0) == 0
        for o_ref, g in zip(o_refs[len(gi):len(gi) + nc], grads[nr:]):
            @pl.when(first)
            def _(o_ref=o_ref, g=g):
                o_ref[...] = g

            @pl.when(jnp.logical_not(first))
            def _(o_ref=o_ref, g=g):
                o_ref[...] += g
        for o_ref, i in zip(o_refs[len(gi) + nc:], emit):
            o_ref[...] = res[i].astype(o_ref.dtype)

    out_shape = ([jax.ShapeDtypeStruct(rows[i].shape, dt) for i, dt in zip(gi, row_dtypes, strict=True)]
                 + [jax.ShapeDtypeStruct(x.shape, F32) for x in consts]
                 + [jax.ShapeDtypeStruct((r_dim, outs[i].shape[1]), F32) for i in emit])
    out_specs = ([pl.BlockSpec((tr, rows[i].shape[1]), lambda i_: (i_, 0)) for i in gi]
                 + [pl.BlockSpec(x.shape, lambda i_: (0, 0)) for x in consts]
                 + [pl.BlockSpec((tr, outs[i].shape[1]), lambda i_: (i_, 0)) for i in emit])
    res = pl.pallas_call(
        body, name=name,
        out_shape=out_shape,
        grid=(r_dim // tr,),
        in_specs=[pl.BlockSpec((tr, x.shape[1]), lambda i: (i, 0)) for x in rows]
        + [pl.BlockSpec(x.shape, lambda i: (0, 0)) for x in consts]
        + [pl.BlockSpec((tr, x.shape[1]), lambda i: (i, 0)) for x in list(cts) + adds],
        out_specs=out_specs,
        compiler_params=_cparams("arbitrary"),
    )(*rows, *consts, *cts, *adds)
    return res[:len(gi)], res[len(gi):len(gi) + nc], res[len(gi) + nc:]


def transpose(a, *, name, out_dtype=BF16, tr=512, tc=512):
    r_dim, c_dim = a.shape
    tr, tc = _pick(r_dim, tr), _pick(c_dim, tc)

    def body(a_ref, o_ref):
        o_ref[...] = a_ref[...].astype(F32).T.astype(o_ref.dtype)

    return pl.pallas_call(
        body, name=name, out_shape=jax.ShapeDtypeStruct((c_dim, r_dim), out_dtype),
        grid=(r_dim // tr, c_dim // tc),
        in_specs=[pl.BlockSpec((tr, tc), lambda i, j: (i, j))],
        out_specs=pl.BlockSpec((tc, tr), lambda i, j: (j, i)),
        compiler_params=_cparams("parallel", "parallel"),
    )(a)


CONV_HALO = 32
CONV_CHUNK = 256


def conv_fwd(x, w, b, *, name, cb=256):
    s_dim, c_dim = x.shape
    taps = w.shape[0]
    assert taps - 1 <= CONV_HALO and s_dim % CONV_CHUNK == 0 and c_dim % cb == 0
    n_chunks = s_dim // CONV_CHUNK
    ext = CONV_CHUNK + CONV_HALO

    def body(x_ref, w_ref, b_ref, o_ref, xp_ref):
        xp_ref[pl.ds(0, CONV_HALO), :] = jnp.zeros((CONV_HALO, cb), F32)
        xp_ref[pl.ds(CONV_HALO, s_dim), :] = x_ref[...].astype(F32)
        wv = w_ref[...].astype(F32)
        bv = b_ref[...].astype(F32)

        def chunk(t, carry):
            base = pl.multiple_of(t * CONV_CHUNK, CONV_CHUNK)
            xe = xp_ref[pl.ds(base, ext), :]
            acc = jnp.broadcast_to(bv, (CONV_CHUNK, cb))
            for j in range(taps):
                sh = xe if j == 0 else pltpu.roll(xe, shift=j, axis=0)
                acc = acc + wv[taps - 1 - j:taps - j, :] * sh[CONV_HALO:, :]
            o_ref[pl.ds(base, CONV_CHUNK), :] = acc
            return carry

        lax.fori_loop(0, n_chunks, chunk, 0)

    return pl.pallas_call(
        body, name=name,
        out_shape=jax.ShapeDtypeStruct((s_dim, c_dim), F32),
        grid=(c_dim // cb,),
        in_specs=[pl.BlockSpec((s_dim, cb), lambda i: (0, i)), pl.BlockSpec((taps, cb), lambda i: (0, i)),
                  pl.BlockSpec((1, cb), lambda i: (0, i))],
        out_specs=pl.BlockSpec((s_dim, cb), lambda i: (0, i)),
        scratch_shapes=[pltpu.VMEM((s_dim + CONV_HALO, cb), F32)],
        compiler_params=_cparams("parallel"),
    )(x, w, b)


def conv_bwd(x, w, g, *, name, cb=256):
    s_dim, c_dim = x.shape
    taps = w.shape[0]
    n_chunks = s_dim // CONV_CHUNK
    ext = CONV_CHUNK + CONV_HALO
    taps_pad = -(-taps // 8) * 8

    def body(x_ref, w_ref, g_ref, dx_ref, dw_ref, db_ref, xp_ref, gp_ref, acc_ref):
        xp_ref[pl.ds(0, CONV_HALO), :] = jnp.zeros((CONV_HALO, cb), F32)
        xp_ref[pl.ds(CONV_HALO, s_dim), :] = x_ref[...].astype(F32)
        gp_ref[pl.ds(0, s_dim), :] = g_ref[...].astype(F32)
        gp_ref[pl.ds(s_dim, CONV_HALO), :] = jnp.zeros((CONV_HALO, cb), F32)
        acc_ref[...] = jnp.zeros_like(acc_ref)
        wv = w_ref[...].astype(F32)

        def chunk(t, carry):
            base = pl.multiple_of(t * CONV_CHUNK, CONV_CHUNK)
            xe = xp_ref[pl.ds(base, ext), :]
            ge = gp_ref[pl.ds(base, ext), :]
            gc = ge[:CONV_CHUNK, :]
            dx = jnp.zeros((CONV_CHUNK, cb), F32)
            for j in range(taps):
                xs = xe if j == 0 else pltpu.roll(xe, shift=j, axis=0)
                gs = ge if j == 0 else pltpu.roll(ge, shift=ext - j, axis=0)
                k = taps - 1 - j
                dx = dx + wv[k:k + 1, :] * gs[:CONV_CHUNK, :]
                acc_ref[k:k + 1, :] += jnp.sum(gc * xs[CONV_HALO:, :], axis=0, keepdims=True)
            acc_ref[taps_pad:taps_pad + 1, :] += jnp.sum(gc, axis=0, keepdims=True)
            dx_ref[pl.ds(base, CONV_CHUNK), :] = dx
            return carry

        lax.fori_loop(0, n_chunks, chunk, 0)
        dw_ref[...] = acc_ref[0:taps, :]
        db_ref[...] = acc_ref[taps_pad:taps_pad + 1, :]

    return pl.pallas_call(
        body, name=name,
        out_shape=[jax.ShapeDtypeStruct((s_dim, c_dim), F32), jax.ShapeDtypeStruct((taps, c_dim), F32),
                   jax.ShapeDtypeStruct((1, c_dim), F32)],
        grid=(c_dim // cb,),
        in_specs=[pl.BlockSpec((s_dim, cb), lambda i: (0, i)), pl.BlockSpec((taps, cb), lambda i: (0, i)),
                  pl.BlockSpec((s_dim, cb), lambda i: (0, i))],
        out_specs=[pl.BlockSpec((s_dim, cb), lambda i: (0, i)), pl.BlockSpec((taps, cb), lambda i: (0, i)),
                   pl.BlockSpec((1, cb), lambda i: (0, i))],
        scratch_shapes=[pltpu.VMEM((s_dim + CONV_HALO, cb), F32), pltpu.VMEM((s_dim + CONV_HALO, cb), F32),
                        pltpu.VMEM((taps_pad + 8, cb), F32)],
        compiler_params=_cparams("parallel"),
    )(x, w, g)


def _iota2(n, axis):
    return lax.broadcasted_iota(jnp.int32, (n, n), axis)


def _to_col(row):
    n = row.shape[1]
    return jnp.sum(jnp.where(_iota2(n, 0) == _iota2(n, 1), jnp.broadcast_to(row, (n, n)), 0.0), axis=1, keepdims=True)


def _softplus(x):
    return jnp.maximum(x, 0.0) + jnp.log(1.0 + jnp.exp(-jnp.abs(x)))


def ssd_heads(x, dtraw, dt_bias, a_log, dskip, bm, cm, prev):
    h, q, _ = x.shape
    n = bm.shape[1]
    li = lax.broadcasted_iota(jnp.int32, (1, q, q), 1)
    si = lax.broadcasted_iota(jnp.int32, (1, q, q), 2)

    def to_col(row):
        return jnp.sum(jnp.where(li == si, jnp.broadcast_to(row, (h, q, q)), 0.0), axis=2, keepdims=True)

    dt_row = _softplus(dtraw + dt_bias)
    a_row = dt_row * (-jnp.exp(a_log))
    a_col = to_col(a_row)
    acs_col = jnp.sum(jnp.where(si <= li, jnp.broadcast_to(a_row, (h, q, q)), 0.0), axis=2, keepdims=True)
    acs_row = jnp.sum(jnp.where(li <= si, jnp.broadcast_to(a_col, (h, q, q)), 0.0), axis=1, keepdims=True)
    total = jnp.sum(a_row, axis=2, keepdims=True)
    xdt = x * to_col(dt_row)
    lmat = jnp.exp(jnp.where(li >= si, acs_col - acs_row, -1e30))
    bmb = jnp.broadcast_to(bm[None], (h, q, n))
    cmb = jnp.broadcast_to(cm[None], (h, q, n))
    y = bmm(mm_nt(cm, bm)[None] * lmat, xdt)
    y = y + bmm_nt(cmb, prev) * jnp.exp(acs_col)
    y = y + dskip * x
    state = bmm_tn(xdt * jnp.exp(total - acs_col), bmb)
    return y, jnp.exp(total) * prev + state


HEADS_PER_GROUP = SSM_HEADS // SSM_GROUPS
BM_COL0 = SSM_INNER // SSM_STATE
CM_COL0 = BM_COL0 + SSM_GROUPS


def ssd_fwd(xs_hm, dtraw_t, dt_bias, a_log, dskip, xbc):
    hg = HEADS_PER_GROUP

    def body(x_ref, dt_ref, dtb_ref, al_ref, dk_ref, bm_ref, cm_ref, y_ref, prev_ref, state_ref):
        @pl.when(pl.program_id(1) == 0)
        def _():
            state_ref[...] = jnp.zeros_like(state_ref)

        prev = state_ref[...]
        prev_ref[0] = prev
        y, nxt = ssd_heads(x_ref[...], dt_ref[...], dtb_ref[...], al_ref[...], dk_ref[...], bm_ref[...], cm_ref[...], prev)
        y_ref[...] = y
        state_ref[...] = nxt

    hp = pl.BlockSpec((hg, 1, 1), lambda g, c: (g, 0, 0))
    dtraw_t, dt_bias, a_log, dskip = [a.reshape(SSM_HEADS, 1, -1) for a in (dtraw_t, dt_bias, a_log, dskip)]
    return pl.pallas_call(
        body, name="ssd_fwd",
        out_shape=[jax.ShapeDtypeStruct((SSM_HEADS, S, SSM_HDIM), F32),
                   jax.ShapeDtypeStruct((N_CHUNKS, SSM_HEADS, SSM_HDIM, SSM_STATE), F32)],
        grid=(SSM_GROUPS, N_CHUNKS),
        in_specs=[pl.BlockSpec((hg, CHUNK, SSM_HDIM), lambda g, c: (g, c, 0)),
                  pl.BlockSpec((hg, 1, CHUNK), lambda g, c: (g, 0, c)), hp, hp, hp,
                  pl.BlockSpec((CHUNK, SSM_STATE), lambda g, c: (c, BM_COL0 + g)),
                  pl.BlockSpec((CHUNK, SSM_STATE), lambda g, c: (c, CM_COL0 + g))],
        out_specs=[pl.BlockSpec((hg, CHUNK, SSM_HDIM), lambda g, c: (g, c, 0)),
                   pl.BlockSpec((1, hg, SSM_HDIM, SSM_STATE), lambda g, c: (c, g, 0, 0))],
        scratch_shapes=[pltpu.VMEM((hg, SSM_HDIM, SSM_STATE), F32)],
        compiler_params=_cparams("parallel", "arbitrary"),
    )(xs_hm, dtraw_t, dt_bias, a_log, dskip, xbc, xbc)


def ssd_bwd(xs_hm, dtraw_t, dt_bias, a_log, dskip, xbc, prev_all, dy_hm):
    hg = HEADS_PER_GROUP
    last = N_CHUNKS - 1

    def body(x_ref, dt_ref, dtb_ref, al_ref, dk_ref, bm_ref, cm_ref, prev_ref, dy_ref,
             dx_ref, ddt_ref, ddtb_ref, dal_ref, ddk_ref, dbm_ref, dcm_ref, dstate_ref):
        @pl.when(pl.program_id(1) == 0)
        def _():
            dstate_ref[...] = jnp.zeros_like(dstate_ref)
            ddtb_ref[...] = jnp.zeros_like(ddtb_ref)
            dal_ref[...] = jnp.zeros_like(dal_ref)
            ddk_ref[...] = jnp.zeros_like(ddk_ref)

        _, vjp = jax.vjp(ssd_heads, x_ref[...], dt_ref[...], dtb_ref[...], al_ref[...], dk_ref[...], bm_ref[...],
                         cm_ref[...], prev_ref[0])
        dx, ddt, ddtb, dal, ddk, dbm, dcm, dprev = vjp((dy_ref[...], dstate_ref[...]))
        dx_ref[...] = dx
        ddt_ref[...] = ddt
        ddtb_ref[...] += ddtb
        dal_ref[...] += dal
        ddk_ref[...] += ddk
        dbm_ref[...] = dbm
        dcm_ref[...] = dcm
        dstate_ref[...] = dprev

    hp = pl.BlockSpec((hg, 1, 1), lambda g, c: (g, 0, 0))
    xspec = pl.BlockSpec((hg, CHUNK, SSM_HDIM), lambda g, c: (g, last - c, 0))
    tspec = pl.BlockSpec((hg, 1, CHUNK), lambda g, c: (g, 0, last - c))
    gspec = pl.BlockSpec((CHUNK, SSM_STATE), lambda g, c: (last - c, g))
    dtraw_t, dt_bias, a_log, dskip = [a.reshape(SSM_HEADS, 1, -1) for a in (dtraw_t, dt_bias, a_log, dskip)]
    res = pl.pallas_call(
        body, name="ssd_bwd",
        out_shape=[jax.ShapeDtypeStruct((SSM_HEADS, S, SSM_HDIM), F32), jax.ShapeDtypeStruct((SSM_HEADS, 1, S), F32),
                   jax.ShapeDtypeStruct((SSM_HEADS, 1, 1), F32), jax.ShapeDtypeStruct((SSM_HEADS, 1, 1), F32),
                   jax.ShapeDtypeStruct((SSM_HEADS, 1, 1), F32),
                   jax.ShapeDtypeStruct((S, SSM_GROUPS * SSM_STATE), F32),
                   jax.ShapeDtypeStruct((S, SSM_GROUPS * SSM_STATE), F32)],
        grid=(SSM_GROUPS, N_CHUNKS),
        in_specs=[xspec, tspec, hp, hp, hp,
                  pl.BlockSpec((CHUNK, SSM_STATE), lambda g, c: (last - c, BM_COL0 + g)),
                  pl.BlockSpec((CHUNK, SSM_STATE), lambda g, c: (last - c, CM_COL0 + g)),
                  pl.BlockSpec((1, hg, SSM_HDIM, SSM_STATE), lambda g, c: (last - c, g, 0, 0)), xspec],
        out_specs=[xspec, tspec, hp, hp, hp, gspec, gspec],
        scratch_shapes=[pltpu.VMEM((hg, SSM_HDIM, SSM_STATE), F32)],
        compiler_params=_cparams("parallel", "arbitrary"),
    )(xs_hm, dtraw_t, dt_bias, a_log, dskip, xbc, xbc, prev_all, dy_hm)
    return [res[0]] + [r.reshape(SSM_HEADS, -1) for r in res[1:5]] + list(res[5:])


ATT_HB = 8


def _bdot(a, b, ca, cb):
    return lax.dot_general(a.astype(BF16), b.astype(BF16), (((ca,), (cb,)), ((0,), (0,))), preferred_element_type=F32)


@jax.custom_vjp
def bmm(a, b):
    return _bdot(a, b, 2, 1)


def _bmm_fwd(a, b):
    return _bdot(a, b, 2, 1), (a, b)


def _bmm_bwd(res, g):
    a, b = res
    return _bdot(g, b, 2, 2).astype(a.dtype), _bdot(a, g, 1, 1).astype(b.dtype)


bmm.defvjp(_bmm_fwd, _bmm_bwd)


@jax.custom_vjp
def bmm_nt(a, b):
    return _bdot(a, b, 2, 2)


def _bmm_nt_fwd(a, b):
    return _bdot(a, b, 2, 2), (a, b)


def _bmm_nt_bwd(res, g):
    a, b = res
    return _bdot(g, b, 2, 1).astype(a.dtype), _bdot(g, a, 1, 1).astype(b.dtype)


bmm_nt.defvjp(_bmm_nt_fwd, _bmm_nt_bwd)


@jax.custom_vjp
def bmm_tn(a, b):
    return _bdot(a, b, 1, 1)


def _bmm_tn_fwd(a, b):
    return _bdot(a, b, 1, 1), (a, b)


def _bmm_tn_bwd(res, g):
    a, b = res
    return _bdot(b, g, 2, 2).astype(a.dtype), _bdot(a, g, 2, 1).astype(b.dtype)


bmm_tn.defvjp(_bmm_tn_fwd, _bmm_tn_bwd)


def att_heads(q, kp, kc, vp, vc, bias_p, bias_c, has_prev):
    h, b, dh = q.shape
    i = lax.broadcasted_iota(jnp.int32, (1, b, b), 1)
    j = lax.broadcasted_iota(jnp.int32, (1, b, b), 2)
    scale = dh ** -0.5
    sp = jnp.where(jnp.logical_and(j >= i, has_prev), bmm_nt(q, kp) * scale + bias_p, -1e30)
    sc = jnp.where(j <= i, bmm_nt(q, kc) * scale + bias_c, -1e30)
    m = lax.stop_gradient(jnp.maximum(jnp.max(sp, axis=2, keepdims=True), jnp.max(sc, axis=2, keepdims=True)))
    pp, pc = jnp.exp(sp - m), jnp.exp(sc - m)
    l = jnp.sum(pp, axis=2, keepdims=True) + jnp.sum(pc, axis=2, keepdims=True)
    o = bmm(pp / l, vp) + bmm(pc / l, vc)
    return o, jnp.broadcast_to(m + jnp.log(l), (h, b, dh))


def _att_specs(nb):
    hb, blk = ATT_HB, ATT_BLK
    cur = pl.BlockSpec((hb, blk, ATT_HDIM), lambda h, b: (h, b, 0))
    prv = pl.BlockSpec((hb, blk, ATT_HDIM), lambda h, b: (h, jnp.maximum(b - 1, 0), 0))
    bias = pl.BlockSpec((hb, 2, blk, blk), lambda h, b: (h, 0, 0, 0))
    return cur, prv, bias


def att_fwd(q, k, v, bias, nb, *, name):
    cur, prv, bspec = _att_specs(nb)

    def body(q_ref, kp_ref, kc_ref, vp_ref, vc_ref, b_ref, o_ref, l_ref):
        has_prev = (pl.program_id(1) % nb) != 0
        o, lse = att_heads(q_ref[...], kp_ref[...], kc_ref[...], vp_ref[...], vc_ref[...], b_ref[:, 0], b_ref[:, 1],
                           has_prev)
        o_ref[...] = o
        l_ref[...] = lse

    shp = jax.ShapeDtypeStruct((ATT_HEADS, S, ATT_HDIM), F32)
    return pl.pallas_call(
        body, name=name, out_shape=[shp, shp],
        grid=(ATT_HEADS // ATT_HB, S // ATT_BLK),
        in_specs=[cur, prv, cur, prv, cur, bspec],
        out_specs=[cur, cur],
        compiler_params=_cparams("parallel", "parallel"),
    )(q, k, k, v, v, bias)


def att_bwd(q, k, v, bias, do, dlse, nb, *, name):
    cur, prv, bspec = _att_specs(nb)

    def body(q_ref, kp_ref, kc_ref, vp_ref, vc_ref, b_ref, do_ref, dl_ref,
             dq_ref, dkc_ref, dkp_ref, dvc_ref, dvp_ref, db_ref):
        has_prev = (pl.program_id(1) % nb) != 0

        @pl.when(pl.program_id(1) == 0)
        def _():
            db_ref[...] = jnp.zeros_like(db_ref)

        ins = _f32([q_ref[...], kp_ref[...], kc_ref[...], vp_ref[...], vc_ref[...]]) + [b_ref[:, 0], b_ref[:, 1]]
        _, vjp = jax.vjp(functools.partial(att_heads, has_prev=has_prev), *ins)
        dq, dkp, dkc, dvp, dvc, dbp, dbc = vjp((do_ref[...], dl_ref[...]))
        dq_ref[...] = dq
        dkc_ref[...] = dkc
        dkp_ref[...] = dkp
        dvc_ref[...] = dvc
        dvp_ref[...] = dvp
        db_ref[:, 0] += dbp
        db_ref[:, 1] += dbc

    shp = jax.ShapeDtypeStruct((ATT_HEADS, S, ATT_HDIM), F32)
    return pl.pallas_call(
        body, name=name,
        out_shape=[shp] * 5 + [jax.ShapeDtypeStruct((ATT_HEADS, 2, ATT_BLK, ATT_BLK), F32)],
        grid=(ATT_HEADS // ATT_HB, S // ATT_BLK),
        in_specs=[cur, prv, cur, prv, cur, bspec, cur, cur],
        out_specs=[cur] * 5 + [bspec],
        compiler_params=_cparams("parallel", "arbitrary"),
    )(q, k, k, v, v, bias, do, dlse)


def shift_add(cur, prev, nb, *, name):
    n_blocks = S // ATT_BLK

    def body(c_ref, p_ref, o_ref):
        nxt = pl.program_id(0) + 1
        keep = jnp.where((nxt % nb) != 0, 1.0, 0.0)
        o_ref[...] = c_ref[...] + keep * p_ref[...]

    return pl.pallas_call(
        body, name=name, out_shape=jax.ShapeDtypeStruct(cur.shape, F32),
        grid=(n_blocks,),
        in_specs=[pl.BlockSpec((ATT_HEADS, ATT_BLK, ATT_HDIM), lambda b: (0, b, 0)),
                  pl.BlockSpec((ATT_HEADS, ATT_BLK, ATT_HDIM), lambda b: (0, jnp.minimum(b + 1, n_blocks - 1), 0))],
        out_specs=pl.BlockSpec((ATT_HEADS, ATT_BLK, ATT_HDIM), lambda b: (0, b, 0)),
        compiler_params=_cparams("parallel"),
    )(cur, prev)


ATT_PAIRS = ATT_HEADS // 2
PAIR_W = 2 * ATT_HDIM


def att_pairs(q, kp, kc, vp, vc, bias, has_prev):
    t, b, w = q.shape
    i = lax.broadcasted_iota(jnp.int32, (1, b, b), 1)
    j = lax.broadcasted_iota(jnp.int32, (1, b, b), 2)
    first = lax.broadcasted_iota(jnp.int32, (1, 1, w), 2) < ATT_HDIM
    scale = ATT_HDIM ** -0.5
    outs, lses = [], []
    for ab in range(2):
        qh = jnp.where(first if ab == 0 else jnp.logical_not(first), q, 0.0)
        sp = jnp.where(jnp.logical_and(j >= i, has_prev), bmm_nt(qh, kp) * scale + bias[:, ab, 0], -1e30)
        sc = jnp.where(j <= i, bmm_nt(qh, kc) * scale + bias[:, ab, 1], -1e30)
        m = lax.stop_gradient(jnp.maximum(jnp.max(sp, axis=2, keepdims=True), jnp.max(sc, axis=2, keepdims=True)))
        pp, pc = jnp.exp(sp - m), jnp.exp(sc - m)
        l = jnp.sum(pp, axis=2, keepdims=True) + jnp.sum(pc, axis=2, keepdims=True)
        outs.append(bmm(pp / l, vp) + bmm(pc / l, vc))
        lses.append(jnp.broadcast_to(m + jnp.log(l), (t, b, w)))
    return jnp.where(first, outs[0], outs[1]), jnp.where(first, lses[0], lses[1])


def _pair_tiles(ref):
    return jnp.stack([ref[:, PAIR_W * t:PAIR_W * (t + 1)] for t in range(ATT_PAIRS)])


def _store_pair_tiles(ref, val):
    for t in range(ATT_PAIRS):
        ref[:, PAIR_W * t:PAIR_W * (t + 1)] = val[t].astype(ref.dtype)


def pair_bias(bias):
    return bias.reshape(ATT_PAIRS, 2, 2, ATT_BLK, ATT_BLK)


def att2_fwd(q, k, v, bias, nb, cols, *, name):
    n_blocks = S // ATT_BLK
    qc, kc, vc = cols

    def body(q_ref, k_ref, v_ref, b_ref, o_ref, l_ref, kprev, vprev):
        blk = pl.program_id(0)

        @pl.when(blk == 0)
        def _():
            kprev[...] = jnp.zeros_like(kprev)
            vprev[...] = jnp.zeros_like(vprev)

        k3, v3 = _pair_tiles(k_ref), _pair_tiles(v_ref)
        o, lse = att_pairs(_pair_tiles(q_ref), kprev[...], k3, vprev[...], v3, b_ref[...], (blk % nb) != 0)
        _store_pair_tiles(o_ref, o)
        _store_pair_tiles(l_ref, lse)
        kprev[...] = k3
        vprev[...] = v3

    def spec(c):
        return pl.BlockSpec((ATT_BLK, D), lambda b: (b, c))

    shp = jax.ShapeDtypeStruct((S, D), F32)
    return pl.pallas_call(
        body, name=name, out_shape=[shp, shp], grid=(n_blocks,),
        in_specs=[spec(qc), spec(kc), spec(vc), pl.BlockSpec(bias.shape, lambda b: (0, 0, 0, 0, 0))],
        out_specs=[spec(0), spec(0)],
        scratch_shapes=[pltpu.VMEM((ATT_PAIRS, ATT_BLK, PAIR_W), BF16), pltpu.VMEM((ATT_PAIRS, ATT_BLK, PAIR_W), BF16)],
        compiler_params=_cparams("arbitrary"),
    )(q, k, v, bias)


def att2_bwd(q, k, v, bias, do, dlse, nb, cols, *, name):
    n_blocks = S // ATT_BLK
    qc, kc, vc = cols

    def body(q_ref, k_ref, v_ref, b_ref, do_ref, dl_ref, dq_ref, dk_ref, dv_ref, db_ref, kprev, vprev, dk_own, dv_own):
        blk = pl.program_id(0)

        @pl.when(blk == 0)
        def _():
            for r in (kprev, vprev, dk_own, dv_own, db_ref):
                r[...] = jnp.zeros_like(r)

        @pl.when(blk < n_blocks)
        def _():
            k3, v3 = _pair_tiles(k_ref), _pair_tiles(v_ref)
            ins = _f32([_pair_tiles(q_ref), kprev[...], k3, vprev[...], v3]) + [b_ref[...]]
            _, vjp = jax.vjp(functools.partial(att_pairs, has_prev=(blk % nb) != 0), *ins)
            dq, dkp, dkc, dvp, dvc, db = vjp((_pair_tiles(do_ref), _pair_tiles(dl_ref)))
            _store_pair_tiles(dq_ref, dq)
            _store_pair_tiles(dk_ref, dk_own[...] + dkp)
            _store_pair_tiles(dv_ref, dv_own[...] + dvp)
            dk_own[...] = dkc
            dv_own[...] = dvc
            db_ref[...] += db
            kprev[...] = k3
            vprev[...] = v3

        @pl.when(blk == n_blocks)
        def _():
            _store_pair_tiles(dk_ref, dk_own[...])
            _store_pair_tiles(dv_ref, dv_own[...])

    def spec(c):
        return pl.BlockSpec((ATT_BLK, D), lambda b: (jnp.minimum(b, n_blocks - 1), c))

    late = pl.BlockSpec((ATT_BLK, D), lambda b: (jnp.maximum(b - 1, 0), 0))
    bspec = pl.BlockSpec(bias.shape, lambda b: (0, 0, 0, 0, 0))
    tile_f32 = pltpu.VMEM((ATT_PAIRS, ATT_BLK, PAIR_W), F32)
    tile_bf16 = pltpu.VMEM((ATT_PAIRS, ATT_BLK, PAIR_W), BF16)
    return pl.pallas_call(
        body, name=name,
        out_shape=[jax.ShapeDtypeStruct((S, D), BF16), jax.ShapeDtypeStruct((S, D), F32),
                   jax.ShapeDtypeStruct((S, D), F32), jax.ShapeDtypeStruct(bias.shape, F32)],
        grid=(n_blocks + 1,),
        in_specs=[spec(qc), spec(kc), spec(vc), bspec, spec(0), spec(0)],
        out_specs=[spec(0), late, late, bspec],
        scratch_shapes=[tile_bf16, tile_bf16, tile_f32, tile_f32],
        compiler_params=_cparams("arbitrary"),
    )(q, k, v, bias, do, dlse)


def regroup(a, dil, inverse=False):
    if dil == 1:
        return a
    c_dim = a.shape[1]
    shape = (dil, S // dil, c_dim) if inverse else (S // dil, dil, c_dim)
    return jnp.transpose(a.reshape(shape), (1, 0, 2)).reshape(S, c_dim)


def _silu(x):
    return x * jax.nn.sigmoid(x)


def _rms(x):
    return x * lax.rsqrt(jnp.mean(x * x, -1, keepdims=True) + EPS)


def f_normmod(x, g, sc, sh):
    return (_rms(x) * g * (1.0 + sc) + sh,)


def f_resid(x, mix, gate):
    return (x + gate * mix,)


def f_resid_bias(x, mix, gate, b):
    return (x + gate * (mix + b),)


def f_swiglu(hgu):
    return (_silu(hgu[:, :FFN_HIDDEN]) * hgu[:, FFN_HIDDEN:],)


def f_silu(x):
    return (_silu(x),)


def f_gated_norm(y, z, g):
    return (_rms(y * _silu(z)) * g,)


def f_glu(y, b):
    y = y + b
    return (y[:, :D] * jax.nn.sigmoid(y[:, D:]),)


def f_ln_silu(u, g, b):
    mu = jnp.mean(u, -1, keepdims=True)
    var = jnp.mean(jnp.square(u - mu), -1, keepdims=True)
    return (_silu((u - mu) * lax.rsqrt(var + EPS) * g + b),)


def f_combine(o1, o2, o3, l1, l2, l3):
    m = lax.stop_gradient(jnp.maximum(jnp.maximum(l1, l2), l3))
    e1, e2, e3 = jnp.exp(l1 - m), jnp.exp(l2 - m), jnp.exp(l3 - m)
    return ((e1 * o1 + e2 * o2 + e3 * o3) / (e1 + e2 + e3),)


def f_head(x, tgt, g):
    return (0.5 * jnp.mean(jnp.square(_rms(x) * g - tgt), -1, keepdims=True),)


def f_sum3(a, b, c):
    return (a + b + c,)


def f_sum4(a, b, c, d):
    return (a + b + c + d,)


def f_add(a, b):
    return (a + b,)


def f_adamw(w, g, m, v):
    m = ADAM_B1 * m + (1.0 - ADAM_B1) * g
    v = ADAM_B2 * v + (1.0 - ADAM_B2) * jnp.square(g)
    m_hat = m / (1.0 - ADAM_B1 ** ADAM_STEP)
    v_hat = v / (1.0 - ADAM_B2 ** ADAM_STEP)
    return -ADAM_LR * (m_hat / (jnp.sqrt(v_hat) + ADAM_EPS) + ADAM_WD * w), m, v


def _rows_tile(r, cap=256):
    return _pick(r, cap, mult=8)


def adamw(w, g, m, v, *, name):
    shape = w.shape
    c_dim = shape[-1] if len(shape) > 1 else shape[0]
    flat = [a.reshape(-1, c_dim) for a in (w, g, m, v)]
    res = rowmap(f_adamw, flat, [], [F32] * 3, name=name, tr=_rows_tile(flat[0].shape[0], cap=128))
    return [r.reshape(shape) for r in res]


def _t5_bucket(dist):
    max_exact = REL_BUCKETS // 2
    n = jnp.maximum(dist, 1).astype(F32)
    large = max_exact + jnp.log(n / max_exact) / math.log(REL_MAX_DIST / max_exact) * (REL_BUCKETS - max_exact)
    large = jnp.minimum(large.astype(jnp.int32), REL_BUCKETS - 1)
    return jnp.where(dist < max_exact, dist, large)


def _att_buckets(dil):
    i = jnp.arange(ATT_BLK)[:, None]
    j = jnp.arange(2 * ATT_BLK)[None, :]
    bkt = _t5_bucket(jnp.maximum(ATT_BLK + i - j, 0) * dil)
    return jnp.transpose(bkt.reshape(ATT_BLK, 2, ATT_BLK), (1, 0, 2))


def att_bias(rel_table, p, dil):
    tab = rel_table[:, p * ATT_HEADS:(p + 1) * ATT_HEADS]
    onehot = (jnp.arange(REL_BUCKETS)[:, None] == _att_buckets(dil).reshape(1, -1)).astype(F32)
    bias = lax.dot_general(tab, onehot, (((0,), (0,)), ((), ())), precision=lax.Precision.HIGHEST)
    return bias.reshape(ATT_HEADS, 2, ATT_BLK, ATT_BLK)


def att_bias_grad(dbias, dil, *, name):
    onehot = (_att_buckets(dil).reshape(-1, 1) == jnp.arange(LANES)[None, :]).astype(BF16)
    dtab = matmul(dbias.reshape(ATT_HEADS, -1), onehot, mode="nn", out_dtype=F32, name=name, tk_cap=2048)
    return dtab[:, :REL_BUCKETS].T


def to_heads(a, n_heads, dil=1):
    hd = a.shape[1] // n_heads
    return jnp.transpose(a.reshape(S // dil, dil, n_heads, hd), (2, 1, 0, 3)).reshape(n_heads, S, hd)


def from_heads(a, dil=1):
    n_heads, _, hd = a.shape
    return jnp.transpose(a.reshape(n_heads, dil, S // dil, hd), (2, 1, 0, 3)).reshape(S, n_heads * hd)


def regroup_heads(a, dil, inverse=False):
    n_heads, _, hd = a.shape
    if dil == 1:
        return a
    if inverse:
        return jnp.transpose(a.reshape(n_heads, dil, S // dil, hd), (0, 2, 1, 3)).reshape(n_heads, S, hd)
    return jnp.transpose(a.reshape(n_heads, S // dil, dil, hd), (0, 2, 1, 3)).reshape(n_heads, S, hd)


HY_Z, HY_XBC, HY_DT, HY_Q, HY_K, HY_V = 2048, 3072, 32, 3072, 1024, 1024
HY_IN = HY_Z + HY_XBC + HY_DT + HY_Q + HY_K + HY_V
OFF_Z, OFF_XBC, OFF_Q, OFF_KV, OFF_DT = 0, 2048, 5120, 8192, 10240
HY_CAT = OFF_DT + LANES
DT_PAD = LANES


def hy_to_cat(w):
    z, xbc, dt, qkv = w[:2048], w[2048:5120], w[5120:5152], w[5152:]
    return jnp.concatenate([z, xbc, qkv, dt, jnp.zeros((DT_PAD - HY_DT,) + w.shape[1:], w.dtype)], axis=0)


def hy_from_cat(w, axis=0):
    part = lambda a, b: lax.slice_in_dim(w, a, b, axis=axis)
    return jnp.concatenate([part(0, 5120), part(OFF_DT, OFF_DT + HY_DT), part(5120, OFF_DT)], axis=axis)


def device_step(x, tgt, mods, wts, sp):
    g = {}
    dmods = [[None] * 6 for _ in range(2)]

    def normmod(xi, gain, sc, sh, nm):
        return rowmap(f_normmod, [xi], [gain, sc, sh], [BF16], name=nm)[0]

    def ffn_fwd(xi, i, gate, nm):
        h = normmod(xi, sp["norm_ffn_g"][i], mods[i][4], mods[i][3], nm + "_norm")
        hgu = matmul(h, wts["gu_t"][i], mode="nt", out_dtype=F32, name=nm + "_gu")
        act = rowmap(f_swiglu, [hgu], [], [BF16], name=nm + "_act", tr=128)[0]
        out = matmul(act, wts["down"][i], mode="nn", out_dtype=F32, name=nm + "_down")
        xo = rowmap(f_resid, [xi, out], [gate], [F32], name=nm + "_res")[0]
        return xo, (h, hgu, act, out)

    def ffn_bwd(dres, xi, i, saved, nm):
        h, hgu, act, out = saved
        (dout,), (dgate,), _ = rowmap_bwd(f_resid, [xi, out], [mods[i][5]], [dres], name=nm + "_res_b",
                                          row_grad=[False, True], row_dtypes=[BF16])
        dmods[i][5] = dgate
        dact = matmul(dout, wts["down"][i], mode="nt", out_dtype=F32, name=nm + "_down_dx")
        g[f"down{i}"] = matmul(transpose(dout, name=nm + "_dout_t"), act, mode="nn", out_dtype=F32, name=nm + "_down_dw")
        (dhgu,), _, _ = rowmap_bwd(f_swiglu, [hgu], [], [dact], name=nm + "_act_b", row_grad=[True],
                                   row_dtypes=[BF16], tr=128)
        g[f"gu_t{i}"] = matmul(transpose(h, name=nm + "_h_t"), dhgu, mode="nn", out_dtype=F32, name=nm + "_gu_dw")
        dh = matmul(dhgu, wts["gu_t"][i], mode="nn", out_dtype=F32, name=nm + "_gu_dx")
        (dres,), (dg_, dsc, dsh), _ = rowmap_bwd(f_normmod, [xi], [sp["norm_ffn_g"][i], mods[i][4], mods[i][3]], [dh],
                                                 name=nm + "_norm_b", row_grad=[True], row_add=[dres])
        g[f"norm_ffn_g{i}"] = dg_
        dmods[i][4], dmods[i][3] = dsc, dsh
        return dres

    h0 = normmod(x, sp["norm_mix_g"][0], mods[0][1], mods[0][0], "l0_norm")
    w_in = wts["hy_in_t"]
    z = matmul(h0, w_in, mode="nt", out_dtype=F32, name="hy_z", n=HY_Z, b_off=OFF_Z)
    xbc_raw = matmul(h0, w_in, mode="nt", out_dtype=F32, name="hy_xbc", n=HY_XBC, b_off=OFF_XBC)
    q = matmul(h0, w_in, mode="nt", out_dtype=BF16, name="hy_q", n=HY_Q, b_off=OFF_Q)
    kv = matmul(h0, w_in, mode="nt", out_dtype=BF16, name="hy_kv", n=HY_K + HY_V, b_off=OFF_KV)
    dtr = matmul(h0, w_in, mode="nt", out_dtype=F32, name="hy_dt", n=DT_PAD, b_off=OFF_DT)
    xbc_pre = conv_fwd(xbc_raw, sp["hy_conv_w"], sp["hy_conv_b"], name="hy_conv")
    xbc = rowmap(f_silu, [xbc_pre], [], [F32], name="hy_conv_act", tr=128)[0]
    xs_hm = to_heads(xbc[:, :SSM_INNER], SSM_HEADS)
    dtraw_t = dtr[:, :HY_DT].T
    y_hm, prev_all = ssd_fwd(xs_hm, dtraw_t, sp["hy_dt_bias"], sp["hy_a_log"], sp["hy_d_skip"], xbc)
    y = from_heads(y_hm)
    ysn = rowmap(f_gated_norm, [y, z], [sp["hy_ssm_norm_g"]], [BF16], name="hy_gnorm", tr=128)[0]
    att_in, att_o, att_l = [], [], []
    for p, (win, dil) in enumerate(ATT_PATTERNS):
        if dil == 1:
            qa, ka, va, cols = q, kv, kv, (p, 0, 1)
        else:
            qa, ka, cols = regroup(q[:, p * D:(p + 1) * D], dil), regroup(kv, dil), (0, 0, 1)
            va = ka
        bias = pair_bias(att_bias(sp["rel_table"], p, dil))
        nb = S // dil // ATT_BLK
        o, lse = att2_fwd(qa, ka, va, bias, nb, cols, name=f"att_fwd{p}")
        att_in.append((qa, ka, va, bias, nb, cols))
        att_o.append(regroup(o, dil, inverse=True))
        att_l.append(regroup(lse, dil, inverse=True))
    att = rowmap(f_combine, att_o + att_l, [], [BF16], name="att_combine", tr=128)[0]
    cat = jnp.concatenate([ysn, att], axis=-1)
    mix0 = matmul(cat, wts["hy_out"], mode="nn", out_dtype=F32, name="hy_out")
    x1 = rowmap(f_resid, [x, mix0], [mods[0][2]], [F32], name="l0_res")[0]
    x2, ffn0 = ffn_fwd(x1, 0, mods[0][5], "ffn0")

    h1 = normmod(x2, sp["norm_mix_g"][1], mods[1][1], mods[1][0], "l1_norm")
    p1 = matmul(h1, wts["pw1_t"], mode="nt", out_dtype=F32, name="cv_pw1")
    u = rowmap(f_glu, [p1], [sp["cv_b_pw1"]], [F32], name="cv_glu")[0]
    uc = conv_fwd(u, sp["cv_w_dw"], sp["cv_b_dw"], name="cv_conv")
    ul = rowmap(f_ln_silu, [uc], [sp["cv_ln_g"], sp["cv_ln_b"]], [BF16], name="cv_ln")[0]
    mix1 = matmul(ul, wts["pw2"], mode="nn", out_dtype=F32, name="cv_pw2")
    x3 = rowmap(f_resid_bias, [x2, mix1], [mods[1][2], sp["cv_b_pw2"]], [F32], name="l1_res")[0]
    x4, ffn1 = ffn_fwd(x3, 1, mods[1][5], "ffn1")

    ones = jnp.ones((S, 1), F32)
    (dres,), (dfinal,), (loss_rows,) = rowmap_bwd(f_head, [x4, tgt], [sp["final_norm_g"]], [ones], name="head",
                                                  row_grad=[True, False], emit=(0,))
    g["final_norm_g"] = dfinal

    dres = ffn_bwd(dres, x3, 1, ffn1, "ffn1")
    (dmix1,), (dg1, db2), _ = rowmap_bwd(f_resid_bias, [x2, mix1], [mods[1][2], sp["cv_b_pw2"]], [dres], name="l1_res_b",
                                         row_grad=[False, True], row_dtypes=[BF16])
    dmods[1][2] = dg1
    g["cv_b_pw2"] = db2
    dul = matmul(dmix1, wts["pw2"], mode="nt", out_dtype=F32, name="cv_pw2_dx")
    g["pw2"] = matmul(transpose(dmix1, name="cv_dmix_t"), ul, mode="nn", out_dtype=F32, name="cv_pw2_dw")
    (duc,), (g["cv_ln_g"], g["cv_ln_b"]), _ = rowmap_bwd(f_ln_silu, [uc], [sp["cv_ln_g"], sp["cv_ln_b"]], [dul],
                                                         name="cv_ln_b", row_grad=[True])
    du, g["cv_w_dw"], g["cv_b_dw"] = conv_bwd(u, sp["cv_w_dw"], duc, name="cv_conv_b")
    (dp1,), (g["cv_b_pw1"],), _ = rowmap_bwd(f_glu, [p1], [sp["cv_b_pw1"]], [du], name="cv_glu_b", row_grad=[True],
                                             row_dtypes=[BF16])
    g["pw1_t"] = matmul(transpose(h1, name="cv_h_t"), dp1, mode="nn", out_dtype=F32, name="cv_pw1_dw")
    dh1 = matmul(dp1, wts["pw1_t"], mode="nn", out_dtype=F32, name="cv_pw1_dx")
    (dres,), (dg_, dsc, dsh), _ = rowmap_bwd(f_normmod, [x2], [sp["norm_mix_g"][1], mods[1][1], mods[1][0]], [dh1],
                                             name="l1_norm_b", row_grad=[True], row_add=[dres])
    g["norm_mix_g1"] = dg_
    dmods[1][1], dmods[1][0] = dsc, dsh

    dres = ffn_bwd(dres, x1, 0, ffn0, "ffn0")
    (dmix0,), (dg1,), _ = rowmap_bwd(f_resid, [x, mix0], [mods[0][2]], [dres], name="l0_res_b",
                                     row_grad=[False, True], row_dtypes=[BF16])
    dmods[0][2] = dg1
    dysn = matmul(dmix0, wts["hy_out"], mode="nt", out_dtype=F32, name="hy_out_dy", n=SSM_INNER, b_off=0)
    datt = matmul(dmix0, wts["hy_out"], mode="nt", out_dtype=F32, name="hy_out_da", n=D, b_off=SSM_INNER)
    g["hy_out"] = matmul(transpose(dmix0, name="hy_dmix_t"), cat, mode="nn", out_dtype=F32, name="hy_out_dw")
    (dy, dz), (g["hy_ssm_norm_g"],), _ = rowmap_bwd(f_gated_norm, [y, z], [sp["hy_ssm_norm_g"]], [dysn], name="hy_gnorm_b",
                                                    row_grad=[True, True], tr=128)
    dxs_hm, ddtraw_t, g["hy_dt_bias"], g["hy_a_log"], g["hy_d_skip"], dbm, dcm = ssd_bwd(
        xs_hm, dtraw_t, sp["hy_dt_bias"], sp["hy_a_log"], sp["hy_d_skip"], xbc, prev_all, to_heads(dy, SSM_HEADS))
    dxbc = jnp.concatenate([from_heads(dxs_hm), dbm, dcm], axis=-1)
    (dxbc_pre,), _, _ = rowmap_bwd(f_silu, [xbc_pre], [], [dxbc], name="hy_conv_act_b", row_grad=[True], tr=128)
    dxbc_raw, g["hy_conv_w"], g["hy_conv_b"] = conv_bwd(xbc_raw, sp["hy_conv_w"], dxbc_pre, name="hy_conv_b")
    dol, _, _ = rowmap_bwd(f_combine, att_o + att_l, [], [datt], name="att_combine_b", row_grad=[True] * 6, tr=128)
    dqs, dks, dvs, dtabs = [], [], [], []
    for p, (win, dil) in enumerate(ATT_PATTERNS):
        qa, ka, va, bias, nb, cols = att_in[p]
        dq, dkp_, dvp_, dbias = att2_bwd(qa, ka, va, bias, regroup(dol[p], dil), regroup(dol[3 + p], dil), nb, cols,
                                         name=f"att_bwd{p}")
        dqs.append(regroup(dq, dil, inverse=True))
        dks.append(regroup(dkp_, dil, inverse=True))
        dvs.append(regroup(dvp_, dil, inverse=True))
        dtabs.append(att_bias_grad(dbias.reshape(ATT_HEADS, 2, ATT_BLK, ATT_BLK), dil, name=f"att_dtab{p}"))
    g["rel_table"] = jnp.concatenate(dtabs, axis=1)
    dk = rowmap(f_sum3, dks, [], [F32], name="att_dk_sum")[0]
    dv = rowmap(f_sum3, dvs, [], [F32], name="att_dv_sum")[0]
    ddt = jnp.pad(ddtraw_t.T, ((0, 0), (0, DT_PAD - HY_DT)))
    dproj = jnp.concatenate([dz, dxbc_raw] + dqs + [dk, dv, ddt], axis=-1).astype(BF16)
    g["hy_in_t"] = matmul(transpose(h0, name="hy_h_t"), dproj, mode="nn", out_dtype=F32, name="hy_in_dw")
    dh0 = matmul(dproj, w_in, mode="nn", out_dtype=F32, name="hy_in_dx")
    (dres,), (dg_, dsc, dsh), _ = rowmap_bwd(f_normmod, [x], [sp["norm_mix_g"][0], mods[0][1], mods[0][0]], [dh0],
                                             name="l0_norm_b", row_grad=[True], row_add=[dres])
    g["norm_mix_g0"] = dg_
    dmods[0][1], dmods[0][0] = dsc, dsh
    return loss_rows, dres, g, dmods


ANY = pl.BlockSpec(memory_space=pl.ANY)
WHOLE_VMEM = pl.BlockSpec(memory_space=pltpu.VMEM)


def _place():
    return lax.axis_index("x"), lax.axis_index("y"), lax.axis_index("c")


def _other_chips(x, y):
    return [(1 - x, y), (x, 1 - y), (1 - x, 1 - y)]


def allgather_small(v, *, name):
    m_per = v.shape[0]

    def body(x_ref, out_ref, send_sems, recv_sems, local_sem):
        x, y, c = _place()
        me, sibling = (x, y, c), (x, y, 1 - c)
        chips = _other_chips(x, y)

        def rows(px, py, pc):
            return out_ref.at[pl.ds((4 * px + 2 * py + pc) * m_per, m_per), :]

        def copy(k, block, to, src=None):
            return pltpu.make_async_remote_copy(
                src_ref=rows(*block) if src is None else src, dst_ref=rows(*block),
                send_sem=send_sems.at[k], recv_sem=recv_sems.at[k], device_id=to, device_id_type=MESH)

        mine = pltpu.make_async_copy(x_ref, rows(*me), local_sem)
        mine.start()
        first = [copy(0, me, sibling, src=x_ref)]
        first += [copy(1 + j, me, (*chip, c), src=x_ref) for j, chip in enumerate(chips)]
        for cp in first:
            cp.start()
        passed = [copy(4 + j, (*chip, c), sibling) for j, chip in enumerate(chips)]
        for j, chip in enumerate(chips):
            copy(1 + j, (*chip, c), me).wait_recv()
            passed[j].start()
        copy(0, sibling, me).wait_recv()
        for j, chip in enumerate(chips):
            copy(4 + j, (*chip, 1 - c), me).wait_recv()
        for cp in first + passed:
            cp.wait_send()
        mine.wait()

    return pl.pallas_call(
        body, name=name,
        out_shape=jax.ShapeDtypeStruct((N_DEV * m_per, LANES), v.dtype),
        in_specs=[WHOLE_VMEM], out_specs=WHOLE_VMEM,
        scratch_shapes=[pltpu.SemaphoreType.DMA((7,)), pltpu.SemaphoreType.DMA((7,)), pltpu.SemaphoreType.DMA],
    )(v)


def allgather_chips(pack, *, name):
    half_rows = pack.shape[0] // 2

    def body(p_ref, o_ref, send_sems, recv_sems, local_sem):
        x, y, c = _place()
        chips = _other_chips(x, y)
        sibling = (x, y, 1 - c)
        my_half = pl.ds(c * half_rows, half_rows)
        its_half = pl.ds((1 - c) * half_rows, half_rows)
        mine = pltpu.make_async_copy(p_ref, o_ref.at[2 * x + y], local_sem)
        mine.start()
        sends = [pltpu.make_async_remote_copy(
            src_ref=p_ref.at[my_half], dst_ref=o_ref.at[2 * x + y, my_half],
            send_sem=send_sems.at[k], recv_sem=recv_sems.at[k],
            device_id=(cx, cy, c), device_id_type=MESH) for k, (cx, cy) in enumerate(chips)]
        for cp in sends:
            cp.start()
        passed = []
        for k, (cx, cy) in enumerate(chips):
            landed = o_ref.at[2 * cx + cy, my_half]
            pltpu.make_async_remote_copy(
                src_ref=p_ref.at[my_half], dst_ref=landed, send_sem=send_sems.at[k], recv_sem=recv_sems.at[k],
                device_id=(cx, cy, c), device_id_type=MESH).wait_recv()
            cp = pltpu.make_async_remote_copy(
                src_ref=landed, dst_ref=landed, send_sem=send_sems.at[3 + k], recv_sem=recv_sems.at[3 + k],
                device_id=sibling, device_id_type=MESH)
            cp.start()
            passed.append(cp)
        for k, (cx, cy) in enumerate(chips):
            from_sibling = o_ref.at[2 * cx + cy, its_half]
            pltpu.make_async_remote_copy(
                src_ref=from_sibling, dst_ref=from_sibling, send_sem=send_sems.at[3 + k], recv_sem=recv_sems.at[3 + k],
                device_id=sibling, device_id_type=MESH).wait_recv()
        for cp in sends + passed:
            cp.wait_send()
        mine.wait()

    return pl.pallas_call(
        body, name=name,
        out_shape=jax.ShapeDtypeStruct((N_CHIPS,) + pack.shape, pack.dtype),
        in_specs=[ANY], out_specs=ANY,
        scratch_shapes=[pltpu.SemaphoreType.DMA((6,)), pltpu.SemaphoreType.DMA((6,)), pltpu.SemaphoreType.DMA],
    )(pack)


def swap_halves(gpack, *, name):
    half_rows = gpack.shape[1] // 2

    def body(g_ref, r_ref, send_sems, recv_sems):
        x, y, c = _place()
        its_half = pl.ds((1 - c) * half_rows, half_rows)
        copies = [pltpu.make_async_remote_copy(
            src_ref=g_ref.at[s, its_half], dst_ref=r_ref.at[s], send_sem=send_sems.at[s], recv_sem=recv_sems.at[s],
            device_id=(x, y, 1 - c), device_id_type=MESH) for s in range(N_CHIPS)]
        for cp in copies:
            cp.start()
        for cp in copies:
            cp.wait()

    return pl.pallas_call(
        body, name=name,
        out_shape=jax.ShapeDtypeStruct((N_CHIPS, half_rows) + gpack.shape[2:], gpack.dtype),
        in_specs=[ANY], out_specs=ANY,
        scratch_shapes=[pltpu.SemaphoreType.DMA((N_CHIPS,)), pltpu.SemaphoreType.DMA((N_CHIPS,))],
    )(gpack)


def scatter_chips(gpack, *, name):
    def body(g_ref, own_ref, recv_ref, send_sems, recv_sems, local_sem):
        x, y, c = _place()
        chips = _other_chips(x, y)
        mine = pltpu.make_async_copy(g_ref.at[2 * x + y], own_ref, local_sem)
        mine.start()
        sends = [pltpu.make_async_remote_copy(
            src_ref=g_ref.at[2 * cx + cy], dst_ref=recv_ref.at[k], send_sem=send_sems.at[k], recv_sem=recv_sems.at[k],
            device_id=(cx, cy, c), device_id_type=MESH) for k, (cx, cy) in enumerate(chips)]
        for cp in sends:
            cp.start()
        for cp in sends:
            cp.wait_recv()
        for cp in sends:
            cp.wait_send()
        mine.wait()

    slot = jax.ShapeDtypeStruct(gpack.shape[1:], gpack.dtype)
    return pl.pallas_call(
        body, name=name,
        out_shape=[slot, jax.ShapeDtypeStruct((3,) + gpack.shape[1:], gpack.dtype)],
        in_specs=[ANY], out_specs=[ANY, ANY],
        scratch_shapes=[pltpu.SemaphoreType.DMA((3,)), pltpu.SemaphoreType.DMA((3,)), pltpu.SemaphoreType.DMA],
    )(gpack)


def sibling_swap(p, *, name):
    def body(p_ref, r_ref, send_sem, recv_sem):
        x, y, c = _place()
        cp = pltpu.make_async_remote_copy(src_ref=p_ref, dst_ref=r_ref, send_sem=send_sem, recv_sem=recv_sem,
                                          device_id=(x, y, 1 - c), device_id_type=MESH)
        cp.start()
        cp.wait()

    return pl.pallas_call(
        body, name=name, out_shape=jax.ShapeDtypeStruct(p.shape, p.dtype),
        in_specs=[ANY], out_specs=ANY,
        scratch_shapes=[pltpu.SemaphoreType.DMA, pltpu.SemaphoreType.DMA],
    )(p)


def sum_devices(v_all, *, name):
    m_per = v_all.shape[0] // N_DEV

    def body(v_ref, o_ref):
        acc = v_ref[pl.ds(0, m_per), :]
        for d in range(1, N_DEV):
            acc = acc + v_ref[pl.ds(d * m_per, m_per), :]
        o_ref[...] = acc

    return pl.pallas_call(
        body, name=name, out_shape=jax.ShapeDtypeStruct((m_per, LANES), F32),
        in_specs=[WHOLE_VMEM], out_specs=WHOLE_VMEM,
    )(v_all)


WEIGHTS = ['ada_w', 'ada_b', 'norm_mix_g', 'norm_ffn_g', 'hy_w_in', 'hy_conv_w', 'hy_conv_b', 'hy_dt_bias', 'hy_a_log',
           'hy_d_skip', 'hy_ssm_norm_g', 'hy_w_out', 'rel_table', 'cv_w_pw1', 'cv_b_pw1', 'cv_w_dw', 'cv_b_dw', 'cv_ln_g',
           'cv_ln_b', 'cv_w_pw2', 'cv_b_pw2', 'ffn_w_gate', 'ffn_w_up', 'ffn_w_down', 'final_norm_g']
BIG = ('ada_w', 'hy_w_in', 'hy_w_out', 'cv_w_pw1', 'cv_w_pw2', 'ffn_w_gate', 'ffn_w_up', 'ffn_w_down')
SMALL_SHARDED = {'hy_conv_w': (1, 4, 3072), 'cv_b_pw1': (1, 2048), 'cv_w_dw': (1, 31, 1024), 'cv_b_dw': (1, 1024),
                 'cv_ln_g': (1, 1024), 'cv_ln_b': (1, 1024), 'cv_b_pw2': (1, 1024)}
SMALL_GRADS = {'ada_b': (2, 6144), 'norm_mix_g': (2, 1024), 'norm_ffn_g': (2, 1024), 'hy_conv_w': (1, 4, 3072),
               'hy_conv_b': (1, 3072), 'hy_dt_bias': (1, 32), 'hy_a_log': (1, 32), 'hy_d_skip': (1, 32),
               'hy_ssm_norm_g': (1, 2048), 'rel_table': (32, 48), 'cv_b_pw1': (1, 2048), 'cv_w_dw': (1, 31, 1024),
               'cv_b_dw': (1, 1024), 'cv_ln_g': (1, 1024), 'cv_ln_b': (1, 1024), 'cv_b_pw2': (1, 1024),
               'final_norm_g': (1024,), 'loss': (1,)}

PACK_LAYOUT = (('hy_in_t', 2568), ('hy_out', 768), ('pw1_t', 512), ('pw2', 256),
               ('gate_t0', 704), ('up_t0', 704), ('down0', 704), ('gate_t1', 704), ('up_t1', 704), ('down1', 704))
PACK_ROWS = 8448


def _pack_offsets():
    off, out = 0, {}
    for nm, r in PACK_LAYOUT:
        out[nm] = (off, r)
        off += r
    return out


PACK_OFF = _pack_offsets()


def _to_lanes(flat):
    n = flat.shape[0]
    m = -(-n // (8 * LANES)) * 8
    return jnp.pad(flat, (0, m * LANES - n)).reshape(m, LANES)


def _split(flat, shapes):
    out, off = {}, 0
    for nm, shp in shapes.items():
        n = int(np.prod(shp))
        out[nm] = flat[off:off + n].reshape(shp)
        off += n
    return out


def kernel(x, c, ada_w, ada_b, norm_mix_g, norm_ffn_g, hy_w_in, hy_conv_w, hy_conv_b, hy_dt_bias, hy_a_log, hy_d_skip, hy_ssm_norm_g, hy_w_out, rel_table, cv_w_pw1, cv_b_pw1, cv_w_dw, cv_b_dw, cv_ln_g, cv_ln_b, cv_w_pw2, cv_b_pw2, ffn_w_gate, ffn_w_up, ffn_w_down, final_norm_g, loss_target, m_ada_w, m_ada_b, m_norm_mix_g, m_norm_ffn_g, m_hy_w_in, m_hy_conv_w, m_hy_conv_b, m_hy_dt_bias, m_hy_a_log, m_hy_d_skip, m_hy_ssm_norm_g, m_hy_w_out, m_rel_table, m_cv_w_pw1, m_cv_b_pw1, m_cv_w_dw, m_cv_b_dw, m_cv_ln_g, m_cv_ln_b, m_cv_w_pw2, m_cv_b_pw2, m_ffn_w_gate, m_ffn_w_up, m_ffn_w_down, m_final_norm_g, v_ada_w, v_ada_b, v_norm_mix_g, v_norm_ffn_g, v_hy_w_in, v_hy_conv_w, v_hy_conv_b, v_hy_dt_bias, v_hy_a_log, v_hy_d_skip, v_hy_ssm_norm_g, v_hy_w_out, v_rel_table, v_cv_w_pw1, v_cv_b_pw1, v_cv_w_dw, v_cv_b_dw, v_cv_ln_g, v_cv_ln_b, v_cv_w_pw2, v_cv_b_pw2, v_ffn_w_gate, v_ffn_w_up, v_ffn_w_down, v_final_norm_g):
    args = (x, c, ada_w, ada_b, norm_mix_g, norm_ffn_g, hy_w_in, hy_conv_w, hy_conv_b, hy_dt_bias, hy_a_log, hy_d_skip, hy_ssm_norm_g, hy_w_out, rel_table, cv_w_pw1, cv_b_pw1, cv_w_dw, cv_b_dw, cv_ln_g, cv_ln_b, cv_w_pw2, cv_b_pw2, ffn_w_gate, ffn_w_up, ffn_w_down, final_norm_g, loss_target, m_ada_w, m_ada_b, m_norm_mix_g, m_norm_ffn_g, m_hy_w_in, m_hy_conv_w, m_hy_conv_b, m_hy_dt_bias, m_hy_a_log, m_hy_d_skip, m_hy_ssm_norm_g, m_hy_w_out, m_rel_table, m_cv_w_pw1, m_cv_b_pw1, m_cv_w_dw, m_cv_b_dw, m_cv_ln_g, m_cv_ln_b, m_cv_w_pw2, m_cv_b_pw2, m_ffn_w_gate, m_ffn_w_up, m_ffn_w_down, m_final_norm_g, v_ada_w, v_ada_b, v_norm_mix_g, v_norm_ffn_g, v_hy_w_in, v_hy_conv_w, v_hy_conv_b, v_hy_dt_bias, v_hy_a_log, v_hy_d_skip, v_hy_ssm_norm_g, v_hy_w_out, v_rel_table, v_cv_w_pw1, v_cv_b_pw1, v_cv_w_dw, v_cv_b_dw, v_cv_ln_g, v_cv_ln_b, v_cv_w_pw2, v_cv_b_pw2, v_ffn_w_gate, v_ffn_w_up, v_ffn_w_down, v_final_norm_g)
    x_in, c_in = args[0], args[1]
    w = dict(zip(WEIGHTS, args[2:27], strict=True))
    tgt = args[27]
    m_in = dict(zip(WEIGHTS, args[28:53], strict=True))
    v_in = dict(zip(WEIGHTS, args[53:78], strict=True))
    xi, yi, ci = _place()
    chip = 2 * xi + yi
    dev = 2 * chip + ci

    cs = rowmap(f_silu, [c_in.reshape(8, LANES)], [], [F32], name="cond_silu", tr=8)[0]
    cs_all = allgather_small(cs, name="gather_cond").reshape(N_DEV, D)
    cs16 = jnp.pad(cs_all, ((0, 8), (0, 0)))
    modpart = jnp.stack([matmul(cs16, w['ada_w'][i], mode="nn", out_dtype=F32, name=f"ada_fwd{i}")[:N_DEV]
                         for i in range(2)], axis=1)
    shard_names = list(SMALL_SHARDED)
    payload = jnp.concatenate([modpart.reshape(-1)] + [w[nm].reshape(-1) for nm in shard_names])
    got = allgather_small(_to_lanes(payload), name="gather_mod").reshape(N_DEV, -1)[0::2]
    modparts = got[:, :modpart.size].reshape(N_CHIPS, N_DEV, 2, 1536)
    mine = lax.dynamic_index_in_dim(modparts, dev, axis=1, keepdims=False)
    mod = jnp.transpose(mine, (1, 0, 2)).reshape(2, 6 * D) + w['ada_b']
    mods = [[mod[i, j * D:(j + 1) * D].reshape(1, D) for j in range(6)] for i in range(2)]
    sp = {}
    off = modpart.size
    for nm in shard_names:
        shp = w[nm].shape
        n = int(np.prod(shp))
        parts = got[:, off:off + n].reshape((N_CHIPS,) + shp)
        sp[nm] = jnp.concatenate([parts[s] for s in range(N_CHIPS)], axis=-1)
        off += n

    def rows_of(nm, i=None):
        a = w[nm][0 if i is None else i]
        return (a.T if nm in ('hy_w_in', 'cv_w_pw1', 'ffn_w_gate', 'ffn_w_up') else a).astype(BF16)

    pieces = [rows_of('hy_w_in'), rows_of('hy_w_out'), rows_of('cv_w_pw1'), rows_of('cv_w_pw2')]
    for i in range(2):
        pieces += [rows_of('ffn_w_gate', i), rows_of('ffn_w_up', i), rows_of('ffn_w_down', i)]
    n_rows = sum(p.shape[0] for p in pieces)
    pack = jnp.concatenate(pieces + [jnp.zeros((PACK_ROWS - n_rows, D), BF16)], axis=0)
    full = allgather_chips(pack, name="gather_weights")

    def whole(nm):
        o, r = PACK_OFF[nm]
        return full[:, o:o + r].reshape(N_CHIPS * r, D)

    wts = {"hy_in_t": hy_to_cat(whole('hy_in_t')), "hy_out": whole('hy_out'), "pw1_t": whole('pw1_t'), "pw2": whole('pw2'),
           "gu_t": [jnp.concatenate([whole(f'gate_t{i}'), whole(f'up_t{i}')], axis=0) for i in range(2)],
           "down": [whole(f'down{i}') for i in range(2)]}

    sp = {"norm_mix_g": [w['norm_mix_g'][i].reshape(1, D) for i in range(2)],
          "norm_ffn_g": [w['norm_ffn_g'][i].reshape(1, D) for i in range(2)],
          "hy_conv_w": sp['hy_conv_w'][0], "hy_conv_b": w['hy_conv_b'],
          "hy_dt_bias": w['hy_dt_bias'].reshape(SSM_HEADS, 1), "hy_a_log": w['hy_a_log'].reshape(SSM_HEADS, 1),
          "hy_d_skip": w['hy_d_skip'].reshape(SSM_HEADS, 1), "hy_ssm_norm_g": w['hy_ssm_norm_g'],
          "rel_table": w['rel_table'], "cv_b_pw1": sp['cv_b_pw1'], "cv_w_dw": sp['cv_w_dw'][0], "cv_b_dw": sp['cv_b_dw'],
          "cv_ln_g": sp['cv_ln_g'], "cv_ln_b": sp['cv_ln_b'], "cv_b_pw2": sp['cv_b_pw2'],
          "final_norm_g": w['final_norm_g'].reshape(1, D)}

    loss_rows, grad_x, g, dmods = device_step(x_in[0], tgt[0], mods, wts, sp)

    dmod = jnp.stack([jnp.concatenate([d.reshape(-1) for d in dmods[i]]) for i in range(2)])
    small = {'ada_b': dmod, 'norm_mix_g': jnp.stack([g[f'norm_mix_g{i}'].reshape(-1) for i in range(2)]),
             'norm_ffn_g': jnp.stack([g[f'norm_ffn_g{i}'].reshape(-1) for i in range(2)]),
             'loss': jnp.sum(loss_rows).reshape(1)}
    for nm in SMALL_GRADS:
        if nm not in small:
            small[nm] = g[nm]
    vec = _to_lanes(jnp.concatenate([small[nm].reshape(-1) for nm in SMALL_GRADS]))
    vec_all = allgather_small(vec, name="gather_small_grads")
    tot = _split(sum_devices(vec_all, name="sum_small_grads").reshape(-1), SMALL_GRADS)
    dmod_all = vec_all.reshape(N_DEV, -1)[:, :2 * 6 * D].reshape(N_DEV, 2, 6 * D)

    def rows_bf16(nm):
        return transpose(g[nm], name="grad_t_" + nm)

    gp = [hy_from_cat(rows_bf16('hy_in_t')), rows_bf16('hy_out'), rows_bf16('pw1_t'), rows_bf16('pw2')]
    for i in range(2):
        gu = rows_bf16(f'gu_t{i}')
        gp += [gu[:FFN_HIDDEN], gu[FFN_HIDDEN:], rows_bf16(f'down{i}')]
    gp = [a.reshape(N_CHIPS, a.shape[0] // N_CHIPS, D) for a in gp]
    gpack = jnp.concatenate(gp + [jnp.zeros((N_CHIPS, PACK_ROWS - n_rows, D), BF16)], axis=1)
    half = PACK_ROWS // 2
    theirs = swap_halves(gpack, name="swap_grad_halves")
    ours = lax.dynamic_slice_in_dim(gpack, ci * half, half, axis=1)
    chip_sum = rowmap(f_add, [ours.reshape(N_CHIPS * half, D), theirs.reshape(N_CHIPS * half, D)], [], [BF16],
                      name="sum_core_grads")[0].reshape(N_CHIPS, half, D)
    own, recv = scatter_chips(chip_sum, name="scatter_grads")
    mine_half = rowmap(f_sum4, [own, recv[0], recv[1], recv[2]], [], [F32], name="sum_chip_grads")[0]
    its_half = sibling_swap(mine_half, name="swap_grads")
    red = jnp.concatenate([jnp.where(ci == 0, mine_half, its_half), jnp.where(ci == 0, its_half, mine_half)], axis=0)

    def shard_grad(nm, i=None):
        key = {'hy_w_in': 'hy_in_t', 'hy_w_out': 'hy_out', 'cv_w_pw1': 'pw1_t', 'cv_w_pw2': 'pw2'}.get(nm)
        if key is None:
            key = {'ffn_w_gate': 'gate_t', 'ffn_w_up': 'up_t', 'ffn_w_down': 'down'}[nm] + str(i)
        o, r = PACK_OFF[key]
        a = red[o:o + r]
        return a.T if key.endswith('_t') or key[:-1].endswith('_t') else a

    grads = {}
    grads['hy_w_in'] = shard_grad('hy_w_in')[None]
    grads['hy_w_out'] = shard_grad('hy_w_out')[None]
    grads['cv_w_pw1'] = shard_grad('cv_w_pw1')[None]
    grads['cv_w_pw2'] = shard_grad('cv_w_pw2')[None]
    for nm in ('ffn_w_gate', 'ffn_w_up', 'ffn_w_down'):
        grads[nm] = jnp.stack([shard_grad(nm, i) for i in range(2)])
    cs16 = jnp.pad(cs_all, ((0, 8), (0, 0)))
    dm_mine = lax.dynamic_slice_in_dim(dmod_all, chip * 1536, 1536, axis=2)
    dm16 = jnp.pad(dm_mine, ((0, 8), (0, 0), (0, 0)))
    grads['ada_w'] = jnp.stack([matmul(cs16, dm16[:, i], mode="tn", out_dtype=F32, name=f"ada_dw{i}") for i in range(2)])
    for nm, shp in SMALL_GRADS.items():
        if nm == 'loss':
            continue
        if nm in SMALL_SHARDED:
            n = w[nm].shape[-1]
            grads[nm] = lax.dynamic_slice_in_dim(tot[nm], chip * n, n, axis=len(shp) - 1)
        else:
            grads[nm] = tot[nm].reshape(w[nm].shape)

    delta, new_m, new_v = {}, {}, {}
    for nm in BIG:
        delta[nm], new_m[nm], new_v[nm] = adamw(w[nm], grads[nm], m_in[nm], v_in[nm], name="adamw_" + nm)
    smalls = [nm for nm in WEIGHTS if nm not in BIG]
    packed = [_to_lanes(jnp.concatenate([d[nm].reshape(-1) for nm in smalls])) for d in (w, grads, m_in, v_in)]
    res = rowmap(f_adamw, packed, [], [F32] * 3, name="adamw_small", tr=_rows_tile(packed[0].shape[0]))
    for d, r in zip((delta, new_m, new_v), res, strict=True):
        d.update(_split(r.reshape(-1), {nm: w[nm].shape for nm in smalls}))

    loss = tot['loss'].reshape(())
    return (loss, grad_x[None], *[grads[nm] for nm in WEIGHTS], *[delta[nm] for nm in WEIGHTS],
            *[new_m[nm] for nm in WEIGHTS], *[new_v[nm] for nm in WEIGHTS])
```

```python
import functools
import math

import jax
import jax.numpy as jnp
import numpy as np
from jax import lax
from jax.experimental import pallas as pl
from jax.experimental.pallas import tpu as pltpu

F32 = jnp.float32
BF16 = jnp.bfloat16
MESH = pl.DeviceIdType.MESH

D = 1024
S = 4096
EPS = 1e-6
SSM_INNER = 2048
SSM_HEADS = 32
SSM_HDIM = 64
SSM_GROUPS = 4
SSM_STATE = 128
SSM_CONVK = 4
SSM_CONV_DIM = 3072
CHUNK = 128
N_CHUNKS = S // CHUNK
ATT_HEADS = 16
ATT_HDIM = 64
ATT_PATTERNS = ((128, 1), (512, 4), (2048, 16))
ATT_BLK = 128
REL_BUCKETS = 32
REL_MAX_DIST = 2048
CONV_WIDTH = 31
FFN_HIDDEN = 2816
N_CHIPS = 4
N_DEV = 8
ADAM_LR, ADAM_B1, ADAM_B2, ADAM_EPS, ADAM_WD, ADAM_STEP = 0.001, 0.9, 0.999, 1e-08, 0.01, 10

VMEM_LIMIT_BYTES = 56 * 1024 * 1024
LANES = 128


def _cparams(*sem):
    return pltpu.CompilerParams(dimension_semantics=sem, vmem_limit_bytes=VMEM_LIMIT_BYTES)


def _pick(n, cap, mult=LANES):
    best = None
    for t in range(mult, min(n, cap) + 1, mult):
        if n % t == 0:
            best = t
    return best or n


def _dot(a, b, ca, cb):
    return lax.dot_general(a.astype(BF16), b.astype(BF16), (((ca,), (cb,)), ((), ())), preferred_element_type=F32)


@jax.custom_vjp
def mm(a, b):
    return _dot(a, b, 1, 0)


def _mm_fwd(a, b):
    return _dot(a, b, 1, 0), (a, b)


def _mm_bwd(res, g):
    a, b = res
    return _dot(g, b, 1, 1).astype(a.dtype), _dot(a, g, 0, 0).astype(b.dtype)


mm.defvjp(_mm_fwd, _mm_bwd)


@jax.custom_vjp
def mm_nt(a, b):
    return _dot(a, b, 1, 1)


def _mm_nt_fwd(a, b):
    return _dot(a, b, 1, 1), (a, b)


def _mm_nt_bwd(res, g):
    a, b = res
    return _dot(g, b, 1, 0).astype(a.dtype), _dot(g, a, 0, 0).astype(b.dtype)


mm_nt.defvjp(_mm_nt_fwd, _mm_nt_bwd)


@jax.custom_vjp
def mm_tn(a, b):
    return _dot(a, b, 0, 0)


def _mm_tn_fwd(a, b):
    return _dot(a, b, 0, 0), (a, b)


def _mm_tn_bwd(res, g):
    a, b = res
    return _dot(b, g, 1, 1).astype(a.dtype), _dot(a, g, 1, 0).astype(b.dtype)


mm_tn.defvjp(_mm_tn_fwd, _mm_tn_bwd)


def matmul(a, b, *, mode, out_dtype, name, n=None, b_off=0, tm_cap=1024, tn_cap=512, tk_cap=1536):
    if mode == "tn":
        k_dim, m_dim = a.shape
    else:
        m_dim, k_dim = a.shape
    n_dim = n if n is not None else (b.shape[0] if mode == "nt" else b.shape[1])
    tm = m_dim if m_dim < LANES else _pick(m_dim, tm_cap)
    tn = _pick(n_dim, tn_cap)
    tk = k_dim if k_dim < LANES else _pick(k_dim, tk_cap)
    assert m_dim % tm == 0 and n_dim % tn == 0 and k_dim % tk == 0 and b_off % tn == 0
    nk = k_dim // tk
    off = b_off // tn
    if mode == "nn":
        a_spec = pl.BlockSpec((tm, tk), lambda i, j, k: (i, k))
        b_spec = pl.BlockSpec((tk, tn), lambda i, j, k: (k, j))
        ca, cb = 1, 0
    elif mode == "nt":
        a_spec = pl.BlockSpec((tm, tk), lambda i, j, k: (i, k))
        b_spec = pl.BlockSpec((tn, tk), lambda i, j, k: (j + off, k))
        ca, cb = 1, 1
    else:
        a_spec = pl.BlockSpec((tk, tm), lambda i, j, k: (k, i))
        b_spec = pl.BlockSpec((tk, tn), lambda i, j, k: (k, j))
        ca, cb = 0, 0

    def body(a_ref, b_ref, o_ref, acc_ref):
        part = _dot(a_ref[...], b_ref[...], ca, cb)
        if nk == 1:
            o_ref[...] = part.astype(o_ref.dtype)
        else:
            k = pl.program_id(2)

            @pl.when(k == 0)
            def _():
                acc_ref[...] = part

            @pl.when(k > 0)
            def _():
                acc_ref[...] += part

            @pl.when(k == nk - 1)
            def _():
                o_ref[...] = acc_ref[...].astype(o_ref.dtype)

    return pl.pallas_call(
        body, name=name,
        out_shape=jax.ShapeDtypeStruct((m_dim, n_dim), out_dtype),
        grid=(m_dim // tm, n_dim // tn, nk),
        in_specs=[a_spec, b_spec],
        out_specs=pl.BlockSpec((tm, tn), lambda i, j, k: (i, j)),
        scratch_shapes=[pltpu.VMEM((tm, tn), F32)],
        compiler_params=_cparams("parallel", "parallel", "arbitrary"),
    )(a, b)


def _f32(xs):
    return [x.astype(F32) for x in xs]


def rowmap(f, rows, consts, out_dtypes, *, name, tr=256):
    r_dim = rows[0].shape[0]
    tr = _pick(r_dim, tr, mult=8)
    assert r_dim % tr == 0
    nr, nc = len(rows), len(consts)
    outs = jax.eval_shape(lambda *xs: f(*xs), *[jax.ShapeDtypeStruct((tr, x.shape[1]), F32) for x in rows],
                          *[jax.ShapeDtypeStruct(x.shape, F32) for x in consts])

    def body(*refs):
        res = f(*_f32([r[...] for r in refs[:nr + nc]]))
        for o_ref, o in zip(refs[nr + nc:], res, strict=True):
            o_ref[...] = o.astype(o_ref.dtype)

    return pl.pallas_call(
        body, name=name,
        out_shape=[jax.ShapeDtypeStruct((r_dim, o.shape[1]), dt) for o, dt in zip(outs, out_dtypes, strict=True)],
        grid=(r_dim // tr,),
        in_specs=[pl.BlockSpec((tr, x.shape[1]), lambda i: (i, 0)) for x in rows]
        + [pl.BlockSpec(x.shape, lambda i: (0, 0)) for x in consts],
        out_specs=[pl.BlockSpec((tr, o.shape[1]), lambda i: (i, 0)) for o in outs],
        compiler_params=_cparams("parallel"),
    )(*rows, *consts)


def rowmap_bwd(f, rows, consts, cts, *, name, row_grad, row_dtypes=None, tr=256, emit=(), row_add=None):
    r_dim = rows[0].shape[0]
    tr = _pick(r_dim, tr, mult=8)
    assert r_dim % tr == 0
    nr, nc, nct = len(rows), len(consts), len(cts)
    gi = [i for i, flag in enumerate(row_grad) if flag]
    row_dtypes = row_dtypes or [F32] * len(gi)
    row_add = row_add or [None] * len(gi)
    adds = [a for a in row_add if a is not None]
    outs = jax.eval_shape(lambda *xs: f(*xs), *[jax.ShapeDtypeStruct((tr, x.shape[1]), F32) for x in rows],
                          *[jax.ShapeDtypeStruct(x.shape, F32) for x in consts])

    def body(*refs):
        ins = _f32([r[...] for r in refs[:nr + nc]])
        ct = _f32([r[...] for r in refs[nr + nc:nr + nc + nct]])
        add_refs = list(refs[nr + nc + nct:nr + nc + nct + len(adds)])
        o_refs = refs[nr + nc + nct + len(adds):]
        res, vjp = jax.vjp(f, *ins)
        grads = vjp(tuple(ct))
        for o_ref, i, a in zip(o_refs[:len(gi)], gi, row_add):
            g = grads[i] if a is None else grads[i] + add_refs.pop(0)[...].astype(F32)
            o_ref[...] = g.astype(o_ref.dtype)
        first = pl.program_id(0) == 0
        for o_ref, g in zip(o_refs[len(gi):len(gi) + nc], grads[nr:]):
            @pl.when(first)
            def _(o_ref=o_ref, g=g):
                o_ref[...] = g

            @pl.when(jnp.logical_not(first))
            def _(o_ref=o_ref, g=g):
                o_ref[...] += g
        for o_ref, i in zip(o_refs[len(gi) + nc:], emit):
            o_ref[...] = res[i].astype(o_ref.dtype)

    out_shape = ([jax.ShapeDtypeStruct(rows[i].shape, dt) for i, dt in zip(gi, row_dtypes, strict=True)]
                 + [jax.ShapeDtypeStruct(x.shape, F32) for x in consts]
                 + [jax.ShapeDtypeStruct((r_dim, outs[i].shape[1]), F32) for i in emit])
    out_specs = ([pl.BlockSpec((tr, rows[i].shape[1]), lambda i_: (i_, 0)) for i in gi]
                 + [pl.BlockSpec(x.shape, lambda i_: (0, 0)) for x in consts]
                 + [pl.BlockSpec((tr, outs[i].shape[1]), lambda i_: (i_, 0)) for i in emit])
    res = pl.pallas_call(
        body, name=name,
        out_shape=out_shape,
        grid=(r_dim // tr,),
        in_specs=[pl.BlockSpec((tr, x.shape[1]), lambda i: (i, 0)) for x in rows]
        + [pl.BlockSpec(x.shape, lambda i: (0, 0)) for x in consts]
        + [pl.BlockSpec((tr, x.shape[1]), lambda i: (i, 0)) for x in list(cts) + adds],
        out_specs=out_specs,
        compiler_params=_cparams("arbitrary"),
    )(*rows, *consts, *cts, *adds)
    return res[:len(gi)], res[len(gi):len(gi) + nc], res[len(gi) + nc:]


def transpose(a, *, name, out_dtype=BF16, tr=512, tc=512):
    r_dim, c_dim = a.shape
    tr, tc = _pick(r_dim, tr), _pick(c_dim, tc)

    def body(a_ref, o_ref):
        o_ref[...] = a_ref[...].astype(F32).T.astype(o_ref.dtype)

    return pl.pallas_call(
        body, name=name, out_shape=jax.ShapeDtypeStruct((c_dim, r_dim), out_dtype),
        grid=(r_dim // tr, c_dim // tc),
        in_specs=[pl.BlockSpec((tr, tc), lambda i, j: (i, j))],
        out_specs=pl.BlockSpec((tc, tr), lambda i, j: (j, i)),
        compiler_params=_cparams("parallel", "parallel"),
    )(a)


CONV_HALO = 32
CONV_CHUNK = 256


def conv_fwd(x, w, b, *, name, cb=256):
    s_dim, c_dim = x.shape
    taps = w.shape[0]
    assert taps - 1 <= CONV_HALO and s_dim % CONV_CHUNK == 0 and c_dim % cb == 0
    n_chunks = s_dim // CONV_CHUNK
    ext = CONV_CHUNK + CONV_HALO

    def body(x_ref, w_ref, b_ref, o_ref, xp_ref):
        xp_ref[pl.ds(0, CONV_HALO), :] = jnp.zeros((CONV_HALO, cb), F32)
        xp_ref[pl.ds(CONV_HALO, s_dim), :] = x_ref[...].astype(F32)
        wv = w_ref[...].astype(F32)
        bv = b_ref[...].astype(F32)

        def chunk(t, carry):
            base = pl.multiple_of(t * CONV_CHUNK, CONV_CHUNK)
            xe = xp_ref[pl.ds(base, ext), :]
            acc = jnp.broadcast_to(bv, (CONV_CHUNK, cb))
            for j in range(taps):
                sh = xe if j == 0 else pltpu.roll(xe, shift=j, axis=0)
                acc = acc + wv[taps - 1 - j:taps - j, :] * sh[CONV_HALO:, :]
            o_ref[pl.ds(base, CONV_CHUNK), :] = acc
            return carry

        lax.fori_loop(0, n_chunks, chunk, 0)

    return pl.pallas_call(
        body, name=name,
        out_shape=jax.ShapeDtypeStruct((s_dim, c_dim), F32),
        grid=(c_dim // cb,),
        in_specs=[pl.BlockSpec((s_dim, cb), lambda i: (0, i)), pl.BlockSpec((taps, cb), lambda i: (0, i)),
                  pl.BlockSpec((1, cb), lambda i: (0, i))],
        out_specs=pl.BlockSpec((s_dim, cb), lambda i: (0, i)),
        scratch_shapes=[pltpu.VMEM((s_dim + CONV_HALO, cb), F32)],
        compiler_params=_cparams("parallel"),
    )(x, w, b)


def conv_bwd(x, w, g, *, name, cb=256):
    s_dim, c_dim = x.shape
    taps = w.shape[0]
    n_chunks = s_dim // CONV_CHUNK
    ext = CONV_CHUNK + CONV_HALO
    taps_pad = -(-taps // 8) * 8

    def body(x_ref, w_ref, g_ref, dx_ref, dw_ref, db_ref, xp_ref, gp_ref, acc_ref):
        xp_ref[pl.ds(0, CONV_HALO), :] = jnp.zeros((CONV_HALO, cb), F32)
        xp_ref[pl.ds(CONV_HALO, s_dim), :] = x_ref[...].astype(F32)
        gp_ref[pl.ds(0, s_dim), :] = g_ref[...].astype(F32)
        gp_ref[pl.ds(s_dim, CONV_HALO), :] = jnp.zeros((CONV_HALO, cb), F32)
        acc_ref[...] = jnp.zeros_like(acc_ref)
        wv = w_ref[...].astype(F32)

        def chunk(t, carry):
            base = pl.multiple_of(t * CONV_CHUNK, CONV_CHUNK)
            xe = xp_ref[pl.ds(base, ext), :]
            ge = gp_ref[pl.ds(base, ext), :]
            gc = ge[:CONV_CHUNK, :]
            dx = jnp.zeros((CONV_CHUNK, cb), F32)
            for j in range(taps):
                xs = xe if j == 0 else pltpu.roll(xe, shift=j, axis=0)
                gs = ge if j == 0 else pltpu.roll(ge, shift=ext - j, axis=0)
                k = taps - 1 - j
                dx = dx + wv[k:k + 1, :] * gs[:CONV_CHUNK, :]
                acc_ref[k:k + 1, :] += jnp.sum(gc * xs[CONV_HALO:, :], axis=0, keepdims=True)
            acc_ref[taps_pad:taps_pad + 1, :] += jnp.sum(gc, axis=0, keepdims=True)
            dx_ref[pl.ds(base, CONV_CHUNK), :] = dx
            return carry

        lax.fori_loop(0, n_chunks, chunk, 0)
        dw_ref[...] = acc_ref[0:taps, :]
        db_ref[...] = acc_ref[taps_pad:taps_pad + 1, :]

    return pl.pallas_call(
        body, name=name,
        out_shape=[jax.ShapeDtypeStruct((s_dim, c_dim), F32), jax.ShapeDtypeStruct((taps, c_dim), F32),
                   jax.ShapeDtypeStruct((1, c_dim), F32)],
        grid=(c_dim // cb,),
        in_specs=[pl.BlockSpec((s_dim, cb), lambda i: (0, i)), pl.BlockSpec((taps, cb), lambda i: (0, i)),
                  pl.BlockSpec((s_dim, cb), lambda i: (0, i))],
        out_specs=[pl.BlockSpec((s_dim, cb), lambda i: (0, i)), pl.BlockSpec((taps, cb), lambda i: (0, i)),
                   pl.BlockSpec((1, cb), lambda i: (0, i))],
        scratch_shapes=[pltpu.VMEM((s_dim + CONV_HALO, cb), F32), pltpu.VMEM((s_dim + CONV_HALO, cb), F32),
                        pltpu.VMEM((taps_pad + 8, cb), F32)],
        compiler_params=_cparams("parallel"),
    )(x, w, g)


def _iota2(n, axis):
    return lax.broadcasted_iota(jnp.int32, (n, n), axis)


def _to_col(row):
    n = row.shape[1]
    return jnp.sum(jnp.where(_iota2(n, 0) == _iota2(n, 1), jnp.broadcast_to(row, (n, n)), 0.0), axis=1, keepdims=True)


def _softplus(x):
    return jnp.maximum(x, 0.0) + jnp.log(1.0 + jnp.exp(-jnp.abs(x)))


def ssd_heads(x, dtraw, dt_bias, a_log, dskip, bm, cm, prev):
    h, q, _ = x.shape
    n = bm.shape[1]
    li = lax.broadcasted_iota(jnp.int32, (1, q, q), 1)
    si = lax.broadcasted_iota(jnp.int32, (1, q, q), 2)

    def to_col(row):
        return jnp.sum(jnp.where(li == si, jnp.broadcast_to(row, (h, q, q)), 0.0), axis=2, keepdims=True)

    dt_row = _softplus(dtraw + dt_bias)
    a_row = dt_row * (-jnp.exp(a_log))
    a_col = to_col(a_row)
    acs_col = jnp.sum(jnp.where(si <= li, jnp.broadcast_to(a_row, (h, q, q)), 0.0), axis=2, keepdims=True)
    acs_row = jnp.sum(jnp.where(li <= si, jnp.broadcast_to(a_col, (h, q, q)), 0.0), axis=1, keepdims=True)
    total = jnp.sum(a_row, axis=2, keepdims=True)
    xdt = x * to_col(dt_row)
    lmat = jnp.exp(jnp.where(li >= si, acs_col - acs_row, -1e30))
    bmb = jnp.broadcast_to(bm[None], (h, q, n))
    cmb = jnp.broadcast_to(cm[None], (h, q, n))
    y = bmm(mm_nt(cm, bm)[None] * lmat, xdt)
    y = y + bmm_nt(cmb, prev) * jnp.exp(acs_col)
    y = y + dskip * x
    state = bmm_tn(xdt * jnp.exp(total - acs_col), bmb)
    return y, jnp.exp(total) * prev + state


HEADS_PER_GROUP = SSM_HEADS // SSM_GROUPS
BM_COL0 = SSM_INNER // SSM_STATE
CM_COL0 = BM_COL0 + SSM_GROUPS


def ssd_fwd(xs_hm, dtraw_t, dt_bias, a_log, dskip, xbc, side=None):
    hg = HEADS_PER_GROUP

    def body(x_ref, dt_ref, dtb_ref, al_ref, dk_ref, bm_ref, cm_ref, y_ref, prev_ref, state_ref):
        @pl.when(pl.program_id(1) == 0)
        def _():
            state_ref[...] = jnp.zeros_like(state_ref)

        prev = state_ref[...]
        prev_ref[0] = prev
        y, nxt = ssd_heads(x_ref[...], dt_ref[...], dtb_ref[...], al_ref[...], dk_ref[...], bm_ref[...], cm_ref[...], prev)
        y_ref[...] = y
        state_ref[...] = nxt

    hp = pl.BlockSpec((hg, 1, 1), lambda g, c: (g, 0, 0))
    dtraw_t, dt_bias, a_log, dskip = [a.reshape(SSM_HEADS, 1, -1) for a in (dtraw_t, dt_bias, a_log, dskip)]
    return grid_call(
        body, (xs_hm, dtraw_t, dt_bias, a_log, dskip, xbc, xbc), name="ssd_fwd",
        out_shape=[jax.ShapeDtypeStruct((SSM_HEADS, S, SSM_HDIM), F32),
                   jax.ShapeDtypeStruct((N_CHUNKS, SSM_HEADS, SSM_HDIM, SSM_STATE), F32)],
        grid=(SSM_GROUPS, N_CHUNKS),
        in_specs=[pl.BlockSpec((hg, CHUNK, SSM_HDIM), lambda g, c: (g, c, 0)),
                  pl.BlockSpec((hg, 1, CHUNK), lambda g, c: (g, 0, c)), hp, hp, hp,
                  pl.BlockSpec((CHUNK, SSM_STATE), lambda g, c: (c, BM_COL0 + g)),
                  pl.BlockSpec((CHUNK, SSM_STATE), lambda g, c: (c, CM_COL0 + g))],
        out_specs=[pl.BlockSpec((hg, CHUNK, SSM_HDIM), lambda g, c: (g, c, 0)),
                   pl.BlockSpec((1, hg, SSM_HDIM, SSM_STATE), lambda g, c: (c, g, 0, 0))],
        scratch_shapes=[pltpu.VMEM((hg, SSM_HDIM, SSM_STATE), F32)],
        semantics=("parallel", "arbitrary"), side=side)


def ssd_bwd(xs_hm, dtraw_t, dt_bias, a_log, dskip, xbc, prev_all, dy_hm, side=None):
    hg = HEADS_PER_GROUP
    last = N_CHUNKS - 1

    def body(x_ref, dt_ref, dtb_ref, al_ref, dk_ref, bm_ref, cm_ref, prev_ref, dy_ref,
             dx_ref, ddt_ref, ddtb_ref, dal_ref, ddk_ref, dbm_ref, dcm_ref, dstate_ref):
        @pl.when(pl.program_id(1) == 0)
        def _():
            dstate_ref[...] = jnp.zeros_like(dstate_ref)
            ddtb_ref[...] = jnp.zeros_like(ddtb_ref)
            dal_ref[...] = jnp.zeros_like(dal_ref)
            ddk_ref[...] = jnp.zeros_like(ddk_ref)

        _, vjp = jax.vjp(ssd_heads, x_ref[...], dt_ref[...], dtb_ref[...], al_ref[...], dk_ref[...], bm_ref[...],
                         cm_ref[...], prev_ref[0])
        dx, ddt, ddtb, dal, ddk, dbm, dcm, dprev = vjp((dy_ref[...], dstate_ref[...]))
        dx_ref[...] = dx
        ddt_ref[...] = ddt
        ddtb_ref[...] += ddtb
        dal_ref[...] += dal
        ddk_ref[...] += ddk
        dbm_ref[...] = dbm
        dcm_ref[...] = dcm
        dstate_ref[...] = dprev

    hp = pl.BlockSpec((hg, 1, 1), lambda g, c: (g, 0, 0))
    xspec = pl.BlockSpec((hg, CHUNK, SSM_HDIM), lambda g, c: (g, last - c, 0))
    tspec = pl.BlockSpec((hg, 1, CHUNK), lambda g, c: (g, 0, last - c))
    gspec = pl.BlockSpec((CHUNK, SSM_STATE), lambda g, c: (last - c, g))
    dtraw_t, dt_bias, a_log, dskip = [a.reshape(SSM_HEADS, 1, -1) for a in (dtraw_t, dt_bias, a_log, dskip)]
    res, side_dst = grid_call(
        body, (xs_hm, dtraw_t, dt_bias, a_log, dskip, xbc, xbc, prev_all, dy_hm), name="ssd_bwd",
        out_shape=[jax.ShapeDtypeStruct((SSM_HEADS, S, SSM_HDIM), F32), jax.ShapeDtypeStruct((SSM_HEADS, 1, S), F32),
                   jax.ShapeDtypeStruct((SSM_HEADS, 1, 1), F32), jax.ShapeDtypeStruct((SSM_HEADS, 1, 1), F32),
                   jax.ShapeDtypeStruct((SSM_HEADS, 1, 1), F32),
                   jax.ShapeDtypeStruct((S, SSM_GROUPS * SSM_STATE), F32),
                   jax.ShapeDtypeStruct((S, SSM_GROUPS * SSM_STATE), F32)],
        grid=(SSM_GROUPS, N_CHUNKS),
        in_specs=[xspec, tspec, hp, hp, hp,
                  pl.BlockSpec((CHUNK, SSM_STATE), lambda g, c: (last - c, BM_COL0 + g)),
                  pl.BlockSpec((CHUNK, SSM_STATE), lambda g, c: (last - c, CM_COL0 + g)),
                  pl.BlockSpec((1, hg, SSM_HDIM, SSM_STATE), lambda g, c: (last - c, g, 0, 0)), xspec],
        out_specs=[xspec, tspec, hp, hp, hp, gspec, gspec],
        scratch_shapes=[pltpu.VMEM((hg, SSM_HDIM, SSM_STATE), F32)],
        semantics=("parallel", "arbitrary"), side=side)
    return [res[0]] + [r.reshape(SSM_HEADS, -1) for r in res[1:5]] + list(res[5:]), side_dst


ATT_HB = 8


def _bdot(a, b, ca, cb):
    return lax.dot_general(a.astype(BF16), b.astype(BF16), (((ca,), (cb,)), ((0,), (0,))), preferred_element_type=F32)


@jax.custom_vjp
def bmm(a, b):
    return _bdot(a, b, 2, 1)


def _bmm_fwd(a, b):
    return _bdot(a, b, 2, 1), (a, b)


def _bmm_bwd(res, g):
    a, b = res
    return _bdot(g, b, 2, 2).astype(a.dtype), _bdot(a, g, 1, 1).astype(b.dtype)


bmm.defvjp(_bmm_fwd, _bmm_bwd)


@jax.custom_vjp
def bmm_nt(a, b):
    return _bdot(a, b, 2, 2)


def _bmm_nt_fwd(a, b):
    return _bdot(a, b, 2, 2), (a, b)


def _bmm_nt_bwd(res, g):
    a, b = res
    return _bdot(g, b, 2, 1).astype(a.dtype), _bdot(g, a, 1, 1).astype(b.dtype)


bmm_nt.defvjp(_bmm_nt_fwd, _bmm_nt_bwd)


@jax.custom_vjp
def bmm_tn(a, b):
    return _bdot(a, b, 1, 1)


def _bmm_tn_fwd(a, b):
    return _bdot(a, b, 1, 1), (a, b)


def _bmm_tn_bwd(res, g):
    a, b = res
    return _bdot(b, g, 2, 2).astype(a.dtype), _bdot(a, g, 2, 1).astype(b.dtype)


bmm_tn.defvjp(_bmm_tn_fwd, _bmm_tn_bwd)


def att_heads(q, kp, kc, vp, vc, bias_p, bias_c, has_prev):
    h, b, dh = q.shape
    i = lax.broadcasted_iota(jnp.int32, (1, b, b), 1)
    j = lax.broadcasted_iota(jnp.int32, (1, b, b), 2)
    scale = dh ** -0.5
    sp = jnp.where(jnp.logical_and(j >= i, has_prev), bmm_nt(q, kp) * scale + bias_p, -1e30)
    sc = jnp.where(j <= i, bmm_nt(q, kc) * scale + bias_c, -1e30)
    m = lax.stop_gradient(jnp.maximum(jnp.max(sp, axis=2, keepdims=True), jnp.max(sc, axis=2, keepdims=True)))
    pp, pc = jnp.exp(sp - m), jnp.exp(sc - m)
    l = jnp.sum(pp, axis=2, keepdims=True) + jnp.sum(pc, axis=2, keepdims=True)
    o = bmm(pp / l, vp) + bmm(pc / l, vc)
    return o, jnp.broadcast_to(m + jnp.log(l), (h, b, dh))


def _att_specs(nb):
    hb, blk = ATT_HB, ATT_BLK
    cur = pl.BlockSpec((hb, blk, ATT_HDIM), lambda h, b: (h, b, 0))
    prv = pl.BlockSpec((hb, blk, ATT_HDIM), lambda h, b: (h, jnp.maximum(b - 1, 0), 0))
    bias = pl.BlockSpec((hb, 2, blk, blk), lambda h, b: (h, 0, 0, 0))
    return cur, prv, bias


def att_fwd(q, k, v, bias, nb, *, name):
    cur, prv, bspec = _att_specs(nb)

    def body(q_ref, kp_ref, kc_ref, vp_ref, vc_ref, b_ref, o_ref, l_ref):
        has_prev = (pl.program_id(1) % nb) != 0
        o, lse = att_heads(q_ref[...], kp_ref[...], kc_ref[...], vp_ref[...], vc_ref[...], b_ref[:, 0], b_ref[:, 1],
                           has_prev)
        o_ref[...] = o
        l_ref[...] = lse

    shp = jax.ShapeDtypeStruct((ATT_HEADS, S, ATT_HDIM), F32)
    return pl.pallas_call(
        body, name=name, out_shape=[shp, shp],
        grid=(ATT_HEADS // ATT_HB, S // ATT_BLK),
        in_specs=[cur, prv, cur, prv, cur, bspec],
        out_specs=[cur, cur],
        compiler_params=_cparams("parallel", "parallel"),
    )(q, k, k, v, v, bias)


def att_bwd(q, k, v, bias, do, dlse, nb, *, name):
    cur, prv, bspec = _att_specs(nb)

    def body(q_ref, kp_ref, kc_ref, vp_ref, vc_ref, b_ref, do_ref, dl_ref,
             dq_ref, dkc_ref, dkp_ref, dvc_ref, dvp_ref, db_ref):
        has_prev = (pl.program_id(1) % nb) != 0

        @pl.when(pl.program_id(1) == 0)
        def _():
            db_ref[...] = jnp.zeros_like(db_ref)

        ins = _f32([q_ref[...], kp_ref[...], kc_ref[...], vp_ref[...], vc_ref[...]]) + [b_ref[:, 0], b_ref[:, 1]]
        _, vjp = jax.vjp(functools.partial(att_heads, has_prev=has_prev), *ins)
        dq, dkp, dkc, dvp, dvc, dbp, dbc = vjp((do_ref[...], dl_ref[...]))
        dq_ref[...] = dq
        dkc_ref[...] = dkc
        dkp_ref[...] = dkp
        dvc_ref[...] = dvc
        dvp_ref[...] = dvp
        db_ref[:, 0] += dbp
        db_ref[:, 1] += dbc

    shp = jax.ShapeDtypeStruct((ATT_HEADS, S, ATT_HDIM), F32)
    return pl.pallas_call(
        body, name=name,
        out_shape=[shp] * 5 + [jax.ShapeDtypeStruct((ATT_HEADS, 2, ATT_BLK, ATT_BLK), F32)],
        grid=(ATT_HEADS // ATT_HB, S // ATT_BLK),
        in_specs=[cur, prv, cur, prv, cur, bspec, cur, cur],
        out_specs=[cur] * 5 + [bspec],
        compiler_params=_cparams("parallel", "arbitrary"),
    )(q, k, k, v, v, bias, do, dlse)


def shift_add(cur, prev, nb, *, name):
    n_blocks = S // ATT_BLK

    def body(c_ref, p_ref, o_ref):
        nxt = pl.program_id(0) + 1
        keep = jnp.where((nxt % nb) != 0, 1.0, 0.0)
        o_ref[...] = c_ref[...] + keep * p_ref[...]

    return pl.pallas_call(
        body, name=name, out_shape=jax.ShapeDtypeStruct(cur.shape, F32),
        grid=(n_blocks,),
        in_specs=[pl.BlockSpec((ATT_HEADS, ATT_BLK, ATT_HDIM), lambda b: (0, b, 0)),
                  pl.BlockSpec((ATT_HEADS, ATT_BLK, ATT_HDIM), lambda b: (0, jnp.minimum(b + 1, n_blocks - 1), 0))],
        out_specs=pl.BlockSpec((ATT_HEADS, ATT_BLK, ATT_HDIM), lambda b: (0, b, 0)),
        compiler_params=_cparams("parallel"),
    )(cur, prev)


ATT_PAIRS = ATT_HEADS // 2
PAIR_W = 2 * ATT_HDIM


def att_pairs(q, kp, kc, vp, vc, bias, has_prev):
    t, b, w = q.shape
    i = lax.broadcasted_iota(jnp.int32, (1, b, b), 1)
    j = lax.broadcasted_iota(jnp.int32, (1, b, b), 2)
    first = lax.broadcasted_iota(jnp.int32, (1, 1, w), 2) < ATT_HDIM
    scale = ATT_HDIM ** -0.5
    outs, lses = [], []
    for ab in range(2):
        qh = jnp.where(first if ab == 0 else jnp.logical_not(first), q, 0.0)
        sp = jnp.where(jnp.logical_and(j >= i, has_prev), bmm_nt(qh, kp) * scale + bias[:, ab, 0], -1e30)
        sc = jnp.where(j <= i, bmm_nt(qh, kc) * scale + bias[:, ab, 1], -1e30)
        m = lax.stop_gradient(jnp.maximum(jnp.max(sp, axis=2, keepdims=True), jnp.max(sc, axis=2, keepdims=True)))
        pp, pc = jnp.exp(sp - m), jnp.exp(sc - m)
        l = jnp.sum(pp, axis=2, keepdims=True) + jnp.sum(pc, axis=2, keepdims=True)
        outs.append(bmm(pp / l, vp) + bmm(pc / l, vc))
        lses.append(jnp.broadcast_to(m + jnp.log(l), (t, b, w)))
    return jnp.where(first, outs[0], outs[1]), jnp.where(first, lses[0], lses[1])


def _pair_tiles(ref):
    return jnp.stack([ref[:, PAIR_W * t:PAIR_W * (t + 1)] for t in range(ATT_PAIRS)])


def _store_pair_tiles(ref, val):
    for t in range(ATT_PAIRS):
        ref[:, PAIR_W * t:PAIR_W * (t + 1)] = val[t].astype(ref.dtype)


def pair_bias(bias):
    return bias.reshape(ATT_PAIRS, 2, 2, ATT_BLK, ATT_BLK)


def att2_fwd(q, k, v, bias, nb, cols, *, name, side=None):
    n_blocks = S // ATT_BLK
    qc, kc, vc = cols

    def body(q_ref, k_ref, v_ref, b_ref, o_ref, l_ref, kprev, vprev):
        blk = pl.program_id(0)

        @pl.when(blk == 0)
        def _():
            kprev[...] = jnp.zeros_like(kprev)
            vprev[...] = jnp.zeros_like(vprev)

        k3, v3 = _pair_tiles(k_ref), _pair_tiles(v_ref)
        o, lse = att_pairs(_pair_tiles(q_ref), kprev[...], k3, vprev[...], v3, b_ref[...], (blk % nb) != 0)
        _store_pair_tiles(o_ref, o)
        _store_pair_tiles(l_ref, lse)
        kprev[...] = k3
        vprev[...] = v3

    def spec(c):
        return pl.BlockSpec((ATT_BLK, D), lambda b: (b, c))

    shp = jax.ShapeDtypeStruct((S, D), F32)
    return grid_call(
        body, (q, k, v, bias), name=name, out_shape=[shp, shp], grid=(n_blocks,),
        in_specs=[spec(qc), spec(kc), spec(vc), pl.BlockSpec(bias.shape, lambda b: (0, 0, 0, 0, 0))],
        out_specs=[spec(0), spec(0)],
        scratch_shapes=[pltpu.VMEM((ATT_PAIRS, ATT_BLK, PAIR_W), BF16), pltpu.VMEM((ATT_PAIRS, ATT_BLK, PAIR_W), BF16)],
        semantics=("arbitrary",), side=side)


def att2_bwd(q, k, v, bias, do, dlse, nb, cols, *, name, side=None):
    n_blocks = S // ATT_BLK
    qc, kc, vc = cols

    def body(q_ref, k_ref, v_ref, b_ref, do_ref, dl_ref, dq_ref, dk_ref, dv_ref, db_ref, kprev, vprev, dk_own, dv_own):
        blk = pl.program_id(0)

        @pl.when(blk == 0)
        def _():
            for r in (kprev, vprev, dk_own, dv_own, db_ref):
                r[...] = jnp.zeros_like(r)

        @pl.when(blk < n_blocks)
        def _():
            k3, v3 = _pair_tiles(k_ref), _pair_tiles(v_ref)
            ins = _f32([_pair_tiles(q_ref), kprev[...], k3, vprev[...], v3]) + [b_ref[...]]
            _, vjp = jax.vjp(functools.partial(att_pairs, has_prev=(blk % nb) != 0), *ins)
            dq, dkp, dkc, dvp, dvc, db = vjp((_pair_tiles(do_ref), _pair_tiles(dl_ref)))
            _store_pair_tiles(dq_ref, dq)
            _store_pair_tiles(dk_ref, dk_own[...] + dkp)
            _store_pair_tiles(dv_ref, dv_own[...] + dvp)
            dk_own[...] = dkc
            dv_own[...] = dvc
            db_ref[...] += db
            kprev[...] = k3
            vprev[...] = v3

        @pl.when(blk == n_blocks)
        def _():
            _store_pair_tiles(dk_ref, dk_own[...])
            _store_pair_tiles(dv_ref, dv_own[...])

    def spec(c):
        return pl.BlockSpec((ATT_BLK, D), lambda b: (jnp.minimum(b, n_blocks - 1), c))

    late = pl.BlockSpec((ATT_BLK, D), lambda b: (jnp.maximum(b - 1, 0), 0))
    bspec = pl.BlockSpec(bias.shape, lambda b: (0, 0, 0, 0, 0))
    tile_f32 = pltpu.VMEM((ATT_PAIRS, ATT_BLK, PAIR_W), F32)
    tile_bf16 = pltpu.VMEM((ATT_PAIRS, ATT_BLK, PAIR_W), BF16)
    return grid_call(
        body, (q, k, v, bias, do, dlse), name=name,
        out_shape=[jax.ShapeDtypeStruct((S, D), BF16), jax.ShapeDtypeStruct((S, D), F32),
                   jax.ShapeDtypeStruct((S, D), F32), jax.ShapeDtypeStruct(bias.shape, F32)],
        grid=(n_blocks + 1,),
        in_specs=[spec(qc), spec(kc), spec(vc), bspec, spec(0), spec(0)],
        out_specs=[spec(0), late, late, bspec],
        scratch_shapes=[tile_bf16, tile_bf16, tile_f32, tile_f32],
        semantics=("arbitrary",), side=side)


def regroup(a, dil, inverse=False):
    if dil == 1:
        return a
    c_dim = a.shape[1]
    shape = (dil, S // dil, c_dim) if inverse else (S // dil, dil, c_dim)
    return jnp.transpose(a.reshape(shape), (1, 0, 2)).reshape(S, c_dim)


def _silu(x):
    return x * jax.nn.sigmoid(x)


def _rms(x):
    return x * lax.rsqrt(jnp.mean(x * x, -1, keepdims=True) + EPS)


def f_normmod(x, g, sc, sh):
    return (_rms(x) * g * (1.0 + sc) + sh,)


def f_resid(x, mix, gate):
    return (x + gate * mix,)


def f_resid_bias(x, mix, gate, b):
    return (x + gate * (mix + b),)


def f_swiglu(hgu):
    return (_silu(hgu[:, :FFN_HIDDEN]) * hgu[:, FFN_HIDDEN:],)


def f_silu(x):
    return (_silu(x),)


def f_gated_norm(y, z, g):
    return (_rms(y * _silu(z)) * g,)


def f_glu(y, b):
    y = y + b
    return (y[:, :D] * jax.nn.sigmoid(y[:, D:]),)


def f_ln_silu(u, g, b):
    mu = jnp.mean(u, -1, keepdims=True)
    var = jnp.mean(jnp.square(u - mu), -1, keepdims=True)
    return (_silu((u - mu) * lax.rsqrt(var + EPS) * g + b),)


def f_combine(o1, o2, o3, l1, l2, l3):
    m = lax.stop_gradient(jnp.maximum(jnp.maximum(l1, l2), l3))
    e1, e2, e3 = jnp.exp(l1 - m), jnp.exp(l2 - m), jnp.exp(l3 - m)
    return ((e1 * o1 + e2 * o2 + e3 * o3) / (e1 + e2 + e3),)


def f_head(x, tgt, g):
    return (0.5 * jnp.mean(jnp.square(_rms(x) * g - tgt), -1, keepdims=True),)


def f_sum3(a, b, c):
    return (a + b + c,)


def f_sum4(a, b, c, d):
    return (a + b + c + d,)


def f_add(a, b):
    return (a + b,)


def f_adamw(w, g, m, v):
    m = ADAM_B1 * m + (1.0 - ADAM_B1) * g
    v = ADAM_B2 * v + (1.0 - ADAM_B2) * jnp.square(g)
    m_hat = m / (1.0 - ADAM_B1 ** ADAM_STEP)
    v_hat = v / (1.0 - ADAM_B2 ** ADAM_STEP)
    return -ADAM_LR * (m_hat / (jnp.sqrt(v_hat) + ADAM_EPS) + ADAM_WD * w), m, v


def _rows_tile(r, cap=256):
    return _pick(r, cap, mult=8)


def adamw(w, g, m, v, *, name):
    shape = w.shape
    c_dim = shape[-1] if len(shape) > 1 else shape[0]
    flat = [a.reshape(-1, c_dim) for a in (w, g, m, v)]
    res = rowmap(f_adamw, flat, [], [F32] * 3, name=name, tr=_rows_tile(flat[0].shape[0], cap=128))
    return [r.reshape(shape) for r in res]


def _t5_bucket(dist):
    max_exact = REL_BUCKETS // 2
    n = jnp.maximum(dist, 1).astype(F32)
    large = max_exact + jnp.log(n / max_exact) / math.log(REL_MAX_DIST / max_exact) * (REL_BUCKETS - max_exact)
    large = jnp.minimum(large.astype(jnp.int32), REL_BUCKETS - 1)
    return jnp.where(dist < max_exact, dist, large)


def _att_buckets(dil):
    i = jnp.arange(ATT_BLK)[:, None]
    j = jnp.arange(2 * ATT_BLK)[None, :]
    bkt = _t5_bucket(jnp.maximum(ATT_BLK + i - j, 0) * dil)
    return jnp.transpose(bkt.reshape(ATT_BLK, 2, ATT_BLK), (1, 0, 2))


def att_bias(rel_table, p, dil):
    tab = rel_table[:, p * ATT_HEADS:(p + 1) * ATT_HEADS]
    onehot = (jnp.arange(REL_BUCKETS)[:, None] == _att_buckets(dil).reshape(1, -1)).astype(F32)
    bias = lax.dot_general(tab, onehot, (((0,), (0,)), ((), ())), precision=lax.Precision.HIGHEST)
    return bias.reshape(ATT_HEADS, 2, ATT_BLK, ATT_BLK)


def att_bias_grad(dbias, dil, *, name):
    onehot = (_att_buckets(dil).reshape(-1, 1) == jnp.arange(LANES)[None, :]).astype(BF16)
    dtab = matmul(dbias.reshape(ATT_HEADS, -1), onehot, mode="nn", out_dtype=F32, name=name, tk_cap=2048)
    return dtab[:, :REL_BUCKETS].T


def to_heads(a, n_heads, dil=1):
    hd = a.shape[1] // n_heads
    return jnp.transpose(a.reshape(S // dil, dil, n_heads, hd), (2, 1, 0, 3)).reshape(n_heads, S, hd)


def from_heads(a, dil=1):
    n_heads, _, hd = a.shape
    return jnp.transpose(a.reshape(n_heads, dil, S // dil, hd), (2, 1, 0, 3)).reshape(S, n_heads * hd)


def regroup_heads(a, dil, inverse=False):
    n_heads, _, hd = a.shape
    if dil == 1:
        return a
    if inverse:
        return jnp.transpose(a.reshape(n_heads, dil, S // dil, hd), (0, 2, 1, 3)).reshape(n_heads, S, hd)
    return jnp.transpose(a.reshape(n_heads, S // dil, dil, hd), (0, 2, 1, 3)).reshape(n_heads, S, hd)


HY_Z, HY_XBC, HY_DT, HY_Q, HY_K, HY_V = 2048, 3072, 32, 3072, 1024, 1024
HY_IN = HY_Z + HY_XBC + HY_DT + HY_Q + HY_K + HY_V
OFF_Z, OFF_XBC, OFF_Q, OFF_KV, OFF_DT = 0, 2048, 5120, 8192, 10240
HY_CAT = OFF_DT + LANES
DT_PAD = LANES


def hy_to_cat(w):
    z, xbc, dt, qkv = w[:2048], w[2048:5120], w[5120:5152], w[5152:]
    return jnp.concatenate([z, xbc, qkv, dt, jnp.zeros((DT_PAD - HY_DT,) + w.shape[1:], w.dtype)], axis=0)


def hy_from_cat(w, axis=0):
    part = lambda a, b: lax.slice_in_dim(w, a, b, axis=axis)
    return jnp.concatenate([part(0, 5120), part(OFF_DT, OFF_DT + HY_DT), part(5120, OFF_DT)], axis=axis)


def device_step(x, tgt, mods, wts, sp, comm=None):
    g = {}
    dmods = [[None] * 6 for _ in range(2)]
    wts = dict(wts)

    def w_side(i):
        return None if comm is None else GatherRows(comm["pack"], comm["full"], *W_BATCHES[i])

    def g_side(i):
        return None if comm is None else ScatterRows(comm["ga"], comm["recv"], *G_BATCHES[i])

    def normmod(xi, gain, sc, sh, nm):
        return rowmap(f_normmod, [xi], [gain, sc, sh], [BF16], name=nm)[0]

    def ffn_fwd(xi, i, gate, nm):
        h = normmod(xi, sp["norm_ffn_g"][i], mods[i][4], mods[i][3], nm + "_norm")
        hgu = matmul(h, wts["gu_t"][i], mode="nt", out_dtype=F32, name=nm + "_gu")
        act = rowmap(f_swiglu, [hgu], [], [BF16], name=nm + "_act", tr=128)[0]
        out = matmul(act, wts["down"][i], mode="nn", out_dtype=F32, name=nm + "_down")
        xo = rowmap(f_resid, [xi, out], [gate], [F32], name=nm + "_res")[0]
        return xo, (h, hgu, act, out)

    def ffn_bwd(dres, xi, i, saved, nm):
        h, hgu, act, out = saved
        (dout,), (dgate,), _ = rowmap_bwd(f_resid, [xi, out], [mods[i][5]], [dres], name=nm + "_res_b",
                                          row_grad=[False, True], row_dtypes=[BF16])
        dmods[i][5] = dgate
        dact = matmul(dout, wts["down"][i], mode="nt", out_dtype=F32, name=nm + "_down_dx")
        g[f"down{i}"] = matmul(transpose(dout, name=nm + "_dout_t"), act, mode="nn", out_dtype=F32, name=nm + "_down_dw")
        (dhgu,), _, _ = rowmap_bwd(f_swiglu, [hgu], [], [dact], name=nm + "_act_b", row_grad=[True],
                                   row_dtypes=[BF16], tr=128)
        g[f"gu_t{i}"] = matmul(transpose(h, name=nm + "_h_t"), dhgu, mode="nn", out_dtype=F32, name=nm + "_gu_dw")
        dh = matmul(dhgu, wts["gu_t"][i], mode="nn", out_dtype=F32, name=nm + "_gu_dx")
        (dres,), (dg_, dsc, dsh), _ = rowmap_bwd(f_normmod, [xi], [sp["norm_ffn_g"][i], mods[i][4], mods[i][3]], [dh],
                                                 name=nm + "_norm_b", row_grad=[True], row_add=[dres])
        g[f"norm_ffn_g{i}"] = dg_
        dmods[i][4], dmods[i][3] = dsc, dsh
        return dres

    h0 = normmod(x, sp["norm_mix_g"][0], mods[0][1], mods[0][0], "l0_norm")
    w_in = wts["hy_in_t"]
    z = matmul(h0, w_in, mode="nt", out_dtype=F32, name="hy_z", n=HY_Z, b_off=OFF_Z)
    xbc_raw = matmul(h0, w_in, mode="nt", out_dtype=F32, name="hy_xbc", n=HY_XBC, b_off=OFF_XBC)
    q = matmul(h0, w_in, mode="nt", out_dtype=BF16, name="hy_q", n=HY_Q, b_off=OFF_Q)
    kv = matmul(h0, w_in, mode="nt", out_dtype=BF16, name="hy_kv", n=HY_K + HY_V, b_off=OFF_KV)
    dtr = matmul(h0, w_in, mode="nt", out_dtype=F32, name="hy_dt", n=DT_PAD, b_off=OFF_DT)
    xbc_pre = conv_fwd(xbc_raw, sp["hy_conv_w"], sp["hy_conv_b"], name="hy_conv")
    xbc = rowmap(f_silu, [xbc_pre], [], [F32], name="hy_conv_act", tr=128)[0]
    xs_hm = to_heads(xbc[:, :SSM_INNER], SSM_HEADS)
    dtraw_t = dtr[:, :HY_DT].T
    (y_hm, prev_all), full = ssd_fwd(xs_hm, dtraw_t, sp["hy_dt_bias"], sp["hy_a_log"], sp["hy_d_skip"], xbc, side=w_side(1))
    if comm is not None:
        comm["full"] = full
    y = from_heads(y_hm)
    ysn = rowmap(f_gated_norm, [y, z], [sp["hy_ssm_norm_g"]], [BF16], name="hy_gnorm", tr=128)[0]
    att_in, att_o, att_l = [], [], []
    for p, (win, dil) in enumerate(ATT_PATTERNS):
        if dil == 1:
            qa, ka, va, cols = q, kv, kv, (p, 0, 1)
        else:
            qa, ka, cols = regroup(q[:, p * D:(p + 1) * D], dil), regroup(kv, dil), (0, 0, 1)
            va = ka
        bias = pair_bias(att_bias(sp["rel_table"], p, dil))
        nb = S // dil // ATT_BLK
        (o, lse), full = att2_fwd(qa, ka, va, bias, nb, cols, name=f"att_fwd{p}", side=w_side(2 + p))
        if comm is not None:
            comm["full"] = full
        att_in.append((qa, ka, va, bias, nb, cols))
        att_o.append(regroup(o, dil, inverse=True))
        att_l.append(regroup(lse, dil, inverse=True))
    if comm is not None:
        wts.update(unpack_weights(comm["full"], skip=("hy_in_t",)))
    att = rowmap(f_combine, att_o + att_l, [], [BF16], name="att_combine", tr=128)[0]
    cat = jnp.concatenate([ysn, att], axis=-1)
    mix0 = matmul(cat, wts["hy_out"], mode="nn", out_dtype=F32, name="hy_out")
    x1 = rowmap(f_resid, [x, mix0], [mods[0][2]], [F32], name="l0_res")[0]
    x2, ffn0 = ffn_fwd(x1, 0, mods[0][5], "ffn0")

    h1 = normmod(x2, sp["norm_mix_g"][1], mods[1][1], mods[1][0], "l1_norm")
    p1 = matmul(h1, wts["pw1_t"], mode="nt", out_dtype=F32, name="cv_pw1")
    u = rowmap(f_glu, [p1], [sp["cv_b_pw1"]], [F32], name="cv_glu")[0]
    uc = conv_fwd(u, sp["cv_w_dw"], sp["cv_b_dw"], name="cv_conv")
    ul = rowmap(f_ln_silu, [uc], [sp["cv_ln_g"], sp["cv_ln_b"]], [BF16], name="cv_ln")[0]
    mix1 = matmul(ul, wts["pw2"], mode="nn", out_dtype=F32, name="cv_pw2")
    x3 = rowmap(f_resid_bias, [x2, mix1], [mods[1][2], sp["cv_b_pw2"]], [F32], name="l1_res")[0]
    x4, ffn1 = ffn_fwd(x3, 1, mods[1][5], "ffn1")

    ones = jnp.ones((S, 1), F32)
    (dres,), (dfinal,), (loss_rows,) = rowmap_bwd(f_head, [x4, tgt], [sp["final_norm_g"]], [ones], name="head",
                                                  row_grad=[True, False], emit=(0,))
    g["final_norm_g"] = dfinal

    dres = ffn_bwd(dres, x3, 1, ffn1, "ffn1")
    (dmix1,), (dg1, db2), _ = rowmap_bwd(f_resid_bias, [x2, mix1], [mods[1][2], sp["cv_b_pw2"]], [dres], name="l1_res_b",
                                         row_grad=[False, True], row_dtypes=[BF16])
    dmods[1][2] = dg1
    g["cv_b_pw2"] = db2
    dul = matmul(dmix1, wts["pw2"], mode="nt", out_dtype=F32, name="cv_pw2_dx")
    g["pw2"] = matmul(transpose(dmix1, name="cv_dmix_t"), ul, mode="nn", out_dtype=F32, name="cv_pw2_dw")
    (duc,), (g["cv_ln_g"], g["cv_ln_b"]), _ = rowmap_bwd(f_ln_silu, [uc], [sp["cv_ln_g"], sp["cv_ln_b"]], [dul],
                                                         name="cv_ln_b", row_grad=[True])
    du, g["cv_w_dw"], g["cv_b_dw"] = conv_bwd(u, sp["cv_w_dw"], duc, name="cv_conv_b")
    (dp1,), (g["cv_b_pw1"],), _ = rowmap_bwd(f_glu, [p1], [sp["cv_b_pw1"]], [du], name="cv_glu_b", row_grad=[True],
                                             row_dtypes=[BF16])
    g["pw1_t"] = matmul(transpose(h1, name="cv_h_t"), dp1, mode="nn", out_dtype=F32, name="cv_pw1_dw")
    dh1 = matmul(dp1, wts["pw1_t"], mode="nn", out_dtype=F32, name="cv_pw1_dx")
    (dres,), (dg_, dsc, dsh), _ = rowmap_bwd(f_normmod, [x2], [sp["norm_mix_g"][1], mods[1][1], mods[1][0]], [dh1],
                                             name="l1_norm_b", row_grad=[True], row_add=[dres])
    g["norm_mix_g1"] = dg_
    dmods[1][1], dmods[1][0] = dsc, dsh

    dres = ffn_bwd(dres, x1, 0, ffn0, "ffn0")
    (dmix0,), (dg1,), _ = rowmap_bwd(f_resid, [x, mix0], [mods[0][2]], [dres], name="l0_res_b",
                                     row_grad=[False, True], row_dtypes=[BF16])
    dmods[0][2] = dg1
    dysn = matmul(dmix0, wts["hy_out"], mode="nt", out_dtype=F32, name="hy_out_dy", n=SSM_INNER, b_off=0)
    datt = matmul(dmix0, wts["hy_out"], mode="nt", out_dtype=F32, name="hy_out_da", n=D, b_off=SSM_INNER)
    g["hy_out"] = matmul(transpose(dmix0, name="hy_dmix_t"), cat, mode="nn", out_dtype=F32, name="hy_out_dw")
    (dy, dz), (g["hy_ssm_norm_g"],), _ = rowmap_bwd(f_gated_norm, [y, z], [sp["hy_ssm_norm_g"]], [dysn], name="hy_gnorm_b",
                                                    row_grad=[True, True], tr=128)
    if comm is not None:
        comm["ga"] = pack_grads(g, GA_LAYOUT, GA_ROWS)
        comm["recv"] = jnp.zeros((3, GA_ROWS, D), BF16)
    (dxs_hm, ddtraw_t, g["hy_dt_bias"], g["hy_a_log"], g["hy_d_skip"], dbm, dcm), recv = ssd_bwd(
        xs_hm, dtraw_t, sp["hy_dt_bias"], sp["hy_a_log"], sp["hy_d_skip"], xbc, prev_all, to_heads(dy, SSM_HEADS),
        side=g_side(0))
    if comm is not None:
        comm["recv"] = recv
    dxbc = jnp.concatenate([from_heads(dxs_hm), dbm, dcm], axis=-1)
    (dxbc_pre,), _, _ = rowmap_bwd(f_silu, [xbc_pre], [], [dxbc], name="hy_conv_act_b", row_grad=[True], tr=128)
    dxbc_raw, g["hy_conv_w"], g["hy_conv_b"] = conv_bwd(xbc_raw, sp["hy_conv_w"], dxbc_pre, name="hy_conv_b")
    dol, _, _ = rowmap_bwd(f_combine, att_o + att_l, [], [datt], name="att_combine_b", row_grad=[True] * 6, tr=128)
    dqs, dks, dvs, dtabs = [], [], [], []
    for p, (win, dil) in enumerate(ATT_PATTERNS):
        qa, ka, va, bias, nb, cols = att_in[p]
        (dq, dkp_, dvp_, dbias), recv = att2_bwd(qa, ka, va, bias, regroup(dol[p], dil), regroup(dol[3 + p], dil), nb,
                                                 cols, name=f"att_bwd{p}", side=g_side(1 + p))
        if comm is not None:
            comm["recv"] = recv
        dqs.append(regroup(dq, dil, inverse=True))
        dks.append(regroup(dkp_, dil, inverse=True))
        dvs.append(regroup(dvp_, dil, inverse=True))
        dtabs.append(att_bias_grad(dbias.reshape(ATT_HEADS, 2, ATT_BLK, ATT_BLK), dil, name=f"att_dtab{p}"))
    g["rel_table"] = jnp.concatenate(dtabs, axis=1)
    dk = rowmap(f_sum3, dks, [], [F32], name="att_dk_sum")[0]
    dv = rowmap(f_sum3, dvs, [], [F32], name="att_dv_sum")[0]
    ddt = jnp.pad(ddtraw_t.T, ((0, 0), (0, DT_PAD - HY_DT)))
    dproj = jnp.concatenate([dz, dxbc_raw] + dqs + [dk, dv, ddt], axis=-1).astype(BF16)
    g["hy_in_t"] = matmul(transpose(h0, name="hy_h_t"), dproj, mode="nn", out_dtype=F32, name="hy_in_dw")
    dh0 = matmul(dproj, w_in, mode="nn", out_dtype=F32, name="hy_in_dx")
    (dres,), (dg_, dsc, dsh), _ = rowmap_bwd(f_normmod, [x], [sp["norm_mix_g"][0], mods[0][1], mods[0][0]], [dh0],
                                             name="l0_norm_b", row_grad=[True], row_add=[dres])
    g["norm_mix_g0"] = dg_
    dmods[0][1], dmods[0][0] = dsc, dsh
    return loss_rows, dres, g, dmods


ANY = pl.BlockSpec(memory_space=pl.ANY)
WHOLE_VMEM = pl.BlockSpec(memory_space=pltpu.VMEM)


def _place():
    return lax.axis_index("x"), lax.axis_index("y"), lax.axis_index("c")


def _other_chips(x, y):
    return [(1 - x, y), (x, 1 - y), (1 - x, 1 - y)]


def allgather_small(v, *, name):
    m_per = v.shape[0]

    def body(x_ref, out_ref, send_sems, recv_sems, local_sem):
        x, y, c = _place()
        me, sibling = (x, y, c), (x, y, 1 - c)
        chips = _other_chips(x, y)

        def rows(px, py, pc):
            return out_ref.at[pl.ds((4 * px + 2 * py + pc) * m_per, m_per), :]

        def copy(k, block, to, src=None):
            return pltpu.make_async_remote_copy(
                src_ref=rows(*block) if src is None else src, dst_ref=rows(*block),
                send_sem=send_sems.at[k], recv_sem=recv_sems.at[k], device_id=to, device_id_type=MESH)

        mine = pltpu.make_async_copy(x_ref, rows(*me), local_sem)
        mine.start()
        first = [copy(0, me, sibling, src=x_ref)]
        first += [copy(1 + j, me, (*chip, c), src=x_ref) for j, chip in enumerate(chips)]
        for cp in first:
            cp.start()
        passed = [copy(4 + j, (*chip, c), sibling) for j, chip in enumerate(chips)]
        for j, chip in enumerate(chips):
            copy(1 + j, (*chip, c), me).wait_recv()
            passed[j].start()
        copy(0, sibling, me).wait_recv()
        for j, chip in enumerate(chips):
            copy(4 + j, (*chip, 1 - c), me).wait_recv()
        for cp in first + passed:
            cp.wait_send()
        mine.wait()

    return pl.pallas_call(
        body, name=name,
        out_shape=jax.ShapeDtypeStruct((N_DEV * m_per, LANES), v.dtype),
        in_specs=[WHOLE_VMEM], out_specs=WHOLE_VMEM,
        scratch_shapes=[pltpu.SemaphoreType.DMA((7,)), pltpu.SemaphoreType.DMA((7,)), pltpu.SemaphoreType.DMA],
    )(v)


def allgather_chips(pack, *, name):
    half_rows = pack.shape[0] // 2

    def body(p_ref, o_ref, send_sems, recv_sems, local_sem):
        x, y, c = _place()
        chips = _other_chips(x, y)
        sibling = (x, y, 1 - c)
        my_half = pl.ds(c * half_rows, half_rows)
        its_half = pl.ds((1 - c) * half_rows, half_rows)
        mine = pltpu.make_async_copy(p_ref, o_ref.at[2 * x + y], local_sem)
        mine.start()
        sends = [pltpu.make_async_remote_copy(
            src_ref=p_ref.at[my_half], dst_ref=o_ref.at[2 * x + y, my_half],
            send_sem=send_sems.at[k], recv_sem=recv_sems.at[k],
            device_id=(cx, cy, c), device_id_type=MESH) for k, (cx, cy) in enumerate(chips)]
        for cp in sends:
            cp.start()
        passed = []
        for k, (cx, cy) in enumerate(chips):
            landed = o_ref.at[2 * cx + cy, my_half]
            pltpu.make_async_remote_copy(
                src_ref=p_ref.at[my_half], dst_ref=landed, send_sem=send_sems.at[k], recv_sem=recv_sems.at[k],
                device_id=(cx, cy, c), device_id_type=MESH).wait_recv()
            cp = pltpu.make_async_remote_copy(
                src_ref=landed, dst_ref=landed, send_sem=send_sems.at[3 + k], recv_sem=recv_sems.at[3 + k],
                device_id=sibling, device_id_type=MESH)
            cp.start()
            passed.append(cp)
        for k, (cx, cy) in enumerate(chips):
            from_sibling = o_ref.at[2 * cx + cy, its_half]
            pltpu.make_async_remote_copy(
                src_ref=from_sibling, dst_ref=from_sibling, send_sem=send_sems.at[3 + k], recv_sem=recv_sems.at[3 + k],
                device_id=sibling, device_id_type=MESH).wait_recv()
        for cp in sends + passed:
            cp.wait_send()
        mine.wait()

    return pl.pallas_call(
        body, name=name,
        out_shape=jax.ShapeDtypeStruct((N_CHIPS,) + pack.shape, pack.dtype),
        in_specs=[ANY], out_specs=ANY,
        scratch_shapes=[pltpu.SemaphoreType.DMA((6,)), pltpu.SemaphoreType.DMA((6,)), pltpu.SemaphoreType.DMA],
    )(pack)


def swap_halves(gpack, *, name):
    half_rows = gpack.shape[1] // 2

    def body(g_ref, r_ref, send_sems, recv_sems):
        x, y, c = _place()
        its_half = pl.ds((1 - c) * half_rows, half_rows)
        copies = [pltpu.make_async_remote_copy(
            src_ref=g_ref.at[s, its_half], dst_ref=r_ref.at[s], send_sem=send_sems.at[s], recv_sem=recv_sems.at[s],
            device_id=(x, y, 1 - c), device_id_type=MESH) for s in range(N_CHIPS)]
        for cp in copies:
            cp.start()
        for cp in copies:
            cp.wait()

    return pl.pallas_call(
        body, name=name,
        out_shape=jax.ShapeDtypeStruct((N_CHIPS, half_rows) + gpack.shape[2:], gpack.dtype),
        in_specs=[ANY], out_specs=ANY,
        scratch_shapes=[pltpu.SemaphoreType.DMA((N_CHIPS,)), pltpu.SemaphoreType.DMA((N_CHIPS,))],
    )(gpack)


def scatter_chips(gpack, *, name):
    def body(g_ref, own_ref, recv_ref, send_sems, recv_sems, local_sem):
        x, y, c = _place()
        chips = _other_chips(x, y)
        mine = pltpu.make_async_copy(g_ref.at[2 * x + y], own_ref, local_sem)
        mine.start()
        sends = [pltpu.make_async_remote_copy(
            src_ref=g_ref.at[2 * cx + cy], dst_ref=recv_ref.at[k], send_sem=send_sems.at[k], recv_sem=recv_sems.at[k],
            device_id=(cx, cy, c), device_id_type=MESH) for k, (cx, cy) in enumerate(chips)]
        for cp in sends:
            cp.start()
        for cp in sends:
            cp.wait_recv()
        for cp in sends:
            cp.wait_send()
        mine.wait()

    slot = jax.ShapeDtypeStruct(gpack.shape[1:], gpack.dtype)
    return pl.pallas_call(
        body, name=name,
        out_shape=[slot, jax.ShapeDtypeStruct((3,) + gpack.shape[1:], gpack.dtype)],
        in_specs=[ANY], out_specs=[ANY, ANY],
        scratch_shapes=[pltpu.SemaphoreType.DMA((3,)), pltpu.SemaphoreType.DMA((3,)), pltpu.SemaphoreType.DMA],
    )(gpack)


class GatherRows:
    def __init__(self, pack, full, lo, hi):
        assert (hi - lo) % 32 == 0 and lo % 16 == 0
        self.src, self.dst, self.lo, self.hi = pack, full, lo, hi

    def sems(self):
        return [pltpu.SemaphoreType.DMA((6,)), pltpu.SemaphoreType.DMA((6,)), pltpu.SemaphoreType.DMA]

    def _parts(self, pack_ref, full_ref, sems):
        send_sems, recv_sems, local_sem = sems
        x, y, c = _place()
        half = (self.hi - self.lo) // 2
        mine, its = pl.ds(self.lo + c * half, half), pl.ds(self.lo + (1 - c) * half, half)
        rows = pl.ds(self.lo, self.hi - self.lo)
        local = pltpu.make_async_copy(pack_ref.at[rows], full_ref.at[2 * x + y, rows], local_sem)
        chips = _other_chips(x, y)

        def remote(src, dst, k, to):
            return pltpu.make_async_remote_copy(src_ref=src, dst_ref=dst, send_sem=send_sems.at[k],
                                                recv_sem=recv_sems.at[k], device_id=to, device_id_type=MESH)

        sends = [remote(pack_ref.at[mine], full_ref.at[2 * x + y, mine], k, (cx, cy, c)) for k, (cx, cy) in enumerate(chips)]
        landed = [full_ref.at[2 * cx + cy, mine] for cx, cy in chips]
        arrive = [remote(pack_ref.at[mine], landed[k], k, (cx, cy, c)) for k, (cx, cy) in enumerate(chips)]
        passed = [remote(landed[k], landed[k], 3 + k, (x, y, 1 - c)) for k in range(3)]
        from_sibling = [remote(landed[k], full_ref.at[2 * cx + cy, its], 3 + k, (x, y, 1 - c))
                        for k, (cx, cy) in enumerate(chips)]
        return local, sends, arrive, passed, from_sibling

    def start(self, pack_ref, full_ref, sems):
        local, sends, _, _, _ = self._parts(pack_ref, full_ref, sems)
        local.start()
        for cp in sends:
            cp.start()

    def finish(self, pack_ref, full_ref, sems):
        local, sends, arrive, passed, from_sibling = self._parts(pack_ref, full_ref, sems)
        for k in range(3):
            arrive[k].wait_recv()
            passed[k].start()
        for cp in from_sibling:
            cp.wait_recv()
        for cp in sends + passed:
            cp.wait_send()
        local.wait()


class ScatterRows:
    def __init__(self, gpack, recv, lo, hi):
        assert lo % 16 == 0 and hi % 16 == 0
        self.src, self.dst, self.lo, self.hi = gpack, recv, lo, hi

    def sems(self):
        return [pltpu.SemaphoreType.DMA((3,)), pltpu.SemaphoreType.DMA((3,))]

    def _parts(self, g_ref, recv_ref, sems):
        send_sems, recv_sems = sems
        x, y, c = _place()
        rows = pl.ds(self.lo, self.hi - self.lo)
        return [pltpu.make_async_remote_copy(
            src_ref=g_ref.at[2 * cx + cy, rows], dst_ref=recv_ref.at[k, rows], send_sem=send_sems.at[k],
            recv_sem=recv_sems.at[k], device_id=(cx, cy, c), device_id_type=MESH)
            for k, (cx, cy) in enumerate(_other_chips(x, y))]

    def start(self, g_ref, recv_ref, sems):
        for cp in self._parts(g_ref, recv_ref, sems):
            cp.start()

    def finish(self, g_ref, recv_ref, sems):
        sends = self._parts(g_ref, recv_ref, sems)
        for cp in sends:
            cp.wait_recv()
        for cp in sends:
            cp.wait_send()


def side_call(side, *, name):
    def body(src_ref, dst_in_ref, dst_ref, *sems):
        side.start(src_ref, dst_ref, sems)
        side.finish(src_ref, dst_ref, sems)

    return pl.pallas_call(
        body, name=name, out_shape=jax.ShapeDtypeStruct(side.dst.shape, side.dst.dtype),
        in_specs=[ANY, ANY], out_specs=ANY, scratch_shapes=side.sems(), input_output_aliases={1: 0},
    )(side.src, side.dst)


def grid_call(body, args, *, name, out_shape, grid, in_specs, out_specs, scratch_shapes, semantics, side=None):
    if side is None:
        res = pl.pallas_call(body, name=name, out_shape=out_shape, grid=grid, in_specs=in_specs, out_specs=out_specs,
                             scratch_shapes=scratch_shapes, compiler_params=_cparams(*semantics))(*args)
        return res, None
    n_in, n_out, n_scr = len(args), len(out_shape), len(scratch_shapes)

    def wrapped(*refs):
        ins, (src_ref, _) = refs[:n_in], refs[n_in:n_in + 2]
        outs, dst_ref = refs[n_in + 2:n_in + 2 + n_out], refs[n_in + 2 + n_out]
        scr, sems = refs[n_in + 3 + n_out:n_in + 3 + n_out + n_scr], refs[n_in + 3 + n_out + n_scr:]
        first = functools.reduce(jnp.logical_and, [pl.program_id(i) == 0 for i in range(len(grid))])
        last = functools.reduce(jnp.logical_and, [pl.program_id(i) == n - 1 for i, n in enumerate(grid)])

        @pl.when(first)
        def _():
            side.start(src_ref, dst_ref, sems)

        body(*ins, *outs, *scr)

        @pl.when(last)
        def _():
            side.finish(src_ref, dst_ref, sems)

    res = pl.pallas_call(
        wrapped, name=name,
        out_shape=list(out_shape) + [jax.ShapeDtypeStruct(side.dst.shape, side.dst.dtype)],
        grid=grid, in_specs=list(in_specs) + [ANY, ANY], out_specs=list(out_specs) + [ANY],
        scratch_shapes=list(scratch_shapes) + side.sems(), input_output_aliases={n_in + 1: n_out},
        compiler_params=_cparams(*(["arbitrary"] * len(grid))),
    )(*args, side.src, side.dst)
    return res[:-1], res[-1]


def sibling_swap(p, *, name):
    def body(p_ref, r_ref, send_sem, recv_sem):
        x, y, c = _place()
        cp = pltpu.make_async_remote_copy(src_ref=p_ref, dst_ref=r_ref, send_sem=send_sem, recv_sem=recv_sem,
                                          device_id=(x, y, 1 - c), device_id_type=MESH)
        cp.start()
        cp.wait()

    return pl.pallas_call(
        body, name=name, out_shape=jax.ShapeDtypeStruct(p.shape, p.dtype),
        in_specs=[ANY], out_specs=ANY,
        scratch_shapes=[pltpu.SemaphoreType.DMA, pltpu.SemaphoreType.DMA],
    )(p)


def sum_devices(v_all, *, name):
    m_per = v_all.shape[0] // N_DEV

    def body(v_ref, o_ref):
        acc = v_ref[pl.ds(0, m_per), :]
        for d in range(1, N_DEV):
            acc = acc + v_ref[pl.ds(d * m_per, m_per), :]
        o_ref[...] = acc

    return pl.pallas_call(
        body, name=name, out_shape=jax.ShapeDtypeStruct((m_per, LANES), F32),
        in_specs=[WHOLE_VMEM], out_specs=WHOLE_VMEM,
    )(v_all)


WEIGHTS = ['ada_w', 'ada_b', 'norm_mix_g', 'norm_ffn_g', 'hy_w_in', 'hy_conv_w', 'hy_conv_b', 'hy_dt_bias', 'hy_a_log',
           'hy_d_skip', 'hy_ssm_norm_g', 'hy_w_out', 'rel_table', 'cv_w_pw1', 'cv_b_pw1', 'cv_w_dw', 'cv_b_dw', 'cv_ln_g',
           'cv_ln_b', 'cv_w_pw2', 'cv_b_pw2', 'ffn_w_gate', 'ffn_w_up', 'ffn_w_down', 'final_norm_g']
BIG = ('ada_w', 'hy_w_in', 'hy_w_out', 'cv_w_pw1', 'cv_w_pw2', 'ffn_w_gate', 'ffn_w_up', 'ffn_w_down')
SMALL_SHARDED = {'hy_conv_w': (1, 4, 3072), 'cv_b_pw1': (1, 2048), 'cv_w_dw': (1, 31, 1024), 'cv_b_dw': (1, 1024),
                 'cv_ln_g': (1, 1024), 'cv_ln_b': (1, 1024), 'cv_b_pw2': (1, 1024)}
SMALL_GRADS = {'ada_b': (2, 6144), 'norm_mix_g': (2, 1024), 'norm_ffn_g': (2, 1024), 'hy_conv_w': (1, 4, 3072),
               'hy_conv_b': (1, 3072), 'hy_dt_bias': (1, 32), 'hy_a_log': (1, 32), 'hy_d_skip': (1, 32),
               'hy_ssm_norm_g': (1, 2048), 'rel_table': (32, 48), 'cv_b_pw1': (1, 2048), 'cv_w_dw': (1, 31, 1024),
               'cv_b_dw': (1, 1024), 'cv_ln_g': (1, 1024), 'cv_ln_b': (1, 1024), 'cv_b_pw2': (1, 1024),
               'final_norm_g': (1024,), 'loss': (1,)}

PACK_LAYOUT = (('hy_in_t', 2568), ('hy_out', 768), ('pw1_t', 512), ('pw2', 256),
               ('gate_t0', 704), ('up_t0', 704), ('down0', 704), ('gate_t1', 704), ('up_t1', 704), ('down1', 704))
PACK_ROWS = 8448


def _pack_offsets(layout):
    off, out = 0, {}
    for nm, r in layout:
        out[nm] = (off, r)
        off += r
    return out


PACK_OFF = _pack_offsets(PACK_LAYOUT)
W_BATCHES = ((0, 2624), (2624, 5248), (5248, 6336), (6336, 7424), (7424, 8448))
GA_LAYOUT = PACK_LAYOUT[1:]
GA_ROWS = 5888
GA_OFF = _pack_offsets(GA_LAYOUT)
G_BATCHES = ((0, 2560), (2560, 3712), (3712, 4864), (4864, 5888))
GB_LAYOUT = PACK_LAYOUT[:1]
GB_ROWS = 2816


def pack_grads(g, layout, n_rows):
    cache = {}

    def rows_bf16(nm):
        if nm not in cache:
            cache[nm] = transpose(g[nm], name="grad_t_" + nm)
        return cache[nm]

    parts = []
    for key, r in layout:
        if key == 'hy_in_t':
            a = hy_from_cat(rows_bf16('hy_in_t'))
        elif key.startswith('gate_t'):
            a = rows_bf16('gu_t' + key[-1])[:FFN_HIDDEN]
        elif key.startswith('up_t'):
            a = rows_bf16('gu_t' + key[-1])[FFN_HIDDEN:]
        else:
            a = rows_bf16(key)
        parts.append(a.reshape(N_CHIPS, r, D))
    used = sum(r for _, r in layout)
    return jnp.concatenate(parts + [jnp.zeros((N_CHIPS, n_rows - used, D), BF16)], axis=1)


def unpack_weights(full, skip=()):
    def whole(nm):
        o, r = PACK_OFF[nm]
        return full[:, o:o + r].reshape(N_CHIPS * r, D)

    out = {"hy_out": whole('hy_out'), "pw1_t": whole('pw1_t'), "pw2": whole('pw2'),
           "gu_t": [jnp.concatenate([whole(f'gate_t{i}'), whole(f'up_t{i}')], axis=0) for i in range(2)],
           "down": [whole(f'down{i}') for i in range(2)]}
    if "hy_in_t" not in skip:
        out["hy_in_t"] = hy_to_cat(whole('hy_in_t'))
    return out


def _to_lanes(flat):
    n = flat.shape[0]
    m = -(-n // (8 * LANES)) * 8
    return jnp.pad(flat, (0, m * LANES - n)).reshape(m, LANES)


def _split(flat, shapes):
    out, off = {}, 0
    for nm, shp in shapes.items():
        n = int(np.prod(shp))
        out[nm] = flat[off:off + n].reshape(shp)
        off += n
    return out


def kernel(x, c, ada_w, ada_b, norm_mix_g, norm_ffn_g, hy_w_in, hy_conv_w, hy_conv_b, hy_dt_bias, hy_a_log, hy_d_skip, hy_ssm_norm_g, hy_w_out, rel_table, cv_w_pw1, cv_b_pw1, cv_w_dw, cv_b_dw, cv_ln_g, cv_ln_b, cv_w_pw2, cv_b_pw2, ffn_w_gate, ffn_w_up, ffn_w_down, final_norm_g, loss_target, m_ada_w, m_ada_b, m_norm_mix_g, m_norm_ffn_g, m_hy_w_in, m_hy_conv_w, m_hy_conv_b, m_hy_dt_bias, m_hy_a_log, m_hy_d_skip, m_hy_ssm_norm_g, m_hy_w_out, m_rel_table, m_cv_w_pw1, m_cv_b_pw1, m_cv_w_dw, m_cv_b_dw, m_cv_ln_g, m_cv_ln_b, m_cv_w_pw2, m_cv_b_pw2, m_ffn_w_gate, m_ffn_w_up, m_ffn_w_down, m_final_norm_g, v_ada_w, v_ada_b, v_norm_mix_g, v_norm_ffn_g, v_hy_w_in, v_hy_conv_w, v_hy_conv_b, v_hy_dt_bias, v_hy_a_log, v_hy_d_skip, v_hy_ssm_norm_g, v_hy_w_out, v_rel_table, v_cv_w_pw1, v_cv_b_pw1, v_cv_w_dw, v_cv_b_dw, v_cv_ln_g, v_cv_ln_b, v_cv_w_pw2, v_cv_b_pw2, v_ffn_w_gate, v_ffn_w_up, v_ffn_w_down, v_final_norm_g):
    args = (x, c, ada_w, ada_b, norm_mix_g, norm_ffn_g, hy_w_in, hy_conv_w, hy_conv_b, hy_dt_bias, hy_a_log, hy_d_skip, hy_ssm_norm_g, hy_w_out, rel_table, cv_w_pw1, cv_b_pw1, cv_w_dw, cv_b_dw, cv_ln_g, cv_ln_b, cv_w_pw2, cv_b_pw2, ffn_w_gate, ffn_w_up, ffn_w_down, final_norm_g, loss_target, m_ada_w, m_ada_b, m_norm_mix_g, m_norm_ffn_g, m_hy_w_in, m_hy_conv_w, m_hy_conv_b, m_hy_dt_bias, m_hy_a_log, m_hy_d_skip, m_hy_ssm_norm_g, m_hy_w_out, m_rel_table, m_cv_w_pw1, m_cv_b_pw1, m_cv_w_dw, m_cv_b_dw, m_cv_ln_g, m_cv_ln_b, m_cv_w_pw2, m_cv_b_pw2, m_ffn_w_gate, m_ffn_w_up, m_ffn_w_down, m_final_norm_g, v_ada_w, v_ada_b, v_norm_mix_g, v_norm_ffn_g, v_hy_w_in, v_hy_conv_w, v_hy_conv_b, v_hy_dt_bias, v_hy_a_log, v_hy_d_skip, v_hy_ssm_norm_g, v_hy_w_out, v_rel_table, v_cv_w_pw1, v_cv_b_pw1, v_cv_w_dw, v_cv_b_dw, v_cv_ln_g, v_cv_ln_b, v_cv_w_pw2, v_cv_b_pw2, v_ffn_w_gate, v_ffn_w_up, v_ffn_w_down, v_final_norm_g)
    x_in, c_in = args[0], args[1]
    w = dict(zip(WEIGHTS, args[2:27], strict=True))
    tgt = args[27]
    m_in = dict(zip(WEIGHTS, args[28:53], strict=True))
    v_in = dict(zip(WEIGHTS, args[53:78], strict=True))
    xi, yi, ci = _place()
    chip = 2 * xi + yi
    dev = 2 * chip + ci

    cs = rowmap(f_silu, [c_in.reshape(8, LANES)], [], [F32], name="cond_silu", tr=8)[0]
    cs_all = allgather_small(cs, name="gather_cond").reshape(N_DEV, D)
    cs16 = jnp.pad(cs_all, ((0, 8), (0, 0)))
    modpart = jnp.stack([matmul(cs16, w['ada_w'][i], mode="nn", out_dtype=F32, name=f"ada_fwd{i}")[:N_DEV]
                         for i in range(2)], axis=1)
    shard_names = list(SMALL_SHARDED)
    payload = jnp.concatenate([modpart.reshape(-1)] + [w[nm].reshape(-1) for nm in shard_names])
    got = allgather_small(_to_lanes(payload), name="gather_mod").reshape(N_DEV, -1)[0::2]
    modparts = got[:, :modpart.size].reshape(N_CHIPS, N_DEV, 2, 1536)
    mine = lax.dynamic_index_in_dim(modparts, dev, axis=1, keepdims=False)
    mod = jnp.transpose(mine, (1, 0, 2)).reshape(2, 6 * D) + w['ada_b']
    mods = [[mod[i, j * D:(j + 1) * D].reshape(1, D) for j in range(6)] for i in range(2)]
    sp = {}
    off = modpart.size
    for nm in shard_names:
        shp = w[nm].shape
        n = int(np.prod(shp))
        parts = got[:, off:off + n].reshape((N_CHIPS,) + shp)
        sp[nm] = jnp.concatenate([parts[s] for s in range(N_CHIPS)], axis=-1)
        off += n

    def rows_of(nm, i=None):
        a = w[nm][0 if i is None else i]
        return (a.T if nm in ('hy_w_in', 'cv_w_pw1', 'ffn_w_gate', 'ffn_w_up') else a).astype(BF16)

    pieces = [rows_of('hy_w_in'), rows_of('hy_w_out'), rows_of('cv_w_pw1'), rows_of('cv_w_pw2')]
    for i in range(2):
        pieces += [rows_of('ffn_w_gate', i), rows_of('ffn_w_up', i), rows_of('ffn_w_down', i)]
    n_rows = sum(p.shape[0] for p in pieces)
    pack = jnp.concatenate(pieces + [jnp.zeros((PACK_ROWS - n_rows, D), BF16)], axis=0)
    full = side_call(GatherRows(pack, lax.empty((N_CHIPS, PACK_ROWS, D), BF16), *W_BATCHES[0]), name="gather_weights")
    o_in, r_in = PACK_OFF['hy_in_t']
    wts = {"hy_in_t": hy_to_cat(full[:, o_in:o_in + r_in].reshape(N_CHIPS * r_in, D))}
    comm = {"pack": pack, "full": full}

    sp = {"norm_mix_g": [w['norm_mix_g'][i].reshape(1, D) for i in range(2)],
          "norm_ffn_g": [w['norm_ffn_g'][i].reshape(1, D) for i in range(2)],
          "hy_conv_w": sp['hy_conv_w'][0], "hy_conv_b": w['hy_conv_b'],
          "hy_dt_bias": w['hy_dt_bias'].reshape(SSM_HEADS, 1), "hy_a_log": w['hy_a_log'].reshape(SSM_HEADS, 1),
          "hy_d_skip": w['hy_d_skip'].reshape(SSM_HEADS, 1), "hy_ssm_norm_g": w['hy_ssm_norm_g'],
          "rel_table": w['rel_table'], "cv_b_pw1": sp['cv_b_pw1'], "cv_w_dw": sp['cv_w_dw'][0], "cv_b_dw": sp['cv_b_dw'],
          "cv_ln_g": sp['cv_ln_g'], "cv_ln_b": sp['cv_ln_b'], "cv_b_pw2": sp['cv_b_pw2'],
          "final_norm_g": w['final_norm_g'].reshape(1, D)}

    loss_rows, grad_x, g, dmods = device_step(x_in[0], tgt[0], mods, wts, sp, comm)

    dmod = jnp.stack([jnp.concatenate([d.reshape(-1) for d in dmods[i]]) for i in range(2)])
    small = {'ada_b': dmod, 'norm_mix_g': jnp.stack([g[f'norm_mix_g{i}'].reshape(-1) for i in range(2)]),
             'norm_ffn_g': jnp.stack([g[f'norm_ffn_g{i}'].reshape(-1) for i in range(2)]),
             'loss': jnp.sum(loss_rows).reshape(1)}
    for nm in SMALL_GRADS:
        if nm not in small:
            small[nm] = g[nm]
    vec = _to_lanes(jnp.concatenate([small[nm].reshape(-1) for nm in SMALL_GRADS]))
    vec_all = allgather_small(vec, name="gather_small_grads")
    tot = _split(sum_devices(vec_all, name="sum_small_grads").reshape(-1), SMALL_GRADS)
    dmod_all = vec_all.reshape(N_DEV, -1)[:, :2 * 6 * D].reshape(N_DEV, 2, 6 * D)

    recv = comm["recv"]
    own_a = lax.dynamic_index_in_dim(comm["ga"], chip, axis=0, keepdims=False)
    part_a = rowmap(f_sum4, [own_a, recv[0], recv[1], recv[2]], [], [F32], name="sum_chip_grads")[0]
    red_a = rowmap(f_add, [part_a, sibling_swap(part_a, name="swap_grads")], [], [F32], name="sum_core_grads")[0]
    gb = pack_grads(g, GB_LAYOUT, GB_ROWS)
    half = GB_ROWS // 2
    theirs = swap_halves(gb, name="swap_in_halves")
    ours = lax.dynamic_slice_in_dim(gb, ci * half, half, axis=1)
    chip_sum = rowmap(f_add, [ours.reshape(N_CHIPS * half, D), theirs.reshape(N_CHIPS * half, D)], [], [BF16],
                      name="sum_in_cores")[0].reshape(N_CHIPS, half, D)
    own_b, recv_b = scatter_chips(chip_sum, name="scatter_in")
    mine_half = rowmap(f_sum4, [own_b, recv_b[0], recv_b[1], recv_b[2]], [], [F32], name="sum_in_chips")[0]
    its_half = sibling_swap(mine_half, name="swap_in")
    red_b = jnp.concatenate([jnp.where(ci == 0, mine_half, its_half), jnp.where(ci == 0, its_half, mine_half)], axis=0)

    def shard_grad(nm, i=None):
        key = {'hy_w_in': 'hy_in_t', 'hy_w_out': 'hy_out', 'cv_w_pw1': 'pw1_t', 'cv_w_pw2': 'pw2'}.get(nm)
        if key is None:
            key = {'ffn_w_gate': 'gate_t', 'ffn_w_up': 'up_t', 'ffn_w_down': 'down'}[nm] + str(i)
        if key == 'hy_in_t':
            a = red_b[:PACK_OFF[key][1]]
        else:
            o, r = GA_OFF[key]
            a = red_a[o:o + r]
        return a.T if key.endswith('_t') or key[:-1].endswith('_t') else a

    grads = {}
    grads['hy_w_in'] = shard_grad('hy_w_in')[None]
    grads['hy_w_out'] = shard_grad('hy_w_out')[None]
    grads['cv_w_pw1'] = shard_grad('cv_w_pw1')[None]
    grads['cv_w_pw2'] = shard_grad('cv_w_pw2')[None]
    for nm in ('ffn_w_gate', 'ffn_w_up', 'ffn_w_down'):
        grads[nm] = jnp.stack([shard_grad(nm, i) for i in range(2)])
    cs16 = jnp.pad(cs_all, ((0, 8), (0, 0)))
    dm_mine = lax.dynamic_slice_in_dim(dmod_all, chip * 1536, 1536, axis=2)
    dm16 = jnp.pad(dm_mine, ((0, 8), (0, 0), (0, 0)))
    grads['ada_w'] = jnp.stack([matmul(cs16, dm16[:, i], mode="tn", out_dtype=F32, name=f"ada_dw{i}") for i in range(2)])
    for nm, shp in SMALL_GRADS.items():
        if nm == 'loss':
            continue
        if nm in SMALL_SHARDED:
            n = w[nm].shape[-1]
            grads[nm] = lax.dynamic_slice_in_dim(tot[nm], chip * n, n, axis=len(shp) - 1)
        else:
            grads[nm] = tot[nm].reshape(w[nm].shape)

    delta, new_m, new_v = {}, {}, {}
    for nm in BIG:
        delta[nm], new_m[nm], new_v[nm] = adamw(w[nm], grads[nm], m_in[nm], v_in[nm], name="adamw_" + nm)
    smalls = [nm for nm in WEIGHTS if nm not in BIG]
    packed = [_to_lanes(jnp.concatenate([d[nm].reshape(-1) for nm in smalls])) for d in (w, grads, m_in, v_in)]
    res = rowmap(f_adamw, packed, [], [F32] * 3, name="adamw_small", tr=_rows_tile(packed[0].shape[0]))
    for d, r in zip((delta, new_m, new_v), res, strict=True):
        d.update(_split(r.reshape(-1), {nm: w[nm].shape for nm in smalls}))

    loss = tot['loss'].reshape(())
    return (loss, grad_x[None], *[grads[nm] for nm in WEIGHTS], *[delta[nm] for nm in WEIGHTS],
            *[new_m[nm] for nm in WEIGHTS], *[new_v[nm] for nm in WEIGHTS])
```

```python
import functools
import math

import jax
import jax.numpy as jnp
import numpy as np
from jax import lax
from jax.experimental import pallas as pl
from jax.experimental.pallas import tpu as pltpu

F32 = jnp.float32
BF16 = jnp.bfloat16
MESH = pl.DeviceIdType.MESH

D = 1024
S = 4096
EPS = 1e-6
SSM_INNER = 2048
SSM_HEADS = 32
SSM_HDIM = 64
SSM_GROUPS = 4
SSM_STATE = 128
SSM_CONVK = 4
SSM_CONV_DIM = 3072
CHUNK = 128
N_CHUNKS = S // CHUNK
ATT_HEADS = 16
ATT_HDIM = 64
ATT_PATTERNS = ((128, 1), (512, 4), (2048, 16))
ATT_BLK = 128
REL_BUCKETS = 32
REL_MAX_DIST = 2048
CONV_WIDTH = 31
FFN_HIDDEN = 2816
N_CHIPS = 4
N_DEV = 8
ADAM_LR, ADAM_B1, ADAM_B2, ADAM_EPS, ADAM_WD, ADAM_STEP = 0.001, 0.9, 0.999, 1e-08, 0.01, 10

VMEM_LIMIT_BYTES = 56 * 1024 * 1024
LANES = 128


def _cparams(*sem):
    return pltpu.CompilerParams(dimension_semantics=sem, vmem_limit_bytes=VMEM_LIMIT_BYTES)


def _pick(n, cap, mult=LANES):
    best = None
    for t in range(mult, min(n, cap) + 1, mult):
        if n % t == 0:
            best = t
    return best or n


def _dot(a, b, ca, cb):
    return lax.dot_general(a.astype(BF16), b.astype(BF16), (((ca,), (cb,)), ((), ())), preferred_element_type=F32)


@jax.custom_vjp
def mm(a, b):
    return _dot(a, b, 1, 0)


def _mm_fwd(a, b):
    return _dot(a, b, 1, 0), (a, b)


def _mm_bwd(res, g):
    a, b = res
    return _dot(g, b, 1, 1).astype(a.dtype), _dot(a, g, 0, 0).astype(b.dtype)


mm.defvjp(_mm_fwd, _mm_bwd)


@jax.custom_vjp
def mm_nt(a, b):
    return _dot(a, b, 1, 1)


def _mm_nt_fwd(a, b):
    return _dot(a, b, 1, 1), (a, b)


def _mm_nt_bwd(res, g):
    a, b = res
    return _dot(g, b, 1, 0).astype(a.dtype), _dot(g, a, 0, 0).astype(b.dtype)


mm_nt.defvjp(_mm_nt_fwd, _mm_nt_bwd)


@jax.custom_vjp
def mm_tn(a, b):
    return _dot(a, b, 0, 0)


def _mm_tn_fwd(a, b):
    return _dot(a, b, 0, 0), (a, b)


def _mm_tn_bwd(res, g):
    a, b = res
    return _dot(b, g, 1, 1).astype(a.dtype), _dot(a, g, 1, 0).astype(b.dtype)


mm_tn.defvjp(_mm_tn_fwd, _mm_tn_bwd)


def matmul(a, b, *, mode, out_dtype, name, n=None, b_off=0, tm_cap=1024, tn_cap=512, tk_cap=1536, side=None):
    if mode == "tn":
        k_dim, m_dim = a.shape
    else:
        m_dim, k_dim = a.shape
    n_dim = n if n is not None else (b.shape[0] if mode == "nt" else b.shape[1])
    tm = m_dim if m_dim < LANES else _pick(m_dim, tm_cap)
    tn = _pick(n_dim, tn_cap)
    tk = k_dim if k_dim < LANES else _pick(k_dim, tk_cap)
    assert m_dim % tm == 0 and n_dim % tn == 0 and k_dim % tk == 0 and b_off % tn == 0
    nk = k_dim // tk
    off = b_off // tn
    if mode == "nn":
        a_spec = pl.BlockSpec((tm, tk), lambda i, j, k: (i, k))
        b_spec = pl.BlockSpec((tk, tn), lambda i, j, k: (k, j))
        ca, cb = 1, 0
    elif mode == "nt":
        a_spec = pl.BlockSpec((tm, tk), lambda i, j, k: (i, k))
        b_spec = pl.BlockSpec((tn, tk), lambda i, j, k: (j + off, k))
        ca, cb = 1, 1
    else:
        a_spec = pl.BlockSpec((tk, tm), lambda i, j, k: (k, i))
        b_spec = pl.BlockSpec((tk, tn), lambda i, j, k: (k, j))
        ca, cb = 0, 0

    def body(a_ref, b_ref, o_ref, acc_ref):
        part = _dot(a_ref[...], b_ref[...], ca, cb)
        if nk == 1:
            o_ref[...] = part.astype(o_ref.dtype)
        else:
            k = pl.program_id(2)

            @pl.when(k == 0)
            def _():
                acc_ref[...] = part

            @pl.when(k > 0)
            def _():
                acc_ref[...] += part

            @pl.when(k == nk - 1)
            def _():
                o_ref[...] = acc_ref[...].astype(o_ref.dtype)

    (out,), side_dst = grid_call(
        body, (a, b), name=name,
        out_shape=[jax.ShapeDtypeStruct((m_dim, n_dim), out_dtype)],
        grid=(m_dim // tm, n_dim // tn, nk),
        in_specs=[a_spec, b_spec],
        out_specs=[pl.BlockSpec((tm, tn), lambda i, j, k: (i, j))],
        scratch_shapes=[pltpu.VMEM((tm, tn), F32)],
        semantics=("parallel", "parallel", "arbitrary"), side=side)
    return out if side is None else (out, side_dst)


def _f32(xs):
    return [x.astype(F32) for x in xs]


def rowmap(f, rows, consts, out_dtypes, *, name, tr=256):
    r_dim = rows[0].shape[0]
    tr = _pick(r_dim, tr, mult=8)
    assert r_dim % tr == 0
    nr, nc = len(rows), len(consts)
    outs = jax.eval_shape(lambda *xs: f(*xs), *[jax.ShapeDtypeStruct((tr, x.shape[1]), F32) for x in rows],
                          *[jax.ShapeDtypeStruct(x.shape, F32) for x in consts])

    def body(*refs):
        res = f(*_f32([r[...] for r in refs[:nr + nc]]))
        for o_ref, o in zip(refs[nr + nc:], res, strict=True):
            o_ref[...] = o.astype(o_ref.dtype)

    return pl.pallas_call(
        body, name=name,
        out_shape=[jax.ShapeDtypeStruct((r_dim, o.shape[1]), dt) for o, dt in zip(outs, out_dtypes, strict=True)],
        grid=(r_dim // tr,),
        in_specs=[pl.BlockSpec((tr, x.shape[1]), lambda i: (i, 0)) for x in rows]
        + [pl.BlockSpec(x.shape, lambda i: (0, 0)) for x in consts],
        out_specs=[pl.BlockSpec((tr, o.shape[1]), lambda i: (i, 0)) for o in outs],
        compiler_params=_cparams("parallel"),
    )(*rows, *consts)


def rowmap_bwd(f, rows, consts, cts, *, name, row_grad, row_dtypes=None, tr=256, emit=(), row_add=None):
    r_dim = rows[0].shape[0]
    tr = _pick(r_dim, tr, mult=8)
    assert r_dim % tr == 0
    nr, nc, nct = len(rows), len(consts), len(cts)
    gi = [i for i, flag in enumerate(row_grad) if flag]
    row_dtypes = row_dtypes or [F32] * len(gi)
    row_add = row_add or [None] * len(gi)
    adds = [a for a in row_add if a is not None]
    outs = jax.eval_shape(lambda *xs: f(*xs), *[jax.ShapeDtypeStruct((tr, x.shape[1]), F32) for x in rows],
                          *[jax.ShapeDtypeStruct(x.shape, F32) for x in consts])

    def body(*refs):
        ins = _f32([r[...] for r in refs[:nr + nc]])
        ct = _f32([r[...] for r in refs[nr + nc:nr + nc + nct]])
        add_refs = list(refs[nr + nc + nct:nr + nc + nct + len(adds)])
        o_refs = refs[nr + nc + nct + len(adds):]
        res, vjp = jax.vjp(f, *ins)
        grads = vjp(tuple(ct))
        for o_ref, i, a in zip(o_refs[:len(gi)], gi, row_add):
            g = grads[i] if a is None else grads[i] + add_refs.pop(0)[...].astype(F32)
            o_ref[...] = g.astype(o_ref.dtype)
        first = pl.program_id(0) == 0
        for o_ref, g in zip(o_refs[len(gi):len(gi) + nc], grads[nr:]):
            @pl.when(first)
            def _(o_ref=o_ref, g=g):
                o_ref[...] = g

            @pl.when(jnp.logical_not(first))
            def _(o_ref=o_ref, g=g):
                o_ref[...] += g
        for o_ref, i in zip(o_refs[len(gi) + nc:], emit):
            o_ref[...] = res[i].astype(o_ref.dtype)

    out_shape = ([jax.ShapeDtypeStruct(rows[i].shape, dt) for i, dt in zip(gi, row_dtypes, strict=True)]
                 + [jax.ShapeDtypeStruct(x.shape, F32) for x in consts]
                 + [jax.ShapeDtypeStruct((r_dim, outs[i].shape[1]), F32) for i in emit])
    out_specs = ([pl.BlockSpec((tr, rows[i].shape[1]), lambda i_: (i_, 0)) for i in gi]
                 + [pl.BlockSpec(x.shape, lambda i_: (0, 0)) for x in consts]
                 + [pl.BlockSpec((tr, outs[i].shape[1]), lambda i_: (i_, 0)) for i in emit])
    res = pl.pallas_call(
        body, name=name,
        out_shape=out_shape,
        grid=(r_dim // tr,),
        in_specs=[pl.BlockSpec((tr, x.shape[1]), lambda i: (i, 0)) for x in rows]
        + [pl.BlockSpec(x.shape, lambda i: (0, 0)) for x in consts]
        + [pl.BlockSpec((tr, x.shape[1]), lambda i: (i, 0)) for x in list(cts) + adds],
        out_specs=out_specs,
        compiler_params=_cparams("arbitrary"),
    )(*rows, *consts, *cts, *adds)
    return res[:len(gi)], res[len(gi):len(gi) + nc], res[len(gi) + nc:]


def transpose(a, *, name, out_dtype=BF16, tr=512, tc=512):
    r_dim, c_dim = a.shape
    tr, tc = _pick(r_dim, tr), _pick(c_dim, tc)

    def body(a_ref, o_ref):
        o_ref[...] = a_ref[...].astype(F32).T.astype(o_ref.dtype)

    return pl.pallas_call(
        body, name=name, out_shape=jax.ShapeDtypeStruct((c_dim, r_dim), out_dtype),
        grid=(r_dim // tr, c_dim // tc),
        in_specs=[pl.BlockSpec((tr, tc), lambda i, j: (i, j))],
        out_specs=pl.BlockSpec((tc, tr), lambda i, j: (j, i)),
        compiler_params=_cparams("parallel", "parallel"),
    )(a)


CONV_HALO = 32
CONV_CHUNK = 256


def conv_fwd(x, w, b, *, name, cb=256):
    s_dim, c_dim = x.shape
    taps = w.shape[0]
    assert taps - 1 <= CONV_HALO and s_dim % CONV_CHUNK == 0 and c_dim % cb == 0
    n_chunks = s_dim // CONV_CHUNK
    ext = CONV_CHUNK + CONV_HALO

    def body(x_ref, w_ref, b_ref, o_ref, xp_ref):
        xp_ref[pl.ds(0, CONV_HALO), :] = jnp.zeros((CONV_HALO, cb), F32)
        xp_ref[pl.ds(CONV_HALO, s_dim), :] = x_ref[...].astype(F32)
        wv = w_ref[...].astype(F32)
        bv = b_ref[...].astype(F32)

        def chunk(t, carry):
            base = pl.multiple_of(t * CONV_CHUNK, CONV_CHUNK)
            xe = xp_ref[pl.ds(base, ext), :]
            acc = jnp.broadcast_to(bv, (CONV_CHUNK, cb))
            for j in range(taps):
                sh = xe if j == 0 else pltpu.roll(xe, shift=j, axis=0)
                acc = acc + wv[taps - 1 - j:taps - j, :] * sh[CONV_HALO:, :]
            o_ref[pl.ds(base, CONV_CHUNK), :] = acc
            return carry

        lax.fori_loop(0, n_chunks, chunk, 0)

    return pl.pallas_call(
        body, name=name,
        out_shape=jax.ShapeDtypeStruct((s_dim, c_dim), F32),
        grid=(c_dim // cb,),
        in_specs=[pl.BlockSpec((s_dim, cb), lambda i: (0, i)), pl.BlockSpec((taps, cb), lambda i: (0, i)),
                  pl.BlockSpec((1, cb), lambda i: (0, i))],
        out_specs=pl.BlockSpec((s_dim, cb), lambda i: (0, i)),
        scratch_shapes=[pltpu.VMEM((s_dim + CONV_HALO, cb), F32)],
        compiler_params=_cparams("parallel"),
    )(x, w, b)


def conv_bwd(x, w, g, *, name, cb=256):
    s_dim, c_dim = x.shape
    taps = w.shape[0]
    n_chunks = s_dim // CONV_CHUNK
    ext = CONV_CHUNK + CONV_HALO

    def rows8(a):
        return jnp.sum(a.reshape(CONV_CHUNK // 8, 8, cb), axis=0)

    def body(x_ref, w_ref, g_ref, dx_ref, dw_ref, db_ref, xp_ref, gp_ref, acc_ref):
        xp_ref[pl.ds(0, CONV_HALO), :] = jnp.zeros((CONV_HALO, cb), F32)
        xp_ref[pl.ds(CONV_HALO, s_dim), :] = x_ref[...].astype(F32)
        gp_ref[pl.ds(0, s_dim), :] = g_ref[...].astype(F32)
        gp_ref[pl.ds(s_dim, CONV_HALO), :] = jnp.zeros((CONV_HALO, cb), F32)
        acc_ref[...] = jnp.zeros_like(acc_ref)
        wv = w_ref[...].astype(F32)

        def chunk(t, carry):
            base = pl.multiple_of(t * CONV_CHUNK, CONV_CHUNK)
            xe = xp_ref[pl.ds(base, ext), :]
            ge = gp_ref[pl.ds(base, ext), :]
            gc = ge[:CONV_CHUNK, :]
            dx = jnp.zeros((CONV_CHUNK, cb), F32)
            for j in range(taps):
                xs = xe if j == 0 else pltpu.roll(xe, shift=j, axis=0)
                gs = ge if j == 0 else pltpu.roll(ge, shift=ext - j, axis=0)
                k = taps - 1 - j
                dx = dx + wv[k:k + 1, :] * gs[:CONV_CHUNK, :]
                acc_ref[8 * k:8 * k + 8, :] += rows8(gc * xs[CONV_HALO:, :])
            acc_ref[8 * taps:8 * taps + 8, :] += rows8(gc)
            dx_ref[pl.ds(base, CONV_CHUNK), :] = dx
            return carry

        lax.fori_loop(0, n_chunks, chunk, 0)
        sums = jnp.sum(acc_ref[...].reshape(taps + 1, 8, cb), axis=1)
        dw_ref[...] = sums[0:taps, :]
        db_ref[...] = sums[taps:taps + 1, :]

    return pl.pallas_call(
        body, name=name,
        out_shape=[jax.ShapeDtypeStruct((s_dim, c_dim), F32), jax.ShapeDtypeStruct((taps, c_dim), F32),
                   jax.ShapeDtypeStruct((1, c_dim), F32)],
        grid=(c_dim // cb,),
        in_specs=[pl.BlockSpec((s_dim, cb), lambda i: (0, i)), pl.BlockSpec((taps, cb), lambda i: (0, i)),
                  pl.BlockSpec((s_dim, cb), lambda i: (0, i))],
        out_specs=[pl.BlockSpec((s_dim, cb), lambda i: (0, i)), pl.BlockSpec((taps, cb), lambda i: (0, i)),
                   pl.BlockSpec((1, cb), lambda i: (0, i))],
        scratch_shapes=[pltpu.VMEM((s_dim + CONV_HALO, cb), F32), pltpu.VMEM((s_dim + CONV_HALO, cb), F32),
                        pltpu.VMEM((8 * (taps + 1), cb), F32)],
        compiler_params=_cparams("parallel"),
    )(x, w, g)


def _iota2(n, axis):
    return lax.broadcasted_iota(jnp.int32, (n, n), axis)


def _to_col(row):
    n = row.shape[1]
    return jnp.sum(jnp.where(_iota2(n, 0) == _iota2(n, 1), jnp.broadcast_to(row, (n, n)), 0.0), axis=1, keepdims=True)


def _softplus(x):
    return jnp.maximum(x, 0.0) + jnp.log(1.0 + jnp.exp(-jnp.abs(x)))


def ssd_heads(x, dtraw, dt_bias, a_log, dskip, bm, cm, prev):
    h, q, _ = x.shape
    n = bm.shape[1]
    li = lax.broadcasted_iota(jnp.int32, (1, q, q), 1)
    si = lax.broadcasted_iota(jnp.int32, (1, q, q), 2)

    def to_col(row):
        return jnp.sum(jnp.where(li == si, jnp.broadcast_to(row, (h, q, q)), 0.0), axis=2, keepdims=True)

    dt_row = _softplus(dtraw + dt_bias)
    a_row = dt_row * (-jnp.exp(a_log))
    a_col = to_col(a_row)
    acs_col = jnp.sum(jnp.where(si <= li, jnp.broadcast_to(a_row, (h, q, q)), 0.0), axis=2, keepdims=True)
    acs_row = jnp.sum(jnp.where(li <= si, jnp.broadcast_to(a_col, (h, q, q)), 0.0), axis=1, keepdims=True)
    total = jnp.sum(a_row, axis=2, keepdims=True)
    xdt = x * to_col(dt_row)
    lmat = jnp.exp(jnp.where(li >= si, acs_col - acs_row, -1e30))
    bmb = jnp.broadcast_to(bm[None], (h, q, n))
    cmb = jnp.broadcast_to(cm[None], (h, q, n))
    y = bmm(mm_nt(cm, bm)[None] * lmat, xdt)
    y = y + bmm_nt(cmb, prev) * jnp.exp(acs_col)
    y = y + dskip * x
    state = bmm_tn(xdt * jnp.exp(total - acs_col), bmb)
    return y, jnp.exp(total) * prev + state


HEADS_PER_GROUP = SSM_HEADS // SSM_GROUPS
BM_COL0 = SSM_INNER // SSM_STATE
CM_COL0 = BM_COL0 + SSM_GROUPS


def ssd_fwd(xs_hm, dtraw_t, dt_bias, a_log, dskip, xbc, side=None):
    hg = HEADS_PER_GROUP

    def body(x_ref, dt_ref, dtb_ref, al_ref, dk_ref, bm_ref, cm_ref, y_ref, prev_ref, state_ref):
        @pl.when(pl.program_id(1) == 0)
        def _():
            state_ref[...] = jnp.zeros_like(state_ref)

        prev = state_ref[...]
        prev_ref[0] = prev
        y, nxt = ssd_heads(x_ref[...], dt_ref[...], dtb_ref[...], al_ref[...], dk_ref[...], bm_ref[...], cm_ref[...], prev)
        y_ref[...] = y
        state_ref[...] = nxt

    hp = pl.BlockSpec((hg, 1, 1), lambda g, c: (g, 0, 0))
    dtraw_t, dt_bias, a_log, dskip = [a.reshape(SSM_HEADS, 1, -1) for a in (dtraw_t, dt_bias, a_log, dskip)]
    return grid_call(
        body, (xs_hm, dtraw_t, dt_bias, a_log, dskip, xbc, xbc), name="ssd_fwd",
        out_shape=[jax.ShapeDtypeStruct((SSM_HEADS, S, SSM_HDIM), F32),
                   jax.ShapeDtypeStruct((N_CHUNKS, SSM_HEADS, SSM_HDIM, SSM_STATE), F32)],
        grid=(SSM_GROUPS, N_CHUNKS),
        in_specs=[pl.BlockSpec((hg, CHUNK, SSM_HDIM), lambda g, c: (g, c, 0)),
                  pl.BlockSpec((hg, 1, CHUNK), lambda g, c: (g, 0, c)), hp, hp, hp,
                  pl.BlockSpec((CHUNK, SSM_STATE), lambda g, c: (c, BM_COL0 + g)),
                  pl.BlockSpec((CHUNK, SSM_STATE), lambda g, c: (c, CM_COL0 + g))],
        out_specs=[pl.BlockSpec((hg, CHUNK, SSM_HDIM), lambda g, c: (g, c, 0)),
                   pl.BlockSpec((1, hg, SSM_HDIM, SSM_STATE), lambda g, c: (c, g, 0, 0))],
        scratch_shapes=[pltpu.VMEM((hg, SSM_HDIM, SSM_STATE), F32)],
        semantics=("parallel", "arbitrary"), side=side)


def ssd_bwd(xs_hm, dtraw_t, dt_bias, a_log, dskip, xbc, prev_all, dy_hm, side=None):
    hg = HEADS_PER_GROUP
    last = N_CHUNKS - 1

    def body(x_ref, dt_ref, dtb_ref, al_ref, dk_ref, bm_ref, cm_ref, prev_ref, dy_ref,
             dx_ref, ddt_ref, ddtb_ref, dal_ref, ddk_ref, dbm_ref, dcm_ref, dstate_ref):
        @pl.when(pl.program_id(1) == 0)
        def _():
            dstate_ref[...] = jnp.zeros_like(dstate_ref)
            ddtb_ref[...] = jnp.zeros_like(ddtb_ref)
            dal_ref[...] = jnp.zeros_like(dal_ref)
            ddk_ref[...] = jnp.zeros_like(ddk_ref)

        _, vjp = jax.vjp(ssd_heads, x_ref[...], dt_ref[...], dtb_ref[...], al_ref[...], dk_ref[...], bm_ref[...],
                         cm_ref[...], prev_ref[0])
        dx, ddt, ddtb, dal, ddk, dbm, dcm, dprev = vjp((dy_ref[...], dstate_ref[...]))
        dx_ref[...] = dx
        ddt_ref[...] = ddt
        ddtb_ref[...] += ddtb
        dal_ref[...] += dal
        ddk_ref[...] += ddk
        dbm_ref[...] = dbm
        dcm_ref[...] = dcm
        dstate_ref[...] = dprev

    hp = pl.BlockSpec((hg, 1, 1), lambda g, c: (g, 0, 0))
    xspec = pl.BlockSpec((hg, CHUNK, SSM_HDIM), lambda g, c: (g, last - c, 0))
    tspec = pl.BlockSpec((hg, 1, CHUNK), lambda g, c: (g, 0, last - c))
    gspec = pl.BlockSpec((CHUNK, SSM_STATE), lambda g, c: (last - c, g))
    dtraw_t, dt_bias, a_log, dskip = [a.reshape(SSM_HEADS, 1, -1) for a in (dtraw_t, dt_bias, a_log, dskip)]
    res, side_dst = grid_call(
        body, (xs_hm, dtraw_t, dt_bias, a_log, dskip, xbc, xbc, prev_all, dy_hm), name="ssd_bwd",
        out_shape=[jax.ShapeDtypeStruct((SSM_HEADS, S, SSM_HDIM), F32), jax.ShapeDtypeStruct((SSM_HEADS, 1, S), F32),
                   jax.ShapeDtypeStruct((SSM_HEADS, 1, 1), F32), jax.ShapeDtypeStruct((SSM_HEADS, 1, 1), F32),
                   jax.ShapeDtypeStruct((SSM_HEADS, 1, 1), F32),
                   jax.ShapeDtypeStruct((S, SSM_GROUPS * SSM_STATE), F32),
                   jax.ShapeDtypeStruct((S, SSM_GROUPS * SSM_STATE), F32)],
        grid=(SSM_GROUPS, N_CHUNKS),
        in_specs=[xspec, tspec, hp, hp, hp,
                  pl.BlockSpec((CHUNK, SSM_STATE), lambda g, c: (last - c, BM_COL0 + g)),
                  pl.BlockSpec((CHUNK, SSM_STATE), lambda g, c: (last - c, CM_COL0 + g)),
                  pl.BlockSpec((1, hg, SSM_HDIM, SSM_STATE), lambda g, c: (last - c, g, 0, 0)), xspec],
        out_specs=[xspec, tspec, hp, hp, hp, gspec, gspec],
        scratch_shapes=[pltpu.VMEM((hg, SSM_HDIM, SSM_STATE), F32)],
        semantics=("parallel", "arbitrary"), side=side)
    return [res[0]] + [r.reshape(SSM_HEADS, -1) for r in res[1:5]] + list(res[5:]), side_dst


ATT_HB = 8


def _bdot(a, b, ca, cb):
    return lax.dot_general(a.astype(BF16), b.astype(BF16), (((ca,), (cb,)), ((0,), (0,))), preferred_element_type=F32)


@jax.custom_vjp
def bmm(a, b):
    return _bdot(a, b, 2, 1)


def _bmm_fwd(a, b):
    return _bdot(a, b, 2, 1), (a, b)


def _bmm_bwd(res, g):
    a, b = res
    return _bdot(g, b, 2, 2).astype(a.dtype), _bdot(a, g, 1, 1).astype(b.dtype)


bmm.defvjp(_bmm_fwd, _bmm_bwd)


@jax.custom_vjp
def bmm_nt(a, b):
    return _bdot(a, b, 2, 2)


def _bmm_nt_fwd(a, b):
    return _bdot(a, b, 2, 2), (a, b)


def _bmm_nt_bwd(res, g):
    a, b = res
    return _bdot(g, b, 2, 1).astype(a.dtype), _bdot(g, a, 1, 1).astype(b.dtype)


bmm_nt.defvjp(_bmm_nt_fwd, _bmm_nt_bwd)


@jax.custom_vjp
def bmm_tn(a, b):
    return _bdot(a, b, 1, 1)


def _bmm_tn_fwd(a, b):
    return _bdot(a, b, 1, 1), (a, b)


def _bmm_tn_bwd(res, g):
    a, b = res
    return _bdot(b, g, 2, 2).astype(a.dtype), _bdot(a, g, 2, 1).astype(b.dtype)


bmm_tn.defvjp(_bmm_tn_fwd, _bmm_tn_bwd)


def att_heads(q, kp, kc, vp, vc, bias_p, bias_c, has_prev):
    h, b, dh = q.shape
    i = lax.broadcasted_iota(jnp.int32, (1, b, b), 1)
    j = lax.broadcasted_iota(jnp.int32, (1, b, b), 2)
    scale = dh ** -0.5
    sp = jnp.where(jnp.logical_and(j >= i, has_prev), bmm_nt(q, kp) * scale + bias_p, -1e30)
    sc = jnp.where(j <= i, bmm_nt(q, kc) * scale + bias_c, -1e30)
    m = lax.stop_gradient(jnp.maximum(jnp.max(sp, axis=2, keepdims=True), jnp.max(sc, axis=2, keepdims=True)))
    pp, pc = jnp.exp(sp - m), jnp.exp(sc - m)
    l = jnp.sum(pp, axis=2, keepdims=True) + jnp.sum(pc, axis=2, keepdims=True)
    o = bmm(pp / l, vp) + bmm(pc / l, vc)
    return o, jnp.broadcast_to(m + jnp.log(l), (h, b, dh))


def _att_specs(nb):
    hb, blk = ATT_HB, ATT_BLK
    cur = pl.BlockSpec((hb, blk, ATT_HDIM), lambda h, b: (h, b, 0))
    prv = pl.BlockSpec((hb, blk, ATT_HDIM), lambda h, b: (h, jnp.maximum(b - 1, 0), 0))
    bias = pl.BlockSpec((hb, 2, blk, blk), lambda h, b: (h, 0, 0, 0))
    return cur, prv, bias


def att_fwd(q, k, v, bias, nb, *, name):
    cur, prv, bspec = _att_specs(nb)

    def body(q_ref, kp_ref, kc_ref, vp_ref, vc_ref, b_ref, o_ref, l_ref):
        has_prev = (pl.program_id(1) % nb) != 0
        o, lse = att_heads(q_ref[...], kp_ref[...], kc_ref[...], vp_ref[...], vc_ref[...], b_ref[:, 0], b_ref[:, 1],
                           has_prev)
        o_ref[...] = o
        l_ref[...] = lse

    shp = jax.ShapeDtypeStruct((ATT_HEADS, S, ATT_HDIM), F32)
    return pl.pallas_call(
        body, name=name, out_shape=[shp, shp],
        grid=(ATT_HEADS // ATT_HB, S // ATT_BLK),
        in_specs=[cur, prv, cur, prv, cur, bspec],
        out_specs=[cur, cur],
        compiler_params=_cparams("parallel", "parallel"),
    )(q, k, k, v, v, bias)


def att_bwd(q, k, v, bias, do, dlse, nb, *, name):
    cur, prv, bspec = _att_specs(nb)

    def body(q_ref, kp_ref, kc_ref, vp_ref, vc_ref, b_ref, do_ref, dl_ref,
             dq_ref, dkc_ref, dkp_ref, dvc_ref, dvp_ref, db_ref):
        has_prev = (pl.program_id(1) % nb) != 0

        @pl.when(pl.program_id(1) == 0)
        def _():
            db_ref[...] = jnp.zeros_like(db_ref)

        ins = _f32([q_ref[...], kp_ref[...], kc_ref[...], vp_ref[...], vc_ref[...]]) + [b_ref[:, 0], b_ref[:, 1]]
        _, vjp = jax.vjp(functools.partial(att_heads, has_prev=has_prev), *ins)
        dq, dkp, dkc, dvp, dvc, dbp, dbc = vjp((do_ref[...], dl_ref[...]))
        dq_ref[...] = dq
        dkc_ref[...] = dkc
        dkp_ref[...] = dkp
        dvc_ref[...] = dvc
        dvp_ref[...] = dvp
        db_ref[:, 0] += dbp
        db_ref[:, 1] += dbc

    shp = jax.ShapeDtypeStruct((ATT_HEADS, S, ATT_HDIM), F32)
    return pl.pallas_call(
        body, name=name,
        out_shape=[shp] * 5 + [jax.ShapeDtypeStruct((ATT_HEADS, 2, ATT_BLK, ATT_BLK), F32)],
        grid=(ATT_HEADS // ATT_HB, S // ATT_BLK),
        in_specs=[cur, prv, cur, prv, cur, bspec, cur, cur],
        out_specs=[cur] * 5 + [bspec],
        compiler_params=_cparams("parallel", "arbitrary"),
    )(q, k, k, v, v, bias, do, dlse)


def shift_add(cur, prev, nb, *, name):
    n_blocks = S // ATT_BLK

    def body(c_ref, p_ref, o_ref):
        nxt = pl.program_id(0) + 1
        keep = jnp.where((nxt % nb) != 0, 1.0, 0.0)
        o_ref[...] = c_ref[...] + keep * p_ref[...]

    return pl.pallas_call(
        body, name=name, out_shape=jax.ShapeDtypeStruct(cur.shape, F32),
        grid=(n_blocks,),
        in_specs=[pl.BlockSpec((ATT_HEADS, ATT_BLK, ATT_HDIM), lambda b: (0, b, 0)),
                  pl.BlockSpec((ATT_HEADS, ATT_BLK, ATT_HDIM), lambda b: (0, jnp.minimum(b + 1, n_blocks - 1), 0))],
        out_specs=pl.BlockSpec((ATT_HEADS, ATT_BLK, ATT_HDIM), lambda b: (0, b, 0)),
        compiler_params=_cparams("parallel"),
    )(cur, prev)


ATT_PAIRS = ATT_HEADS // 2
PAIR_W = 2 * ATT_HDIM


def att_pairs(q, kp, kc, vp, vc, bias, has_prev):
    t, b, w = q.shape
    i = lax.broadcasted_iota(jnp.int32, (1, b, b), 1)
    j = lax.broadcasted_iota(jnp.int32, (1, b, b), 2)
    first = lax.broadcasted_iota(jnp.int32, (1, 1, w), 2) < ATT_HDIM
    scale = ATT_HDIM ** -0.5
    outs, lses = [], []
    for ab in range(2):
        qh = jnp.where(first if ab == 0 else jnp.logical_not(first), q, 0.0)
        sp = jnp.where(jnp.logical_and(j >= i, has_prev), bmm_nt(qh, kp) * scale + bias[:, ab, 0], -1e30)
        sc = jnp.where(j <= i, bmm_nt(qh, kc) * scale + bias[:, ab, 1], -1e30)
        m = lax.stop_gradient(jnp.maximum(jnp.max(sp, axis=2, keepdims=True), jnp.max(sc, axis=2, keepdims=True)))
        pp, pc = jnp.exp(sp - m), jnp.exp(sc - m)
        l = jnp.sum(pp, axis=2, keepdims=True) + jnp.sum(pc, axis=2, keepdims=True)
        outs.append(bmm(pp / l, vp) + bmm(pc / l, vc))
        lses.append(jnp.broadcast_to(m + jnp.log(l), (t, b, w)))
    return jnp.where(first, outs[0], outs[1]), jnp.where(first, lses[0], lses[1])


def _pair_tiles(ref):
    return jnp.stack([ref[:, PAIR_W * t:PAIR_W * (t + 1)] for t in range(ATT_PAIRS)])


def _store_pair_tiles(ref, val):
    for t in range(ATT_PAIRS):
        ref[:, PAIR_W * t:PAIR_W * (t + 1)] = val[t].astype(ref.dtype)


def pair_bias(bias):
    return bias.reshape(ATT_PAIRS, 2, 2, ATT_BLK, ATT_BLK)


def att2_fwd(q, k, v, bias, nb, cols, *, name, side=None):
    n_blocks = S // ATT_BLK
    qc, kc, vc = cols

    def body(q_ref, k_ref, v_ref, b_ref, o_ref, l_ref, kprev, vprev):
        blk = pl.program_id(0)

        @pl.when(blk == 0)
        def _():
            kprev[...] = jnp.zeros_like(kprev)
            vprev[...] = jnp.zeros_like(vprev)

        k3, v3 = _pair_tiles(k_ref), _pair_tiles(v_ref)
        o, lse = att_pairs(_pair_tiles(q_ref), kprev[...], k3, vprev[...], v3, b_ref[...], (blk % nb) != 0)
        _store_pair_tiles(o_ref, o)
        _store_pair_tiles(l_ref, lse)
        kprev[...] = k3
        vprev[...] = v3

    def spec(c):
        return pl.BlockSpec((ATT_BLK, D), lambda b: (b, c))

    shp = jax.ShapeDtypeStruct((S, D), F32)
    return grid_call(
        body, (q, k, v, bias), name=name, out_shape=[shp, shp], grid=(n_blocks,),
        in_specs=[spec(qc), spec(kc), spec(vc), pl.BlockSpec(bias.shape, lambda b: (0, 0, 0, 0, 0))],
        out_specs=[spec(0), spec(0)],
        scratch_shapes=[pltpu.VMEM((ATT_PAIRS, ATT_BLK, PAIR_W), BF16), pltpu.VMEM((ATT_PAIRS, ATT_BLK, PAIR_W), BF16)],
        semantics=("arbitrary",), side=side)


def att2_bwd(q, k, v, bias, do, dlse, nb, cols, *, name, side=None):
    n_blocks = S // ATT_BLK
    qc, kc, vc = cols

    def body(q_ref, k_ref, v_ref, b_ref, do_ref, dl_ref, dq_ref, dk_ref, dv_ref, db_ref, kprev, vprev, dk_own, dv_own):
        blk = pl.program_id(0)

        @pl.when(blk == 0)
        def _():
            for r in (kprev, vprev, dk_own, dv_own, db_ref):
                r[...] = jnp.zeros_like(r)

        @pl.when(blk < n_blocks)
        def _():
            k3, v3 = _pair_tiles(k_ref), _pair_tiles(v_ref)
            ins = _f32([_pair_tiles(q_ref), kprev[...], k3, vprev[...], v3]) + [b_ref[...]]
            _, vjp = jax.vjp(functools.partial(att_pairs, has_prev=(blk % nb) != 0), *ins)
            dq, dkp, dkc, dvp, dvc, db = vjp((_pair_tiles(do_ref), _pair_tiles(dl_ref)))
            _store_pair_tiles(dq_ref, dq)
            _store_pair_tiles(dk_ref, dk_own[...] + dkp)
            _store_pair_tiles(dv_ref, dv_own[...] + dvp)
            dk_own[...] = dkc
            dv_own[...] = dvc
            db_ref[...] += db
            kprev[...] = k3
            vprev[...] = v3

        @pl.when(blk == n_blocks)
        def _():
            _store_pair_tiles(dk_ref, dk_own[...])
            _store_pair_tiles(dv_ref, dv_own[...])

    def spec(c):
        return pl.BlockSpec((ATT_BLK, D), lambda b: (jnp.minimum(b, n_blocks - 1), c))

    late = pl.BlockSpec((ATT_BLK, D), lambda b: (jnp.maximum(b - 1, 0), 0))
    bspec = pl.BlockSpec(bias.shape, lambda b: (0, 0, 0, 0, 0))
    tile_f32 = pltpu.VMEM((ATT_PAIRS, ATT_BLK, PAIR_W), F32)
    tile_bf16 = pltpu.VMEM((ATT_PAIRS, ATT_BLK, PAIR_W), BF16)
    return grid_call(
        body, (q, k, v, bias, do, dlse), name=name,
        out_shape=[jax.ShapeDtypeStruct((S, D), BF16), jax.ShapeDtypeStruct((S, D), F32),
                   jax.ShapeDtypeStruct((S, D), F32), jax.ShapeDtypeStruct(bias.shape, F32)],
        grid=(n_blocks + 1,),
        in_specs=[spec(qc), spec(kc), spec(vc), bspec, spec(0), spec(0)],
        out_specs=[spec(0), late, late, bspec],
        scratch_shapes=[tile_bf16, tile_bf16, tile_f32, tile_f32],
        semantics=("arbitrary",), side=side)


def regroup(a, dil, inverse=False):
    if dil == 1:
        return a
    c_dim = a.shape[1]
    shape = (dil, S // dil, c_dim) if inverse else (S // dil, dil, c_dim)
    return jnp.transpose(a.reshape(shape), (1, 0, 2)).reshape(S, c_dim)


SSD_PAIRS = SSM_HEADS // 2
PAIRS_PER_GROUP = SSD_PAIRS // SSM_GROUPS
GROUP_W = HEADS_PER_GROUP * SSM_HDIM


def ssd_pairs(x, dtraw, dt_bias, a_log, dskip, bm, cm, prev):
    t, q, w = x.shape
    n = bm.shape[1]
    li = lax.broadcasted_iota(jnp.int32, (1, q, q), 1)
    si = lax.broadcasted_iota(jnp.int32, (1, q, q), 2)
    first_lane = lax.broadcasted_iota(jnp.int32, (1, 1, w), 2) < SSM_HDIM
    first_row = lax.broadcasted_iota(jnp.int32, (1, w, 1), 1) < SSM_HDIM

    def to_col(row):
        return jnp.sum(jnp.where(li == si, jnp.broadcast_to(row, (t, q, q)), 0.0), axis=2, keepdims=True)

    def lanes(a0, a1):
        return jnp.where(first_lane, a0, a1)

    dt_col, acs_col, total, lmat = [], [], [], []
    for ab in range(2):
        dt_row = _softplus(dtraw[ab] + dt_bias[ab])
        a_row = dt_row * (-jnp.exp(a_log[ab]))
        a_col = to_col(a_row)
        acs_c = jnp.sum(jnp.where(si <= li, jnp.broadcast_to(a_row, (t, q, q)), 0.0), axis=2, keepdims=True)
        acs_r = jnp.sum(jnp.where(li <= si, jnp.broadcast_to(a_col, (t, q, q)), 0.0), axis=1, keepdims=True)
        dt_col.append(to_col(dt_row))
        acs_col.append(acs_c)
        total.append(jnp.sum(a_row, axis=2, keepdims=True))
        lmat.append(jnp.exp(jnp.where(li >= si, acs_c - acs_r, -1e30)))
    cb = mm_nt(cm, bm)[None]
    bmb = jnp.broadcast_to(bm[None], (t, q, n))
    cmb = jnp.broadcast_to(cm[None], (t, q, n))
    xdt = x * lanes(dt_col[0], dt_col[1])
    y = lanes(bmm(cb * lmat[0], xdt), bmm(cb * lmat[1], xdt))
    y = y + bmm_nt(cmb, prev) * lanes(jnp.exp(acs_col[0]), jnp.exp(acs_col[1]))
    y = y + lanes(dskip[0], dskip[1]) * x
    state = bmm_tn(xdt * lanes(jnp.exp(total[0] - acs_col[0]), jnp.exp(total[1] - acs_col[1])), bmb)
    return y, jnp.where(first_row, jnp.exp(total[0]), jnp.exp(total[1])) * prev + state


def _group_tiles(ref):
    return jnp.stack([ref[:, PAIR_W * t:PAIR_W * (t + 1)] for t in range(PAIRS_PER_GROUP)])


def _store_group_tiles(ref, val):
    for t in range(PAIRS_PER_GROUP):
        ref[:, PAIR_W * t:PAIR_W * (t + 1)] = val[t]


def _by_pair(a):
    return jnp.transpose(a.reshape(SSD_PAIRS, 2, 1, -1), (1, 0, 2, 3))


def _by_head(a):
    return jnp.transpose(a, (1, 0, 2, 3)).reshape(SSM_HEADS, -1)


def _ssd2_specs(chunk_of):
    tp = PAIRS_PER_GROUP
    xspec = pl.BlockSpec((CHUNK, GROUP_W), lambda g, c: (chunk_of(c), g))
    tspec = pl.BlockSpec((2, tp, 1, CHUNK), lambda g, c: (0, g, 0, chunk_of(c)))
    hp = pl.BlockSpec((2, tp, 1, 1), lambda g, c: (0, g, 0, 0))
    gspec = pl.BlockSpec((CHUNK, SSM_STATE), lambda g, c: (chunk_of(c), g))
    sspec = pl.BlockSpec((1, tp, PAIR_W, SSM_STATE), lambda g, c: (chunk_of(c), g, 0, 0))
    return xspec, tspec, hp, gspec, sspec


def ssd2_fwd(xs, dtraw_t, dt_bias, a_log, dskip, bm, cm, side=None):
    def body(x_ref, dt_ref, dtb_ref, al_ref, dk_ref, bm_ref, cm_ref, y_ref, prev_ref, state_ref):
        @pl.when(pl.program_id(1) == 0)
        def _():
            state_ref[...] = jnp.zeros_like(state_ref)

        prev = state_ref[...]
        prev_ref[0] = prev
        y, nxt = ssd_pairs(_group_tiles(x_ref), dt_ref[...], dtb_ref[...], al_ref[...], dk_ref[...], bm_ref[...],
                           cm_ref[...], prev)
        _store_group_tiles(y_ref, y)
        state_ref[...] = nxt

    xspec, tspec, hp, gspec, sspec = _ssd2_specs(lambda c: c)
    return grid_call(
        body, (xs, _by_pair(dtraw_t), _by_pair(dt_bias), _by_pair(a_log), _by_pair(dskip), bm, cm), name="ssd_fwd",
        out_shape=[jax.ShapeDtypeStruct((S, SSM_INNER), F32),
                   jax.ShapeDtypeStruct((N_CHUNKS, SSD_PAIRS, PAIR_W, SSM_STATE), F32)],
        grid=(SSM_GROUPS, N_CHUNKS), in_specs=[xspec, tspec, hp, hp, hp, gspec, gspec], out_specs=[xspec, sspec],
        scratch_shapes=[pltpu.VMEM((PAIRS_PER_GROUP, PAIR_W, SSM_STATE), F32)],
        semantics=("parallel", "arbitrary"), side=side)


def ssd2_bwd(xs, dtraw_t, dt_bias, a_log, dskip, bm, cm, prev_all, dy, side=None):
    def body(x_ref, dt_ref, dtb_ref, al_ref, dk_ref, bm_ref, cm_ref, prev_ref, dy_ref,
             dx_ref, ddt_ref, ddtb_ref, dal_ref, ddk_ref, dbm_ref, dcm_ref, dstate_ref):
        @pl.when(pl.program_id(1) == 0)
        def _():
            for r in (dstate_ref, ddtb_ref, dal_ref, ddk_ref):
                r[...] = jnp.zeros_like(r)

        _, vjp = jax.vjp(ssd_pairs, _group_tiles(x_ref), dt_ref[...], dtb_ref[...], al_ref[...], dk_ref[...], bm_ref[...],
                         cm_ref[...], prev_ref[0])
        dx, ddt, ddtb, dal, ddk, dbm, dcm, dprev = vjp((_group_tiles(dy_ref), dstate_ref[...]))
        _store_group_tiles(dx_ref, dx)
        ddt_ref[...] = ddt
        ddtb_ref[...] += ddtb
        dal_ref[...] += dal
        ddk_ref[...] += ddk
        dbm_ref[...] = dbm
        dcm_ref[...] = dcm
        dstate_ref[...] = dprev

    xspec, tspec, hp, gspec, sspec = _ssd2_specs(lambda c: N_CHUNKS - 1 - c)
    par = jax.ShapeDtypeStruct((2, SSD_PAIRS, 1, 1), F32)
    res, side_dst = grid_call(
        body, (xs, _by_pair(dtraw_t), _by_pair(dt_bias), _by_pair(a_log), _by_pair(dskip), bm, cm, prev_all, dy),
        name="ssd_bwd",
        out_shape=[jax.ShapeDtypeStruct((S, SSM_INNER), F32), jax.ShapeDtypeStruct((2, SSD_PAIRS, 1, S), F32), par, par, par,
                   jax.ShapeDtypeStruct((S, SSM_GROUPS * SSM_STATE), F32),
                   jax.ShapeDtypeStruct((S, SSM_GROUPS * SSM_STATE), F32)],
        grid=(SSM_GROUPS, N_CHUNKS), in_specs=[xspec, tspec, hp, hp, hp, gspec, gspec, sspec, xspec],
        out_specs=[xspec, tspec, hp, hp, hp, gspec, gspec],
        scratch_shapes=[pltpu.VMEM((PAIRS_PER_GROUP, PAIR_W, SSM_STATE), F32)],
        semantics=("parallel", "arbitrary"), side=side)
    return [res[0]] + [_by_head(r) for r in res[1:5]] + list(res[5:]), side_dst


def _silu(x):
    return x * jax.nn.sigmoid(x)


def _rms(x):
    return x * lax.rsqrt(jnp.mean(x * x, -1, keepdims=True) + EPS)


def f_normmod(x, g, sc, sh):
    return (_rms(x) * g * (1.0 + sc) + sh,)


def f_resid(x, mix, gate):
    return (x + gate * mix,)


def f_resid_bias(x, mix, gate, b):
    return (x + gate * (mix + b),)


def f_swiglu(hgu):
    return (_silu(hgu[:, :FFN_HIDDEN]) * hgu[:, FFN_HIDDEN:],)


def f_silu(x):
    return (_silu(x),)


def f_silu_xbc(x):
    y = _silu(x)
    n_b = SSM_GROUPS * SSM_STATE
    return y[:, :SSM_INNER], y[:, SSM_INNER:SSM_INNER + n_b], y[:, SSM_INNER + n_b:]


def f_gated_norm(y, z, g):
    return (_rms(y * _silu(z)) * g,)


def f_glu(y, b):
    y = y + b
    return (y[:, :D] * jax.nn.sigmoid(y[:, D:]),)


def f_ln_silu(u, g, b):
    mu = jnp.mean(u, -1, keepdims=True)
    var = jnp.mean(jnp.square(u - mu), -1, keepdims=True)
    return (_silu((u - mu) * lax.rsqrt(var + EPS) * g + b),)


def f_combine(o1, o2, o3, l1, l2, l3):
    m = lax.stop_gradient(jnp.maximum(jnp.maximum(l1, l2), l3))
    e1, e2, e3 = jnp.exp(l1 - m), jnp.exp(l2 - m), jnp.exp(l3 - m)
    return ((e1 * o1 + e2 * o2 + e3 * o3) / (e1 + e2 + e3),)


def f_head(x, tgt, g):
    return (0.5 * jnp.mean(jnp.square(_rms(x) * g - tgt), -1, keepdims=True),)


def f_sum3(a, b, c):
    return (a + b + c,)


def f_sum4(a, b, c, d):
    return (a + b + c + d,)


def f_add(a, b):
    return (a + b,)


def f_adamw(w, g, m, v):
    m = ADAM_B1 * m + (1.0 - ADAM_B1) * g
    v = ADAM_B2 * v + (1.0 - ADAM_B2) * jnp.square(g)
    m_hat = m / (1.0 - ADAM_B1 ** ADAM_STEP)
    v_hat = v / (1.0 - ADAM_B2 ** ADAM_STEP)
    return -ADAM_LR * (m_hat / (jnp.sqrt(v_hat) + ADAM_EPS) + ADAM_WD * w), m, v


def _rows_tile(r, cap=256):
    return _pick(r, cap, mult=8)


def adamw(w, g, m, v, *, name):
    shape = w.shape
    c_dim = shape[-1] if len(shape) > 1 else shape[0]
    flat = [a.reshape(-1, c_dim) for a in (w, g, m, v)]
    res = rowmap(f_adamw, flat, [], [F32] * 3, name=name, tr=_rows_tile(flat[0].shape[0], cap=128))
    return [r.reshape(shape) for r in res]


def _t5_bucket(dist):
    max_exact = REL_BUCKETS // 2
    n = jnp.maximum(dist, 1).astype(F32)
    large = max_exact + jnp.log(n / max_exact) / math.log(REL_MAX_DIST / max_exact) * (REL_BUCKETS - max_exact)
    large = jnp.minimum(large.astype(jnp.int32), REL_BUCKETS - 1)
    return jnp.where(dist < max_exact, dist, large)


def _att_buckets(dil):
    i = jnp.arange(ATT_BLK)[:, None]
    j = jnp.arange(2 * ATT_BLK)[None, :]
    bkt = _t5_bucket(jnp.maximum(ATT_BLK + i - j, 0) * dil)
    return jnp.transpose(bkt.reshape(ATT_BLK, 2, ATT_BLK), (1, 0, 2))


def att_bias(rel_table, p, dil):
    tab = rel_table[:, p * ATT_HEADS:(p + 1) * ATT_HEADS]
    onehot = (jnp.arange(REL_BUCKETS)[:, None] == _att_buckets(dil).reshape(1, -1)).astype(F32)
    bias = lax.dot_general(tab, onehot, (((0,), (0,)), ((), ())), precision=lax.Precision.HIGHEST)
    return bias.reshape(ATT_HEADS, 2, ATT_BLK, ATT_BLK)


def att_bias_grad(dbias, dil, *, name):
    onehot = (_att_buckets(dil).reshape(-1, 1) == jnp.arange(LANES)[None, :]).astype(BF16)
    dtab = matmul(dbias.reshape(ATT_HEADS, -1), onehot, mode="nn", out_dtype=F32, name=name, tk_cap=2048)
    return dtab[:, :REL_BUCKETS].T


def to_heads(a, n_heads, dil=1):
    hd = a.shape[1] // n_heads
    return jnp.transpose(a.reshape(S // dil, dil, n_heads, hd), (2, 1, 0, 3)).reshape(n_heads, S, hd)


def from_heads(a, dil=1):
    n_heads, _, hd = a.shape
    return jnp.transpose(a.reshape(n_heads, dil, S // dil, hd), (2, 1, 0, 3)).reshape(S, n_heads * hd)


def regroup_heads(a, dil, inverse=False):
    n_heads, _, hd = a.shape
    if dil == 1:
        return a
    if inverse:
        return jnp.transpose(a.reshape(n_heads, dil, S // dil, hd), (0, 2, 1, 3)).reshape(n_heads, S, hd)
    return jnp.transpose(a.reshape(n_heads, S // dil, dil, hd), (0, 2, 1, 3)).reshape(n_heads, S, hd)


HY_Z, HY_XBC, HY_DT, HY_Q, HY_K, HY_V = 2048, 3072, 32, 3072, 1024, 1024
HY_IN = HY_Z + HY_XBC + HY_DT + HY_Q + HY_K + HY_V
OFF_Z, OFF_XBC, OFF_Q, OFF_KV, OFF_DT = 0, 2048, 5120, 8192, 10240
HY_CAT = OFF_DT + LANES
DT_PAD = LANES


def hy_to_cat(w):
    z, xbc, dt, qkv = w[:2048], w[2048:5120], w[5120:5152], w[5152:]
    return jnp.concatenate([z, xbc, qkv, dt, jnp.zeros((DT_PAD - HY_DT,) + w.shape[1:], w.dtype)], axis=0)


def hy_from_cat(w, axis=0):
    part = lambda a, b: lax.slice_in_dim(w, a, b, axis=axis)
    return jnp.concatenate([part(0, 5120), part(OFF_DT, OFF_DT + HY_DT), part(5120, OFF_DT)], axis=axis)


def device_step(x, tgt, mods, wts, sp, comm=None):
    g = {}
    dmods = [[None] * 6 for _ in range(2)]
    wts = dict(wts)

    def w_side(i):
        return None if comm is None else GatherRows(comm["pack"], comm["full"], *W_BATCHES[i])

    def g_side(i):
        return None if comm is None else ScatterRows(comm["ga"], comm["recv"], *G_BATCHES[i])

    def normmod(xi, gain, sc, sh, nm):
        return rowmap(f_normmod, [xi], [gain, sc, sh], [BF16], name=nm)[0]

    def ffn_fwd(xi, i, gate, nm):
        h = normmod(xi, sp["norm_ffn_g"][i], mods[i][4], mods[i][3], nm + "_norm")
        hgu = matmul(h, wts["gu_t"][i], mode="nt", out_dtype=BF16, name=nm + "_gu")
        act = rowmap(f_swiglu, [hgu], [], [BF16], name=nm + "_act", tr=128)[0]
        out = matmul(act, wts["down"][i], mode="nn", out_dtype=F32, name=nm + "_down")
        xo = rowmap(f_resid, [xi, out], [gate], [F32], name=nm + "_res")[0]
        return xo, (h, hgu, act, out)

    def ffn_bwd(dres, xi, i, saved, nm):
        h, hgu, act, out = saved
        (dout,), (dgate,), _ = rowmap_bwd(f_resid, [xi, out], [mods[i][5]], [dres], name=nm + "_res_b",
                                          row_grad=[False, True], row_dtypes=[BF16])
        dmods[i][5] = dgate
        dact = matmul(dout, wts["down"][i], mode="nt", out_dtype=BF16, name=nm + "_down_dx")
        g[f"down{i}"] = matmul(transpose(dout, name=nm + "_dout_t"), act, mode="nn", out_dtype=F32, name=nm + "_down_dw")
        (dhgu,), _, _ = rowmap_bwd(f_swiglu, [hgu], [], [dact], name=nm + "_act_b", row_grad=[True],
                                   row_dtypes=[BF16], tr=128)
        g[f"gu_t{i}"] = matmul(transpose(h, name=nm + "_h_t"), dhgu, mode="nn", out_dtype=F32, name=nm + "_gu_dw")
        dh = matmul(dhgu, wts["gu_t"][i], mode="nn", out_dtype=F32, name=nm + "_gu_dx")
        (dres,), (dg_, dsc, dsh), _ = rowmap_bwd(f_normmod, [xi], [sp["norm_ffn_g"][i], mods[i][4], mods[i][3]], [dh],
                                                 name=nm + "_norm_b", row_grad=[True], row_add=[dres])
        g[f"norm_ffn_g{i}"] = dg_
        dmods[i][4], dmods[i][3] = dsc, dsh
        return dres

    h0 = normmod(x, sp["norm_mix_g"][0], mods[0][1], mods[0][0], "l0_norm")
    w_in = wts["hy_in_t"]
    z = matmul(h0, w_in, mode="nt", out_dtype=F32, name="hy_z", n=HY_Z, b_off=OFF_Z)
    xbc_raw = matmul(h0, w_in, mode="nt", out_dtype=F32, name="hy_xbc", n=HY_XBC, b_off=OFF_XBC)
    q = matmul(h0, w_in, mode="nt", out_dtype=BF16, name="hy_q", n=HY_Q, b_off=OFF_Q)
    kv = matmul(h0, w_in, mode="nt", out_dtype=BF16, name="hy_kv", n=HY_K + HY_V, b_off=OFF_KV)
    dtr = matmul(h0, w_in, mode="nt", out_dtype=F32, name="hy_dt", n=DT_PAD, b_off=OFF_DT)
    xbc_pre = conv_fwd(xbc_raw, sp["hy_conv_w"], sp["hy_conv_b"], name="hy_conv")
    xs, bm, cm = rowmap(f_silu_xbc, [xbc_pre], [], [F32] * 3, name="hy_conv_act", tr=128)
    dtraw_t = dtr[:, :HY_DT].T
    (y, prev_all), full = ssd2_fwd(xs, dtraw_t, sp["hy_dt_bias"], sp["hy_a_log"], sp["hy_d_skip"], bm, cm, side=w_side(1))
    if comm is not None:
        comm["full"] = full
    ysn = rowmap(f_gated_norm, [y, z], [sp["hy_ssm_norm_g"]], [BF16], name="hy_gnorm", tr=128)[0]
    att_in, att_o, att_l = [], [], []
    for p, (win, dil) in enumerate(ATT_PATTERNS):
        if dil == 1:
            qa, ka, va, cols = q, kv, kv, (p, 0, 1)
        else:
            qa, ka, cols = regroup(q[:, p * D:(p + 1) * D], dil), regroup(kv, dil), (0, 0, 1)
            va = ka
        bias = pair_bias(att_bias(sp["rel_table"], p, dil))
        nb = S // dil // ATT_BLK
        (o, lse), full = att2_fwd(qa, ka, va, bias, nb, cols, name=f"att_fwd{p}", side=w_side(2 + p))
        if comm is not None:
            comm["full"] = full
        att_in.append((qa, ka, va, bias, nb, cols))
        att_o.append(regroup(o, dil, inverse=True))
        att_l.append(regroup(lse, dil, inverse=True))
    if comm is not None:
        wts.update(unpack_weights(comm["full"], skip=("hy_in_t",)))
    att = rowmap(f_combine, att_o + att_l, [], [BF16], name="att_combine", tr=128)[0]
    cat = jnp.concatenate([ysn, att], axis=-1)
    mix0 = matmul(cat, wts["hy_out"], mode="nn", out_dtype=F32, name="hy_out")
    x1 = rowmap(f_resid, [x, mix0], [mods[0][2]], [F32], name="l0_res")[0]
    x2, ffn0 = ffn_fwd(x1, 0, mods[0][5], "ffn0")

    h1 = normmod(x2, sp["norm_mix_g"][1], mods[1][1], mods[1][0], "l1_norm")
    p1 = matmul(h1, wts["pw1_t"], mode="nt", out_dtype=F32, name="cv_pw1")
    u = rowmap(f_glu, [p1], [sp["cv_b_pw1"]], [F32], name="cv_glu")[0]
    uc = conv_fwd(u, sp["cv_w_dw"], sp["cv_b_dw"], name="cv_conv")
    ul = rowmap(f_ln_silu, [uc], [sp["cv_ln_g"], sp["cv_ln_b"]], [BF16], name="cv_ln")[0]
    mix1 = matmul(ul, wts["pw2"], mode="nn", out_dtype=F32, name="cv_pw2")
    x3 = rowmap(f_resid_bias, [x2, mix1], [mods[1][2], sp["cv_b_pw2"]], [F32], name="l1_res")[0]
    x4, ffn1 = ffn_fwd(x3, 1, mods[1][5], "ffn1")

    ones = jnp.ones((S, 1), F32)
    (dres,), (dfinal,), (loss_rows,) = rowmap_bwd(f_head, [x4, tgt], [sp["final_norm_g"]], [ones], name="head",
                                                  row_grad=[True, False], emit=(0,))
    g["final_norm_g"] = dfinal

    dres = ffn_bwd(dres, x3, 1, ffn1, "ffn1")
    (dmix1,), (dg1, db2), _ = rowmap_bwd(f_resid_bias, [x2, mix1], [mods[1][2], sp["cv_b_pw2"]], [dres], name="l1_res_b",
                                         row_grad=[False, True], row_dtypes=[BF16])
    dmods[1][2] = dg1
    g["cv_b_pw2"] = db2
    dul = matmul(dmix1, wts["pw2"], mode="nt", out_dtype=F32, name="cv_pw2_dx")
    g["pw2"] = matmul(transpose(dmix1, name="cv_dmix_t"), ul, mode="nn", out_dtype=F32, name="cv_pw2_dw")
    (duc,), (g["cv_ln_g"], g["cv_ln_b"]), _ = rowmap_bwd(f_ln_silu, [uc], [sp["cv_ln_g"], sp["cv_ln_b"]], [dul],
                                                         name="cv_ln_b", row_grad=[True])
    du, g["cv_w_dw"], g["cv_b_dw"] = conv_bwd(u, sp["cv_w_dw"], duc, name="cv_conv_b")
    (dp1,), (g["cv_b_pw1"],), _ = rowmap_bwd(f_glu, [p1], [sp["cv_b_pw1"]], [du], name="cv_glu_b", row_grad=[True],
                                             row_dtypes=[BF16])
    g["pw1_t"] = matmul(transpose(h1, name="cv_h_t"), dp1, mode="nn", out_dtype=F32, name="cv_pw1_dw")
    dh1 = matmul(dp1, wts["pw1_t"], mode="nn", out_dtype=F32, name="cv_pw1_dx")
    (dres,), (dg_, dsc, dsh), _ = rowmap_bwd(f_normmod, [x2], [sp["norm_mix_g"][1], mods[1][1], mods[1][0]], [dh1],
                                             name="l1_norm_b", row_grad=[True], row_add=[dres])
    g["norm_mix_g1"] = dg_
    dmods[1][1], dmods[1][0] = dsc, dsh

    dres = ffn_bwd(dres, x1, 0, ffn0, "ffn0")
    (dmix0,), (dg1,), _ = rowmap_bwd(f_resid, [x, mix0], [mods[0][2]], [dres], name="l0_res_b",
                                     row_grad=[False, True], row_dtypes=[BF16])
    dmods[0][2] = dg1
    dysn = matmul(dmix0, wts["hy_out"], mode="nt", out_dtype=F32, name="hy_out_dy", n=SSM_INNER, b_off=0)
    datt = matmul(dmix0, wts["hy_out"], mode="nt", out_dtype=F32, name="hy_out_da", n=D, b_off=SSM_INNER)
    g["hy_out"] = matmul(transpose(dmix0, name="hy_dmix_t"), cat, mode="nn", out_dtype=F32, name="hy_out_dw")
    (dy, dz), (g["hy_ssm_norm_g"],), _ = rowmap_bwd(f_gated_norm, [y, z], [sp["hy_ssm_norm_g"]], [dysn], name="hy_gnorm_b",
                                                    row_grad=[True, True], tr=128)
    if comm is not None:
        comm["ga"] = pack_grads(g, GA_LAYOUT, GA_ROWS)
        comm["recv"] = jnp.zeros((3, GA_ROWS, D), BF16)
    (dxs, ddtraw_t, g["hy_dt_bias"], g["hy_a_log"], g["hy_d_skip"], dbm, dcm), recv = ssd2_bwd(
        xs, dtraw_t, sp["hy_dt_bias"], sp["hy_a_log"], sp["hy_d_skip"], bm, cm, prev_all, dy, side=g_side(0))
    if comm is not None:
        comm["recv"] = recv
    (dxbc_pre,), _, _ = rowmap_bwd(f_silu_xbc, [xbc_pre], [], [dxs, dbm, dcm], name="hy_conv_act_b", row_grad=[True],
                                   tr=128)
    dxbc_raw, g["hy_conv_w"], g["hy_conv_b"] = conv_bwd(xbc_raw, sp["hy_conv_w"], dxbc_pre, name="hy_conv_b")
    dol, _, _ = rowmap_bwd(f_combine, att_o + att_l, [], [datt], name="att_combine_b", row_grad=[True] * 6, tr=128)
    dqs, dks, dvs, dtabs = [], [], [], []
    for p, (win, dil) in enumerate(ATT_PATTERNS):
        qa, ka, va, bias, nb, cols = att_in[p]
        (dq, dkp_, dvp_, dbias), recv = att2_bwd(qa, ka, va, bias, regroup(dol[p], dil), regroup(dol[3 + p], dil), nb,
                                                 cols, name=f"att_bwd{p}", side=g_side(1 + p))
        if comm is not None:
            comm["recv"] = recv
        dqs.append(regroup(dq, dil, inverse=True))
        dks.append(regroup(dkp_, dil, inverse=True))
        dvs.append(regroup(dvp_, dil, inverse=True))
        dtabs.append(att_bias_grad(dbias.reshape(ATT_HEADS, 2, ATT_BLK, ATT_BLK), dil, name=f"att_dtab{p}"))
    g["rel_table"] = jnp.concatenate(dtabs, axis=1)
    dk = rowmap(f_sum3, dks, [], [F32], name="att_dk_sum")[0]
    dv = rowmap(f_sum3, dvs, [], [F32], name="att_dv_sum")[0]
    ddt = jnp.pad(ddtraw_t.T, ((0, 0), (0, DT_PAD - HY_DT)))
    dproj = jnp.concatenate([dz, dxbc_raw] + dqs + [dk, dv, ddt], axis=-1).astype(BF16)
    g["hy_in_t"] = matmul(transpose(h0, name="hy_h_t"), dproj, mode="nn", out_dtype=F32, name="hy_in_dw")
    if comm is None:
        dh0 = matmul(dproj, w_in, mode="nn", out_dtype=F32, name="hy_in_dx")
    else:
        gb = pack_grads(g, GB_LAYOUT, GB_ROWS)
        half = GB_ROWS // 2
        theirs = swap_halves(gb, name="swap_in_halves")
        ours = lax.dynamic_slice_in_dim(gb, lax.axis_index("c") * half, half, axis=1)
        comm["gb"] = rowmap(f_add, [ours.reshape(N_CHIPS * half, D), theirs.reshape(N_CHIPS * half, D)], [], [BF16],
                            name="sum_in_cores")[0].reshape(N_CHIPS, half, D)
        dh0, comm["recv_b"] = matmul(dproj, w_in, mode="nn", out_dtype=F32, name="hy_in_dx",
                                     side=ScatterRows(comm["gb"], jnp.zeros((3, half, D), BF16), 0, half))
    (dres,), (dg_, dsc, dsh), _ = rowmap_bwd(f_normmod, [x], [sp["norm_mix_g"][0], mods[0][1], mods[0][0]], [dh0],
                                             name="l0_norm_b", row_grad=[True], row_add=[dres])
    g["norm_mix_g0"] = dg_
    dmods[0][1], dmods[0][0] = dsc, dsh
    return loss_rows, dres, g, dmods


ANY = pl.BlockSpec(memory_space=pl.ANY)
WHOLE_VMEM = pl.BlockSpec(memory_space=pltpu.VMEM)


def _place():
    return lax.axis_index("x"), lax.axis_index("y"), lax.axis_index("c")


def _other_chips(x, y):
    return [(1 - x, y), (x, 1 - y), (1 - x, 1 - y)]


def allgather_small(v, *, name):
    m_per = v.shape[0]

    def body(x_ref, out_ref, send_sems, recv_sems, local_sem):
        x, y, c = _place()
        me, sibling = (x, y, c), (x, y, 1 - c)
        chips = _other_chips(x, y)

        def rows(px, py, pc):
            return out_ref.at[pl.ds((4 * px + 2 * py + pc) * m_per, m_per), :]

        def copy(k, block, to, src=None):
            return pltpu.make_async_remote_copy(
                src_ref=rows(*block) if src is None else src, dst_ref=rows(*block),
                send_sem=send_sems.at[k], recv_sem=recv_sems.at[k], device_id=to, device_id_type=MESH)

        mine = pltpu.make_async_copy(x_ref, rows(*me), local_sem)
        mine.start()
        first = [copy(0, me, sibling, src=x_ref)]
        first += [copy(1 + j, me, (*chip, c), src=x_ref) for j, chip in enumerate(chips)]
        for cp in first:
            cp.start()
        passed = [copy(4 + j, (*chip, c), sibling) for j, chip in enumerate(chips)]
        for j, chip in enumerate(chips):
            copy(1 + j, (*chip, c), me).wait_recv()
            passed[j].start()
        copy(0, sibling, me).wait_recv()
        for j, chip in enumerate(chips):
            copy(4 + j, (*chip, 1 - c), me).wait_recv()
        for cp in first + passed:
            cp.wait_send()
        mine.wait()

    return pl.pallas_call(
        body, name=name,
        out_shape=jax.ShapeDtypeStruct((N_DEV * m_per, LANES), v.dtype),
        in_specs=[WHOLE_VMEM], out_specs=WHOLE_VMEM,
        scratch_shapes=[pltpu.SemaphoreType.DMA((7,)), pltpu.SemaphoreType.DMA((7,)), pltpu.SemaphoreType.DMA],
    )(v)


def allgather_chips(pack, *, name):
    half_rows = pack.shape[0] // 2

    def body(p_ref, o_ref, send_sems, recv_sems, local_sem):
        x, y, c = _place()
        chips = _other_chips(x, y)
        sibling = (x, y, 1 - c)
        my_half = pl.ds(c * half_rows, half_rows)
        its_half = pl.ds((1 - c) * half_rows, half_rows)
        mine = pltpu.make_async_copy(p_ref, o_ref.at[2 * x + y], local_sem)
        mine.start()
        sends = [pltpu.make_async_remote_copy(
            src_ref=p_ref.at[my_half], dst_ref=o_ref.at[2 * x + y, my_half],
            send_sem=send_sems.at[k], recv_sem=recv_sems.at[k],
            device_id=(cx, cy, c), device_id_type=MESH) for k, (cx, cy) in enumerate(chips)]
        for cp in sends:
            cp.start()
        passed = []
        for k, (cx, cy) in enumerate(chips):
            landed = o_ref.at[2 * cx + cy, my_half]
            pltpu.make_async_remote_copy(
                src_ref=p_ref.at[my_half], dst_ref=landed, send_sem=send_sems.at[k], recv_sem=recv_sems.at[k],
                device_id=(cx, cy, c), device_id_type=MESH).wait_recv()
            cp = pltpu.make_async_remote_copy(
                src_ref=landed, dst_ref=landed, send_sem=send_sems.at[3 + k], recv_sem=recv_sems.at[3 + k],
                device_id=sibling, device_id_type=MESH)
            cp.start()
            passed.append(cp)
        for k, (cx, cy) in enumerate(chips):
            from_sibling = o_ref.at[2 * cx + cy, its_half]
            pltpu.make_async_remote_copy(
                src_ref=from_sibling, dst_ref=from_sibling, send_sem=send_sems.at[3 + k], recv_sem=recv_sems.at[3 + k],
                device_id=sibling, device_id_type=MESH).wait_recv()
        for cp in sends + passed:
            cp.wait_send()
        mine.wait()

    return pl.pallas_call(
        body, name=name,
        out_shape=jax.ShapeDtypeStruct((N_CHIPS,) + pack.shape, pack.dtype),
        in_specs=[ANY], out_specs=ANY,
        scratch_shapes=[pltpu.SemaphoreType.DMA((6,)), pltpu.SemaphoreType.DMA((6,)), pltpu.SemaphoreType.DMA],
    )(pack)


def swap_halves(gpack, *, name):
    half_rows = gpack.shape[1] // 2

    def body(g_ref, r_ref, send_sems, recv_sems):
        x, y, c = _place()
        its_half = pl.ds((1 - c) * half_rows, half_rows)
        copies = [pltpu.make_async_remote_copy(
            src_ref=g_ref.at[s, its_half], dst_ref=r_ref.at[s], send_sem=send_sems.at[s], recv_sem=recv_sems.at[s],
            device_id=(x, y, 1 - c), device_id_type=MESH) for s in range(N_CHIPS)]
        for cp in copies:
            cp.start()
        for cp in copies:
            cp.wait()

    return pl.pallas_call(
        body, name=name,
        out_shape=jax.ShapeDtypeStruct((N_CHIPS, half_rows) + gpack.shape[2:], gpack.dtype),
        in_specs=[ANY], out_specs=ANY,
        scratch_shapes=[pltpu.SemaphoreType.DMA((N_CHIPS,)), pltpu.SemaphoreType.DMA((N_CHIPS,))],
    )(gpack)


def scatter_chips(gpack, *, name):
    def body(g_ref, own_ref, recv_ref, send_sems, recv_sems, local_sem):
        x, y, c = _place()
        chips = _other_chips(x, y)
        mine = pltpu.make_async_copy(g_ref.at[2 * x + y], own_ref, local_sem)
        mine.start()
        sends = [pltpu.make_async_remote_copy(
            src_ref=g_ref.at[2 * cx + cy], dst_ref=recv_ref.at[k], send_sem=send_sems.at[k], recv_sem=recv_sems.at[k],
            device_id=(cx, cy, c), device_id_type=MESH) for k, (cx, cy) in enumerate(chips)]
        for cp in sends:
            cp.start()
        for cp in sends:
            cp.wait_recv()
        for cp in sends:
            cp.wait_send()
        mine.wait()

    slot = jax.ShapeDtypeStruct(gpack.shape[1:], gpack.dtype)
    return pl.pallas_call(
        body, name=name,
        out_shape=[slot, jax.ShapeDtypeStruct((3,) + gpack.shape[1:], gpack.dtype)],
        in_specs=[ANY], out_specs=[ANY, ANY],
        scratch_shapes=[pltpu.SemaphoreType.DMA((3,)), pltpu.SemaphoreType.DMA((3,)), pltpu.SemaphoreType.DMA],
    )(gpack)


class GatherRows:
    def __init__(self, pack, full, lo, hi):
        assert (hi - lo) % 32 == 0 and lo % 16 == 0
        self.src, self.dst, self.lo, self.hi = pack, full, lo, hi

    def sems(self):
        return [pltpu.SemaphoreType.DMA((6,)), pltpu.SemaphoreType.DMA((6,)), pltpu.SemaphoreType.DMA]

    def _parts(self, pack_ref, full_ref, sems):
        send_sems, recv_sems, local_sem = sems
        x, y, c = _place()
        half = (self.hi - self.lo) // 2
        mine, its = pl.ds(self.lo + c * half, half), pl.ds(self.lo + (1 - c) * half, half)
        rows = pl.ds(self.lo, self.hi - self.lo)
        local = pltpu.make_async_copy(pack_ref.at[rows], full_ref.at[2 * x + y, rows], local_sem)
        chips = _other_chips(x, y)

        def remote(src, dst, k, to):
            return pltpu.make_async_remote_copy(src_ref=src, dst_ref=dst, send_sem=send_sems.at[k],
                                                recv_sem=recv_sems.at[k], device_id=to, device_id_type=MESH)

        sends = [remote(pack_ref.at[mine], full_ref.at[2 * x + y, mine], k, (cx, cy, c)) for k, (cx, cy) in enumerate(chips)]
        landed = [full_ref.at[2 * cx + cy, mine] for cx, cy in chips]
        arrive = [remote(pack_ref.at[mine], landed[k], k, (cx, cy, c)) for k, (cx, cy) in enumerate(chips)]
        passed = [remote(landed[k], landed[k], 3 + k, (x, y, 1 - c)) for k in range(3)]
        from_sibling = [remote(landed[k], full_ref.at[2 * cx + cy, its], 3 + k, (x, y, 1 - c))
                        for k, (cx, cy) in enumerate(chips)]
        return local, sends, arrive, passed, from_sibling

    def start(self, pack_ref, full_ref, sems):
        local, sends, _, _, _ = self._parts(pack_ref, full_ref, sems)
        local.start()
        for cp in sends:
            cp.start()

    def finish(self, pack_ref, full_ref, sems):
        local, sends, arrive, passed, from_sibling = self._parts(pack_ref, full_ref, sems)
        for k in range(3):
            arrive[k].wait_recv()
            passed[k].start()
        for cp in from_sibling:
            cp.wait_recv()
        for cp in sends + passed:
            cp.wait_send()
        local.wait()


class ScatterRows:
    def __init__(self, gpack, recv, lo, hi):
        assert lo % 16 == 0 and hi % 16 == 0
        self.src, self.dst, self.lo, self.hi = gpack, recv, lo, hi

    def sems(self):
        return [pltpu.SemaphoreType.DMA((3,)), pltpu.SemaphoreType.DMA((3,))]

    def _parts(self, g_ref, recv_ref, sems):
        send_sems, recv_sems = sems
        x, y, c = _place()
        rows = pl.ds(self.lo, self.hi - self.lo)
        return [pltpu.make_async_remote_copy(
            src_ref=g_ref.at[2 * cx + cy, rows], dst_ref=recv_ref.at[k, rows], send_sem=send_sems.at[k],
            recv_sem=recv_sems.at[k], device_id=(cx, cy, c), device_id_type=MESH)
            for k, (cx, cy) in enumerate(_other_chips(x, y))]

    def start(self, g_ref, recv_ref, sems):
        for cp in self._parts(g_ref, recv_ref, sems):
            cp.start()

    def finish(self, g_ref, recv_ref, sems):
        sends = self._parts(g_ref, recv_ref, sems)
        for cp in sends:
            cp.wait_recv()
        for cp in sends:
            cp.wait_send()


def side_call(side, *, name):
    def body(src_ref, dst_in_ref, dst_ref, *sems):
        side.start(src_ref, dst_ref, sems)
        side.finish(src_ref, dst_ref, sems)

    return pl.pallas_call(
        body, name=name, out_shape=jax.ShapeDtypeStruct(side.dst.shape, side.dst.dtype),
        in_specs=[ANY, ANY], out_specs=ANY, scratch_shapes=side.sems(), input_output_aliases={1: 0},
    )(side.src, side.dst)


def grid_call(body, args, *, name, out_shape, grid, in_specs, out_specs, scratch_shapes, semantics, side=None):
    if side is None:
        res = pl.pallas_call(body, name=name, out_shape=out_shape, grid=grid, in_specs=in_specs, out_specs=out_specs,
                             scratch_shapes=scratch_shapes, compiler_params=_cparams(*semantics))(*args)
        return res, None
    n_in, n_out, n_scr = len(args), len(out_shape), len(scratch_shapes)

    def wrapped(*refs):
        ins, (src_ref, _) = refs[:n_in], refs[n_in:n_in + 2]
        outs, dst_ref = refs[n_in + 2:n_in + 2 + n_out], refs[n_in + 2 + n_out]
        scr, sems = refs[n_in + 3 + n_out:n_in + 3 + n_out + n_scr], refs[n_in + 3 + n_out + n_scr:]
        first = functools.reduce(jnp.logical_and, [pl.program_id(i) == 0 for i in range(len(grid))])
        last = functools.reduce(jnp.logical_and, [pl.program_id(i) == n - 1 for i, n in enumerate(grid)])

        @pl.when(first)
        def _():
            side.start(src_ref, dst_ref, sems)

        body(*ins, *outs, *scr)

        @pl.when(last)
        def _():
            side.finish(src_ref, dst_ref, sems)

    res = pl.pallas_call(
        wrapped, name=name,
        out_shape=list(out_shape) + [jax.ShapeDtypeStruct(side.dst.shape, side.dst.dtype)],
        grid=grid, in_specs=list(in_specs) + [ANY, ANY], out_specs=list(out_specs) + [ANY],
        scratch_shapes=list(scratch_shapes) + side.sems(), input_output_aliases={n_in + 1: n_out},
        compiler_params=_cparams(*(["arbitrary"] * len(grid))),
    )(*args, side.src, side.dst)
    return res[:-1], res[-1]


def sibling_swap(p, *, name):
    def body(p_ref, r_ref, send_sem, recv_sem):
        x, y, c = _place()
        cp = pltpu.make_async_remote_copy(src_ref=p_ref, dst_ref=r_ref, send_sem=send_sem, recv_sem=recv_sem,
                                          device_id=(x, y, 1 - c), device_id_type=MESH)
        cp.start()
        cp.wait()

    return pl.pallas_call(
        body, name=name, out_shape=jax.ShapeDtypeStruct(p.shape, p.dtype),
        in_specs=[ANY], out_specs=ANY,
        scratch_shapes=[pltpu.SemaphoreType.DMA, pltpu.SemaphoreType.DMA],
    )(p)


def sum_devices(v_all, *, name):
    m_per = v_all.shape[0] // N_DEV

    def body(v_ref, o_ref):
        acc = v_ref[pl.ds(0, m_per), :]
        for d in range(1, N_DEV):
            acc = acc + v_ref[pl.ds(d * m_per, m_per), :]
        o_ref[...] = acc

    return pl.pallas_call(
        body, name=name, out_shape=jax.ShapeDtypeStruct((m_per, LANES), F32),
        in_specs=[WHOLE_VMEM], out_specs=WHOLE_VMEM,
    )(v_all)


WEIGHTS = ['ada_w', 'ada_b', 'norm_mix_g', 'norm_ffn_g', 'hy_w_in', 'hy_conv_w', 'hy_conv_b', 'hy_dt_bias', 'hy_a_log',
           'hy_d_skip', 'hy_ssm_norm_g', 'hy_w_out', 'rel_table', 'cv_w_pw1', 'cv_b_pw1', 'cv_w_dw', 'cv_b_dw', 'cv_ln_g',
           'cv_ln_b', 'cv_w_pw2', 'cv_b_pw2', 'ffn_w_gate', 'ffn_w_up', 'ffn_w_down', 'final_norm_g']
BIG = ('ada_w', 'hy_w_in', 'hy_w_out', 'cv_w_pw1', 'cv_w_pw2', 'ffn_w_gate', 'ffn_w_up', 'ffn_w_down')
SMALL_SHARDED = {'hy_conv_w': (1, 4, 3072), 'cv_b_pw1': (1, 2048), 'cv_w_dw': (1, 31, 1024), 'cv_b_dw': (1, 1024),
                 'cv_ln_g': (1, 1024), 'cv_ln_b': (1, 1024), 'cv_b_pw2': (1, 1024)}
SMALL_GRADS = {'ada_b': (2, 6144), 'norm_mix_g': (2, 1024), 'norm_ffn_g': (2, 1024), 'hy_conv_w': (1, 4, 3072),
               'hy_conv_b': (1, 3072), 'hy_dt_bias': (1, 32), 'hy_a_log': (1, 32), 'hy_d_skip': (1, 32),
               'hy_ssm_norm_g': (1, 2048), 'rel_table': (32, 48), 'cv_b_pw1': (1, 2048), 'cv_w_dw': (1, 31, 1024),
               'cv_b_dw': (1, 1024), 'cv_ln_g': (1, 1024), 'cv_ln_b': (1, 1024), 'cv_b_pw2': (1, 1024),
               'final_norm_g': (1024,), 'loss': (1,)}

PACK_LAYOUT = (('hy_in_t', 2568), ('hy_out', 768), ('pw1_t', 512), ('pw2', 256),
               ('gate_t0', 704), ('up_t0', 704), ('down0', 704), ('gate_t1', 704), ('up_t1', 704), ('down1', 704))
PACK_ROWS = 8448


def _pack_offsets(layout):
    off, out = 0, {}
    for nm, r in layout:
        out[nm] = (off, r)
        off += r
    return out


PACK_OFF = _pack_offsets(PACK_LAYOUT)
W_BATCHES = ((0, 2624), (2624, 5248), (5248, 6336), (6336, 7424), (7424, 8448))
GA_LAYOUT = PACK_LAYOUT[1:]
GA_ROWS = 5888
GA_OFF = _pack_offsets(GA_LAYOUT)
G_BATCHES = ((0, 2560), (2560, 3712), (3712, 4864), (4864, 5888))
GB_LAYOUT = PACK_LAYOUT[:1]
GB_ROWS = 2816


def pack_grads(g, layout, n_rows):
    cache = {}

    def rows_bf16(nm):
        if nm not in cache:
            cache[nm] = transpose(g[nm], name="grad_t_" + nm)
        return cache[nm]

    parts = []
    for key, r in layout:
        if key == 'hy_in_t':
            a = hy_from_cat(rows_bf16('hy_in_t'))
        elif key.startswith('gate_t'):
            a = rows_bf16('gu_t' + key[-1])[:FFN_HIDDEN]
        elif key.startswith('up_t'):
            a = rows_bf16('gu_t' + key[-1])[FFN_HIDDEN:]
        else:
            a = rows_bf16(key)
        parts.append(a.reshape(N_CHIPS, r, D))
    used = sum(r for _, r in layout)
    return jnp.concatenate(parts + [jnp.zeros((N_CHIPS, n_rows - used, D), BF16)], axis=1)


def unpack_weights(full, skip=()):
    def whole(nm):
        o, r = PACK_OFF[nm]
        return full[:, o:o + r].reshape(N_CHIPS * r, D)

    out = {"hy_out": whole('hy_out'), "pw1_t": whole('pw1_t'), "pw2": whole('pw2'),
           "gu_t": [jnp.concatenate([whole(f'gate_t{i}'), whole(f'up_t{i}')], axis=0) for i in range(2)],
           "down": [whole(f'down{i}') for i in range(2)]}
    if "hy_in_t" not in skip:
        out["hy_in_t"] = hy_to_cat(whole('hy_in_t'))
    return out


def _to_lanes(flat):
    n = flat.shape[0]
    m = -(-n // (8 * LANES)) * 8
    return jnp.pad(flat, (0, m * LANES - n)).reshape(m, LANES)


def _split(flat, shapes):
    out, off = {}, 0
    for nm, shp in shapes.items():
        n = int(np.prod(shp))
        out[nm] = flat[off:off + n].reshape(shp)
        off += n
    return out


def kernel(x, c, ada_w, ada_b, norm_mix_g, norm_ffn_g, hy_w_in, hy_conv_w, hy_conv_b, hy_dt_bias, hy_a_log, hy_d_skip, hy_ssm_norm_g, hy_w_out, rel_table, cv_w_pw1, cv_b_pw1, cv_w_dw, cv_b_dw, cv_ln_g, cv_ln_b, cv_w_pw2, cv_b_pw2, ffn_w_gate, ffn_w_up, ffn_w_down, final_norm_g, loss_target, m_ada_w, m_ada_b, m_norm_mix_g, m_norm_ffn_g, m_hy_w_in, m_hy_conv_w, m_hy_conv_b, m_hy_dt_bias, m_hy_a_log, m_hy_d_skip, m_hy_ssm_norm_g, m_hy_w_out, m_rel_table, m_cv_w_pw1, m_cv_b_pw1, m_cv_w_dw, m_cv_b_dw, m_cv_ln_g, m_cv_ln_b, m_cv_w_pw2, m_cv_b_pw2, m_ffn_w_gate, m_ffn_w_up, m_ffn_w_down, m_final_norm_g, v_ada_w, v_ada_b, v_norm_mix_g, v_norm_ffn_g, v_hy_w_in, v_hy_conv_w, v_hy_conv_b, v_hy_dt_bias, v_hy_a_log, v_hy_d_skip, v_hy_ssm_norm_g, v_hy_w_out, v_rel_table, v_cv_w_pw1, v_cv_b_pw1, v_cv_w_dw, v_cv_b_dw, v_cv_ln_g, v_cv_ln_b, v_cv_w_pw2, v_cv_b_pw2, v_ffn_w_gate, v_ffn_w_up, v_ffn_w_down, v_final_norm_g):
    args = (x, c, ada_w, ada_b, norm_mix_g, norm_ffn_g, hy_w_in, hy_conv_w, hy_conv_b, hy_dt_bias, hy_a_log, hy_d_skip, hy_ssm_norm_g, hy_w_out, rel_table, cv_w_pw1, cv_b_pw1, cv_w_dw, cv_b_dw, cv_ln_g, cv_ln_b, cv_w_pw2, cv_b_pw2, ffn_w_gate, ffn_w_up, ffn_w_down, final_norm_g, loss_target, m_ada_w, m_ada_b, m_norm_mix_g, m_norm_ffn_g, m_hy_w_in, m_hy_conv_w, m_hy_conv_b, m_hy_dt_bias, m_hy_a_log, m_hy_d_skip, m_hy_ssm_norm_g, m_hy_w_out, m_rel_table, m_cv_w_pw1, m_cv_b_pw1, m_cv_w_dw, m_cv_b_dw, m_cv_ln_g, m_cv_ln_b, m_cv_w_pw2, m_cv_b_pw2, m_ffn_w_gate, m_ffn_w_up, m_ffn_w_down, m_final_norm_g, v_ada_w, v_ada_b, v_norm_mix_g, v_norm_ffn_g, v_hy_w_in, v_hy_conv_w, v_hy_conv_b, v_hy_dt_bias, v_hy_a_log, v_hy_d_skip, v_hy_ssm_norm_g, v_hy_w_out, v_rel_table, v_cv_w_pw1, v_cv_b_pw1, v_cv_w_dw, v_cv_b_dw, v_cv_ln_g, v_cv_ln_b, v_cv_w_pw2, v_cv_b_pw2, v_ffn_w_gate, v_ffn_w_up, v_ffn_w_down, v_final_norm_g)
    x_in, c_in = args[0], args[1]
    w = dict(zip(WEIGHTS, args[2:27], strict=True))
    tgt = args[27]
    m_in = dict(zip(WEIGHTS, args[28:53], strict=True))
    v_in = dict(zip(WEIGHTS, args[53:78], strict=True))
    xi, yi, ci = _place()
    chip = 2 * xi + yi
    dev = 2 * chip + ci

    cs = rowmap(f_silu, [c_in.reshape(8, LANES)], [], [F32], name="cond_silu", tr=8)[0]
    cs_all = allgather_small(cs, name="gather_cond").reshape(N_DEV, D)
    cs16 = jnp.pad(cs_all, ((0, 8), (0, 0)))
    modpart = jnp.stack([matmul(cs16, w['ada_w'][i], mode="nn", out_dtype=F32, name=f"ada_fwd{i}")[:N_DEV]
                         for i in range(2)], axis=1)
    shard_names = list(SMALL_SHARDED)
    payload = jnp.concatenate([modpart.reshape(-1)] + [w[nm].reshape(-1) for nm in shard_names])
    got = allgather_small(_to_lanes(payload), name="gather_mod").reshape(N_DEV, -1)[0::2]
    modparts = got[:, :modpart.size].reshape(N_CHIPS, N_DEV, 2, 1536)
    mine = lax.dynamic_index_in_dim(modparts, dev, axis=1, keepdims=False)
    mod = jnp.transpose(mine, (1, 0, 2)).reshape(2, 6 * D) + w['ada_b']
    mods = [[mod[i, j * D:(j + 1) * D].reshape(1, D) for j in range(6)] for i in range(2)]
    sp = {}
    off = modpart.size
    for nm in shard_names:
        shp = w[nm].shape
        n = int(np.prod(shp))
        parts = got[:, off:off + n].reshape((N_CHIPS,) + shp)
        sp[nm] = jnp.concatenate([parts[s] for s in range(N_CHIPS)], axis=-1)
        off += n

    def rows_of(nm, i=None):
        a = w[nm][0 if i is None else i]
        return (a.T if nm in ('hy_w_in', 'cv_w_pw1', 'ffn_w_gate', 'ffn_w_up') else a).astype(BF16)

    pieces = [rows_of('hy_w_in'), rows_of('hy_w_out'), rows_of('cv_w_pw1'), rows_of('cv_w_pw2')]
    for i in range(2):
        pieces += [rows_of('ffn_w_gate', i), rows_of('ffn_w_up', i), rows_of('ffn_w_down', i)]
    n_rows = sum(p.shape[0] for p in pieces)
    pack = jnp.concatenate(pieces + [jnp.zeros((PACK_ROWS - n_rows, D), BF16)], axis=0)
    full = side_call(GatherRows(pack, lax.empty((N_CHIPS, PACK_ROWS, D), BF16), *W_BATCHES[0]), name="gather_weights")
    o_in, r_in = PACK_OFF['hy_in_t']
    wts = {"hy_in_t": hy_to_cat(full[:, o_in:o_in + r_in].reshape(N_CHIPS * r_in, D))}
    comm = {"pack": pack, "full": full}

    sp = {"norm_mix_g": [w['norm_mix_g'][i].reshape(1, D) for i in range(2)],
          "norm_ffn_g": [w['norm_ffn_g'][i].reshape(1, D) for i in range(2)],
          "hy_conv_w": sp['hy_conv_w'][0], "hy_conv_b": w['hy_conv_b'],
          "hy_dt_bias": w['hy_dt_bias'].reshape(SSM_HEADS, 1), "hy_a_log": w['hy_a_log'].reshape(SSM_HEADS, 1),
          "hy_d_skip": w['hy_d_skip'].reshape(SSM_HEADS, 1), "hy_ssm_norm_g": w['hy_ssm_norm_g'],
          "rel_table": w['rel_table'], "cv_b_pw1": sp['cv_b_pw1'], "cv_w_dw": sp['cv_w_dw'][0], "cv_b_dw": sp['cv_b_dw'],
          "cv_ln_g": sp['cv_ln_g'], "cv_ln_b": sp['cv_ln_b'], "cv_b_pw2": sp['cv_b_pw2'],
          "final_norm_g": w['final_norm_g'].reshape(1, D)}

    loss_rows, grad_x, g, dmods = device_step(x_in[0], tgt[0], mods, wts, sp, comm)

    dmod = jnp.stack([jnp.concatenate([d.reshape(-1) for d in dmods[i]]) for i in range(2)])
    small = {'ada_b': dmod, 'norm_mix_g': jnp.stack([g[f'norm_mix_g{i}'].reshape(-1) for i in range(2)]),
             'norm_ffn_g': jnp.stack([g[f'norm_ffn_g{i}'].reshape(-1) for i in range(2)]),
             'loss': jnp.sum(loss_rows).reshape(1)}
    for nm in SMALL_GRADS:
        if nm not in small:
            small[nm] = g[nm]
    vec = _to_lanes(jnp.concatenate([small[nm].reshape(-1) for nm in SMALL_GRADS]))
    vec_all = allgather_small(vec, name="gather_small_grads")
    tot = _split(sum_devices(vec_all, name="sum_small_grads").reshape(-1), SMALL_GRADS)
    dmod_all = vec_all.reshape(N_DEV, -1)[:, :2 * 6 * D].reshape(N_DEV, 2, 6 * D)

    recv = comm["recv"]
    own_a = lax.dynamic_index_in_dim(comm["ga"], chip, axis=0, keepdims=False)
    part_a = rowmap(f_sum4, [own_a, recv[0], recv[1], recv[2]], [], [F32], name="sum_chip_grads")[0]
    red_a = rowmap(f_add, [part_a, sibling_swap(part_a, name="swap_grads")], [], [F32], name="sum_core_grads")[0]
    recv_b = comm["recv_b"]
    own_b = lax.dynamic_index_in_dim(comm["gb"], chip, axis=0, keepdims=False)
    mine_half = rowmap(f_sum4, [own_b, recv_b[0], recv_b[1], recv_b[2]], [], [F32], name="sum_in_chips")[0]
    its_half = sibling_swap(mine_half, name="swap_in")
    red_b = jnp.concatenate([jnp.where(ci == 0, mine_half, its_half), jnp.where(ci == 0, its_half, mine_half)], axis=0)

    def shard_grad(nm, i=None):
        key = {'hy_w_in': 'hy_in_t', 'hy_w_out': 'hy_out', 'cv_w_pw1': 'pw1_t', 'cv_w_pw2': 'pw2'}.get(nm)
        if key is None:
            key = {'ffn_w_gate': 'gate_t', 'ffn_w_up': 'up_t', 'ffn_w_down': 'down'}[nm] + str(i)
        if key == 'hy_in_t':
            a = red_b[:PACK_OFF[key][1]]
        else:
            o, r = GA_OFF[key]
            a = red_a[o:o + r]
        return a.T if key.endswith('_t') or key[:-1].endswith('_t') else a

    grads = {}
    grads['hy_w_in'] = shard_grad('hy_w_in')[None]
    grads['hy_w_out'] = shard_grad('hy_w_out')[None]
    grads['cv_w_pw1'] = shard_grad('cv_w_pw1')[None]
    grads['cv_w_pw2'] = shard_grad('cv_w_pw2')[None]
    for nm in ('ffn_w_gate', 'ffn_w_up', 'ffn_w_down'):
        grads[nm] = jnp.stack([shard_grad(nm, i) for i in range(2)])
    cs16 = jnp.pad(cs_all, ((0, 8), (0, 0)))
    dm_mine = lax.dynamic_slice_in_dim(dmod_all, chip * 1536, 1536, axis=2)
    dm16 = jnp.pad(dm_mine, ((0, 8), (0, 0), (0, 0)))
    grads['ada_w'] = jnp.stack([matmul(cs16, dm16[:, i], mode="tn", out_dtype=F32, name=f"ada_dw{i}") for i in range(2)])
    for nm, shp in SMALL_GRADS.items():
        if nm == 'loss':
            continue
        if nm in SMALL_SHARDED:
            n = w[nm].shape[-1]
            grads[nm] = lax.dynamic_slice_in_dim(tot[nm], chip * n, n, axis=len(shp) - 1)
        else:
            grads[nm] = tot[nm].reshape(w[nm].shape)

    delta, new_m, new_v = {}, {}, {}
    for nm in BIG:
        delta[nm], new_m[nm], new_v[nm] = adamw(w[nm], grads[nm], m_in[nm], v_in[nm], name="adamw_" + nm)
    smalls = [nm for nm in WEIGHTS if nm not in BIG]
    packed = [_to_lanes(jnp.concatenate([d[nm].reshape(-1) for nm in smalls])) for d in (w, grads, m_in, v_in)]
    res = rowmap(f_adamw, packed, [], [F32] * 3, name="adamw_small", tr=_rows_tile(packed[0].shape[0]))
    for d, r in zip((delta, new_m, new_v), res, strict=True):
        d.update(_split(r.reshape(-1), {nm: w[nm].shape for nm in smalls}))

    loss = tot['loss'].reshape(())
    return (loss, grad_x[None], *[grads[nm] for nm in WEIGHTS], *[delta[nm] for nm in WEIGHTS],
            *[new_m[nm] for nm in WEIGHTS], *[new_v[nm] for nm in WEIGHTS])
```

```python
import functools
import math

import jax
import jax.numpy as jnp
import numpy as np
from jax import lax
from jax.experimental import pallas as pl
from jax.experimental.pallas import tpu as pltpu

F32 = jnp.float32
BF16 = jnp.bfloat16
MESH = pl.DeviceIdType.MESH

D = 1024
S = 4096
EPS = 1e-6
SSM_INNER = 2048
SSM_HEADS = 32
SSM_HDIM = 64
SSM_GROUPS = 4
SSM_STATE = 128
SSM_CONVK = 4
SSM_CONV_DIM = 3072
CHUNK = 128
N_CHUNKS = S // CHUNK
ATT_HEADS = 16
ATT_HDIM = 64
ATT_PATTERNS = ((128, 1), (512, 4), (2048, 16))
ATT_BLK = 128
REL_BUCKETS = 32
REL_MAX_DIST = 2048
CONV_WIDTH = 31
FFN_HIDDEN = 2816
N_CHIPS = 4
N_DEV = 8
ADAM_LR, ADAM_B1, ADAM_B2, ADAM_EPS, ADAM_WD, ADAM_STEP = 0.001, 0.9, 0.999, 1e-08, 0.01, 10

VMEM_LIMIT_BYTES = 56 * 1024 * 1024
LANES = 128


def _cparams(*sem):
    return pltpu.CompilerParams(dimension_semantics=sem, vmem_limit_bytes=VMEM_LIMIT_BYTES)


def _pick(n, cap, mult=LANES):
    best = None
    for t in range(mult, min(n, cap) + 1, mult):
        if n % t == 0:
            best = t
    return best or n


def _dot(a, b, ca, cb):
    return lax.dot_general(a.astype(BF16), b.astype(BF16), (((ca,), (cb,)), ((), ())), preferred_element_type=F32)


@jax.custom_vjp
def mm(a, b):
    return _dot(a, b, 1, 0)


def _mm_fwd(a, b):
    return _dot(a, b, 1, 0), (a, b)


def _mm_bwd(res, g):
    a, b = res
    return _dot(g, b, 1, 1).astype(a.dtype), _dot(a, g, 0, 0).astype(b.dtype)


mm.defvjp(_mm_fwd, _mm_bwd)


@jax.custom_vjp
def mm_nt(a, b):
    return _dot(a, b, 1, 1)


def _mm_nt_fwd(a, b):
    return _dot(a, b, 1, 1), (a, b)


def _mm_nt_bwd(res, g):
    a, b = res
    return _dot(g, b, 1, 0).astype(a.dtype), _dot(g, a, 0, 0).astype(b.dtype)


mm_nt.defvjp(_mm_nt_fwd, _mm_nt_bwd)


@jax.custom_vjp
def mm_tn(a, b):
    return _dot(a, b, 0, 0)


def _mm_tn_fwd(a, b):
    return _dot(a, b, 0, 0), (a, b)


def _mm_tn_bwd(res, g):
    a, b = res
    return _dot(b, g, 1, 1).astype(a.dtype), _dot(a, g, 1, 0).astype(b.dtype)


mm_tn.defvjp(_mm_tn_fwd, _mm_tn_bwd)


def matmul(a, b, *, mode, out_dtype, name, n=None, b_off=0, tm_cap=1024, tn_cap=512, tk_cap=1536, side=None,
           out_t=False):
    if mode == "tn":
        k_dim, m_dim = a.shape
    else:
        m_dim, k_dim = a.shape
    n_dim = n if n is not None else (b.shape[0] if mode == "nt" else b.shape[1])
    tm = m_dim if m_dim < LANES else _pick(m_dim, tm_cap)
    tn = _pick(n_dim, tn_cap)
    tk = k_dim if k_dim < LANES else _pick(k_dim, tk_cap)
    assert m_dim % tm == 0 and n_dim % tn == 0 and k_dim % tk == 0 and b_off % tn == 0
    nk = k_dim // tk
    off = b_off // tn
    if mode == "nn":
        a_spec = pl.BlockSpec((tm, tk), lambda i, j, k: (i, k))
        b_spec = pl.BlockSpec((tk, tn), lambda i, j, k: (k, j))
        ca, cb = 1, 0
    elif mode == "nt":
        a_spec = pl.BlockSpec((tm, tk), lambda i, j, k: (i, k))
        b_spec = pl.BlockSpec((tn, tk), lambda i, j, k: (j + off, k))
        ca, cb = 1, 1
    else:
        a_spec = pl.BlockSpec((tk, tm), lambda i, j, k: (k, i))
        b_spec = pl.BlockSpec((tk, tn), lambda i, j, k: (k, j))
        ca, cb = 0, 0

    def emit(o_ref, val):
        o_ref[...] = (val.T if out_t else val).astype(o_ref.dtype)

    def body(a_ref, b_ref, o_ref, acc_ref):
        part = _dot(a_ref[...], b_ref[...], ca, cb)
        if nk == 1:
            emit(o_ref, part)
        else:
            k = pl.program_id(2)

            @pl.when(k == 0)
            def _():
                acc_ref[...] = part

            @pl.when(k > 0)
            def _():
                acc_ref[...] += part

            @pl.when(k == nk - 1)
            def _():
                emit(o_ref, acc_ref[...])

    if out_t:
        out_shape, out_spec = (n_dim, m_dim), pl.BlockSpec((tn, tm), lambda i, j, k: (j, i))
    else:
        out_shape, out_spec = (m_dim, n_dim), pl.BlockSpec((tm, tn), lambda i, j, k: (i, j))
    (out,), side_dst = grid_call(
        body, (a, b), name=name,
        out_shape=[jax.ShapeDtypeStruct(out_shape, out_dtype)],
        grid=(m_dim // tm, n_dim // tn, nk),
        in_specs=[a_spec, b_spec],
        out_specs=[out_spec],
        scratch_shapes=[pltpu.VMEM((tm, tn), F32)],
        semantics=("parallel", "parallel", "arbitrary"), side=side)
    return out if side is None else (out, side_dst)


def _f32(xs):
    return [x.astype(F32) for x in xs]


def rowmap(f, rows, consts, out_dtypes, *, name, tr=256):
    r_dim = rows[0].shape[0]
    tr = _pick(r_dim, tr, mult=8)
    assert r_dim % tr == 0
    nr, nc = len(rows), len(consts)
    outs = jax.eval_shape(lambda *xs: f(*xs), *[jax.ShapeDtypeStruct((tr, x.shape[1]), F32) for x in rows],
                          *[jax.ShapeDtypeStruct(x.shape, F32) for x in consts])

    def body(*refs):
        res = f(*_f32([r[...] for r in refs[:nr + nc]]))
        for o_ref, o in zip(refs[nr + nc:], res, strict=True):
            o_ref[...] = o.astype(o_ref.dtype)

    return pl.pallas_call(
        body, name=name,
        out_shape=[jax.ShapeDtypeStruct((r_dim, o.shape[1]), dt) for o, dt in zip(outs, out_dtypes, strict=True)],
        grid=(r_dim // tr,),
        in_specs=[pl.BlockSpec((tr, x.shape[1]), lambda i: (i, 0)) for x in rows]
        + [pl.BlockSpec(x.shape, lambda i: (0, 0)) for x in consts],
        out_specs=[pl.BlockSpec((tr, o.shape[1]), lambda i: (i, 0)) for o in outs],
        compiler_params=_cparams("parallel"),
    )(*rows, *consts)


def rowmap_bwd(f, rows, consts, cts, *, name, row_grad, row_dtypes=None, tr=256, emit=(), row_add=None):
    r_dim = rows[0].shape[0]
    tr = _pick(r_dim, tr, mult=8)
    assert r_dim % tr == 0
    nr, nc, nct = len(rows), len(consts), len(cts)
    gi = [i for i, flag in enumerate(row_grad) if flag]
    row_dtypes = row_dtypes or [F32] * len(gi)
    row_add = row_add or [None] * len(gi)
    adds = [a for a in row_add if a is not None]
    outs = jax.eval_shape(lambda *xs: f(*xs), *[jax.ShapeDtypeStruct((tr, x.shape[1]), F32) for x in rows],
                          *[jax.ShapeDtypeStruct(x.shape, F32) for x in consts])

    def body(*refs):
        ins = _f32([r[...] for r in refs[:nr + nc]])
        ct = _f32([r[...] for r in refs[nr + nc:nr + nc + nct]])
        add_refs = list(refs[nr + nc + nct:nr + nc + nct + len(adds)])
        o_refs = refs[nr + nc + nct + len(adds):]
        res, vjp = jax.vjp(f, *ins)
        grads = vjp(tuple(ct))
        for o_ref, i, a in zip(o_refs[:len(gi)], gi, row_add):
            g = grads[i] if a is None else grads[i] + add_refs.pop(0)[...].astype(F32)
            o_ref[...] = g.astype(o_ref.dtype)
        first = pl.program_id(0) == 0
        for o_ref, g in zip(o_refs[len(gi):len(gi) + nc], grads[nr:]):
            @pl.when(first)
            def _(o_ref=o_ref, g=g):
                o_ref[...] = g

            @pl.when(jnp.logical_not(first))
            def _(o_ref=o_ref, g=g):
                o_ref[...] += g
        for o_ref, i in zip(o_refs[len(gi) + nc:], emit):
            o_ref[...] = res[i].astype(o_ref.dtype)

    out_shape = ([jax.ShapeDtypeStruct(rows[i].shape, dt) for i, dt in zip(gi, row_dtypes, strict=True)]
                 + [jax.ShapeDtypeStruct(x.shape, F32) for x in consts]
                 + [jax.ShapeDtypeStruct((r_dim, outs[i].shape[1]), F32) for i in emit])
    out_specs = ([pl.BlockSpec((tr, rows[i].shape[1]), lambda i_: (i_, 0)) for i in gi]
                 + [pl.BlockSpec(x.shape, lambda i_: (0, 0)) for x in consts]
                 + [pl.BlockSpec((tr, outs[i].shape[1]), lambda i_: (i_, 0)) for i in emit])
    res = pl.pallas_call(
        body, name=name,
        out_shape=out_shape,
        grid=(r_dim // tr,),
        in_specs=[pl.BlockSpec((tr, x.shape[1]), lambda i: (i, 0)) for x in rows]
        + [pl.BlockSpec(x.shape, lambda i: (0, 0)) for x in consts]
        + [pl.BlockSpec((tr, x.shape[1]), lambda i: (i, 0)) for x in list(cts) + adds],
        out_specs=out_specs,
        compiler_params=_cparams("arbitrary"),
    )(*rows, *consts, *cts, *adds)
    return res[:len(gi)], res[len(gi):len(gi) + nc], res[len(gi) + nc:]


def transpose(a, *, name, out_dtype=BF16, tr=512, tc=512):
    r_dim, c_dim = a.shape
    tr, tc = _pick(r_dim, tr), _pick(c_dim, tc)

    def body(a_ref, o_ref):
        o_ref[...] = a_ref[...].astype(F32).T.astype(o_ref.dtype)

    return pl.pallas_call(
        body, name=name, out_shape=jax.ShapeDtypeStruct((c_dim, r_dim), out_dtype),
        grid=(r_dim // tr, c_dim // tc),
        in_specs=[pl.BlockSpec((tr, tc), lambda i, j: (i, j))],
        out_specs=pl.BlockSpec((tc, tr), lambda i, j: (j, i)),
        compiler_params=_cparams("parallel", "parallel"),
    )(a)


CONV_HALO = 32
CONV_ROWS = 256


def conv_fwd(x, w, b, *, name, cb=256, chunk_rows=CONV_ROWS):
    s_dim, c_dim = x.shape
    taps = w.shape[0]
    assert taps - 1 <= CONV_HALO and s_dim % chunk_rows == 0 and c_dim % cb == 0
    n_chunks = s_dim // chunk_rows
    ext = chunk_rows + CONV_HALO

    def body(x_ref, w_ref, b_ref, o_ref, xp_ref):
        xp_ref[pl.ds(0, CONV_HALO), :] = jnp.zeros((CONV_HALO, cb), F32)
        xp_ref[pl.ds(CONV_HALO, s_dim), :] = x_ref[...].astype(F32)
        wv = w_ref[...].astype(F32)
        bv = b_ref[...].astype(F32)

        def chunk(t, carry):
            base = pl.multiple_of(t * chunk_rows, chunk_rows)
            xe = xp_ref[pl.ds(base, ext), :]
            acc = jnp.broadcast_to(bv, (chunk_rows, cb))
            for j in range(taps):
                sh = xe if j == 0 else pltpu.roll(xe, shift=j, axis=0)
                acc = acc + wv[taps - 1 - j:taps - j, :] * sh[CONV_HALO:, :]
            o_ref[pl.ds(base, chunk_rows), :] = acc
            return carry

        lax.fori_loop(0, n_chunks, chunk, 0)

    return pl.pallas_call(
        body, name=name,
        out_shape=jax.ShapeDtypeStruct((s_dim, c_dim), F32),
        grid=(c_dim // cb,),
        in_specs=[pl.BlockSpec((s_dim, cb), lambda i: (0, i)), pl.BlockSpec((taps, cb), lambda i: (0, i)),
                  pl.BlockSpec((1, cb), lambda i: (0, i))],
        out_specs=pl.BlockSpec((s_dim, cb), lambda i: (0, i)),
        scratch_shapes=[pltpu.VMEM((s_dim + CONV_HALO, cb), F32)],
        compiler_params=_cparams("parallel"),
    )(x, w, b)


def conv_bwd(x, w, g, *, name, cb=256, chunk_rows=CONV_ROWS, dx_dtype=F32):
    s_dim, c_dim = x.shape
    taps = w.shape[0]
    n_chunks = s_dim // chunk_rows
    ext = chunk_rows + CONV_HALO

    def rows8(a):
        return jnp.sum(a.reshape(chunk_rows // 8, 8, cb), axis=0)

    def body(x_ref, w_ref, g_ref, dx_ref, dw_ref, db_ref, xp_ref, gp_ref, acc_ref):
        xp_ref[pl.ds(0, CONV_HALO), :] = jnp.zeros((CONV_HALO, cb), F32)
        xp_ref[pl.ds(CONV_HALO, s_dim), :] = x_ref[...].astype(F32)
        gp_ref[pl.ds(0, s_dim), :] = g_ref[...].astype(F32)
        gp_ref[pl.ds(s_dim, CONV_HALO), :] = jnp.zeros((CONV_HALO, cb), F32)
        acc_ref[...] = jnp.zeros_like(acc_ref)
        wv = w_ref[...].astype(F32)

        def chunk(t, carry):
            base = pl.multiple_of(t * chunk_rows, chunk_rows)
            xe = xp_ref[pl.ds(base, ext), :]
            ge = gp_ref[pl.ds(base, ext), :]
            gc = ge[:chunk_rows, :]
            dx = jnp.zeros((chunk_rows, cb), F32)
            for j in range(taps):
                xs = xe if j == 0 else pltpu.roll(xe, shift=j, axis=0)
                gs = ge if j == 0 else pltpu.roll(ge, shift=ext - j, axis=0)
                k = taps - 1 - j
                dx = dx + wv[k:k + 1, :] * gs[:chunk_rows, :]
                acc_ref[8 * k:8 * k + 8, :] += rows8(gc * xs[CONV_HALO:, :])
            acc_ref[8 * taps:8 * taps + 8, :] += rows8(gc)
            dx_ref[pl.ds(base, chunk_rows), :] = dx.astype(dx_ref.dtype)
            return carry

        lax.fori_loop(0, n_chunks, chunk, 0)
        sums = jnp.sum(acc_ref[...].reshape(taps + 1, 8, cb), axis=1)
        dw_ref[...] = sums[0:taps, :]
        db_ref[...] = sums[taps:taps + 1, :]

    return pl.pallas_call(
        body, name=name,
        out_shape=[jax.ShapeDtypeStruct((s_dim, c_dim), dx_dtype), jax.ShapeDtypeStruct((taps, c_dim), F32),
                   jax.ShapeDtypeStruct((1, c_dim), F32)],
        grid=(c_dim // cb,),
        in_specs=[pl.BlockSpec((s_dim, cb), lambda i: (0, i)), pl.BlockSpec((taps, cb), lambda i: (0, i)),
                  pl.BlockSpec((s_dim, cb), lambda i: (0, i))],
        out_specs=[pl.BlockSpec((s_dim, cb), lambda i: (0, i)), pl.BlockSpec((taps, cb), lambda i: (0, i)),
                   pl.BlockSpec((1, cb), lambda i: (0, i))],
        scratch_shapes=[pltpu.VMEM((s_dim + CONV_HALO, cb), F32), pltpu.VMEM((s_dim + CONV_HALO, cb), F32),
                        pltpu.VMEM((8 * (taps + 1), cb), F32)],
        compiler_params=_cparams("parallel"),
    )(x, w, g)


def _iota2(n, axis):
    return lax.broadcasted_iota(jnp.int32, (n, n), axis)


def _to_col(row):
    n = row.shape[1]
    return jnp.sum(jnp.where(_iota2(n, 0) == _iota2(n, 1), jnp.broadcast_to(row, (n, n)), 0.0), axis=1, keepdims=True)


def _softplus(x):
    return jnp.maximum(x, 0.0) + jnp.log(1.0 + jnp.exp(-jnp.abs(x)))


def ssd_heads(x, dtraw, dt_bias, a_log, dskip, bm, cm, prev):
    h, q, _ = x.shape
    n = bm.shape[1]
    li = lax.broadcasted_iota(jnp.int32, (1, q, q), 1)
    si = lax.broadcasted_iota(jnp.int32, (1, q, q), 2)

    def to_col(row):
        return jnp.sum(jnp.where(li == si, jnp.broadcast_to(row, (h, q, q)), 0.0), axis=2, keepdims=True)

    dt_row = _softplus(dtraw + dt_bias)
    a_row = dt_row * (-jnp.exp(a_log))
    a_col = to_col(a_row)
    acs_col = jnp.sum(jnp.where(si <= li, jnp.broadcast_to(a_row, (h, q, q)), 0.0), axis=2, keepdims=True)
    acs_row = jnp.sum(jnp.where(li <= si, jnp.broadcast_to(a_col, (h, q, q)), 0.0), axis=1, keepdims=True)
    total = jnp.sum(a_row, axis=2, keepdims=True)
    xdt = x * to_col(dt_row)
    lmat = jnp.exp(jnp.where(li >= si, acs_col - acs_row, -1e30))
    bmb = jnp.broadcast_to(bm[None], (h, q, n))
    cmb = jnp.broadcast_to(cm[None], (h, q, n))
    y = bmm(mm_nt(cm, bm)[None] * lmat, xdt)
    y = y + bmm_nt(cmb, prev) * jnp.exp(acs_col)
    y = y + dskip * x
    state = bmm_tn(xdt * jnp.exp(total - acs_col), bmb)
    return y, jnp.exp(total) * prev + state


HEADS_PER_GROUP = SSM_HEADS // SSM_GROUPS
BM_COL0 = SSM_INNER // SSM_STATE
CM_COL0 = BM_COL0 + SSM_GROUPS


def ssd_fwd(xs_hm, dtraw_t, dt_bias, a_log, dskip, xbc, side=None):
    hg = HEADS_PER_GROUP

    def body(x_ref, dt_ref, dtb_ref, al_ref, dk_ref, bm_ref, cm_ref, y_ref, prev_ref, state_ref):
        @pl.when(pl.program_id(1) == 0)
        def _():
            state_ref[...] = jnp.zeros_like(state_ref)

        prev = state_ref[...]
        prev_ref[0] = prev
        y, nxt = ssd_heads(x_ref[...], dt_ref[...], dtb_ref[...], al_ref[...], dk_ref[...], bm_ref[...], cm_ref[...], prev)
        y_ref[...] = y
        state_ref[...] = nxt

    hp = pl.BlockSpec((hg, 1, 1), lambda g, c: (g, 0, 0))
    dtraw_t, dt_bias, a_log, dskip = [a.reshape(SSM_HEADS, 1, -1) for a in (dtraw_t, dt_bias, a_log, dskip)]
    return grid_call(
        body, (xs_hm, dtraw_t, dt_bias, a_log, dskip, xbc, xbc), name="ssd_fwd",
        out_shape=[jax.ShapeDtypeStruct((SSM_HEADS, S, SSM_HDIM), F32),
                   jax.ShapeDtypeStruct((N_CHUNKS, SSM_HEADS, SSM_HDIM, SSM_STATE), F32)],
        grid=(SSM_GROUPS, N_CHUNKS),
        in_specs=[pl.BlockSpec((hg, CHUNK, SSM_HDIM), lambda g, c: (g, c, 0)),
                  pl.BlockSpec((hg, 1, CHUNK), lambda g, c: (g, 0, c)), hp, hp, hp,
                  pl.BlockSpec((CHUNK, SSM_STATE), lambda g, c: (c, BM_COL0 + g)),
                  pl.BlockSpec((CHUNK, SSM_STATE), lambda g, c: (c, CM_COL0 + g))],
        out_specs=[pl.BlockSpec((hg, CHUNK, SSM_HDIM), lambda g, c: (g, c, 0)),
                   pl.BlockSpec((1, hg, SSM_HDIM, SSM_STATE), lambda g, c: (c, g, 0, 0))],
        scratch_shapes=[pltpu.VMEM((hg, SSM_HDIM, SSM_STATE), F32)],
        semantics=("parallel", "arbitrary"), side=side)


def ssd_bwd(xs_hm, dtraw_t, dt_bias, a_log, dskip, xbc, prev_all, dy_hm, side=None):
    hg = HEADS_PER_GROUP
    last = N_CHUNKS - 1

    def body(x_ref, dt_ref, dtb_ref, al_ref, dk_ref, bm_ref, cm_ref, prev_ref, dy_ref,
             dx_ref, ddt_ref, ddtb_ref, dal_ref, ddk_ref, dbm_ref, dcm_ref, dstate_ref):
        @pl.when(pl.program_id(1) == 0)
        def _():
            dstate_ref[...] = jnp.zeros_like(dstate_ref)
            ddtb_ref[...] = jnp.zeros_like(ddtb_ref)
            dal_ref[...] = jnp.zeros_like(dal_ref)
            ddk_ref[...] = jnp.zeros_like(ddk_ref)

        _, vjp = jax.vjp(ssd_heads, x_ref[...], dt_ref[...], dtb_ref[...], al_ref[...], dk_ref[...], bm_ref[...],
                         cm_ref[...], prev_ref[0])
        dx, ddt, ddtb, dal, ddk, dbm, dcm, dprev = vjp((dy_ref[...], dstate_ref[...]))
        dx_ref[...] = dx
        ddt_ref[...] = ddt
        ddtb_ref[...] += ddtb
        dal_ref[...] += dal
        ddk_ref[...] += ddk
        dbm_ref[...] = dbm
        dcm_ref[...] = dcm
        dstate_ref[...] = dprev

    hp = pl.BlockSpec((hg, 1, 1), lambda g, c: (g, 0, 0))
    xspec = pl.BlockSpec((hg, CHUNK, SSM_HDIM), lambda g, c: (g, last - c, 0))
    tspec = pl.BlockSpec((hg, 1, CHUNK), lambda g, c: (g, 0, last - c))
    gspec = pl.BlockSpec((CHUNK, SSM_STATE), lambda g, c: (last - c, g))
    dtraw_t, dt_bias, a_log, dskip = [a.reshape(SSM_HEADS, 1, -1) for a in (dtraw_t, dt_bias, a_log, dskip)]
    res, side_dst = grid_call(
        body, (xs_hm, dtraw_t, dt_bias, a_log, dskip, xbc, xbc, prev_all, dy_hm), name="ssd_bwd",
        out_shape=[jax.ShapeDtypeStruct((SSM_HEADS, S, SSM_HDIM), F32), jax.ShapeDtypeStruct((SSM_HEADS, 1, S), F32),
                   jax.ShapeDtypeStruct((SSM_HEADS, 1, 1), F32), jax.ShapeDtypeStruct((SSM_HEADS, 1, 1), F32),
                   jax.ShapeDtypeStruct((SSM_HEADS, 1, 1), F32),
                   jax.ShapeDtypeStruct((S, SSM_GROUPS * SSM_STATE), F32),
                   jax.ShapeDtypeStruct((S, SSM_GROUPS * SSM_STATE), F32)],
        grid=(SSM_GROUPS, N_CHUNKS),
        in_specs=[xspec, tspec, hp, hp, hp,
                  pl.BlockSpec((CHUNK, SSM_STATE), lambda g, c: (last - c, BM_COL0 + g)),
                  pl.BlockSpec((CHUNK, SSM_STATE), lambda g, c: (last - c, CM_COL0 + g)),
                  pl.BlockSpec((1, hg, SSM_HDIM, SSM_STATE), lambda g, c: (last - c, g, 0, 0)), xspec],
        out_specs=[xspec, tspec, hp, hp, hp, gspec, gspec],
        scratch_shapes=[pltpu.VMEM((hg, SSM_HDIM, SSM_STATE), F32)],
        semantics=("parallel", "arbitrary"), side=side)
    return [res[0]] + [r.reshape(SSM_HEADS, -1) for r in res[1:5]] + list(res[5:]), side_dst


ATT_HB = 8


def _bdot(a, b, ca, cb):
    return lax.dot_general(a.astype(BF16), b.astype(BF16), (((ca,), (cb,)), ((0,), (0,))), preferred_element_type=F32)


@jax.custom_vjp
def bmm(a, b):
    return _bdot(a, b, 2, 1)


def _bmm_fwd(a, b):
    return _bdot(a, b, 2, 1), (a, b)


def _bmm_bwd(res, g):
    a, b = res
    return _bdot(g, b, 2, 2).astype(a.dtype), _bdot(a, g, 1, 1).astype(b.dtype)


bmm.defvjp(_bmm_fwd, _bmm_bwd)


@jax.custom_vjp
def bmm_nt(a, b):
    return _bdot(a, b, 2, 2)


def _bmm_nt_fwd(a, b):
    return _bdot(a, b, 2, 2), (a, b)


def _bmm_nt_bwd(res, g):
    a, b = res
    return _bdot(g, b, 2, 1).astype(a.dtype), _bdot(g, a, 1, 1).astype(b.dtype)


bmm_nt.defvjp(_bmm_nt_fwd, _bmm_nt_bwd)


@jax.custom_vjp
def bmm_tn(a, b):
    return _bdot(a, b, 1, 1)


def _bmm_tn_fwd(a, b):
    return _bdot(a, b, 1, 1), (a, b)


def _bmm_tn_bwd(res, g):
    a, b = res
    return _bdot(b, g, 2, 2).astype(a.dtype), _bdot(a, g, 2, 1).astype(b.dtype)


bmm_tn.defvjp(_bmm_tn_fwd, _bmm_tn_bwd)


def att_heads(q, kp, kc, vp, vc, bias_p, bias_c, has_prev):
    h, b, dh = q.shape
    i = lax.broadcasted_iota(jnp.int32, (1, b, b), 1)
    j = lax.broadcasted_iota(jnp.int32, (1, b, b), 2)
    scale = dh ** -0.5
    sp = jnp.where(jnp.logical_and(j >= i, has_prev), bmm_nt(q, kp) * scale + bias_p, -1e30)
    sc = jnp.where(j <= i, bmm_nt(q, kc) * scale + bias_c, -1e30)
    m = lax.stop_gradient(jnp.maximum(jnp.max(sp, axis=2, keepdims=True), jnp.max(sc, axis=2, keepdims=True)))
    pp, pc = jnp.exp(sp - m), jnp.exp(sc - m)
    l = jnp.sum(pp, axis=2, keepdims=True) + jnp.sum(pc, axis=2, keepdims=True)
    o = bmm(pp / l, vp) + bmm(pc / l, vc)
    return o, jnp.broadcast_to(m + jnp.log(l), (h, b, dh))


def _att_specs(nb):
    hb, blk = ATT_HB, ATT_BLK
    cur = pl.BlockSpec((hb, blk, ATT_HDIM), lambda h, b: (h, b, 0))
    prv = pl.BlockSpec((hb, blk, ATT_HDIM), lambda h, b: (h, jnp.maximum(b - 1, 0), 0))
    bias = pl.BlockSpec((hb, 2, blk, blk), lambda h, b: (h, 0, 0, 0))
    return cur, prv, bias


def att_fwd(q, k, v, bias, nb, *, name):
    cur, prv, bspec = _att_specs(nb)

    def body(q_ref, kp_ref, kc_ref, vp_ref, vc_ref, b_ref, o_ref, l_ref):
        has_prev = (pl.program_id(1) % nb) != 0
        o, lse = att_heads(q_ref[...], kp_ref[...], kc_ref[...], vp_ref[...], vc_ref[...], b_ref[:, 0], b_ref[:, 1],
                           has_prev)
        o_ref[...] = o
        l_ref[...] = lse

    shp = jax.ShapeDtypeStruct((ATT_HEADS, S, ATT_HDIM), F32)
    return pl.pallas_call(
        body, name=name, out_shape=[shp, shp],
        grid=(ATT_HEADS // ATT_HB, S // ATT_BLK),
        in_specs=[cur, prv, cur, prv, cur, bspec],
        out_specs=[cur, cur],
        compiler_params=_cparams("parallel", "parallel"),
    )(q, k, k, v, v, bias)


def att_bwd(q, k, v, bias, do, dlse, nb, *, name):
    cur, prv, bspec = _att_specs(nb)

    def body(q_ref, kp_ref, kc_ref, vp_ref, vc_ref, b_ref, do_ref, dl_ref,
             dq_ref, dkc_ref, dkp_ref, dvc_ref, dvp_ref, db_ref):
        has_prev = (pl.program_id(1) % nb) != 0

        @pl.when(pl.program_id(1) == 0)
        def _():
            db_ref[...] = jnp.zeros_like(db_ref)

        ins = _f32([q_ref[...], kp_ref[...], kc_ref[...], vp_ref[...], vc_ref[...]]) + [b_ref[:, 0], b_ref[:, 1]]
        _, vjp = jax.vjp(functools.partial(att_heads, has_prev=has_prev), *ins)
        dq, dkp, dkc, dvp, dvc, dbp, dbc = vjp((do_ref[...], dl_ref[...]))
        dq_ref[...] = dq
        dkc_ref[...] = dkc
        dkp_ref[...] = dkp
        dvc_ref[...] = dvc
        dvp_ref[...] = dvp
        db_ref[:, 0] += dbp
        db_ref[:, 1] += dbc

    shp = jax.ShapeDtypeStruct((ATT_HEADS, S, ATT_HDIM), F32)
    return pl.pallas_call(
        body, name=name,
        out_shape=[shp] * 5 + [jax.ShapeDtypeStruct((ATT_HEADS, 2, ATT_BLK, ATT_BLK), F32)],
        grid=(ATT_HEADS // ATT_HB, S // ATT_BLK),
        in_specs=[cur, prv, cur, prv, cur, bspec, cur, cur],
        out_specs=[cur] * 5 + [bspec],
        compiler_params=_cparams("parallel", "arbitrary"),
    )(q, k, k, v, v, bias, do, dlse)


def shift_add(cur, prev, nb, *, name):
    n_blocks = S // ATT_BLK

    def body(c_ref, p_ref, o_ref):
        nxt = pl.program_id(0) + 1
        keep = jnp.where((nxt % nb) != 0, 1.0, 0.0)
        o_ref[...] = c_ref[...] + keep * p_ref[...]

    return pl.pallas_call(
        body, name=name, out_shape=jax.ShapeDtypeStruct(cur.shape, F32),
        grid=(n_blocks,),
        in_specs=[pl.BlockSpec((ATT_HEADS, ATT_BLK, ATT_HDIM), lambda b: (0, b, 0)),
                  pl.BlockSpec((ATT_HEADS, ATT_BLK, ATT_HDIM), lambda b: (0, jnp.minimum(b + 1, n_blocks - 1), 0))],
        out_specs=pl.BlockSpec((ATT_HEADS, ATT_BLK, ATT_HDIM), lambda b: (0, b, 0)),
        compiler_params=_cparams("parallel"),
    )(cur, prev)


ATT_PAIRS = ATT_HEADS // 2
PAIR_W = 2 * ATT_HDIM


def att_pairs(q, kp, kc, vp, vc, bias, has_prev):
    t, b, w = q.shape
    i = lax.broadcasted_iota(jnp.int32, (1, b, b), 1)
    j = lax.broadcasted_iota(jnp.int32, (1, b, b), 2)
    first = lax.broadcasted_iota(jnp.int32, (1, 1, w), 2) < ATT_HDIM
    scale = ATT_HDIM ** -0.5
    outs, lses = [], []
    for ab in range(2):
        qh = jnp.where(first if ab == 0 else jnp.logical_not(first), q, 0.0)
        sp = jnp.where(jnp.logical_and(j >= i, has_prev), bmm_nt(qh, kp) * scale + bias[:, ab, 0], -1e30)
        sc = jnp.where(j <= i, bmm_nt(qh, kc) * scale + bias[:, ab, 1], -1e30)
        m = lax.stop_gradient(jnp.maximum(jnp.max(sp, axis=2, keepdims=True), jnp.max(sc, axis=2, keepdims=True)))
        pp, pc = jnp.exp(sp - m), jnp.exp(sc - m)
        l = jnp.sum(pp, axis=2, keepdims=True) + jnp.sum(pc, axis=2, keepdims=True)
        outs.append(bmm(pp / l, vp) + bmm(pc / l, vc))
        lses.append(jnp.broadcast_to(m + jnp.log(l), (t, b, w)))
    return jnp.where(first, outs[0], outs[1]), jnp.where(first, lses[0], lses[1])


def _pair_tiles(ref):
    return jnp.stack([ref[:, PAIR_W * t:PAIR_W * (t + 1)] for t in range(ATT_PAIRS)])


def _store_pair_tiles(ref, val):
    for t in range(ATT_PAIRS):
        ref[:, PAIR_W * t:PAIR_W * (t + 1)] = val[t].astype(ref.dtype)


def pair_bias(bias):
    return bias.reshape(ATT_PAIRS, 2, 2, ATT_BLK, ATT_BLK)


def att2_fwd(q, k, v, bias, nb, cols, *, name, side=None):
    n_blocks = S // ATT_BLK
    qc, kc, vc = cols

    def body(q_ref, k_ref, v_ref, b_ref, o_ref, l_ref, kprev, vprev):
        blk = pl.program_id(0)

        @pl.when(blk == 0)
        def _():
            kprev[...] = jnp.zeros_like(kprev)
            vprev[...] = jnp.zeros_like(vprev)

        k3, v3 = _pair_tiles(k_ref), _pair_tiles(v_ref)
        o, lse = att_pairs(_pair_tiles(q_ref), kprev[...], k3, vprev[...], v3, b_ref[...], (blk % nb) != 0)
        _store_pair_tiles(o_ref, o)
        _store_pair_tiles(l_ref, lse)
        kprev[...] = k3
        vprev[...] = v3

    def spec(c):
        return pl.BlockSpec((ATT_BLK, D), lambda b: (b, c))

    shp = jax.ShapeDtypeStruct((S, D), F32)
    return grid_call(
        body, (q, k, v, bias), name=name, out_shape=[shp, shp], grid=(n_blocks,),
        in_specs=[spec(qc), spec(kc), spec(vc), pl.BlockSpec(bias.shape, lambda b: (0, 0, 0, 0, 0))],
        out_specs=[spec(0), spec(0)],
        scratch_shapes=[pltpu.VMEM((ATT_PAIRS, ATT_BLK, PAIR_W), BF16), pltpu.VMEM((ATT_PAIRS, ATT_BLK, PAIR_W), BF16)],
        semantics=("arbitrary",), side=side)


def att2_bwd(q, k, v, bias, do, dlse, nb, cols, *, name, side=None):
    n_blocks = S // ATT_BLK
    qc, kc, vc = cols

    def body(q_ref, k_ref, v_ref, b_ref, do_ref, dl_ref, dq_ref, dk_ref, dv_ref, db_ref, kprev, vprev, dk_own, dv_own):
        blk = pl.program_id(0)

        @pl.when(blk == 0)
        def _():
            for r in (kprev, vprev, dk_own, dv_own, db_ref):
                r[...] = jnp.zeros_like(r)

        @pl.when(blk < n_blocks)
        def _():
            k3, v3 = _pair_tiles(k_ref), _pair_tiles(v_ref)
            ins = _f32([_pair_tiles(q_ref), kprev[...], k3, vprev[...], v3]) + [b_ref[...]]
            _, vjp = jax.vjp(functools.partial(att_pairs, has_prev=(blk % nb) != 0), *ins)
            dq, dkp, dkc, dvp, dvc, db = vjp((_pair_tiles(do_ref), _pair_tiles(dl_ref)))
            _store_pair_tiles(dq_ref, dq)
            _store_pair_tiles(dk_ref, dk_own[...] + dkp)
            _store_pair_tiles(dv_ref, dv_own[...] + dvp)
            dk_own[...] = dkc
            dv_own[...] = dvc
            db_ref[...] += db
            kprev[...] = k3
            vprev[...] = v3

        @pl.when(blk == n_blocks)
        def _():
            _store_pair_tiles(dk_ref, dk_own[...])
            _store_pair_tiles(dv_ref, dv_own[...])

    def spec(c):
        return pl.BlockSpec((ATT_BLK, D), lambda b: (jnp.minimum(b, n_blocks - 1), c))

    late = pl.BlockSpec((ATT_BLK, D), lambda b: (jnp.maximum(b - 1, 0), 0))
    bspec = pl.BlockSpec(bias.shape, lambda b: (0, 0, 0, 0, 0))
    tile_f32 = pltpu.VMEM((ATT_PAIRS, ATT_BLK, PAIR_W), F32)
    tile_bf16 = pltpu.VMEM((ATT_PAIRS, ATT_BLK, PAIR_W), BF16)
    return grid_call(
        body, (q, k, v, bias, do, dlse), name=name,
        out_shape=[jax.ShapeDtypeStruct((S, D), BF16), jax.ShapeDtypeStruct((S, D), F32),
                   jax.ShapeDtypeStruct((S, D), F32), jax.ShapeDtypeStruct(bias.shape, F32)],
        grid=(n_blocks + 1,),
        in_specs=[spec(qc), spec(kc), spec(vc), bspec, spec(0), spec(0)],
        out_specs=[spec(0), late, late, bspec],
        scratch_shapes=[tile_bf16, tile_bf16, tile_f32, tile_f32],
        semantics=("arbitrary",), side=side)


def regroup(a, dil, inverse=False):
    if dil == 1:
        return a
    c_dim = a.shape[1]
    shape = (dil, S // dil, c_dim) if inverse else (S // dil, dil, c_dim)
    return jnp.transpose(a.reshape(shape), (1, 0, 2)).reshape(S, c_dim)


SSD_PAIRS = SSM_HEADS // 2
PAIRS_PER_GROUP = SSD_PAIRS // SSM_GROUPS
GROUP_W = HEADS_PER_GROUP * SSM_HDIM


def ssd_pairs(x, dtraw, dt_bias, a_log, dskip, bm, cm, prev):
    t, q, w = x.shape
    n = bm.shape[1]
    li = lax.broadcasted_iota(jnp.int32, (1, q, q), 1)
    si = lax.broadcasted_iota(jnp.int32, (1, q, q), 2)
    first_lane = lax.broadcasted_iota(jnp.int32, (1, 1, w), 2) < SSM_HDIM
    first_row = lax.broadcasted_iota(jnp.int32, (1, w, 1), 1) < SSM_HDIM

    def to_col(row):
        return jnp.sum(jnp.where(li == si, jnp.broadcast_to(row, (t, q, q)), 0.0), axis=2, keepdims=True)

    def lanes(a0, a1):
        return jnp.where(first_lane, a0, a1)

    dt_col, acs_col, total, lmat = [], [], [], []
    for ab in range(2):
        dt_row = _softplus(dtraw[ab] + dt_bias[ab])
        a_row = dt_row * (-jnp.exp(a_log[ab]))
        a_col = to_col(a_row)
        acs_c = jnp.sum(jnp.where(si <= li, jnp.broadcast_to(a_row, (t, q, q)), 0.0), axis=2, keepdims=True)
        acs_r = jnp.sum(jnp.where(li <= si, jnp.broadcast_to(a_col, (t, q, q)), 0.0), axis=1, keepdims=True)
        dt_col.append(to_col(dt_row))
        acs_col.append(acs_c)
        total.append(jnp.sum(a_row, axis=2, keepdims=True))
        lmat.append(jnp.exp(jnp.where(li >= si, acs_c - acs_r, -1e30)))
    cb = mm_nt(cm, bm)[None]
    bmb = jnp.broadcast_to(bm[None], (t, q, n))
    cmb = jnp.broadcast_to(cm[None], (t, q, n))
    xdt = x * lanes(dt_col[0], dt_col[1])
    y = lanes(bmm(cb * lmat[0], xdt), bmm(cb * lmat[1], xdt))
    y = y + bmm_nt(cmb, prev) * lanes(jnp.exp(acs_col[0]), jnp.exp(acs_col[1]))
    y = y + lanes(dskip[0], dskip[1]) * x
    state = bmm_tn(xdt * lanes(jnp.exp(total[0] - acs_col[0]), jnp.exp(total[1] - acs_col[1])), bmb)
    return y, jnp.where(first_row, jnp.exp(total[0]), jnp.exp(total[1])) * prev + state


def _group_tiles(ref):
    return jnp.stack([ref[:, PAIR_W * t:PAIR_W * (t + 1)] for t in range(PAIRS_PER_GROUP)])


def _store_group_tiles(ref, val):
    for t in range(PAIRS_PER_GROUP):
        ref[:, PAIR_W * t:PAIR_W * (t + 1)] = val[t]


def _by_pair(a):
    return jnp.transpose(a.reshape(SSD_PAIRS, 2, 1, -1), (1, 0, 2, 3))


def _by_head(a):
    return jnp.transpose(a, (1, 0, 2, 3)).reshape(SSM_HEADS, -1)


def _ssd2_specs(chunk_of):
    tp = PAIRS_PER_GROUP
    xspec = pl.BlockSpec((CHUNK, GROUP_W), lambda g, c: (chunk_of(c), g))
    tspec = pl.BlockSpec((2, tp, 1, CHUNK), lambda g, c: (0, g, 0, chunk_of(c)))
    hp = pl.BlockSpec((2, tp, 1, 1), lambda g, c: (0, g, 0, 0))
    gspec = pl.BlockSpec((CHUNK, SSM_STATE), lambda g, c: (chunk_of(c), g))
    sspec = pl.BlockSpec((1, tp, PAIR_W, SSM_STATE), lambda g, c: (chunk_of(c), g, 0, 0))
    return xspec, tspec, hp, gspec, sspec


def ssd2_fwd(xs, dtraw_t, dt_bias, a_log, dskip, bm, cm, side=None):
    def body(x_ref, dt_ref, dtb_ref, al_ref, dk_ref, bm_ref, cm_ref, y_ref, prev_ref, state_ref):
        @pl.when(pl.program_id(1) == 0)
        def _():
            state_ref[...] = jnp.zeros_like(state_ref)

        prev = state_ref[...]
        prev_ref[0] = prev
        y, nxt = ssd_pairs(_group_tiles(x_ref), dt_ref[...], dtb_ref[...], al_ref[...], dk_ref[...], bm_ref[...],
                           cm_ref[...], prev)
        _store_group_tiles(y_ref, y)
        state_ref[...] = nxt

    xspec, tspec, hp, gspec, sspec = _ssd2_specs(lambda c: c)
    return grid_call(
        body, (xs, _by_pair(dtraw_t), _by_pair(dt_bias), _by_pair(a_log), _by_pair(dskip), bm, cm), name="ssd_fwd",
        out_shape=[jax.ShapeDtypeStruct((S, SSM_INNER), F32),
                   jax.ShapeDtypeStruct((N_CHUNKS, SSD_PAIRS, PAIR_W, SSM_STATE), F32)],
        grid=(SSM_GROUPS, N_CHUNKS), in_specs=[xspec, tspec, hp, hp, hp, gspec, gspec], out_specs=[xspec, sspec],
        scratch_shapes=[pltpu.VMEM((PAIRS_PER_GROUP, PAIR_W, SSM_STATE), F32)],
        semantics=("parallel", "arbitrary"), side=side)


def ssd2_bwd(xs, dtraw_t, dt_bias, a_log, dskip, bm, cm, prev_all, dy, side=None):
    def body(x_ref, dt_ref, dtb_ref, al_ref, dk_ref, bm_ref, cm_ref, prev_ref, dy_ref,
             dx_ref, ddt_ref, ddtb_ref, dal_ref, ddk_ref, dbm_ref, dcm_ref, dstate_ref):
        @pl.when(pl.program_id(1) == 0)
        def _():
            for r in (dstate_ref, ddtb_ref, dal_ref, ddk_ref):
                r[...] = jnp.zeros_like(r)

        _, vjp = jax.vjp(ssd_pairs, _group_tiles(x_ref), dt_ref[...], dtb_ref[...], al_ref[...], dk_ref[...], bm_ref[...],
                         cm_ref[...], prev_ref[0])
        dx, ddt, ddtb, dal, ddk, dbm, dcm, dprev = vjp((_group_tiles(dy_ref), dstate_ref[...]))
        _store_group_tiles(dx_ref, dx)
        ddt_ref[...] = ddt
        ddtb_ref[...] += ddtb
        dal_ref[...] += dal
        ddk_ref[...] += ddk
        dbm_ref[...] = dbm
        dcm_ref[...] = dcm
        dstate_ref[...] = dprev

    xspec, tspec, hp, gspec, sspec = _ssd2_specs(lambda c: N_CHUNKS - 1 - c)
    par = jax.ShapeDtypeStruct((2, SSD_PAIRS, 1, 1), F32)
    res, side_dst = grid_call(
        body, (xs, _by_pair(dtraw_t), _by_pair(dt_bias), _by_pair(a_log), _by_pair(dskip), bm, cm, prev_all, dy),
        name="ssd_bwd",
        out_shape=[jax.ShapeDtypeStruct((S, SSM_INNER), F32), jax.ShapeDtypeStruct((2, SSD_PAIRS, 1, S), F32), par, par, par,
                   jax.ShapeDtypeStruct((S, SSM_GROUPS * SSM_STATE), F32),
                   jax.ShapeDtypeStruct((S, SSM_GROUPS * SSM_STATE), F32)],
        grid=(SSM_GROUPS, N_CHUNKS), in_specs=[xspec, tspec, hp, hp, hp, gspec, gspec, sspec, xspec],
        out_specs=[xspec, tspec, hp, hp, hp, gspec, gspec],
        scratch_shapes=[pltpu.VMEM((PAIRS_PER_GROUP, PAIR_W, SSM_STATE), F32)],
        semantics=("parallel", "arbitrary"), side=side)
    return [res[0]] + [_by_head(r) for r in res[1:5]] + list(res[5:]), side_dst


def _silu(x):
    return x * jax.nn.sigmoid(x)


def _rms(x):
    return x * lax.rsqrt(jnp.mean(x * x, -1, keepdims=True) + EPS)


def f_normmod(x, g, sc, sh):
    return (_rms(x) * g * (1.0 + sc) + sh,)


def f_resid(x, mix, gate):
    return (x + gate * mix,)


def f_resid_bias(x, mix, gate, b):
    return (x + gate * (mix + b),)


def f_swiglu(hgu):
    return (_silu(hgu[:, :FFN_HIDDEN]) * hgu[:, FFN_HIDDEN:],)


def f_silu(x):
    return (_silu(x),)


def f_silu_xbc(x):
    y = _silu(x)
    n_b = SSM_GROUPS * SSM_STATE
    return y[:, :SSM_INNER], y[:, SSM_INNER:SSM_INNER + n_b], y[:, SSM_INNER + n_b:]


def f_gated_norm(y, z, g):
    return (_rms(y * _silu(z)) * g,)


def f_glu(y, b):
    y = y + b
    return (y[:, :D] * jax.nn.sigmoid(y[:, D:]),)


def f_ln_silu(u, g, b):
    mu = jnp.mean(u, -1, keepdims=True)
    var = jnp.mean(jnp.square(u - mu), -1, keepdims=True)
    return (_silu((u - mu) * lax.rsqrt(var + EPS) * g + b),)


def f_combine(o1, o2, o3, l1, l2, l3):
    m = lax.stop_gradient(jnp.maximum(jnp.maximum(l1, l2), l3))
    e1, e2, e3 = jnp.exp(l1 - m), jnp.exp(l2 - m), jnp.exp(l3 - m)
    return ((e1 * o1 + e2 * o2 + e3 * o3) / (e1 + e2 + e3),)


def f_head(x, tgt, g):
    return (0.5 * jnp.mean(jnp.square(_rms(x) * g - tgt), -1, keepdims=True),)


def f_sum3(a, b, c):
    return (a + b + c,)


def f_sum4(a, b, c, d):
    return (a + b + c + d,)


def f_add(a, b):
    return (a + b,)


def f_adamw(w, g, m, v):
    m = ADAM_B1 * m + (1.0 - ADAM_B1) * g
    v = ADAM_B2 * v + (1.0 - ADAM_B2) * jnp.square(g)
    m_hat = m / (1.0 - ADAM_B1 ** ADAM_STEP)
    v_hat = v / (1.0 - ADAM_B2 ** ADAM_STEP)
    return -ADAM_LR * (m_hat / (jnp.sqrt(v_hat) + ADAM_EPS) + ADAM_WD * w), m, v


def _rows_tile(r, cap=256):
    return _pick(r, cap, mult=8)


def adamw(w, g, m, v, *, name):
    shape = w.shape
    c_dim = shape[-1] if len(shape) > 1 else shape[0]
    flat = [a.reshape(-1, c_dim) for a in (w, g, m, v)]
    res = rowmap(f_adamw, flat, [], [F32] * 3, name=name, tr=_rows_tile(flat[0].shape[0], cap=128))
    return [r.reshape(shape) for r in res]


def _t5_bucket(dist):
    max_exact = REL_BUCKETS // 2
    n = jnp.maximum(dist, 1).astype(F32)
    large = max_exact + jnp.log(n / max_exact) / math.log(REL_MAX_DIST / max_exact) * (REL_BUCKETS - max_exact)
    large = jnp.minimum(large.astype(jnp.int32), REL_BUCKETS - 1)
    return jnp.where(dist < max_exact, dist, large)


def _att_buckets(dil):
    i = jnp.arange(ATT_BLK)[:, None]
    j = jnp.arange(2 * ATT_BLK)[None, :]
    bkt = _t5_bucket(jnp.maximum(ATT_BLK + i - j, 0) * dil)
    return jnp.transpose(bkt.reshape(ATT_BLK, 2, ATT_BLK), (1, 0, 2))


def att_bias(rel_table, p, dil):
    tab = rel_table[:, p * ATT_HEADS:(p + 1) * ATT_HEADS]
    onehot = (jnp.arange(REL_BUCKETS)[:, None] == _att_buckets(dil).reshape(1, -1)).astype(F32)
    bias = lax.dot_general(tab, onehot, (((0,), (0,)), ((), ())), precision=lax.Precision.HIGHEST)
    return bias.reshape(ATT_HEADS, 2, ATT_BLK, ATT_BLK)


def att_bias_grad(dbias, dil, *, name):
    onehot = (_att_buckets(dil).reshape(-1, 1) == jnp.arange(LANES)[None, :]).astype(BF16)
    dtab = matmul(dbias.reshape(ATT_HEADS, -1), onehot, mode="nn", out_dtype=F32, name=name, tk_cap=2048)
    return dtab[:, :REL_BUCKETS].T


def to_heads(a, n_heads, dil=1):
    hd = a.shape[1] // n_heads
    return jnp.transpose(a.reshape(S // dil, dil, n_heads, hd), (2, 1, 0, 3)).reshape(n_heads, S, hd)


def from_heads(a, dil=1):
    n_heads, _, hd = a.shape
    return jnp.transpose(a.reshape(n_heads, dil, S // dil, hd), (2, 1, 0, 3)).reshape(S, n_heads * hd)


def regroup_heads(a, dil, inverse=False):
    n_heads, _, hd = a.shape
    if dil == 1:
        return a
    if inverse:
        return jnp.transpose(a.reshape(n_heads, dil, S // dil, hd), (0, 2, 1, 3)).reshape(n_heads, S, hd)
    return jnp.transpose(a.reshape(n_heads, S // dil, dil, hd), (0, 2, 1, 3)).reshape(n_heads, S, hd)


HY_Z, HY_XBC, HY_DT, HY_Q, HY_K, HY_V = 2048, 3072, 32, 3072, 1024, 1024
HY_IN = HY_Z + HY_XBC + HY_DT + HY_Q + HY_K + HY_V
OFF_Z, OFF_XBC, OFF_Q, OFF_KV, OFF_DT = 0, 2048, 5120, 8192, 10240
HY_CAT = OFF_DT + LANES
DT_PAD = LANES


def hy_to_cat(w):
    z, xbc, dt, qkv = w[:2048], w[2048:5120], w[5120:5152], w[5152:]
    return jnp.concatenate([z, xbc, qkv, dt, jnp.zeros((DT_PAD - HY_DT,) + w.shape[1:], w.dtype)], axis=0)


def hy_from_cat(w, axis=0):
    part = lambda a, b: lax.slice_in_dim(w, a, b, axis=axis)
    return jnp.concatenate([part(0, 5120), part(OFF_DT, OFF_DT + HY_DT), part(5120, OFF_DT)], axis=axis)


def device_step(x, tgt, mods, wts, sp, comm=None):
    g = {}
    dmods = [[None] * 6 for _ in range(2)]
    wts = dict(wts)

    def wgrad(tokens_d, tokens_n, nm):
        return matmul(transpose(tokens_d, name=nm + "_t"), tokens_n, mode="nn", out_dtype=BF16, name=nm, out_t=True)

    def w_side(i):
        return None if comm is None else GatherRows(comm["pack"], comm["full"], *W_BATCHES[i])

    def g_side(i):
        return None if comm is None else ScatterRows(comm["ga"], comm["recv"], *G_BATCHES[i])

    def normmod(xi, gain, sc, sh, nm):
        return rowmap(f_normmod, [xi], [gain, sc, sh], [BF16], name=nm)[0]

    def ffn_fwd(xi, i, gate, nm):
        h = normmod(xi, sp["norm_ffn_g"][i], mods[i][4], mods[i][3], nm + "_norm")
        hgu = matmul(h, wts["gu_t"][i], mode="nt", out_dtype=BF16, name=nm + "_gu")
        act = rowmap(f_swiglu, [hgu], [], [BF16], name=nm + "_act", tr=128)[0]
        out = matmul(act, wts["down"][i], mode="nn", out_dtype=F32, name=nm + "_down")
        xo = rowmap(f_resid, [xi, out], [gate], [F32], name=nm + "_res")[0]
        return xo, (h, hgu, act, out)

    def ffn_bwd(dres, xi, i, saved, nm):
        h, hgu, act, out = saved
        (dout,), (dgate,), _ = rowmap_bwd(f_resid, [xi, out], [mods[i][5]], [dres], name=nm + "_res_b",
                                          row_grad=[False, True], row_dtypes=[BF16])
        dmods[i][5] = dgate
        dact = matmul(dout, wts["down"][i], mode="nt", out_dtype=BF16, name=nm + "_down_dx")
        g[f"down{i}"] = wgrad(dout, act, nm + "_down_dw")
        (dhgu,), _, _ = rowmap_bwd(f_swiglu, [hgu], [], [dact], name=nm + "_act_b", row_grad=[True],
                                   row_dtypes=[BF16], tr=128)
        g[f"gu_t{i}"] = wgrad(h, dhgu, nm + "_gu_dw")
        dh = matmul(dhgu, wts["gu_t"][i], mode="nn", out_dtype=F32, name=nm + "_gu_dx")
        (dres,), (dg_, dsc, dsh), _ = rowmap_bwd(f_normmod, [xi], [sp["norm_ffn_g"][i], mods[i][4], mods[i][3]], [dh],
                                                 name=nm + "_norm_b", row_grad=[True], row_add=[dres])
        g[f"norm_ffn_g{i}"] = dg_
        dmods[i][4], dmods[i][3] = dsc, dsh
        return dres

    h0 = normmod(x, sp["norm_mix_g"][0], mods[0][1], mods[0][0], "l0_norm")
    w_in = wts["hy_in_t"]
    z = matmul(h0, w_in, mode="nt", out_dtype=F32, name="hy_z", n=HY_Z, b_off=OFF_Z)
    xbc_raw = matmul(h0, w_in, mode="nt", out_dtype=F32, name="hy_xbc", n=HY_XBC, b_off=OFF_XBC)
    q = matmul(h0, w_in, mode="nt", out_dtype=BF16, name="hy_q", n=HY_Q, b_off=OFF_Q)
    kv = matmul(h0, w_in, mode="nt", out_dtype=BF16, name="hy_kv", n=HY_K + HY_V, b_off=OFF_KV)
    dtr = matmul(h0, w_in, mode="nt", out_dtype=F32, name="hy_dt", n=DT_PAD, b_off=OFF_DT)
    xbc_pre = conv_fwd(xbc_raw, sp["hy_conv_w"], sp["hy_conv_b"], name="hy_conv", cb=128, chunk_rows=128)
    xs, bm, cm = rowmap(f_silu_xbc, [xbc_pre], [], [F32] * 3, name="hy_conv_act", tr=128)
    dtraw_t = dtr[:, :HY_DT].T
    (y, prev_all), full = ssd2_fwd(xs, dtraw_t, sp["hy_dt_bias"], sp["hy_a_log"], sp["hy_d_skip"], bm, cm, side=w_side(1))
    if comm is not None:
        comm["full"] = full
    ysn = rowmap(f_gated_norm, [y, z], [sp["hy_ssm_norm_g"]], [BF16], name="hy_gnorm", tr=128)[0]
    att_in, att_o, att_l = [], [], []
    for p, (win, dil) in enumerate(ATT_PATTERNS):
        if dil == 1:
            qa, ka, va, cols = q, kv, kv, (p, 0, 1)
        else:
            qa, ka, cols = regroup(q[:, p * D:(p + 1) * D], dil), regroup(kv, dil), (0, 0, 1)
            va = ka
        bias = pair_bias(att_bias(sp["rel_table"], p, dil))
        nb = S // dil // ATT_BLK
        (o, lse), full = att2_fwd(qa, ka, va, bias, nb, cols, name=f"att_fwd{p}", side=w_side(2 + p))
        if comm is not None:
            comm["full"] = full
        att_in.append((qa, ka, va, bias, nb, cols))
        att_o.append(regroup(o, dil, inverse=True))
        att_l.append(regroup(lse, dil, inverse=True))
    if comm is not None:
        wts.update(unpack_weights(comm["full"], skip=("hy_in_t",)))
    att = rowmap(f_combine, att_o + att_l, [], [BF16], name="att_combine", tr=128)[0]
    cat = jnp.concatenate([ysn, att], axis=-1)
    mix0 = matmul(cat, wts["hy_out"], mode="nn", out_dtype=F32, name="hy_out")
    x1 = rowmap(f_resid, [x, mix0], [mods[0][2]], [F32], name="l0_res")[0]
    x2, ffn0 = ffn_fwd(x1, 0, mods[0][5], "ffn0")

    h1 = normmod(x2, sp["norm_mix_g"][1], mods[1][1], mods[1][0], "l1_norm")
    p1 = matmul(h1, wts["pw1_t"], mode="nt", out_dtype=F32, name="cv_pw1")
    u = rowmap(f_glu, [p1], [sp["cv_b_pw1"]], [F32], name="cv_glu")[0]
    uc = conv_fwd(u, sp["cv_w_dw"], sp["cv_b_dw"], name="cv_conv", cb=128, chunk_rows=128)
    ul = rowmap(f_ln_silu, [uc], [sp["cv_ln_g"], sp["cv_ln_b"]], [BF16], name="cv_ln")[0]
    mix1 = matmul(ul, wts["pw2"], mode="nn", out_dtype=F32, name="cv_pw2")
    x3 = rowmap(f_resid_bias, [x2, mix1], [mods[1][2], sp["cv_b_pw2"]], [F32], name="l1_res")[0]
    x4, ffn1 = ffn_fwd(x3, 1, mods[1][5], "ffn1")

    ones = jnp.ones((S, 1), F32)
    (dres,), (dfinal,), (loss_rows,) = rowmap_bwd(f_head, [x4, tgt], [sp["final_norm_g"]], [ones], name="head",
                                                  row_grad=[True, False], emit=(0,))
    g["final_norm_g"] = dfinal

    dres = ffn_bwd(dres, x3, 1, ffn1, "ffn1")
    (dmix1,), (dg1, db2), _ = rowmap_bwd(f_resid_bias, [x2, mix1], [mods[1][2], sp["cv_b_pw2"]], [dres], name="l1_res_b",
                                         row_grad=[False, True], row_dtypes=[BF16])
    dmods[1][2] = dg1
    g["cv_b_pw2"] = db2
    dul = matmul(dmix1, wts["pw2"], mode="nt", out_dtype=F32, name="cv_pw2_dx")
    g["pw2"] = wgrad(dmix1, ul, "cv_pw2_dw")
    (duc,), (g["cv_ln_g"], g["cv_ln_b"]), _ = rowmap_bwd(f_ln_silu, [uc], [sp["cv_ln_g"], sp["cv_ln_b"]], [dul],
                                                         name="cv_ln_b", row_grad=[True])
    du, g["cv_w_dw"], g["cv_b_dw"] = conv_bwd(u, sp["cv_w_dw"], duc, name="cv_conv_b", cb=128, chunk_rows=128)
    (dp1,), (g["cv_b_pw1"],), _ = rowmap_bwd(f_glu, [p1], [sp["cv_b_pw1"]], [du], name="cv_glu_b", row_grad=[True],
                                             row_dtypes=[BF16])
    g["pw1_t"] = wgrad(h1, dp1, "cv_pw1_dw")
    dh1 = matmul(dp1, wts["pw1_t"], mode="nn", out_dtype=F32, name="cv_pw1_dx")
    (dres,), (dg_, dsc, dsh), _ = rowmap_bwd(f_normmod, [x2], [sp["norm_mix_g"][1], mods[1][1], mods[1][0]], [dh1],
                                             name="l1_norm_b", row_grad=[True], row_add=[dres])
    g["norm_mix_g1"] = dg_
    dmods[1][1], dmods[1][0] = dsc, dsh

    dres = ffn_bwd(dres, x1, 0, ffn0, "ffn0")
    (dmix0,), (dg1,), _ = rowmap_bwd(f_resid, [x, mix0], [mods[0][2]], [dres], name="l0_res_b",
                                     row_grad=[False, True], row_dtypes=[BF16])
    dmods[0][2] = dg1
    dysn = matmul(dmix0, wts["hy_out"], mode="nt", out_dtype=F32, name="hy_out_dy", n=SSM_INNER, b_off=0)
    datt = matmul(dmix0, wts["hy_out"], mode="nt", out_dtype=F32, name="hy_out_da", n=D, b_off=SSM_INNER)
    g["hy_out"] = wgrad(dmix0, cat, "hy_out_dw")
    (dy, dz), (g["hy_ssm_norm_g"],), _ = rowmap_bwd(f_gated_norm, [y, z], [sp["hy_ssm_norm_g"]], [dysn], name="hy_gnorm_b",
                                                    row_grad=[True, True], row_dtypes=[F32, BF16], tr=128)
    if comm is not None:
        comm["ga"] = pack_grads(g, GA_LAYOUT, GA_ROWS)
        comm["recv"] = jnp.zeros((3, GA_ROWS, D), BF16)
    (dxs, ddtraw_t, g["hy_dt_bias"], g["hy_a_log"], g["hy_d_skip"], dbm, dcm), recv = ssd2_bwd(
        xs, dtraw_t, sp["hy_dt_bias"], sp["hy_a_log"], sp["hy_d_skip"], bm, cm, prev_all, dy, side=g_side(0))
    if comm is not None:
        comm["recv"] = recv
    (dxbc_pre,), _, _ = rowmap_bwd(f_silu_xbc, [xbc_pre], [], [dxs, dbm, dcm], name="hy_conv_act_b", row_grad=[True],
                                   tr=128)
    dxbc_raw, g["hy_conv_w"], g["hy_conv_b"] = conv_bwd(xbc_raw, sp["hy_conv_w"], dxbc_pre, name="hy_conv_b", cb=128, chunk_rows=128, dx_dtype=BF16)
    dol, _, _ = rowmap_bwd(f_combine, att_o + att_l, [], [datt], name="att_combine_b", row_grad=[True] * 6, tr=128)
    dqs, dks, dvs, dtabs = [], [], [], []
    for p, (win, dil) in enumerate(ATT_PATTERNS):
        qa, ka, va, bias, nb, cols = att_in[p]
        (dq, dkp_, dvp_, dbias), recv = att2_bwd(qa, ka, va, bias, regroup(dol[p], dil), regroup(dol[3 + p], dil), nb,
                                                 cols, name=f"att_bwd{p}", side=g_side(1 + p))
        if comm is not None:
            comm["recv"] = recv
        dqs.append(regroup(dq, dil, inverse=True))
        dks.append(regroup(dkp_, dil, inverse=True))
        dvs.append(regroup(dvp_, dil, inverse=True))
        dtabs.append(att_bias_grad(dbias.reshape(ATT_HEADS, 2, ATT_BLK, ATT_BLK), dil, name=f"att_dtab{p}"))
    g["rel_table"] = jnp.concatenate(dtabs, axis=1)
    dk = rowmap(f_sum3, dks, [], [BF16], name="att_dk_sum")[0]
    dv = rowmap(f_sum3, dvs, [], [BF16], name="att_dv_sum")[0]
    ddt = jnp.pad(ddtraw_t.T, ((0, 0), (0, DT_PAD - HY_DT)))
    dproj = jnp.concatenate([dz, dxbc_raw] + dqs + [dk, dv, ddt.astype(BF16)], axis=-1)
    g["hy_in_t"] = wgrad(h0, dproj, "hy_in_dw")
    if comm is None:
        dh0 = matmul(dproj, w_in, mode="nn", out_dtype=F32, name="hy_in_dx")
    else:
        gb = pack_grads(g, GB_LAYOUT, GB_ROWS)
        half = GB_ROWS // 2
        theirs = swap_halves(gb, name="swap_in_halves")
        ours = lax.dynamic_slice_in_dim(gb, lax.axis_index("c") * half, half, axis=1)
        comm["gb"] = rowmap(f_add, [ours.reshape(N_CHIPS * half, D), theirs.reshape(N_CHIPS * half, D)], [], [BF16],
                            name="sum_in_cores")[0].reshape(N_CHIPS, half, D)
        dh0, comm["recv_b"] = matmul(dproj, w_in, mode="nn", out_dtype=F32, name="hy_in_dx",
                                     side=ScatterRows(comm["gb"], jnp.zeros((3, half, D), BF16), 0, half))
    (dres,), (dg_, dsc, dsh), _ = rowmap_bwd(f_normmod, [x], [sp["norm_mix_g"][0], mods[0][1], mods[0][0]], [dh0],
                                             name="l0_norm_b", row_grad=[True], row_add=[dres])
    g["norm_mix_g0"] = dg_
    dmods[0][1], dmods[0][0] = dsc, dsh
    return loss_rows, dres, g, dmods


ANY = pl.BlockSpec(memory_space=pl.ANY)
WHOLE_VMEM = pl.BlockSpec(memory_space=pltpu.VMEM)


def _place():
    return lax.axis_index("x"), lax.axis_index("y"), lax.axis_index("c")


def _other_chips(x, y):
    return [(1 - x, y), (x, 1 - y), (1 - x, 1 - y)]


def allgather_small(v, *, name):
    m_per = v.shape[0]

    def body(x_ref, out_ref, send_sems, recv_sems, local_sem):
        x, y, c = _place()
        me, sibling = (x, y, c), (x, y, 1 - c)
        chips = _other_chips(x, y)

        def rows(px, py, pc):
            return out_ref.at[pl.ds((4 * px + 2 * py + pc) * m_per, m_per), :]

        def copy(k, block, to, src=None):
            return pltpu.make_async_remote_copy(
                src_ref=rows(*block) if src is None else src, dst_ref=rows(*block),
                send_sem=send_sems.at[k], recv_sem=recv_sems.at[k], device_id=to, device_id_type=MESH)

        mine = pltpu.make_async_copy(x_ref, rows(*me), local_sem)
        mine.start()
        first = [copy(0, me, sibling, src=x_ref)]
        first += [copy(1 + j, me, (*chip, c), src=x_ref) for j, chip in enumerate(chips)]
        for cp in first:
            cp.start()
        passed = [copy(4 + j, (*chip, c), sibling) for j, chip in enumerate(chips)]
        for j, chip in enumerate(chips):
            copy(1 + j, (*chip, c), me).wait_recv()
            passed[j].start()
        copy(0, sibling, me).wait_recv()
        for j, chip in enumerate(chips):
            copy(4 + j, (*chip, 1 - c), me).wait_recv()
        for cp in first + passed:
            cp.wait_send()
        mine.wait()

    return pl.pallas_call(
        body, name=name,
        out_shape=jax.ShapeDtypeStruct((N_DEV * m_per, LANES), v.dtype),
        in_specs=[WHOLE_VMEM], out_specs=WHOLE_VMEM,
        scratch_shapes=[pltpu.SemaphoreType.DMA((7,)), pltpu.SemaphoreType.DMA((7,)), pltpu.SemaphoreType.DMA],
    )(v)


def allgather_chips(pack, *, name):
    half_rows = pack.shape[0] // 2

    def body(p_ref, o_ref, send_sems, recv_sems, local_sem):
        x, y, c = _place()
        chips = _other_chips(x, y)
        sibling = (x, y, 1 - c)
        my_half = pl.ds(c * half_rows, half_rows)
        its_half = pl.ds((1 - c) * half_rows, half_rows)
        mine = pltpu.make_async_copy(p_ref, o_ref.at[2 * x + y], local_sem)
        mine.start()
        sends = [pltpu.make_async_remote_copy(
            src_ref=p_ref.at[my_half], dst_ref=o_ref.at[2 * x + y, my_half],
            send_sem=send_sems.at[k], recv_sem=recv_sems.at[k],
            device_id=(cx, cy, c), device_id_type=MESH) for k, (cx, cy) in enumerate(chips)]
        for cp in sends:
            cp.start()
        passed = []
        for k, (cx, cy) in enumerate(chips):
            landed = o_ref.at[2 * cx + cy, my_half]
            pltpu.make_async_remote_copy(
                src_ref=p_ref.at[my_half], dst_ref=landed, send_sem=send_sems.at[k], recv_sem=recv_sems.at[k],
                device_id=(cx, cy, c), device_id_type=MESH).wait_recv()
            cp = pltpu.make_async_remote_copy(
                src_ref=landed, dst_ref=landed, send_sem=send_sems.at[3 + k], recv_sem=recv_sems.at[3 + k],
                device_id=sibling, device_id_type=MESH)
            cp.start()
            passed.append(cp)
        for k, (cx, cy) in enumerate(chips):
            from_sibling = o_ref.at[2 * cx + cy, its_half]
            pltpu.make_async_remote_copy(
                src_ref=from_sibling, dst_ref=from_sibling, send_sem=send_sems.at[3 + k], recv_sem=recv_sems.at[3 + k],
                device_id=sibling, device_id_type=MESH).wait_recv()
        for cp in sends + passed:
            cp.wait_send()
        mine.wait()

    return pl.pallas_call(
        body, name=name,
        out_shape=jax.ShapeDtypeStruct((N_CHIPS,) + pack.shape, pack.dtype),
        in_specs=[ANY], out_specs=ANY,
        scratch_shapes=[pltpu.SemaphoreType.DMA((6,)), pltpu.SemaphoreType.DMA((6,)), pltpu.SemaphoreType.DMA],
    )(pack)


def swap_halves(gpack, *, name):
    half_rows = gpack.shape[1] // 2

    def body(g_ref, r_ref, send_sems, recv_sems):
        x, y, c = _place()
        its_half = pl.ds((1 - c) * half_rows, half_rows)
        copies = [pltpu.make_async_remote_copy(
            src_ref=g_ref.at[s, its_half], dst_ref=r_ref.at[s], send_sem=send_sems.at[s], recv_sem=recv_sems.at[s],
            device_id=(x, y, 1 - c), device_id_type=MESH) for s in range(N_CHIPS)]
        for cp in copies:
            cp.start()
        for cp in copies:
            cp.wait()

    return pl.pallas_call(
        body, name=name,
        out_shape=jax.ShapeDtypeStruct((N_CHIPS, half_rows) + gpack.shape[2:], gpack.dtype),
        in_specs=[ANY], out_specs=ANY,
        scratch_shapes=[pltpu.SemaphoreType.DMA((N_CHIPS,)), pltpu.SemaphoreType.DMA((N_CHIPS,))],
    )(gpack)


def scatter_chips(gpack, *, name):
    def body(g_ref, own_ref, recv_ref, send_sems, recv_sems, local_sem):
        x, y, c = _place()
        chips = _other_chips(x, y)
        mine = pltpu.make_async_copy(g_ref.at[2 * x + y], own_ref, local_sem)
        mine.start()
        sends = [pltpu.make_async_remote_copy(
            src_ref=g_ref.at[2 * cx + cy], dst_ref=recv_ref.at[k], send_sem=send_sems.at[k], recv_sem=recv_sems.at[k],
            device_id=(cx, cy, c), device_id_type=MESH) for k, (cx, cy) in enumerate(chips)]
        for cp in sends:
            cp.start()
        for cp in sends:
            cp.wait_recv()
        for cp in sends:
            cp.wait_send()
        mine.wait()

    slot = jax.ShapeDtypeStruct(gpack.shape[1:], gpack.dtype)
    return pl.pallas_call(
        body, name=name,
        out_shape=[slot, jax.ShapeDtypeStruct((3,) + gpack.shape[1:], gpack.dtype)],
        in_specs=[ANY], out_specs=[ANY, ANY],
        scratch_shapes=[pltpu.SemaphoreType.DMA((3,)), pltpu.SemaphoreType.DMA((3,)), pltpu.SemaphoreType.DMA],
    )(gpack)


class GatherRows:
    def __init__(self, pack, full, lo, hi):
        assert (hi - lo) % 32 == 0 and lo % 16 == 0
        self.src, self.dst, self.lo, self.hi = pack, full, lo, hi

    def sems(self):
        return [pltpu.SemaphoreType.DMA((6,)), pltpu.SemaphoreType.DMA((6,)), pltpu.SemaphoreType.DMA]

    def _parts(self, pack_ref, full_ref, sems):
        send_sems, recv_sems, local_sem = sems
        x, y, c = _place()
        half = (self.hi - self.lo) // 2
        mine, its = pl.ds(self.lo + c * half, half), pl.ds(self.lo + (1 - c) * half, half)
        rows = pl.ds(self.lo, self.hi - self.lo)
        local = pltpu.make_async_copy(pack_ref.at[rows], full_ref.at[2 * x + y, rows], local_sem)
        chips = _other_chips(x, y)

        def remote(src, dst, k, to):
            return pltpu.make_async_remote_copy(src_ref=src, dst_ref=dst, send_sem=send_sems.at[k],
                                                recv_sem=recv_sems.at[k], device_id=to, device_id_type=MESH)

        sends = [remote(pack_ref.at[mine], full_ref.at[2 * x + y, mine], k, (cx, cy, c)) for k, (cx, cy) in enumerate(chips)]
        landed = [full_ref.at[2 * cx + cy, mine] for cx, cy in chips]
        arrive = [remote(pack_ref.at[mine], landed[k], k, (cx, cy, c)) for k, (cx, cy) in enumerate(chips)]
        passed = [remote(landed[k], landed[k], 3 + k, (x, y, 1 - c)) for k in range(3)]
        from_sibling = [remote(landed[k], full_ref.at[2 * cx + cy, its], 3 + k, (x, y, 1 - c))
                        for k, (cx, cy) in enumerate(chips)]
        return local, sends, arrive, passed, from_sibling

    def start(self, pack_ref, full_ref, sems):
        local, sends, _, _, _ = self._parts(pack_ref, full_ref, sems)
        local.start()
        for cp in sends:
            cp.start()

    def finish(self, pack_ref, full_ref, sems):
        local, sends, arrive, passed, from_sibling = self._parts(pack_ref, full_ref, sems)
        for k in range(3):
            arrive[k].wait_recv()
            passed[k].start()
        for cp in from_sibling:
            cp.wait_recv()
        for cp in sends + passed:
            cp.wait_send()
        local.wait()


class ScatterRows:
    def __init__(self, gpack, recv, lo, hi):
        assert lo % 16 == 0 and hi % 16 == 0
        self.src, self.dst, self.lo, self.hi = gpack, recv, lo, hi

    def sems(self):
        return [pltpu.SemaphoreType.DMA((3,)), pltpu.SemaphoreType.DMA((3,))]

    def _parts(self, g_ref, recv_ref, sems):
        send_sems, recv_sems = sems
        x, y, c = _place()
        rows = pl.ds(self.lo, self.hi - self.lo)
        return [pltpu.make_async_remote_copy(
            src_ref=g_ref.at[2 * cx + cy, rows], dst_ref=recv_ref.at[k, rows], send_sem=send_sems.at[k],
            recv_sem=recv_sems.at[k], device_id=(cx, cy, c), device_id_type=MESH)
            for k, (cx, cy) in enumerate(_other_chips(x, y))]

    def start(self, g_ref, recv_ref, sems):
        for cp in self._parts(g_ref, recv_ref, sems):
            cp.start()

    def finish(self, g_ref, recv_ref, sems):
        sends = self._parts(g_ref, recv_ref, sems)
        for cp in sends:
            cp.wait_recv()
        for cp in sends:
            cp.wait_send()


def side_call(side, *, name):
    def body(src_ref, dst_in_ref, dst_ref, *sems):
        side.start(src_ref, dst_ref, sems)
        side.finish(src_ref, dst_ref, sems)

    return pl.pallas_call(
        body, name=name, out_shape=jax.ShapeDtypeStruct(side.dst.shape, side.dst.dtype),
        in_specs=[ANY, ANY], out_specs=ANY, scratch_shapes=side.sems(), input_output_aliases={1: 0},
    )(side.src, side.dst)


def grid_call(body, args, *, name, out_shape, grid, in_specs, out_specs, scratch_shapes, semantics, side=None):
    if side is None:
        res = pl.pallas_call(body, name=name, out_shape=out_shape, grid=grid, in_specs=in_specs, out_specs=out_specs,
                             scratch_shapes=scratch_shapes, compiler_params=_cparams(*semantics))(*args)
        return res, None
    n_in, n_out, n_scr = len(args), len(out_shape), len(scratch_shapes)

    def wrapped(*refs):
        ins, (src_ref, _) = refs[:n_in], refs[n_in:n_in + 2]
        outs, dst_ref = refs[n_in + 2:n_in + 2 + n_out], refs[n_in + 2 + n_out]
        scr, sems = refs[n_in + 3 + n_out:n_in + 3 + n_out + n_scr], refs[n_in + 3 + n_out + n_scr:]
        first = functools.reduce(jnp.logical_and, [pl.program_id(i) == 0 for i in range(len(grid))])
        last = functools.reduce(jnp.logical_and, [pl.program_id(i) == n - 1 for i, n in enumerate(grid)])

        @pl.when(first)
        def _():
            side.start(src_ref, dst_ref, sems)

        body(*ins, *outs, *scr)

        @pl.when(last)
        def _():
            side.finish(src_ref, dst_ref, sems)

    res = pl.pallas_call(
        wrapped, name=name,
        out_shape=list(out_shape) + [jax.ShapeDtypeStruct(side.dst.shape, side.dst.dtype)],
        grid=grid, in_specs=list(in_specs) + [ANY, ANY], out_specs=list(out_specs) + [ANY],
        scratch_shapes=list(scratch_shapes) + side.sems(), input_output_aliases={n_in + 1: n_out},
        compiler_params=_cparams(*(["arbitrary"] * len(grid))),
    )(*args, side.src, side.dst)
    return res[:-1], res[-1]


def sibling_swap(p, *, name):
    def body(p_ref, r_ref, send_sem, recv_sem):
        x, y, c = _place()
        cp = pltpu.make_async_remote_copy(src_ref=p_ref, dst_ref=r_ref, send_sem=send_sem, recv_sem=recv_sem,
                                          device_id=(x, y, 1 - c), device_id_type=MESH)
        cp.start()
        cp.wait()

    return pl.pallas_call(
        body, name=name, out_shape=jax.ShapeDtypeStruct(p.shape, p.dtype),
        in_specs=[ANY], out_specs=ANY,
        scratch_shapes=[pltpu.SemaphoreType.DMA, pltpu.SemaphoreType.DMA],
    )(p)


def sum_devices(v_all, *, name):
    m_per = v_all.shape[0] // N_DEV

    def body(v_ref, o_ref):
        acc = v_ref[pl.ds(0, m_per), :]
        for d in range(1, N_DEV):
            acc = acc + v_ref[pl.ds(d * m_per, m_per), :]
        o_ref[...] = acc

    return pl.pallas_call(
        body, name=name, out_shape=jax.ShapeDtypeStruct((m_per, LANES), F32),
        in_specs=[WHOLE_VMEM], out_specs=WHOLE_VMEM,
    )(v_all)


WEIGHTS = ['ada_w', 'ada_b', 'norm_mix_g', 'norm_ffn_g', 'hy_w_in', 'hy_conv_w', 'hy_conv_b', 'hy_dt_bias', 'hy_a_log',
           'hy_d_skip', 'hy_ssm_norm_g', 'hy_w_out', 'rel_table', 'cv_w_pw1', 'cv_b_pw1', 'cv_w_dw', 'cv_b_dw', 'cv_ln_g',
           'cv_ln_b', 'cv_w_pw2', 'cv_b_pw2', 'ffn_w_gate', 'ffn_w_up', 'ffn_w_down', 'final_norm_g']
BIG = ('ada_w', 'hy_w_in', 'hy_w_out', 'cv_w_pw1', 'cv_w_pw2', 'ffn_w_gate', 'ffn_w_up', 'ffn_w_down')
SMALL_SHARDED = {'hy_conv_w': (1, 4, 3072), 'cv_b_pw1': (1, 2048), 'cv_w_dw': (1, 31, 1024), 'cv_b_dw': (1, 1024),
                 'cv_ln_g': (1, 1024), 'cv_ln_b': (1, 1024), 'cv_b_pw2': (1, 1024)}
SMALL_GRADS = {'ada_b': (2, 6144), 'norm_mix_g': (2, 1024), 'norm_ffn_g': (2, 1024), 'hy_conv_w': (1, 4, 3072),
               'hy_conv_b': (1, 3072), 'hy_dt_bias': (1, 32), 'hy_a_log': (1, 32), 'hy_d_skip': (1, 32),
               'hy_ssm_norm_g': (1, 2048), 'rel_table': (32, 48), 'cv_b_pw1': (1, 2048), 'cv_w_dw': (1, 31, 1024),
               'cv_b_dw': (1, 1024), 'cv_ln_g': (1, 1024), 'cv_ln_b': (1, 1024), 'cv_b_pw2': (1, 1024),
               'final_norm_g': (1024,), 'loss': (1,)}

PACK_LAYOUT = (('hy_in_t', 2568), ('hy_out', 768), ('pw1_t', 512), ('pw2', 256),
               ('gate_t0', 704), ('up_t0', 704), ('down0', 704), ('gate_t1', 704), ('up_t1', 704), ('down1', 704))
PACK_ROWS = 8448


def _pack_offsets(layout):
    off, out = 0, {}
    for nm, r in layout:
        out[nm] = (off, r)
        off += r
    return out


PACK_OFF = _pack_offsets(PACK_LAYOUT)
W_BATCHES = ((0, 2624), (2624, 5248), (5248, 6336), (6336, 7424), (7424, 8448))
GA_LAYOUT = PACK_LAYOUT[1:]
GA_ROWS = 5888
GA_OFF = _pack_offsets(GA_LAYOUT)
G_BATCHES = ((0, 2560), (2560, 3712), (3712, 4864), (4864, 5888))
GB_LAYOUT = PACK_LAYOUT[:1]
GB_ROWS = 2816


def pack_grads(g, layout, n_rows):
    def rows_bf16(nm):
        return g[nm]

    parts = []
    for key, r in layout:
        if key == 'hy_in_t':
            a = hy_from_cat(rows_bf16('hy_in_t'))
        elif key.startswith('gate_t'):
            a = rows_bf16('gu_t' + key[-1])[:FFN_HIDDEN]
        elif key.startswith('up_t'):
            a = rows_bf16('gu_t' + key[-1])[FFN_HIDDEN:]
        else:
            a = rows_bf16(key)
        parts.append(a.reshape(N_CHIPS, r, D))
    used = sum(r for _, r in layout)
    return jnp.concatenate(parts + [jnp.zeros((N_CHIPS, n_rows - used, D), BF16)], axis=1)


def unpack_weights(full, skip=()):
    def whole(nm):
        o, r = PACK_OFF[nm]
        return full[:, o:o + r].reshape(N_CHIPS * r, D)

    out = {"hy_out": whole('hy_out'), "pw1_t": whole('pw1_t'), "pw2": whole('pw2'),
           "gu_t": [jnp.concatenate([whole(f'gate_t{i}'), whole(f'up_t{i}')], axis=0) for i in range(2)],
           "down": [whole(f'down{i}') for i in range(2)]}
    if "hy_in_t" not in skip:
        out["hy_in_t"] = hy_to_cat(whole('hy_in_t'))
    return out


def _to_lanes(flat):
    n = flat.shape[0]
    m = -(-n // (8 * LANES)) * 8
    return jnp.pad(flat, (0, m * LANES - n)).reshape(m, LANES)


def _split(flat, shapes):
    out, off = {}, 0
    for nm, shp in shapes.items():
        n = int(np.prod(shp))
        out[nm] = flat[off:off + n].reshape(shp)
        off += n
    return out


def kernel(x, c, ada_w, ada_b, norm_mix_g, norm_ffn_g, hy_w_in, hy_conv_w, hy_conv_b, hy_dt_bias, hy_a_log, hy_d_skip, hy_ssm_norm_g, hy_w_out, rel_table, cv_w_pw1, cv_b_pw1, cv_w_dw, cv_b_dw, cv_ln_g, cv_ln_b, cv_w_pw2, cv_b_pw2, ffn_w_gate, ffn_w_up, ffn_w_down, final_norm_g, loss_target, m_ada_w, m_ada_b, m_norm_mix_g, m_norm_ffn_g, m_hy_w_in, m_hy_conv_w, m_hy_conv_b, m_hy_dt_bias, m_hy_a_log, m_hy_d_skip, m_hy_ssm_norm_g, m_hy_w_out, m_rel_table, m_cv_w_pw1, m_cv_b_pw1, m_cv_w_dw, m_cv_b_dw, m_cv_ln_g, m_cv_ln_b, m_cv_w_pw2, m_cv_b_pw2, m_ffn_w_gate, m_ffn_w_up, m_ffn_w_down, m_final_norm_g, v_ada_w, v_ada_b, v_norm_mix_g, v_norm_ffn_g, v_hy_w_in, v_hy_conv_w, v_hy_conv_b, v_hy_dt_bias, v_hy_a_log, v_hy_d_skip, v_hy_ssm_norm_g, v_hy_w_out, v_rel_table, v_cv_w_pw1, v_cv_b_pw1, v_cv_w_dw, v_cv_b_dw, v_cv_ln_g, v_cv_ln_b, v_cv_w_pw2, v_cv_b_pw2, v_ffn_w_gate, v_ffn_w_up, v_ffn_w_down, v_final_norm_g):
    args = (x, c, ada_w, ada_b, norm_mix_g, norm_ffn_g, hy_w_in, hy_conv_w, hy_conv_b, hy_dt_bias, hy_a_log, hy_d_skip, hy_ssm_norm_g, hy_w_out, rel_table, cv_w_pw1, cv_b_pw1, cv_w_dw, cv_b_dw, cv_ln_g, cv_ln_b, cv_w_pw2, cv_b_pw2, ffn_w_gate, ffn_w_up, ffn_w_down, final_norm_g, loss_target, m_ada_w, m_ada_b, m_norm_mix_g, m_norm_ffn_g, m_hy_w_in, m_hy_conv_w, m_hy_conv_b, m_hy_dt_bias, m_hy_a_log, m_hy_d_skip, m_hy_ssm_norm_g, m_hy_w_out, m_rel_table, m_cv_w_pw1, m_cv_b_pw1, m_cv_w_dw, m_cv_b_dw, m_cv_ln_g, m_cv_ln_b, m_cv_w_pw2, m_cv_b_pw2, m_ffn_w_gate, m_ffn_w_up, m_ffn_w_down, m_final_norm_g, v_ada_w, v_ada_b, v_norm_mix_g, v_norm_ffn_g, v_hy_w_in, v_hy_conv_w, v_hy_conv_b, v_hy_dt_bias, v_hy_a_log, v_hy_d_skip, v_hy_ssm_norm_g, v_hy_w_out, v_rel_table, v_cv_w_pw1, v_cv_b_pw1, v_cv_w_dw, v_cv_b_dw, v_cv_ln_g, v_cv_ln_b, v_cv_w_pw2, v_cv_b_pw2, v_ffn_w_gate, v_ffn_w_up, v_ffn_w_down, v_final_norm_g)
    x_in, c_in = args[0], args[1]
    w = dict(zip(WEIGHTS, args[2:27], strict=True))
    tgt = args[27]
    m_in = dict(zip(WEIGHTS, args[28:53], strict=True))
    v_in = dict(zip(WEIGHTS, args[53:78], strict=True))
    xi, yi, ci = _place()
    chip = 2 * xi + yi
    dev = 2 * chip + ci

    cs = rowmap(f_silu, [c_in.reshape(8, LANES)], [], [F32], name="cond_silu", tr=8)[0]
    cs_all = allgather_small(cs, name="gather_cond").reshape(N_DEV, D)
    cs16 = jnp.pad(cs_all, ((0, 8), (0, 0)))
    modpart = jnp.stack([matmul(cs16, w['ada_w'][i], mode="nn", out_dtype=F32, name=f"ada_fwd{i}")[:N_DEV]
                         for i in range(2)], axis=1)
    shard_names = list(SMALL_SHARDED)
    payload = jnp.concatenate([modpart.reshape(-1)] + [w[nm].reshape(-1) for nm in shard_names])
    got = allgather_small(_to_lanes(payload), name="gather_mod").reshape(N_DEV, -1)[0::2]
    modparts = got[:, :modpart.size].reshape(N_CHIPS, N_DEV, 2, 1536)
    mine = lax.dynamic_index_in_dim(modparts, dev, axis=1, keepdims=False)
    mod = jnp.transpose(mine, (1, 0, 2)).reshape(2, 6 * D) + w['ada_b']
    mods = [[mod[i, j * D:(j + 1) * D].reshape(1, D) for j in range(6)] for i in range(2)]
    sp = {}
    off = modpart.size
    for nm in shard_names:
        shp = w[nm].shape
        n = int(np.prod(shp))
        parts = got[:, off:off + n].reshape((N_CHIPS,) + shp)
        sp[nm] = jnp.concatenate([parts[s] for s in range(N_CHIPS)], axis=-1)
        off += n

    def rows_of(nm, i=None):
        a = w[nm][0 if i is None else i]
        return (a.T if nm in ('hy_w_in', 'cv_w_pw1', 'ffn_w_gate', 'ffn_w_up') else a).astype(BF16)

    pieces = [rows_of('hy_w_in'), rows_of('hy_w_out'), rows_of('cv_w_pw1'), rows_of('cv_w_pw2')]
    for i in range(2):
        pieces += [rows_of('ffn_w_gate', i), rows_of('ffn_w_up', i), rows_of('ffn_w_down', i)]
    n_rows = sum(p.shape[0] for p in pieces)
    pack = jnp.concatenate(pieces + [jnp.zeros((PACK_ROWS - n_rows, D), BF16)], axis=0)
    full = side_call(GatherRows(pack, lax.empty((N_CHIPS, PACK_ROWS, D), BF16), *W_BATCHES[0]), name="gather_weights")
    o_in, r_in = PACK_OFF['hy_in_t']
    wts = {"hy_in_t": hy_to_cat(full[:, o_in:o_in + r_in].reshape(N_CHIPS * r_in, D))}
    comm = {"pack": pack, "full": full}

    sp = {"norm_mix_g": [w['norm_mix_g'][i].reshape(1, D) for i in range(2)],
          "norm_ffn_g": [w['norm_ffn_g'][i].reshape(1, D) for i in range(2)],
          "hy_conv_w": sp['hy_conv_w'][0], "hy_conv_b": w['hy_conv_b'],
          "hy_dt_bias": w['hy_dt_bias'].reshape(SSM_HEADS, 1), "hy_a_log": w['hy_a_log'].reshape(SSM_HEADS, 1),
          "hy_d_skip": w['hy_d_skip'].reshape(SSM_HEADS, 1), "hy_ssm_norm_g": w['hy_ssm_norm_g'],
          "rel_table": w['rel_table'], "cv_b_pw1": sp['cv_b_pw1'], "cv_w_dw": sp['cv_w_dw'][0], "cv_b_dw": sp['cv_b_dw'],
          "cv_ln_g": sp['cv_ln_g'], "cv_ln_b": sp['cv_ln_b'], "cv_b_pw2": sp['cv_b_pw2'],
          "final_norm_g": w['final_norm_g'].reshape(1, D)}

    loss_rows, grad_x, g, dmods = device_step(x_in[0], tgt[0], mods, wts, sp, comm)

    dmod = jnp.stack([jnp.concatenate([d.reshape(-1) for d in dmods[i]]) for i in range(2)])
    small = {'ada_b': dmod, 'norm_mix_g': jnp.stack([g[f'norm_mix_g{i}'].reshape(-1) for i in range(2)]),
             'norm_ffn_g': jnp.stack([g[f'norm_ffn_g{i}'].reshape(-1) for i in range(2)]),
             'loss': jnp.sum(loss_rows).reshape(1)}
    for nm in SMALL_GRADS:
        if nm not in small:
            small[nm] = g[nm]
    vec = _to_lanes(jnp.concatenate([small[nm].reshape(-1) for nm in SMALL_GRADS]))
    vec_all = allgather_small(vec, name="gather_small_grads")
    tot = _split(sum_devices(vec_all, name="sum_small_grads").reshape(-1), SMALL_GRADS)
    dmod_all = vec_all.reshape(N_DEV, -1)[:, :2 * 6 * D].reshape(N_DEV, 2, 6 * D)

    recv = comm["recv"]
    own_a = lax.dynamic_index_in_dim(comm["ga"], chip, axis=0, keepdims=False)
    part_a = rowmap(f_sum4, [own_a, recv[0], recv[1], recv[2]], [], [F32], name="sum_chip_grads")[0]
    red_a = rowmap(f_add, [part_a, sibling_swap(part_a, name="swap_grads")], [], [F32], name="sum_core_grads")[0]
    recv_b = comm["recv_b"]
    own_b = lax.dynamic_index_in_dim(comm["gb"], chip, axis=0, keepdims=False)
    mine_half = rowmap(f_sum4, [own_b, recv_b[0], recv_b[1], recv_b[2]], [], [F32], name="sum_in_chips")[0]
    its_half = sibling_swap(mine_half, name="swap_in")
    red_b = jnp.concatenate([jnp.where(ci == 0, mine_half, its_half), jnp.where(ci == 0, its_half, mine_half)], axis=0)

    def shard_grad(nm, i=None):
        key = {'hy_w_in': 'hy_in_t', 'hy_w_out': 'hy_out', 'cv_w_pw1': 'pw1_t', 'cv_w_pw2': 'pw2'}.get(nm)
        if key is None:
            key = {'ffn_w_gate': 'gate_t', 'ffn_w_up': 'up_t', 'ffn_w_down': 'down'}[nm] + str(i)
        if key == 'hy_in_t':
            a = red_b[:PACK_OFF[key][1]]
        else:
            o, r = GA_OFF[key]
            a = red_a[o:o + r]
        return a.T if key.endswith('_t') or key[:-1].endswith('_t') else a

    grads = {}
    grads['hy_w_in'] = shard_grad('hy_w_in')[None]
    grads['hy_w_out'] = shard_grad('hy_w_out')[None]
    grads['cv_w_pw1'] = shard_grad('cv_w_pw1')[None]
    grads['cv_w_pw2'] = shard_grad('cv_w_pw2')[None]
    for nm in ('ffn_w_gate', 'ffn_w_up', 'ffn_w_down'):
        grads[nm] = jnp.stack([shard_grad(nm, i) for i in range(2)])
    cs16 = jnp.pad(cs_all, ((0, 8), (0, 0)))
    dm_mine = lax.dynamic_slice_in_dim(dmod_all, chip * 1536, 1536, axis=2)
    dm16 = jnp.pad(dm_mine, ((0, 8), (0, 0), (0, 0)))
    grads['ada_w'] = jnp.stack([matmul(cs16, dm16[:, i], mode="tn", out_dtype=F32, name=f"ada_dw{i}") for i in range(2)])
    for nm, shp in SMALL_GRADS.items():
        if nm == 'loss':
            continue
        if nm in SMALL_SHARDED:
            n = w[nm].shape[-1]
            grads[nm] = lax.dynamic_slice_in_dim(tot[nm], chip * n, n, axis=len(shp) - 1)
        else:
            grads[nm] = tot[nm].reshape(w[nm].shape)

    delta, new_m, new_v = {}, {}, {}
    for nm in BIG:
        delta[nm], new_m[nm], new_v[nm] = adamw(w[nm], grads[nm], m_in[nm], v_in[nm], name="adamw_" + nm)
    smalls = [nm for nm in WEIGHTS if nm not in BIG]
    packed = [_to_lanes(jnp.concatenate([d[nm].reshape(-1) for nm in smalls])) for d in (w, grads, m_in, v_in)]
    res = rowmap(f_adamw, packed, [], [F32] * 3, name="adamw_small", tr=_rows_tile(packed[0].shape[0]))
    for d, r in zip((delta, new_m, new_v), res, strict=True):
        d.update(_split(r.reshape(-1), {nm: w[nm].shape for nm in smalls}))

    loss = tot['loss'].reshape(())
    return (loss, grad_x[None], *[grads[nm] for nm in WEIGHTS], *[delta[nm] for nm in WEIGHTS],
            *[new_m[nm] for nm in WEIGHTS], *[new_v[nm] for nm in WEIGHTS])
```

```python
import functools
import math

import jax
import jax.numpy as jnp
import numpy as np
from jax import lax
from jax.experimental import pallas as pl
from jax.experimental.pallas import tpu as pltpu

F32 = jnp.float32
BF16 = jnp.bfloat16
MESH = pl.DeviceIdType.MESH

D = 1024
S = 4096
EPS = 1e-6
SSM_INNER = 2048
SSM_HEADS = 32
SSM_HDIM = 64
SSM_GROUPS = 4
SSM_STATE = 128
SSM_CONVK = 4
SSM_CONV_DIM = 3072
CHUNK = 128
N_CHUNKS = S // CHUNK
ATT_HEADS = 16
ATT_HDIM = 64
ATT_PATTERNS = ((128, 1), (512, 4), (2048, 16))
ATT_BLK = 128
REL_BUCKETS = 32
REL_MAX_DIST = 2048
CONV_WIDTH = 31
FFN_HIDDEN = 2816
N_CHIPS = 4
N_DEV = 8
ADAM_LR, ADAM_B1, ADAM_B2, ADAM_EPS, ADAM_WD, ADAM_STEP = 0.001, 0.9, 0.999, 1e-08, 0.01, 10

VMEM_LIMIT_BYTES = 56 * 1024 * 1024
LANES = 128


def _cparams(*sem):
    return pltpu.CompilerParams(dimension_semantics=sem, vmem_limit_bytes=VMEM_LIMIT_BYTES)


def _pick(n, cap, mult=LANES):
    best = None
    for t in range(mult, min(n, cap) + 1, mult):
        if n % t == 0:
            best = t
    return best or n


def _dot(a, b, ca, cb):
    return lax.dot_general(a.astype(BF16), b.astype(BF16), (((ca,), (cb,)), ((), ())), preferred_element_type=F32)


@jax.custom_vjp
def mm(a, b):
    return _dot(a, b, 1, 0)


def _mm_fwd(a, b):
    return _dot(a, b, 1, 0), (a, b)


def _mm_bwd(res, g):
    a, b = res
    return _dot(g, b, 1, 1).astype(a.dtype), _dot(a, g, 0, 0).astype(b.dtype)


mm.defvjp(_mm_fwd, _mm_bwd)


@jax.custom_vjp
def mm_nt(a, b):
    return _dot(a, b, 1, 1)


def _mm_nt_fwd(a, b):
    return _dot(a, b, 1, 1), (a, b)


def _mm_nt_bwd(res, g):
    a, b = res
    return _dot(g, b, 1, 0).astype(a.dtype), _dot(g, a, 0, 0).astype(b.dtype)


mm_nt.defvjp(_mm_nt_fwd, _mm_nt_bwd)


@jax.custom_vjp
def mm_tn(a, b):
    return _dot(a, b, 0, 0)


def _mm_tn_fwd(a, b):
    return _dot(a, b, 0, 0), (a, b)


def _mm_tn_bwd(res, g):
    a, b = res
    return _dot(b, g, 1, 1).astype(a.dtype), _dot(a, g, 1, 0).astype(b.dtype)


mm_tn.defvjp(_mm_tn_fwd, _mm_tn_bwd)


def matmul(a, b, *, mode, out_dtype, name, n=None, b_off=0, tm_cap=1024, tn_cap=512, tk_cap=1536, side=None,
           out_t=False):
    if mode == "tn":
        k_dim, m_dim = a.shape
    else:
        m_dim, k_dim = a.shape
    n_dim = n if n is not None else (b.shape[0] if mode == "nt" else b.shape[1])
    tm = m_dim if m_dim < LANES else _pick(m_dim, tm_cap)
    tn = _pick(n_dim, tn_cap)
    tk = k_dim if k_dim < LANES else _pick(k_dim, tk_cap)
    assert m_dim % tm == 0 and n_dim % tn == 0 and k_dim % tk == 0 and b_off % tn == 0
    nk = k_dim // tk
    off = b_off // tn
    if mode == "nn":
        a_spec = pl.BlockSpec((tm, tk), lambda i, j, k: (i, k))
        b_spec = pl.BlockSpec((tk, tn), lambda i, j, k: (k, j))
        ca, cb = 1, 0
    elif mode == "nt":
        a_spec = pl.BlockSpec((tm, tk), lambda i, j, k: (i, k))
        b_spec = pl.BlockSpec((tn, tk), lambda i, j, k: (j + off, k))
        ca, cb = 1, 1
    else:
        a_spec = pl.BlockSpec((tk, tm), lambda i, j, k: (k, i))
        b_spec = pl.BlockSpec((tk, tn), lambda i, j, k: (k, j))
        ca, cb = 0, 0

    def emit(o_ref, val):
        o_ref[...] = (val.T if out_t else val).astype(o_ref.dtype)

    def body(a_ref, b_ref, o_ref, acc_ref):
        part = _dot(a_ref[...], b_ref[...], ca, cb)
        if nk == 1:
            emit(o_ref, part)
        else:
            k = pl.program_id(2)

            @pl.when(k == 0)
            def _():
                acc_ref[...] = part

            @pl.when(k > 0)
            def _():
                acc_ref[...] += part

            @pl.when(k == nk - 1)
            def _():
                emit(o_ref, acc_ref[...])

    if out_t:
        out_shape, out_spec = (n_dim, m_dim), pl.BlockSpec((tn, tm), lambda i, j, k: (j, i))
    else:
        out_shape, out_spec = (m_dim, n_dim), pl.BlockSpec((tm, tn), lambda i, j, k: (i, j))
    (out,), side_dst = grid_call(
        body, (a, b), name=name,
        out_shape=[jax.ShapeDtypeStruct(out_shape, out_dtype)],
        grid=(m_dim // tm, n_dim // tn, nk),
        in_specs=[a_spec, b_spec],
        out_specs=[out_spec],
        scratch_shapes=[pltpu.VMEM((tm, tn), F32)],
        semantics=("parallel", "parallel", "arbitrary"), side=side)
    return out if side is None else (out, side_dst)


def _f32(xs):
    return [x.astype(F32) for x in xs]


def rowmap(f, rows, consts, out_dtypes, *, name, tr=256):
    r_dim = rows[0].shape[0]
    tr = _pick(r_dim, tr, mult=8)
    assert r_dim % tr == 0
    nr, nc = len(rows), len(consts)
    outs = jax.eval_shape(lambda *xs: f(*xs), *[jax.ShapeDtypeStruct((tr, x.shape[1]), F32) for x in rows],
                          *[jax.ShapeDtypeStruct(x.shape, F32) for x in consts])

    def body(*refs):
        res = f(*_f32([r[...] for r in refs[:nr + nc]]))
        for o_ref, o in zip(refs[nr + nc:], res, strict=True):
            o_ref[...] = o.astype(o_ref.dtype)

    return pl.pallas_call(
        body, name=name,
        out_shape=[jax.ShapeDtypeStruct((r_dim, o.shape[1]), dt) for o, dt in zip(outs, out_dtypes, strict=True)],
        grid=(r_dim // tr,),
        in_specs=[pl.BlockSpec((tr, x.shape[1]), lambda i: (i, 0)) for x in rows]
        + [pl.BlockSpec(x.shape, lambda i: (0, 0)) for x in consts],
        out_specs=[pl.BlockSpec((tr, o.shape[1]), lambda i: (i, 0)) for o in outs],
        compiler_params=_cparams("parallel"),
    )(*rows, *consts)


def rowmap_bwd(f, rows, consts, cts, *, name, row_grad, row_dtypes=None, tr=256, emit=(), row_add=None):
    r_dim = rows[0].shape[0]
    tr = _pick(r_dim, tr, mult=8)
    assert r_dim % tr == 0
    nr, nc, nct = len(rows), len(consts), len(cts)
    gi = [i for i, flag in enumerate(row_grad) if flag]
    row_dtypes = row_dtypes or [F32] * len(gi)
    row_add = row_add or [None] * len(gi)
    adds = [a for a in row_add if a is not None]
    outs = jax.eval_shape(lambda *xs: f(*xs), *[jax.ShapeDtypeStruct((tr, x.shape[1]), F32) for x in rows],
                          *[jax.ShapeDtypeStruct(x.shape, F32) for x in consts])

    def body(*refs):
        ins = _f32([r[...] for r in refs[:nr + nc]])
        ct = _f32([r[...] for r in refs[nr + nc:nr + nc + nct]])
        add_refs = list(refs[nr + nc + nct:nr + nc + nct + len(adds)])
        o_refs = refs[nr + nc + nct + len(adds):]
        res, vjp = jax.vjp(f, *ins)
        grads = vjp(tuple(ct))
        for o_ref, i, a in zip(o_refs[:len(gi)], gi, row_add):
            g = grads[i] if a is None else grads[i] + add_refs.pop(0)[...].astype(F32)
            o_ref[...] = g.astype(o_ref.dtype)
        first = pl.program_id(0) == 0
        for o_ref, g in zip(o_refs[len(gi):len(gi) + nc], grads[nr:]):
            @pl.when(first)
            def _(o_ref=o_ref, g=g):
                o_ref[...] = g

            @pl.when(jnp.logical_not(first))
            def _(o_ref=o_ref, g=g):
                o_ref[...] += g
        for o_ref, i in zip(o_refs[len(gi) + nc:], emit):
            o_ref[...] = res[i].astype(o_ref.dtype)

    out_shape = ([jax.ShapeDtypeStruct(rows[i].shape, dt) for i, dt in zip(gi, row_dtypes, strict=True)]
                 + [jax.ShapeDtypeStruct(x.shape, F32) for x in consts]
                 + [jax.ShapeDtypeStruct((r_dim, outs[i].shape[1]), F32) for i in emit])
    out_specs = ([pl.BlockSpec((tr, rows[i].shape[1]), lambda i_: (i_, 0)) for i in gi]
                 + [pl.BlockSpec(x.shape, lambda i_: (0, 0)) for x in consts]
                 + [pl.BlockSpec((tr, outs[i].shape[1]), lambda i_: (i_, 0)) for i in emit])
    res = pl.pallas_call(
        body, name=name,
        out_shape=out_shape,
        grid=(r_dim // tr,),
        in_specs=[pl.BlockSpec((tr, x.shape[1]), lambda i: (i, 0)) for x in rows]
        + [pl.BlockSpec(x.shape, lambda i: (0, 0)) for x in consts]
        + [pl.BlockSpec((tr, x.shape[1]), lambda i: (i, 0)) for x in list(cts) + adds],
        out_specs=out_specs,
        compiler_params=_cparams("arbitrary"),
    )(*rows, *consts, *cts, *adds)
    return res[:len(gi)], res[len(gi):len(gi) + nc], res[len(gi) + nc:]


def transpose(a, *, name, out_dtype=BF16, tr=512, tc=512):
    r_dim, c_dim = a.shape
    tr, tc = _pick(r_dim, tr), _pick(c_dim, tc)

    def body(a_ref, o_ref):
        o_ref[...] = a_ref[...].astype(F32).T.astype(o_ref.dtype)

    return pl.pallas_call(
        body, name=name, out_shape=jax.ShapeDtypeStruct((c_dim, r_dim), out_dtype),
        grid=(r_dim // tr, c_dim // tc),
        in_specs=[pl.BlockSpec((tr, tc), lambda i, j: (i, j))],
        out_specs=pl.BlockSpec((tc, tr), lambda i, j: (j, i)),
        compiler_params=_cparams("parallel", "parallel"),
    )(a)


CONV_HALO = 32
CONV_ROWS = 256


def conv_fwd(x, w, b, *, name, cb=256, chunk_rows=CONV_ROWS):
    s_dim, c_dim = x.shape
    taps = w.shape[0]
    assert taps - 1 <= CONV_HALO and s_dim % chunk_rows == 0 and c_dim % cb == 0
    n_chunks = s_dim // chunk_rows
    ext = chunk_rows + CONV_HALO

    def body(x_ref, w_ref, b_ref, o_ref, xp_ref):
        xp_ref[pl.ds(0, CONV_HALO), :] = jnp.zeros((CONV_HALO, cb), F32)
        xp_ref[pl.ds(CONV_HALO, s_dim), :] = x_ref[...].astype(F32)
        wv = w_ref[...].astype(F32)
        bv = b_ref[...].astype(F32)

        def chunk(t, carry):
            base = pl.multiple_of(t * chunk_rows, chunk_rows)
            xe = xp_ref[pl.ds(base, ext), :]
            acc = jnp.broadcast_to(bv, (chunk_rows, cb))
            for j in range(taps):
                sh = xe if j == 0 else pltpu.roll(xe, shift=j, axis=0)
                acc = acc + wv[taps - 1 - j:taps - j, :] * sh[CONV_HALO:, :]
            o_ref[pl.ds(base, chunk_rows), :] = acc
            return carry

        lax.fori_loop(0, n_chunks, chunk, 0)

    return pl.pallas_call(
        body, name=name,
        out_shape=jax.ShapeDtypeStruct((s_dim, c_dim), F32),
        grid=(c_dim // cb,),
        in_specs=[pl.BlockSpec((s_dim, cb), lambda i: (0, i)), pl.BlockSpec((taps, cb), lambda i: (0, i)),
                  pl.BlockSpec((1, cb), lambda i: (0, i))],
        out_specs=pl.BlockSpec((s_dim, cb), lambda i: (0, i)),
        scratch_shapes=[pltpu.VMEM((s_dim + CONV_HALO, cb), F32)],
        compiler_params=_cparams("parallel"),
    )(x, w, b)


def conv_bwd(x, w, g, *, name, cb=256, chunk_rows=CONV_ROWS, dx_dtype=F32):
    s_dim, c_dim = x.shape
    taps = w.shape[0]
    n_chunks = s_dim // chunk_rows
    ext = chunk_rows + CONV_HALO

    def rows8(a):
        return jnp.sum(a.reshape(chunk_rows // 8, 8, cb), axis=0)

    def body(x_ref, w_ref, g_ref, dx_ref, dw_ref, db_ref, xp_ref, gp_ref, acc_ref):
        xp_ref[pl.ds(0, CONV_HALO), :] = jnp.zeros((CONV_HALO, cb), F32)
        xp_ref[pl.ds(CONV_HALO, s_dim), :] = x_ref[...].astype(F32)
        gp_ref[pl.ds(0, s_dim), :] = g_ref[...].astype(F32)
        gp_ref[pl.ds(s_dim, CONV_HALO), :] = jnp.zeros((CONV_HALO, cb), F32)
        acc_ref[...] = jnp.zeros_like(acc_ref)
        wv = w_ref[...].astype(F32)

        def chunk(t, carry):
            base = pl.multiple_of(t * chunk_rows, chunk_rows)
            xe = xp_ref[pl.ds(base, ext), :]
            ge = gp_ref[pl.ds(base, ext), :]
            gc = ge[:chunk_rows, :]
            dx = jnp.zeros((chunk_rows, cb), F32)
            for j in range(taps):
                xs = xe if j == 0 else pltpu.roll(xe, shift=j, axis=0)
                gs = ge if j == 0 else pltpu.roll(ge, shift=ext - j, axis=0)
                k = taps - 1 - j
                dx = dx + wv[k:k + 1, :] * gs[:chunk_rows, :]
                acc_ref[8 * k:8 * k + 8, :] += rows8(gc * xs[CONV_HALO:, :])
            acc_ref[8 * taps:8 * taps + 8, :] += rows8(gc)
            dx_ref[pl.ds(base, chunk_rows), :] = dx.astype(dx_ref.dtype)
            return carry

        lax.fori_loop(0, n_chunks, chunk, 0)
        sums = jnp.sum(acc_ref[...].reshape(taps + 1, 8, cb), axis=1)
        dw_ref[...] = sums[0:taps, :]
        db_ref[...] = sums[taps:taps + 1, :]

    return pl.pallas_call(
        body, name=name,
        out_shape=[jax.ShapeDtypeStruct((s_dim, c_dim), dx_dtype), jax.ShapeDtypeStruct((taps, c_dim), F32),
                   jax.ShapeDtypeStruct((1, c_dim), F32)],
        grid=(c_dim // cb,),
        in_specs=[pl.BlockSpec((s_dim, cb), lambda i: (0, i)), pl.BlockSpec((taps, cb), lambda i: (0, i)),
                  pl.BlockSpec((s_dim, cb), lambda i: (0, i))],
        out_specs=[pl.BlockSpec((s_dim, cb), lambda i: (0, i)), pl.BlockSpec((taps, cb), lambda i: (0, i)),
                   pl.BlockSpec((1, cb), lambda i: (0, i))],
        scratch_shapes=[pltpu.VMEM((s_dim + CONV_HALO, cb), F32), pltpu.VMEM((s_dim + CONV_HALO, cb), F32),
                        pltpu.VMEM((8 * (taps + 1), cb), F32)],
        compiler_params=_cparams("parallel"),
    )(x, w, g)


def _iota2(n, axis):
    return lax.broadcasted_iota(jnp.int32, (n, n), axis)


def _to_col(row):
    n = row.shape[1]
    return jnp.sum(jnp.where(_iota2(n, 0) == _iota2(n, 1), jnp.broadcast_to(row, (n, n)), 0.0), axis=1, keepdims=True)


def _softplus(x):
    return jnp.maximum(x, 0.0) + jnp.log(1.0 + jnp.exp(-jnp.abs(x)))


def ssd_heads(x, dtraw, dt_bias, a_log, dskip, bm, cm, prev):
    h, q, _ = x.shape
    n = bm.shape[1]
    li = lax.broadcasted_iota(jnp.int32, (1, q, q), 1)
    si = lax.broadcasted_iota(jnp.int32, (1, q, q), 2)

    def to_col(row):
        return jnp.sum(jnp.where(li == si, jnp.broadcast_to(row, (h, q, q)), 0.0), axis=2, keepdims=True)

    dt_row = _softplus(dtraw + dt_bias)
    a_row = dt_row * (-jnp.exp(a_log))
    a_col = to_col(a_row)
    acs_col = jnp.sum(jnp.where(si <= li, jnp.broadcast_to(a_row, (h, q, q)), 0.0), axis=2, keepdims=True)
    acs_row = jnp.sum(jnp.where(li <= si, jnp.broadcast_to(a_col, (h, q, q)), 0.0), axis=1, keepdims=True)
    total = jnp.sum(a_row, axis=2, keepdims=True)
    xdt = x * to_col(dt_row)
    lmat = jnp.exp(jnp.where(li >= si, acs_col - acs_row, -1e30))
    bmb = jnp.broadcast_to(bm[None], (h, q, n))
    cmb = jnp.broadcast_to(cm[None], (h, q, n))
    y = bmm(mm_nt(cm, bm)[None] * lmat, xdt)
    y = y + bmm_nt(cmb, prev) * jnp.exp(acs_col)
    y = y + dskip * x
    state = bmm_tn(xdt * jnp.exp(total - acs_col), bmb)
    return y, jnp.exp(total) * prev + state


HEADS_PER_GROUP = SSM_HEADS // SSM_GROUPS
BM_COL0 = SSM_INNER // SSM_STATE
CM_COL0 = BM_COL0 + SSM_GROUPS


def ssd_fwd(xs_hm, dtraw_t, dt_bias, a_log, dskip, xbc, side=None):
    hg = HEADS_PER_GROUP

    def body(x_ref, dt_ref, dtb_ref, al_ref, dk_ref, bm_ref, cm_ref, y_ref, prev_ref, state_ref):
        @pl.when(pl.program_id(1) == 0)
        def _():
            state_ref[...] = jnp.zeros_like(state_ref)

        prev = state_ref[...]
        prev_ref[0] = prev
        y, nxt = ssd_heads(x_ref[...], dt_ref[...], dtb_ref[...], al_ref[...], dk_ref[...], bm_ref[...], cm_ref[...], prev)
        y_ref[...] = y
        state_ref[...] = nxt

    hp = pl.BlockSpec((hg, 1, 1), lambda g, c: (g, 0, 0))
    dtraw_t, dt_bias, a_log, dskip = [a.reshape(SSM_HEADS, 1, -1) for a in (dtraw_t, dt_bias, a_log, dskip)]
    return grid_call(
        body, (xs_hm, dtraw_t, dt_bias, a_log, dskip, xbc, xbc), name="ssd_fwd",
        out_shape=[jax.ShapeDtypeStruct((SSM_HEADS, S, SSM_HDIM), F32),
                   jax.ShapeDtypeStruct((N_CHUNKS, SSM_HEADS, SSM_HDIM, SSM_STATE), F32)],
        grid=(SSM_GROUPS, N_CHUNKS),
        in_specs=[pl.BlockSpec((hg, CHUNK, SSM_HDIM), lambda g, c: (g, c, 0)),
                  pl.BlockSpec((hg, 1, CHUNK), lambda g, c: (g, 0, c)), hp, hp, hp,
                  pl.BlockSpec((CHUNK, SSM_STATE), lambda g, c: (c, BM_COL0 + g)),
                  pl.BlockSpec((CHUNK, SSM_STATE), lambda g, c: (c, CM_COL0 + g))],
        out_specs=[pl.BlockSpec((hg, CHUNK, SSM_HDIM), lambda g, c: (g, c, 0)),
                   pl.BlockSpec((1, hg, SSM_HDIM, SSM_STATE), lambda g, c: (c, g, 0, 0))],
        scratch_shapes=[pltpu.VMEM((hg, SSM_HDIM, SSM_STATE), F32)],
        semantics=("parallel", "arbitrary"), side=side)


def ssd_bwd(xs_hm, dtraw_t, dt_bias, a_log, dskip, xbc, prev_all, dy_hm, side=None):
    hg = HEADS_PER_GROUP
    last = N_CHUNKS - 1

    def body(x_ref, dt_ref, dtb_ref, al_ref, dk_ref, bm_ref, cm_ref, prev_ref, dy_ref,
             dx_ref, ddt_ref, ddtb_ref, dal_ref, ddk_ref, dbm_ref, dcm_ref, dstate_ref):
        @pl.when(pl.program_id(1) == 0)
        def _():
            dstate_ref[...] = jnp.zeros_like(dstate_ref)
            ddtb_ref[...] = jnp.zeros_like(ddtb_ref)
            dal_ref[...] = jnp.zeros_like(dal_ref)
            ddk_ref[...] = jnp.zeros_like(ddk_ref)

        _, vjp = jax.vjp(ssd_heads, x_ref[...], dt_ref[...], dtb_ref[...], al_ref[...], dk_ref[...], bm_ref[...],
                         cm_ref[...], prev_ref[0])
        dx, ddt, ddtb, dal, ddk, dbm, dcm, dprev = vjp((dy_ref[...], dstate_ref[...]))
        dx_ref[...] = dx
        ddt_ref[...] = ddt
        ddtb_ref[...] += ddtb
        dal_ref[...] += dal
        ddk_ref[...] += ddk
        dbm_ref[...] = dbm
        dcm_ref[...] = dcm
        dstate_ref[...] = dprev

    hp = pl.BlockSpec((hg, 1, 1), lambda g, c: (g, 0, 0))
    xspec = pl.BlockSpec((hg, CHUNK, SSM_HDIM), lambda g, c: (g, last - c, 0))
    tspec = pl.BlockSpec((hg, 1, CHUNK), lambda g, c: (g, 0, last - c))
    gspec = pl.BlockSpec((CHUNK, SSM_STATE), lambda g, c: (last - c, g))
    dtraw_t, dt_bias, a_log, dskip = [a.reshape(SSM_HEADS, 1, -1) for a in (dtraw_t, dt_bias, a_log, dskip)]
    res, side_dst = grid_call(
        body, (xs_hm, dtraw_t, dt_bias, a_log, dskip, xbc, xbc, prev_all, dy_hm), name="ssd_bwd",
        out_shape=[jax.ShapeDtypeStruct((SSM_HEADS, S, SSM_HDIM), F32), jax.ShapeDtypeStruct((SSM_HEADS, 1, S), F32),
                   jax.ShapeDtypeStruct((SSM_HEADS, 1, 1), F32), jax.ShapeDtypeStruct((SSM_HEADS, 1, 1), F32),
                   jax.ShapeDtypeStruct((SSM_HEADS, 1, 1), F32),
                   jax.ShapeDtypeStruct((S, SSM_GROUPS * SSM_STATE), F32),
                   jax.ShapeDtypeStruct((S, SSM_GROUPS * SSM_STATE), F32)],
        grid=(SSM_GROUPS, N_CHUNKS),
        in_specs=[xspec, tspec, hp, hp, hp,
                  pl.BlockSpec((CHUNK, SSM_STATE), lambda g, c: (last - c, BM_COL0 + g)),
                  pl.BlockSpec((CHUNK, SSM_STATE), lambda g, c: (last - c, CM_COL0 + g)),
                  pl.BlockSpec((1, hg, SSM_HDIM, SSM_STATE), lambda g, c: (last - c, g, 0, 0)), xspec],
        out_specs=[xspec, tspec, hp, hp, hp, gspec, gspec],
        scratch_shapes=[pltpu.VMEM((hg, SSM_HDIM, SSM_STATE), F32)],
        semantics=("parallel", "arbitrary"), side=side)
    return [res[0]] + [r.reshape(SSM_HEADS, -1) for r in res[1:5]] + list(res[5:]), side_dst


ATT_HB = 8


def _bdot(a, b, ca, cb):
    return lax.dot_general(a.astype(BF16), b.astype(BF16), (((ca,), (cb,)), ((0,), (0,))), preferred_element_type=F32)


@jax.custom_vjp
def bmm(a, b):
    return _bdot(a, b, 2, 1)


def _bmm_fwd(a, b):
    return _bdot(a, b, 2, 1), (a, b)


def _bmm_bwd(res, g):
    a, b = res
    return _bdot(g, b, 2, 2).astype(a.dtype), _bdot(a, g, 1, 1).astype(b.dtype)


bmm.defvjp(_bmm_fwd, _bmm_bwd)


@jax.custom_vjp
def bmm_nt(a, b):
    return _bdot(a, b, 2, 2)


def _bmm_nt_fwd(a, b):
    return _bdot(a, b, 2, 2), (a, b)


def _bmm_nt_bwd(res, g):
    a, b = res
    return _bdot(g, b, 2, 1).astype(a.dtype), _bdot(g, a, 1, 1).astype(b.dtype)


bmm_nt.defvjp(_bmm_nt_fwd, _bmm_nt_bwd)


@jax.custom_vjp
def bmm_tn(a, b):
    return _bdot(a, b, 1, 1)


def _bmm_tn_fwd(a, b):
    return _bdot(a, b, 1, 1), (a, b)


def _bmm_tn_bwd(res, g):
    a, b = res
    return _bdot(b, g, 2, 2).astype(a.dtype), _bdot(a, g, 2, 1).astype(b.dtype)


bmm_tn.defvjp(_bmm_tn_fwd, _bmm_tn_bwd)


def att_heads(q, kp, kc, vp, vc, bias_p, bias_c, has_prev):
    h, b, dh = q.shape
    i = lax.broadcasted_iota(jnp.int32, (1, b, b), 1)
    j = lax.broadcasted_iota(jnp.int32, (1, b, b), 2)
    scale = dh ** -0.5
    sp = jnp.where(jnp.logical_and(j >= i, has_prev), bmm_nt(q, kp) * scale + bias_p, -1e30)
    sc = jnp.where(j <= i, bmm_nt(q, kc) * scale + bias_c, -1e30)
    m = lax.stop_gradient(jnp.maximum(jnp.max(sp, axis=2, keepdims=True), jnp.max(sc, axis=2, keepdims=True)))
    pp, pc = jnp.exp(sp - m), jnp.exp(sc - m)
    l = jnp.sum(pp, axis=2, keepdims=True) + jnp.sum(pc, axis=2, keepdims=True)
    o = bmm(pp / l, vp) + bmm(pc / l, vc)
    return o, jnp.broadcast_to(m + jnp.log(l), (h, b, dh))


def _att_specs(nb):
    hb, blk = ATT_HB, ATT_BLK
    cur = pl.BlockSpec((hb, blk, ATT_HDIM), lambda h, b: (h, b, 0))
    prv = pl.BlockSpec((hb, blk, ATT_HDIM), lambda h, b: (h, jnp.maximum(b - 1, 0), 0))
    bias = pl.BlockSpec((hb, 2, blk, blk), lambda h, b: (h, 0, 0, 0))
    return cur, prv, bias


def att_fwd(q, k, v, bias, nb, *, name):
    cur, prv, bspec = _att_specs(nb)

    def body(q_ref, kp_ref, kc_ref, vp_ref, vc_ref, b_ref, o_ref, l_ref):
        has_prev = (pl.program_id(1) % nb) != 0
        o, lse = att_heads(q_ref[...], kp_ref[...], kc_ref[...], vp_ref[...], vc_ref[...], b_ref[:, 0], b_ref[:, 1],
                           has_prev)
        o_ref[...] = o
        l_ref[...] = lse

    shp = jax.ShapeDtypeStruct((ATT_HEADS, S, ATT_HDIM), F32)
    return pl.pallas_call(
        body, name=name, out_shape=[shp, shp],
        grid=(ATT_HEADS // ATT_HB, S // ATT_BLK),
        in_specs=[cur, prv, cur, prv, cur, bspec],
        out_specs=[cur, cur],
        compiler_params=_cparams("parallel", "parallel"),
    )(q, k, k, v, v, bias)


def att_bwd(q, k, v, bias, do, dlse, nb, *, name):
    cur, prv, bspec = _att_specs(nb)

    def body(q_ref, kp_ref, kc_ref, vp_ref, vc_ref, b_ref, do_ref, dl_ref,
             dq_ref, dkc_ref, dkp_ref, dvc_ref, dvp_ref, db_ref):
        has_prev = (pl.program_id(1) % nb) != 0

        @pl.when(pl.program_id(1) == 0)
        def _():
            db_ref[...] = jnp.zeros_like(db_ref)

        ins = _f32([q_ref[...], kp_ref[...], kc_ref[...], vp_ref[...], vc_ref[...]]) + [b_ref[:, 0], b_ref[:, 1]]
        _, vjp = jax.vjp(functools.partial(att_heads, has_prev=has_prev), *ins)
        dq, dkp, dkc, dvp, dvc, dbp, dbc = vjp((do_ref[...], dl_ref[...]))
        dq_ref[...] = dq
        dkc_ref[...] = dkc
        dkp_ref[...] = dkp
        dvc_ref[...] = dvc
        dvp_ref[...] = dvp
        db_ref[:, 0] += dbp
        db_ref[:, 1] += dbc

    shp = jax.ShapeDtypeStruct((ATT_HEADS, S, ATT_HDIM), F32)
    return pl.pallas_call(
        body, name=name,
        out_shape=[shp] * 5 + [jax.ShapeDtypeStruct((ATT_HEADS, 2, ATT_BLK, ATT_BLK), F32)],
        grid=(ATT_HEADS // ATT_HB, S // ATT_BLK),
        in_specs=[cur, prv, cur, prv, cur, bspec, cur, cur],
        out_specs=[cur] * 5 + [bspec],
        compiler_params=_cparams("parallel", "arbitrary"),
    )(q, k, k, v, v, bias, do, dlse)


def shift_add(cur, prev, nb, *, name):
    n_blocks = S // ATT_BLK

    def body(c_ref, p_ref, o_ref):
        nxt = pl.program_id(0) + 1
        keep = jnp.where((nxt % nb) != 0, 1.0, 0.0)
        o_ref[...] = c_ref[...] + keep * p_ref[...]

    return pl.pallas_call(
        body, name=name, out_shape=jax.ShapeDtypeStruct(cur.shape, F32),
        grid=(n_blocks,),
        in_specs=[pl.BlockSpec((ATT_HEADS, ATT_BLK, ATT_HDIM), lambda b: (0, b, 0)),
                  pl.BlockSpec((ATT_HEADS, ATT_BLK, ATT_HDIM), lambda b: (0, jnp.minimum(b + 1, n_blocks - 1), 0))],
        out_specs=pl.BlockSpec((ATT_HEADS, ATT_BLK, ATT_HDIM), lambda b: (0, b, 0)),
        compiler_params=_cparams("parallel"),
    )(cur, prev)


ATT_PAIRS = ATT_HEADS // 2
PAIR_W = 2 * ATT_HDIM


def att_pairs(q, kp, kc, vp, vc, bias, has_prev):
    t, b, w = q.shape
    i = lax.broadcasted_iota(jnp.int32, (1, b, b), 1)
    j = lax.broadcasted_iota(jnp.int32, (1, b, b), 2)
    first = lax.broadcasted_iota(jnp.int32, (1, 1, w), 2) < ATT_HDIM
    scale = ATT_HDIM ** -0.5
    outs, lses = [], []
    for ab in range(2):
        qh = jnp.where(first if ab == 0 else jnp.logical_not(first), q, 0.0)
        sp = jnp.where(jnp.logical_and(j >= i, has_prev), bmm_nt(qh, kp) * scale + bias[:, ab, 0], -1e30)
        sc = jnp.where(j <= i, bmm_nt(qh, kc) * scale + bias[:, ab, 1], -1e30)
        m = lax.stop_gradient(jnp.maximum(jnp.max(sp, axis=2, keepdims=True), jnp.max(sc, axis=2, keepdims=True)))
        pp, pc = jnp.exp(sp - m), jnp.exp(sc - m)
        l = jnp.sum(pp, axis=2, keepdims=True) + jnp.sum(pc, axis=2, keepdims=True)
        outs.append(bmm(pp / l, vp) + bmm(pc / l, vc))
        lses.append(jnp.broadcast_to(m + jnp.log(l), (t, b, w)))
    return jnp.where(first, outs[0], outs[1]), jnp.where(first, lses[0], lses[1])


def _pair_tiles(ref):
    return jnp.stack([ref[:, PAIR_W * t:PAIR_W * (t + 1)] for t in range(ATT_PAIRS)])


def _store_pair_tiles(ref, val):
    for t in range(ATT_PAIRS):
        ref[:, PAIR_W * t:PAIR_W * (t + 1)] = val[t].astype(ref.dtype)


def pair_bias(bias):
    return bias.reshape(ATT_PAIRS, 2, 2, ATT_BLK, ATT_BLK)


def att2_fwd(q, k, v, bias, nb, cols, *, name, side=None):
    n_blocks = S // ATT_BLK
    qc, kc, vc = cols

    def body(q_ref, k_ref, v_ref, b_ref, o_ref, l_ref, kprev, vprev):
        blk = pl.program_id(0)

        @pl.when(blk == 0)
        def _():
            kprev[...] = jnp.zeros_like(kprev)
            vprev[...] = jnp.zeros_like(vprev)

        k3, v3 = _pair_tiles(k_ref), _pair_tiles(v_ref)
        o, lse = att_pairs(_pair_tiles(q_ref), kprev[...], k3, vprev[...], v3, b_ref[...], (blk % nb) != 0)
        _store_pair_tiles(o_ref, o)
        _store_pair_tiles(l_ref, lse)
        kprev[...] = k3
        vprev[...] = v3

    def spec(c):
        return pl.BlockSpec((ATT_BLK, D), lambda b: (b, c))

    return grid_call(
        body, (q, k, v, bias), name=name,
        out_shape=[jax.ShapeDtypeStruct((S, D), BF16), jax.ShapeDtypeStruct((S, D), F32)], grid=(n_blocks,),
        in_specs=[spec(qc), spec(kc), spec(vc), pl.BlockSpec(bias.shape, lambda b: (0, 0, 0, 0, 0))],
        out_specs=[spec(0), spec(0)],
        scratch_shapes=[pltpu.VMEM((ATT_PAIRS, ATT_BLK, PAIR_W), BF16), pltpu.VMEM((ATT_PAIRS, ATT_BLK, PAIR_W), BF16)],
        semantics=("arbitrary",), side=side)


def att2_bwd(q, k, v, bias, do, dlse, nb, cols, *, name, side=None):
    n_blocks = S // ATT_BLK
    qc, kc, vc = cols

    def body(q_ref, k_ref, v_ref, b_ref, do_ref, dl_ref, dq_ref, dk_ref, dv_ref, db_ref, kprev, vprev, dk_own, dv_own):
        blk = pl.program_id(0)

        @pl.when(blk == 0)
        def _():
            for r in (kprev, vprev, dk_own, dv_own, db_ref):
                r[...] = jnp.zeros_like(r)

        @pl.when(blk < n_blocks)
        def _():
            k3, v3 = _pair_tiles(k_ref), _pair_tiles(v_ref)
            ins = _f32([_pair_tiles(q_ref), kprev[...], k3, vprev[...], v3]) + [b_ref[...]]
            _, vjp = jax.vjp(functools.partial(att_pairs, has_prev=(blk % nb) != 0), *ins)
            dq, dkp, dkc, dvp, dvc, db = vjp(tuple(_f32([_pair_tiles(do_ref), _pair_tiles(dl_ref)])))
            _store_pair_tiles(dq_ref, dq)
            _store_pair_tiles(dk_ref, dk_own[...] + dkp)
            _store_pair_tiles(dv_ref, dv_own[...] + dvp)
            dk_own[...] = dkc
            dv_own[...] = dvc
            db_ref[...] += db
            kprev[...] = k3
            vprev[...] = v3

        @pl.when(blk == n_blocks)
        def _():
            _store_pair_tiles(dk_ref, dk_own[...])
            _store_pair_tiles(dv_ref, dv_own[...])

    def spec(c):
        return pl.BlockSpec((ATT_BLK, D), lambda b: (jnp.minimum(b, n_blocks - 1), c))

    late = pl.BlockSpec((ATT_BLK, D), lambda b: (jnp.maximum(b - 1, 0), 0))
    bspec = pl.BlockSpec(bias.shape, lambda b: (0, 0, 0, 0, 0))
    tile_f32 = pltpu.VMEM((ATT_PAIRS, ATT_BLK, PAIR_W), F32)
    tile_bf16 = pltpu.VMEM((ATT_PAIRS, ATT_BLK, PAIR_W), BF16)
    return grid_call(
        body, (q, k, v, bias, do, dlse), name=name,
        out_shape=[jax.ShapeDtypeStruct((S, D), BF16), jax.ShapeDtypeStruct((S, D), BF16),
                   jax.ShapeDtypeStruct((S, D), BF16), jax.ShapeDtypeStruct(bias.shape, F32)],
        grid=(n_blocks + 1,),
        in_specs=[spec(qc), spec(kc), spec(vc), bspec, spec(0), spec(0)],
        out_specs=[spec(0), late, late, bspec],
        scratch_shapes=[tile_bf16, tile_bf16, tile_f32, tile_f32],
        semantics=("arbitrary",), side=side)


def regroup(a, dil, inverse=False):
    if dil == 1:
        return a
    c_dim = a.shape[1]
    shape = (dil, S // dil, c_dim) if inverse else (S // dil, dil, c_dim)
    return jnp.transpose(a.reshape(shape), (1, 0, 2)).reshape(S, c_dim)


SSD_PAIRS = SSM_HEADS // 2
PAIRS_PER_GROUP = SSD_PAIRS // SSM_GROUPS
GROUP_W = HEADS_PER_GROUP * SSM_HDIM


def ssd_pairs(x, dtraw, dt_bias, a_log, dskip, bm, cm, prev):
    t, q, w = x.shape
    n = bm.shape[1]
    li = lax.broadcasted_iota(jnp.int32, (1, q, q), 1)
    si = lax.broadcasted_iota(jnp.int32, (1, q, q), 2)
    first_lane = lax.broadcasted_iota(jnp.int32, (1, 1, w), 2) < SSM_HDIM
    first_row = lax.broadcasted_iota(jnp.int32, (1, w, 1), 1) < SSM_HDIM

    def to_col(row):
        return jnp.sum(jnp.where(li == si, jnp.broadcast_to(row, (t, q, q)), 0.0), axis=2, keepdims=True)

    def lanes(a0, a1):
        return jnp.where(first_lane, a0, a1)

    dt_col, acs_col, total, lmat = [], [], [], []
    for ab in range(2):
        dt_row = _softplus(dtraw[ab] + dt_bias[ab])
        a_row = dt_row * (-jnp.exp(a_log[ab]))
        a_col = to_col(a_row)
        acs_c = jnp.sum(jnp.where(si <= li, jnp.broadcast_to(a_row, (t, q, q)), 0.0), axis=2, keepdims=True)
        acs_r = jnp.sum(jnp.where(li <= si, jnp.broadcast_to(a_col, (t, q, q)), 0.0), axis=1, keepdims=True)
        dt_col.append(to_col(dt_row))
        acs_col.append(acs_c)
        total.append(jnp.sum(a_row, axis=2, keepdims=True))
        lmat.append(jnp.exp(jnp.where(li >= si, acs_c - acs_r, -1e30)))
    cb = mm_nt(cm, bm)[None]
    bmb = jnp.broadcast_to(bm[None], (t, q, n))
    cmb = jnp.broadcast_to(cm[None], (t, q, n))
    xdt = x * lanes(dt_col[0], dt_col[1])
    y = lanes(bmm(cb * lmat[0], xdt), bmm(cb * lmat[1], xdt))
    y = y + bmm_nt(cmb, prev) * lanes(jnp.exp(acs_col[0]), jnp.exp(acs_col[1]))
    y = y + lanes(dskip[0], dskip[1]) * x
    state = bmm_tn(xdt * lanes(jnp.exp(total[0] - acs_col[0]), jnp.exp(total[1] - acs_col[1])), bmb)
    return y, jnp.where(first_row, jnp.exp(total[0]), jnp.exp(total[1])) * prev + state


def _group_tiles(ref):
    return jnp.stack([ref[:, PAIR_W * t:PAIR_W * (t + 1)] for t in range(PAIRS_PER_GROUP)])


def _store_group_tiles(ref, val):
    for t in range(PAIRS_PER_GROUP):
        ref[:, PAIR_W * t:PAIR_W * (t + 1)] = val[t]


def _by_pair(a):
    return jnp.transpose(a.reshape(SSD_PAIRS, 2, 1, -1), (1, 0, 2, 3))


def _by_head(a):
    return jnp.transpose(a, (1, 0, 2, 3)).reshape(SSM_HEADS, -1)


def _ssd2_specs(chunk_of):
    tp = PAIRS_PER_GROUP
    xspec = pl.BlockSpec((CHUNK, GROUP_W), lambda g, c: (chunk_of(c), g))
    tspec = pl.BlockSpec((2, tp, 1, CHUNK), lambda g, c: (0, g, 0, chunk_of(c)))
    hp = pl.BlockSpec((2, tp, 1, 1), lambda g, c: (0, g, 0, 0))
    gspec = pl.BlockSpec((CHUNK, SSM_STATE), lambda g, c: (chunk_of(c), g))
    sspec = pl.BlockSpec((1, tp, PAIR_W, SSM_STATE), lambda g, c: (chunk_of(c), g, 0, 0))
    return xspec, tspec, hp, gspec, sspec


def ssd2_fwd(xs, dtraw_t, dt_bias, a_log, dskip, bm, cm, side=None):
    def body(x_ref, dt_ref, dtb_ref, al_ref, dk_ref, bm_ref, cm_ref, y_ref, prev_ref, state_ref):
        @pl.when(pl.program_id(1) == 0)
        def _():
            state_ref[...] = jnp.zeros_like(state_ref)

        prev = state_ref[...]
        prev_ref[0] = prev
        y, nxt = ssd_pairs(_group_tiles(x_ref), dt_ref[...], dtb_ref[...], al_ref[...], dk_ref[...], bm_ref[...],
                           cm_ref[...], prev)
        _store_group_tiles(y_ref, y)
        state_ref[...] = nxt

    xspec, tspec, hp, gspec, sspec = _ssd2_specs(lambda c: c)
    return grid_call(
        body, (xs, _by_pair(dtraw_t), _by_pair(dt_bias), _by_pair(a_log), _by_pair(dskip), bm, cm), name="ssd_fwd",
        out_shape=[jax.ShapeDtypeStruct((S, SSM_INNER), F32),
                   jax.ShapeDtypeStruct((N_CHUNKS, SSD_PAIRS, PAIR_W, SSM_STATE), F32)],
        grid=(SSM_GROUPS, N_CHUNKS), in_specs=[xspec, tspec, hp, hp, hp, gspec, gspec], out_specs=[xspec, sspec],
        scratch_shapes=[pltpu.VMEM((PAIRS_PER_GROUP, PAIR_W, SSM_STATE), F32)],
        semantics=("parallel", "arbitrary"), side=side)


def ssd2_bwd(xs, dtraw_t, dt_bias, a_log, dskip, bm, cm, prev_all, dy, side=None):
    def body(x_ref, dt_ref, dtb_ref, al_ref, dk_ref, bm_ref, cm_ref, prev_ref, dy_ref,
             dx_ref, ddt_ref, ddtb_ref, dal_ref, ddk_ref, dbm_ref, dcm_ref, dstate_ref):
        @pl.when(pl.program_id(1) == 0)
        def _():
            for r in (dstate_ref, ddtb_ref, dal_ref, ddk_ref):
                r[...] = jnp.zeros_like(r)

        _, vjp = jax.vjp(ssd_pairs, _group_tiles(x_ref), dt_ref[...], dtb_ref[...], al_ref[...], dk_ref[...], bm_ref[...],
                         cm_ref[...], prev_ref[0])
        dx, ddt, ddtb, dal, ddk, dbm, dcm, dprev = vjp((_group_tiles(dy_ref), dstate_ref[...]))
        _store_group_tiles(dx_ref, dx)
        ddt_ref[...] = ddt
        ddtb_ref[...] += ddtb
        dal_ref[...] += dal
        ddk_ref[...] += ddk
        dbm_ref[...] = dbm
        dcm_ref[...] = dcm
        dstate_ref[...] = dprev

    xspec, tspec, hp, gspec, sspec = _ssd2_specs(lambda c: N_CHUNKS - 1 - c)
    par = jax.ShapeDtypeStruct((2, SSD_PAIRS, 1, 1), F32)
    res, side_dst = grid_call(
        body, (xs, _by_pair(dtraw_t), _by_pair(dt_bias), _by_pair(a_log), _by_pair(dskip), bm, cm, prev_all, dy),
        name="ssd_bwd",
        out_shape=[jax.ShapeDtypeStruct((S, SSM_INNER), F32), jax.ShapeDtypeStruct((2, SSD_PAIRS, 1, S), F32), par, par, par,
                   jax.ShapeDtypeStruct((S, SSM_GROUPS * SSM_STATE), F32),
                   jax.ShapeDtypeStruct((S, SSM_GROUPS * SSM_STATE), F32)],
        grid=(SSM_GROUPS, N_CHUNKS), in_specs=[xspec, tspec, hp, hp, hp, gspec, gspec, sspec, xspec],
        out_specs=[xspec, tspec, hp, hp, hp, gspec, gspec],
        scratch_shapes=[pltpu.VMEM((PAIRS_PER_GROUP, PAIR_W, SSM_STATE), F32)],
        semantics=("parallel", "arbitrary"), side=side)
    return [res[0]] + [_by_head(r) for r in res[1:5]] + list(res[5:]), side_dst


def _silu(x):
    return x * jax.nn.sigmoid(x)


def _rms(x):
    return x * lax.rsqrt(jnp.mean(x * x, -1, keepdims=True) + EPS)


def f_normmod(x, g, sc, sh):
    return (_rms(x) * g * (1.0 + sc) + sh,)


def f_resid(x, mix, gate):
    return (x + gate * mix,)


def f_resid_bias(x, mix, gate, b):
    return (x + gate * (mix + b),)


def f_swiglu(hgu):
    return (_silu(hgu[:, :FFN_HIDDEN]) * hgu[:, FFN_HIDDEN:],)


def f_silu(x):
    return (_silu(x),)


def f_silu_xbc(x):
    y = _silu(x)
    n_b = SSM_GROUPS * SSM_STATE
    return y[:, :SSM_INNER], y[:, SSM_INNER:SSM_INNER + n_b], y[:, SSM_INNER + n_b:]


def f_gated_norm(y, z, g):
    return (_rms(y * _silu(z)) * g,)


def f_glu(y, b):
    y = y + b
    return (y[:, :D] * jax.nn.sigmoid(y[:, D:]),)


def f_ln_silu(u, g, b):
    mu = jnp.mean(u, -1, keepdims=True)
    var = jnp.mean(jnp.square(u - mu), -1, keepdims=True)
    return (_silu((u - mu) * lax.rsqrt(var + EPS) * g + b),)


def f_combine(o1, o2, o3, l1, l2, l3):
    m = lax.stop_gradient(jnp.maximum(jnp.maximum(l1, l2), l3))
    e1, e2, e3 = jnp.exp(l1 - m), jnp.exp(l2 - m), jnp.exp(l3 - m)
    return ((e1 * o1 + e2 * o2 + e3 * o3) / (e1 + e2 + e3),)


def f_head(x, tgt, g):
    return (0.5 * jnp.mean(jnp.square(_rms(x) * g - tgt), -1, keepdims=True),)


def f_sum3(a, b, c):
    return (a + b + c,)


def f_sum4(a, b, c, d):
    return (a + b + c + d,)


def f_add(a, b):
    return (a + b,)


def f_adamw(w, g, m, v):
    m = ADAM_B1 * m + (1.0 - ADAM_B1) * g
    v = ADAM_B2 * v + (1.0 - ADAM_B2) * jnp.square(g)
    m_hat = m / (1.0 - ADAM_B1 ** ADAM_STEP)
    v_hat = v / (1.0 - ADAM_B2 ** ADAM_STEP)
    return -ADAM_LR * (m_hat / (jnp.sqrt(v_hat) + ADAM_EPS) + ADAM_WD * w), m, v


def _rows_tile(r, cap=256):
    return _pick(r, cap, mult=8)


def adamw(w, g, m, v, *, name):
    shape = w.shape
    c_dim = shape[-1] if len(shape) > 1 else shape[0]
    flat = [a.reshape(-1, c_dim) for a in (w, g, m, v)]
    res = rowmap(f_adamw, flat, [], [F32] * 3, name=name, tr=_rows_tile(flat[0].shape[0], cap=128))
    return [r.reshape(shape) for r in res]


def _t5_bucket(dist):
    max_exact = REL_BUCKETS // 2
    n = jnp.maximum(dist, 1).astype(F32)
    large = max_exact + jnp.log(n / max_exact) / math.log(REL_MAX_DIST / max_exact) * (REL_BUCKETS - max_exact)
    large = jnp.minimum(large.astype(jnp.int32), REL_BUCKETS - 1)
    return jnp.where(dist < max_exact, dist, large)


def _att_buckets(dil):
    i = jnp.arange(ATT_BLK)[:, None]
    j = jnp.arange(2 * ATT_BLK)[None, :]
    bkt = _t5_bucket(jnp.maximum(ATT_BLK + i - j, 0) * dil)
    return jnp.transpose(bkt.reshape(ATT_BLK, 2, ATT_BLK), (1, 0, 2))


def att_bias(rel_table, p, dil):
    tab = rel_table[:, p * ATT_HEADS:(p + 1) * ATT_HEADS]
    onehot = (jnp.arange(REL_BUCKETS)[:, None] == _att_buckets(dil).reshape(1, -1)).astype(F32)
    bias = lax.dot_general(tab, onehot, (((0,), (0,)), ((), ())), precision=lax.Precision.HIGHEST)
    return bias.reshape(ATT_HEADS, 2, ATT_BLK, ATT_BLK)


def att_bias_grad(dbias, dil, *, name):
    onehot = (_att_buckets(dil).reshape(-1, 1) == jnp.arange(LANES)[None, :]).astype(BF16)
    dtab = matmul(dbias.reshape(ATT_HEADS, -1), onehot, mode="nn", out_dtype=F32, name=name, tk_cap=2048)
    return dtab[:, :REL_BUCKETS].T


def to_heads(a, n_heads, dil=1):
    hd = a.shape[1] // n_heads
    return jnp.transpose(a.reshape(S // dil, dil, n_heads, hd), (2, 1, 0, 3)).reshape(n_heads, S, hd)


def from_heads(a, dil=1):
    n_heads, _, hd = a.shape
    return jnp.transpose(a.reshape(n_heads, dil, S // dil, hd), (2, 1, 0, 3)).reshape(S, n_heads * hd)


def regroup_heads(a, dil, inverse=False):
    n_heads, _, hd = a.shape
    if dil == 1:
        return a
    if inverse:
        return jnp.transpose(a.reshape(n_heads, dil, S // dil, hd), (0, 2, 1, 3)).reshape(n_heads, S, hd)
    return jnp.transpose(a.reshape(n_heads, S // dil, dil, hd), (0, 2, 1, 3)).reshape(n_heads, S, hd)


HY_Z, HY_XBC, HY_DT, HY_Q, HY_K, HY_V = 2048, 3072, 32, 3072, 1024, 1024
HY_IN = HY_Z + HY_XBC + HY_DT + HY_Q + HY_K + HY_V
OFF_Z, OFF_XBC, OFF_Q, OFF_KV, OFF_DT = 0, 2048, 5120, 8192, 10240
HY_CAT = OFF_DT + LANES
DT_PAD = LANES


def hy_to_cat(w):
    z, xbc, dt, qkv = w[:2048], w[2048:5120], w[5120:5152], w[5152:]
    return jnp.concatenate([z, xbc, qkv, dt, jnp.zeros((DT_PAD - HY_DT,) + w.shape[1:], w.dtype)], axis=0)


def hy_from_cat(w, axis=0):
    part = lambda a, b: lax.slice_in_dim(w, a, b, axis=axis)
    return jnp.concatenate([part(0, 5120), part(OFF_DT, OFF_DT + HY_DT), part(5120, OFF_DT)], axis=axis)


def device_step(x, tgt, mods, wts, sp, comm=None):
    g = {}
    dmods = [[None] * 6 for _ in range(2)]
    wts = dict(wts)

    def wgrad(tokens_d, tokens_n, nm):
        return matmul(transpose(tokens_d, name=nm + "_t"), tokens_n, mode="nn", out_dtype=BF16, name=nm, out_t=True)

    def w_side(i):
        return None if comm is None else GatherRows(comm["pack"], comm["full"], *W_BATCHES[i])

    def g_side(i):
        return None if comm is None else ScatterRows(comm["ga"], comm["recv"], *G_BATCHES[i])

    def normmod(xi, gain, sc, sh, nm):
        return rowmap(f_normmod, [xi], [gain, sc, sh], [BF16], name=nm)[0]

    def ffn_fwd(xi, i, gate, nm):
        h = normmod(xi, sp["norm_ffn_g"][i], mods[i][4], mods[i][3], nm + "_norm")
        hgu = matmul(h, wts["gu_t"][i], mode="nt", out_dtype=BF16, name=nm + "_gu")
        act = rowmap(f_swiglu, [hgu], [], [BF16], name=nm + "_act", tr=128)[0]
        out = matmul(act, wts["down"][i], mode="nn", out_dtype=F32, name=nm + "_down")
        xo = rowmap(f_resid, [xi, out], [gate], [F32], name=nm + "_res")[0]
        return xo, (h, hgu, act, out)

    def ffn_bwd(dres, xi, i, saved, nm):
        h, hgu, act, out = saved
        (dout,), (dgate,), _ = rowmap_bwd(f_resid, [xi, out], [mods[i][5]], [dres], name=nm + "_res_b",
                                          row_grad=[False, True], row_dtypes=[BF16])
        dmods[i][5] = dgate
        dact = matmul(dout, wts["down"][i], mode="nt", out_dtype=BF16, name=nm + "_down_dx")
        g[f"down{i}"] = wgrad(dout, act, nm + "_down_dw")
        (dhgu,), _, _ = rowmap_bwd(f_swiglu, [hgu], [], [dact], name=nm + "_act_b", row_grad=[True],
                                   row_dtypes=[BF16], tr=128)
        g[f"gu_t{i}"] = wgrad(h, dhgu, nm + "_gu_dw")
        dh = matmul(dhgu, wts["gu_t"][i], mode="nn", out_dtype=F32, name=nm + "_gu_dx")
        (dres,), (dg_, dsc, dsh), _ = rowmap_bwd(f_normmod, [xi], [sp["norm_ffn_g"][i], mods[i][4], mods[i][3]], [dh],
                                                 name=nm + "_norm_b", row_grad=[True], row_add=[dres])
        g[f"norm_ffn_g{i}"] = dg_
        dmods[i][4], dmods[i][3] = dsc, dsh
        return dres

    h0 = normmod(x, sp["norm_mix_g"][0], mods[0][1], mods[0][0], "l0_norm")
    w_in = wts["hy_in_t"]
    z = matmul(h0, w_in, mode="nt", out_dtype=F32, name="hy_z", n=HY_Z, b_off=OFF_Z)
    xbc_raw = matmul(h0, w_in, mode="nt", out_dtype=F32, name="hy_xbc", n=HY_XBC, b_off=OFF_XBC)
    q = matmul(h0, w_in, mode="nt", out_dtype=BF16, name="hy_q", n=HY_Q, b_off=OFF_Q)
    kv = matmul(h0, w_in, mode="nt", out_dtype=BF16, name="hy_kv", n=HY_K + HY_V, b_off=OFF_KV)
    dtr = matmul(h0, w_in, mode="nt", out_dtype=F32, name="hy_dt", n=DT_PAD, b_off=OFF_DT)
    xbc_pre = conv_fwd(xbc_raw, sp["hy_conv_w"], sp["hy_conv_b"], name="hy_conv")
    xs, bm, cm = rowmap(f_silu_xbc, [xbc_pre], [], [F32] * 3, name="hy_conv_act", tr=128)
    dtraw_t = dtr[:, :HY_DT].T
    (y, prev_all), full = ssd2_fwd(xs, dtraw_t, sp["hy_dt_bias"], sp["hy_a_log"], sp["hy_d_skip"], bm, cm, side=w_side(1))
    if comm is not None:
        comm["full"] = full
    ysn = rowmap(f_gated_norm, [y, z], [sp["hy_ssm_norm_g"]], [BF16], name="hy_gnorm", tr=128)[0]
    att_in, att_o, att_l = [], [], []
    for p, (win, dil) in enumerate(ATT_PATTERNS):
        if dil == 1:
            qa, ka, va, cols = q, kv, kv, (p, 0, 1)
        else:
            qa, ka, cols = regroup(q[:, p * D:(p + 1) * D], dil), regroup(kv, dil), (0, 0, 1)
            va = ka
        bias = pair_bias(att_bias(sp["rel_table"], p, dil))
        nb = S // dil // ATT_BLK
        (o, lse), full = att2_fwd(qa, ka, va, bias, nb, cols, name=f"att_fwd{p}", side=w_side(2 + p))
        if comm is not None:
            comm["full"] = full
        att_in.append((qa, ka, va, bias, nb, cols))
        att_o.append(regroup(o, dil, inverse=True))
        att_l.append(regroup(lse, dil, inverse=True))
    if comm is not None:
        wts.update(unpack_weights(comm["full"], skip=("hy_in_t",)))
    att = rowmap(f_combine, att_o + att_l, [], [BF16], name="att_combine", tr=128)[0]
    cat = jnp.concatenate([ysn, att], axis=-1)
    mix0 = matmul(cat, wts["hy_out"], mode="nn", out_dtype=F32, name="hy_out")
    x1 = rowmap(f_resid, [x, mix0], [mods[0][2]], [F32], name="l0_res")[0]
    x2, ffn0 = ffn_fwd(x1, 0, mods[0][5], "ffn0")

    h1 = normmod(x2, sp["norm_mix_g"][1], mods[1][1], mods[1][0], "l1_norm")
    p1 = matmul(h1, wts["pw1_t"], mode="nt", out_dtype=F32, name="cv_pw1")
    u = rowmap(f_glu, [p1], [sp["cv_b_pw1"]], [F32], name="cv_glu")[0]
    uc = conv_fwd(u, sp["cv_w_dw"], sp["cv_b_dw"], name="cv_conv")
    ul = rowmap(f_ln_silu, [uc], [sp["cv_ln_g"], sp["cv_ln_b"]], [BF16], name="cv_ln")[0]
    mix1 = matmul(ul, wts["pw2"], mode="nn", out_dtype=F32, name="cv_pw2")
    x3 = rowmap(f_resid_bias, [x2, mix1], [mods[1][2], sp["cv_b_pw2"]], [F32], name="l1_res")[0]
    x4, ffn1 = ffn_fwd(x3, 1, mods[1][5], "ffn1")

    ones = jnp.ones((S, 1), F32)
    (dres,), (dfinal,), (loss_rows,) = rowmap_bwd(f_head, [x4, tgt], [sp["final_norm_g"]], [ones], name="head",
                                                  row_grad=[True, False], emit=(0,))
    g["final_norm_g"] = dfinal

    dres = ffn_bwd(dres, x3, 1, ffn1, "ffn1")
    (dmix1,), (dg1, db2), _ = rowmap_bwd(f_resid_bias, [x2, mix1], [mods[1][2], sp["cv_b_pw2"]], [dres], name="l1_res_b",
                                         row_grad=[False, True], row_dtypes=[BF16])
    dmods[1][2] = dg1
    g["cv_b_pw2"] = db2
    dul = matmul(dmix1, wts["pw2"], mode="nt", out_dtype=F32, name="cv_pw2_dx")
    g["pw2"] = wgrad(dmix1, ul, "cv_pw2_dw")
    (duc,), (g["cv_ln_g"], g["cv_ln_b"]), _ = rowmap_bwd(f_ln_silu, [uc], [sp["cv_ln_g"], sp["cv_ln_b"]], [dul],
                                                         name="cv_ln_b", row_grad=[True])
    du, g["cv_w_dw"], g["cv_b_dw"] = conv_bwd(u, sp["cv_w_dw"], duc, name="cv_conv_b", cb=128, chunk_rows=128)
    (dp1,), (g["cv_b_pw1"],), _ = rowmap_bwd(f_glu, [p1], [sp["cv_b_pw1"]], [du], name="cv_glu_b", row_grad=[True],
                                             row_dtypes=[BF16])
    g["pw1_t"] = wgrad(h1, dp1, "cv_pw1_dw")
    dh1 = matmul(dp1, wts["pw1_t"], mode="nn", out_dtype=F32, name="cv_pw1_dx")
    (dres,), (dg_, dsc, dsh), _ = rowmap_bwd(f_normmod, [x2], [sp["norm_mix_g"][1], mods[1][1], mods[1][0]], [dh1],
                                             name="l1_norm_b", row_grad=[True], row_add=[dres])
    g["norm_mix_g1"] = dg_
    dmods[1][1], dmods[1][0] = dsc, dsh

    dres = ffn_bwd(dres, x1, 0, ffn0, "ffn0")
    (dmix0,), (dg1,), _ = rowmap_bwd(f_resid, [x, mix0], [mods[0][2]], [dres], name="l0_res_b",
                                     row_grad=[False, True], row_dtypes=[BF16])
    dmods[0][2] = dg1
    dysn = matmul(dmix0, wts["hy_out"], mode="nt", out_dtype=F32, name="hy_out_dy", n=SSM_INNER, b_off=0)
    datt = matmul(dmix0, wts["hy_out"], mode="nt", out_dtype=F32, name="hy_out_da", n=D, b_off=SSM_INNER)
    g["hy_out"] = wgrad(dmix0, cat, "hy_out_dw")
    (dy, dz), (g["hy_ssm_norm_g"],), _ = rowmap_bwd(f_gated_norm, [y, z], [sp["hy_ssm_norm_g"]], [dysn], name="hy_gnorm_b",
                                                    row_grad=[True, True], row_dtypes=[F32, BF16], tr=128)
    if comm is not None:
        comm["ga"] = pack_grads(g, GA_LAYOUT, GA_ROWS)
        comm["recv"] = lax.empty((3, GA_ROWS, D), BF16)
    (dxs, ddtraw_t, g["hy_dt_bias"], g["hy_a_log"], g["hy_d_skip"], dbm, dcm), recv = ssd2_bwd(
        xs, dtraw_t, sp["hy_dt_bias"], sp["hy_a_log"], sp["hy_d_skip"], bm, cm, prev_all, dy, side=g_side(0))
    if comm is not None:
        comm["recv"] = recv
    (dxbc_pre,), _, _ = rowmap_bwd(f_silu_xbc, [xbc_pre], [], [dxs, dbm, dcm], name="hy_conv_act_b", row_grad=[True],
                                   tr=128)
    dxbc_raw, g["hy_conv_w"], g["hy_conv_b"] = conv_bwd(xbc_raw, sp["hy_conv_w"], dxbc_pre, name="hy_conv_b", cb=128, chunk_rows=128, dx_dtype=BF16)
    dol, _, _ = rowmap_bwd(f_combine, att_o + att_l, [], [datt], name="att_combine_b", row_grad=[True] * 6,
                           row_dtypes=[BF16] * 3 + [F32] * 3, tr=128)
    dqs, dks, dvs, dtabs = [], [], [], []
    for p, (win, dil) in enumerate(ATT_PATTERNS):
        qa, ka, va, bias, nb, cols = att_in[p]
        (dq, dkp_, dvp_, dbias), recv = att2_bwd(qa, ka, va, bias, regroup(dol[p], dil), regroup(dol[3 + p], dil), nb,
                                                 cols, name=f"att_bwd{p}", side=g_side(1 + p))
        if comm is not None:
            comm["recv"] = recv
        dqs.append(regroup(dq, dil, inverse=True))
        dks.append(regroup(dkp_, dil, inverse=True))
        dvs.append(regroup(dvp_, dil, inverse=True))
        dtabs.append(att_bias_grad(dbias.reshape(ATT_HEADS, 2, ATT_BLK, ATT_BLK), dil, name=f"att_dtab{p}"))
    g["rel_table"] = jnp.concatenate(dtabs, axis=1)
    dk = rowmap(f_sum3, dks, [], [BF16], name="att_dk_sum")[0]
    dv = rowmap(f_sum3, dvs, [], [BF16], name="att_dv_sum")[0]
    ddt = jnp.pad(ddtraw_t.T, ((0, 0), (0, DT_PAD - HY_DT)))
    dproj = jnp.concatenate([dz, dxbc_raw] + dqs + [dk, dv, ddt.astype(BF16)], axis=-1)
    g["hy_in_t"] = wgrad(h0, dproj, "hy_in_dw")
    if comm is None:
        dh0 = matmul(dproj, w_in, mode="nn", out_dtype=F32, name="hy_in_dx")
    else:
        gb = pack_grads(g, GB_LAYOUT, GB_ROWS)
        half = GB_ROWS // 2
        theirs = swap_halves(gb, name="swap_in_halves")
        ours = lax.dynamic_slice_in_dim(gb, lax.axis_index("c") * half, half, axis=1)
        comm["gb"] = rowmap(f_add, [ours.reshape(N_CHIPS * half, D), theirs.reshape(N_CHIPS * half, D)], [], [BF16],
                            name="sum_in_cores")[0].reshape(N_CHIPS, half, D)
        dh0, comm["recv_b"] = matmul(dproj, w_in, mode="nn", out_dtype=F32, name="hy_in_dx",
                                     side=ScatterRows(comm["gb"], lax.empty((3, half, D), BF16), 0, half))
    (dres,), (dg_, dsc, dsh), _ = rowmap_bwd(f_normmod, [x], [sp["norm_mix_g"][0], mods[0][1], mods[0][0]], [dh0],
                                             name="l0_norm_b", row_grad=[True], row_add=[dres])
    g["norm_mix_g0"] = dg_
    dmods[0][1], dmods[0][0] = dsc, dsh
    return loss_rows, dres, g, dmods


ANY = pl.BlockSpec(memory_space=pl.ANY)
WHOLE_VMEM = pl.BlockSpec(memory_space=pltpu.VMEM)


def _place():
    return lax.axis_index("x"), lax.axis_index("y"), lax.axis_index("c")


def _other_chips(x, y):
    return [(1 - x, y), (x, 1 - y), (1 - x, 1 - y)]


def allgather_small(v, *, name):
    m_per = v.shape[0]

    def body(x_ref, out_ref, send_sems, recv_sems, local_sem):
        x, y, c = _place()
        me, sibling = (x, y, c), (x, y, 1 - c)
        chips = _other_chips(x, y)

        def rows(px, py, pc):
            return out_ref.at[pl.ds((4 * px + 2 * py + pc) * m_per, m_per), :]

        def copy(k, block, to, src=None):
            return pltpu.make_async_remote_copy(
                src_ref=rows(*block) if src is None else src, dst_ref=rows(*block),
                send_sem=send_sems.at[k], recv_sem=recv_sems.at[k], device_id=to, device_id_type=MESH)

        mine = pltpu.make_async_copy(x_ref, rows(*me), local_sem)
        mine.start()
        first = [copy(0, me, sibling, src=x_ref)]
        first += [copy(1 + j, me, (*chip, c), src=x_ref) for j, chip in enumerate(chips)]
        for cp in first:
            cp.start()
        passed = [copy(4 + j, (*chip, c), sibling) for j, chip in enumerate(chips)]
        for j, chip in enumerate(chips):
            copy(1 + j, (*chip, c), me).wait_recv()
            passed[j].start()
        copy(0, sibling, me).wait_recv()
        for j, chip in enumerate(chips):
            copy(4 + j, (*chip, 1 - c), me).wait_recv()
        for cp in first + passed:
            cp.wait_send()
        mine.wait()

    return pl.pallas_call(
        body, name=name,
        out_shape=jax.ShapeDtypeStruct((N_DEV * m_per, LANES), v.dtype),
        in_specs=[WHOLE_VMEM], out_specs=WHOLE_VMEM,
        scratch_shapes=[pltpu.SemaphoreType.DMA((7,)), pltpu.SemaphoreType.DMA((7,)), pltpu.SemaphoreType.DMA],
    )(v)


def allgather_chips(pack, *, name):
    half_rows = pack.shape[0] // 2

    def body(p_ref, o_ref, send_sems, recv_sems, local_sem):
        x, y, c = _place()
        chips = _other_chips(x, y)
        sibling = (x, y, 1 - c)
        my_half = pl.ds(c * half_rows, half_rows)
        its_half = pl.ds((1 - c) * half_rows, half_rows)
        mine = pltpu.make_async_copy(p_ref, o_ref.at[2 * x + y], local_sem)
        mine.start()
        sends = [pltpu.make_async_remote_copy(
            src_ref=p_ref.at[my_half], dst_ref=o_ref.at[2 * x + y, my_half],
            send_sem=send_sems.at[k], recv_sem=recv_sems.at[k],
            device_id=(cx, cy, c), device_id_type=MESH) for k, (cx, cy) in enumerate(chips)]
        for cp in sends:
            cp.start()
        passed = []
        for k, (cx, cy) in enumerate(chips):
            landed = o_ref.at[2 * cx + cy, my_half]
            pltpu.make_async_remote_copy(
                src_ref=p_ref.at[my_half], dst_ref=landed, send_sem=send_sems.at[k], recv_sem=recv_sems.at[k],
                device_id=(cx, cy, c), device_id_type=MESH).wait_recv()
            cp = pltpu.make_async_remote_copy(
                src_ref=landed, dst_ref=landed, send_sem=send_sems.at[3 + k], recv_sem=recv_sems.at[3 + k],
                device_id=sibling, device_id_type=MESH)
            cp.start()
            passed.append(cp)
        for k, (cx, cy) in enumerate(chips):
            from_sibling = o_ref.at[2 * cx + cy, its_half]
            pltpu.make_async_remote_copy(
                src_ref=from_sibling, dst_ref=from_sibling, send_sem=send_sems.at[3 + k], recv_sem=recv_sems.at[3 + k],
                device_id=sibling, device_id_type=MESH).wait_recv()
        for cp in sends + passed:
            cp.wait_send()
        mine.wait()

    return pl.pallas_call(
        body, name=name,
        out_shape=jax.ShapeDtypeStruct((N_CHIPS,) + pack.shape, pack.dtype),
        in_specs=[ANY], out_specs=ANY,
        scratch_shapes=[pltpu.SemaphoreType.DMA((6,)), pltpu.SemaphoreType.DMA((6,)), pltpu.SemaphoreType.DMA],
    )(pack)


def swap_halves(gpack, *, name):
    half_rows = gpack.shape[1] // 2

    def body(g_ref, r_ref, send_sems, recv_sems):
        x, y, c = _place()
        its_half = pl.ds((1 - c) * half_rows, half_rows)
        copies = [pltpu.make_async_remote_copy(
            src_ref=g_ref.at[s, its_half], dst_ref=r_ref.at[s], send_sem=send_sems.at[s], recv_sem=recv_sems.at[s],
            device_id=(x, y, 1 - c), device_id_type=MESH) for s in range(N_CHIPS)]
        for cp in copies:
            cp.start()
        for cp in copies:
            cp.wait()

    return pl.pallas_call(
        body, name=name,
        out_shape=jax.ShapeDtypeStruct((N_CHIPS, half_rows) + gpack.shape[2:], gpack.dtype),
        in_specs=[ANY], out_specs=ANY,
        scratch_shapes=[pltpu.SemaphoreType.DMA((N_CHIPS,)), pltpu.SemaphoreType.DMA((N_CHIPS,))],
    )(gpack)


def scatter_chips(gpack, *, name):
    def body(g_ref, own_ref, recv_ref, send_sems, recv_sems, local_sem):
        x, y, c = _place()
        chips = _other_chips(x, y)
        mine = pltpu.make_async_copy(g_ref.at[2 * x + y], own_ref, local_sem)
        mine.start()
        sends = [pltpu.make_async_remote_copy(
            src_ref=g_ref.at[2 * cx + cy], dst_ref=recv_ref.at[k], send_sem=send_sems.at[k], recv_sem=recv_sems.at[k],
            device_id=(cx, cy, c), device_id_type=MESH) for k, (cx, cy) in enumerate(chips)]
        for cp in sends:
            cp.start()
        for cp in sends:
            cp.wait_recv()
        for cp in sends:
            cp.wait_send()
        mine.wait()

    slot = jax.ShapeDtypeStruct(gpack.shape[1:], gpack.dtype)
    return pl.pallas_call(
        body, name=name,
        out_shape=[slot, jax.ShapeDtypeStruct((3,) + gpack.shape[1:], gpack.dtype)],
        in_specs=[ANY], out_specs=[ANY, ANY],
        scratch_shapes=[pltpu.SemaphoreType.DMA((3,)), pltpu.SemaphoreType.DMA((3,)), pltpu.SemaphoreType.DMA],
    )(gpack)


class GatherRows:
    def __init__(self, pack, full, lo, hi):
        assert (hi - lo) % 32 == 0 and lo % 16 == 0
        self.src, self.dst, self.lo, self.hi = pack, full, lo, hi

    def sems(self):
        return [pltpu.SemaphoreType.DMA((6,)), pltpu.SemaphoreType.DMA((6,)), pltpu.SemaphoreType.DMA]

    def _parts(self, pack_ref, full_ref, sems):
        send_sems, recv_sems, local_sem = sems
        x, y, c = _place()
        half = (self.hi - self.lo) // 2
        mine, its = pl.ds(self.lo + c * half, half), pl.ds(self.lo + (1 - c) * half, half)
        rows = pl.ds(self.lo, self.hi - self.lo)
        local = pltpu.make_async_copy(pack_ref.at[rows], full_ref.at[2 * x + y, rows], local_sem)
        chips = _other_chips(x, y)

        def remote(src, dst, k, to):
            return pltpu.make_async_remote_copy(src_ref=src, dst_ref=dst, send_sem=send_sems.at[k],
                                                recv_sem=recv_sems.at[k], device_id=to, device_id_type=MESH)

        sends = [remote(pack_ref.at[mine], full_ref.at[2 * x + y, mine], k, (cx, cy, c)) for k, (cx, cy) in enumerate(chips)]
        landed = [full_ref.at[2 * cx + cy, mine] for cx, cy in chips]
        arrive = [remote(pack_ref.at[mine], landed[k], k, (cx, cy, c)) for k, (cx, cy) in enumerate(chips)]
        passed = [remote(landed[k], landed[k], 3 + k, (x, y, 1 - c)) for k in range(3)]
        from_sibling = [remote(landed[k], full_ref.at[2 * cx + cy, its], 3 + k, (x, y, 1 - c))
                        for k, (cx, cy) in enumerate(chips)]
        return local, sends, arrive, passed, from_sibling

    def start(self, pack_ref, full_ref, sems):
        local, sends, _, _, _ = self._parts(pack_ref, full_ref, sems)
        local.start()
        for cp in sends:
            cp.start()

    def finish(self, pack_ref, full_ref, sems):
        local, sends, arrive, passed, from_sibling = self._parts(pack_ref, full_ref, sems)
        for k in range(3):
            arrive[k].wait_recv()
            passed[k].start()
        for cp in from_sibling:
            cp.wait_recv()
        for cp in sends + passed:
            cp.wait_send()
        local.wait()


class ScatterRows:
    def __init__(self, gpack, recv, lo, hi):
        assert lo % 16 == 0 and hi % 16 == 0
        self.src, self.dst, self.lo, self.hi = gpack, recv, lo, hi

    def sems(self):
        return [pltpu.SemaphoreType.DMA((3,)), pltpu.SemaphoreType.DMA((3,))]

    def _parts(self, g_ref, recv_ref, sems):
        send_sems, recv_sems = sems
        x, y, c = _place()
        rows = pl.ds(self.lo, self.hi - self.lo)
        return [pltpu.make_async_remote_copy(
            src_ref=g_ref.at[2 * cx + cy, rows], dst_ref=recv_ref.at[k, rows], send_sem=send_sems.at[k],
            recv_sem=recv_sems.at[k], device_id=(cx, cy, c), device_id_type=MESH)
            for k, (cx, cy) in enumerate(_other_chips(x, y))]

    def start(self, g_ref, recv_ref, sems):
        for cp in self._parts(g_ref, recv_ref, sems):
            cp.start()

    def finish(self, g_ref, recv_ref, sems):
        sends = self._parts(g_ref, recv_ref, sems)
        for cp in sends:
            cp.wait_recv()
        for cp in sends:
            cp.wait_send()


def side_call(side, *, name):
    def body(src_ref, dst_in_ref, dst_ref, *sems):
        side.start(src_ref, dst_ref, sems)
        side.finish(src_ref, dst_ref, sems)

    return pl.pallas_call(
        body, name=name, out_shape=jax.ShapeDtypeStruct(side.dst.shape, side.dst.dtype),
        in_specs=[ANY, ANY], out_specs=ANY, scratch_shapes=side.sems(), input_output_aliases={1: 0},
    )(side.src, side.dst)


def grid_call(body, args, *, name, out_shape, grid, in_specs, out_specs, scratch_shapes, semantics, side=None):
    if side is None:
        res = pl.pallas_call(body, name=name, out_shape=out_shape, grid=grid, in_specs=in_specs, out_specs=out_specs,
                             scratch_shapes=scratch_shapes, compiler_params=_cparams(*semantics))(*args)
        return res, None
    n_in, n_out, n_scr = len(args), len(out_shape), len(scratch_shapes)

    def wrapped(*refs):
        ins, (src_ref, _) = refs[:n_in], refs[n_in:n_in + 2]
        outs, dst_ref = refs[n_in + 2:n_in + 2 + n_out], refs[n_in + 2 + n_out]
        scr, sems = refs[n_in + 3 + n_out:n_in + 3 + n_out + n_scr], refs[n_in + 3 + n_out + n_scr:]
        first = functools.reduce(jnp.logical_and, [pl.program_id(i) == 0 for i in range(len(grid))])
        last = functools.reduce(jnp.logical_and, [pl.program_id(i) == n - 1 for i, n in enumerate(grid)])

        @pl.when(first)
        def _():
            side.start(src_ref, dst_ref, sems)

        body(*ins, *outs, *scr)

        @pl.when(last)
        def _():
            side.finish(src_ref, dst_ref, sems)

    res = pl.pallas_call(
        wrapped, name=name,
        out_shape=list(out_shape) + [jax.ShapeDtypeStruct(side.dst.shape, side.dst.dtype)],
        grid=grid, in_specs=list(in_specs) + [ANY, ANY], out_specs=list(out_specs) + [ANY],
        scratch_shapes=list(scratch_shapes) + side.sems(), input_output_aliases={n_in + 1: n_out},
        compiler_params=_cparams(*(["arbitrary"] * len(grid))),
    )(*args, side.src, side.dst)
    return res[:-1], res[-1]


def sibling_swap(p, *, name):
    def body(p_ref, r_ref, send_sem, recv_sem):
        x, y, c = _place()
        cp = pltpu.make_async_remote_copy(src_ref=p_ref, dst_ref=r_ref, send_sem=send_sem, recv_sem=recv_sem,
                                          device_id=(x, y, 1 - c), device_id_type=MESH)
        cp.start()
        cp.wait()

    return pl.pallas_call(
        body, name=name, out_shape=jax.ShapeDtypeStruct(p.shape, p.dtype),
        in_specs=[ANY], out_specs=ANY,
        scratch_shapes=[pltpu.SemaphoreType.DMA, pltpu.SemaphoreType.DMA],
    )(p)


def sum_slots(own, recv, *, name):
    r_dim, c_dim = own.shape
    tr = _pick(r_dim, 256, mult=16)

    def body(o_ref, r_ref, out_ref):
        acc = o_ref[...].astype(F32)
        for k in range(3):
            acc = acc + r_ref[k].astype(F32)
        out_ref[...] = acc

    return pl.pallas_call(
        body, name=name, out_shape=jax.ShapeDtypeStruct((r_dim, c_dim), F32), grid=(r_dim // tr,),
        in_specs=[pl.BlockSpec((tr, c_dim), lambda i: (i, 0)), pl.BlockSpec((3, tr, c_dim), lambda i: (0, i, 0))],
        out_specs=pl.BlockSpec((tr, c_dim), lambda i: (i, 0)),
        compiler_params=_cparams("parallel"),
    )(own, recv)


def sum_devices(v_all, *, name):
    m_per = v_all.shape[0] // N_DEV

    def body(v_ref, o_ref):
        acc = v_ref[pl.ds(0, m_per), :]
        for d in range(1, N_DEV):
            acc = acc + v_ref[pl.ds(d * m_per, m_per), :]
        o_ref[...] = acc

    return pl.pallas_call(
        body, name=name, out_shape=jax.ShapeDtypeStruct((m_per, LANES), F32),
        in_specs=[WHOLE_VMEM], out_specs=WHOLE_VMEM,
    )(v_all)


WEIGHTS = ['ada_w', 'ada_b', 'norm_mix_g', 'norm_ffn_g', 'hy_w_in', 'hy_conv_w', 'hy_conv_b', 'hy_dt_bias', 'hy_a_log',
           'hy_d_skip', 'hy_ssm_norm_g', 'hy_w_out', 'rel_table', 'cv_w_pw1', 'cv_b_pw1', 'cv_w_dw', 'cv_b_dw', 'cv_ln_g',
           'cv_ln_b', 'cv_w_pw2', 'cv_b_pw2', 'ffn_w_gate', 'ffn_w_up', 'ffn_w_down', 'final_norm_g']
BIG = ('ada_w', 'hy_w_in', 'hy_w_out', 'cv_w_pw1', 'cv_w_pw2', 'ffn_w_gate', 'ffn_w_up', 'ffn_w_down')
SMALL_SHARDED = {'hy_conv_w': (1, 4, 3072), 'cv_b_pw1': (1, 2048), 'cv_w_dw': (1, 31, 1024), 'cv_b_dw': (1, 1024),
                 'cv_ln_g': (1, 1024), 'cv_ln_b': (1, 1024), 'cv_b_pw2': (1, 1024)}
SMALL_GRADS = {'ada_b': (2, 6144), 'norm_mix_g': (2, 1024), 'norm_ffn_g': (2, 1024), 'hy_conv_w': (1, 4, 3072),
               'hy_conv_b': (1, 3072), 'hy_dt_bias': (1, 32), 'hy_a_log': (1, 32), 'hy_d_skip': (1, 32),
               'hy_ssm_norm_g': (1, 2048), 'rel_table': (32, 48), 'cv_b_pw1': (1, 2048), 'cv_w_dw': (1, 31, 1024),
               'cv_b_dw': (1, 1024), 'cv_ln_g': (1, 1024), 'cv_ln_b': (1, 1024), 'cv_b_pw2': (1, 1024),
               'final_norm_g': (1024,), 'loss': (1,)}

PACK_LAYOUT = (('hy_in_t', 2568), ('hy_out', 768), ('pw1_t', 512), ('pw2', 256),
               ('gate_t0', 704), ('up_t0', 704), ('down0', 704), ('gate_t1', 704), ('up_t1', 704), ('down1', 704))
PACK_ROWS = 8448


def _pack_offsets(layout):
    off, out = 0, {}
    for nm, r in layout:
        out[nm] = (off, r)
        off += r
    return out


PACK_OFF = _pack_offsets(PACK_LAYOUT)
W_BATCHES = ((0, 2624), (2624, 5248), (5248, 6336), (6336, 7424), (7424, 8448))
GA_LAYOUT = PACK_LAYOUT[1:]
GA_ROWS = 5888
GA_OFF = _pack_offsets(GA_LAYOUT)
G_BATCHES = ((0, 2560), (2560, 3712), (3712, 4864), (4864, 5888))
GB_LAYOUT = PACK_LAYOUT[:1]
GB_ROWS = 2816


def pack_grads(g, layout, n_rows):
    def rows_bf16(nm):
        return g[nm]

    parts = []
    for key, r in layout:
        if key == 'hy_in_t':
            a = hy_from_cat(rows_bf16('hy_in_t'))
        elif key.startswith('gate_t'):
            a = rows_bf16('gu_t' + key[-1])[:FFN_HIDDEN]
        elif key.startswith('up_t'):
            a = rows_bf16('gu_t' + key[-1])[FFN_HIDDEN:]
        else:
            a = rows_bf16(key)
        parts.append(a.reshape(N_CHIPS, r, D))
    used = sum(r for _, r in layout)
    return jnp.concatenate(parts + [jnp.zeros((N_CHIPS, n_rows - used, D), BF16)], axis=1)


def unpack_weights(full, skip=()):
    def whole(nm):
        o, r = PACK_OFF[nm]
        return full[:, o:o + r].reshape(N_CHIPS * r, D)

    out = {"hy_out": whole('hy_out'), "pw1_t": whole('pw1_t'), "pw2": whole('pw2'),
           "gu_t": [jnp.concatenate([whole(f'gate_t{i}'), whole(f'up_t{i}')], axis=0) for i in range(2)],
           "down": [whole(f'down{i}') for i in range(2)]}
    if "hy_in_t" not in skip:
        out["hy_in_t"] = hy_to_cat(whole('hy_in_t'))
    return out


def _to_lanes(flat):
    n = flat.shape[0]
    m = -(-n // (8 * LANES)) * 8
    return jnp.pad(flat, (0, m * LANES - n)).reshape(m, LANES)


def _split(flat, shapes):
    out, off = {}, 0
    for nm, shp in shapes.items():
        n = int(np.prod(shp))
        out[nm] = flat[off:off + n].reshape(shp)
        off += n
    return out


def kernel(x, c, ada_w, ada_b, norm_mix_g, norm_ffn_g, hy_w_in, hy_conv_w, hy_conv_b, hy_dt_bias, hy_a_log, hy_d_skip, hy_ssm_norm_g, hy_w_out, rel_table, cv_w_pw1, cv_b_pw1, cv_w_dw, cv_b_dw, cv_ln_g, cv_ln_b, cv_w_pw2, cv_b_pw2, ffn_w_gate, ffn_w_up, ffn_w_down, final_norm_g, loss_target, m_ada_w, m_ada_b, m_norm_mix_g, m_norm_ffn_g, m_hy_w_in, m_hy_conv_w, m_hy_conv_b, m_hy_dt_bias, m_hy_a_log, m_hy_d_skip, m_hy_ssm_norm_g, m_hy_w_out, m_rel_table, m_cv_w_pw1, m_cv_b_pw1, m_cv_w_dw, m_cv_b_dw, m_cv_ln_g, m_cv_ln_b, m_cv_w_pw2, m_cv_b_pw2, m_ffn_w_gate, m_ffn_w_up, m_ffn_w_down, m_final_norm_g, v_ada_w, v_ada_b, v_norm_mix_g, v_norm_ffn_g, v_hy_w_in, v_hy_conv_w, v_hy_conv_b, v_hy_dt_bias, v_hy_a_log, v_hy_d_skip, v_hy_ssm_norm_g, v_hy_w_out, v_rel_table, v_cv_w_pw1, v_cv_b_pw1, v_cv_w_dw, v_cv_b_dw, v_cv_ln_g, v_cv_ln_b, v_cv_w_pw2, v_cv_b_pw2, v_ffn_w_gate, v_ffn_w_up, v_ffn_w_down, v_final_norm_g):
    args = (x, c, ada_w, ada_b, norm_mix_g, norm_ffn_g, hy_w_in, hy_conv_w, hy_conv_b, hy_dt_bias, hy_a_log, hy_d_skip, hy_ssm_norm_g, hy_w_out, rel_table, cv_w_pw1, cv_b_pw1, cv_w_dw, cv_b_dw, cv_ln_g, cv_ln_b, cv_w_pw2, cv_b_pw2, ffn_w_gate, ffn_w_up, ffn_w_down, final_norm_g, loss_target, m_ada_w, m_ada_b, m_norm_mix_g, m_norm_ffn_g, m_hy_w_in, m_hy_conv_w, m_hy_conv_b, m_hy_dt_bias, m_hy_a_log, m_hy_d_skip, m_hy_ssm_norm_g, m_hy_w_out, m_rel_table, m_cv_w_pw1, m_cv_b_pw1, m_cv_w_dw, m_cv_b_dw, m_cv_ln_g, m_cv_ln_b, m_cv_w_pw2, m_cv_b_pw2, m_ffn_w_gate, m_ffn_w_up, m_ffn_w_down, m_final_norm_g, v_ada_w, v_ada_b, v_norm_mix_g, v_norm_ffn_g, v_hy_w_in, v_hy_conv_w, v_hy_conv_b, v_hy_dt_bias, v_hy_a_log, v_hy_d_skip, v_hy_ssm_norm_g, v_hy_w_out, v_rel_table, v_cv_w_pw1, v_cv_b_pw1, v_cv_w_dw, v_cv_b_dw, v_cv_ln_g, v_cv_ln_b, v_cv_w_pw2, v_cv_b_pw2, v_ffn_w_gate, v_ffn_w_up, v_ffn_w_down, v_final_norm_g)
    x_in, c_in = args[0], args[1]
    w = dict(zip(WEIGHTS, args[2:27], strict=True))
    tgt = args[27]
    m_in = dict(zip(WEIGHTS, args[28:53], strict=True))
    v_in = dict(zip(WEIGHTS, args[53:78], strict=True))
    xi, yi, ci = _place()
    chip = 2 * xi + yi
    dev = 2 * chip + ci

    cs = rowmap(f_silu, [c_in.reshape(8, LANES)], [], [F32], name="cond_silu", tr=8)[0]
    cs_all = allgather_small(cs, name="gather_cond").reshape(N_DEV, D)
    cs16 = jnp.pad(cs_all, ((0, 8), (0, 0)))
    modpart = jnp.stack([matmul(cs16, w['ada_w'][i], mode="nn", out_dtype=F32, name=f"ada_fwd{i}")[:N_DEV]
                         for i in range(2)], axis=1)
    shard_names = list(SMALL_SHARDED)
    payload = jnp.concatenate([modpart.reshape(-1)] + [w[nm].reshape(-1) for nm in shard_names])
    got = allgather_small(_to_lanes(payload), name="gather_mod").reshape(N_DEV, -1)[0::2]
    modparts = got[:, :modpart.size].reshape(N_CHIPS, N_DEV, 2, 1536)
    mine = lax.dynamic_index_in_dim(modparts, dev, axis=1, keepdims=False)
    mod = jnp.transpose(mine, (1, 0, 2)).reshape(2, 6 * D) + w['ada_b']
    mods = [[mod[i, j * D:(j + 1) * D].reshape(1, D) for j in range(6)] for i in range(2)]
    sp = {}
    off = modpart.size
    for nm in shard_names:
        shp = w[nm].shape
        n = int(np.prod(shp))
        parts = got[:, off:off + n].reshape((N_CHIPS,) + shp)
        sp[nm] = jnp.concatenate([parts[s] for s in range(N_CHIPS)], axis=-1)
        off += n

    def rows_of(nm, i=None):
        a = w[nm][0 if i is None else i]
        return (a.T if nm in ('hy_w_in', 'cv_w_pw1', 'ffn_w_gate', 'ffn_w_up') else a).astype(BF16)

    pieces = [rows_of('hy_w_in'), rows_of('hy_w_out'), rows_of('cv_w_pw1'), rows_of('cv_w_pw2')]
    for i in range(2):
        pieces += [rows_of('ffn_w_gate', i), rows_of('ffn_w_up', i), rows_of('ffn_w_down', i)]
    n_rows = sum(p.shape[0] for p in pieces)
    pack = jnp.concatenate(pieces + [jnp.zeros((PACK_ROWS - n_rows, D), BF16)], axis=0)
    full = side_call(GatherRows(pack, lax.empty((N_CHIPS, PACK_ROWS, D), BF16), *W_BATCHES[0]), name="gather_weights")
    o_in, r_in = PACK_OFF['hy_in_t']
    wts = {"hy_in_t": hy_to_cat(full[:, o_in:o_in + r_in].reshape(N_CHIPS * r_in, D))}
    comm = {"pack": pack, "full": full}

    sp = {"norm_mix_g": [w['norm_mix_g'][i].reshape(1, D) for i in range(2)],
          "norm_ffn_g": [w['norm_ffn_g'][i].reshape(1, D) for i in range(2)],
          "hy_conv_w": sp['hy_conv_w'][0], "hy_conv_b": w['hy_conv_b'],
          "hy_dt_bias": w['hy_dt_bias'].reshape(SSM_HEADS, 1), "hy_a_log": w['hy_a_log'].reshape(SSM_HEADS, 1),
          "hy_d_skip": w['hy_d_skip'].reshape(SSM_HEADS, 1), "hy_ssm_norm_g": w['hy_ssm_norm_g'],
          "rel_table": w['rel_table'], "cv_b_pw1": sp['cv_b_pw1'], "cv_w_dw": sp['cv_w_dw'][0], "cv_b_dw": sp['cv_b_dw'],
          "cv_ln_g": sp['cv_ln_g'], "cv_ln_b": sp['cv_ln_b'], "cv_b_pw2": sp['cv_b_pw2'],
          "final_norm_g": w['final_norm_g'].reshape(1, D)}

    loss_rows, grad_x, g, dmods = device_step(x_in[0], tgt[0], mods, wts, sp, comm)

    dmod = jnp.stack([jnp.concatenate([d.reshape(-1) for d in dmods[i]]) for i in range(2)])
    small = {'ada_b': dmod, 'norm_mix_g': jnp.stack([g[f'norm_mix_g{i}'].reshape(-1) for i in range(2)]),
             'norm_ffn_g': jnp.stack([g[f'norm_ffn_g{i}'].reshape(-1) for i in range(2)]),
             'loss': jnp.sum(loss_rows).reshape(1)}
    for nm in SMALL_GRADS:
        if nm not in small:
            small[nm] = g[nm]
    vec = _to_lanes(jnp.concatenate([small[nm].reshape(-1) for nm in SMALL_GRADS]))
    vec_all = allgather_small(vec, name="gather_small_grads")
    tot = _split(sum_devices(vec_all, name="sum_small_grads").reshape(-1), SMALL_GRADS)
    dmod_all = vec_all.reshape(N_DEV, -1)[:, :2 * 6 * D].reshape(N_DEV, 2, 6 * D)

    recv = comm["recv"]
    own_a = lax.dynamic_index_in_dim(comm["ga"], chip, axis=0, keepdims=False)
    part_a = sum_slots(own_a, recv, name="sum_chip_grads")
    red_a = rowmap(f_add, [part_a, sibling_swap(part_a, name="swap_grads")], [], [F32], name="sum_core_grads")[0]
    recv_b = comm["recv_b"]
    own_b = lax.dynamic_index_in_dim(comm["gb"], chip, axis=0, keepdims=False)
    mine_half = sum_slots(own_b, recv_b, name="sum_in_chips")
    its_half = sibling_swap(mine_half, name="swap_in")
    red_b = jnp.concatenate([jnp.where(ci == 0, mine_half, its_half), jnp.where(ci == 0, its_half, mine_half)], axis=0)

    def shard_grad(nm, i=None):
        key = {'hy_w_in': 'hy_in_t', 'hy_w_out': 'hy_out', 'cv_w_pw1': 'pw1_t', 'cv_w_pw2': 'pw2'}.get(nm)
        if key is None:
            key = {'ffn_w_gate': 'gate_t', 'ffn_w_up': 'up_t', 'ffn_w_down': 'down'}[nm] + str(i)
        if key == 'hy_in_t':
            a = red_b[:PACK_OFF[key][1]]
        else:
            o, r = GA_OFF[key]
            a = red_a[o:o + r]
        return a.T if key.endswith('_t') or key[:-1].endswith('_t') else a

    grads = {}
    grads['hy_w_in'] = shard_grad('hy_w_in')[None]
    grads['hy_w_out'] = shard_grad('hy_w_out')[None]
    grads['cv_w_pw1'] = shard_grad('cv_w_pw1')[None]
    grads['cv_w_pw2'] = shard_grad('cv_w_pw2')[None]
    for nm in ('ffn_w_gate', 'ffn_w_up', 'ffn_w_down'):
        grads[nm] = jnp.stack([shard_grad(nm, i) for i in range(2)])
    cs16 = jnp.pad(cs_all, ((0, 8), (0, 0)))
    dm_mine = lax.dynamic_slice_in_dim(dmod_all, chip * 1536, 1536, axis=2)
    dm16 = jnp.pad(dm_mine, ((0, 8), (0, 0), (0, 0)))
    grads['ada_w'] = jnp.stack([matmul(cs16, dm16[:, i], mode="tn", out_dtype=F32, name=f"ada_dw{i}") for i in range(2)])
    for nm, shp in SMALL_GRADS.items():
        if nm == 'loss':
            continue
        if nm in SMALL_SHARDED:
            n = w[nm].shape[-1]
            grads[nm] = lax.dynamic_slice_in_dim(tot[nm], chip * n, n, axis=len(shp) - 1)
        else:
            grads[nm] = tot[nm].reshape(w[nm].shape)

    delta, new_m, new_v = {}, {}, {}
    for nm in BIG:
        delta[nm], new_m[nm], new_v[nm] = adamw(w[nm], grads[nm], m_in[nm], v_in[nm], name="adamw_" + nm)
    smalls = [nm for nm in WEIGHTS if nm not in BIG]
    packed = [_to_lanes(jnp.concatenate([d[nm].reshape(-1) for nm in smalls])) for d in (w, grads, m_in, v_in)]
    res = rowmap(f_adamw, packed, [], [F32] * 3, name="adamw_small", tr=_rows_tile(packed[0].shape[0]))
    for d, r in zip((delta, new_m, new_v), res, strict=True):
        d.update(_split(r.reshape(-1), {nm: w[nm].shape for nm in smalls}))

    loss = tot['loss'].reshape(())
    return (loss, grad_x[None], *[grads[nm] for nm in WEIGHTS], *[delta[nm] for nm in WEIGHTS],
            *[new_m[nm] for nm in WEIGHTS], *[new_v[nm] for nm in WEIGHTS])
```

```python
import functools
import math

import jax
import jax.numpy as jnp
import numpy as np
from jax import lax
from jax.experimental import pallas as pl
from jax.experimental.pallas import tpu as pltpu

F32 = jnp.float32
BF16 = jnp.bfloat16
MESH = pl.DeviceIdType.MESH

D = 1024
S = 4096
EPS = 1e-6
SSM_INNER = 2048
SSM_HEADS = 32
SSM_HDIM = 64
SSM_GROUPS = 4
SSM_STATE = 128
SSM_CONVK = 4
SSM_CONV_DIM = 3072
CHUNK = 128
N_CHUNKS = S // CHUNK
ATT_HEADS = 16
ATT_HDIM = 64
ATT_PATTERNS = ((128, 1), (512, 4), (2048, 16))
ATT_BLK = 128
REL_BUCKETS = 32
REL_MAX_DIST = 2048
CONV_WIDTH = 31
FFN_HIDDEN = 2816
N_CHIPS = 4
N_DEV = 8
ADAM_LR, ADAM_B1, ADAM_B2, ADAM_EPS, ADAM_WD, ADAM_STEP = 0.001, 0.9, 0.999, 1e-08, 0.01, 10

VMEM_LIMIT_BYTES = 56 * 1024 * 1024
LANES = 128


def _cparams(*sem):
    return pltpu.CompilerParams(dimension_semantics=sem, vmem_limit_bytes=VMEM_LIMIT_BYTES)


def _pick(n, cap, mult=LANES):
    best = None
    for t in range(mult, min(n, cap) + 1, mult):
        if n % t == 0:
            best = t
    return best or n


def _dot(a, b, ca, cb):
    return lax.dot_general(a.astype(BF16), b.astype(BF16), (((ca,), (cb,)), ((), ())), preferred_element_type=F32)


@jax.custom_vjp
def mm(a, b):
    return _dot(a, b, 1, 0)


def _mm_fwd(a, b):
    return _dot(a, b, 1, 0), (a, b)


def _mm_bwd(res, g):
    a, b = res
    return _dot(g, b, 1, 1).astype(a.dtype), _dot(a, g, 0, 0).astype(b.dtype)


mm.defvjp(_mm_fwd, _mm_bwd)


@jax.custom_vjp
def mm_nt(a, b):
    return _dot(a, b, 1, 1)


def _mm_nt_fwd(a, b):
    return _dot(a, b, 1, 1), (a, b)


def _mm_nt_bwd(res, g):
    a, b = res
    return _dot(g, b, 1, 0).astype(a.dtype), _dot(g, a, 0, 0).astype(b.dtype)


mm_nt.defvjp(_mm_nt_fwd, _mm_nt_bwd)


@jax.custom_vjp
def mm_tn(a, b):
    return _dot(a, b, 0, 0)


def _mm_tn_fwd(a, b):
    return _dot(a, b, 0, 0), (a, b)


def _mm_tn_bwd(res, g):
    a, b = res
    return _dot(b, g, 1, 1).astype(a.dtype), _dot(a, g, 1, 0).astype(b.dtype)


mm_tn.defvjp(_mm_tn_fwd, _mm_tn_bwd)


def matmul(a, b, *, mode, out_dtype, name, n=None, b_off=0, tm_cap=1024, tn_cap=512, tk_cap=1536, side=None,
           out_t=False):
    if mode == "tn":
        k_dim, m_dim = a.shape
    else:
        m_dim, k_dim = a.shape
    n_dim = n if n is not None else (b.shape[0] if mode == "nt" else b.shape[1])
    tm = m_dim if m_dim < LANES else _pick(m_dim, tm_cap)
    tn = _pick(n_dim, tn_cap)
    tk = k_dim if k_dim < LANES else _pick(k_dim, tk_cap)
    assert m_dim % tm == 0 and n_dim % tn == 0 and k_dim % tk == 0 and b_off % tn == 0
    nk = k_dim // tk
    off = b_off // tn
    if mode == "nn":
        a_spec = pl.BlockSpec((tm, tk), lambda i, j, k: (i, k))
        b_spec = pl.BlockSpec((tk, tn), lambda i, j, k: (k, j))
        ca, cb = 1, 0
    elif mode == "nt":
        a_spec = pl.BlockSpec((tm, tk), lambda i, j, k: (i, k))
        b_spec = pl.BlockSpec((tn, tk), lambda i, j, k: (j + off, k))
        ca, cb = 1, 1
    else:
        a_spec = pl.BlockSpec((tk, tm), lambda i, j, k: (k, i))
        b_spec = pl.BlockSpec((tk, tn), lambda i, j, k: (k, j))
        ca, cb = 0, 0

    def emit(o_ref, val):
        o_ref[...] = (val.T if out_t else val).astype(o_ref.dtype)

    def body(a_ref, b_ref, o_ref, acc_ref):
        part = _dot(a_ref[...], b_ref[...], ca, cb)
        if nk == 1:
            emit(o_ref, part)
        else:
            k = pl.program_id(2)

            @pl.when(k == 0)
            def _():
                acc_ref[...] = part

            @pl.when(k > 0)
            def _():
                acc_ref[...] += part

            @pl.when(k == nk - 1)
            def _():
                emit(o_ref, acc_ref[...])

    if out_t:
        out_shape, out_spec = (n_dim, m_dim), pl.BlockSpec((tn, tm), lambda i, j, k: (j, i))
    else:
        out_shape, out_spec = (m_dim, n_dim), pl.BlockSpec((tm, tn), lambda i, j, k: (i, j))
    (out,), side_dst = grid_call(
        body, (a, b), name=name,
        out_shape=[jax.ShapeDtypeStruct(out_shape, out_dtype)],
        grid=(m_dim // tm, n_dim // tn, nk),
        in_specs=[a_spec, b_spec],
        out_specs=[out_spec],
        scratch_shapes=[pltpu.VMEM((tm, tn), F32)],
        semantics=("parallel", "parallel", "arbitrary"), side=side)
    return out if side is None else (out, side_dst)


def _f32(xs):
    return [x.astype(F32) for x in xs]


def rowmap(f, rows, consts, out_dtypes, *, name, tr=256):
    r_dim = rows[0].shape[0]
    tr = _pick(r_dim, tr, mult=8)
    assert r_dim % tr == 0
    nr, nc = len(rows), len(consts)
    outs = jax.eval_shape(lambda *xs: f(*xs), *[jax.ShapeDtypeStruct((tr, x.shape[1]), F32) for x in rows],
                          *[jax.ShapeDtypeStruct(x.shape, F32) for x in consts])

    def body(*refs):
        res = f(*_f32([r[...] for r in refs[:nr + nc]]))
        for o_ref, o in zip(refs[nr + nc:], res, strict=True):
            o_ref[...] = o.astype(o_ref.dtype)

    return pl.pallas_call(
        body, name=name,
        out_shape=[jax.ShapeDtypeStruct((r_dim, o.shape[1]), dt) for o, dt in zip(outs, out_dtypes, strict=True)],
        grid=(r_dim // tr,),
        in_specs=[pl.BlockSpec((tr, x.shape[1]), lambda i: (i, 0)) for x in rows]
        + [pl.BlockSpec(x.shape, lambda i: (0, 0)) for x in consts],
        out_specs=[pl.BlockSpec((tr, o.shape[1]), lambda i: (i, 0)) for o in outs],
        compiler_params=_cparams("parallel"),
    )(*rows, *consts)


def rowmap_bwd(f, rows, consts, cts, *, name, row_grad, row_dtypes=None, tr=256, emit=(), row_add=None):
    r_dim = rows[0].shape[0]
    tr = _pick(r_dim, tr, mult=8)
    assert r_dim % tr == 0
    nr, nc, nct = len(rows), len(consts), len(cts)
    gi = [i for i, flag in enumerate(row_grad) if flag]
    row_dtypes = row_dtypes or [F32] * len(gi)
    row_add = row_add or [None] * len(gi)
    adds = [a for a in row_add if a is not None]
    outs = jax.eval_shape(lambda *xs: f(*xs), *[jax.ShapeDtypeStruct((tr, x.shape[1]), F32) for x in rows],
                          *[jax.ShapeDtypeStruct(x.shape, F32) for x in consts])

    def body(*refs):
        ins = _f32([r[...] for r in refs[:nr + nc]])
        ct = _f32([r[...] for r in refs[nr + nc:nr + nc + nct]])
        add_refs = list(refs[nr + nc + nct:nr + nc + nct + len(adds)])
        o_refs = refs[nr + nc + nct + len(adds):]
        res, vjp = jax.vjp(f, *ins)
        grads = vjp(tuple(ct))
        for o_ref, i, a in zip(o_refs[:len(gi)], gi, row_add):
            g = grads[i] if a is None else grads[i] + add_refs.pop(0)[...].astype(F32)
            o_ref[...] = g.astype(o_ref.dtype)
        first = pl.program_id(0) == 0
        for o_ref, g in zip(o_refs[len(gi):len(gi) + nc], grads[nr:]):
            @pl.when(first)
            def _(o_ref=o_ref, g=g):
                o_ref[...] = g

            @pl.when(jnp.logical_not(first))
            def _(o_ref=o_ref, g=g):
                o_ref[...] += g
        for o_ref, i in zip(o_refs[len(gi) + nc:], emit):
            o_ref[...] = res[i].astype(o_ref.dtype)

    out_shape = ([jax.ShapeDtypeStruct(rows[i].shape, dt) for i, dt in zip(gi, row_dtypes, strict=True)]
                 + [jax.ShapeDtypeStruct(x.shape, F32) for x in consts]
                 + [jax.ShapeDtypeStruct((r_dim, outs[i].shape[1]), F32) for i in emit])
    out_specs = ([pl.BlockSpec((tr, rows[i].shape[1]), lambda i_: (i_, 0)) for i in gi]
                 + [pl.BlockSpec(x.shape, lambda i_: (0, 0)) for x in consts]
                 + [pl.BlockSpec((tr, outs[i].shape[1]), lambda i_: (i_, 0)) for i in emit])
    res = pl.pallas_call(
        body, name=name,
        out_shape=out_shape,
        grid=(r_dim // tr,),
        in_specs=[pl.BlockSpec((tr, x.shape[1]), lambda i: (i, 0)) for x in rows]
        + [pl.BlockSpec(x.shape, lambda i: (0, 0)) for x in consts]
        + [pl.BlockSpec((tr, x.shape[1]), lambda i: (i, 0)) for x in list(cts) + adds],
        out_specs=out_specs,
        compiler_params=_cparams("arbitrary"),
    )(*rows, *consts, *cts, *adds)
    return res[:len(gi)], res[len(gi):len(gi) + nc], res[len(gi) + nc:]


def transpose(a, *, name, out_dtype=BF16, tr=512, tc=512):
    r_dim, c_dim = a.shape
    tr, tc = _pick(r_dim, tr), _pick(c_dim, tc)

    def body(a_ref, o_ref):
        o_ref[...] = a_ref[...].astype(F32).T.astype(o_ref.dtype)

    return pl.pallas_call(
        body, name=name, out_shape=jax.ShapeDtypeStruct((c_dim, r_dim), out_dtype),
        grid=(r_dim // tr, c_dim // tc),
        in_specs=[pl.BlockSpec((tr, tc), lambda i, j: (i, j))],
        out_specs=pl.BlockSpec((tc, tr), lambda i, j: (j, i)),
        compiler_params=_cparams("parallel", "parallel"),
    )(a)


CONV_HALO = 32
CONV_ROWS = 256


def conv_fwd(x, w, b, *, name, cb=256, chunk_rows=CONV_ROWS):
    s_dim, c_dim = x.shape
    taps = w.shape[0]
    assert taps - 1 <= CONV_HALO and s_dim % chunk_rows == 0 and c_dim % cb == 0
    n_chunks = s_dim // chunk_rows
    ext = chunk_rows + CONV_HALO

    def body(x_ref, w_ref, b_ref, o_ref, xp_ref):
        xp_ref[pl.ds(0, CONV_HALO), :] = jnp.zeros((CONV_HALO, cb), F32)
        xp_ref[pl.ds(CONV_HALO, s_dim), :] = x_ref[...].astype(F32)
        wv = w_ref[...].astype(F32)
        bv = b_ref[...].astype(F32)

        def chunk(t, carry):
            base = pl.multiple_of(t * chunk_rows, chunk_rows)
            xe = xp_ref[pl.ds(base, ext), :]
            acc = jnp.broadcast_to(bv, (chunk_rows, cb))
            for j in range(taps):
                sh = xe if j == 0 else pltpu.roll(xe, shift=j, axis=0)
                acc = acc + wv[taps - 1 - j:taps - j, :] * sh[CONV_HALO:, :]
            o_ref[pl.ds(base, chunk_rows), :] = acc
            return carry

        lax.fori_loop(0, n_chunks, chunk, 0)

    return pl.pallas_call(
        body, name=name,
        out_shape=jax.ShapeDtypeStruct((s_dim, c_dim), F32),
        grid=(c_dim // cb,),
        in_specs=[pl.BlockSpec((s_dim, cb), lambda i: (0, i)), pl.BlockSpec((taps, cb), lambda i: (0, i)),
                  pl.BlockSpec((1, cb), lambda i: (0, i))],
        out_specs=pl.BlockSpec((s_dim, cb), lambda i: (0, i)),
        scratch_shapes=[pltpu.VMEM((s_dim + CONV_HALO, cb), F32)],
        compiler_params=_cparams("parallel"),
    )(x, w, b)


def conv_bwd(x, w, g, *, name, cb=256, chunk_rows=CONV_ROWS, dx_dtype=F32):
    s_dim, c_dim = x.shape
    taps = w.shape[0]
    n_chunks = s_dim // chunk_rows
    ext = chunk_rows + CONV_HALO

    def rows8(a):
        return jnp.sum(a.reshape(chunk_rows // 8, 8, cb), axis=0)

    def body(x_ref, w_ref, g_ref, dx_ref, dw_ref, db_ref, xp_ref, gp_ref, acc_ref):
        xp_ref[pl.ds(0, CONV_HALO), :] = jnp.zeros((CONV_HALO, cb), F32)
        xp_ref[pl.ds(CONV_HALO, s_dim), :] = x_ref[...].astype(F32)
        gp_ref[pl.ds(0, s_dim), :] = g_ref[...].astype(F32)
        gp_ref[pl.ds(s_dim, CONV_HALO), :] = jnp.zeros((CONV_HALO, cb), F32)
        acc_ref[...] = jnp.zeros_like(acc_ref)
        wv = w_ref[...].astype(F32)

        def chunk(t, carry):
            base = pl.multiple_of(t * chunk_rows, chunk_rows)
            xe = xp_ref[pl.ds(base, ext), :]
            ge = gp_ref[pl.ds(base, ext), :]
            gc = ge[:chunk_rows, :]
            dx = jnp.zeros((chunk_rows, cb), F32)
            for j in range(taps):
                xs = xe if j == 0 else pltpu.roll(xe, shift=j, axis=0)
                gs = ge if j == 0 else pltpu.roll(ge, shift=ext - j, axis=0)
                k = taps - 1 - j
                dx = dx + wv[k:k + 1, :] * gs[:chunk_rows, :]
                acc_ref[8 * k:8 * k + 8, :] += rows8(gc * xs[CONV_HALO:, :])
            acc_ref[8 * taps:8 * taps + 8, :] += rows8(gc)
            dx_ref[pl.ds(base, chunk_rows), :] = dx.astype(dx_ref.dtype)
            return carry

        lax.fori_loop(0, n_chunks, chunk, 0)
        sums = jnp.sum(acc_ref[...].reshape(taps + 1, 8, cb), axis=1)
        dw_ref[...] = sums[0:taps, :]
        db_ref[...] = sums[taps:taps + 1, :]

    return pl.pallas_call(
        body, name=name,
        out_shape=[jax.ShapeDtypeStruct((s_dim, c_dim), dx_dtype), jax.ShapeDtypeStruct((taps, c_dim), F32),
                   jax.ShapeDtypeStruct((1, c_dim), F32)],
        grid=(c_dim // cb,),
        in_specs=[pl.BlockSpec((s_dim, cb), lambda i: (0, i)), pl.BlockSpec((taps, cb), lambda i: (0, i)),
                  pl.BlockSpec((s_dim, cb), lambda i: (0, i))],
        out_specs=[pl.BlockSpec((s_dim, cb), lambda i: (0, i)), pl.BlockSpec((taps, cb), lambda i: (0, i)),
                   pl.BlockSpec((1, cb), lambda i: (0, i))],
        scratch_shapes=[pltpu.VMEM((s_dim + CONV_HALO, cb), F32), pltpu.VMEM((s_dim + CONV_HALO, cb), F32),
                        pltpu.VMEM((8 * (taps + 1), cb), F32)],
        compiler_params=_cparams("parallel"),
    )(x, w, g)


def _iota2(n, axis):
    return lax.broadcasted_iota(jnp.int32, (n, n), axis)


def _to_col(row):
    n = row.shape[1]
    return jnp.sum(jnp.where(_iota2(n, 0) == _iota2(n, 1), jnp.broadcast_to(row, (n, n)), 0.0), axis=1, keepdims=True)


def _softplus(x):
    return jnp.maximum(x, 0.0) + jnp.log(1.0 + jnp.exp(-jnp.abs(x)))


def ssd_heads(x, dtraw, dt_bias, a_log, dskip, bm, cm, prev):
    h, q, _ = x.shape
    n = bm.shape[1]
    li = lax.broadcasted_iota(jnp.int32, (1, q, q), 1)
    si = lax.broadcasted_iota(jnp.int32, (1, q, q), 2)

    def to_col(row):
        return jnp.sum(jnp.where(li == si, jnp.broadcast_to(row, (h, q, q)), 0.0), axis=2, keepdims=True)

    dt_row = _softplus(dtraw + dt_bias)
    a_row = dt_row * (-jnp.exp(a_log))
    a_col = to_col(a_row)
    acs_col = jnp.sum(jnp.where(si <= li, jnp.broadcast_to(a_row, (h, q, q)), 0.0), axis=2, keepdims=True)
    acs_row = jnp.sum(jnp.where(li <= si, jnp.broadcast_to(a_col, (h, q, q)), 0.0), axis=1, keepdims=True)
    total = jnp.sum(a_row, axis=2, keepdims=True)
    xdt = x * to_col(dt_row)
    lmat = jnp.exp(jnp.where(li >= si, acs_col - acs_row, -1e30))
    bmb = jnp.broadcast_to(bm[None], (h, q, n))
    cmb = jnp.broadcast_to(cm[None], (h, q, n))
    y = bmm(mm_nt(cm, bm)[None] * lmat, xdt)
    y = y + bmm_nt(cmb, prev) * jnp.exp(acs_col)
    y = y + dskip * x
    state = bmm_tn(xdt * jnp.exp(total - acs_col), bmb)
    return y, jnp.exp(total) * prev + state


HEADS_PER_GROUP = SSM_HEADS // SSM_GROUPS
BM_COL0 = SSM_INNER // SSM_STATE
CM_COL0 = BM_COL0 + SSM_GROUPS


def ssd_fwd(xs_hm, dtraw_t, dt_bias, a_log, dskip, xbc, side=None):
    hg = HEADS_PER_GROUP

    def body(x_ref, dt_ref, dtb_ref, al_ref, dk_ref, bm_ref, cm_ref, y_ref, prev_ref, state_ref):
        @pl.when(pl.program_id(1) == 0)
        def _():
            state_ref[...] = jnp.zeros_like(state_ref)

        prev = state_ref[...]
        prev_ref[0] = prev
        y, nxt = ssd_heads(x_ref[...], dt_ref[...], dtb_ref[...], al_ref[...], dk_ref[...], bm_ref[...], cm_ref[...], prev)
        y_ref[...] = y
        state_ref[...] = nxt

    hp = pl.BlockSpec((hg, 1, 1), lambda g, c: (g, 0, 0))
    dtraw_t, dt_bias, a_log, dskip = [a.reshape(SSM_HEADS, 1, -1) for a in (dtraw_t, dt_bias, a_log, dskip)]
    return grid_call(
        body, (xs_hm, dtraw_t, dt_bias, a_log, dskip, xbc, xbc), name="ssd_fwd",
        out_shape=[jax.ShapeDtypeStruct((SSM_HEADS, S, SSM_HDIM), F32),
                   jax.ShapeDtypeStruct((N_CHUNKS, SSM_HEADS, SSM_HDIM, SSM_STATE), F32)],
        grid=(SSM_GROUPS, N_CHUNKS),
        in_specs=[pl.BlockSpec((hg, CHUNK, SSM_HDIM), lambda g, c: (g, c, 0)),
                  pl.BlockSpec((hg, 1, CHUNK), lambda g, c: (g, 0, c)), hp, hp, hp,
                  pl.BlockSpec((CHUNK, SSM_STATE), lambda g, c: (c, BM_COL0 + g)),
                  pl.BlockSpec((CHUNK, SSM_STATE), lambda g, c: (c, CM_COL0 + g))],
        out_specs=[pl.BlockSpec((hg, CHUNK, SSM_HDIM), lambda g, c: (g, c, 0)),
                   pl.BlockSpec((1, hg, SSM_HDIM, SSM_STATE), lambda g, c: (c, g, 0, 0))],
        scratch_shapes=[pltpu.VMEM((hg, SSM_HDIM, SSM_STATE), F32)],
        semantics=("parallel", "arbitrary"), side=side)


def ssd_bwd(xs_hm, dtraw_t, dt_bias, a_log, dskip, xbc, prev_all, dy_hm, side=None):
    hg = HEADS_PER_GROUP
    last = N_CHUNKS - 1

    def body(x_ref, dt_ref, dtb_ref, al_ref, dk_ref, bm_ref, cm_ref, prev_ref, dy_ref,
             dx_ref, ddt_ref, ddtb_ref, dal_ref, ddk_ref, dbm_ref, dcm_ref, dstate_ref):
        @pl.when(pl.program_id(1) == 0)
        def _():
            dstate_ref[...] = jnp.zeros_like(dstate_ref)
            ddtb_ref[...] = jnp.zeros_like(ddtb_ref)
            dal_ref[...] = jnp.zeros_like(dal_ref)
            ddk_ref[...] = jnp.zeros_like(ddk_ref)

        _, vjp = jax.vjp(ssd_heads, x_ref[...], dt_ref[...], dtb_ref[...], al_ref[...], dk_ref[...], bm_ref[...],
                         cm_ref[...], prev_ref[0])
        dx, ddt, ddtb, dal, ddk, dbm, dcm, dprev = vjp((dy_ref[...], dstate_ref[...]))
        dx_ref[...] = dx
        ddt_ref[...] = ddt
        ddtb_ref[...] += ddtb
        dal_ref[...] += dal
        ddk_ref[...] += ddk
        dbm_ref[...] = dbm
        dcm_ref[...] = dcm
        dstate_ref[...] = dprev

    hp = pl.BlockSpec((hg, 1, 1), lambda g, c: (g, 0, 0))
    xspec = pl.BlockSpec((hg, CHUNK, SSM_HDIM), lambda g, c: (g, last - c, 0))
    tspec = pl.BlockSpec((hg, 1, CHUNK), lambda g, c: (g, 0, last - c))
    gspec = pl.BlockSpec((CHUNK, SSM_STATE), lambda g, c: (last - c, g))
    dtraw_t, dt_bias, a_log, dskip = [a.reshape(SSM_HEADS, 1, -1) for a in (dtraw_t, dt_bias, a_log, dskip)]
    res, side_dst = grid_call(
        body, (xs_hm, dtraw_t, dt_bias, a_log, dskip, xbc, xbc, prev_all, dy_hm), name="ssd_bwd",
        out_shape=[jax.ShapeDtypeStruct((SSM_HEADS, S, SSM_HDIM), F32), jax.ShapeDtypeStruct((SSM_HEADS, 1, S), F32),
                   jax.ShapeDtypeStruct((SSM_HEADS, 1, 1), F32), jax.ShapeDtypeStruct((SSM_HEADS, 1, 1), F32),
                   jax.ShapeDtypeStruct((SSM_HEADS, 1, 1), F32),
                   jax.ShapeDtypeStruct((S, SSM_GROUPS * SSM_STATE), F32),
                   jax.ShapeDtypeStruct((S, SSM_GROUPS * SSM_STATE), F32)],
        grid=(SSM_GROUPS, N_CHUNKS),
        in_specs=[xspec, tspec, hp, hp, hp,
                  pl.BlockSpec((CHUNK, SSM_STATE), lambda g, c: (last - c, BM_COL0 + g)),
                  pl.BlockSpec((CHUNK, SSM_STATE), lambda g, c: (last - c, CM_COL0 + g)),
                  pl.BlockSpec((1, hg, SSM_HDIM, SSM_STATE), lambda g, c: (last - c, g, 0, 0)), xspec],
        out_specs=[xspec, tspec, hp, hp, hp, gspec, gspec],
        scratch_shapes=[pltpu.VMEM((hg, SSM_HDIM, SSM_STATE), F32)],
        semantics=("parallel", "arbitrary"), side=side)
    return [res[0]] + [r.reshape(SSM_HEADS, -1) for r in res[1:5]] + list(res[5:]), side_dst


ATT_HB = 8


def _bdot(a, b, ca, cb):
    return lax.dot_general(a.astype(BF16), b.astype(BF16), (((ca,), (cb,)), ((0,), (0,))), preferred_element_type=F32)


@jax.custom_vjp
def bmm(a, b):
    return _bdot(a, b, 2, 1)


def _bmm_fwd(a, b):
    return _bdot(a, b, 2, 1), (a, b)


def _bmm_bwd(res, g):
    a, b = res
    return _bdot(g, b, 2, 2).astype(a.dtype), _bdot(a, g, 1, 1).astype(b.dtype)


bmm.defvjp(_bmm_fwd, _bmm_bwd)


@jax.custom_vjp
def bmm_nt(a, b):
    return _bdot(a, b, 2, 2)


def _bmm_nt_fwd(a, b):
    return _bdot(a, b, 2, 2), (a, b)


def _bmm_nt_bwd(res, g):
    a, b = res
    return _bdot(g, b, 2, 1).astype(a.dtype), _bdot(g, a, 1, 1).astype(b.dtype)


bmm_nt.defvjp(_bmm_nt_fwd, _bmm_nt_bwd)


@jax.custom_vjp
def bmm_tn(a, b):
    return _bdot(a, b, 1, 1)


def _bmm_tn_fwd(a, b):
    return _bdot(a, b, 1, 1), (a, b)


def _bmm_tn_bwd(res, g):
    a, b = res
    return _bdot(b, g, 2, 2).astype(a.dtype), _bdot(a, g, 2, 1).astype(b.dtype)


bmm_tn.defvjp(_bmm_tn_fwd, _bmm_tn_bwd)


def att_heads(q, kp, kc, vp, vc, bias_p, bias_c, has_prev):
    h, b, dh = q.shape
    i = lax.broadcasted_iota(jnp.int32, (1, b, b), 1)
    j = lax.broadcasted_iota(jnp.int32, (1, b, b), 2)
    scale = dh ** -0.5
    sp = jnp.where(jnp.logical_and(j >= i, has_prev), bmm_nt(q, kp) * scale + bias_p, -1e30)
    sc = jnp.where(j <= i, bmm_nt(q, kc) * scale + bias_c, -1e30)
    m = lax.stop_gradient(jnp.maximum(jnp.max(sp, axis=2, keepdims=True), jnp.max(sc, axis=2, keepdims=True)))
    pp, pc = jnp.exp(sp - m), jnp.exp(sc - m)
    l = jnp.sum(pp, axis=2, keepdims=True) + jnp.sum(pc, axis=2, keepdims=True)
    o = bmm(pp / l, vp) + bmm(pc / l, vc)
    return o, jnp.broadcast_to(m + jnp.log(l), (h, b, dh))


def _att_specs(nb):
    hb, blk = ATT_HB, ATT_BLK
    cur = pl.BlockSpec((hb, blk, ATT_HDIM), lambda h, b: (h, b, 0))
    prv = pl.BlockSpec((hb, blk, ATT_HDIM), lambda h, b: (h, jnp.maximum(b - 1, 0), 0))
    bias = pl.BlockSpec((hb, 2, blk, blk), lambda h, b: (h, 0, 0, 0))
    return cur, prv, bias


def att_fwd(q, k, v, bias, nb, *, name):
    cur, prv, bspec = _att_specs(nb)

    def body(q_ref, kp_ref, kc_ref, vp_ref, vc_ref, b_ref, o_ref, l_ref):
        has_prev = (pl.program_id(1) % nb) != 0
        o, lse = att_heads(q_ref[...], kp_ref[...], kc_ref[...], vp_ref[...], vc_ref[...], b_ref[:, 0], b_ref[:, 1],
                           has_prev)
        o_ref[...] = o
        l_ref[...] = lse

    shp = jax.ShapeDtypeStruct((ATT_HEADS, S, ATT_HDIM), F32)
    return pl.pallas_call(
        body, name=name, out_shape=[shp, shp],
        grid=(ATT_HEADS // ATT_HB, S // ATT_BLK),
        in_specs=[cur, prv, cur, prv, cur, bspec],
        out_specs=[cur, cur],
        compiler_params=_cparams("parallel", "parallel"),
    )(q, k, k, v, v, bias)


def att_bwd(q, k, v, bias, do, dlse, nb, *, name):
    cur, prv, bspec = _att_specs(nb)

    def body(q_ref, kp_ref, kc_ref, vp_ref, vc_ref, b_ref, do_ref, dl_ref,
             dq_ref, dkc_ref, dkp_ref, dvc_ref, dvp_ref, db_ref):
        has_prev = (pl.program_id(1) % nb) != 0

        @pl.when(pl.program_id(1) == 0)
        def _():
            db_ref[...] = jnp.zeros_like(db_ref)

        ins = _f32([q_ref[...], kp_ref[...], kc_ref[...], vp_ref[...], vc_ref[...]]) + [b_ref[:, 0], b_ref[:, 1]]
        _, vjp = jax.vjp(functools.partial(att_heads, has_prev=has_prev), *ins)
        dq, dkp, dkc, dvp, dvc, dbp, dbc = vjp((do_ref[...], dl_ref[...]))
        dq_ref[...] = dq
        dkc_ref[...] = dkc
        dkp_ref[...] = dkp
        dvc_ref[...] = dvc
        dvp_ref[...] = dvp
        db_ref[:, 0] += dbp
        db_ref[:, 1] += dbc

    shp = jax.ShapeDtypeStruct((ATT_HEADS, S, ATT_HDIM), F32)
    return pl.pallas_call(
        body, name=name,
        out_shape=[shp] * 5 + [jax.ShapeDtypeStruct((ATT_HEADS, 2, ATT_BLK, ATT_BLK), F32)],
        grid=(ATT_HEADS // ATT_HB, S // ATT_BLK),
        in_specs=[cur, prv, cur, prv, cur, bspec, cur, cur],
        out_specs=[cur] * 5 + [bspec],
        compiler_params=_cparams("parallel", "arbitrary"),
    )(q, k, k, v, v, bias, do, dlse)


def shift_add(cur, prev, nb, *, name):
    n_blocks = S // ATT_BLK

    def body(c_ref, p_ref, o_ref):
        nxt = pl.program_id(0) + 1
        keep = jnp.where((nxt % nb) != 0, 1.0, 0.0)
        o_ref[...] = c_ref[...] + keep * p_ref[...]

    return pl.pallas_call(
        body, name=name, out_shape=jax.ShapeDtypeStruct(cur.shape, F32),
        grid=(n_blocks,),
        in_specs=[pl.BlockSpec((ATT_HEADS, ATT_BLK, ATT_HDIM), lambda b: (0, b, 0)),
                  pl.BlockSpec((ATT_HEADS, ATT_BLK, ATT_HDIM), lambda b: (0, jnp.minimum(b + 1, n_blocks - 1), 0))],
        out_specs=pl.BlockSpec((ATT_HEADS, ATT_BLK, ATT_HDIM), lambda b: (0, b, 0)),
        compiler_params=_cparams("parallel"),
    )(cur, prev)


ATT_PAIRS = ATT_HEADS // 2
PAIR_W = 2 * ATT_HDIM


def att_pairs(q, kp, kc, vp, vc, bias, has_prev):
    t, b, w = q.shape
    i = lax.broadcasted_iota(jnp.int32, (1, b, b), 1)
    j = lax.broadcasted_iota(jnp.int32, (1, b, b), 2)
    first = lax.broadcasted_iota(jnp.int32, (1, 1, w), 2) < ATT_HDIM
    scale = ATT_HDIM ** -0.5
    outs, lses = [], []
    for ab in range(2):
        qh = jnp.where(first if ab == 0 else jnp.logical_not(first), q, 0.0)
        sp = jnp.where(jnp.logical_and(j >= i, has_prev), bmm_nt(qh, kp) * scale + bias[:, ab, 0], -1e30)
        sc = jnp.where(j <= i, bmm_nt(qh, kc) * scale + bias[:, ab, 1], -1e30)
        m = lax.stop_gradient(jnp.maximum(jnp.max(sp, axis=2, keepdims=True), jnp.max(sc, axis=2, keepdims=True)))
        pp, pc = jnp.exp(sp - m), jnp.exp(sc - m)
        l = jnp.sum(pp, axis=2, keepdims=True) + jnp.sum(pc, axis=2, keepdims=True)
        outs.append(bmm(pp / l, vp) + bmm(pc / l, vc))
        lses.append(jnp.broadcast_to(m + jnp.log(l), (t, b, w)))
    return jnp.where(first, outs[0], outs[1]), jnp.where(first, lses[0], lses[1])


def _pair_tiles(ref):
    return jnp.stack([ref[:, PAIR_W * t:PAIR_W * (t + 1)] for t in range(ATT_PAIRS)])


def _store_pair_tiles(ref, val):
    for t in range(ATT_PAIRS):
        ref[:, PAIR_W * t:PAIR_W * (t + 1)] = val[t].astype(ref.dtype)


def pair_bias(bias):
    return bias.reshape(ATT_PAIRS, 2, 2, ATT_BLK, ATT_BLK)


def att2_fwd(q, k, v, bias, nb, cols, *, name, side=None):
    n_blocks = S // ATT_BLK
    qc, kc, vc = cols

    def body(q_ref, k_ref, v_ref, b_ref, o_ref, l_ref, kprev, vprev):
        blk = pl.program_id(0)

        @pl.when(blk == 0)
        def _():
            kprev[...] = jnp.zeros_like(kprev)
            vprev[...] = jnp.zeros_like(vprev)

        k3, v3 = _pair_tiles(k_ref), _pair_tiles(v_ref)
        o, lse = att_pairs(_pair_tiles(q_ref), kprev[...], k3, vprev[...], v3, b_ref[...], (blk % nb) != 0)
        _store_pair_tiles(o_ref, o)
        _store_pair_tiles(l_ref, lse)
        kprev[...] = k3
        vprev[...] = v3

    def spec(c):
        return pl.BlockSpec((ATT_BLK, D), lambda b: (b, c))

    return grid_call(
        body, (q, k, v, bias), name=name,
        out_shape=[jax.ShapeDtypeStruct((S, D), BF16), jax.ShapeDtypeStruct((S, D), F32)], grid=(n_blocks,),
        in_specs=[spec(qc), spec(kc), spec(vc), pl.BlockSpec(bias.shape, lambda b: (0, 0, 0, 0, 0))],
        out_specs=[spec(0), spec(0)],
        scratch_shapes=[pltpu.VMEM((ATT_PAIRS, ATT_BLK, PAIR_W), BF16), pltpu.VMEM((ATT_PAIRS, ATT_BLK, PAIR_W), BF16)],
        semantics=("arbitrary",), side=side)


def att2_bwd(q, k, v, bias, do, dlse, nb, cols, *, name, side=None):
    n_blocks = S // ATT_BLK
    qc, kc, vc = cols

    def body(q_ref, k_ref, v_ref, b_ref, do_ref, dl_ref, dq_ref, dk_ref, dv_ref, db_ref, kprev, vprev, dk_own, dv_own):
        blk = pl.program_id(0)

        @pl.when(blk == 0)
        def _():
            for r in (kprev, vprev, dk_own, dv_own, db_ref):
                r[...] = jnp.zeros_like(r)

        @pl.when(blk < n_blocks)
        def _():
            k3, v3 = _pair_tiles(k_ref), _pair_tiles(v_ref)
            ins = _f32([_pair_tiles(q_ref), kprev[...], k3, vprev[...], v3]) + [b_ref[...]]
            _, vjp = jax.vjp(functools.partial(att_pairs, has_prev=(blk % nb) != 0), *ins)
            dq, dkp, dkc, dvp, dvc, db = vjp(tuple(_f32([_pair_tiles(do_ref), _pair_tiles(dl_ref)])))
            _store_pair_tiles(dq_ref, dq)
            _store_pair_tiles(dk_ref, dk_own[...] + dkp)
            _store_pair_tiles(dv_ref, dv_own[...] + dvp)
            dk_own[...] = dkc
            dv_own[...] = dvc
            db_ref[...] += db
            kprev[...] = k3
            vprev[...] = v3

        @pl.when(blk == n_blocks)
        def _():
            _store_pair_tiles(dk_ref, dk_own[...])
            _store_pair_tiles(dv_ref, dv_own[...])

    def spec(c):
        return pl.BlockSpec((ATT_BLK, D), lambda b: (jnp.minimum(b, n_blocks - 1), c))

    late = pl.BlockSpec((ATT_BLK, D), lambda b: (jnp.maximum(b - 1, 0), 0))
    bspec = pl.BlockSpec(bias.shape, lambda b: (0, 0, 0, 0, 0))
    tile_f32 = pltpu.VMEM((ATT_PAIRS, ATT_BLK, PAIR_W), F32)
    tile_bf16 = pltpu.VMEM((ATT_PAIRS, ATT_BLK, PAIR_W), BF16)
    return grid_call(
        body, (q, k, v, bias, do, dlse), name=name,
        out_shape=[jax.ShapeDtypeStruct((S, D), BF16), jax.ShapeDtypeStruct((S, D), BF16),
                   jax.ShapeDtypeStruct((S, D), BF16), jax.ShapeDtypeStruct(bias.shape, F32)],
        grid=(n_blocks + 1,),
        in_specs=[spec(qc), spec(kc), spec(vc), bspec, spec(0), spec(0)],
        out_specs=[spec(0), late, late, bspec],
        scratch_shapes=[tile_bf16, tile_bf16, tile_f32, tile_f32],
        semantics=("arbitrary",), side=side)


def regroup(a, dil, inverse=False):
    if dil == 1:
        return a
    c_dim = a.shape[1]
    shape = (dil, S // dil, c_dim) if inverse else (S // dil, dil, c_dim)
    return jnp.transpose(a.reshape(shape), (1, 0, 2)).reshape(S, c_dim)


SSD_PAIRS = SSM_HEADS // 2
PAIRS_PER_GROUP = SSD_PAIRS // SSM_GROUPS
GROUP_W = HEADS_PER_GROUP * SSM_HDIM


def ssd_pairs(x, dtraw, dt_bias, a_log, dskip, bm, cm, prev):
    t, q, w = x.shape
    n = bm.shape[1]
    li = lax.broadcasted_iota(jnp.int32, (1, q, q), 1)
    si = lax.broadcasted_iota(jnp.int32, (1, q, q), 2)
    first_lane = lax.broadcasted_iota(jnp.int32, (1, 1, w), 2) < SSM_HDIM
    first_row = lax.broadcasted_iota(jnp.int32, (1, w, 1), 1) < SSM_HDIM

    def to_col(row):
        return jnp.sum(jnp.where(li == si, jnp.broadcast_to(row, (t, q, q)), 0.0), axis=2, keepdims=True)

    def lanes(a0, a1):
        return jnp.where(first_lane, a0, a1)

    dt_col, acs_col, total, lmat = [], [], [], []
    for ab in range(2):
        dt_row = _softplus(dtraw[ab] + dt_bias[ab])
        a_row = dt_row * (-jnp.exp(a_log[ab]))
        a_col = to_col(a_row)
        acs_c = jnp.sum(jnp.where(si <= li, jnp.broadcast_to(a_row, (t, q, q)), 0.0), axis=2, keepdims=True)
        acs_r = jnp.sum(jnp.where(li <= si, jnp.broadcast_to(a_col, (t, q, q)), 0.0), axis=1, keepdims=True)
        dt_col.append(to_col(dt_row))
        acs_col.append(acs_c)
        total.append(jnp.sum(a_row, axis=2, keepdims=True))
        lmat.append(jnp.exp(jnp.where(li >= si, acs_c - acs_r, -1e30)))
    cb = mm_nt(cm, bm)[None]
    bmb = jnp.broadcast_to(bm[None], (t, q, n))
    cmb = jnp.broadcast_to(cm[None], (t, q, n))
    xdt = x * lanes(dt_col[0], dt_col[1])
    y = lanes(bmm(cb * lmat[0], xdt), bmm(cb * lmat[1], xdt))
    y = y + bmm_nt(cmb, prev) * lanes(jnp.exp(acs_col[0]), jnp.exp(acs_col[1]))
    y = y + lanes(dskip[0], dskip[1]) * x
    state = bmm_tn(xdt * lanes(jnp.exp(total[0] - acs_col[0]), jnp.exp(total[1] - acs_col[1])), bmb)
    return y, jnp.where(first_row, jnp.exp(total[0]), jnp.exp(total[1])) * prev + state


def _group_tiles(ref):
    return jnp.stack([ref[:, PAIR_W * t:PAIR_W * (t + 1)] for t in range(PAIRS_PER_GROUP)])


def _store_group_tiles(ref, val):
    for t in range(PAIRS_PER_GROUP):
        ref[:, PAIR_W * t:PAIR_W * (t + 1)] = val[t]


def _by_pair(a):
    return jnp.transpose(a.reshape(SSD_PAIRS, 2, 1, -1), (1, 0, 2, 3))


def _by_head(a):
    return jnp.transpose(a, (1, 0, 2, 3)).reshape(SSM_HEADS, -1)


def _ssd2_specs(chunk_of):
    tp = PAIRS_PER_GROUP
    xspec = pl.BlockSpec((CHUNK, GROUP_W), lambda g, c: (chunk_of(c), g))
    tspec = pl.BlockSpec((2, tp, 1, CHUNK), lambda g, c: (0, g, 0, chunk_of(c)))
    hp = pl.BlockSpec((2, tp, 1, 1), lambda g, c: (0, g, 0, 0))
    gspec = pl.BlockSpec((CHUNK, SSM_STATE), lambda g, c: (chunk_of(c), g))
    sspec = pl.BlockSpec((1, tp, PAIR_W, SSM_STATE), lambda g, c: (chunk_of(c), g, 0, 0))
    return xspec, tspec, hp, gspec, sspec


def ssd2_fwd(xs, dtraw_t, dt_bias, a_log, dskip, bm, cm, side=None):
    def body(x_ref, dt_ref, dtb_ref, al_ref, dk_ref, bm_ref, cm_ref, y_ref, prev_ref, state_ref):
        @pl.when(pl.program_id(1) == 0)
        def _():
            state_ref[...] = jnp.zeros_like(state_ref)

        prev = state_ref[...]
        prev_ref[0] = prev
        y, nxt = ssd_pairs(_group_tiles(x_ref), dt_ref[...], dtb_ref[...], al_ref[...], dk_ref[...], bm_ref[...],
                           cm_ref[...], prev)
        _store_group_tiles(y_ref, y)
        state_ref[...] = nxt

    xspec, tspec, hp, gspec, sspec = _ssd2_specs(lambda c: c)
    return grid_call(
        body, (xs, _by_pair(dtraw_t), _by_pair(dt_bias), _by_pair(a_log), _by_pair(dskip), bm, cm), name="ssd_fwd",
        out_shape=[jax.ShapeDtypeStruct((S, SSM_INNER), F32),
                   jax.ShapeDtypeStruct((N_CHUNKS, SSD_PAIRS, PAIR_W, SSM_STATE), F32)],
        grid=(SSM_GROUPS, N_CHUNKS), in_specs=[xspec, tspec, hp, hp, hp, gspec, gspec], out_specs=[xspec, sspec],
        scratch_shapes=[pltpu.VMEM((PAIRS_PER_GROUP, PAIR_W, SSM_STATE), F32)],
        semantics=("parallel", "arbitrary"), side=side)


def ssd2_bwd(xs, dtraw_t, dt_bias, a_log, dskip, bm, cm, prev_all, dy, side=None):
    def body(x_ref, dt_ref, dtb_ref, al_ref, dk_ref, bm_ref, cm_ref, prev_ref, dy_ref,
             dx_ref, ddt_ref, ddtb_ref, dal_ref, ddk_ref, dbm_ref, dcm_ref, dstate_ref):
        @pl.when(pl.program_id(1) == 0)
        def _():
            for r in (dstate_ref, ddtb_ref, dal_ref, ddk_ref):
                r[...] = jnp.zeros_like(r)

        _, vjp = jax.vjp(ssd_pairs, _group_tiles(x_ref), dt_ref[...], dtb_ref[...], al_ref[...], dk_ref[...], bm_ref[...],
                         cm_ref[...], prev_ref[0])
        dx, ddt, ddtb, dal, ddk, dbm, dcm, dprev = vjp((_group_tiles(dy_ref), dstate_ref[...]))
        _store_group_tiles(dx_ref, dx)
        ddt_ref[...] = ddt
        ddtb_ref[...] += ddtb
        dal_ref[...] += dal
        ddk_ref[...] += ddk
        dbm_ref[...] = dbm
        dcm_ref[...] = dcm
        dstate_ref[...] = dprev

    xspec, tspec, hp, gspec, sspec = _ssd2_specs(lambda c: N_CHUNKS - 1 - c)
    par = jax.ShapeDtypeStruct((2, SSD_PAIRS, 1, 1), F32)
    res, side_dst = grid_call(
        body, (xs, _by_pair(dtraw_t), _by_pair(dt_bias), _by_pair(a_log), _by_pair(dskip), bm, cm, prev_all, dy),
        name="ssd_bwd",
        out_shape=[jax.ShapeDtypeStruct((S, SSM_INNER), F32), jax.ShapeDtypeStruct((2, SSD_PAIRS, 1, S), F32), par, par, par,
                   jax.ShapeDtypeStruct((S, SSM_GROUPS * SSM_STATE), F32),
                   jax.ShapeDtypeStruct((S, SSM_GROUPS * SSM_STATE), F32)],
        grid=(SSM_GROUPS, N_CHUNKS), in_specs=[xspec, tspec, hp, hp, hp, gspec, gspec, sspec, xspec],
        out_specs=[xspec, tspec, hp, hp, hp, gspec, gspec],
        scratch_shapes=[pltpu.VMEM((PAIRS_PER_GROUP, PAIR_W, SSM_STATE), F32)],
        semantics=("parallel", "arbitrary"), side=side)
    return [res[0]] + [_by_head(r) for r in res[1:5]] + list(res[5:]), side_dst


def _silu(x):
    return x * jax.nn.sigmoid(x)


def _rms(x):
    return x * lax.rsqrt(jnp.mean(x * x, -1, keepdims=True) + EPS)


def f_normmod(x, g, sc, sh):
    return (_rms(x) * g * (1.0 + sc) + sh,)


def f_resid(x, mix, gate):
    return (x + gate * mix,)


def f_resid_bias(x, mix, gate, b):
    return (x + gate * (mix + b),)


def f_swiglu(hgu):
    return (_silu(hgu[:, :FFN_HIDDEN]) * hgu[:, FFN_HIDDEN:],)


def f_silu(x):
    return (_silu(x),)


def f_silu_xbc(x):
    y = _silu(x)
    n_b = SSM_GROUPS * SSM_STATE
    return y[:, :SSM_INNER], y[:, SSM_INNER:SSM_INNER + n_b], y[:, SSM_INNER + n_b:]


def f_gated_norm(y, z, g):
    return (_rms(y * _silu(z)) * g,)


def f_glu(y, b):
    y = y + b
    return (y[:, :D] * jax.nn.sigmoid(y[:, D:]),)


def f_ln_silu(u, g, b):
    mu = jnp.mean(u, -1, keepdims=True)
    var = jnp.mean(jnp.square(u - mu), -1, keepdims=True)
    return (_silu((u - mu) * lax.rsqrt(var + EPS) * g + b),)


def f_combine(o1, o2, o3, l1, l2, l3):
    m = lax.stop_gradient(jnp.maximum(jnp.maximum(l1, l2), l3))
    e1, e2, e3 = jnp.exp(l1 - m), jnp.exp(l2 - m), jnp.exp(l3 - m)
    return ((e1 * o1 + e2 * o2 + e3 * o3) / (e1 + e2 + e3),)


def f_head(x, tgt, g):
    return (0.5 * jnp.mean(jnp.square(_rms(x) * g - tgt), -1, keepdims=True),)


def f_sum3(a, b, c):
    return (a + b + c,)


def f_sum4(a, b, c, d):
    return (a + b + c + d,)


def f_add(a, b):
    return (a + b,)


def f_adamw(w, g, m, v):
    m = ADAM_B1 * m + (1.0 - ADAM_B1) * g
    v = ADAM_B2 * v + (1.0 - ADAM_B2) * jnp.square(g)
    m_hat = m / (1.0 - ADAM_B1 ** ADAM_STEP)
    v_hat = v / (1.0 - ADAM_B2 ** ADAM_STEP)
    return -ADAM_LR * (m_hat / (jnp.sqrt(v_hat) + ADAM_EPS) + ADAM_WD * w), m, v


def _rows_tile(r, cap=256):
    return _pick(r, cap, mult=8)


def adamw(w, g, m, v, *, name):
    l_dim, r_dim, c_dim = w.shape
    tr = _rows_tile(r_dim, cap=128)

    def body(w_ref, g_ref, m_ref, v_ref, d_ref, mo_ref, vo_ref):
        d_ref[...], mo_ref[...], vo_ref[...] = f_adamw(w_ref[...], g_ref[...], m_ref[...], v_ref[...])

    spec = pl.BlockSpec((1, tr, c_dim), lambda l, i: (l, i, 0))
    return pl.pallas_call(
        body, name=name, out_shape=[jax.ShapeDtypeStruct(w.shape, F32)] * 3, grid=(l_dim, r_dim // tr),
        in_specs=[spec] * 4, out_specs=[spec] * 3, compiler_params=_cparams("parallel", "parallel"),
    )(w, g, m, v)


def _t5_bucket(dist):
    max_exact = REL_BUCKETS // 2
    n = jnp.maximum(dist, 1).astype(F32)
    large = max_exact + jnp.log(n / max_exact) / math.log(REL_MAX_DIST / max_exact) * (REL_BUCKETS - max_exact)
    large = jnp.minimum(large.astype(jnp.int32), REL_BUCKETS - 1)
    return jnp.where(dist < max_exact, dist, large)


def _att_buckets(dil):
    i = jnp.arange(ATT_BLK)[:, None]
    j = jnp.arange(2 * ATT_BLK)[None, :]
    bkt = _t5_bucket(jnp.maximum(ATT_BLK + i - j, 0) * dil)
    return jnp.transpose(bkt.reshape(ATT_BLK, 2, ATT_BLK), (1, 0, 2))


def att_bias(rel_table, p, dil):
    tab = rel_table[:, p * ATT_HEADS:(p + 1) * ATT_HEADS]
    onehot = (jnp.arange(REL_BUCKETS)[:, None] == _att_buckets(dil).reshape(1, -1)).astype(F32)
    bias = lax.dot_general(tab, onehot, (((0,), (0,)), ((), ())), precision=lax.Precision.HIGHEST)
    return bias.reshape(ATT_HEADS, 2, ATT_BLK, ATT_BLK)


def att_bias_grad(dbias, dil, *, name):
    onehot = (_att_buckets(dil).reshape(-1, 1) == jnp.arange(LANES)[None, :]).astype(BF16)
    dtab = matmul(dbias.reshape(ATT_HEADS, -1), onehot, mode="nn", out_dtype=F32, name=name, tk_cap=2048)
    return dtab[:, :REL_BUCKETS].T


def to_heads(a, n_heads, dil=1):
    hd = a.shape[1] // n_heads
    return jnp.transpose(a.reshape(S // dil, dil, n_heads, hd), (2, 1, 0, 3)).reshape(n_heads, S, hd)


def from_heads(a, dil=1):
    n_heads, _, hd = a.shape
    return jnp.transpose(a.reshape(n_heads, dil, S // dil, hd), (2, 1, 0, 3)).reshape(S, n_heads * hd)


def regroup_heads(a, dil, inverse=False):
    n_heads, _, hd = a.shape
    if dil == 1:
        return a
    if inverse:
        return jnp.transpose(a.reshape(n_heads, dil, S // dil, hd), (0, 2, 1, 3)).reshape(n_heads, S, hd)
    return jnp.transpose(a.reshape(n_heads, S // dil, dil, hd), (0, 2, 1, 3)).reshape(n_heads, S, hd)


HY_Z, HY_XBC, HY_DT, HY_Q, HY_K, HY_V = 2048, 3072, 32, 3072, 1024, 1024
HY_IN = HY_Z + HY_XBC + HY_DT + HY_Q + HY_K + HY_V
OFF_Z, OFF_XBC, OFF_Q, OFF_KV, OFF_DT = 0, 2048, 5120, 8192, 10240
HY_CAT = OFF_DT + LANES
DT_PAD = LANES


def hy_to_cat(w):
    z, xbc, dt, qkv = w[:2048], w[2048:5120], w[5120:5152], w[5152:]
    return jnp.concatenate([z, xbc, qkv, dt, jnp.zeros((DT_PAD - HY_DT,) + w.shape[1:], w.dtype)], axis=0)


def hy_from_cat(w, axis=0):
    part = lambda a, b: lax.slice_in_dim(w, a, b, axis=axis)
    return jnp.concatenate([part(0, 5120), part(OFF_DT, OFF_DT + HY_DT), part(5120, OFF_DT)], axis=axis)


def device_step(x, tgt, mods, wts, sp, comm=None):
    g = {}
    dmods = [[None] * 6 for _ in range(2)]
    wts = dict(wts)

    def wgrad(tokens_d, tokens_n, nm):
        return matmul(transpose(tokens_d, name=nm + "_t"), tokens_n, mode="nn", out_dtype=BF16, name=nm, out_t=True,
                      tk_cap=2048)

    def w_side(i):
        return None if comm is None else GatherRows(comm["pack"], comm["full"], *W_BATCHES[i])

    def g_side(i):
        return None if comm is None else ScatterRows(comm["ga"], comm["recv"], *G_BATCHES[i])

    def normmod(xi, gain, sc, sh, nm):
        return rowmap(f_normmod, [xi], [gain, sc, sh], [BF16], name=nm)[0]

    def ffn_fwd(xi, i, gate, nm):
        h = normmod(xi, sp["norm_ffn_g"][i], mods[i][4], mods[i][3], nm + "_norm")
        hgu = matmul(h, wts["gu_t"][i], mode="nt", out_dtype=BF16, name=nm + "_gu")
        act = rowmap(f_swiglu, [hgu], [], [BF16], name=nm + "_act", tr=128)[0]
        out = matmul(act, wts["down"][i], mode="nn", out_dtype=F32, name=nm + "_down")
        xo = rowmap(f_resid, [xi, out], [gate], [F32], name=nm + "_res")[0]
        return xo, (h, hgu, act, out)

    def ffn_bwd(dres, xi, i, saved, nm):
        h, hgu, act, out = saved
        (dout,), (dgate,), _ = rowmap_bwd(f_resid, [xi, out], [mods[i][5]], [dres], name=nm + "_res_b",
                                          row_grad=[False, True], row_dtypes=[BF16])
        dmods[i][5] = dgate
        dact = matmul(dout, wts["down"][i], mode="nt", out_dtype=BF16, name=nm + "_down_dx")
        g[f"down{i}"] = wgrad(dout, act, nm + "_down_dw")
        (dhgu,), _, _ = rowmap_bwd(f_swiglu, [hgu], [], [dact], name=nm + "_act_b", row_grad=[True],
                                   row_dtypes=[BF16], tr=128)
        g[f"gu_t{i}"] = wgrad(h, dhgu, nm + "_gu_dw")
        dh = matmul(dhgu, wts["gu_t"][i], mode="nn", out_dtype=F32, name=nm + "_gu_dx")
        (dres,), (dg_, dsc, dsh), _ = rowmap_bwd(f_normmod, [xi], [sp["norm_ffn_g"][i], mods[i][4], mods[i][3]], [dh],
                                                 name=nm + "_norm_b", row_grad=[True], row_add=[dres])
        g[f"norm_ffn_g{i}"] = dg_
        dmods[i][4], dmods[i][3] = dsc, dsh
        return dres

    h0 = normmod(x, sp["norm_mix_g"][0], mods[0][1], mods[0][0], "l0_norm")
    w_in = wts["hy_in_t"]
    z = matmul(h0, w_in, mode="nt", out_dtype=F32, name="hy_z", n=HY_Z, b_off=OFF_Z)
    xbc_raw = matmul(h0, w_in, mode="nt", out_dtype=F32, name="hy_xbc", n=HY_XBC, b_off=OFF_XBC)
    q = matmul(h0, w_in, mode="nt", out_dtype=BF16, name="hy_q", n=HY_Q, b_off=OFF_Q)
    kv = matmul(h0, w_in, mode="nt", out_dtype=BF16, name="hy_kv", n=HY_K + HY_V, b_off=OFF_KV)
    dtr = matmul(h0, w_in, mode="nt", out_dtype=F32, name="hy_dt", n=DT_PAD, b_off=OFF_DT)
    xbc_pre = conv_fwd(xbc_raw, sp["hy_conv_w"], sp["hy_conv_b"], name="hy_conv")
    xs, bm, cm = rowmap(f_silu_xbc, [xbc_pre], [], [F32] * 3, name="hy_conv_act", tr=128)
    dtraw_t = dtr[:, :HY_DT].T
    (y, prev_all), full = ssd2_fwd(xs, dtraw_t, sp["hy_dt_bias"], sp["hy_a_log"], sp["hy_d_skip"], bm, cm, side=w_side(1))
    if comm is not None:
        comm["full"] = full
    ysn = rowmap(f_gated_norm, [y, z], [sp["hy_ssm_norm_g"]], [BF16], name="hy_gnorm", tr=128)[0]
    att_in, att_o, att_l = [], [], []
    for p, (win, dil) in enumerate(ATT_PATTERNS):
        if dil == 1:
            qa, ka, va, cols = q, kv, kv, (p, 0, 1)
        else:
            qa, ka, cols = regroup(q[:, p * D:(p + 1) * D], dil), regroup(kv, dil), (0, 0, 1)
            va = ka
        bias = pair_bias(att_bias(sp["rel_table"], p, dil))
        nb = S // dil // ATT_BLK
        (o, lse), full = att2_fwd(qa, ka, va, bias, nb, cols, name=f"att_fwd{p}", side=w_side(2 + p))
        if comm is not None:
            comm["full"] = full
        att_in.append((qa, ka, va, bias, nb, cols))
        att_o.append(regroup(o, dil, inverse=True))
        att_l.append(regroup(lse, dil, inverse=True))
    if comm is not None:
        wts.update(unpack_weights(comm["full"], skip=("hy_in_t",)))
    att = rowmap(f_combine, att_o + att_l, [], [BF16], name="att_combine", tr=128)[0]
    cat = jnp.concatenate([ysn, att], axis=-1)
    mix0 = matmul(cat, wts["hy_out"], mode="nn", out_dtype=F32, name="hy_out")
    x1 = rowmap(f_resid, [x, mix0], [mods[0][2]], [F32], name="l0_res")[0]
    x2, ffn0 = ffn_fwd(x1, 0, mods[0][5], "ffn0")

    h1 = normmod(x2, sp["norm_mix_g"][1], mods[1][1], mods[1][0], "l1_norm")
    p1 = matmul(h1, wts["pw1_t"], mode="nt", out_dtype=F32, name="cv_pw1")
    u = rowmap(f_glu, [p1], [sp["cv_b_pw1"]], [F32], name="cv_glu")[0]
    uc = conv_fwd(u, sp["cv_w_dw"], sp["cv_b_dw"], name="cv_conv")
    ul = rowmap(f_ln_silu, [uc], [sp["cv_ln_g"], sp["cv_ln_b"]], [BF16], name="cv_ln")[0]
    mix1 = matmul(ul, wts["pw2"], mode="nn", out_dtype=F32, name="cv_pw2")
    x3 = rowmap(f_resid_bias, [x2, mix1], [mods[1][2], sp["cv_b_pw2"]], [F32], name="l1_res")[0]
    x4, ffn1 = ffn_fwd(x3, 1, mods[1][5], "ffn1")

    ones = jnp.ones((S, 1), F32)
    (dres,), (dfinal,), (loss_rows,) = rowmap_bwd(f_head, [x4, tgt], [sp["final_norm_g"]], [ones], name="head",
                                                  row_grad=[True, False], emit=(0,))
    g["final_norm_g"] = dfinal

    dres = ffn_bwd(dres, x3, 1, ffn1, "ffn1")
    (dmix1,), (dg1, db2), _ = rowmap_bwd(f_resid_bias, [x2, mix1], [mods[1][2], sp["cv_b_pw2"]], [dres], name="l1_res_b",
                                         row_grad=[False, True], row_dtypes=[BF16])
    dmods[1][2] = dg1
    g["cv_b_pw2"] = db2
    dul = matmul(dmix1, wts["pw2"], mode="nt", out_dtype=F32, name="cv_pw2_dx")
    g["pw2"] = wgrad(dmix1, ul, "cv_pw2_dw")
    (duc,), (g["cv_ln_g"], g["cv_ln_b"]), _ = rowmap_bwd(f_ln_silu, [uc], [sp["cv_ln_g"], sp["cv_ln_b"]], [dul],
                                                         name="cv_ln_b", row_grad=[True])
    du, g["cv_w_dw"], g["cv_b_dw"] = conv_bwd(u, sp["cv_w_dw"], duc, name="cv_conv_b", cb=128, chunk_rows=128)
    (dp1,), (g["cv_b_pw1"],), _ = rowmap_bwd(f_glu, [p1], [sp["cv_b_pw1"]], [du], name="cv_glu_b", row_grad=[True],
                                             row_dtypes=[BF16])
    g["pw1_t"] = wgrad(h1, dp1, "cv_pw1_dw")
    dh1 = matmul(dp1, wts["pw1_t"], mode="nn", out_dtype=F32, name="cv_pw1_dx")
    (dres,), (dg_, dsc, dsh), _ = rowmap_bwd(f_normmod, [x2], [sp["norm_mix_g"][1], mods[1][1], mods[1][0]], [dh1],
                                             name="l1_norm_b", row_grad=[True], row_add=[dres])
    g["norm_mix_g1"] = dg_
    dmods[1][1], dmods[1][0] = dsc, dsh

    dres = ffn_bwd(dres, x1, 0, ffn0, "ffn0")
    (dmix0,), (dg1,), _ = rowmap_bwd(f_resid, [x, mix0], [mods[0][2]], [dres], name="l0_res_b",
                                     row_grad=[False, True], row_dtypes=[BF16])
    dmods[0][2] = dg1
    dysn = matmul(dmix0, wts["hy_out"], mode="nt", out_dtype=F32, name="hy_out_dy", n=SSM_INNER, b_off=0)
    datt = matmul(dmix0, wts["hy_out"], mode="nt", out_dtype=F32, name="hy_out_da", n=D, b_off=SSM_INNER)
    g["hy_out"] = wgrad(dmix0, cat, "hy_out_dw")
    (dy, dz), (g["hy_ssm_norm_g"],), _ = rowmap_bwd(f_gated_norm, [y, z], [sp["hy_ssm_norm_g"]], [dysn], name="hy_gnorm_b",
                                                    row_grad=[True, True], row_dtypes=[F32, BF16], tr=128)
    if comm is not None:
        comm["ga"] = pack_grads(g, GA_LAYOUT, GA_ROWS)
        comm["recv"] = lax.empty((3, GA_ROWS, D), BF16)
    (dxs, ddtraw_t, g["hy_dt_bias"], g["hy_a_log"], g["hy_d_skip"], dbm, dcm), recv = ssd2_bwd(
        xs, dtraw_t, sp["hy_dt_bias"], sp["hy_a_log"], sp["hy_d_skip"], bm, cm, prev_all, dy, side=g_side(0))
    if comm is not None:
        comm["recv"] = recv
    (dxbc_pre,), _, _ = rowmap_bwd(f_silu_xbc, [xbc_pre], [], [dxs, dbm, dcm], name="hy_conv_act_b", row_grad=[True],
                                   tr=128)
    dxbc_raw, g["hy_conv_w"], g["hy_conv_b"] = conv_bwd(xbc_raw, sp["hy_conv_w"], dxbc_pre, name="hy_conv_b", cb=128, chunk_rows=128, dx_dtype=BF16)
    dol, _, _ = rowmap_bwd(f_combine, att_o + att_l, [], [datt], name="att_combine_b", row_grad=[True] * 6,
                           row_dtypes=[BF16] * 3 + [F32] * 3, tr=128)
    dqs, dks, dvs, dtabs = [], [], [], []
    for p, (win, dil) in enumerate(ATT_PATTERNS):
        qa, ka, va, bias, nb, cols = att_in[p]
        (dq, dkp_, dvp_, dbias), recv = att2_bwd(qa, ka, va, bias, regroup(dol[p], dil), regroup(dol[3 + p], dil), nb,
                                                 cols, name=f"att_bwd{p}", side=g_side(1 + p))
        if comm is not None:
            comm["recv"] = recv
        dqs.append(regroup(dq, dil, inverse=True))
        dks.append(regroup(dkp_, dil, inverse=True))
        dvs.append(regroup(dvp_, dil, inverse=True))
        dtabs.append(att_bias_grad(dbias.reshape(ATT_HEADS, 2, ATT_BLK, ATT_BLK), dil, name=f"att_dtab{p}"))
    g["rel_table"] = jnp.concatenate(dtabs, axis=1)
    dk = rowmap(f_sum3, dks, [], [BF16], name="att_dk_sum")[0]
    dv = rowmap(f_sum3, dvs, [], [BF16], name="att_dv_sum")[0]
    ddt = jnp.pad(ddtraw_t.T, ((0, 0), (0, DT_PAD - HY_DT)))
    dproj = jnp.concatenate([dz, dxbc_raw] + dqs + [dk, dv, ddt.astype(BF16)], axis=-1)
    g["hy_in_t"] = wgrad(h0, dproj, "hy_in_dw")
    if comm is None:
        dh0 = matmul(dproj, w_in, mode="nn", out_dtype=F32, name="hy_in_dx")
    else:
        gb = pack_grads(g, GB_LAYOUT, GB_ROWS)
        half = GB_ROWS // 2
        theirs = swap_halves(gb, name="swap_in_halves")
        ours = lax.dynamic_slice_in_dim(gb, lax.axis_index("c") * half, half, axis=1)
        comm["gb"] = rowmap(f_add, [ours.reshape(N_CHIPS * half, D), theirs.reshape(N_CHIPS * half, D)], [], [BF16],
                            name="sum_in_cores")[0].reshape(N_CHIPS, half, D)
        dh0, comm["recv_b"] = matmul(dproj, w_in, mode="nn", out_dtype=F32, name="hy_in_dx",
                                     side=ScatterRows(comm["gb"], lax.empty((3, half, D), BF16), 0, half))
    (dres,), (dg_, dsc, dsh), _ = rowmap_bwd(f_normmod, [x], [sp["norm_mix_g"][0], mods[0][1], mods[0][0]], [dh0],
                                             name="l0_norm_b", row_grad=[True], row_add=[dres])
    g["norm_mix_g0"] = dg_
    dmods[0][1], dmods[0][0] = dsc, dsh
    return loss_rows, dres, g, dmods


ANY = pl.BlockSpec(memory_space=pl.ANY)
WHOLE_VMEM = pl.BlockSpec(memory_space=pltpu.VMEM)


def _place():
    return lax.axis_index("x"), lax.axis_index("y"), lax.axis_index("c")


def _other_chips(x, y):
    return [(1 - x, y), (x, 1 - y), (1 - x, 1 - y)]


def allgather_small(v, *, name):
    m_per = v.shape[0]

    def body(x_ref, out_ref, send_sems, recv_sems, local_sem):
        x, y, c = _place()
        me, sibling = (x, y, c), (x, y, 1 - c)
        chips = _other_chips(x, y)

        def rows(px, py, pc):
            return out_ref.at[pl.ds((4 * px + 2 * py + pc) * m_per, m_per), :]

        def copy(k, block, to, src=None):
            return pltpu.make_async_remote_copy(
                src_ref=rows(*block) if src is None else src, dst_ref=rows(*block),
                send_sem=send_sems.at[k], recv_sem=recv_sems.at[k], device_id=to, device_id_type=MESH)

        mine = pltpu.make_async_copy(x_ref, rows(*me), local_sem)
        mine.start()
        first = [copy(0, me, sibling, src=x_ref)]
        first += [copy(1 + j, me, (*chip, c), src=x_ref) for j, chip in enumerate(chips)]
        for cp in first:
            cp.start()
        passed = [copy(4 + j, (*chip, c), sibling) for j, chip in enumerate(chips)]
        for j, chip in enumerate(chips):
            copy(1 + j, (*chip, c), me).wait_recv()
            passed[j].start()
        copy(0, sibling, me).wait_recv()
        for j, chip in enumerate(chips):
            copy(4 + j, (*chip, 1 - c), me).wait_recv()
        for cp in first + passed:
            cp.wait_send()
        mine.wait()

    return pl.pallas_call(
        body, name=name,
        out_shape=jax.ShapeDtypeStruct((N_DEV * m_per, LANES), v.dtype),
        in_specs=[WHOLE_VMEM], out_specs=WHOLE_VMEM,
        scratch_shapes=[pltpu.SemaphoreType.DMA((7,)), pltpu.SemaphoreType.DMA((7,)), pltpu.SemaphoreType.DMA],
    )(v)


def allgather_chips(pack, *, name):
    half_rows = pack.shape[0] // 2

    def body(p_ref, o_ref, send_sems, recv_sems, local_sem):
        x, y, c = _place()
        chips = _other_chips(x, y)
        sibling = (x, y, 1 - c)
        my_half = pl.ds(c * half_rows, half_rows)
        its_half = pl.ds((1 - c) * half_rows, half_rows)
        mine = pltpu.make_async_copy(p_ref, o_ref.at[2 * x + y], local_sem)
        mine.start()
        sends = [pltpu.make_async_remote_copy(
            src_ref=p_ref.at[my_half], dst_ref=o_ref.at[2 * x + y, my_half],
            send_sem=send_sems.at[k], recv_sem=recv_sems.at[k],
            device_id=(cx, cy, c), device_id_type=MESH) for k, (cx, cy) in enumerate(chips)]
        for cp in sends:
            cp.start()
        passed = []
        for k, (cx, cy) in enumerate(chips):
            landed = o_ref.at[2 * cx + cy, my_half]
            pltpu.make_async_remote_copy(
                src_ref=p_ref.at[my_half], dst_ref=landed, send_sem=send_sems.at[k], recv_sem=recv_sems.at[k],
                device_id=(cx, cy, c), device_id_type=MESH).wait_recv()
            cp = pltpu.make_async_remote_copy(
                src_ref=landed, dst_ref=landed, send_sem=send_sems.at[3 + k], recv_sem=recv_sems.at[3 + k],
                device_id=sibling, device_id_type=MESH)
            cp.start()
            passed.append(cp)
        for k, (cx, cy) in enumerate(chips):
            from_sibling = o_ref.at[2 * cx + cy, its_half]
            pltpu.make_async_remote_copy(
                src_ref=from_sibling, dst_ref=from_sibling, send_sem=send_sems.at[3 + k], recv_sem=recv_sems.at[3 + k],
                device_id=sibling, device_id_type=MESH).wait_recv()
        for cp in sends + passed:
            cp.wait_send()
        mine.wait()

    return pl.pallas_call(
        body, name=name,
        out_shape=jax.ShapeDtypeStruct((N_CHIPS,) + pack.shape, pack.dtype),
        in_specs=[ANY], out_specs=ANY,
        scratch_shapes=[pltpu.SemaphoreType.DMA((6,)), pltpu.SemaphoreType.DMA((6,)), pltpu.SemaphoreType.DMA],
    )(pack)


def swap_halves(gpack, *, name):
    half_rows = gpack.shape[1] // 2

    def body(g_ref, r_ref, send_sems, recv_sems):
        x, y, c = _place()
        its_half = pl.ds((1 - c) * half_rows, half_rows)
        copies = [pltpu.make_async_remote_copy(
            src_ref=g_ref.at[s, its_half], dst_ref=r_ref.at[s], send_sem=send_sems.at[s], recv_sem=recv_sems.at[s],
            device_id=(x, y, 1 - c), device_id_type=MESH) for s in range(N_CHIPS)]
        for cp in copies:
            cp.start()
        for cp in copies:
            cp.wait()

    return pl.pallas_call(
        body, name=name,
        out_shape=jax.ShapeDtypeStruct((N_CHIPS, half_rows) + gpack.shape[2:], gpack.dtype),
        in_specs=[ANY], out_specs=ANY,
        scratch_shapes=[pltpu.SemaphoreType.DMA((N_CHIPS,)), pltpu.SemaphoreType.DMA((N_CHIPS,))],
    )(gpack)


def scatter_chips(gpack, *, name):
    def body(g_ref, own_ref, recv_ref, send_sems, recv_sems, local_sem):
        x, y, c = _place()
        chips = _other_chips(x, y)
        mine = pltpu.make_async_copy(g_ref.at[2 * x + y], own_ref, local_sem)
        mine.start()
        sends = [pltpu.make_async_remote_copy(
            src_ref=g_ref.at[2 * cx + cy], dst_ref=recv_ref.at[k], send_sem=send_sems.at[k], recv_sem=recv_sems.at[k],
            device_id=(cx, cy, c), device_id_type=MESH) for k, (cx, cy) in enumerate(chips)]
        for cp in sends:
            cp.start()
        for cp in sends:
            cp.wait_recv()
        for cp in sends:
            cp.wait_send()
        mine.wait()

    slot = jax.ShapeDtypeStruct(gpack.shape[1:], gpack.dtype)
    return pl.pallas_call(
        body, name=name,
        out_shape=[slot, jax.ShapeDtypeStruct((3,) + gpack.shape[1:], gpack.dtype)],
        in_specs=[ANY], out_specs=[ANY, ANY],
        scratch_shapes=[pltpu.SemaphoreType.DMA((3,)), pltpu.SemaphoreType.DMA((3,)), pltpu.SemaphoreType.DMA],
    )(gpack)


class GatherRows:
    def __init__(self, pack, full, lo, hi):
        assert (hi - lo) % 32 == 0 and lo % 16 == 0
        self.src, self.dst, self.lo, self.hi = pack, full, lo, hi

    def sems(self):
        return [pltpu.SemaphoreType.DMA((6,)), pltpu.SemaphoreType.DMA((6,)), pltpu.SemaphoreType.DMA]

    def _parts(self, pack_ref, full_ref, sems):
        send_sems, recv_sems, local_sem = sems
        x, y, c = _place()
        half = (self.hi - self.lo) // 2
        mine, its = pl.ds(self.lo + c * half, half), pl.ds(self.lo + (1 - c) * half, half)
        rows = pl.ds(self.lo, self.hi - self.lo)
        local = pltpu.make_async_copy(pack_ref.at[rows], full_ref.at[2 * x + y, rows], local_sem)
        chips = _other_chips(x, y)

        def remote(src, dst, k, to):
            return pltpu.make_async_remote_copy(src_ref=src, dst_ref=dst, send_sem=send_sems.at[k],
                                                recv_sem=recv_sems.at[k], device_id=to, device_id_type=MESH)

        sends = [remote(pack_ref.at[mine], full_ref.at[2 * x + y, mine], k, (cx, cy, c)) for k, (cx, cy) in enumerate(chips)]
        landed = [full_ref.at[2 * cx + cy, mine] for cx, cy in chips]
        arrive = [remote(pack_ref.at[mine], landed[k], k, (cx, cy, c)) for k, (cx, cy) in enumerate(chips)]
        passed = [remote(landed[k], landed[k], 3 + k, (x, y, 1 - c)) for k in range(3)]
        from_sibling = [remote(landed[k], full_ref.at[2 * cx + cy, its], 3 + k, (x, y, 1 - c))
                        for k, (cx, cy) in enumerate(chips)]
        return local, sends, arrive, passed, from_sibling

    def start(self, pack_ref, full_ref, sems):
        local, sends, _, _, _ = self._parts(pack_ref, full_ref, sems)
        local.start()
        for cp in sends:
            cp.start()

    def finish(self, pack_ref, full_ref, sems):
        local, sends, arrive, passed, from_sibling = self._parts(pack_ref, full_ref, sems)
        for k in range(3):
            arrive[k].wait_recv()
            passed[k].start()
        for cp in from_sibling:
            cp.wait_recv()
        for cp in sends + passed:
            cp.wait_send()
        local.wait()


class ScatterRows:
    def __init__(self, gpack, recv, lo, hi):
        assert lo % 16 == 0 and hi % 16 == 0
        self.src, self.dst, self.lo, self.hi = gpack, recv, lo, hi

    def sems(self):
        return [pltpu.SemaphoreType.DMA((3,)), pltpu.SemaphoreType.DMA((3,))]

    def _parts(self, g_ref, recv_ref, sems):
        send_sems, recv_sems = sems
        x, y, c = _place()
        rows = pl.ds(self.lo, self.hi - self.lo)
        return [pltpu.make_async_remote_copy(
            src_ref=g_ref.at[2 * cx + cy, rows], dst_ref=recv_ref.at[k, rows], send_sem=send_sems.at[k],
            recv_sem=recv_sems.at[k], device_id=(cx, cy, c), device_id_type=MESH)
            for k, (cx, cy) in enumerate(_other_chips(x, y))]

    def start(self, g_ref, recv_ref, sems):
        for cp in self._parts(g_ref, recv_ref, sems):
            cp.start()

    def finish(self, g_ref, recv_ref, sems):
        sends = self._parts(g_ref, recv_ref, sems)
        for cp in sends:
            cp.wait_recv()
        for cp in sends:
            cp.wait_send()


def side_call(side, *, name):
    def body(src_ref, dst_in_ref, dst_ref, *sems):
        side.start(src_ref, dst_ref, sems)
        side.finish(src_ref, dst_ref, sems)

    return pl.pallas_call(
        body, name=name, out_shape=jax.ShapeDtypeStruct(side.dst.shape, side.dst.dtype),
        in_specs=[ANY, ANY], out_specs=ANY, scratch_shapes=side.sems(), input_output_aliases={1: 0},
    )(side.src, side.dst)


def grid_call(body, args, *, name, out_shape, grid, in_specs, out_specs, scratch_shapes, semantics, side=None):
    if side is None:
        res = pl.pallas_call(body, name=name, out_shape=out_shape, grid=grid, in_specs=in_specs, out_specs=out_specs,
                             scratch_shapes=scratch_shapes, compiler_params=_cparams(*semantics))(*args)
        return res, None
    n_in, n_out, n_scr = len(args), len(out_shape), len(scratch_shapes)

    def wrapped(*refs):
        ins, (src_ref, _) = refs[:n_in], refs[n_in:n_in + 2]
        outs, dst_ref = refs[n_in + 2:n_in + 2 + n_out], refs[n_in + 2 + n_out]
        scr, sems = refs[n_in + 3 + n_out:n_in + 3 + n_out + n_scr], refs[n_in + 3 + n_out + n_scr:]
        first = functools.reduce(jnp.logical_and, [pl.program_id(i) == 0 for i in range(len(grid))])
        last = functools.reduce(jnp.logical_and, [pl.program_id(i) == n - 1 for i, n in enumerate(grid)])

        @pl.when(first)
        def _():
            side.start(src_ref, dst_ref, sems)

        body(*ins, *outs, *scr)

        @pl.when(last)
        def _():
            side.finish(src_ref, dst_ref, sems)

    res = pl.pallas_call(
        wrapped, name=name,
        out_shape=list(out_shape) + [jax.ShapeDtypeStruct(side.dst.shape, side.dst.dtype)],
        grid=grid, in_specs=list(in_specs) + [ANY, ANY], out_specs=list(out_specs) + [ANY],
        scratch_shapes=list(scratch_shapes) + side.sems(), input_output_aliases={n_in + 1: n_out},
        compiler_params=_cparams(*(["arbitrary"] * len(grid))),
    )(*args, side.src, side.dst)
    return res[:-1], res[-1]


def sibling_swap(p, *, name):
    def body(p_ref, r_ref, send_sem, recv_sem):
        x, y, c = _place()
        cp = pltpu.make_async_remote_copy(src_ref=p_ref, dst_ref=r_ref, send_sem=send_sem, recv_sem=recv_sem,
                                          device_id=(x, y, 1 - c), device_id_type=MESH)
        cp.start()
        cp.wait()

    return pl.pallas_call(
        body, name=name, out_shape=jax.ShapeDtypeStruct(p.shape, p.dtype),
        in_specs=[ANY], out_specs=ANY,
        scratch_shapes=[pltpu.SemaphoreType.DMA, pltpu.SemaphoreType.DMA],
    )(p)


def sum_slots(own, recv, *, name):
    r_dim, c_dim = own.shape
    tr = _pick(r_dim, 256, mult=16)

    def body(o_ref, r_ref, out_ref):
        acc = o_ref[...].astype(F32)
        for k in range(3):
            acc = acc + r_ref[k].astype(F32)
        out_ref[...] = acc

    return pl.pallas_call(
        body, name=name, out_shape=jax.ShapeDtypeStruct((r_dim, c_dim), F32), grid=(r_dim // tr,),
        in_specs=[pl.BlockSpec((tr, c_dim), lambda i: (i, 0)), pl.BlockSpec((3, tr, c_dim), lambda i: (0, i, 0))],
        out_specs=pl.BlockSpec((tr, c_dim), lambda i: (i, 0)),
        compiler_params=_cparams("parallel"),
    )(own, recv)


def sum_devices(v_all, *, name):
    m_per = v_all.shape[0] // N_DEV

    def body(v_ref, o_ref):
        acc = v_ref[pl.ds(0, m_per), :]
        for d in range(1, N_DEV):
            acc = acc + v_ref[pl.ds(d * m_per, m_per), :]
        o_ref[...] = acc

    return pl.pallas_call(
        body, name=name, out_shape=jax.ShapeDtypeStruct((m_per, LANES), F32),
        in_specs=[WHOLE_VMEM], out_specs=WHOLE_VMEM,
    )(v_all)


WEIGHTS = ['ada_w', 'ada_b', 'norm_mix_g', 'norm_ffn_g', 'hy_w_in', 'hy_conv_w', 'hy_conv_b', 'hy_dt_bias', 'hy_a_log',
           'hy_d_skip', 'hy_ssm_norm_g', 'hy_w_out', 'rel_table', 'cv_w_pw1', 'cv_b_pw1', 'cv_w_dw', 'cv_b_dw', 'cv_ln_g',
           'cv_ln_b', 'cv_w_pw2', 'cv_b_pw2', 'ffn_w_gate', 'ffn_w_up', 'ffn_w_down', 'final_norm_g']
BIG = ('ada_w', 'hy_w_in', 'hy_w_out', 'cv_w_pw1', 'cv_w_pw2', 'ffn_w_gate', 'ffn_w_up', 'ffn_w_down')
SMALL_SHARDED = {'hy_conv_w': (1, 4, 3072), 'cv_b_pw1': (1, 2048), 'cv_w_dw': (1, 31, 1024), 'cv_b_dw': (1, 1024),
                 'cv_ln_g': (1, 1024), 'cv_ln_b': (1, 1024), 'cv_b_pw2': (1, 1024)}
SMALL_GRADS = {'ada_b': (2, 6144), 'norm_mix_g': (2, 1024), 'norm_ffn_g': (2, 1024), 'hy_conv_w': (1, 4, 3072),
               'hy_conv_b': (1, 3072), 'hy_dt_bias': (1, 32), 'hy_a_log': (1, 32), 'hy_d_skip': (1, 32),
               'hy_ssm_norm_g': (1, 2048), 'rel_table': (32, 48), 'cv_b_pw1': (1, 2048), 'cv_w_dw': (1, 31, 1024),
               'cv_b_dw': (1, 1024), 'cv_ln_g': (1, 1024), 'cv_ln_b': (1, 1024), 'cv_b_pw2': (1, 1024),
               'final_norm_g': (1024,), 'loss': (1,)}

PACK_LAYOUT = (('hy_in_t', 2568), ('hy_out', 768), ('pw1_t', 512), ('pw2', 256),
               ('gate_t0', 704), ('up_t0', 704), ('down0', 704), ('gate_t1', 704), ('up_t1', 704), ('down1', 704))
PACK_ROWS = 8448


def _pack_offsets(layout):
    off, out = 0, {}
    for nm, r in layout:
        out[nm] = (off, r)
        off += r
    return out


PACK_OFF = _pack_offsets(PACK_LAYOUT)
W_BATCHES = ((0, 2624), (2624, 5248), (5248, 6336), (6336, 7424), (7424, 8448))
GA_LAYOUT = PACK_LAYOUT[1:]
GA_ROWS = 5888
GA_OFF = _pack_offsets(GA_LAYOUT)
G_BATCHES = ((0, 2560), (2560, 3712), (3712, 4864), (4864, 5888))
GB_LAYOUT = PACK_LAYOUT[:1]
GB_ROWS = 2816


def pack_grads(g, layout, n_rows):
    def rows_bf16(nm):
        return g[nm]

    parts = []
    for key, r in layout:
        if key == 'hy_in_t':
            a = hy_from_cat(rows_bf16('hy_in_t'))
        elif key.startswith('gate_t'):
            a = rows_bf16('gu_t' + key[-1])[:FFN_HIDDEN]
        elif key.startswith('up_t'):
            a = rows_bf16('gu_t' + key[-1])[FFN_HIDDEN:]
        else:
            a = rows_bf16(key)
        parts.append(a.reshape(N_CHIPS, r, D))
    used = sum(r for _, r in layout)
    return jnp.concatenate(parts + [jnp.zeros((N_CHIPS, n_rows - used, D), BF16)], axis=1)


def unpack_weights(full, skip=()):
    def whole(nm):
        o, r = PACK_OFF[nm]
        return full[:, o:o + r].reshape(N_CHIPS * r, D)

    out = {"hy_out": whole('hy_out'), "pw1_t": whole('pw1_t'), "pw2": whole('pw2'),
           "gu_t": [jnp.concatenate([whole(f'gate_t{i}'), whole(f'up_t{i}')], axis=0) for i in range(2)],
           "down": [whole(f'down{i}') for i in range(2)]}
    if "hy_in_t" not in skip:
        out["hy_in_t"] = hy_to_cat(whole('hy_in_t'))
    return out


def _to_lanes(flat):
    n = flat.shape[0]
    m = -(-n // (8 * LANES)) * 8
    return jnp.pad(flat, (0, m * LANES - n)).reshape(m, LANES)


def _split(flat, shapes):
    out, off = {}, 0
    for nm, shp in shapes.items():
        n = int(np.prod(shp))
        out[nm] = flat[off:off + n].reshape(shp)
        off += n
    return out


def kernel(x, c, ada_w, ada_b, norm_mix_g, norm_ffn_g, hy_w_in, hy_conv_w, hy_conv_b, hy_dt_bias, hy_a_log, hy_d_skip, hy_ssm_norm_g, hy_w_out, rel_table, cv_w_pw1, cv_b_pw1, cv_w_dw, cv_b_dw, cv_ln_g, cv_ln_b, cv_w_pw2, cv_b_pw2, ffn_w_gate, ffn_w_up, ffn_w_down, final_norm_g, loss_target, m_ada_w, m_ada_b, m_norm_mix_g, m_norm_ffn_g, m_hy_w_in, m_hy_conv_w, m_hy_conv_b, m_hy_dt_bias, m_hy_a_log, m_hy_d_skip, m_hy_ssm_norm_g, m_hy_w_out, m_rel_table, m_cv_w_pw1, m_cv_b_pw1, m_cv_w_dw, m_cv_b_dw, m_cv_ln_g, m_cv_ln_b, m_cv_w_pw2, m_cv_b_pw2, m_ffn_w_gate, m_ffn_w_up, m_ffn_w_down, m_final_norm_g, v_ada_w, v_ada_b, v_norm_mix_g, v_norm_ffn_g, v_hy_w_in, v_hy_conv_w, v_hy_conv_b, v_hy_dt_bias, v_hy_a_log, v_hy_d_skip, v_hy_ssm_norm_g, v_hy_w_out, v_rel_table, v_cv_w_pw1, v_cv_b_pw1, v_cv_w_dw, v_cv_b_dw, v_cv_ln_g, v_cv_ln_b, v_cv_w_pw2, v_cv_b_pw2, v_ffn_w_gate, v_ffn_w_up, v_ffn_w_down, v_final_norm_g):
    args = (x, c, ada_w, ada_b, norm_mix_g, norm_ffn_g, hy_w_in, hy_conv_w, hy_conv_b, hy_dt_bias, hy_a_log, hy_d_skip, hy_ssm_norm_g, hy_w_out, rel_table, cv_w_pw1, cv_b_pw1, cv_w_dw, cv_b_dw, cv_ln_g, cv_ln_b, cv_w_pw2, cv_b_pw2, ffn_w_gate, ffn_w_up, ffn_w_down, final_norm_g, loss_target, m_ada_w, m_ada_b, m_norm_mix_g, m_norm_ffn_g, m_hy_w_in, m_hy_conv_w, m_hy_conv_b, m_hy_dt_bias, m_hy_a_log, m_hy_d_skip, m_hy_ssm_norm_g, m_hy_w_out, m_rel_table, m_cv_w_pw1, m_cv_b_pw1, m_cv_w_dw, m_cv_b_dw, m_cv_ln_g, m_cv_ln_b, m_cv_w_pw2, m_cv_b_pw2, m_ffn_w_gate, m_ffn_w_up, m_ffn_w_down, m_final_norm_g, v_ada_w, v_ada_b, v_norm_mix_g, v_norm_ffn_g, v_hy_w_in, v_hy_conv_w, v_hy_conv_b, v_hy_dt_bias, v_hy_a_log, v_hy_d_skip, v_hy_ssm_norm_g, v_hy_w_out, v_rel_table, v_cv_w_pw1, v_cv_b_pw1, v_cv_w_dw, v_cv_b_dw, v_cv_ln_g, v_cv_ln_b, v_cv_w_pw2, v_cv_b_pw2, v_ffn_w_gate, v_ffn_w_up, v_ffn_w_down, v_final_norm_g)
    x_in, c_in = args[0], args[1]
    w = dict(zip(WEIGHTS, args[2:27], strict=True))
    tgt = args[27]
    m_in = dict(zip(WEIGHTS, args[28:53], strict=True))
    v_in = dict(zip(WEIGHTS, args[53:78], strict=True))
    xi, yi, ci = _place()
    chip = 2 * xi + yi
    dev = 2 * chip + ci

    cs = rowmap(f_silu, [c_in.reshape(8, LANES)], [], [F32], name="cond_silu", tr=8)[0]
    cs_all = allgather_small(cs, name="gather_cond").reshape(N_DEV, D)
    cs16 = jnp.pad(cs_all, ((0, 8), (0, 0)))
    modpart = jnp.stack([matmul(cs16, w['ada_w'][i], mode="nn", out_dtype=F32, name=f"ada_fwd{i}")[:N_DEV]
                         for i in range(2)], axis=1)
    shard_names = list(SMALL_SHARDED)
    payload = jnp.concatenate([modpart.reshape(-1)] + [w[nm].reshape(-1) for nm in shard_names])
    got = allgather_small(_to_lanes(payload), name="gather_mod").reshape(N_DEV, -1)[0::2]
    modparts = got[:, :modpart.size].reshape(N_CHIPS, N_DEV, 2, 1536)
    mine = lax.dynamic_index_in_dim(modparts, dev, axis=1, keepdims=False)
    mod = jnp.transpose(mine, (1, 0, 2)).reshape(2, 6 * D) + w['ada_b']
    mods = [[mod[i, j * D:(j + 1) * D].reshape(1, D) for j in range(6)] for i in range(2)]
    sp = {}
    off = modpart.size
    for nm in shard_names:
        shp = w[nm].shape
        n = int(np.prod(shp))
        parts = got[:, off:off + n].reshape((N_CHIPS,) + shp)
        sp[nm] = jnp.concatenate([parts[s] for s in range(N_CHIPS)], axis=-1)
        off += n

    def rows_of(nm, i=None):
        a = w[nm][0 if i is None else i]
        return (a.T if nm in ('hy_w_in', 'cv_w_pw1', 'ffn_w_gate', 'ffn_w_up') else a).astype(BF16)

    pieces = [rows_of('hy_w_in'), rows_of('hy_w_out'), rows_of('cv_w_pw1'), rows_of('cv_w_pw2')]
    for i in range(2):
        pieces += [rows_of('ffn_w_gate', i), rows_of('ffn_w_up', i), rows_of('ffn_w_down', i)]
    n_rows = sum(p.shape[0] for p in pieces)
    pack = jnp.concatenate(pieces + [jnp.zeros((PACK_ROWS - n_rows, D), BF16)], axis=0)
    full = side_call(GatherRows(pack, lax.empty((N_CHIPS, PACK_ROWS, D), BF16), *W_BATCHES[0]), name="gather_weights")
    o_in, r_in = PACK_OFF['hy_in_t']
    wts = {"hy_in_t": hy_to_cat(full[:, o_in:o_in + r_in].reshape(N_CHIPS * r_in, D))}
    comm = {"pack": pack, "full": full}

    sp = {"norm_mix_g": [w['norm_mix_g'][i].reshape(1, D) for i in range(2)],
          "norm_ffn_g": [w['norm_ffn_g'][i].reshape(1, D) for i in range(2)],
          "hy_conv_w": sp['hy_conv_w'][0], "hy_conv_b": w['hy_conv_b'],
          "hy_dt_bias": w['hy_dt_bias'].reshape(SSM_HEADS, 1), "hy_a_log": w['hy_a_log'].reshape(SSM_HEADS, 1),
          "hy_d_skip": w['hy_d_skip'].reshape(SSM_HEADS, 1), "hy_ssm_norm_g": w['hy_ssm_norm_g'],
          "rel_table": w['rel_table'], "cv_b_pw1": sp['cv_b_pw1'], "cv_w_dw": sp['cv_w_dw'][0], "cv_b_dw": sp['cv_b_dw'],
          "cv_ln_g": sp['cv_ln_g'], "cv_ln_b": sp['cv_ln_b'], "cv_b_pw2": sp['cv_b_pw2'],
          "final_norm_g": w['final_norm_g'].reshape(1, D)}

    loss_rows, grad_x, g, dmods = device_step(x_in[0], tgt[0], mods, wts, sp, comm)

    dmod = jnp.stack([jnp.concatenate([d.reshape(-1) for d in dmods[i]]) for i in range(2)])
    small = {'ada_b': dmod, 'norm_mix_g': jnp.stack([g[f'norm_mix_g{i}'].reshape(-1) for i in range(2)]),
             'norm_ffn_g': jnp.stack([g[f'norm_ffn_g{i}'].reshape(-1) for i in range(2)]),
             'loss': jnp.sum(loss_rows).reshape(1)}
    for nm in SMALL_GRADS:
        if nm not in small:
            small[nm] = g[nm]
    vec = _to_lanes(jnp.concatenate([small[nm].reshape(-1) for nm in SMALL_GRADS]))
    vec_all = allgather_small(vec, name="gather_small_grads")
    tot = _split(sum_devices(vec_all, name="sum_small_grads").reshape(-1), SMALL_GRADS)
    dmod_all = vec_all.reshape(N_DEV, -1)[:, :2 * 6 * D].reshape(N_DEV, 2, 6 * D)

    recv = comm["recv"]
    own_a = lax.dynamic_index_in_dim(comm["ga"], chip, axis=0, keepdims=False)
    part_a = sum_slots(own_a, recv, name="sum_chip_grads")
    red_a = rowmap(f_add, [part_a, sibling_swap(part_a, name="swap_grads")], [], [F32], name="sum_core_grads")[0]
    recv_b = comm["recv_b"]
    own_b = lax.dynamic_index_in_dim(comm["gb"], chip, axis=0, keepdims=False)
    mine_half = sum_slots(own_b, recv_b, name="sum_in_chips")
    its_half = sibling_swap(mine_half, name="swap_in")
    red_b = jnp.concatenate([jnp.where(ci == 0, mine_half, its_half), jnp.where(ci == 0, its_half, mine_half)], axis=0)

    def shard_grad(nm, i=None):
        key = {'hy_w_in': 'hy_in_t', 'hy_w_out': 'hy_out', 'cv_w_pw1': 'pw1_t', 'cv_w_pw2': 'pw2'}.get(nm)
        if key is None:
            key = {'ffn_w_gate': 'gate_t', 'ffn_w_up': 'up_t', 'ffn_w_down': 'down'}[nm] + str(i)
        if key == 'hy_in_t':
            a = red_b[:PACK_OFF[key][1]]
        else:
            o, r = GA_OFF[key]
            a = red_a[o:o + r]
        return a.T if key.endswith('_t') or key[:-1].endswith('_t') else a

    grads = {}
    grads['hy_w_in'] = shard_grad('hy_w_in')[None]
    grads['hy_w_out'] = shard_grad('hy_w_out')[None]
    grads['cv_w_pw1'] = shard_grad('cv_w_pw1')[None]
    grads['cv_w_pw2'] = shard_grad('cv_w_pw2')[None]
    for nm in ('ffn_w_gate', 'ffn_w_up', 'ffn_w_down'):
        grads[nm] = jnp.stack([shard_grad(nm, i) for i in range(2)])
    cs16 = jnp.pad(cs_all, ((0, 8), (0, 0)))
    dm_mine = lax.dynamic_slice_in_dim(dmod_all, chip * 1536, 1536, axis=2)
    dm16 = jnp.pad(dm_mine, ((0, 8), (0, 0), (0, 0)))
    grads['ada_w'] = jnp.stack([matmul(cs16, dm16[:, i], mode="tn", out_dtype=F32, name=f"ada_dw{i}") for i in range(2)])
    for nm, shp in SMALL_GRADS.items():
        if nm == 'loss':
            continue
        if nm in SMALL_SHARDED:
            n = w[nm].shape[-1]
            grads[nm] = lax.dynamic_slice_in_dim(tot[nm], chip * n, n, axis=len(shp) - 1)
        else:
            grads[nm] = tot[nm].reshape(w[nm].shape)

    delta, new_m, new_v = {}, {}, {}
    for nm in BIG:
        delta[nm], new_m[nm], new_v[nm] = adamw(w[nm], grads[nm], m_in[nm], v_in[nm], name="adamw_" + nm)
    smalls = [nm for nm in WEIGHTS if nm not in BIG]
    packed = [_to_lanes(jnp.concatenate([d[nm].reshape(-1) for nm in smalls])) for d in (w, grads, m_in, v_in)]
    res = rowmap(f_adamw, packed, [], [F32] * 3, name="adamw_small", tr=_rows_tile(packed[0].shape[0]))
    for d, r in zip((delta, new_m, new_v), res, strict=True):
        d.update(_split(r.reshape(-1), {nm: w[nm].shape for nm in smalls}))

    loss = tot['loss'].reshape(())
    return (loss, grad_x[None], *[grads[nm] for nm in WEIGHTS], *[delta[nm] for nm in WEIGHTS],
            *[new_m[nm] for nm in WEIGHTS], *[new_v[nm] for nm in WEIGHTS])
```

```python
import functools
import math

import jax
import jax.numpy as jnp
import numpy as np
from jax import lax
from jax.experimental import pallas as pl
from jax.experimental.pallas import tpu as pltpu

F32 = jnp.float32
BF16 = jnp.bfloat16
MESH = pl.DeviceIdType.MESH

D = 1024
S = 4096
EPS = 1e-6
SSM_INNER = 2048
SSM_HEADS = 32
SSM_HDIM = 64
SSM_GROUPS = 4
SSM_STATE = 128
SSM_CONVK = 4
SSM_CONV_DIM = 3072
CHUNK = 128
N_CHUNKS = S // CHUNK
ATT_HEADS = 16
ATT_HDIM = 64
ATT_PATTERNS = ((128, 1), (512, 4), (2048, 16))
ATT_BLK = 128
REL_BUCKETS = 32
REL_MAX_DIST = 2048
CONV_WIDTH = 31
FFN_HIDDEN = 2816
N_CHIPS = 4
N_DEV = 8
ADAM_LR, ADAM_B1, ADAM_B2, ADAM_EPS, ADAM_WD, ADAM_STEP = 0.001, 0.9, 0.999, 1e-08, 0.01, 10

VMEM_LIMIT_BYTES = 56 * 1024 * 1024
LANES = 128


def _cparams(*sem):
    return pltpu.CompilerParams(dimension_semantics=sem, vmem_limit_bytes=VMEM_LIMIT_BYTES)


def _pick(n, cap, mult=LANES):
    best = None
    for t in range(mult, min(n, cap) + 1, mult):
        if n % t == 0:
            best = t
    return best or n


def _dot(a, b, ca, cb):
    return lax.dot_general(a.astype(BF16), b.astype(BF16), (((ca,), (cb,)), ((), ())), preferred_element_type=F32)


@jax.custom_vjp
def mm(a, b):
    return _dot(a, b, 1, 0)


def _mm_fwd(a, b):
    return _dot(a, b, 1, 0), (a, b)


def _mm_bwd(res, g):
    a, b = res
    return _dot(g, b, 1, 1).astype(a.dtype), _dot(a, g, 0, 0).astype(b.dtype)


mm.defvjp(_mm_fwd, _mm_bwd)


@jax.custom_vjp
def mm_nt(a, b):
    return _dot(a, b, 1, 1)


def _mm_nt_fwd(a, b):
    return _dot(a, b, 1, 1), (a, b)


def _mm_nt_bwd(res, g):
    a, b = res
    return _dot(g, b, 1, 0).astype(a.dtype), _dot(g, a, 0, 0).astype(b.dtype)


mm_nt.defvjp(_mm_nt_fwd, _mm_nt_bwd)


@jax.custom_vjp
def mm_tn(a, b):
    return _dot(a, b, 0, 0)


def _mm_tn_fwd(a, b):
    return _dot(a, b, 0, 0), (a, b)


def _mm_tn_bwd(res, g):
    a, b = res
    return _dot(b, g, 1, 1).astype(a.dtype), _dot(a, g, 1, 0).astype(b.dtype)


mm_tn.defvjp(_mm_tn_fwd, _mm_tn_bwd)


def matmul(a, b, *, mode, out_dtype, name, n=None, b_off=0, tm_cap=1024, tn_cap=512, tk_cap=3584, side=None,
           out_t=False):
    if mode == "tn":
        k_dim, m_dim = a.shape
    else:
        m_dim, k_dim = a.shape
    n_dim = n if n is not None else (b.shape[0] if mode == "nt" else b.shape[1])
    tm = m_dim if m_dim < LANES else _pick(m_dim, tm_cap)
    tn = _pick(n_dim, tn_cap)
    tk = k_dim if k_dim < LANES else _pick(k_dim, tk_cap)
    assert m_dim % tm == 0 and n_dim % tn == 0 and k_dim % tk == 0 and b_off % tn == 0
    nk = k_dim // tk
    off = b_off // tn
    if mode == "nn":
        a_spec = pl.BlockSpec((tm, tk), lambda i, j, k: (i, k))
        b_spec = pl.BlockSpec((tk, tn), lambda i, j, k: (k, j))
        ca, cb = 1, 0
    elif mode == "nt":
        a_spec = pl.BlockSpec((tm, tk), lambda i, j, k: (i, k))
        b_spec = pl.BlockSpec((tn, tk), lambda i, j, k: (j + off, k))
        ca, cb = 1, 1
    else:
        a_spec = pl.BlockSpec((tk, tm), lambda i, j, k: (k, i))
        b_spec = pl.BlockSpec((tk, tn), lambda i, j, k: (k, j))
        ca, cb = 0, 0

    def emit(o_ref, val):
        o_ref[...] = (val.T if out_t else val).astype(o_ref.dtype)

    def body(a_ref, b_ref, o_ref, acc_ref):
        part = _dot(a_ref[...], b_ref[...], ca, cb)
        if nk == 1:
            emit(o_ref, part)
        else:
            k = pl.program_id(2)

            @pl.when(k == 0)
            def _():
                acc_ref[...] = part

            @pl.when(k > 0)
            def _():
                acc_ref[...] += part

            @pl.when(k == nk - 1)
            def _():
                emit(o_ref, acc_ref[...])

    if out_t:
        out_shape, out_spec = (n_dim, m_dim), pl.BlockSpec((tn, tm), lambda i, j, k: (j, i))
    else:
        out_shape, out_spec = (m_dim, n_dim), pl.BlockSpec((tm, tn), lambda i, j, k: (i, j))
    (out,), side_dst = grid_call(
        body, (a, b), name=name,
        out_shape=[jax.ShapeDtypeStruct(out_shape, out_dtype)],
        grid=(m_dim // tm, n_dim // tn, nk),
        in_specs=[a_spec, b_spec],
        out_specs=[out_spec],
        scratch_shapes=[pltpu.VMEM((tm, tn), F32)],
        semantics=("parallel", "parallel", "arbitrary"), side=side)
    return out if side is None else (out, side_dst)


def _f32(xs):
    return [x.astype(F32) for x in xs]


def rowmap(f, rows, consts, out_dtypes, *, name, tr=256):
    r_dim = rows[0].shape[0]
    tr = _pick(r_dim, tr, mult=8)
    assert r_dim % tr == 0
    nr, nc = len(rows), len(consts)
    outs = jax.eval_shape(lambda *xs: f(*xs), *[jax.ShapeDtypeStruct((tr, x.shape[1]), F32) for x in rows],
                          *[jax.ShapeDtypeStruct(x.shape, F32) for x in consts])

    def body(*refs):
        res = f(*_f32([r[...] for r in refs[:nr + nc]]))
        for o_ref, o in zip(refs[nr + nc:], res, strict=True):
            o_ref[...] = o.astype(o_ref.dtype)

    return pl.pallas_call(
        body, name=name,
        out_shape=[jax.ShapeDtypeStruct((r_dim, o.shape[1]), dt) for o, dt in zip(outs, out_dtypes, strict=True)],
        grid=(r_dim // tr,),
        in_specs=[pl.BlockSpec((tr, x.shape[1]), lambda i: (i, 0)) for x in rows]
        + [pl.BlockSpec(x.shape, lambda i: (0, 0)) for x in consts],
        out_specs=[pl.BlockSpec((tr, o.shape[1]), lambda i: (i, 0)) for o in outs],
        compiler_params=_cparams("parallel"),
    )(*rows, *consts)


def rowmap_bwd(f, rows, consts, cts, *, name, row_grad, row_dtypes=None, tr=256, emit=(), row_add=None):
    r_dim = rows[0].shape[0]
    tr = _pick(r_dim, tr, mult=8)
    assert r_dim % tr == 0
    nr, nc, nct = len(rows), len(consts), len(cts)
    gi = [i for i, flag in enumerate(row_grad) if flag]
    row_dtypes = row_dtypes or [F32] * len(gi)
    row_add = row_add or [None] * len(gi)
    adds = [a for a in row_add if a is not None]
    outs = jax.eval_shape(lambda *xs: f(*xs), *[jax.ShapeDtypeStruct((tr, x.shape[1]), F32) for x in rows],
                          *[jax.ShapeDtypeStruct(x.shape, F32) for x in consts])

    def body(*refs):
        ins = _f32([r[...] for r in refs[:nr + nc]])
        ct = _f32([r[...] for r in refs[nr + nc:nr + nc + nct]])
        add_refs = list(refs[nr + nc + nct:nr + nc + nct + len(adds)])
        o_refs = refs[nr + nc + nct + len(adds):]
        res, vjp = jax.vjp(f, *ins)
        grads = vjp(tuple(ct))
        for o_ref, i, a in zip(o_refs[:len(gi)], gi, row_add):
            g = grads[i] if a is None else grads[i] + add_refs.pop(0)[...].astype(F32)
            o_ref[...] = g.astype(o_ref.dtype)
        first = pl.program_id(0) == 0
        for o_ref, g in zip(o_refs[len(gi):len(gi) + nc], grads[nr:]):
            @pl.when(first)
            def _(o_ref=o_ref, g=g):
                o_ref[...] = g

            @pl.when(jnp.logical_not(first))
            def _(o_ref=o_ref, g=g):
                o_ref[...] += g
        for o_ref, i in zip(o_refs[len(gi) + nc:], emit):
            o_ref[...] = res[i].astype(o_ref.dtype)

    out_shape = ([jax.ShapeDtypeStruct(rows[i].shape, dt) for i, dt in zip(gi, row_dtypes, strict=True)]
                 + [jax.ShapeDtypeStruct(x.shape, F32) for x in consts]
                 + [jax.ShapeDtypeStruct((r_dim, outs[i].shape[1]), F32) for i in emit])
    out_specs = ([pl.BlockSpec((tr, rows[i].shape[1]), lambda i_: (i_, 0)) for i in gi]
                 + [pl.BlockSpec(x.shape, lambda i_: (0, 0)) for x in consts]
                 + [pl.BlockSpec((tr, outs[i].shape[1]), lambda i_: (i_, 0)) for i in emit])
    res = pl.pallas_call(
        body, name=name,
        out_shape=out_shape,
        grid=(r_dim // tr,),
        in_specs=[pl.BlockSpec((tr, x.shape[1]), lambda i: (i, 0)) for x in rows]
        + [pl.BlockSpec(x.shape, lambda i: (0, 0)) for x in consts]
        + [pl.BlockSpec((tr, x.shape[1]), lambda i: (i, 0)) for x in list(cts) + adds],
        out_specs=out_specs,
        compiler_params=_cparams("arbitrary"),
    )(*rows, *consts, *cts, *adds)
    return res[:len(gi)], res[len(gi):len(gi) + nc], res[len(gi) + nc:]


def transpose(a, *, name, out_dtype=BF16, tr=512, tc=512):
    r_dim, c_dim = a.shape
    tr, tc = _pick(r_dim, tr), _pick(c_dim, tc)

    def body(a_ref, o_ref):
        o_ref[...] = a_ref[...].astype(F32).T.astype(o_ref.dtype)

    return pl.pallas_call(
        body, name=name, out_shape=jax.ShapeDtypeStruct((c_dim, r_dim), out_dtype),
        grid=(r_dim // tr, c_dim // tc),
        in_specs=[pl.BlockSpec((tr, tc), lambda i, j: (i, j))],
        out_specs=pl.BlockSpec((tc, tr), lambda i, j: (j, i)),
        compiler_params=_cparams("parallel", "parallel"),
    )(a)


CONV_HALO = 32
CONV_ROWS = 256


def conv_fwd(x, w, b, *, name, cb=256, chunk_rows=CONV_ROWS):
    s_dim, c_dim = x.shape
    taps = w.shape[0]
    assert taps - 1 <= CONV_HALO and s_dim % chunk_rows == 0 and c_dim % cb == 0
    n_chunks = s_dim // chunk_rows
    ext = chunk_rows + CONV_HALO

    def body(x_ref, w_ref, b_ref, o_ref, xp_ref):
        xp_ref[pl.ds(0, CONV_HALO), :] = jnp.zeros((CONV_HALO, cb), F32)
        xp_ref[pl.ds(CONV_HALO, s_dim), :] = x_ref[...].astype(F32)
        wv = w_ref[...].astype(F32)
        bv = b_ref[...].astype(F32)

        def chunk(t, carry):
            base = pl.multiple_of(t * chunk_rows, chunk_rows)
            xe = xp_ref[pl.ds(base, ext), :]
            acc = jnp.broadcast_to(bv, (chunk_rows, cb))
            for j in range(taps):
                sh = xe if j == 0 else pltpu.roll(xe, shift=j, axis=0)
                acc = acc + wv[taps - 1 - j:taps - j, :] * sh[CONV_HALO:, :]
            o_ref[pl.ds(base, chunk_rows), :] = acc
            return carry

        lax.fori_loop(0, n_chunks, chunk, 0)

    return pl.pallas_call(
        body, name=name,
        out_shape=jax.ShapeDtypeStruct((s_dim, c_dim), F32),
        grid=(c_dim // cb,),
        in_specs=[pl.BlockSpec((s_dim, cb), lambda i: (0, i)), pl.BlockSpec((taps, cb), lambda i: (0, i)),
                  pl.BlockSpec((1, cb), lambda i: (0, i))],
        out_specs=pl.BlockSpec((s_dim, cb), lambda i: (0, i)),
        scratch_shapes=[pltpu.VMEM((s_dim + CONV_HALO, cb), F32)],
        compiler_params=_cparams("parallel"),
    )(x, w, b)


def conv_bwd(x, w, g, *, name, cb=256, chunk_rows=CONV_ROWS, dx_dtype=F32):
    s_dim, c_dim = x.shape
    taps = w.shape[0]
    n_chunks = s_dim // chunk_rows
    ext = chunk_rows + CONV_HALO

    def rows8(a):
        return jnp.sum(a.reshape(chunk_rows // 8, 8, cb), axis=0)

    def body(x_ref, w_ref, g_ref, dx_ref, dw_ref, db_ref, xp_ref, gp_ref, acc_ref):
        xp_ref[pl.ds(0, CONV_HALO), :] = jnp.zeros((CONV_HALO, cb), F32)
        xp_ref[pl.ds(CONV_HALO, s_dim), :] = x_ref[...].astype(F32)
        gp_ref[pl.ds(0, s_dim), :] = g_ref[...].astype(F32)
        gp_ref[pl.ds(s_dim, CONV_HALO), :] = jnp.zeros((CONV_HALO, cb), F32)
        acc_ref[...] = jnp.zeros_like(acc_ref)
        wv = w_ref[...].astype(F32)

        def chunk(t, carry):
            base = pl.multiple_of(t * chunk_rows, chunk_rows)
            xe = xp_ref[pl.ds(base, ext), :]
            ge = gp_ref[pl.ds(base, ext), :]
            gc = ge[:chunk_rows, :]
            dx = jnp.zeros((chunk_rows, cb), F32)
            for j in range(taps):
                xs = xe if j == 0 else pltpu.roll(xe, shift=j, axis=0)
                gs = ge if j == 0 else pltpu.roll(ge, shift=ext - j, axis=0)
                k = taps - 1 - j
                dx = dx + wv[k:k + 1, :] * gs[:chunk_rows, :]
                acc_ref[8 * k:8 * k + 8, :] += rows8(gc * xs[CONV_HALO:, :])
            acc_ref[8 * taps:8 * taps + 8, :] += rows8(gc)
            dx_ref[pl.ds(base, chunk_rows), :] = dx.astype(dx_ref.dtype)
            return carry

        lax.fori_loop(0, n_chunks, chunk, 0)
        sums = jnp.sum(acc_ref[...].reshape(taps + 1, 8, cb), axis=1)
        dw_ref[...] = sums[0:taps, :]
        db_ref[...] = sums[taps:taps + 1, :]

    return pl.pallas_call(
        body, name=name,
        out_shape=[jax.ShapeDtypeStruct((s_dim, c_dim), dx_dtype), jax.ShapeDtypeStruct((taps, c_dim), F32),
                   jax.ShapeDtypeStruct((1, c_dim), F32)],
        grid=(c_dim // cb,),
        in_specs=[pl.BlockSpec((s_dim, cb), lambda i: (0, i)), pl.BlockSpec((taps, cb), lambda i: (0, i)),
                  pl.BlockSpec((s_dim, cb), lambda i: (0, i))],
        out_specs=[pl.BlockSpec((s_dim, cb), lambda i: (0, i)), pl.BlockSpec((taps, cb), lambda i: (0, i)),
                   pl.BlockSpec((1, cb), lambda i: (0, i))],
        scratch_shapes=[pltpu.VMEM((s_dim + CONV_HALO, cb), F32), pltpu.VMEM((s_dim + CONV_HALO, cb), F32),
                        pltpu.VMEM((8 * (taps + 1), cb), F32)],
        compiler_params=_cparams("parallel"),
    )(x, w, g)


def _iota2(n, axis):
    return lax.broadcasted_iota(jnp.int32, (n, n), axis)


def _to_col(row):
    n = row.shape[1]
    return jnp.sum(jnp.where(_iota2(n, 0) == _iota2(n, 1), jnp.broadcast_to(row, (n, n)), 0.0), axis=1, keepdims=True)


def _softplus(x):
    return jnp.maximum(x, 0.0) + jnp.log(1.0 + jnp.exp(-jnp.abs(x)))


def ssd_heads(x, dtraw, dt_bias, a_log, dskip, bm, cm, prev):
    h, q, _ = x.shape
    n = bm.shape[1]
    li = lax.broadcasted_iota(jnp.int32, (1, q, q), 1)
    si = lax.broadcasted_iota(jnp.int32, (1, q, q), 2)

    def to_col(row):
        return jnp.sum(jnp.where(li == si, jnp.broadcast_to(row, (h, q, q)), 0.0), axis=2, keepdims=True)

    dt_row = _softplus(dtraw + dt_bias)
    a_row = dt_row * (-jnp.exp(a_log))
    a_col = to_col(a_row)
    acs_col = jnp.sum(jnp.where(si <= li, jnp.broadcast_to(a_row, (h, q, q)), 0.0), axis=2, keepdims=True)
    acs_row = jnp.sum(jnp.where(li <= si, jnp.broadcast_to(a_col, (h, q, q)), 0.0), axis=1, keepdims=True)
    total = jnp.sum(a_row, axis=2, keepdims=True)
    xdt = x * to_col(dt_row)
    lmat = jnp.exp(jnp.where(li >= si, acs_col - acs_row, -1e30))
    bmb = jnp.broadcast_to(bm[None], (h, q, n))
    cmb = jnp.broadcast_to(cm[None], (h, q, n))
    y = bmm(mm_nt(cm, bm)[None] * lmat, xdt)
    y = y + bmm_nt(cmb, prev) * jnp.exp(acs_col)
    y = y + dskip * x
    state = bmm_tn(xdt * jnp.exp(total - acs_col), bmb)
    return y, jnp.exp(total) * prev + state


HEADS_PER_GROUP = SSM_HEADS // SSM_GROUPS
BM_COL0 = SSM_INNER // SSM_STATE
CM_COL0 = BM_COL0 + SSM_GROUPS


def ssd_fwd(xs_hm, dtraw_t, dt_bias, a_log, dskip, xbc, side=None):
    hg = HEADS_PER_GROUP

    def body(x_ref, dt_ref, dtb_ref, al_ref, dk_ref, bm_ref, cm_ref, y_ref, prev_ref, state_ref):
        @pl.when(pl.program_id(1) == 0)
        def _():
            state_ref[...] = jnp.zeros_like(state_ref)

        prev = state_ref[...]
        prev_ref[0] = prev
        y, nxt = ssd_heads(x_ref[...], dt_ref[...], dtb_ref[...], al_ref[...], dk_ref[...], bm_ref[...], cm_ref[...], prev)
        y_ref[...] = y
        state_ref[...] = nxt

    hp = pl.BlockSpec((hg, 1, 1), lambda g, c: (g, 0, 0))
    dtraw_t, dt_bias, a_log, dskip = [a.reshape(SSM_HEADS, 1, -1) for a in (dtraw_t, dt_bias, a_log, dskip)]
    return grid_call(
        body, (xs_hm, dtraw_t, dt_bias, a_log, dskip, xbc, xbc), name="ssd_fwd",
        out_shape=[jax.ShapeDtypeStruct((SSM_HEADS, S, SSM_HDIM), F32),
                   jax.ShapeDtypeStruct((N_CHUNKS, SSM_HEADS, SSM_HDIM, SSM_STATE), F32)],
        grid=(SSM_GROUPS, N_CHUNKS),
        in_specs=[pl.BlockSpec((hg, CHUNK, SSM_HDIM), lambda g, c: (g, c, 0)),
                  pl.BlockSpec((hg, 1, CHUNK), lambda g, c: (g, 0, c)), hp, hp, hp,
                  pl.BlockSpec((CHUNK, SSM_STATE), lambda g, c: (c, BM_COL0 + g)),
                  pl.BlockSpec((CHUNK, SSM_STATE), lambda g, c: (c, CM_COL0 + g))],
        out_specs=[pl.BlockSpec((hg, CHUNK, SSM_HDIM), lambda g, c: (g, c, 0)),
                   pl.BlockSpec((1, hg, SSM_HDIM, SSM_STATE), lambda g, c: (c, g, 0, 0))],
        scratch_shapes=[pltpu.VMEM((hg, SSM_HDIM, SSM_STATE), F32)],
        semantics=("parallel", "arbitrary"), side=side)


def ssd_bwd(xs_hm, dtraw_t, dt_bias, a_log, dskip, xbc, prev_all, dy_hm, side=None):
    hg = HEADS_PER_GROUP
    last = N_CHUNKS - 1

    def body(x_ref, dt_ref, dtb_ref, al_ref, dk_ref, bm_ref, cm_ref, prev_ref, dy_ref,
             dx_ref, ddt_ref, ddtb_ref, dal_ref, ddk_ref, dbm_ref, dcm_ref, dstate_ref):
        @pl.when(pl.program_id(1) == 0)
        def _():
            dstate_ref[...] = jnp.zeros_like(dstate_ref)
            ddtb_ref[...] = jnp.zeros_like(ddtb_ref)
            dal_ref[...] = jnp.zeros_like(dal_ref)
            ddk_ref[...] = jnp.zeros_like(ddk_ref)

        _, vjp = jax.vjp(ssd_heads, x_ref[...], dt_ref[...], dtb_ref[...], al_ref[...], dk_ref[...], bm_ref[...],
                         cm_ref[...], prev_ref[0])
        dx, ddt, ddtb, dal, ddk, dbm, dcm, dprev = vjp((dy_ref[...], dstate_ref[...]))
        dx_ref[...] = dx
        ddt_ref[...] = ddt
        ddtb_ref[...] += ddtb
        dal_ref[...] += dal
        ddk_ref[...] += ddk
        dbm_ref[...] = dbm
        dcm_ref[...] = dcm
        dstate_ref[...] = dprev

    hp = pl.BlockSpec((hg, 1, 1), lambda g, c: (g, 0, 0))
    xspec = pl.BlockSpec((hg, CHUNK, SSM_HDIM), lambda g, c: (g, last - c, 0))
    tspec = pl.BlockSpec((hg, 1, CHUNK), lambda g, c: (g, 0, last - c))
    gspec = pl.BlockSpec((CHUNK, SSM_STATE), lambda g, c: (last - c, g))
    dtraw_t, dt_bias, a_log, dskip = [a.reshape(SSM_HEADS, 1, -1) for a in (dtraw_t, dt_bias, a_log, dskip)]
    res, side_dst = grid_call(
        body, (xs_hm, dtraw_t, dt_bias, a_log, dskip, xbc, xbc, prev_all, dy_hm), name="ssd_bwd",
        out_shape=[jax.ShapeDtypeStruct((SSM_HEADS, S, SSM_HDIM), F32), jax.ShapeDtypeStruct((SSM_HEADS, 1, S), F32),
                   jax.ShapeDtypeStruct((SSM_HEADS, 1, 1), F32), jax.ShapeDtypeStruct((SSM_HEADS, 1, 1), F32),
                   jax.ShapeDtypeStruct((SSM_HEADS, 1, 1), F32),
                   jax.ShapeDtypeStruct((S, SSM_GROUPS * SSM_STATE), F32),
                   jax.ShapeDtypeStruct((S, SSM_GROUPS * SSM_STATE), F32)],
        grid=(SSM_GROUPS, N_CHUNKS),
        in_specs=[xspec, tspec, hp, hp, hp,
                  pl.BlockSpec((CHUNK, SSM_STATE), lambda g, c: (last - c, BM_COL0 + g)),
                  pl.BlockSpec((CHUNK, SSM_STATE), lambda g, c: (last - c, CM_COL0 + g)),
                  pl.BlockSpec((1, hg, SSM_HDIM, SSM_STATE), lambda g, c: (last - c, g, 0, 0)), xspec],
        out_specs=[xspec, tspec, hp, hp, hp, gspec, gspec],
        scratch_shapes=[pltpu.VMEM((hg, SSM_HDIM, SSM_STATE), F32)],
        semantics=("parallel", "arbitrary"), side=side)
    return [res[0]] + [r.reshape(SSM_HEADS, -1) for r in res[1:5]] + list(res[5:]), side_dst


ATT_HB = 8


def _bdot(a, b, ca, cb):
    return lax.dot_general(a.astype(BF16), b.astype(BF16), (((ca,), (cb,)), ((0,), (0,))), preferred_element_type=F32)


@jax.custom_vjp
def bmm(a, b):
    return _bdot(a, b, 2, 1)


def _bmm_fwd(a, b):
    return _bdot(a, b, 2, 1), (a, b)


def _bmm_bwd(res, g):
    a, b = res
    return _bdot(g, b, 2, 2).astype(a.dtype), _bdot(a, g, 1, 1).astype(b.dtype)


bmm.defvjp(_bmm_fwd, _bmm_bwd)


@jax.custom_vjp
def bmm_nt(a, b):
    return _bdot(a, b, 2, 2)


def _bmm_nt_fwd(a, b):
    return _bdot(a, b, 2, 2), (a, b)


def _bmm_nt_bwd(res, g):
    a, b = res
    return _bdot(g, b, 2, 1).astype(a.dtype), _bdot(g, a, 1, 1).astype(b.dtype)


bmm_nt.defvjp(_bmm_nt_fwd, _bmm_nt_bwd)


@jax.custom_vjp
def bmm_tn(a, b):
    return _bdot(a, b, 1, 1)


def _bmm_tn_fwd(a, b):
    return _bdot(a, b, 1, 1), (a, b)


def _bmm_tn_bwd(res, g):
    a, b = res
    return _bdot(b, g, 2, 2).astype(a.dtype), _bdot(a, g, 2, 1).astype(b.dtype)


bmm_tn.defvjp(_bmm_tn_fwd, _bmm_tn_bwd)


def att_heads(q, kp, kc, vp, vc, bias_p, bias_c, has_prev):
    h, b, dh = q.shape
    i = lax.broadcasted_iota(jnp.int32, (1, b, b), 1)
    j = lax.broadcasted_iota(jnp.int32, (1, b, b), 2)
    scale = dh ** -0.5
    sp = jnp.where(jnp.logical_and(j >= i, has_prev), bmm_nt(q, kp) * scale + bias_p, -1e30)
    sc = jnp.where(j <= i, bmm_nt(q, kc) * scale + bias_c, -1e30)
    m = lax.stop_gradient(jnp.maximum(jnp.max(sp, axis=2, keepdims=True), jnp.max(sc, axis=2, keepdims=True)))
    pp, pc = jnp.exp(sp - m), jnp.exp(sc - m)
    l = jnp.sum(pp, axis=2, keepdims=True) + jnp.sum(pc, axis=2, keepdims=True)
    o = bmm(pp / l, vp) + bmm(pc / l, vc)
    return o, jnp.broadcast_to(m + jnp.log(l), (h, b, dh))


def _att_specs(nb):
    hb, blk = ATT_HB, ATT_BLK
    cur = pl.BlockSpec((hb, blk, ATT_HDIM), lambda h, b: (h, b, 0))
    prv = pl.BlockSpec((hb, blk, ATT_HDIM), lambda h, b: (h, jnp.maximum(b - 1, 0), 0))
    bias = pl.BlockSpec((hb, 2, blk, blk), lambda h, b: (h, 0, 0, 0))
    return cur, prv, bias


def att_fwd(q, k, v, bias, nb, *, name):
    cur, prv, bspec = _att_specs(nb)

    def body(q_ref, kp_ref, kc_ref, vp_ref, vc_ref, b_ref, o_ref, l_ref):
        has_prev = (pl.program_id(1) % nb) != 0
        o, lse = att_heads(q_ref[...], kp_ref[...], kc_ref[...], vp_ref[...], vc_ref[...], b_ref[:, 0], b_ref[:, 1],
                           has_prev)
        o_ref[...] = o
        l_ref[...] = lse

    shp = jax.ShapeDtypeStruct((ATT_HEADS, S, ATT_HDIM), F32)
    return pl.pallas_call(
        body, name=name, out_shape=[shp, shp],
        grid=(ATT_HEADS // ATT_HB, S // ATT_BLK),
        in_specs=[cur, prv, cur, prv, cur, bspec],
        out_specs=[cur, cur],
        compiler_params=_cparams("parallel", "parallel"),
    )(q, k, k, v, v, bias)


def att_bwd(q, k, v, bias, do, dlse, nb, *, name):
    cur, prv, bspec = _att_specs(nb)

    def body(q_ref, kp_ref, kc_ref, vp_ref, vc_ref, b_ref, do_ref, dl_ref,
             dq_ref, dkc_ref, dkp_ref, dvc_ref, dvp_ref, db_ref):
        has_prev = (pl.program_id(1) % nb) != 0

        @pl.when(pl.program_id(1) == 0)
        def _():
            db_ref[...] = jnp.zeros_like(db_ref)

        ins = _f32([q_ref[...], kp_ref[...], kc_ref[...], vp_ref[...], vc_ref[...]]) + [b_ref[:, 0], b_ref[:, 1]]
        _, vjp = jax.vjp(functools.partial(att_heads, has_prev=has_prev), *ins)
        dq, dkp, dkc, dvp, dvc, dbp, dbc = vjp((do_ref[...], dl_ref[...]))
        dq_ref[...] = dq
        dkc_ref[...] = dkc
        dkp_ref[...] = dkp
        dvc_ref[...] = dvc
        dvp_ref[...] = dvp
        db_ref[:, 0] += dbp
        db_ref[:, 1] += dbc

    shp = jax.ShapeDtypeStruct((ATT_HEADS, S, ATT_HDIM), F32)
    return pl.pallas_call(
        body, name=name,
        out_shape=[shp] * 5 + [jax.ShapeDtypeStruct((ATT_HEADS, 2, ATT_BLK, ATT_BLK), F32)],
        grid=(ATT_HEADS // ATT_HB, S // ATT_BLK),
        in_specs=[cur, prv, cur, prv, cur, bspec, cur, cur],
        out_specs=[cur] * 5 + [bspec],
        compiler_params=_cparams("parallel", "arbitrary"),
    )(q, k, k, v, v, bias, do, dlse)


def shift_add(cur, prev, nb, *, name):
    n_blocks = S // ATT_BLK

    def body(c_ref, p_ref, o_ref):
        nxt = pl.program_id(0) + 1
        keep = jnp.where((nxt % nb) != 0, 1.0, 0.0)
        o_ref[...] = c_ref[...] + keep * p_ref[...]

    return pl.pallas_call(
        body, name=name, out_shape=jax.ShapeDtypeStruct(cur.shape, F32),
        grid=(n_blocks,),
        in_specs=[pl.BlockSpec((ATT_HEADS, ATT_BLK, ATT_HDIM), lambda b: (0, b, 0)),
                  pl.BlockSpec((ATT_HEADS, ATT_BLK, ATT_HDIM), lambda b: (0, jnp.minimum(b + 1, n_blocks - 1), 0))],
        out_specs=pl.BlockSpec((ATT_HEADS, ATT_BLK, ATT_HDIM), lambda b: (0, b, 0)),
        compiler_params=_cparams("parallel"),
    )(cur, prev)


ATT_PAIRS = ATT_HEADS // 2
PAIR_W = 2 * ATT_HDIM


def att_pairs(q, kp, kc, vp, vc, bias, has_prev):
    t, b, w = q.shape
    i = lax.broadcasted_iota(jnp.int32, (1, b, b), 1)
    j = lax.broadcasted_iota(jnp.int32, (1, b, b), 2)
    first = lax.broadcasted_iota(jnp.int32, (1, 1, w), 2) < ATT_HDIM
    scale = ATT_HDIM ** -0.5
    outs, lses = [], []
    for ab in range(2):
        qh = jnp.where(first if ab == 0 else jnp.logical_not(first), q, 0.0)
        sp = jnp.where(jnp.logical_and(j >= i, has_prev), bmm_nt(qh, kp) * scale + bias[:, ab, 0], -1e30)
        sc = jnp.where(j <= i, bmm_nt(qh, kc) * scale + bias[:, ab, 1], -1e30)
        m = lax.stop_gradient(jnp.maximum(jnp.max(sp, axis=2, keepdims=True), jnp.max(sc, axis=2, keepdims=True)))
        pp, pc = jnp.exp(sp - m), jnp.exp(sc - m)
        l = jnp.sum(pp, axis=2, keepdims=True) + jnp.sum(pc, axis=2, keepdims=True)
        outs.append(bmm(pp / l, vp) + bmm(pc / l, vc))
        lses.append(jnp.broadcast_to(m + jnp.log(l), (t, b, w)))
    return jnp.where(first, outs[0], outs[1]), jnp.where(first, lses[0], lses[1])


def _pair_tiles(ref):
    return jnp.stack([ref[:, PAIR_W * t:PAIR_W * (t + 1)] for t in range(ATT_PAIRS)])


def _store_pair_tiles(ref, val):
    for t in range(ATT_PAIRS):
        ref[:, PAIR_W * t:PAIR_W * (t + 1)] = val[t].astype(ref.dtype)


def pair_bias(bias):
    return bias.reshape(ATT_PAIRS, 2, 2, ATT_BLK, ATT_BLK)


def att2_fwd(q, k, v, bias, nb, cols, *, name, side=None):
    n_blocks = S // ATT_BLK
    qc, kc, vc = cols

    def body(q_ref, k_ref, v_ref, b_ref, o_ref, l_ref, kprev, vprev):
        blk = pl.program_id(0)

        @pl.when(blk == 0)
        def _():
            kprev[...] = jnp.zeros_like(kprev)
            vprev[...] = jnp.zeros_like(vprev)

        k3, v3 = _pair_tiles(k_ref), _pair_tiles(v_ref)
        o, lse = att_pairs(_pair_tiles(q_ref), kprev[...], k3, vprev[...], v3, b_ref[...], (blk % nb) != 0)
        _store_pair_tiles(o_ref, o)
        _store_pair_tiles(l_ref, lse)
        kprev[...] = k3
        vprev[...] = v3

    def spec(c):
        return pl.BlockSpec((ATT_BLK, D), lambda b: (b, c))

    return grid_call(
        body, (q, k, v, bias), name=name,
        out_shape=[jax.ShapeDtypeStruct((S, D), BF16), jax.ShapeDtypeStruct((S, D), F32)], grid=(n_blocks,),
        in_specs=[spec(qc), spec(kc), spec(vc), pl.BlockSpec(bias.shape, lambda b: (0, 0, 0, 0, 0))],
        out_specs=[spec(0), spec(0)],
        scratch_shapes=[pltpu.VMEM((ATT_PAIRS, ATT_BLK, PAIR_W), BF16), pltpu.VMEM((ATT_PAIRS, ATT_BLK, PAIR_W), BF16)],
        semantics=("arbitrary",), side=side)


def att2_bwd(q, k, v, bias, do, dlse, nb, cols, *, name, side=None):
    n_blocks = S // ATT_BLK
    qc, kc, vc = cols

    def body(q_ref, k_ref, v_ref, b_ref, do_ref, dl_ref, dq_ref, dk_ref, dv_ref, db_ref, kprev, vprev, dk_own, dv_own):
        blk = pl.program_id(0)

        @pl.when(blk == 0)
        def _():
            for r in (kprev, vprev, dk_own, dv_own, db_ref):
                r[...] = jnp.zeros_like(r)

        @pl.when(blk < n_blocks)
        def _():
            k3, v3 = _pair_tiles(k_ref), _pair_tiles(v_ref)
            ins = _f32([_pair_tiles(q_ref), kprev[...], k3, vprev[...], v3]) + [b_ref[...]]
            _, vjp = jax.vjp(functools.partial(att_pairs, has_prev=(blk % nb) != 0), *ins)
            dq, dkp, dkc, dvp, dvc, db = vjp(tuple(_f32([_pair_tiles(do_ref), _pair_tiles(dl_ref)])))
            _store_pair_tiles(dq_ref, dq)
            _store_pair_tiles(dk_ref, dk_own[...] + dkp)
            _store_pair_tiles(dv_ref, dv_own[...] + dvp)
            dk_own[...] = dkc
            dv_own[...] = dvc
            db_ref[...] += db
            kprev[...] = k3
            vprev[...] = v3

        @pl.when(blk == n_blocks)
        def _():
            _store_pair_tiles(dk_ref, dk_own[...])
            _store_pair_tiles(dv_ref, dv_own[...])

    def spec(c):
        return pl.BlockSpec((ATT_BLK, D), lambda b: (jnp.minimum(b, n_blocks - 1), c))

    late = pl.BlockSpec((ATT_BLK, D), lambda b: (jnp.maximum(b - 1, 0), 0))
    bspec = pl.BlockSpec(bias.shape, lambda b: (0, 0, 0, 0, 0))
    tile_f32 = pltpu.VMEM((ATT_PAIRS, ATT_BLK, PAIR_W), F32)
    tile_bf16 = pltpu.VMEM((ATT_PAIRS, ATT_BLK, PAIR_W), BF16)
    return grid_call(
        body, (q, k, v, bias, do, dlse), name=name,
        out_shape=[jax.ShapeDtypeStruct((S, D), BF16), jax.ShapeDtypeStruct((S, D), BF16),
                   jax.ShapeDtypeStruct((S, D), BF16), jax.ShapeDtypeStruct(bias.shape, F32)],
        grid=(n_blocks + 1,),
        in_specs=[spec(qc), spec(kc), spec(vc), bspec, spec(0), spec(0)],
        out_specs=[spec(0), late, late, bspec],
        scratch_shapes=[tile_bf16, tile_bf16, tile_f32, tile_f32],
        semantics=("arbitrary",), side=side)


def regroup(a, dil, inverse=False):
    if dil == 1:
        return a
    c_dim = a.shape[1]
    shape = (dil, S // dil, c_dim) if inverse else (S // dil, dil, c_dim)
    return jnp.transpose(a.reshape(shape), (1, 0, 2)).reshape(S, c_dim)


SSD_PAIRS = SSM_HEADS // 2
PAIRS_PER_GROUP = SSD_PAIRS // SSM_GROUPS
GROUP_W = HEADS_PER_GROUP * SSM_HDIM


SSD_GROUPS_PER_STEP = 2
SSD_STEP_PAIRS = PAIRS_PER_GROUP * SSD_GROUPS_PER_STEP
SSD_STEP_W = GROUP_W * SSD_GROUPS_PER_STEP


def ssd_pairs(x, dtraw, dt_bias, a_log, dskip, bms, cms, prev):
    t, q, w = x.shape
    n = bms[0].shape[1]
    per = t // len(bms)

    def by_pair(mats):
        return jnp.concatenate([jnp.broadcast_to(m[None], (per,) + m.shape) for m in mats], axis=0)
    li = lax.broadcasted_iota(jnp.int32, (1, q, q), 1)
    si = lax.broadcasted_iota(jnp.int32, (1, q, q), 2)
    first_lane = lax.broadcasted_iota(jnp.int32, (1, 1, w), 2) < SSM_HDIM
    first_row = lax.broadcasted_iota(jnp.int32, (1, w, 1), 1) < SSM_HDIM

    def to_col(row):
        return jnp.sum(jnp.where(li == si, jnp.broadcast_to(row, (t, q, q)), 0.0), axis=2, keepdims=True)

    def lanes(a0, a1):
        return jnp.where(first_lane, a0, a1)

    dt_col, acs_col, total, lmat = [], [], [], []
    for ab in range(2):
        dt_row = _softplus(dtraw[ab] + dt_bias[ab])
        a_row = dt_row * (-jnp.exp(a_log[ab]))
        a_col = to_col(a_row)
        acs_c = jnp.sum(jnp.where(si <= li, jnp.broadcast_to(a_row, (t, q, q)), 0.0), axis=2, keepdims=True)
        acs_r = jnp.sum(jnp.where(li <= si, jnp.broadcast_to(a_col, (t, q, q)), 0.0), axis=1, keepdims=True)
        dt_col.append(to_col(dt_row))
        acs_col.append(acs_c)
        total.append(jnp.sum(a_row, axis=2, keepdims=True))
        lmat.append(jnp.exp(jnp.where(li >= si, acs_c - acs_r, -1e30)))
    cb = by_pair([mm_nt(c_, b_) for c_, b_ in zip(cms, bms, strict=True)])
    bmb, cmb = by_pair(bms), by_pair(cms)
    xdt = x * lanes(dt_col[0], dt_col[1])
    y = lanes(bmm(cb * lmat[0], xdt), bmm(cb * lmat[1], xdt))
    y = y + bmm_nt(cmb, prev) * lanes(jnp.exp(acs_col[0]), jnp.exp(acs_col[1]))
    y = y + lanes(dskip[0], dskip[1]) * x
    state = bmm_tn(xdt * lanes(jnp.exp(total[0] - acs_col[0]), jnp.exp(total[1] - acs_col[1])), bmb)
    return y, jnp.where(first_row, jnp.exp(total[0]), jnp.exp(total[1])) * prev + state


def _group_tiles(ref):
    return jnp.stack([ref[:, PAIR_W * t:PAIR_W * (t + 1)] for t in range(SSD_STEP_PAIRS)])


def _store_group_tiles(ref, val):
    for t in range(SSD_STEP_PAIRS):
        ref[:, PAIR_W * t:PAIR_W * (t + 1)] = val[t]


def _bc_groups(ref):
    return tuple(ref[:, SSM_STATE * i:SSM_STATE * (i + 1)] for i in range(SSD_GROUPS_PER_STEP))


def _by_pair(a):
    return jnp.transpose(a.reshape(SSD_PAIRS, 2, 1, -1), (1, 0, 2, 3))


def _by_head(a):
    return jnp.transpose(a, (1, 0, 2, 3)).reshape(SSM_HEADS, -1)


def _ssd2_specs(chunk_of):
    tp = SSD_STEP_PAIRS
    xspec = pl.BlockSpec((CHUNK, SSD_STEP_W), lambda g, c: (chunk_of(c), g))
    tspec = pl.BlockSpec((2, tp, 1, CHUNK), lambda g, c: (0, g, 0, chunk_of(c)))
    hp = pl.BlockSpec((2, tp, 1, 1), lambda g, c: (0, g, 0, 0))
    gspec = pl.BlockSpec((CHUNK, SSD_GROUPS_PER_STEP * SSM_STATE), lambda g, c: (chunk_of(c), g))
    sspec = pl.BlockSpec((1, tp, PAIR_W, SSM_STATE), lambda g, c: (chunk_of(c), g, 0, 0))
    return xspec, tspec, hp, gspec, sspec


def ssd2_fwd(xs, dtraw_t, dt_bias, a_log, dskip, bm, cm, side=None):
    def body(x_ref, dt_ref, dtb_ref, al_ref, dk_ref, bm_ref, cm_ref, y_ref, prev_ref, state_ref):
        @pl.when(pl.program_id(1) == 0)
        def _():
            state_ref[...] = jnp.zeros_like(state_ref)

        prev = state_ref[...]
        prev_ref[0] = prev
        y, nxt = ssd_pairs(_group_tiles(x_ref), dt_ref[...], dtb_ref[...], al_ref[...], dk_ref[...], _bc_groups(bm_ref),
                           _bc_groups(cm_ref), prev)
        _store_group_tiles(y_ref, y)
        state_ref[...] = nxt

    xspec, tspec, hp, gspec, sspec = _ssd2_specs(lambda c: c)
    return grid_call(
        body, (xs, _by_pair(dtraw_t), _by_pair(dt_bias), _by_pair(a_log), _by_pair(dskip), bm, cm), name="ssd_fwd",
        out_shape=[jax.ShapeDtypeStruct((S, SSM_INNER), F32),
                   jax.ShapeDtypeStruct((N_CHUNKS, SSD_PAIRS, PAIR_W, SSM_STATE), F32)],
        grid=(SSM_GROUPS // SSD_GROUPS_PER_STEP, N_CHUNKS), in_specs=[xspec, tspec, hp, hp, hp, gspec, gspec],
        out_specs=[xspec, sspec],
        scratch_shapes=[pltpu.VMEM((SSD_STEP_PAIRS, PAIR_W, SSM_STATE), F32)],
        semantics=("parallel", "arbitrary"), side=side)


def ssd2_bwd(xs, dtraw_t, dt_bias, a_log, dskip, bm, cm, prev_all, dy, side=None):
    def body(x_ref, dt_ref, dtb_ref, al_ref, dk_ref, bm_ref, cm_ref, prev_ref, dy_ref,
             dx_ref, ddt_ref, ddtb_ref, dal_ref, ddk_ref, dbm_ref, dcm_ref, dstate_ref):
        @pl.when(pl.program_id(1) == 0)
        def _():
            for r in (dstate_ref, ddtb_ref, dal_ref, ddk_ref):
                r[...] = jnp.zeros_like(r)

        _, vjp = jax.vjp(ssd_pairs, _group_tiles(x_ref), dt_ref[...], dtb_ref[...], al_ref[...], dk_ref[...],
                         _bc_groups(bm_ref), _bc_groups(cm_ref), prev_ref[0])
        dx, ddt, ddtb, dal, ddk, dbms, dcms, dprev = vjp((_group_tiles(dy_ref), dstate_ref[...]))
        _store_group_tiles(dx_ref, dx)
        ddt_ref[...] = ddt
        ddtb_ref[...] += ddtb
        dal_ref[...] += dal
        ddk_ref[...] += ddk
        for i in range(SSD_GROUPS_PER_STEP):
            dbm_ref[:, SSM_STATE * i:SSM_STATE * (i + 1)] = dbms[i]
            dcm_ref[:, SSM_STATE * i:SSM_STATE * (i + 1)] = dcms[i]
        dstate_ref[...] = dprev

    xspec, tspec, hp, gspec, sspec = _ssd2_specs(lambda c: N_CHUNKS - 1 - c)
    par = jax.ShapeDtypeStruct((2, SSD_PAIRS, 1, 1), F32)
    res, side_dst = grid_call(
        body, (xs, _by_pair(dtraw_t), _by_pair(dt_bias), _by_pair(a_log), _by_pair(dskip), bm, cm, prev_all, dy),
        name="ssd_bwd",
        out_shape=[jax.ShapeDtypeStruct((S, SSM_INNER), F32), jax.ShapeDtypeStruct((2, SSD_PAIRS, 1, S), F32), par, par, par,
                   jax.ShapeDtypeStruct((S, SSM_GROUPS * SSM_STATE), F32),
                   jax.ShapeDtypeStruct((S, SSM_GROUPS * SSM_STATE), F32)],
        grid=(SSM_GROUPS // SSD_GROUPS_PER_STEP, N_CHUNKS), in_specs=[xspec, tspec, hp, hp, hp, gspec, gspec, sspec, xspec],
        out_specs=[xspec, tspec, hp, hp, hp, gspec, gspec],
        scratch_shapes=[pltpu.VMEM((SSD_STEP_PAIRS, PAIR_W, SSM_STATE), F32)],
        semantics=("parallel", "arbitrary"), side=side)
    return [res[0]] + [_by_head(r) for r in res[1:5]] + list(res[5:]), side_dst


def _silu(x):
    return x * jax.nn.sigmoid(x)


def _rms(x):
    return x * lax.rsqrt(jnp.mean(x * x, -1, keepdims=True) + EPS)


def f_normmod(x, g, sc, sh):
    return (_rms(x) * g * (1.0 + sc) + sh,)


def f_resid(x, mix, gate):
    return (x + gate * mix,)


def f_resid_bias(x, mix, gate, b):
    return (x + gate * (mix + b),)


def f_swiglu(hgu):
    return (_silu(hgu[:, :FFN_HIDDEN]) * hgu[:, FFN_HIDDEN:],)


def f_silu(x):
    return (_silu(x),)


def f_silu_xbc(x):
    y = _silu(x)
    n_b = SSM_GROUPS * SSM_STATE
    return y[:, :SSM_INNER], y[:, SSM_INNER:SSM_INNER + n_b], y[:, SSM_INNER + n_b:]


def f_gated_norm(y, z, g):
    return (_rms(y * _silu(z)) * g,)


def f_glu(y, b):
    y = y + b
    return (y[:, :D] * jax.nn.sigmoid(y[:, D:]),)


def f_ln_silu(u, g, b):
    mu = jnp.mean(u, -1, keepdims=True)
    var = jnp.mean(jnp.square(u - mu), -1, keepdims=True)
    return (_silu((u - mu) * lax.rsqrt(var + EPS) * g + b),)


def f_combine(o1, o2, o3, l1, l2, l3):
    m = lax.stop_gradient(jnp.maximum(jnp.maximum(l1, l2), l3))
    e1, e2, e3 = jnp.exp(l1 - m), jnp.exp(l2 - m), jnp.exp(l3 - m)
    return ((e1 * o1 + e2 * o2 + e3 * o3) / (e1 + e2 + e3),)


def f_head(x, tgt, g):
    return (0.5 * jnp.mean(jnp.square(_rms(x) * g - tgt), -1, keepdims=True),)


def f_sum3(a, b, c):
    return (a + b + c,)


def f_sum4(a, b, c, d):
    return (a + b + c + d,)


def f_add(a, b):
    return (a + b,)


def f_adamw(w, g, m, v):
    m = ADAM_B1 * m + (1.0 - ADAM_B1) * g
    v = ADAM_B2 * v + (1.0 - ADAM_B2) * jnp.square(g)
    m_hat = m / (1.0 - ADAM_B1 ** ADAM_STEP)
    v_hat = v / (1.0 - ADAM_B2 ** ADAM_STEP)
    return -ADAM_LR * (m_hat / (jnp.sqrt(v_hat) + ADAM_EPS) + ADAM_WD * w), m, v


def _rows_tile(r, cap=256):
    return _pick(r, cap, mult=8)


def adamw(w, g, m, v, *, name):
    l_dim, r_dim, c_dim = w.shape
    tr = _rows_tile(r_dim, cap=128)

    def body(w_ref, g_ref, m_ref, v_ref, d_ref, mo_ref, vo_ref):
        d_ref[...], mo_ref[...], vo_ref[...] = f_adamw(w_ref[...], g_ref[...], m_ref[...], v_ref[...])

    spec = pl.BlockSpec((1, tr, c_dim), lambda l, i: (l, i, 0))
    return pl.pallas_call(
        body, name=name, out_shape=[jax.ShapeDtypeStruct(w.shape, F32)] * 3, grid=(l_dim, r_dim // tr),
        in_specs=[spec] * 4, out_specs=[spec] * 3, compiler_params=_cparams("parallel", "parallel"),
    )(w, g, m, v)


def _t5_bucket(dist):
    max_exact = REL_BUCKETS // 2
    n = jnp.maximum(dist, 1).astype(F32)
    large = max_exact + jnp.log(n / max_exact) / math.log(REL_MAX_DIST / max_exact) * (REL_BUCKETS - max_exact)
    large = jnp.minimum(large.astype(jnp.int32), REL_BUCKETS - 1)
    return jnp.where(dist < max_exact, dist, large)


def _att_buckets(dil):
    i = jnp.arange(ATT_BLK)[:, None]
    j = jnp.arange(2 * ATT_BLK)[None, :]
    bkt = _t5_bucket(jnp.maximum(ATT_BLK + i - j, 0) * dil)
    return jnp.transpose(bkt.reshape(ATT_BLK, 2, ATT_BLK), (1, 0, 2))


def att_bias(rel_table, p, dil):
    tab = rel_table[:, p * ATT_HEADS:(p + 1) * ATT_HEADS]
    onehot = (jnp.arange(REL_BUCKETS)[:, None] == _att_buckets(dil).reshape(1, -1)).astype(F32)
    bias = lax.dot_general(tab, onehot, (((0,), (0,)), ((), ())), precision=lax.Precision.HIGHEST)
    return bias.reshape(ATT_HEADS, 2, ATT_BLK, ATT_BLK)


def att_bias_grad(dbias, dil, *, name):
    onehot = (_att_buckets(dil).reshape(-1, 1) == jnp.arange(LANES)[None, :]).astype(BF16)
    dtab = matmul(dbias.reshape(ATT_HEADS, -1), onehot, mode="nn", out_dtype=F32, name=name, tk_cap=2048)
    return dtab[:, :REL_BUCKETS].T


def to_heads(a, n_heads, dil=1):
    hd = a.shape[1] // n_heads
    return jnp.transpose(a.reshape(S // dil, dil, n_heads, hd), (2, 1, 0, 3)).reshape(n_heads, S, hd)


def from_heads(a, dil=1):
    n_heads, _, hd = a.shape
    return jnp.transpose(a.reshape(n_heads, dil, S // dil, hd), (2, 1, 0, 3)).reshape(S, n_heads * hd)


def regroup_heads(a, dil, inverse=False):
    n_heads, _, hd = a.shape
    if dil == 1:
        return a
    if inverse:
        return jnp.transpose(a.reshape(n_heads, dil, S // dil, hd), (0, 2, 1, 3)).reshape(n_heads, S, hd)
    return jnp.transpose(a.reshape(n_heads, S // dil, dil, hd), (0, 2, 1, 3)).reshape(n_heads, S, hd)


HY_Z, HY_XBC, HY_DT, HY_Q, HY_K, HY_V = 2048, 3072, 32, 3072, 1024, 1024
HY_IN = HY_Z + HY_XBC + HY_DT + HY_Q + HY_K + HY_V
OFF_Z, OFF_XBC, OFF_Q, OFF_KV, OFF_DT = 0, 2048, 5120, 8192, 10240
HY_CAT = OFF_DT + LANES
DT_PAD = LANES


def hy_to_cat(w):
    z, xbc, dt, qkv = w[:2048], w[2048:5120], w[5120:5152], w[5152:]
    return jnp.concatenate([z, xbc, qkv, dt, jnp.zeros((DT_PAD - HY_DT,) + w.shape[1:], w.dtype)], axis=0)


def hy_from_cat(w, axis=0):
    part = lambda a, b: lax.slice_in_dim(w, a, b, axis=axis)
    return jnp.concatenate([part(0, 5120), part(OFF_DT, OFF_DT + HY_DT), part(5120, OFF_DT)], axis=axis)


def device_step(x, tgt, mods, wts, sp, comm=None):
    g = {}
    dmods = [[None] * 6 for _ in range(2)]
    wts = dict(wts)

    def wgrad(tokens_d, tokens_n, nm):
        return matmul(transpose(tokens_d, name=nm + "_t"), tokens_n, mode="nn", out_dtype=BF16, name=nm, out_t=True,
                      tk_cap=2048)

    def w_side(i):
        return None if comm is None else GatherRows(comm["pack"], comm["full"], *W_BATCHES[i])

    def g_side(i):
        return None if comm is None else ScatterRows(comm["ga"], comm["recv"], *G_BATCHES[i])

    def normmod(xi, gain, sc, sh, nm):
        return rowmap(f_normmod, [xi], [gain, sc, sh], [BF16], name=nm)[0]

    def ffn_fwd(xi, i, gate, nm):
        h = normmod(xi, sp["norm_ffn_g"][i], mods[i][4], mods[i][3], nm + "_norm")
        hgu = matmul(h, wts["gu_t"][i], mode="nt", out_dtype=BF16, name=nm + "_gu")
        act = rowmap(f_swiglu, [hgu], [], [BF16], name=nm + "_act", tr=128)[0]
        out = matmul(act, wts["down"][i], mode="nn", out_dtype=F32, name=nm + "_down")
        xo = rowmap(f_resid, [xi, out], [gate], [F32], name=nm + "_res")[0]
        return xo, (h, hgu, act, out)

    def ffn_bwd(dres, xi, i, saved, nm):
        h, hgu, act, out = saved
        (dout,), (dgate,), _ = rowmap_bwd(f_resid, [xi, out], [mods[i][5]], [dres], name=nm + "_res_b",
                                          row_grad=[False, True], row_dtypes=[BF16])
        dmods[i][5] = dgate
        dact = matmul(dout, wts["down"][i], mode="nt", out_dtype=BF16, name=nm + "_down_dx")
        g[f"down{i}"] = wgrad(dout, act, nm + "_down_dw")
        (dhgu,), _, _ = rowmap_bwd(f_swiglu, [hgu], [], [dact], name=nm + "_act_b", row_grad=[True],
                                   row_dtypes=[BF16], tr=128)
        g[f"gu_t{i}"] = wgrad(h, dhgu, nm + "_gu_dw")
        dh = matmul(dhgu, wts["gu_t"][i], mode="nn", out_dtype=F32, name=nm + "_gu_dx")
        (dres,), (dg_, dsc, dsh), _ = rowmap_bwd(f_normmod, [xi], [sp["norm_ffn_g"][i], mods[i][4], mods[i][3]], [dh],
                                                 name=nm + "_norm_b", row_grad=[True], row_add=[dres])
        g[f"norm_ffn_g{i}"] = dg_
        dmods[i][4], dmods[i][3] = dsc, dsh
        return dres

    h0 = normmod(x, sp["norm_mix_g"][0], mods[0][1], mods[0][0], "l0_norm")
    w_in = wts["hy_in_t"]
    z = matmul(h0, w_in, mode="nt", out_dtype=F32, name="hy_z", n=HY_Z, b_off=OFF_Z)
    xbc_raw = matmul(h0, w_in, mode="nt", out_dtype=F32, name="hy_xbc", n=HY_XBC, b_off=OFF_XBC)
    q = matmul(h0, w_in, mode="nt", out_dtype=BF16, name="hy_q", n=HY_Q, b_off=OFF_Q)
    kv = matmul(h0, w_in, mode="nt", out_dtype=BF16, name="hy_kv", n=HY_K + HY_V, b_off=OFF_KV)
    dtr = matmul(h0, w_in, mode="nt", out_dtype=F32, name="hy_dt", n=DT_PAD, b_off=OFF_DT)
    xbc_pre = conv_fwd(xbc_raw, sp["hy_conv_w"], sp["hy_conv_b"], name="hy_conv")
    xs, bm, cm = rowmap(f_silu_xbc, [xbc_pre], [], [F32] * 3, name="hy_conv_act", tr=128)
    dtraw_t = dtr[:, :HY_DT].T
    (y, prev_all), full = ssd2_fwd(xs, dtraw_t, sp["hy_dt_bias"], sp["hy_a_log"], sp["hy_d_skip"], bm, cm, side=w_side(1))
    if comm is not None:
        comm["full"] = full
    ysn = rowmap(f_gated_norm, [y, z], [sp["hy_ssm_norm_g"]], [BF16], name="hy_gnorm", tr=128)[0]
    att_in, att_o, att_l = [], [], []
    for p, (win, dil) in enumerate(ATT_PATTERNS):
        if dil == 1:
            qa, ka, va, cols = q, kv, kv, (p, 0, 1)
        else:
            qa, ka, cols = regroup(q[:, p * D:(p + 1) * D], dil), regroup(kv, dil), (0, 0, 1)
            va = ka
        bias = pair_bias(att_bias(sp["rel_table"], p, dil))
        nb = S // dil // ATT_BLK
        (o, lse), full = att2_fwd(qa, ka, va, bias, nb, cols, name=f"att_fwd{p}", side=w_side(2 + p))
        if comm is not None:
            comm["full"] = full
        att_in.append((qa, ka, va, bias, nb, cols))
        att_o.append(regroup(o, dil, inverse=True))
        att_l.append(regroup(lse, dil, inverse=True))
    if comm is not None:
        wts.update(unpack_weights(comm["full"], skip=("hy_in_t",)))
    att = rowmap(f_combine, att_o + att_l, [], [BF16], name="att_combine", tr=128)[0]
    cat = jnp.concatenate([ysn, att], axis=-1)
    mix0 = matmul(cat, wts["hy_out"], mode="nn", out_dtype=F32, name="hy_out")
    x1 = rowmap(f_resid, [x, mix0], [mods[0][2]], [F32], name="l0_res")[0]
    x2, ffn0 = ffn_fwd(x1, 0, mods[0][5], "ffn0")

    h1 = normmod(x2, sp["norm_mix_g"][1], mods[1][1], mods[1][0], "l1_norm")
    p1 = matmul(h1, wts["pw1_t"], mode="nt", out_dtype=F32, name="cv_pw1")
    u = rowmap(f_glu, [p1], [sp["cv_b_pw1"]], [F32], name="cv_glu")[0]
    uc = conv_fwd(u, sp["cv_w_dw"], sp["cv_b_dw"], name="cv_conv")
    ul = rowmap(f_ln_silu, [uc], [sp["cv_ln_g"], sp["cv_ln_b"]], [BF16], name="cv_ln")[0]
    mix1 = matmul(ul, wts["pw2"], mode="nn", out_dtype=F32, name="cv_pw2")
    x3 = rowmap(f_resid_bias, [x2, mix1], [mods[1][2], sp["cv_b_pw2"]], [F32], name="l1_res")[0]
    x4, ffn1 = ffn_fwd(x3, 1, mods[1][5], "ffn1")

    ones = jnp.ones((S, 1), F32)
    (dres,), (dfinal,), (loss_rows,) = rowmap_bwd(f_head, [x4, tgt], [sp["final_norm_g"]], [ones], name="head",
                                                  row_grad=[True, False], emit=(0,))
    g["final_norm_g"] = dfinal

    dres = ffn_bwd(dres, x3, 1, ffn1, "ffn1")
    (dmix1,), (dg1, db2), _ = rowmap_bwd(f_resid_bias, [x2, mix1], [mods[1][2], sp["cv_b_pw2"]], [dres], name="l1_res_b",
                                         row_grad=[False, True], row_dtypes=[BF16])
    dmods[1][2] = dg1
    g["cv_b_pw2"] = db2
    dul = matmul(dmix1, wts["pw2"], mode="nt", out_dtype=F32, name="cv_pw2_dx")
    g["pw2"] = wgrad(dmix1, ul, "cv_pw2_dw")
    (duc,), (g["cv_ln_g"], g["cv_ln_b"]), _ = rowmap_bwd(f_ln_silu, [uc], [sp["cv_ln_g"], sp["cv_ln_b"]], [dul],
                                                         name="cv_ln_b", row_grad=[True])
    du, g["cv_w_dw"], g["cv_b_dw"] = conv_bwd(u, sp["cv_w_dw"], duc, name="cv_conv_b", cb=128, chunk_rows=128)
    (dp1,), (g["cv_b_pw1"],), _ = rowmap_bwd(f_glu, [p1], [sp["cv_b_pw1"]], [du], name="cv_glu_b", row_grad=[True],
                                             row_dtypes=[BF16])
    g["pw1_t"] = wgrad(h1, dp1, "cv_pw1_dw")
    dh1 = matmul(dp1, wts["pw1_t"], mode="nn", out_dtype=F32, name="cv_pw1_dx")
    (dres,), (dg_, dsc, dsh), _ = rowmap_bwd(f_normmod, [x2], [sp["norm_mix_g"][1], mods[1][1], mods[1][0]], [dh1],
                                             name="l1_norm_b", row_grad=[True], row_add=[dres])
    g["norm_mix_g1"] = dg_
    dmods[1][1], dmods[1][0] = dsc, dsh

    dres = ffn_bwd(dres, x1, 0, ffn0, "ffn0")
    (dmix0,), (dg1,), _ = rowmap_bwd(f_resid, [x, mix0], [mods[0][2]], [dres], name="l0_res_b",
                                     row_grad=[False, True], row_dtypes=[BF16])
    dmods[0][2] = dg1
    dysn = matmul(dmix0, wts["hy_out"], mode="nt", out_dtype=F32, name="hy_out_dy", n=SSM_INNER, b_off=0)
    datt = matmul(dmix0, wts["hy_out"], mode="nt", out_dtype=F32, name="hy_out_da", n=D, b_off=SSM_INNER)
    g["hy_out"] = wgrad(dmix0, cat, "hy_out_dw")
    (dy, dz), (g["hy_ssm_norm_g"],), _ = rowmap_bwd(f_gated_norm, [y, z], [sp["hy_ssm_norm_g"]], [dysn], name="hy_gnorm_b",
                                                    row_grad=[True, True], row_dtypes=[F32, BF16], tr=128)
    if comm is not None:
        comm["ga"] = pack_grads(g, GA_LAYOUT, GA_ROWS)
        comm["recv"] = lax.empty((3, GA_ROWS, D), BF16)
    (dxs, ddtraw_t, g["hy_dt_bias"], g["hy_a_log"], g["hy_d_skip"], dbm, dcm), recv = ssd2_bwd(
        xs, dtraw_t, sp["hy_dt_bias"], sp["hy_a_log"], sp["hy_d_skip"], bm, cm, prev_all, dy, side=g_side(0))
    if comm is not None:
        comm["recv"] = recv
    (dxbc_pre,), _, _ = rowmap_bwd(f_silu_xbc, [xbc_pre], [], [dxs, dbm, dcm], name="hy_conv_act_b", row_grad=[True],
                                   tr=128)
    dxbc_raw, g["hy_conv_w"], g["hy_conv_b"] = conv_bwd(xbc_raw, sp["hy_conv_w"], dxbc_pre, name="hy_conv_b", cb=128, chunk_rows=128, dx_dtype=BF16)
    dol, _, _ = rowmap_bwd(f_combine, att_o + att_l, [], [datt], name="att_combine_b", row_grad=[True] * 6,
                           row_dtypes=[BF16] * 3 + [F32] * 3, tr=128)
    dqs, dks, dvs, dtabs = [], [], [], []
    for p, (win, dil) in enumerate(ATT_PATTERNS):
        qa, ka, va, bias, nb, cols = att_in[p]
        (dq, dkp_, dvp_, dbias), recv = att2_bwd(qa, ka, va, bias, regroup(dol[p], dil), regroup(dol[3 + p], dil), nb,
                                                 cols, name=f"att_bwd{p}", side=g_side(1 + p))
        if comm is not None:
            comm["recv"] = recv
        dqs.append(regroup(dq, dil, inverse=True))
        dks.append(regroup(dkp_, dil, inverse=True))
        dvs.append(regroup(dvp_, dil, inverse=True))
        dtabs.append(att_bias_grad(dbias.reshape(ATT_HEADS, 2, ATT_BLK, ATT_BLK), dil, name=f"att_dtab{p}"))
    g["rel_table"] = jnp.concatenate(dtabs, axis=1)
    dk = rowmap(f_sum3, dks, [], [BF16], name="att_dk_sum")[0]
    dv = rowmap(f_sum3, dvs, [], [BF16], name="att_dv_sum")[0]
    ddt = jnp.pad(ddtraw_t.T, ((0, 0), (0, DT_PAD - HY_DT)))
    dproj = jnp.concatenate([dz, dxbc_raw] + dqs + [dk, dv, ddt.astype(BF16)], axis=-1)
    g["hy_in_t"] = wgrad(h0, dproj, "hy_in_dw")
    if comm is None:
        dh0 = matmul(dproj, w_in, mode="nn", out_dtype=F32, name="hy_in_dx")
    else:
        gb = pack_grads(g, GB_LAYOUT, GB_ROWS)
        half = GB_ROWS // 2
        theirs = swap_halves(gb, name="swap_in_halves")
        ours = lax.dynamic_slice_in_dim(gb, lax.axis_index("c") * half, half, axis=1)
        comm["gb"] = rowmap(f_add, [ours.reshape(N_CHIPS * half, D), theirs.reshape(N_CHIPS * half, D)], [], [BF16],
                            name="sum_in_cores")[0].reshape(N_CHIPS, half, D)
        dh0, comm["recv_b"] = matmul(dproj, w_in, mode="nn", out_dtype=F32, name="hy_in_dx",
                                     side=ScatterRows(comm["gb"], lax.empty((3, half, D), BF16), 0, half))
    (dres,), (dg_, dsc, dsh), _ = rowmap_bwd(f_normmod, [x], [sp["norm_mix_g"][0], mods[0][1], mods[0][0]], [dh0],
                                             name="l0_norm_b", row_grad=[True], row_add=[dres])
    g["norm_mix_g0"] = dg_
    dmods[0][1], dmods[0][0] = dsc, dsh
    return loss_rows, dres, g, dmods


ANY = pl.BlockSpec(memory_space=pl.ANY)
WHOLE_VMEM = pl.BlockSpec(memory_space=pltpu.VMEM)


def _place():
    return lax.axis_index("x"), lax.axis_index("y"), lax.axis_index("c")


def _other_chips(x, y):
    return [(1 - x, y), (x, 1 - y), (1 - x, 1 - y)]


def allgather_small(v, *, name):
    m_per = v.shape[0]

    def body(x_ref, out_ref, send_sems, recv_sems, local_sem):
        x, y, c = _place()
        me, sibling = (x, y, c), (x, y, 1 - c)
        chips = _other_chips(x, y)

        def rows(px, py, pc):
            return out_ref.at[pl.ds((4 * px + 2 * py + pc) * m_per, m_per), :]

        def copy(k, block, to, src=None):
            return pltpu.make_async_remote_copy(
                src_ref=rows(*block) if src is None else src, dst_ref=rows(*block),
                send_sem=send_sems.at[k], recv_sem=recv_sems.at[k], device_id=to, device_id_type=MESH)

        mine = pltpu.make_async_copy(x_ref, rows(*me), local_sem)
        mine.start()
        first = [copy(0, me, sibling, src=x_ref)]
        first += [copy(1 + j, me, (*chip, c), src=x_ref) for j, chip in enumerate(chips)]
        for cp in first:
            cp.start()
        passed = [copy(4 + j, (*chip, c), sibling) for j, chip in enumerate(chips)]
        for j, chip in enumerate(chips):
            copy(1 + j, (*chip, c), me).wait_recv()
            passed[j].start()
        copy(0, sibling, me).wait_recv()
        for j, chip in enumerate(chips):
            copy(4 + j, (*chip, 1 - c), me).wait_recv()
        for cp in first + passed:
            cp.wait_send()
        mine.wait()

    return pl.pallas_call(
        body, name=name,
        out_shape=jax.ShapeDtypeStruct((N_DEV * m_per, LANES), v.dtype),
        in_specs=[WHOLE_VMEM], out_specs=WHOLE_VMEM,
        scratch_shapes=[pltpu.SemaphoreType.DMA((7,)), pltpu.SemaphoreType.DMA((7,)), pltpu.SemaphoreType.DMA],
    )(v)


def allgather_chips(pack, *, name):
    half_rows = pack.shape[0] // 2

    def body(p_ref, o_ref, send_sems, recv_sems, local_sem):
        x, y, c = _place()
        chips = _other_chips(x, y)
        sibling = (x, y, 1 - c)
        my_half = pl.ds(c * half_rows, half_rows)
        its_half = pl.ds((1 - c) * half_rows, half_rows)
        mine = pltpu.make_async_copy(p_ref, o_ref.at[2 * x + y], local_sem)
        mine.start()
        sends = [pltpu.make_async_remote_copy(
            src_ref=p_ref.at[my_half], dst_ref=o_ref.at[2 * x + y, my_half],
            send_sem=send_sems.at[k], recv_sem=recv_sems.at[k],
            device_id=(cx, cy, c), device_id_type=MESH) for k, (cx, cy) in enumerate(chips)]
        for cp in sends:
            cp.start()
        passed = []
        for k, (cx, cy) in enumerate(chips):
            landed = o_ref.at[2 * cx + cy, my_half]
            pltpu.make_async_remote_copy(
                src_ref=p_ref.at[my_half], dst_ref=landed, send_sem=send_sems.at[k], recv_sem=recv_sems.at[k],
                device_id=(cx, cy, c), device_id_type=MESH).wait_recv()
            cp = pltpu.make_async_remote_copy(
                src_ref=landed, dst_ref=landed, send_sem=send_sems.at[3 + k], recv_sem=recv_sems.at[3 + k],
                device_id=sibling, device_id_type=MESH)
            cp.start()
            passed.append(cp)
        for k, (cx, cy) in enumerate(chips):
            from_sibling = o_ref.at[2 * cx + cy, its_half]
            pltpu.make_async_remote_copy(
                src_ref=from_sibling, dst_ref=from_sibling, send_sem=send_sems.at[3 + k], recv_sem=recv_sems.at[3 + k],
                device_id=sibling, device_id_type=MESH).wait_recv()
        for cp in sends + passed:
            cp.wait_send()
        mine.wait()

    return pl.pallas_call(
        body, name=name,
        out_shape=jax.ShapeDtypeStruct((N_CHIPS,) + pack.shape, pack.dtype),
        in_specs=[ANY], out_specs=ANY,
        scratch_shapes=[pltpu.SemaphoreType.DMA((6,)), pltpu.SemaphoreType.DMA((6,)), pltpu.SemaphoreType.DMA],
    )(pack)


def swap_halves(gpack, *, name):
    half_rows = gpack.shape[1] // 2

    def body(g_ref, r_ref, send_sems, recv_sems):
        x, y, c = _place()
        its_half = pl.ds((1 - c) * half_rows, half_rows)
        copies = [pltpu.make_async_remote_copy(
            src_ref=g_ref.at[s, its_half], dst_ref=r_ref.at[s], send_sem=send_sems.at[s], recv_sem=recv_sems.at[s],
            device_id=(x, y, 1 - c), device_id_type=MESH) for s in range(N_CHIPS)]
        for cp in copies:
            cp.start()
        for cp in copies:
            cp.wait()

    return pl.pallas_call(
        body, name=name,
        out_shape=jax.ShapeDtypeStruct((N_CHIPS, half_rows) + gpack.shape[2:], gpack.dtype),
        in_specs=[ANY], out_specs=ANY,
        scratch_shapes=[pltpu.SemaphoreType.DMA((N_CHIPS,)), pltpu.SemaphoreType.DMA((N_CHIPS,))],
    )(gpack)


def scatter_chips(gpack, *, name):
    def body(g_ref, own_ref, recv_ref, send_sems, recv_sems, local_sem):
        x, y, c = _place()
        chips = _other_chips(x, y)
        mine = pltpu.make_async_copy(g_ref.at[2 * x + y], own_ref, local_sem)
        mine.start()
        sends = [pltpu.make_async_remote_copy(
            src_ref=g_ref.at[2 * cx + cy], dst_ref=recv_ref.at[k], send_sem=send_sems.at[k], recv_sem=recv_sems.at[k],
            device_id=(cx, cy, c), device_id_type=MESH) for k, (cx, cy) in enumerate(chips)]
        for cp in sends:
            cp.start()
        for cp in sends:
            cp.wait_recv()
        for cp in sends:
            cp.wait_send()
        mine.wait()

    slot = jax.ShapeDtypeStruct(gpack.shape[1:], gpack.dtype)
    return pl.pallas_call(
        body, name=name,
        out_shape=[slot, jax.ShapeDtypeStruct((3,) + gpack.shape[1:], gpack.dtype)],
        in_specs=[ANY], out_specs=[ANY, ANY],
        scratch_shapes=[pltpu.SemaphoreType.DMA((3,)), pltpu.SemaphoreType.DMA((3,)), pltpu.SemaphoreType.DMA],
    )(gpack)


class GatherRows:
    def __init__(self, pack, full, lo, hi):
        assert (hi - lo) % 32 == 0 and lo % 16 == 0
        self.src, self.dst, self.lo, self.hi = pack, full, lo, hi

    def sems(self):
        return [pltpu.SemaphoreType.DMA((6,)), pltpu.SemaphoreType.DMA((6,)), pltpu.SemaphoreType.DMA]

    def _parts(self, pack_ref, full_ref, sems):
        send_sems, recv_sems, local_sem = sems
        x, y, c = _place()
        half = (self.hi - self.lo) // 2
        mine, its = pl.ds(self.lo + c * half, half), pl.ds(self.lo + (1 - c) * half, half)
        rows = pl.ds(self.lo, self.hi - self.lo)
        local = pltpu.make_async_copy(pack_ref.at[rows], full_ref.at[2 * x + y, rows], local_sem)
        chips = _other_chips(x, y)

        def remote(src, dst, k, to):
            return pltpu.make_async_remote_copy(src_ref=src, dst_ref=dst, send_sem=send_sems.at[k],
                                                recv_sem=recv_sems.at[k], device_id=to, device_id_type=MESH)

        sends = [remote(pack_ref.at[mine], full_ref.at[2 * x + y, mine], k, (cx, cy, c)) for k, (cx, cy) in enumerate(chips)]
        landed = [full_ref.at[2 * cx + cy, mine] for cx, cy in chips]
        arrive = [remote(pack_ref.at[mine], landed[k], k, (cx, cy, c)) for k, (cx, cy) in enumerate(chips)]
        passed = [remote(landed[k], landed[k], 3 + k, (x, y, 1 - c)) for k in range(3)]
        from_sibling = [remote(landed[k], full_ref.at[2 * cx + cy, its], 3 + k, (x, y, 1 - c))
                        for k, (cx, cy) in enumerate(chips)]
        return local, sends, arrive, passed, from_sibling

    def start(self, pack_ref, full_ref, sems):
        local, sends, _, _, _ = self._parts(pack_ref, full_ref, sems)
        local.start()
        for cp in sends:
            cp.start()

    def finish(self, pack_ref, full_ref, sems):
        local, sends, arrive, passed, from_sibling = self._parts(pack_ref, full_ref, sems)
        for k in range(3):
            arrive[k].wait_recv()
            passed[k].start()
        for cp in from_sibling:
            cp.wait_recv()
        for cp in sends + passed:
            cp.wait_send()
        local.wait()


class ScatterRows:
    def __init__(self, gpack, recv, lo, hi):
        assert lo % 16 == 0 and hi % 16 == 0
        self.src, self.dst, self.lo, self.hi = gpack, recv, lo, hi

    def sems(self):
        return [pltpu.SemaphoreType.DMA((3,)), pltpu.SemaphoreType.DMA((3,))]

    def _parts(self, g_ref, recv_ref, sems):
        send_sems, recv_sems = sems
        x, y, c = _place()
        rows = pl.ds(self.lo, self.hi - self.lo)
        return [pltpu.make_async_remote_copy(
            src_ref=g_ref.at[2 * cx + cy, rows], dst_ref=recv_ref.at[k, rows], send_sem=send_sems.at[k],
            recv_sem=recv_sems.at[k], device_id=(cx, cy, c), device_id_type=MESH)
            for k, (cx, cy) in enumerate(_other_chips(x, y))]

    def start(self, g_ref, recv_ref, sems):
        for cp in self._parts(g_ref, recv_ref, sems):
            cp.start()

    def finish(self, g_ref, recv_ref, sems):
        sends = self._parts(g_ref, recv_ref, sems)
        for cp in sends:
            cp.wait_recv()
        for cp in sends:
            cp.wait_send()


def side_call(side, *, name):
    def body(src_ref, dst_in_ref, dst_ref, *sems):
        side.start(src_ref, dst_ref, sems)
        side.finish(src_ref, dst_ref, sems)

    return pl.pallas_call(
        body, name=name, out_shape=jax.ShapeDtypeStruct(side.dst.shape, side.dst.dtype),
        in_specs=[ANY, ANY], out_specs=ANY, scratch_shapes=side.sems(), input_output_aliases={1: 0},
    )(side.src, side.dst)


def grid_call(body, args, *, name, out_shape, grid, in_specs, out_specs, scratch_shapes, semantics, side=None):
    if side is None:
        res = pl.pallas_call(body, name=name, out_shape=out_shape, grid=grid, in_specs=in_specs, out_specs=out_specs,
                             scratch_shapes=scratch_shapes, compiler_params=_cparams(*semantics))(*args)
        return res, None
    n_in, n_out, n_scr = len(args), len(out_shape), len(scratch_shapes)

    def wrapped(*refs):
        ins, (src_ref, _) = refs[:n_in], refs[n_in:n_in + 2]
        outs, dst_ref = refs[n_in + 2:n_in + 2 + n_out], refs[n_in + 2 + n_out]
        scr, sems = refs[n_in + 3 + n_out:n_in + 3 + n_out + n_scr], refs[n_in + 3 + n_out + n_scr:]
        first = functools.reduce(jnp.logical_and, [pl.program_id(i) == 0 for i in range(len(grid))])
        last = functools.reduce(jnp.logical_and, [pl.program_id(i) == n - 1 for i, n in enumerate(grid)])

        @pl.when(first)
        def _():
            side.start(src_ref, dst_ref, sems)

        body(*ins, *outs, *scr)

        @pl.when(last)
        def _():
            side.finish(src_ref, dst_ref, sems)

    res = pl.pallas_call(
        wrapped, name=name,
        out_shape=list(out_shape) + [jax.ShapeDtypeStruct(side.dst.shape, side.dst.dtype)],
        grid=grid, in_specs=list(in_specs) + [ANY, ANY], out_specs=list(out_specs) + [ANY],
        scratch_shapes=list(scratch_shapes) + side.sems(), input_output_aliases={n_in + 1: n_out},
        compiler_params=_cparams(*(["arbitrary"] * len(grid))),
    )(*args, side.src, side.dst)
    return res[:-1], res[-1]


def sibling_swap(p, *, name):
    def body(p_ref, r_ref, send_sem, recv_sem):
        x, y, c = _place()
        cp = pltpu.make_async_remote_copy(src_ref=p_ref, dst_ref=r_ref, send_sem=send_sem, recv_sem=recv_sem,
                                          device_id=(x, y, 1 - c), device_id_type=MESH)
        cp.start()
        cp.wait()

    return pl.pallas_call(
        body, name=name, out_shape=jax.ShapeDtypeStruct(p.shape, p.dtype),
        in_specs=[ANY], out_specs=ANY,
        scratch_shapes=[pltpu.SemaphoreType.DMA, pltpu.SemaphoreType.DMA],
    )(p)


def sum_slots(own, recv, *, name):
    r_dim, c_dim = own.shape
    tr = _pick(r_dim, 256, mult=16)

    def body(o_ref, r_ref, out_ref):
        acc = o_ref[...].astype(F32)
        for k in range(3):
            acc = acc + r_ref[k].astype(F32)
        out_ref[...] = acc

    return pl.pallas_call(
        body, name=name, out_shape=jax.ShapeDtypeStruct((r_dim, c_dim), F32), grid=(r_dim // tr,),
        in_specs=[pl.BlockSpec((tr, c_dim), lambda i: (i, 0)), pl.BlockSpec((3, tr, c_dim), lambda i: (0, i, 0))],
        out_specs=pl.BlockSpec((tr, c_dim), lambda i: (i, 0)),
        compiler_params=_cparams("parallel"),
    )(own, recv)


def sum_devices(v_all, *, name):
    m_per = v_all.shape[0] // N_DEV

    def body(v_ref, o_ref):
        acc = v_ref[pl.ds(0, m_per), :]
        for d in range(1, N_DEV):
            acc = acc + v_ref[pl.ds(d * m_per, m_per), :]
        o_ref[...] = acc

    return pl.pallas_call(
        body, name=name, out_shape=jax.ShapeDtypeStruct((m_per, LANES), F32),
        in_specs=[WHOLE_VMEM], out_specs=WHOLE_VMEM,
    )(v_all)


WEIGHTS = ['ada_w', 'ada_b', 'norm_mix_g', 'norm_ffn_g', 'hy_w_in', 'hy_conv_w', 'hy_conv_b', 'hy_dt_bias', 'hy_a_log',
           'hy_d_skip', 'hy_ssm_norm_g', 'hy_w_out', 'rel_table', 'cv_w_pw1', 'cv_b_pw1', 'cv_w_dw', 'cv_b_dw', 'cv_ln_g',
           'cv_ln_b', 'cv_w_pw2', 'cv_b_pw2', 'ffn_w_gate', 'ffn_w_up', 'ffn_w_down', 'final_norm_g']
BIG = ('ada_w', 'hy_w_in', 'hy_w_out', 'cv_w_pw1', 'cv_w_pw2', 'ffn_w_gate', 'ffn_w_up', 'ffn_w_down')
SMALL_SHARDED = {'hy_conv_w': (1, 4, 3072), 'cv_b_pw1': (1, 2048), 'cv_w_dw': (1, 31, 1024), 'cv_b_dw': (1, 1024),
                 'cv_ln_g': (1, 1024), 'cv_ln_b': (1, 1024), 'cv_b_pw2': (1, 1024)}
SMALL_GRADS = {'ada_b': (2, 6144), 'norm_mix_g': (2, 1024), 'norm_ffn_g': (2, 1024), 'hy_conv_w': (1, 4, 3072),
               'hy_conv_b': (1, 3072), 'hy_dt_bias': (1, 32), 'hy_a_log': (1, 32), 'hy_d_skip': (1, 32),
               'hy_ssm_norm_g': (1, 2048), 'rel_table': (32, 48), 'cv_b_pw1': (1, 2048), 'cv_w_dw': (1, 31, 1024),
               'cv_b_dw': (1, 1024), 'cv_ln_g': (1, 1024), 'cv_ln_b': (1, 1024), 'cv_b_pw2': (1, 1024),
               'final_norm_g': (1024,), 'loss': (1,)}

PACK_LAYOUT = (('hy_in_t', 2568), ('hy_out', 768), ('pw1_t', 512), ('pw2', 256),
               ('gate_t0', 704), ('up_t0', 704), ('down0', 704), ('gate_t1', 704), ('up_t1', 704), ('down1', 704))
PACK_ROWS = 8448


def _pack_offsets(layout):
    off, out = 0, {}
    for nm, r in layout:
        out[nm] = (off, r)
        off += r
    return out


PACK_OFF = _pack_offsets(PACK_LAYOUT)
W_BATCHES = ((0, 2624), (2624, 5248), (5248, 6336), (6336, 7424), (7424, 8448))
GA_LAYOUT = PACK_LAYOUT[1:]
GA_ROWS = 5888
GA_OFF = _pack_offsets(GA_LAYOUT)
G_BATCHES = ((0, 2560), (2560, 3712), (3712, 4864), (4864, 5888))
GB_LAYOUT = PACK_LAYOUT[:1]
GB_ROWS = 2816


def pack_grads(g, layout, n_rows):
    def rows_bf16(nm):
        return g[nm]

    parts = []
    for key, r in layout:
        if key == 'hy_in_t':
            a = hy_from_cat(rows_bf16('hy_in_t'))
        elif key.startswith('gate_t'):
            a = rows_bf16('gu_t' + key[-1])[:FFN_HIDDEN]
        elif key.startswith('up_t'):
            a = rows_bf16('gu_t' + key[-1])[FFN_HIDDEN:]
        else:
            a = rows_bf16(key)
        parts.append(a.reshape(N_CHIPS, r, D))
    used = sum(r for _, r in layout)
    return jnp.concatenate(parts + [jnp.zeros((N_CHIPS, n_rows - used, D), BF16)], axis=1)


def unpack_weights(full, skip=()):
    def whole(nm):
        o, r = PACK_OFF[nm]
        return full[:, o:o + r].reshape(N_CHIPS * r, D)

    out = {"hy_out": whole('hy_out'), "pw1_t": whole('pw1_t'), "pw2": whole('pw2'),
           "gu_t": [jnp.concatenate([whole(f'gate_t{i}'), whole(f'up_t{i}')], axis=0) for i in range(2)],
           "down": [whole(f'down{i}') for i in range(2)]}
    if "hy_in_t" not in skip:
        out["hy_in_t"] = hy_to_cat(whole('hy_in_t'))
    return out


def _to_lanes(flat):
    n = flat.shape[0]
    m = -(-n // (8 * LANES)) * 8
    return jnp.pad(flat, (0, m * LANES - n)).reshape(m, LANES)


def _split(flat, shapes):
    out, off = {}, 0
    for nm, shp in shapes.items():
        n = int(np.prod(shp))
        out[nm] = flat[off:off + n].reshape(shp)
        off += n
    return out


def kernel(x, c, ada_w, ada_b, norm_mix_g, norm_ffn_g, hy_w_in, hy_conv_w, hy_conv_b, hy_dt_bias, hy_a_log, hy_d_skip, hy_ssm_norm_g, hy_w_out, rel_table, cv_w_pw1, cv_b_pw1, cv_w_dw, cv_b_dw, cv_ln_g, cv_ln_b, cv_w_pw2, cv_b_pw2, ffn_w_gate, ffn_w_up, ffn_w_down, final_norm_g, loss_target, m_ada_w, m_ada_b, m_norm_mix_g, m_norm_ffn_g, m_hy_w_in, m_hy_conv_w, m_hy_conv_b, m_hy_dt_bias, m_hy_a_log, m_hy_d_skip, m_hy_ssm_norm_g, m_hy_w_out, m_rel_table, m_cv_w_pw1, m_cv_b_pw1, m_cv_w_dw, m_cv_b_dw, m_cv_ln_g, m_cv_ln_b, m_cv_w_pw2, m_cv_b_pw2, m_ffn_w_gate, m_ffn_w_up, m_ffn_w_down, m_final_norm_g, v_ada_w, v_ada_b, v_norm_mix_g, v_norm_ffn_g, v_hy_w_in, v_hy_conv_w, v_hy_conv_b, v_hy_dt_bias, v_hy_a_log, v_hy_d_skip, v_hy_ssm_norm_g, v_hy_w_out, v_rel_table, v_cv_w_pw1, v_cv_b_pw1, v_cv_w_dw, v_cv_b_dw, v_cv_ln_g, v_cv_ln_b, v_cv_w_pw2, v_cv_b_pw2, v_ffn_w_gate, v_ffn_w_up, v_ffn_w_down, v_final_norm_g):
    args = (x, c, ada_w, ada_b, norm_mix_g, norm_ffn_g, hy_w_in, hy_conv_w, hy_conv_b, hy_dt_bias, hy_a_log, hy_d_skip, hy_ssm_norm_g, hy_w_out, rel_table, cv_w_pw1, cv_b_pw1, cv_w_dw, cv_b_dw, cv_ln_g, cv_ln_b, cv_w_pw2, cv_b_pw2, ffn_w_gate, ffn_w_up, ffn_w_down, final_norm_g, loss_target, m_ada_w, m_ada_b, m_norm_mix_g, m_norm_ffn_g, m_hy_w_in, m_hy_conv_w, m_hy_conv_b, m_hy_dt_bias, m_hy_a_log, m_hy_d_skip, m_hy_ssm_norm_g, m_hy_w_out, m_rel_table, m_cv_w_pw1, m_cv_b_pw1, m_cv_w_dw, m_cv_b_dw, m_cv_ln_g, m_cv_ln_b, m_cv_w_pw2, m_cv_b_pw2, m_ffn_w_gate, m_ffn_w_up, m_ffn_w_down, m_final_norm_g, v_ada_w, v_ada_b, v_norm_mix_g, v_norm_ffn_g, v_hy_w_in, v_hy_conv_w, v_hy_conv_b, v_hy_dt_bias, v_hy_a_log, v_hy_d_skip, v_hy_ssm_norm_g, v_hy_w_out, v_rel_table, v_cv_w_pw1, v_cv_b_pw1, v_cv_w_dw, v_cv_b_dw, v_cv_ln_g, v_cv_ln_b, v_cv_w_pw2, v_cv_b_pw2, v_ffn_w_gate, v_ffn_w_up, v_ffn_w_down, v_final_norm_g)
    x_in, c_in = args[0], args[1]
    w = dict(zip(WEIGHTS, args[2:27], strict=True))
    tgt = args[27]
    m_in = dict(zip(WEIGHTS, args[28:53], strict=True))
    v_in = dict(zip(WEIGHTS, args[53:78], strict=True))
    xi, yi, ci = _place()
    chip = 2 * xi + yi
    dev = 2 * chip + ci

    cs = rowmap(f_silu, [c_in.reshape(8, LANES)], [], [F32], name="cond_silu", tr=8)[0]
    cs_all = allgather_small(cs, name="gather_cond").reshape(N_DEV, D)
    cs16 = jnp.pad(cs_all, ((0, 8), (0, 0)))
    modpart = jnp.stack([matmul(cs16, w['ada_w'][i], mode="nn", out_dtype=F32, name=f"ada_fwd{i}")[:N_DEV]
                         for i in range(2)], axis=1)
    shard_names = list(SMALL_SHARDED)
    payload = jnp.concatenate([modpart.reshape(-1)] + [w[nm].reshape(-1) for nm in shard_names])
    got = allgather_small(_to_lanes(payload), name="gather_mod").reshape(N_DEV, -1)[0::2]
    modparts = got[:, :modpart.size].reshape(N_CHIPS, N_DEV, 2, 1536)
    mine = lax.dynamic_index_in_dim(modparts, dev, axis=1, keepdims=False)
    mod = jnp.transpose(mine, (1, 0, 2)).reshape(2, 6 * D) + w['ada_b']
    mods = [[mod[i, j * D:(j + 1) * D].reshape(1, D) for j in range(6)] for i in range(2)]
    sp = {}
    off = modpart.size
    for nm in shard_names:
        shp = w[nm].shape
        n = int(np.prod(shp))
        parts = got[:, off:off + n].reshape((N_CHIPS,) + shp)
        sp[nm] = jnp.concatenate([parts[s] for s in range(N_CHIPS)], axis=-1)
        off += n

    def rows_of(nm, i=None):
        a = w[nm][0 if i is None else i]
        return (a.T if nm in ('hy_w_in', 'cv_w_pw1', 'ffn_w_gate', 'ffn_w_up') else a).astype(BF16)

    pieces = [rows_of('hy_w_in'), rows_of('hy_w_out'), rows_of('cv_w_pw1'), rows_of('cv_w_pw2')]
    for i in range(2):
        pieces += [rows_of('ffn_w_gate', i), rows_of('ffn_w_up', i), rows_of('ffn_w_down', i)]
    n_rows = sum(p.shape[0] for p in pieces)
    pack = jnp.concatenate(pieces + [jnp.zeros((PACK_ROWS - n_rows, D), BF16)], axis=0)
    full = side_call(GatherRows(pack, lax.empty((N_CHIPS, PACK_ROWS, D), BF16), *W_BATCHES[0]), name="gather_weights")
    o_in, r_in = PACK_OFF['hy_in_t']
    wts = {"hy_in_t": hy_to_cat(full[:, o_in:o_in + r_in].reshape(N_CHIPS * r_in, D))}
    comm = {"pack": pack, "full": full}

    sp = {"norm_mix_g": [w['norm_mix_g'][i].reshape(1, D) for i in range(2)],
          "norm_ffn_g": [w['norm_ffn_g'][i].reshape(1, D) for i in range(2)],
          "hy_conv_w": sp['hy_conv_w'][0], "hy_conv_b": w['hy_conv_b'],
          "hy_dt_bias": w['hy_dt_bias'].reshape(SSM_HEADS, 1), "hy_a_log": w['hy_a_log'].reshape(SSM_HEADS, 1),
          "hy_d_skip": w['hy_d_skip'].reshape(SSM_HEADS, 1), "hy_ssm_norm_g": w['hy_ssm_norm_g'],
          "rel_table": w['rel_table'], "cv_b_pw1": sp['cv_b_pw1'], "cv_w_dw": sp['cv_w_dw'][0], "cv_b_dw": sp['cv_b_dw'],
          "cv_ln_g": sp['cv_ln_g'], "cv_ln_b": sp['cv_ln_b'], "cv_b_pw2": sp['cv_b_pw2'],
          "final_norm_g": w['final_norm_g'].reshape(1, D)}

    loss_rows, grad_x, g, dmods = device_step(x_in[0], tgt[0], mods, wts, sp, comm)

    dmod = jnp.stack([jnp.concatenate([d.reshape(-1) for d in dmods[i]]) for i in range(2)])
    small = {'ada_b': dmod, 'norm_mix_g': jnp.stack([g[f'norm_mix_g{i}'].reshape(-1) for i in range(2)]),
             'norm_ffn_g': jnp.stack([g[f'norm_ffn_g{i}'].reshape(-1) for i in range(2)]),
             'loss': jnp.sum(loss_rows).reshape(1)}
    for nm in SMALL_GRADS:
        if nm not in small:
            small[nm] = g[nm]
    vec = _to_lanes(jnp.concatenate([small[nm].reshape(-1) for nm in SMALL_GRADS]))
    vec_all = allgather_small(vec, name="gather_small_grads")
    tot = _split(sum_devices(vec_all, name="sum_small_grads").reshape(-1), SMALL_GRADS)
    dmod_all = vec_all.reshape(N_DEV, -1)[:, :2 * 6 * D].reshape(N_DEV, 2, 6 * D)

    recv = comm["recv"]
    own_a = lax.dynamic_index_in_dim(comm["ga"], chip, axis=0, keepdims=False)
    part_a = sum_slots(own_a, recv, name="sum_chip_grads")
    red_a = rowmap(f_add, [part_a, sibling_swap(part_a, name="swap_grads")], [], [F32], name="sum_core_grads")[0]
    recv_b = comm["recv_b"]
    own_b = lax.dynamic_index_in_dim(comm["gb"], chip, axis=0, keepdims=False)
    mine_half = sum_slots(own_b, recv_b, name="sum_in_chips")
    its_half = sibling_swap(mine_half, name="swap_in")
    red_b = jnp.concatenate([jnp.where(ci == 0, mine_half, its_half), jnp.where(ci == 0, its_half, mine_half)], axis=0)

    def shard_grad(nm, i=None):
        key = {'hy_w_in': 'hy_in_t', 'hy_w_out': 'hy_out', 'cv_w_pw1': 'pw1_t', 'cv_w_pw2': 'pw2'}.get(nm)
        if key is None:
            key = {'ffn_w_gate': 'gate_t', 'ffn_w_up': 'up_t', 'ffn_w_down': 'down'}[nm] + str(i)
        if key == 'hy_in_t':
            a = red_b[:PACK_OFF[key][1]]
        else:
            o, r = GA_OFF[key]
            a = red_a[o:o + r]
        return a.T if key.endswith('_t') or key[:-1].endswith('_t') else a

    grads = {}
    grads['hy_w_in'] = shard_grad('hy_w_in')[None]
    grads['hy_w_out'] = shard_grad('hy_w_out')[None]
    grads['cv_w_pw1'] = shard_grad('cv_w_pw1')[None]
    grads['cv_w_pw2'] = shard_grad('cv_w_pw2')[None]
    for nm in ('ffn_w_gate', 'ffn_w_up', 'ffn_w_down'):
        grads[nm] = jnp.stack([shard_grad(nm, i) for i in range(2)])
    cs16 = jnp.pad(cs_all, ((0, 8), (0, 0)))
    dm_mine = lax.dynamic_slice_in_dim(dmod_all, chip * 1536, 1536, axis=2)
    dm16 = jnp.pad(dm_mine, ((0, 8), (0, 0), (0, 0)))
    grads['ada_w'] = jnp.stack([matmul(cs16, dm16[:, i], mode="tn", out_dtype=F32, name=f"ada_dw{i}") for i in range(2)])
    for nm, shp in SMALL_GRADS.items():
        if nm == 'loss':
            continue
        if nm in SMALL_SHARDED:
            n = w[nm].shape[-1]
            grads[nm] = lax.dynamic_slice_in_dim(tot[nm], chip * n, n, axis=len(shp) - 1)
        else:
            grads[nm] = tot[nm].reshape(w[nm].shape)

    delta, new_m, new_v = {}, {}, {}
    for nm in BIG:
        delta[nm], new_m[nm], new_v[nm] = adamw(w[nm], grads[nm], m_in[nm], v_in[nm], name="adamw_" + nm)
    smalls = [nm for nm in WEIGHTS if nm not in BIG]
    packed = [_to_lanes(jnp.concatenate([d[nm].reshape(-1) for nm in smalls])) for d in (w, grads, m_in, v_in)]
    res = rowmap(f_adamw, packed, [], [F32] * 3, name="adamw_small", tr=_rows_tile(packed[0].shape[0]))
    for d, r in zip((delta, new_m, new_v), res, strict=True):
        d.update(_split(r.reshape(-1), {nm: w[nm].shape for nm in smalls}))

    loss = tot['loss'].reshape(())
    return (loss, grad_x[None], *[grads[nm] for nm in WEIGHTS], *[delta[nm] for nm in WEIGHTS],
            *[new_m[nm] for nm in WEIGHTS], *[new_v[nm] for nm in WEIGHTS])
```

```python
import functools
import math

import jax
import jax.numpy as jnp
import numpy as np
from jax import lax
from jax.experimental import pallas as pl
from jax.experimental.pallas import tpu as pltpu

F32 = jnp.float32
BF16 = jnp.bfloat16
MESH = pl.DeviceIdType.MESH

D = 1024
S = 4096
EPS = 1e-6
SSM_INNER = 2048
SSM_HEADS = 32
SSM_HDIM = 64
SSM_GROUPS = 4
SSM_STATE = 128
SSM_CONVK = 4
SSM_CONV_DIM = 3072
CHUNK = 128
N_CHUNKS = S // CHUNK
ATT_HEADS = 16
ATT_HDIM = 64
ATT_PATTERNS = ((128, 1), (512, 4), (2048, 16))
ATT_BLK = 128
REL_BUCKETS = 32
REL_MAX_DIST = 2048
CONV_WIDTH = 31
FFN_HIDDEN = 2816
N_CHIPS = 4
N_DEV = 8
ADAM_LR, ADAM_B1, ADAM_B2, ADAM_EPS, ADAM_WD, ADAM_STEP = 0.001, 0.9, 0.999, 1e-08, 0.01, 10

VMEM_LIMIT_BYTES = 56 * 1024 * 1024
LANES = 128


def _cparams(*sem):
    return pltpu.CompilerParams(dimension_semantics=sem, vmem_limit_bytes=VMEM_LIMIT_BYTES)


def _pick(n, cap, mult=LANES):
    best = None
    for t in range(mult, min(n, cap) + 1, mult):
        if n % t == 0:
            best = t
    return best or n


def _dot(a, b, ca, cb):
    return lax.dot_general(a.astype(BF16), b.astype(BF16), (((ca,), (cb,)), ((), ())), preferred_element_type=F32)


@jax.custom_vjp
def mm_nt(a, b):
    return _dot(a, b, 1, 1)


def _mm_nt_fwd(a, b):
    return _dot(a, b, 1, 1), (a, b)


def _mm_nt_bwd(res, g):
    a, b = res
    return _dot(g, b, 1, 0).astype(a.dtype), _dot(g, a, 0, 0).astype(b.dtype)


mm_nt.defvjp(_mm_nt_fwd, _mm_nt_bwd)


def matmul(a, b, *, mode, out_dtype, name, n=None, b_off=0, tm_cap=1024, tn_cap=512, tk_cap=3584, side=None,
           out_t=False):
    if mode == "tn":
        k_dim, m_dim = a.shape
    else:
        m_dim, k_dim = a.shape
    n_dim = n if n is not None else (b.shape[0] if mode == "nt" else b.shape[1])
    tm = m_dim if m_dim < LANES else _pick(m_dim, tm_cap)
    tn = _pick(n_dim, tn_cap)
    tk = k_dim if k_dim < LANES else _pick(k_dim, tk_cap)
    assert m_dim % tm == 0 and n_dim % tn == 0 and k_dim % tk == 0 and b_off % tn == 0
    nk = k_dim // tk
    off = b_off // tn
    if mode == "nn":
        a_spec = pl.BlockSpec((tm, tk), lambda i, j, k: (i, k))
        b_spec = pl.BlockSpec((tk, tn), lambda i, j, k: (k, j))
        ca, cb = 1, 0
    elif mode == "nt":
        a_spec = pl.BlockSpec((tm, tk), lambda i, j, k: (i, k))
        b_spec = pl.BlockSpec((tn, tk), lambda i, j, k: (j + off, k))
        ca, cb = 1, 1
    else:
        a_spec = pl.BlockSpec((tk, tm), lambda i, j, k: (k, i))
        b_spec = pl.BlockSpec((tk, tn), lambda i, j, k: (k, j))
        ca, cb = 0, 0

    def emit(o_ref, val):
        o_ref[...] = (val.T if out_t else val).astype(o_ref.dtype)

    def body(a_ref, b_ref, o_ref, acc_ref):
        part = _dot(a_ref[...], b_ref[...], ca, cb)
        if nk == 1:
            emit(o_ref, part)
        else:
            k = pl.program_id(2)

            @pl.when(k == 0)
            def _():
                acc_ref[...] = part

            @pl.when(k > 0)
            def _():
                acc_ref[...] += part

            @pl.when(k == nk - 1)
            def _():
                emit(o_ref, acc_ref[...])

    if out_t:
        out_shape, out_spec = (n_dim, m_dim), pl.BlockSpec((tn, tm), lambda i, j, k: (j, i))
    else:
        out_shape, out_spec = (m_dim, n_dim), pl.BlockSpec((tm, tn), lambda i, j, k: (i, j))
    (out,), side_dst = grid_call(
        body, (a, b), name=name,
        out_shape=[jax.ShapeDtypeStruct(out_shape, out_dtype)],
        grid=(m_dim // tm, n_dim // tn, nk),
        in_specs=[a_spec, b_spec],
        out_specs=[out_spec],
        scratch_shapes=[pltpu.VMEM((tm, tn), F32)],
        semantics=("parallel", "parallel", "arbitrary"), side=side)
    return out if side is None else (out, side_dst)


def _f32(xs):
    return [x.astype(F32) for x in xs]


def rowmap(f, rows, consts, out_dtypes, *, name, tr=256):
    r_dim = rows[0].shape[0]
    tr = _pick(r_dim, tr, mult=8)
    assert r_dim % tr == 0
    nr, nc = len(rows), len(consts)
    outs = jax.eval_shape(lambda *xs: f(*xs), *[jax.ShapeDtypeStruct((tr, x.shape[1]), F32) for x in rows],
                          *[jax.ShapeDtypeStruct(x.shape, F32) for x in consts])

    def body(*refs):
        res = f(*_f32([r[...] for r in refs[:nr + nc]]))
        for o_ref, o in zip(refs[nr + nc:], res, strict=True):
            o_ref[...] = o.astype(o_ref.dtype)

    return pl.pallas_call(
        body, name=name,
        out_shape=[jax.ShapeDtypeStruct((r_dim, o.shape[1]), dt) for o, dt in zip(outs, out_dtypes, strict=True)],
        grid=(r_dim // tr,),
        in_specs=[pl.BlockSpec((tr, x.shape[1]), lambda i: (i, 0)) for x in rows]
        + [pl.BlockSpec(x.shape, lambda i: (0, 0)) for x in consts],
        out_specs=[pl.BlockSpec((tr, o.shape[1]), lambda i: (i, 0)) for o in outs],
        compiler_params=_cparams("parallel"),
    )(*rows, *consts)


def rowmap_bwd(f, rows, consts, cts, *, name, row_grad, row_dtypes=None, tr=256, emit=(), row_add=None):
    r_dim = rows[0].shape[0]
    tr = _pick(r_dim, tr, mult=8)
    assert r_dim % tr == 0
    nr, nc, nct = len(rows), len(consts), len(cts)
    gi = [i for i, flag in enumerate(row_grad) if flag]
    row_dtypes = row_dtypes or [F32] * len(gi)
    row_add = row_add or [None] * len(gi)
    adds = [a for a in row_add if a is not None]
    outs = jax.eval_shape(lambda *xs: f(*xs), *[jax.ShapeDtypeStruct((tr, x.shape[1]), F32) for x in rows],
                          *[jax.ShapeDtypeStruct(x.shape, F32) for x in consts])

    def body(*refs):
        ins = _f32([r[...] for r in refs[:nr + nc]])
        ct = _f32([r[...] for r in refs[nr + nc:nr + nc + nct]])
        add_refs = list(refs[nr + nc + nct:nr + nc + nct + len(adds)])
        o_refs = refs[nr + nc + nct + len(adds):]
        res, vjp = jax.vjp(f, *ins)
        grads = vjp(tuple(ct))
        for o_ref, i, a in zip(o_refs[:len(gi)], gi, row_add):
            g = grads[i] if a is None else grads[i] + add_refs.pop(0)[...].astype(F32)
            o_ref[...] = g.astype(o_ref.dtype)
        first = pl.program_id(0) == 0
        for o_ref, g in zip(o_refs[len(gi):len(gi) + nc], grads[nr:]):
            @pl.when(first)
            def _(o_ref=o_ref, g=g):
                o_ref[...] = g

            @pl.when(jnp.logical_not(first))
            def _(o_ref=o_ref, g=g):
                o_ref[...] += g
        for o_ref, i in zip(o_refs[len(gi) + nc:], emit):
            o_ref[...] = res[i].astype(o_ref.dtype)

    out_shape = ([jax.ShapeDtypeStruct(rows[i].shape, dt) for i, dt in zip(gi, row_dtypes, strict=True)]
                 + [jax.ShapeDtypeStruct(x.shape, F32) for x in consts]
                 + [jax.ShapeDtypeStruct((r_dim, outs[i].shape[1]), F32) for i in emit])
    out_specs = ([pl.BlockSpec((tr, rows[i].shape[1]), lambda i_: (i_, 0)) for i in gi]
                 + [pl.BlockSpec(x.shape, lambda i_: (0, 0)) for x in consts]
                 + [pl.BlockSpec((tr, outs[i].shape[1]), lambda i_: (i_, 0)) for i in emit])
    res = pl.pallas_call(
        body, name=name,
        out_shape=out_shape,
        grid=(r_dim // tr,),
        in_specs=[pl.BlockSpec((tr, x.shape[1]), lambda i: (i, 0)) for x in rows]
        + [pl.BlockSpec(x.shape, lambda i: (0, 0)) for x in consts]
        + [pl.BlockSpec((tr, x.shape[1]), lambda i: (i, 0)) for x in list(cts) + adds],
        out_specs=out_specs,
        compiler_params=_cparams("arbitrary"),
    )(*rows, *consts, *cts, *adds)
    return res[:len(gi)], res[len(gi):len(gi) + nc], res[len(gi) + nc:]


def transpose(a, *, name, out_dtype=BF16, tr=512, tc=512):
    r_dim, c_dim = a.shape
    tr, tc = _pick(r_dim, tr), _pick(c_dim, tc)

    def body(a_ref, o_ref):
        o_ref[...] = a_ref[...].astype(F32).T.astype(o_ref.dtype)

    return pl.pallas_call(
        body, name=name, out_shape=jax.ShapeDtypeStruct((c_dim, r_dim), out_dtype),
        grid=(r_dim // tr, c_dim // tc),
        in_specs=[pl.BlockSpec((tr, tc), lambda i, j: (i, j))],
        out_specs=pl.BlockSpec((tc, tr), lambda i, j: (j, i)),
        compiler_params=_cparams("parallel", "parallel"),
    )(a)


CONV_HALO = 32
CONV_ROWS = 256


def conv_fwd(x, w, b, *, name, cb=256, chunk_rows=CONV_ROWS):
    s_dim, c_dim = x.shape
    taps = w.shape[0]
    assert taps - 1 <= CONV_HALO and s_dim % chunk_rows == 0 and c_dim % cb == 0
    n_chunks = s_dim // chunk_rows
    ext = chunk_rows + CONV_HALO

    def body(x_ref, w_ref, b_ref, o_ref, xp_ref):
        xp_ref[pl.ds(0, CONV_HALO), :] = jnp.zeros((CONV_HALO, cb), F32)
        xp_ref[pl.ds(CONV_HALO, s_dim), :] = x_ref[...].astype(F32)
        wv = w_ref[...].astype(F32)
        bv = b_ref[...].astype(F32)

        def chunk(t, carry):
            base = pl.multiple_of(t * chunk_rows, chunk_rows)
            xe = xp_ref[pl.ds(base, ext), :]
            acc = jnp.broadcast_to(bv, (chunk_rows, cb))
            for j in range(taps):
                sh = xe if j == 0 else pltpu.roll(xe, shift=j, axis=0)
                acc = acc + wv[taps - 1 - j:taps - j, :] * sh[CONV_HALO:, :]
            o_ref[pl.ds(base, chunk_rows), :] = acc
            return carry

        lax.fori_loop(0, n_chunks, chunk, 0)

    return pl.pallas_call(
        body, name=name,
        out_shape=jax.ShapeDtypeStruct((s_dim, c_dim), F32),
        grid=(c_dim // cb,),
        in_specs=[pl.BlockSpec((s_dim, cb), lambda i: (0, i)), pl.BlockSpec((taps, cb), lambda i: (0, i)),
                  pl.BlockSpec((1, cb), lambda i: (0, i))],
        out_specs=pl.BlockSpec((s_dim, cb), lambda i: (0, i)),
        scratch_shapes=[pltpu.VMEM((s_dim + CONV_HALO, cb), F32)],
        compiler_params=_cparams("parallel"),
    )(x, w, b)


def conv_bwd(x, w, g, *, name, cb=256, chunk_rows=CONV_ROWS, dx_dtype=F32):
    s_dim, c_dim = x.shape
    taps = w.shape[0]
    n_chunks = s_dim // chunk_rows
    ext = chunk_rows + CONV_HALO

    def rows8(a):
        return jnp.sum(a.reshape(chunk_rows // 8, 8, cb), axis=0)

    def body(x_ref, w_ref, g_ref, dx_ref, dw_ref, db_ref, xp_ref, gp_ref, acc_ref):
        xp_ref[pl.ds(0, CONV_HALO), :] = jnp.zeros((CONV_HALO, cb), F32)
        xp_ref[pl.ds(CONV_HALO, s_dim), :] = x_ref[...].astype(F32)
        gp_ref[pl.ds(0, s_dim), :] = g_ref[...].astype(F32)
        gp_ref[pl.ds(s_dim, CONV_HALO), :] = jnp.zeros((CONV_HALO, cb), F32)
        acc_ref[...] = jnp.zeros_like(acc_ref)
        wv = w_ref[...].astype(F32)

        def chunk(t, carry):
            base = pl.multiple_of(t * chunk_rows, chunk_rows)
            xe = xp_ref[pl.ds(base, ext), :]
            ge = gp_ref[pl.ds(base, ext), :]
            gc = ge[:chunk_rows, :]
            dx = jnp.zeros((chunk_rows, cb), F32)
            for j in range(taps):
                xs = xe if j == 0 else pltpu.roll(xe, shift=j, axis=0)
                gs = ge if j == 0 else pltpu.roll(ge, shift=ext - j, axis=0)
                k = taps - 1 - j
                dx = dx + wv[k:k + 1, :] * gs[:chunk_rows, :]
                acc_ref[8 * k:8 * k + 8, :] += rows8(gc * xs[CONV_HALO:, :])
            acc_ref[8 * taps:8 * taps + 8, :] += rows8(gc)
            dx_ref[pl.ds(base, chunk_rows), :] = dx.astype(dx_ref.dtype)
            return carry

        lax.fori_loop(0, n_chunks, chunk, 0)
        sums = jnp.sum(acc_ref[...].reshape(taps + 1, 8, cb), axis=1)
        dw_ref[...] = sums[0:taps, :]
        db_ref[...] = sums[taps:taps + 1, :]

    return pl.pallas_call(
        body, name=name,
        out_shape=[jax.ShapeDtypeStruct((s_dim, c_dim), dx_dtype), jax.ShapeDtypeStruct((taps, c_dim), F32),
                   jax.ShapeDtypeStruct((1, c_dim), F32)],
        grid=(c_dim // cb,),
        in_specs=[pl.BlockSpec((s_dim, cb), lambda i: (0, i)), pl.BlockSpec((taps, cb), lambda i: (0, i)),
                  pl.BlockSpec((s_dim, cb), lambda i: (0, i))],
        out_specs=[pl.BlockSpec((s_dim, cb), lambda i: (0, i)), pl.BlockSpec((taps, cb), lambda i: (0, i)),
                   pl.BlockSpec((1, cb), lambda i: (0, i))],
        scratch_shapes=[pltpu.VMEM((s_dim + CONV_HALO, cb), F32), pltpu.VMEM((s_dim + CONV_HALO, cb), F32),
                        pltpu.VMEM((8 * (taps + 1), cb), F32)],
        compiler_params=_cparams("parallel"),
    )(x, w, g)


def _softplus(x):
    return jnp.maximum(x, 0.0) + jnp.log(1.0 + jnp.exp(-jnp.abs(x)))


HEADS_PER_GROUP = SSM_HEADS // SSM_GROUPS


def _bdot(a, b, ca, cb):
    return lax.dot_general(a.astype(BF16), b.astype(BF16), (((ca,), (cb,)), ((0,), (0,))), preferred_element_type=F32)


@jax.custom_vjp
def bmm(a, b):
    return _bdot(a, b, 2, 1)


def _bmm_fwd(a, b):
    return _bdot(a, b, 2, 1), (a, b)


def _bmm_bwd(res, g):
    a, b = res
    return _bdot(g, b, 2, 2).astype(a.dtype), _bdot(a, g, 1, 1).astype(b.dtype)


bmm.defvjp(_bmm_fwd, _bmm_bwd)


@jax.custom_vjp
def bmm_nt(a, b):
    return _bdot(a, b, 2, 2)


def _bmm_nt_fwd(a, b):
    return _bdot(a, b, 2, 2), (a, b)


def _bmm_nt_bwd(res, g):
    a, b = res
    return _bdot(g, b, 2, 1).astype(a.dtype), _bdot(g, a, 1, 1).astype(b.dtype)


bmm_nt.defvjp(_bmm_nt_fwd, _bmm_nt_bwd)


@jax.custom_vjp
def bmm_tn(a, b):
    return _bdot(a, b, 1, 1)


def _bmm_tn_fwd(a, b):
    return _bdot(a, b, 1, 1), (a, b)


def _bmm_tn_bwd(res, g):
    a, b = res
    return _bdot(b, g, 2, 2).astype(a.dtype), _bdot(a, g, 2, 1).astype(b.dtype)


bmm_tn.defvjp(_bmm_tn_fwd, _bmm_tn_bwd)


ATT_PAIRS = ATT_HEADS // 2
PAIR_W = 2 * ATT_HDIM


def att_pairs(q, kp, kc, vp, vc, bias, has_prev):
    t, b, w = q.shape
    i = lax.broadcasted_iota(jnp.int32, (1, b, b), 1)
    j = lax.broadcasted_iota(jnp.int32, (1, b, b), 2)
    first = lax.broadcasted_iota(jnp.int32, (1, 1, w), 2) < ATT_HDIM
    scale = ATT_HDIM ** -0.5
    outs, lses = [], []
    for ab in range(2):
        qh = jnp.where(first if ab == 0 else jnp.logical_not(first), q, 0.0)
        sp = jnp.where(jnp.logical_and(j >= i, has_prev), bmm_nt(qh, kp) * scale + bias[:, ab, 0], -1e30)
        sc = jnp.where(j <= i, bmm_nt(qh, kc) * scale + bias[:, ab, 1], -1e30)
        m = lax.stop_gradient(jnp.maximum(jnp.max(sp, axis=2, keepdims=True), jnp.max(sc, axis=2, keepdims=True)))
        pp, pc = jnp.exp(sp - m), jnp.exp(sc - m)
        l = jnp.sum(pp, axis=2, keepdims=True) + jnp.sum(pc, axis=2, keepdims=True)
        outs.append(bmm(pp / l, vp) + bmm(pc / l, vc))
        lses.append(jnp.broadcast_to(m + jnp.log(l), (t, b, w)))
    return jnp.where(first, outs[0], outs[1]), jnp.where(first, lses[0], lses[1])


def _pair_tiles(ref):
    return jnp.stack([ref[:, PAIR_W * t:PAIR_W * (t + 1)] for t in range(ATT_PAIRS)])


def _store_pair_tiles(ref, val):
    for t in range(ATT_PAIRS):
        ref[:, PAIR_W * t:PAIR_W * (t + 1)] = val[t].astype(ref.dtype)


def pair_bias(bias):
    return bias.reshape(ATT_PAIRS, 2, 2, ATT_BLK, ATT_BLK)


def att2_fwd(q, k, v, bias, nb, cols, *, name, side=None):
    n_blocks = S // ATT_BLK
    qc, kc, vc = cols

    def body(q_ref, k_ref, v_ref, b_ref, o_ref, l_ref, kprev, vprev):
        blk = pl.program_id(0)

        @pl.when(blk == 0)
        def _():
            kprev[...] = jnp.zeros_like(kprev)
            vprev[...] = jnp.zeros_like(vprev)

        k3, v3 = _pair_tiles(k_ref), _pair_tiles(v_ref)
        o, lse = att_pairs(_pair_tiles(q_ref), kprev[...], k3, vprev[...], v3, b_ref[...], (blk % nb) != 0)
        _store_pair_tiles(o_ref, o)
        _store_pair_tiles(l_ref, lse)
        kprev[...] = k3
        vprev[...] = v3

    def spec(c):
        return pl.BlockSpec((ATT_BLK, D), lambda b: (b, c))

    return grid_call(
        body, (q, k, v, bias), name=name,
        out_shape=[jax.ShapeDtypeStruct((S, D), BF16), jax.ShapeDtypeStruct((S, D), F32)], grid=(n_blocks,),
        in_specs=[spec(qc), spec(kc), spec(vc), pl.BlockSpec(bias.shape, lambda b: (0, 0, 0, 0, 0))],
        out_specs=[spec(0), spec(0)],
        scratch_shapes=[pltpu.VMEM((ATT_PAIRS, ATT_BLK, PAIR_W), BF16), pltpu.VMEM((ATT_PAIRS, ATT_BLK, PAIR_W), BF16)],
        semantics=("arbitrary",), side=side)


def att2_bwd(q, k, v, bias, do, dlse, nb, cols, *, name, side=None):
    n_blocks = S // ATT_BLK
    qc, kc, vc = cols

    def body(q_ref, k_ref, v_ref, b_ref, do_ref, dl_ref, dq_ref, dk_ref, dv_ref, db_ref, kprev, vprev, dk_own, dv_own):
        blk = pl.program_id(0)

        @pl.when(blk == 0)
        def _():
            for r in (kprev, vprev, dk_own, dv_own, db_ref):
                r[...] = jnp.zeros_like(r)

        @pl.when(blk < n_blocks)
        def _():
            k3, v3 = _pair_tiles(k_ref), _pair_tiles(v_ref)
            ins = _f32([_pair_tiles(q_ref), kprev[...], k3, vprev[...], v3]) + [b_ref[...]]
            _, vjp = jax.vjp(functools.partial(att_pairs, has_prev=(blk % nb) != 0), *ins)
            dq, dkp, dkc, dvp, dvc, db = vjp(tuple(_f32([_pair_tiles(do_ref), _pair_tiles(dl_ref)])))
            _store_pair_tiles(dq_ref, dq)
            _store_pair_tiles(dk_ref, dk_own[...] + dkp)
            _store_pair_tiles(dv_ref, dv_own[...] + dvp)
            dk_own[...] = dkc
            dv_own[...] = dvc
            db_ref[...] += db
            kprev[...] = k3
            vprev[...] = v3

        @pl.when(blk == n_blocks)
        def _():
            _store_pair_tiles(dk_ref, dk_own[...])
            _store_pair_tiles(dv_ref, dv_own[...])

    def spec(c):
        return pl.BlockSpec((ATT_BLK, D), lambda b: (jnp.minimum(b, n_blocks - 1), c))

    late = pl.BlockSpec((ATT_BLK, D), lambda b: (jnp.maximum(b - 1, 0), 0))
    bspec = pl.BlockSpec(bias.shape, lambda b: (0, 0, 0, 0, 0))
    tile_f32 = pltpu.VMEM((ATT_PAIRS, ATT_BLK, PAIR_W), F32)
    tile_bf16 = pltpu.VMEM((ATT_PAIRS, ATT_BLK, PAIR_W), BF16)
    return grid_call(
        body, (q, k, v, bias, do, dlse), name=name,
        out_shape=[jax.ShapeDtypeStruct((S, D), BF16), jax.ShapeDtypeStruct((S, D), BF16),
                   jax.ShapeDtypeStruct((S, D), BF16), jax.ShapeDtypeStruct(bias.shape, F32)],
        grid=(n_blocks + 1,),
        in_specs=[spec(qc), spec(kc), spec(vc), bspec, spec(0), spec(0)],
        out_specs=[spec(0), late, late, bspec],
        scratch_shapes=[tile_bf16, tile_bf16, tile_f32, tile_f32],
        semantics=("arbitrary",), side=side)


def regroup(a, dil, inverse=False):
    if dil == 1:
        return a
    c_dim = a.shape[1]
    shape = (dil, S // dil, c_dim) if inverse else (S // dil, dil, c_dim)
    return jnp.transpose(a.reshape(shape), (1, 0, 2)).reshape(S, c_dim)


SSD_PAIRS = SSM_HEADS // 2
PAIRS_PER_GROUP = SSD_PAIRS // SSM_GROUPS
GROUP_W = HEADS_PER_GROUP * SSM_HDIM


SSD_GROUPS_PER_STEP = 2
SSD_STEP_PAIRS = PAIRS_PER_GROUP * SSD_GROUPS_PER_STEP
SSD_STEP_W = GROUP_W * SSD_GROUPS_PER_STEP


def ssd_pairs(x, dtraw, dt_bias, a_log, dskip, bms, cms, prev):
    t, q, w = x.shape
    n = bms[0].shape[1]
    per = t // len(bms)

    def by_pair(mats):
        return jnp.concatenate([jnp.broadcast_to(m[None], (per,) + m.shape) for m in mats], axis=0)
    li = lax.broadcasted_iota(jnp.int32, (1, q, q), 1)
    si = lax.broadcasted_iota(jnp.int32, (1, q, q), 2)
    first_lane = lax.broadcasted_iota(jnp.int32, (1, 1, w), 2) < SSM_HDIM
    first_row = lax.broadcasted_iota(jnp.int32, (1, w, 1), 1) < SSM_HDIM

    def to_col(row):
        return jnp.sum(jnp.where(li == si, jnp.broadcast_to(row, (t, q, q)), 0.0), axis=2, keepdims=True)

    def lanes(a0, a1):
        return jnp.where(first_lane, a0, a1)

    dt_col, acs_col, total, lmat = [], [], [], []
    for ab in range(2):
        dt_row = _softplus(dtraw[ab] + dt_bias[ab])
        a_row = dt_row * (-jnp.exp(a_log[ab]))
        a_col = to_col(a_row)
        acs_c = jnp.sum(jnp.where(si <= li, jnp.broadcast_to(a_row, (t, q, q)), 0.0), axis=2, keepdims=True)
        acs_r = jnp.sum(jnp.where(li <= si, jnp.broadcast_to(a_col, (t, q, q)), 0.0), axis=1, keepdims=True)
        dt_col.append(to_col(dt_row))
        acs_col.append(acs_c)
        total.append(jnp.sum(a_row, axis=2, keepdims=True))
        lmat.append(jnp.exp(jnp.where(li >= si, acs_c - acs_r, -1e30)))
    cb = by_pair([mm_nt(c_, b_) for c_, b_ in zip(cms, bms, strict=True)])
    bmb, cmb = by_pair(bms), by_pair(cms)
    xdt = x * lanes(dt_col[0], dt_col[1])
    y = lanes(bmm(cb * lmat[0], xdt), bmm(cb * lmat[1], xdt))
    y = y + bmm_nt(cmb, prev) * lanes(jnp.exp(acs_col[0]), jnp.exp(acs_col[1]))
    y = y + lanes(dskip[0], dskip[1]) * x
    state = bmm_tn(xdt * lanes(jnp.exp(total[0] - acs_col[0]), jnp.exp(total[1] - acs_col[1])), bmb)
    return y, jnp.where(first_row, jnp.exp(total[0]), jnp.exp(total[1])) * prev + state


def _group_tiles(ref):
    return jnp.stack([ref[:, PAIR_W * t:PAIR_W * (t + 1)] for t in range(SSD_STEP_PAIRS)])


def _store_group_tiles(ref, val):
    for t in range(SSD_STEP_PAIRS):
        ref[:, PAIR_W * t:PAIR_W * (t + 1)] = val[t]


def _bc_groups(ref):
    return tuple(ref[:, SSM_STATE * i:SSM_STATE * (i + 1)] for i in range(SSD_GROUPS_PER_STEP))


def _by_pair(a):
    return jnp.transpose(a.reshape(SSD_PAIRS, 2, 1, -1), (1, 0, 2, 3))


def _by_head(a):
    return jnp.transpose(a, (1, 0, 2, 3)).reshape(SSM_HEADS, -1)


def _ssd2_specs(chunk_of):
    tp = SSD_STEP_PAIRS
    xspec = pl.BlockSpec((CHUNK, SSD_STEP_W), lambda g, c: (chunk_of(c), g))
    tspec = pl.BlockSpec((2, tp, 1, CHUNK), lambda g, c: (0, g, 0, chunk_of(c)))
    hp = pl.BlockSpec((2, tp, 1, 1), lambda g, c: (0, g, 0, 0))
    gspec = pl.BlockSpec((CHUNK, SSD_GROUPS_PER_STEP * SSM_STATE), lambda g, c: (chunk_of(c), g))
    sspec = pl.BlockSpec((1, tp, PAIR_W, SSM_STATE), lambda g, c: (chunk_of(c), g, 0, 0))
    return xspec, tspec, hp, gspec, sspec


def ssd2_fwd(xs, dtraw_t, dt_bias, a_log, dskip, bm, cm, side=None):
    def body(x_ref, dt_ref, dtb_ref, al_ref, dk_ref, bm_ref, cm_ref, y_ref, prev_ref, state_ref):
        @pl.when(pl.program_id(1) == 0)
        def _():
            state_ref[...] = jnp.zeros_like(state_ref)

        prev = state_ref[...]
        prev_ref[0] = prev
        y, nxt = ssd_pairs(_group_tiles(x_ref), dt_ref[...], dtb_ref[...], al_ref[...], dk_ref[...], _bc_groups(bm_ref),
                           _bc_groups(cm_ref), prev)
        _store_group_tiles(y_ref, y)
        state_ref[...] = nxt

    xspec, tspec, hp, gspec, sspec = _ssd2_specs(lambda c: c)
    return grid_call(
        body, (xs, _by_pair(dtraw_t), _by_pair(dt_bias), _by_pair(a_log), _by_pair(dskip), bm, cm), name="ssd_fwd",
        out_shape=[jax.ShapeDtypeStruct((S, SSM_INNER), F32),
                   jax.ShapeDtypeStruct((N_CHUNKS, SSD_PAIRS, PAIR_W, SSM_STATE), F32)],
        grid=(SSM_GROUPS // SSD_GROUPS_PER_STEP, N_CHUNKS), in_specs=[xspec, tspec, hp, hp, hp, gspec, gspec],
        out_specs=[xspec, sspec],
        scratch_shapes=[pltpu.VMEM((SSD_STEP_PAIRS, PAIR_W, SSM_STATE), F32)],
        semantics=("parallel", "arbitrary"), side=side)


def ssd2_bwd(xs, dtraw_t, dt_bias, a_log, dskip, bm, cm, prev_all, dy, side=None):
    def body(x_ref, dt_ref, dtb_ref, al_ref, dk_ref, bm_ref, cm_ref, prev_ref, dy_ref,
             dx_ref, ddt_ref, ddtb_ref, dal_ref, ddk_ref, dbm_ref, dcm_ref, dstate_ref):
        @pl.when(pl.program_id(1) == 0)
        def _():
            for r in (dstate_ref, ddtb_ref, dal_ref, ddk_ref):
                r[...] = jnp.zeros_like(r)

        _, vjp = jax.vjp(ssd_pairs, _group_tiles(x_ref), dt_ref[...], dtb_ref[...], al_ref[...], dk_ref[...],
                         _bc_groups(bm_ref), _bc_groups(cm_ref), prev_ref[0])
        dx, ddt, ddtb, dal, ddk, dbms, dcms, dprev = vjp((_group_tiles(dy_ref), dstate_ref[...]))
        _store_group_tiles(dx_ref, dx)
        ddt_ref[...] = ddt
        ddtb_ref[...] += ddtb
        dal_ref[...] += dal
        ddk_ref[...] += ddk
        for i in range(SSD_GROUPS_PER_STEP):
            dbm_ref[:, SSM_STATE * i:SSM_STATE * (i + 1)] = dbms[i]
            dcm_ref[:, SSM_STATE * i:SSM_STATE * (i + 1)] = dcms[i]
        dstate_ref[...] = dprev

    xspec, tspec, hp, gspec, sspec = _ssd2_specs(lambda c: N_CHUNKS - 1 - c)
    par = jax.ShapeDtypeStruct((2, SSD_PAIRS, 1, 1), F32)
    res, side_dst = grid_call(
        body, (xs, _by_pair(dtraw_t), _by_pair(dt_bias), _by_pair(a_log), _by_pair(dskip), bm, cm, prev_all, dy),
        name="ssd_bwd",
        out_shape=[jax.ShapeDtypeStruct((S, SSM_INNER), F32), jax.ShapeDtypeStruct((2, SSD_PAIRS, 1, S), F32), par, par, par,
                   jax.ShapeDtypeStruct((S, SSM_GROUPS * SSM_STATE), F32),
                   jax.ShapeDtypeStruct((S, SSM_GROUPS * SSM_STATE), F32)],
        grid=(SSM_GROUPS // SSD_GROUPS_PER_STEP, N_CHUNKS), in_specs=[xspec, tspec, hp, hp, hp, gspec, gspec, sspec, xspec],
        out_specs=[xspec, tspec, hp, hp, hp, gspec, gspec],
        scratch_shapes=[pltpu.VMEM((SSD_STEP_PAIRS, PAIR_W, SSM_STATE), F32)],
        semantics=("parallel", "arbitrary"), side=side)
    return [res[0]] + [_by_head(r) for r in res[1:5]] + list(res[5:]), side_dst


def _silu(x):
    return x * jax.nn.sigmoid(x)


def _rms(x):
    return x * lax.rsqrt(jnp.mean(x * x, -1, keepdims=True) + EPS)


def f_normmod(x, g, sc, sh):
    return (_rms(x) * g * (1.0 + sc) + sh,)


def f_resid(x, mix, gate):
    return (x + gate * mix,)


def f_resid_bias(x, mix, gate, b):
    return (x + gate * (mix + b),)


def f_swiglu(hgu):
    return (_silu(hgu[:, :FFN_HIDDEN]) * hgu[:, FFN_HIDDEN:],)


def f_silu(x):
    return (_silu(x),)


def f_silu_xbc(x):
    y = _silu(x)
    n_b = SSM_GROUPS * SSM_STATE
    return y[:, :SSM_INNER], y[:, SSM_INNER:SSM_INNER + n_b], y[:, SSM_INNER + n_b:]


def f_gated_norm(y, z, g):
    return (_rms(y * _silu(z)) * g,)


def f_glu(y, b):
    y = y + b
    return (y[:, :D] * jax.nn.sigmoid(y[:, D:]),)


def f_ln_silu(u, g, b):
    mu = jnp.mean(u, -1, keepdims=True)
    var = jnp.mean(jnp.square(u - mu), -1, keepdims=True)
    return (_silu((u - mu) * lax.rsqrt(var + EPS) * g + b),)


def f_combine(o1, o2, o3, l1, l2, l3):
    m = lax.stop_gradient(jnp.maximum(jnp.maximum(l1, l2), l3))
    e1, e2, e3 = jnp.exp(l1 - m), jnp.exp(l2 - m), jnp.exp(l3 - m)
    return ((e1 * o1 + e2 * o2 + e3 * o3) / (e1 + e2 + e3),)


def f_head(x, tgt, g):
    return (0.5 * jnp.mean(jnp.square(_rms(x) * g - tgt), -1, keepdims=True),)


def f_sum3(a, b, c):
    return (a + b + c,)


def f_add(a, b):
    return (a + b,)


def f_adamw(w, g, m, v):
    m = ADAM_B1 * m + (1.0 - ADAM_B1) * g
    v = ADAM_B2 * v + (1.0 - ADAM_B2) * jnp.square(g)
    m_hat = m / (1.0 - ADAM_B1 ** ADAM_STEP)
    v_hat = v / (1.0 - ADAM_B2 ** ADAM_STEP)
    return -ADAM_LR * (m_hat / (jnp.sqrt(v_hat) + ADAM_EPS) + ADAM_WD * w), m, v


def _rows_tile(r, cap=256):
    return _pick(r, cap, mult=8)


def adamw(w, g, m, v, *, name):
    l_dim, r_dim, c_dim = w.shape
    tr = _rows_tile(r_dim, cap=128)

    def body(w_ref, g_ref, m_ref, v_ref, d_ref, mo_ref, vo_ref):
        d_ref[...], mo_ref[...], vo_ref[...] = f_adamw(w_ref[...], g_ref[...], m_ref[...], v_ref[...])

    spec = pl.BlockSpec((1, tr, c_dim), lambda l, i: (l, i, 0))
    return pl.pallas_call(
        body, name=name, out_shape=[jax.ShapeDtypeStruct(w.shape, F32)] * 3, grid=(l_dim, r_dim // tr),
        in_specs=[spec] * 4, out_specs=[spec] * 3, compiler_params=_cparams("parallel", "parallel"),
    )(w, g, m, v)


def _t5_bucket(dist):
    max_exact = REL_BUCKETS // 2
    n = jnp.maximum(dist, 1).astype(F32)
    large = max_exact + jnp.log(n / max_exact) / math.log(REL_MAX_DIST / max_exact) * (REL_BUCKETS - max_exact)
    large = jnp.minimum(large.astype(jnp.int32), REL_BUCKETS - 1)
    return jnp.where(dist < max_exact, dist, large)


def _att_buckets(dil):
    i = jnp.arange(ATT_BLK)[:, None]
    j = jnp.arange(2 * ATT_BLK)[None, :]
    bkt = _t5_bucket(jnp.maximum(ATT_BLK + i - j, 0) * dil)
    return jnp.transpose(bkt.reshape(ATT_BLK, 2, ATT_BLK), (1, 0, 2))


def att_bias(rel_table, p, dil):
    tab = rel_table[:, p * ATT_HEADS:(p + 1) * ATT_HEADS]
    onehot = (jnp.arange(REL_BUCKETS)[:, None] == _att_buckets(dil).reshape(1, -1)).astype(F32)
    bias = lax.dot_general(tab, onehot, (((0,), (0,)), ((), ())), precision=lax.Precision.HIGHEST)
    return bias.reshape(ATT_HEADS, 2, ATT_BLK, ATT_BLK)


def att_bias_grad(dbias, dil, *, name):
    onehot = (_att_buckets(dil).reshape(-1, 1) == jnp.arange(LANES)[None, :]).astype(BF16)
    dtab = matmul(dbias.reshape(ATT_HEADS, -1), onehot, mode="nn", out_dtype=F32, name=name, tk_cap=2048)
    return dtab[:, :REL_BUCKETS].T


HY_Z, HY_XBC, HY_DT, HY_Q, HY_K, HY_V = 2048, 3072, 32, 3072, 1024, 1024
HY_IN = HY_Z + HY_XBC + HY_DT + HY_Q + HY_K + HY_V
OFF_Z, OFF_XBC, OFF_Q, OFF_KV, OFF_DT = 0, 2048, 5120, 8192, 10240
HY_CAT = OFF_DT + LANES
DT_PAD = LANES


def hy_to_cat(w):
    z, xbc, dt, qkv = w[:2048], w[2048:5120], w[5120:5152], w[5152:]
    return jnp.concatenate([z, xbc, qkv, dt, jnp.zeros((DT_PAD - HY_DT,) + w.shape[1:], w.dtype)], axis=0)


def hy_from_cat(w, axis=0):
    part = lambda a, b: lax.slice_in_dim(w, a, b, axis=axis)
    return jnp.concatenate([part(0, 5120), part(OFF_DT, OFF_DT + HY_DT), part(5120, OFF_DT)], axis=axis)


def device_step(x, tgt, mods, wts, sp, comm=None):
    g = {}
    dmods = [[None] * 6 for _ in range(2)]
    wts = dict(wts)

    def wgrad(tokens_d, tokens_n, nm):
        return matmul(transpose(tokens_d, name=nm + "_t"), tokens_n, mode="nn", out_dtype=BF16, name=nm, out_t=True,
                      tk_cap=2048)

    def w_side(i):
        return None if comm is None else GatherRows(comm["pack"], comm["full"], *W_BATCHES[i])

    def g_side(i):
        return None if comm is None else ScatterRows(comm["ga"], comm["recv"], *G_BATCHES[i])

    def normmod(xi, gain, sc, sh, nm):
        return rowmap(f_normmod, [xi], [gain, sc, sh], [BF16], name=nm)[0]

    def ffn_fwd(xi, i, gate, nm):
        h = normmod(xi, sp["norm_ffn_g"][i], mods[i][4], mods[i][3], nm + "_norm")
        hgu = matmul(h, wts["gu_t"][i], mode="nt", out_dtype=BF16, name=nm + "_gu")
        act = rowmap(f_swiglu, [hgu], [], [BF16], name=nm + "_act", tr=128)[0]
        out = matmul(act, wts["down"][i], mode="nn", out_dtype=F32, name=nm + "_down")
        xo = rowmap(f_resid, [xi, out], [gate], [F32], name=nm + "_res")[0]
        return xo, (h, hgu, act, out)

    def ffn_bwd(dres, xi, i, saved, nm):
        h, hgu, act, out = saved
        (dout,), (dgate,), _ = rowmap_bwd(f_resid, [xi, out], [mods[i][5]], [dres], name=nm + "_res_b",
                                          row_grad=[False, True], row_dtypes=[BF16])
        dmods[i][5] = dgate
        dact = matmul(dout, wts["down"][i], mode="nt", out_dtype=BF16, name=nm + "_down_dx")
        g[f"down{i}"] = wgrad(dout, act, nm + "_down_dw")
        (dhgu,), _, _ = rowmap_bwd(f_swiglu, [hgu], [], [dact], name=nm + "_act_b", row_grad=[True],
                                   row_dtypes=[BF16], tr=128)
        g[f"gu_t{i}"] = wgrad(h, dhgu, nm + "_gu_dw")
        dh = matmul(dhgu, wts["gu_t"][i], mode="nn", out_dtype=F32, name=nm + "_gu_dx")
        (dres,), (dg_, dsc, dsh), _ = rowmap_bwd(f_normmod, [xi], [sp["norm_ffn_g"][i], mods[i][4], mods[i][3]], [dh],
                                                 name=nm + "_norm_b", row_grad=[True], row_add=[dres])
        g[f"norm_ffn_g{i}"] = dg_
        dmods[i][4], dmods[i][3] = dsc, dsh
        return dres

    h0 = normmod(x, sp["norm_mix_g"][0], mods[0][1], mods[0][0], "l0_norm")
    w_in = wts["hy_in_t"]
    z = matmul(h0, w_in, mode="nt", out_dtype=F32, name="hy_z", n=HY_Z, b_off=OFF_Z)
    xbc_raw = matmul(h0, w_in, mode="nt", out_dtype=F32, name="hy_xbc", n=HY_XBC, b_off=OFF_XBC)
    q = matmul(h0, w_in, mode="nt", out_dtype=BF16, name="hy_q", n=HY_Q, b_off=OFF_Q)
    kv = matmul(h0, w_in, mode="nt", out_dtype=BF16, name="hy_kv", n=HY_K + HY_V, b_off=OFF_KV)
    dtr = matmul(h0, w_in, mode="nt", out_dtype=F32, name="hy_dt", n=DT_PAD, b_off=OFF_DT)
    xbc_pre = conv_fwd(xbc_raw, sp["hy_conv_w"], sp["hy_conv_b"], name="hy_conv")
    xs, bm, cm = rowmap(f_silu_xbc, [xbc_pre], [], [F32] * 3, name="hy_conv_act", tr=256)
    dtraw_t = dtr[:, :HY_DT].T
    (y, prev_all), full = ssd2_fwd(xs, dtraw_t, sp["hy_dt_bias"], sp["hy_a_log"], sp["hy_d_skip"], bm, cm, side=w_side(1))
    if comm is not None:
        comm["full"] = full
    ysn = rowmap(f_gated_norm, [y, z], [sp["hy_ssm_norm_g"]], [BF16], name="hy_gnorm", tr=128)[0]
    att_in, att_o, att_l = [], [], []
    for p, (win, dil) in enumerate(ATT_PATTERNS):
        if dil == 1:
            qa, ka, va, cols = q, kv, kv, (p, 0, 1)
        else:
            qa, ka, cols = regroup(q[:, p * D:(p + 1) * D], dil), regroup(kv, dil), (0, 0, 1)
            va = ka
        bias = pair_bias(att_bias(sp["rel_table"], p, dil))
        nb = S // dil // ATT_BLK
        (o, lse), full = att2_fwd(qa, ka, va, bias, nb, cols, name=f"att_fwd{p}", side=w_side(2 + p))
        if comm is not None:
            comm["full"] = full
        att_in.append((qa, ka, va, bias, nb, cols))
        att_o.append(regroup(o, dil, inverse=True))
        att_l.append(regroup(lse, dil, inverse=True))
    if comm is not None:
        wts.update(unpack_weights(comm["full"], skip=("hy_in_t",)))
    att = rowmap(f_combine, att_o + att_l, [], [BF16], name="att_combine", tr=256)[0]
    cat = jnp.concatenate([ysn, att], axis=-1)
    mix0 = matmul(cat, wts["hy_out"], mode="nn", out_dtype=F32, name="hy_out")
    x1 = rowmap(f_resid, [x, mix0], [mods[0][2]], [F32], name="l0_res")[0]
    x2, ffn0 = ffn_fwd(x1, 0, mods[0][5], "ffn0")

    h1 = normmod(x2, sp["norm_mix_g"][1], mods[1][1], mods[1][0], "l1_norm")
    p1 = matmul(h1, wts["pw1_t"], mode="nt", out_dtype=F32, name="cv_pw1")
    u = rowmap(f_glu, [p1], [sp["cv_b_pw1"]], [F32], name="cv_glu")[0]
    uc = conv_fwd(u, sp["cv_w_dw"], sp["cv_b_dw"], name="cv_conv")
    ul = rowmap(f_ln_silu, [uc], [sp["cv_ln_g"], sp["cv_ln_b"]], [BF16], name="cv_ln")[0]
    mix1 = matmul(ul, wts["pw2"], mode="nn", out_dtype=F32, name="cv_pw2")
    x3 = rowmap(f_resid_bias, [x2, mix1], [mods[1][2], sp["cv_b_pw2"]], [F32], name="l1_res")[0]
    x4, ffn1 = ffn_fwd(x3, 1, mods[1][5], "ffn1")

    ones = jnp.ones((S, 1), F32)
    (dres,), (dfinal,), (loss_rows,) = rowmap_bwd(f_head, [x4, tgt], [sp["final_norm_g"]], [ones], name="head",
                                                  row_grad=[True, False], emit=(0,))
    g["final_norm_g"] = dfinal

    dres = ffn_bwd(dres, x3, 1, ffn1, "ffn1")
    (dmix1,), (dg1, db2), _ = rowmap_bwd(f_resid_bias, [x2, mix1], [mods[1][2], sp["cv_b_pw2"]], [dres], name="l1_res_b",
                                         row_grad=[False, True], row_dtypes=[BF16])
    dmods[1][2] = dg1
    g["cv_b_pw2"] = db2
    dul = matmul(dmix1, wts["pw2"], mode="nt", out_dtype=F32, name="cv_pw2_dx")
    g["pw2"] = wgrad(dmix1, ul, "cv_pw2_dw")
    (duc,), (g["cv_ln_g"], g["cv_ln_b"]), _ = rowmap_bwd(f_ln_silu, [uc], [sp["cv_ln_g"], sp["cv_ln_b"]], [dul],
                                                         name="cv_ln_b", row_grad=[True])
    du, g["cv_w_dw"], g["cv_b_dw"] = conv_bwd(u, sp["cv_w_dw"], duc, name="cv_conv_b", cb=128, chunk_rows=128)
    (dp1,), (g["cv_b_pw1"],), _ = rowmap_bwd(f_glu, [p1], [sp["cv_b_pw1"]], [du], name="cv_glu_b", row_grad=[True],
                                             row_dtypes=[BF16])
    g["pw1_t"] = wgrad(h1, dp1, "cv_pw1_dw")
    dh1 = matmul(dp1, wts["pw1_t"], mode="nn", out_dtype=F32, name="cv_pw1_dx")
    (dres,), (dg_, dsc, dsh), _ = rowmap_bwd(f_normmod, [x2], [sp["norm_mix_g"][1], mods[1][1], mods[1][0]], [dh1],
                                             name="l1_norm_b", row_grad=[True], row_add=[dres])
    g["norm_mix_g1"] = dg_
    dmods[1][1], dmods[1][0] = dsc, dsh

    dres = ffn_bwd(dres, x1, 0, ffn0, "ffn0")
    (dmix0,), (dg1,), _ = rowmap_bwd(f_resid, [x, mix0], [mods[0][2]], [dres], name="l0_res_b",
                                     row_grad=[False, True], row_dtypes=[BF16])
    dmods[0][2] = dg1
    dysn = matmul(dmix0, wts["hy_out"], mode="nt", out_dtype=F32, name="hy_out_dy", n=SSM_INNER, b_off=0)
    datt = matmul(dmix0, wts["hy_out"], mode="nt", out_dtype=F32, name="hy_out_da", n=D, b_off=SSM_INNER)
    g["hy_out"] = wgrad(dmix0, cat, "hy_out_dw")
    (dy, dz), (g["hy_ssm_norm_g"],), _ = rowmap_bwd(f_gated_norm, [y, z], [sp["hy_ssm_norm_g"]], [dysn], name="hy_gnorm_b",
                                                    row_grad=[True, True], row_dtypes=[F32, BF16], tr=128)
    if comm is not None:
        comm["ga"] = pack_grads(g, GA_LAYOUT, GA_ROWS)
        comm["recv"] = lax.empty((3, GA_ROWS, D), BF16)
    (dxs, ddtraw_t, g["hy_dt_bias"], g["hy_a_log"], g["hy_d_skip"], dbm, dcm), recv = ssd2_bwd(
        xs, dtraw_t, sp["hy_dt_bias"], sp["hy_a_log"], sp["hy_d_skip"], bm, cm, prev_all, dy, side=g_side(0))
    if comm is not None:
        comm["recv"] = recv
    (dxbc_pre,), _, _ = rowmap_bwd(f_silu_xbc, [xbc_pre], [], [dxs, dbm, dcm], name="hy_conv_act_b", row_grad=[True],
                                   tr=128)
    dxbc_raw, g["hy_conv_w"], g["hy_conv_b"] = conv_bwd(xbc_raw, sp["hy_conv_w"], dxbc_pre, name="hy_conv_b", cb=128, chunk_rows=128, dx_dtype=BF16)
    dol, _, _ = rowmap_bwd(f_combine, att_o + att_l, [], [datt], name="att_combine_b", row_grad=[True] * 6,
                           row_dtypes=[BF16] * 3 + [F32] * 3, tr=128)
    dqs, dks, dvs, dtabs = [], [], [], []
    for p, (win, dil) in enumerate(ATT_PATTERNS):
        qa, ka, va, bias, nb, cols = att_in[p]
        (dq, dkp_, dvp_, dbias), recv = att2_bwd(qa, ka, va, bias, regroup(dol[p], dil), regroup(dol[3 + p], dil), nb,
                                                 cols, name=f"att_bwd{p}", side=g_side(1 + p))
        if comm is not None:
            comm["recv"] = recv
        dqs.append(regroup(dq, dil, inverse=True))
        dks.append(regroup(dkp_, dil, inverse=True))
        dvs.append(regroup(dvp_, dil, inverse=True))
        dtabs.append(att_bias_grad(dbias.reshape(ATT_HEADS, 2, ATT_BLK, ATT_BLK), dil, name=f"att_dtab{p}"))
    g["rel_table"] = jnp.concatenate(dtabs, axis=1)
    dk = rowmap(f_sum3, dks, [], [BF16], name="att_dk_sum")[0]
    dv = rowmap(f_sum3, dvs, [], [BF16], name="att_dv_sum")[0]
    ddt = jnp.pad(ddtraw_t.T, ((0, 0), (0, DT_PAD - HY_DT)))
    dproj = jnp.concatenate([dz, dxbc_raw] + dqs + [dk, dv, ddt.astype(BF16)], axis=-1)
    g["hy_in_t"] = wgrad(h0, dproj, "hy_in_dw")
    if comm is None:
        dh0 = matmul(dproj, w_in, mode="nn", out_dtype=F32, name="hy_in_dx")
    else:
        gb = pack_grads(g, GB_LAYOUT, GB_ROWS)
        half = GB_ROWS // 2
        theirs = swap_halves(gb, name="swap_in_halves")
        ours = lax.dynamic_slice_in_dim(gb, lax.axis_index("c") * half, half, axis=1)
        comm["gb"] = rowmap(f_add, [ours.reshape(N_CHIPS * half, D), theirs.reshape(N_CHIPS * half, D)], [], [BF16],
                            name="sum_in_cores")[0].reshape(N_CHIPS, half, D)
        dh0, comm["recv_b"] = matmul(dproj, w_in, mode="nn", out_dtype=F32, name="hy_in_dx",
                                     side=ScatterRows(comm["gb"], lax.empty((3, half, D), BF16), 0, half))
    (dres,), (dg_, dsc, dsh), _ = rowmap_bwd(f_normmod, [x], [sp["norm_mix_g"][0], mods[0][1], mods[0][0]], [dh0],
                                             name="l0_norm_b", row_grad=[True], row_add=[dres])
    g["norm_mix_g0"] = dg_
    dmods[0][1], dmods[0][0] = dsc, dsh
    return loss_rows, dres, g, dmods


ANY = pl.BlockSpec(memory_space=pl.ANY)
WHOLE_VMEM = pl.BlockSpec(memory_space=pltpu.VMEM)


def _place():
    return lax.axis_index("x"), lax.axis_index("y"), lax.axis_index("c")


def _other_chips(x, y):
    return [(1 - x, y), (x, 1 - y), (1 - x, 1 - y)]


def allgather_small(v, *, name):
    m_per = v.shape[0]

    def body(x_ref, out_ref, send_sems, recv_sems, local_sem):
        x, y, c = _place()
        me, sibling = (x, y, c), (x, y, 1 - c)
        chips = _other_chips(x, y)

        def rows(px, py, pc):
            return out_ref.at[pl.ds((4 * px + 2 * py + pc) * m_per, m_per), :]

        def copy(k, block, to, src=None):
            return pltpu.make_async_remote_copy(
                src_ref=rows(*block) if src is None else src, dst_ref=rows(*block),
                send_sem=send_sems.at[k], recv_sem=recv_sems.at[k], device_id=to, device_id_type=MESH)

        mine = pltpu.make_async_copy(x_ref, rows(*me), local_sem)
        mine.start()
        first = [copy(0, me, sibling, src=x_ref)]
        first += [copy(1 + j, me, (*chip, c), src=x_ref) for j, chip in enumerate(chips)]
        for cp in first:
            cp.start()
        passed = [copy(4 + j, (*chip, c), sibling) for j, chip in enumerate(chips)]
        for j, chip in enumerate(chips):
            copy(1 + j, (*chip, c), me).wait_recv()
            passed[j].start()
        copy(0, sibling, me).wait_recv()
        for j, chip in enumerate(chips):
            copy(4 + j, (*chip, 1 - c), me).wait_recv()
        for cp in first + passed:
            cp.wait_send()
        mine.wait()

    return pl.pallas_call(
        body, name=name,
        out_shape=jax.ShapeDtypeStruct((N_DEV * m_per, LANES), v.dtype),
        in_specs=[WHOLE_VMEM], out_specs=WHOLE_VMEM,
        scratch_shapes=[pltpu.SemaphoreType.DMA((7,)), pltpu.SemaphoreType.DMA((7,)), pltpu.SemaphoreType.DMA],
    )(v)


def swap_halves(gpack, *, name):
    half_rows = gpack.shape[1] // 2

    def body(g_ref, r_ref, send_sems, recv_sems):
        x, y, c = _place()
        its_half = pl.ds((1 - c) * half_rows, half_rows)
        copies = [pltpu.make_async_remote_copy(
            src_ref=g_ref.at[s, its_half], dst_ref=r_ref.at[s], send_sem=send_sems.at[s], recv_sem=recv_sems.at[s],
            device_id=(x, y, 1 - c), device_id_type=MESH) for s in range(N_CHIPS)]
        for cp in copies:
            cp.start()
        for cp in copies:
            cp.wait()

    return pl.pallas_call(
        body, name=name,
        out_shape=jax.ShapeDtypeStruct((N_CHIPS, half_rows) + gpack.shape[2:], gpack.dtype),
        in_specs=[ANY], out_specs=ANY,
        scratch_shapes=[pltpu.SemaphoreType.DMA((N_CHIPS,)), pltpu.SemaphoreType.DMA((N_CHIPS,))],
    )(gpack)


class GatherRows:
    def __init__(self, pack, full, lo, hi):
        assert (hi - lo) % 32 == 0 and lo % 16 == 0
        self.src, self.dst, self.lo, self.hi = pack, full, lo, hi

    def sems(self):
        return [pltpu.SemaphoreType.DMA((6,)), pltpu.SemaphoreType.DMA((6,)), pltpu.SemaphoreType.DMA]

    def _parts(self, pack_ref, full_ref, sems):
        send_sems, recv_sems, local_sem = sems
        x, y, c = _place()
        half = (self.hi - self.lo) // 2
        mine, its = pl.ds(self.lo + c * half, half), pl.ds(self.lo + (1 - c) * half, half)
        rows = pl.ds(self.lo, self.hi - self.lo)
        local = pltpu.make_async_copy(pack_ref.at[rows], full_ref.at[2 * x + y, rows], local_sem)
        chips = _other_chips(x, y)

        def remote(src, dst, k, to):
            return pltpu.make_async_remote_copy(src_ref=src, dst_ref=dst, send_sem=send_sems.at[k],
                                                recv_sem=recv_sems.at[k], device_id=to, device_id_type=MESH)

        sends = [remote(pack_ref.at[mine], full_ref.at[2 * x + y, mine], k, (cx, cy, c)) for k, (cx, cy) in enumerate(chips)]
        landed = [full_ref.at[2 * cx + cy, mine] for cx, cy in chips]
        arrive = [remote(pack_ref.at[mine], landed[k], k, (cx, cy, c)) for k, (cx, cy) in enumerate(chips)]
        passed = [remote(landed[k], landed[k], 3 + k, (x, y, 1 - c)) for k in range(3)]
        from_sibling = [remote(landed[k], full_ref.at[2 * cx + cy, its], 3 + k, (x, y, 1 - c))
                        for k, (cx, cy) in enumerate(chips)]
        return local, sends, arrive, passed, from_sibling

    def start(self, pack_ref, full_ref, sems):
        local, sends, _, _, _ = self._parts(pack_ref, full_ref, sems)
        local.start()
        for cp in sends:
            cp.start()

    def finish(self, pack_ref, full_ref, sems):
        local, sends, arrive, passed, from_sibling = self._parts(pack_ref, full_ref, sems)
        for k in range(3):
            arrive[k].wait_recv()
            passed[k].start()
        for cp in from_sibling:
            cp.wait_recv()
        for cp in sends + passed:
            cp.wait_send()
        local.wait()


class ScatterRows:
    def __init__(self, gpack, recv, lo, hi):
        assert lo % 16 == 0 and hi % 16 == 0
        self.src, self.dst, self.lo, self.hi = gpack, recv, lo, hi

    def sems(self):
        return [pltpu.SemaphoreType.DMA((3,)), pltpu.SemaphoreType.DMA((3,))]

    def _parts(self, g_ref, recv_ref, sems):
        send_sems, recv_sems = sems
        x, y, c = _place()
        rows = pl.ds(self.lo, self.hi - self.lo)
        return [pltpu.make_async_remote_copy(
            src_ref=g_ref.at[2 * cx + cy, rows], dst_ref=recv_ref.at[k, rows], send_sem=send_sems.at[k],
            recv_sem=recv_sems.at[k], device_id=(cx, cy, c), device_id_type=MESH)
            for k, (cx, cy) in enumerate(_other_chips(x, y))]

    def start(self, g_ref, recv_ref, sems):
        for cp in self._parts(g_ref, recv_ref, sems):
            cp.start()

    def finish(self, g_ref, recv_ref, sems):
        sends = self._parts(g_ref, recv_ref, sems)
        for cp in sends:
            cp.wait_recv()
        for cp in sends:
            cp.wait_send()


def side_call(side, *, name):
    def body(src_ref, dst_in_ref, dst_ref, *sems):
        side.start(src_ref, dst_ref, sems)
        side.finish(src_ref, dst_ref, sems)

    return pl.pallas_call(
        body, name=name, out_shape=jax.ShapeDtypeStruct(side.dst.shape, side.dst.dtype),
        in_specs=[ANY, ANY], out_specs=ANY, scratch_shapes=side.sems(), input_output_aliases={1: 0},
    )(side.src, side.dst)


def grid_call(body, args, *, name, out_shape, grid, in_specs, out_specs, scratch_shapes, semantics, side=None):
    if side is None:
        res = pl.pallas_call(body, name=name, out_shape=out_shape, grid=grid, in_specs=in_specs, out_specs=out_specs,
                             scratch_shapes=scratch_shapes, compiler_params=_cparams(*semantics))(*args)
        return res, None
    n_in, n_out, n_scr = len(args), len(out_shape), len(scratch_shapes)

    def wrapped(*refs):
        ins, (src_ref, _) = refs[:n_in], refs[n_in:n_in + 2]
        outs, dst_ref = refs[n_in + 2:n_in + 2 + n_out], refs[n_in + 2 + n_out]
        scr, sems = refs[n_in + 3 + n_out:n_in + 3 + n_out + n_scr], refs[n_in + 3 + n_out + n_scr:]
        first = functools.reduce(jnp.logical_and, [pl.program_id(i) == 0 for i in range(len(grid))])
        last = functools.reduce(jnp.logical_and, [pl.program_id(i) == n - 1 for i, n in enumerate(grid)])

        @pl.when(first)
        def _():
            side.start(src_ref, dst_ref, sems)

        body(*ins, *outs, *scr)

        @pl.when(last)
        def _():
            side.finish(src_ref, dst_ref, sems)

    res = pl.pallas_call(
        wrapped, name=name,
        out_shape=list(out_shape) + [jax.ShapeDtypeStruct(side.dst.shape, side.dst.dtype)],
        grid=grid, in_specs=list(in_specs) + [ANY, ANY], out_specs=list(out_specs) + [ANY],
        scratch_shapes=list(scratch_shapes) + side.sems(), input_output_aliases={n_in + 1: n_out},
        compiler_params=_cparams(*(["arbitrary"] * len(grid))),
    )(*args, side.src, side.dst)
    return res[:-1], res[-1]


def sibling_swap(p, *, name):
    def body(p_ref, r_ref, send_sem, recv_sem):
        x, y, c = _place()
        cp = pltpu.make_async_remote_copy(src_ref=p_ref, dst_ref=r_ref, send_sem=send_sem, recv_sem=recv_sem,
                                          device_id=(x, y, 1 - c), device_id_type=MESH)
        cp.start()
        cp.wait()

    return pl.pallas_call(
        body, name=name, out_shape=jax.ShapeDtypeStruct(p.shape, p.dtype),
        in_specs=[ANY], out_specs=ANY,
        scratch_shapes=[pltpu.SemaphoreType.DMA, pltpu.SemaphoreType.DMA],
    )(p)


def sum_slots(own, recv, *, name):
    r_dim, c_dim = own.shape
    tr = _pick(r_dim, 256, mult=16)

    def body(o_ref, r_ref, out_ref):
        acc = o_ref[...].astype(F32)
        for k in range(3):
            acc = acc + r_ref[k].astype(F32)
        out_ref[...] = acc

    return pl.pallas_call(
        body, name=name, out_shape=jax.ShapeDtypeStruct((r_dim, c_dim), F32), grid=(r_dim // tr,),
        in_specs=[pl.BlockSpec((tr, c_dim), lambda i: (i, 0)), pl.BlockSpec((3, tr, c_dim), lambda i: (0, i, 0))],
        out_specs=pl.BlockSpec((tr, c_dim), lambda i: (i, 0)),
        compiler_params=_cparams("parallel"),
    )(own, recv)


def sum_devices(v_all, *, name):
    m_per = v_all.shape[0] // N_DEV

    def body(v_ref, o_ref):
        acc = v_ref[pl.ds(0, m_per), :]
        for d in range(1, N_DEV):
            acc = acc + v_ref[pl.ds(d * m_per, m_per), :]
        o_ref[...] = acc

    return pl.pallas_call(
        body, name=name, out_shape=jax.ShapeDtypeStruct((m_per, LANES), F32),
        in_specs=[WHOLE_VMEM], out_specs=WHOLE_VMEM,
    )(v_all)


WEIGHTS = ['ada_w', 'ada_b', 'norm_mix_g', 'norm_ffn_g', 'hy_w_in', 'hy_conv_w', 'hy_conv_b', 'hy_dt_bias', 'hy_a_log',
           'hy_d_skip', 'hy_ssm_norm_g', 'hy_w_out', 'rel_table', 'cv_w_pw1', 'cv_b_pw1', 'cv_w_dw', 'cv_b_dw', 'cv_ln_g',
           'cv_ln_b', 'cv_w_pw2', 'cv_b_pw2', 'ffn_w_gate', 'ffn_w_up', 'ffn_w_down', 'final_norm_g']
BIG = ('ada_w', 'hy_w_in', 'hy_w_out', 'cv_w_pw1', 'cv_w_pw2', 'ffn_w_gate', 'ffn_w_up', 'ffn_w_down')
SMALL_SHARDED = {'hy_conv_w': (1, 4, 3072), 'cv_b_pw1': (1, 2048), 'cv_w_dw': (1, 31, 1024), 'cv_b_dw': (1, 1024),
                 'cv_ln_g': (1, 1024), 'cv_ln_b': (1, 1024), 'cv_b_pw2': (1, 1024)}
SMALL_GRADS = {'ada_b': (2, 6144), 'norm_mix_g': (2, 1024), 'norm_ffn_g': (2, 1024), 'hy_conv_w': (1, 4, 3072),
               'hy_conv_b': (1, 3072), 'hy_dt_bias': (1, 32), 'hy_a_log': (1, 32), 'hy_d_skip': (1, 32),
               'hy_ssm_norm_g': (1, 2048), 'rel_table': (32, 48), 'cv_b_pw1': (1, 2048), 'cv_w_dw': (1, 31, 1024),
               'cv_b_dw': (1, 1024), 'cv_ln_g': (1, 1024), 'cv_ln_b': (1, 1024), 'cv_b_pw2': (1, 1024),
               'final_norm_g': (1024,), 'loss': (1,)}

PACK_LAYOUT = (('hy_in_t', 2568), ('hy_out', 768), ('pw1_t', 512), ('pw2', 256),
               ('gate_t0', 704), ('up_t0', 704), ('down0', 704), ('gate_t1', 704), ('up_t1', 704), ('down1', 704))
PACK_ROWS = 8448


def _pack_offsets(layout):
    off, out = 0, {}
    for nm, r in layout:
        out[nm] = (off, r)
        off += r
    return out


PACK_OFF = _pack_offsets(PACK_LAYOUT)
W_BATCHES = ((0, 2624), (2624, 5248), (5248, 6336), (6336, 7424), (7424, 8448))
GA_LAYOUT = PACK_LAYOUT[1:]
GA_ROWS = 5888
GA_OFF = _pack_offsets(GA_LAYOUT)
G_BATCHES = ((0, 2560), (2560, 3712), (3712, 4864), (4864, 5888))
GB_LAYOUT = PACK_LAYOUT[:1]
GB_ROWS = 2816


def pack_grads(g, layout, n_rows):
    def rows_bf16(nm):
        return g[nm]

    parts = []
    for key, r in layout:
        if key == 'hy_in_t':
            a = hy_from_cat(rows_bf16('hy_in_t'))
        elif key.startswith('gate_t'):
            a = rows_bf16('gu_t' + key[-1])[:FFN_HIDDEN]
        elif key.startswith('up_t'):
            a = rows_bf16('gu_t' + key[-1])[FFN_HIDDEN:]
        else:
            a = rows_bf16(key)
        parts.append(a.reshape(N_CHIPS, r, D))
    used = sum(r for _, r in layout)
    return jnp.concatenate(parts + [jnp.zeros((N_CHIPS, n_rows - used, D), BF16)], axis=1)


def unpack_weights(full, skip=()):
    def whole(nm):
        o, r = PACK_OFF[nm]
        return full[:, o:o + r].reshape(N_CHIPS * r, D)

    out = {"hy_out": whole('hy_out'), "pw1_t": whole('pw1_t'), "pw2": whole('pw2'),
           "gu_t": [jnp.concatenate([whole(f'gate_t{i}'), whole(f'up_t{i}')], axis=0) for i in range(2)],
           "down": [whole(f'down{i}') for i in range(2)]}
    if "hy_in_t" not in skip:
        out["hy_in_t"] = hy_to_cat(whole('hy_in_t'))
    return out


def _to_lanes(flat):
    n = flat.shape[0]
    m = -(-n // (8 * LANES)) * 8
    return jnp.pad(flat, (0, m * LANES - n)).reshape(m, LANES)


def _split(flat, shapes):
    out, off = {}, 0
    for nm, shp in shapes.items():
        n = int(np.prod(shp))
        out[nm] = flat[off:off + n].reshape(shp)
        off += n
    return out


def kernel(x, c, ada_w, ada_b, norm_mix_g, norm_ffn_g, hy_w_in, hy_conv_w, hy_conv_b, hy_dt_bias, hy_a_log, hy_d_skip, hy_ssm_norm_g, hy_w_out, rel_table, cv_w_pw1, cv_b_pw1, cv_w_dw, cv_b_dw, cv_ln_g, cv_ln_b, cv_w_pw2, cv_b_pw2, ffn_w_gate, ffn_w_up, ffn_w_down, final_norm_g, loss_target, m_ada_w, m_ada_b, m_norm_mix_g, m_norm_ffn_g, m_hy_w_in, m_hy_conv_w, m_hy_conv_b, m_hy_dt_bias, m_hy_a_log, m_hy_d_skip, m_hy_ssm_norm_g, m_hy_w_out, m_rel_table, m_cv_w_pw1, m_cv_b_pw1, m_cv_w_dw, m_cv_b_dw, m_cv_ln_g, m_cv_ln_b, m_cv_w_pw2, m_cv_b_pw2, m_ffn_w_gate, m_ffn_w_up, m_ffn_w_down, m_final_norm_g, v_ada_w, v_ada_b, v_norm_mix_g, v_norm_ffn_g, v_hy_w_in, v_hy_conv_w, v_hy_conv_b, v_hy_dt_bias, v_hy_a_log, v_hy_d_skip, v_hy_ssm_norm_g, v_hy_w_out, v_rel_table, v_cv_w_pw1, v_cv_b_pw1, v_cv_w_dw, v_cv_b_dw, v_cv_ln_g, v_cv_ln_b, v_cv_w_pw2, v_cv_b_pw2, v_ffn_w_gate, v_ffn_w_up, v_ffn_w_down, v_final_norm_g):
    args = (x, c, ada_w, ada_b, norm_mix_g, norm_ffn_g, hy_w_in, hy_conv_w, hy_conv_b, hy_dt_bias, hy_a_log, hy_d_skip, hy_ssm_norm_g, hy_w_out, rel_table, cv_w_pw1, cv_b_pw1, cv_w_dw, cv_b_dw, cv_ln_g, cv_ln_b, cv_w_pw2, cv_b_pw2, ffn_w_gate, ffn_w_up, ffn_w_down, final_norm_g, loss_target, m_ada_w, m_ada_b, m_norm_mix_g, m_norm_ffn_g, m_hy_w_in, m_hy_conv_w, m_hy_conv_b, m_hy_dt_bias, m_hy_a_log, m_hy_d_skip, m_hy_ssm_norm_g, m_hy_w_out, m_rel_table, m_cv_w_pw1, m_cv_b_pw1, m_cv_w_dw, m_cv_b_dw, m_cv_ln_g, m_cv_ln_b, m_cv_w_pw2, m_cv_b_pw2, m_ffn_w_gate, m_ffn_w_up, m_ffn_w_down, m_final_norm_g, v_ada_w, v_ada_b, v_norm_mix_g, v_norm_ffn_g, v_hy_w_in, v_hy_conv_w, v_hy_conv_b, v_hy_dt_bias, v_hy_a_log, v_hy_d_skip, v_hy_ssm_norm_g, v_hy_w_out, v_rel_table, v_cv_w_pw1, v_cv_b_pw1, v_cv_w_dw, v_cv_b_dw, v_cv_ln_g, v_cv_ln_b, v_cv_w_pw2, v_cv_b_pw2, v_ffn_w_gate, v_ffn_w_up, v_ffn_w_down, v_final_norm_g)
    x_in, c_in = args[0], args[1]
    w = dict(zip(WEIGHTS, args[2:27], strict=True))
    tgt = args[27]
    m_in = dict(zip(WEIGHTS, args[28:53], strict=True))
    v_in = dict(zip(WEIGHTS, args[53:78], strict=True))
    xi, yi, ci = _place()
    chip = 2 * xi + yi
    dev = 2 * chip + ci

    cs = rowmap(f_silu, [c_in.reshape(8, LANES)], [], [F32], name="cond_silu", tr=8)[0]
    cs_all = allgather_small(cs, name="gather_cond").reshape(N_DEV, D)
    cs16 = jnp.pad(cs_all, ((0, 8), (0, 0)))
    modpart = jnp.stack([matmul(cs16, w['ada_w'][i], mode="nn", out_dtype=F32, name=f"ada_fwd{i}")[:N_DEV]
                         for i in range(2)], axis=1)
    shard_names = list(SMALL_SHARDED)
    payload = jnp.concatenate([modpart.reshape(-1)] + [w[nm].reshape(-1) for nm in shard_names])
    got = allgather_small(_to_lanes(payload), name="gather_mod").reshape(N_DEV, -1)[0::2]
    modparts = got[:, :modpart.size].reshape(N_CHIPS, N_DEV, 2, 1536)
    mine = lax.dynamic_index_in_dim(modparts, dev, axis=1, keepdims=False)
    mod = jnp.transpose(mine, (1, 0, 2)).reshape(2, 6 * D) + w['ada_b']
    mods = [[mod[i, j * D:(j + 1) * D].reshape(1, D) for j in range(6)] for i in range(2)]
    sp = {}
    off = modpart.size
    for nm in shard_names:
        shp = w[nm].shape
        n = int(np.prod(shp))
        parts = got[:, off:off + n].reshape((N_CHIPS,) + shp)
        sp[nm] = jnp.concatenate([parts[s] for s in range(N_CHIPS)], axis=-1)
        off += n

    def rows_of(nm, i=None):
        a = w[nm][0 if i is None else i]
        return (a.T if nm in ('hy_w_in', 'cv_w_pw1', 'ffn_w_gate', 'ffn_w_up') else a).astype(BF16)

    pieces = [rows_of('hy_w_in'), rows_of('hy_w_out'), rows_of('cv_w_pw1'), rows_of('cv_w_pw2')]
    for i in range(2):
        pieces += [rows_of('ffn_w_gate', i), rows_of('ffn_w_up', i), rows_of('ffn_w_down', i)]
    n_rows = sum(p.shape[0] for p in pieces)
    pack = jnp.concatenate(pieces + [jnp.zeros((PACK_ROWS - n_rows, D), BF16)], axis=0)
    full = side_call(GatherRows(pack, lax.empty((N_CHIPS, PACK_ROWS, D), BF16), *W_BATCHES[0]), name="gather_weights")
    o_in, r_in = PACK_OFF['hy_in_t']
    wts = {"hy_in_t": hy_to_cat(full[:, o_in:o_in + r_in].reshape(N_CHIPS * r_in, D))}
    comm = {"pack": pack, "full": full}

    sp = {"norm_mix_g": [w['norm_mix_g'][i].reshape(1, D) for i in range(2)],
          "norm_ffn_g": [w['norm_ffn_g'][i].reshape(1, D) for i in range(2)],
          "hy_conv_w": sp['hy_conv_w'][0], "hy_conv_b": w['hy_conv_b'],
          "hy_dt_bias": w['hy_dt_bias'].reshape(SSM_HEADS, 1), "hy_a_log": w['hy_a_log'].reshape(SSM_HEADS, 1),
          "hy_d_skip": w['hy_d_skip'].reshape(SSM_HEADS, 1), "hy_ssm_norm_g": w['hy_ssm_norm_g'],
          "rel_table": w['rel_table'], "cv_b_pw1": sp['cv_b_pw1'], "cv_w_dw": sp['cv_w_dw'][0], "cv_b_dw": sp['cv_b_dw'],
          "cv_ln_g": sp['cv_ln_g'], "cv_ln_b": sp['cv_ln_b'], "cv_b_pw2": sp['cv_b_pw2'],
          "final_norm_g": w['final_norm_g'].reshape(1, D)}

    loss_rows, grad_x, g, dmods = device_step(x_in[0], tgt[0], mods, wts, sp, comm)

    dmod = jnp.stack([jnp.concatenate([d.reshape(-1) for d in dmods[i]]) for i in range(2)])
    small = {'ada_b': dmod, 'norm_mix_g': jnp.stack([g[f'norm_mix_g{i}'].reshape(-1) for i in range(2)]),
             'norm_ffn_g': jnp.stack([g[f'norm_ffn_g{i}'].reshape(-1) for i in range(2)]),
             'loss': jnp.sum(loss_rows).reshape(1)}
    for nm in SMALL_GRADS:
        if nm not in small:
            small[nm] = g[nm]
    vec = _to_lanes(jnp.concatenate([small[nm].reshape(-1) for nm in SMALL_GRADS]))
    vec_all = allgather_small(vec, name="gather_small_grads")
    tot = _split(sum_devices(vec_all, name="sum_small_grads").reshape(-1), SMALL_GRADS)
    dmod_all = vec_all.reshape(N_DEV, -1)[:, :2 * 6 * D].reshape(N_DEV, 2, 6 * D)

    recv = comm["recv"]
    own_a = lax.dynamic_index_in_dim(comm["ga"], chip, axis=0, keepdims=False)
    part_a = sum_slots(own_a, recv, name="sum_chip_grads")
    red_a = rowmap(f_add, [part_a, sibling_swap(part_a, name="swap_grads")], [], [F32], name="sum_core_grads")[0]
    recv_b = comm["recv_b"]
    own_b = lax.dynamic_index_in_dim(comm["gb"], chip, axis=0, keepdims=False)
    mine_half = sum_slots(own_b, recv_b, name="sum_in_chips")
    its_half = sibling_swap(mine_half, name="swap_in")
    red_b = jnp.concatenate([jnp.where(ci == 0, mine_half, its_half), jnp.where(ci == 0, its_half, mine_half)], axis=0)

    def shard_grad(nm, i=None):
        key = {'hy_w_in': 'hy_in_t', 'hy_w_out': 'hy_out', 'cv_w_pw1': 'pw1_t', 'cv_w_pw2': 'pw2'}.get(nm)
        if key is None:
            key = {'ffn_w_gate': 'gate_t', 'ffn_w_up': 'up_t', 'ffn_w_down': 'down'}[nm] + str(i)
        if key == 'hy_in_t':
            a = red_b[:PACK_OFF[key][1]]
        else:
            o, r = GA_OFF[key]
            a = red_a[o:o + r]
        return a.T if key.endswith('_t') or key[:-1].endswith('_t') else a

    grads = {}
    grads['hy_w_in'] = shard_grad('hy_w_in')[None]
    grads['hy_w_out'] = shard_grad('hy_w_out')[None]
    grads['cv_w_pw1'] = shard_grad('cv_w_pw1')[None]
    grads['cv_w_pw2'] = shard_grad('cv_w_pw2')[None]
    for nm in ('ffn_w_gate', 'ffn_w_up', 'ffn_w_down'):
        grads[nm] = jnp.stack([shard_grad(nm, i) for i in range(2)])
    cs16 = jnp.pad(cs_all, ((0, 8), (0, 0)))
    dm_mine = lax.dynamic_slice_in_dim(dmod_all, chip * 1536, 1536, axis=2)
    dm16 = jnp.pad(dm_mine, ((0, 8), (0, 0), (0, 0)))
    grads['ada_w'] = jnp.stack([matmul(cs16, dm16[:, i], mode="tn", out_dtype=F32, name=f"ada_dw{i}") for i in range(2)])
    for nm, shp in SMALL_GRADS.items():
        if nm == 'loss':
            continue
        if nm in SMALL_SHARDED:
            n = w[nm].shape[-1]
            grads[nm] = lax.dynamic_slice_in_dim(tot[nm], chip * n, n, axis=len(shp) - 1)
        else:
            grads[nm] = tot[nm].reshape(w[nm].shape)

    delta, new_m, new_v = {}, {}, {}
    for nm in BIG:
        delta[nm], new_m[nm], new_v[nm] = adamw(w[nm], grads[nm], m_in[nm], v_in[nm], name="adamw_" + nm)
    smalls = [nm for nm in WEIGHTS if nm not in BIG]
    packed = [_to_lanes(jnp.concatenate([d[nm].reshape(-1) for nm in smalls])) for d in (w, grads, m_in, v_in)]
    res = rowmap(f_adamw, packed, [], [F32] * 3, name="adamw_small", tr=_rows_tile(packed[0].shape[0]))
    for d, r in zip((delta, new_m, new_v), res, strict=True):
        d.update(_split(r.reshape(-1), {nm: w[nm].shape for nm in smalls}))

    loss = tot['loss'].reshape(())
    return (loss, grad_x[None], *[grads[nm] for nm in WEIGHTS], *[delta[nm] for nm in WEIGHTS],
            *[new_m[nm] for nm in WEIGHTS], *[new_v[nm] for nm in WEIGHTS])
```

```python
import functools
import math

import jax
import jax.numpy as jnp
import numpy as np
from jax import lax
from jax.experimental import pallas as pl
from jax.experimental.pallas import tpu as pltpu

F32 = jnp.float32
BF16 = jnp.bfloat16
MESH = pl.DeviceIdType.MESH

D = 1024
S = 4096
EPS = 1e-6
SSM_INNER = 2048
SSM_HEADS = 32
SSM_HDIM = 64
SSM_GROUPS = 4
SSM_STATE = 128
SSM_CONVK = 4
SSM_CONV_DIM = 3072
CHUNK = 128
N_CHUNKS = S // CHUNK
ATT_HEADS = 16
ATT_HDIM = 64
ATT_PATTERNS = ((128, 1), (512, 4), (2048, 16))
ATT_BLK = 128
REL_BUCKETS = 32
REL_MAX_DIST = 2048
CONV_WIDTH = 31
FFN_HIDDEN = 2816
N_CHIPS = 4
N_DEV = 8
ADAM_LR, ADAM_B1, ADAM_B2, ADAM_EPS, ADAM_WD, ADAM_STEP = 0.001, 0.9, 0.999, 1e-08, 0.01, 10

VMEM_LIMIT_BYTES = 56 * 1024 * 1024
LANES = 128


def _cparams(*sem):
    return pltpu.CompilerParams(dimension_semantics=sem, vmem_limit_bytes=VMEM_LIMIT_BYTES)


def _pick(n, cap, mult=LANES):
    best = None
    for t in range(mult, min(n, cap) + 1, mult):
        if n % t == 0:
            best = t
    return best or n


def _dot(a, b, ca, cb):
    return lax.dot_general(a.astype(BF16), b.astype(BF16), (((ca,), (cb,)), ((), ())), preferred_element_type=F32)


@jax.custom_vjp
def mm_nt(a, b):
    return _dot(a, b, 1, 1)


def _mm_nt_fwd(a, b):
    return _dot(a, b, 1, 1), (a, b)


def _mm_nt_bwd(res, g):
    a, b = res
    return _dot(g, b, 1, 0).astype(a.dtype), _dot(g, a, 0, 0).astype(b.dtype)


mm_nt.defvjp(_mm_nt_fwd, _mm_nt_bwd)


def matmul(a, b, *, mode, out_dtype, name, n=None, b_off=0, tm_cap=1024, tn_cap=512, tk_cap=3584, side=None,
           out_t=False):
    if mode == "tn":
        k_dim, m_dim = a.shape
    else:
        m_dim, k_dim = a.shape
    n_dim = n if n is not None else (b.shape[0] if mode == "nt" else b.shape[1])
    tm = m_dim if m_dim < LANES else _pick(m_dim, tm_cap)
    tn = _pick(n_dim, tn_cap)
    tk = k_dim if k_dim < LANES else _pick(k_dim, tk_cap)
    assert m_dim % tm == 0 and n_dim % tn == 0 and k_dim % tk == 0 and b_off % tn == 0
    nk = k_dim // tk
    off = b_off // tn
    if mode == "nn":
        a_spec = pl.BlockSpec((tm, tk), lambda i, j, k: (i, k))
        b_spec = pl.BlockSpec((tk, tn), lambda i, j, k: (k, j))
        ca, cb = 1, 0
    elif mode == "nt":
        a_spec = pl.BlockSpec((tm, tk), lambda i, j, k: (i, k))
        b_spec = pl.BlockSpec((tn, tk), lambda i, j, k: (j + off, k))
        ca, cb = 1, 1
    else:
        a_spec = pl.BlockSpec((tk, tm), lambda i, j, k: (k, i))
        b_spec = pl.BlockSpec((tk, tn), lambda i, j, k: (k, j))
        ca, cb = 0, 0

    def emit(o_ref, val):
        o_ref[...] = (val.T if out_t else val).astype(o_ref.dtype)

    def body(a_ref, b_ref, o_ref, acc_ref):
        part = _dot(a_ref[...], b_ref[...], ca, cb)
        if nk == 1:
            emit(o_ref, part)
        else:
            k = pl.program_id(2)

            @pl.when(k == 0)
            def _():
                acc_ref[...] = part

            @pl.when(k > 0)
            def _():
                acc_ref[...] += part

            @pl.when(k == nk - 1)
            def _():
                emit(o_ref, acc_ref[...])

    if out_t:
        out_shape, out_spec = (n_dim, m_dim), pl.BlockSpec((tn, tm), lambda i, j, k: (j, i))
    else:
        out_shape, out_spec = (m_dim, n_dim), pl.BlockSpec((tm, tn), lambda i, j, k: (i, j))
    (out,), side_dst = grid_call(
        body, (a, b), name=name,
        out_shape=[jax.ShapeDtypeStruct(out_shape, out_dtype)],
        grid=(m_dim // tm, n_dim // tn, nk),
        in_specs=[a_spec, b_spec],
        out_specs=[out_spec],
        scratch_shapes=[pltpu.VMEM((tm, tn), F32)],
        semantics=("parallel", "parallel", "arbitrary"), side=side)
    return out if side is None else (out, side_dst)


def _f32(xs):
    return [x.astype(F32) for x in xs]


def rowmap(f, rows, consts, out_dtypes, *, name, tr=256):
    r_dim = rows[0].shape[0]
    tr = _pick(r_dim, tr, mult=8)
    assert r_dim % tr == 0
    nr, nc = len(rows), len(consts)
    outs = jax.eval_shape(lambda *xs: f(*xs), *[jax.ShapeDtypeStruct((tr, x.shape[1]), F32) for x in rows],
                          *[jax.ShapeDtypeStruct(x.shape, F32) for x in consts])

    def body(*refs):
        res = f(*_f32([r[...] for r in refs[:nr + nc]]))
        for o_ref, o in zip(refs[nr + nc:], res, strict=True):
            o_ref[...] = o.astype(o_ref.dtype)

    return pl.pallas_call(
        body, name=name,
        out_shape=[jax.ShapeDtypeStruct((r_dim, o.shape[1]), dt) for o, dt in zip(outs, out_dtypes, strict=True)],
        grid=(r_dim // tr,),
        in_specs=[pl.BlockSpec((tr, x.shape[1]), lambda i: (i, 0)) for x in rows]
        + [pl.BlockSpec(x.shape, lambda i: (0, 0)) for x in consts],
        out_specs=[pl.BlockSpec((tr, o.shape[1]), lambda i: (i, 0)) for o in outs],
        compiler_params=_cparams("parallel"),
    )(*rows, *consts)


def rowmap_bwd(f, rows, consts, cts, *, name, row_grad, row_dtypes=None, tr=256, emit=(), row_add=None):
    r_dim = rows[0].shape[0]
    tr = _pick(r_dim, tr, mult=8)
    assert r_dim % tr == 0
    nr, nc, nct = len(rows), len(consts), len(cts)
    gi = [i for i, flag in enumerate(row_grad) if flag]
    row_dtypes = row_dtypes or [F32] * len(gi)
    row_add = row_add or [None] * len(gi)
    adds = [a for a in row_add if a is not None]
    outs = jax.eval_shape(lambda *xs: f(*xs), *[jax.ShapeDtypeStruct((tr, x.shape[1]), F32) for x in rows],
                          *[jax.ShapeDtypeStruct(x.shape, F32) for x in consts])

    def body(*refs):
        ins = _f32([r[...] for r in refs[:nr + nc]])
        ct = _f32([r[...] for r in refs[nr + nc:nr + nc + nct]])
        add_refs = list(refs[nr + nc + nct:nr + nc + nct + len(adds)])
        o_refs = refs[nr + nc + nct + len(adds):]
        res, vjp = jax.vjp(f, *ins)
        grads = vjp(tuple(ct))
        for o_ref, i, a in zip(o_refs[:len(gi)], gi, row_add):
            g = grads[i] if a is None else grads[i] + add_refs.pop(0)[...].astype(F32)
            o_ref[...] = g.astype(o_ref.dtype)
        first = pl.program_id(0) == 0
        for o_ref, g in zip(o_refs[len(gi):len(gi) + nc], grads[nr:]):
            @pl.when(first)
            def _(o_ref=o_ref, g=g):
                o_ref[...] = g

            @pl.when(jnp.logical_not(first))
            def _(o_ref=o_ref, g=g):
                o_ref[...] += g
        for o_ref, i in zip(o_refs[len(gi) + nc:], emit):
            o_ref[...] = res[i].astype(o_ref.dtype)

    out_shape = ([jax.ShapeDtypeStruct(rows[i].shape, dt) for i, dt in zip(gi, row_dtypes, strict=True)]
                 + [jax.ShapeDtypeStruct(x.shape, F32) for x in consts]
                 + [jax.ShapeDtypeStruct((r_dim, outs[i].shape[1]), F32) for i in emit])
    out_specs = ([pl.BlockSpec((tr, rows[i].shape[1]), lambda i_: (i_, 0)) for i in gi]
                 + [pl.BlockSpec(x.shape, lambda i_: (0, 0)) for x in consts]
                 + [pl.BlockSpec((tr, outs[i].shape[1]), lambda i_: (i_, 0)) for i in emit])
    res = pl.pallas_call(
        body, name=name,
        out_shape=out_shape,
        grid=(r_dim // tr,),
        in_specs=[pl.BlockSpec((tr, x.shape[1]), lambda i: (i, 0)) for x in rows]
        + [pl.BlockSpec(x.shape, lambda i: (0, 0)) for x in consts]
        + [pl.BlockSpec((tr, x.shape[1]), lambda i: (i, 0)) for x in list(cts) + adds],
        out_specs=out_specs,
        compiler_params=_cparams("arbitrary"),
    )(*rows, *consts, *cts, *adds)
    return res[:len(gi)], res[len(gi):len(gi) + nc], res[len(gi) + nc:]


def transpose(a, *, name, out_dtype=BF16, tr=512, tc=512):
    r_dim, c_dim = a.shape
    tr, tc = _pick(r_dim, tr), _pick(c_dim, tc)

    def body(a_ref, o_ref):
        o_ref[...] = a_ref[...].astype(F32).T.astype(o_ref.dtype)

    return pl.pallas_call(
        body, name=name, out_shape=jax.ShapeDtypeStruct((c_dim, r_dim), out_dtype),
        grid=(r_dim // tr, c_dim // tc),
        in_specs=[pl.BlockSpec((tr, tc), lambda i, j: (i, j))],
        out_specs=pl.BlockSpec((tc, tr), lambda i, j: (j, i)),
        compiler_params=_cparams("parallel", "parallel"),
    )(a)


CONV_HALO = 32
CONV_ROWS = 256


def conv_fwd(x, w, b, *, name, cb=256, chunk_rows=CONV_ROWS):
    s_dim, c_dim = x.shape
    taps = w.shape[0]
    assert taps - 1 <= CONV_HALO and s_dim % chunk_rows == 0 and c_dim % cb == 0
    n_chunks = s_dim // chunk_rows
    ext = chunk_rows + CONV_HALO

    def body(x_ref, w_ref, b_ref, o_ref, xp_ref):
        xp_ref[pl.ds(0, CONV_HALO), :] = jnp.zeros((CONV_HALO, cb), F32)
        xp_ref[pl.ds(CONV_HALO, s_dim), :] = x_ref[...].astype(F32)
        wv = w_ref[...].astype(F32)
        bv = b_ref[...].astype(F32)

        def chunk(t, carry):
            base = pl.multiple_of(t * chunk_rows, chunk_rows)
            xe = xp_ref[pl.ds(base, ext), :]
            acc = jnp.broadcast_to(bv, (chunk_rows, cb))
            for j in range(taps):
                sh = xe if j == 0 else pltpu.roll(xe, shift=j, axis=0)
                acc = acc + wv[taps - 1 - j:taps - j, :] * sh[CONV_HALO:, :]
            o_ref[pl.ds(base, chunk_rows), :] = acc
            return carry

        lax.fori_loop(0, n_chunks, chunk, 0)

    return pl.pallas_call(
        body, name=name,
        out_shape=jax.ShapeDtypeStruct((s_dim, c_dim), F32),
        grid=(c_dim // cb,),
        in_specs=[pl.BlockSpec((s_dim, cb), lambda i: (0, i)), pl.BlockSpec((taps, cb), lambda i: (0, i)),
                  pl.BlockSpec((1, cb), lambda i: (0, i))],
        out_specs=pl.BlockSpec((s_dim, cb), lambda i: (0, i)),
        scratch_shapes=[pltpu.VMEM((s_dim + CONV_HALO, cb), F32)],
        compiler_params=_cparams("parallel"),
    )(x, w, b)


def conv_bwd(x, w, g, *, name, cb=256, chunk_rows=CONV_ROWS, dx_dtype=F32):
    s_dim, c_dim = x.shape
    taps = w.shape[0]
    n_chunks = s_dim // chunk_rows
    ext = chunk_rows + CONV_HALO

    def rows8(a):
        return jnp.sum(a.reshape(chunk_rows // 8, 8, cb), axis=0)

    def body(x_ref, w_ref, g_ref, dx_ref, dw_ref, db_ref, xp_ref, gp_ref, acc_ref):
        xp_ref[pl.ds(0, CONV_HALO), :] = jnp.zeros((CONV_HALO, cb), F32)
        xp_ref[pl.ds(CONV_HALO, s_dim), :] = x_ref[...].astype(F32)
        gp_ref[pl.ds(0, s_dim), :] = g_ref[...].astype(F32)
        gp_ref[pl.ds(s_dim, CONV_HALO), :] = jnp.zeros((CONV_HALO, cb), F32)
        acc_ref[...] = jnp.zeros_like(acc_ref)
        wv = w_ref[...].astype(F32)

        def chunk(t, carry):
            base = pl.multiple_of(t * chunk_rows, chunk_rows)
            xe = xp_ref[pl.ds(base, ext), :]
            ge = gp_ref[pl.ds(base, ext), :]
            gc = ge[:chunk_rows, :]
            dx = jnp.zeros((chunk_rows, cb), F32)
            for j in range(taps):
                xs = xe if j == 0 else pltpu.roll(xe, shift=j, axis=0)
                gs = ge if j == 0 else pltpu.roll(ge, shift=ext - j, axis=0)
                k = taps - 1 - j
                dx = dx + wv[k:k + 1, :] * gs[:chunk_rows, :]
                acc_ref[8 * k:8 * k + 8, :] += rows8(gc * xs[CONV_HALO:, :])
            acc_ref[8 * taps:8 * taps + 8, :] += rows8(gc)
            dx_ref[pl.ds(base, chunk_rows), :] = dx.astype(dx_ref.dtype)
            return carry

        lax.fori_loop(0, n_chunks, chunk, 0)
        sums = jnp.sum(acc_ref[...].reshape(taps + 1, 8, cb), axis=1)
        dw_ref[...] = sums[0:taps, :]
        db_ref[...] = sums[taps:taps + 1, :]

    return pl.pallas_call(
        body, name=name,
        out_shape=[jax.ShapeDtypeStruct((s_dim, c_dim), dx_dtype), jax.ShapeDtypeStruct((taps, c_dim), F32),
                   jax.ShapeDtypeStruct((1, c_dim), F32)],
        grid=(c_dim // cb,),
        in_specs=[pl.BlockSpec((s_dim, cb), lambda i: (0, i)), pl.BlockSpec((taps, cb), lambda i: (0, i)),
                  pl.BlockSpec((s_dim, cb), lambda i: (0, i))],
        out_specs=[pl.BlockSpec((s_dim, cb), lambda i: (0, i)), pl.BlockSpec((taps, cb), lambda i: (0, i)),
                   pl.BlockSpec((1, cb), lambda i: (0, i))],
        scratch_shapes=[pltpu.VMEM((s_dim + CONV_HALO, cb), F32), pltpu.VMEM((s_dim + CONV_HALO, cb), F32),
                        pltpu.VMEM((8 * (taps + 1), cb), F32)],
        compiler_params=_cparams("parallel"),
    )(x, w, g)


def _softplus(x):
    return jnp.maximum(x, 0.0) + jnp.log(1.0 + jnp.exp(-jnp.abs(x)))


HEADS_PER_GROUP = SSM_HEADS // SSM_GROUPS


def _bdot(a, b, ca, cb):
    return lax.dot_general(a.astype(BF16), b.astype(BF16), (((ca,), (cb,)), ((0,), (0,))), preferred_element_type=F32)


@jax.custom_vjp
def bmm(a, b):
    return _bdot(a, b, 2, 1)


def _bmm_fwd(a, b):
    return _bdot(a, b, 2, 1), (a, b)


def _bmm_bwd(res, g):
    a, b = res
    return _bdot(g, b, 2, 2).astype(a.dtype), _bdot(a, g, 1, 1).astype(b.dtype)


bmm.defvjp(_bmm_fwd, _bmm_bwd)


@jax.custom_vjp
def bmm_nt(a, b):
    return _bdot(a, b, 2, 2)


def _bmm_nt_fwd(a, b):
    return _bdot(a, b, 2, 2), (a, b)


def _bmm_nt_bwd(res, g):
    a, b = res
    return _bdot(g, b, 2, 1).astype(a.dtype), _bdot(g, a, 1, 1).astype(b.dtype)


bmm_nt.defvjp(_bmm_nt_fwd, _bmm_nt_bwd)


@jax.custom_vjp
def bmm_tn(a, b):
    return _bdot(a, b, 1, 1)


def _bmm_tn_fwd(a, b):
    return _bdot(a, b, 1, 1), (a, b)


def _bmm_tn_bwd(res, g):
    a, b = res
    return _bdot(b, g, 2, 2).astype(a.dtype), _bdot(a, g, 2, 1).astype(b.dtype)


bmm_tn.defvjp(_bmm_tn_fwd, _bmm_tn_bwd)


ATT_PAIRS = ATT_HEADS // 2
PAIR_W = 2 * ATT_HDIM


def att_pairs(q, kp, kc, vp, vc, bias, has_prev):
    t, b, w = q.shape
    i = lax.broadcasted_iota(jnp.int32, (1, b, b), 1)
    j = lax.broadcasted_iota(jnp.int32, (1, b, b), 2)
    first = lax.broadcasted_iota(jnp.int32, (1, 1, w), 2) < ATT_HDIM
    scale = ATT_HDIM ** -0.5
    outs, lses = [], []
    for ab in range(2):
        qh = jnp.where(first if ab == 0 else jnp.logical_not(first), q, 0.0)
        sp = jnp.where(jnp.logical_and(j >= i, has_prev), bmm_nt(qh, kp) * scale + bias[:, ab, 0], -1e30)
        sc = jnp.where(j <= i, bmm_nt(qh, kc) * scale + bias[:, ab, 1], -1e30)
        m = lax.stop_gradient(jnp.maximum(jnp.max(sp, axis=2, keepdims=True), jnp.max(sc, axis=2, keepdims=True)))
        pp, pc = jnp.exp(sp - m), jnp.exp(sc - m)
        l = jnp.sum(pp, axis=2, keepdims=True) + jnp.sum(pc, axis=2, keepdims=True)
        outs.append(bmm(pp / l, vp) + bmm(pc / l, vc))
        lses.append(jnp.broadcast_to(m + jnp.log(l), (t, b, w)))
    return jnp.where(first, outs[0], outs[1]), jnp.where(first, lses[0], lses[1])


def _pair_tiles(ref):
    return jnp.stack([ref[:, PAIR_W * t:PAIR_W * (t + 1)] for t in range(ATT_PAIRS)])


def _store_pair_tiles(ref, val):
    for t in range(ATT_PAIRS):
        ref[:, PAIR_W * t:PAIR_W * (t + 1)] = val[t].astype(ref.dtype)


def pair_bias(bias):
    return bias.reshape(ATT_PAIRS, 2, 2, ATT_BLK, ATT_BLK)


def att2_fwd(q, k, v, bias, nb, cols, *, name, side=None):
    n_blocks = S // ATT_BLK
    qc, kc, vc = cols

    def body(q_ref, k_ref, v_ref, b_ref, o_ref, l_ref, kprev, vprev):
        blk = pl.program_id(0)

        @pl.when(blk == 0)
        def _():
            kprev[...] = jnp.zeros_like(kprev)
            vprev[...] = jnp.zeros_like(vprev)

        k3, v3 = _pair_tiles(k_ref), _pair_tiles(v_ref)
        o, lse = att_pairs(_pair_tiles(q_ref), kprev[...], k3, vprev[...], v3, b_ref[...], (blk % nb) != 0)
        _store_pair_tiles(o_ref, o)
        _store_pair_tiles(l_ref, lse)
        kprev[...] = k3
        vprev[...] = v3

    def spec(c):
        return pl.BlockSpec((ATT_BLK, D), lambda b: (b, c))

    return grid_call(
        body, (q, k, v, bias), name=name,
        out_shape=[jax.ShapeDtypeStruct((S, D), BF16), jax.ShapeDtypeStruct((S, D), F32)], grid=(n_blocks,),
        in_specs=[spec(qc), spec(kc), spec(vc), pl.BlockSpec(bias.shape, lambda b: (0, 0, 0, 0, 0))],
        out_specs=[spec(0), spec(0)],
        scratch_shapes=[pltpu.VMEM((ATT_PAIRS, ATT_BLK, PAIR_W), BF16), pltpu.VMEM((ATT_PAIRS, ATT_BLK, PAIR_W), BF16)],
        semantics=("arbitrary",), side=side)


def att2_bwd(q, k, v, bias, do, dlse, nb, cols, *, name, side=None):
    n_blocks = S // ATT_BLK
    qc, kc, vc = cols

    def body(q_ref, k_ref, v_ref, b_ref, do_ref, dl_ref, dq_ref, dk_ref, dv_ref, db_ref, kprev, vprev, dk_own, dv_own):
        blk = pl.program_id(0)

        @pl.when(blk == 0)
        def _():
            for r in (kprev, vprev, dk_own, dv_own, db_ref):
                r[...] = jnp.zeros_like(r)

        @pl.when(blk < n_blocks)
        def _():
            k3, v3 = _pair_tiles(k_ref), _pair_tiles(v_ref)
            ins = _f32([_pair_tiles(q_ref), kprev[...], k3, vprev[...], v3]) + [b_ref[...]]
            _, vjp = jax.vjp(functools.partial(att_pairs, has_prev=(blk % nb) != 0), *ins)
            dq, dkp, dkc, dvp, dvc, db = vjp(tuple(_f32([_pair_tiles(do_ref), _pair_tiles(dl_ref)])))
            _store_pair_tiles(dq_ref, dq)
            _store_pair_tiles(dk_ref, dk_own[...] + dkp)
            _store_pair_tiles(dv_ref, dv_own[...] + dvp)
            dk_own[...] = dkc
            dv_own[...] = dvc
            db_ref[...] += db
            kprev[...] = k3
            vprev[...] = v3

        @pl.when(blk == n_blocks)
        def _():
            _store_pair_tiles(dk_ref, dk_own[...])
            _store_pair_tiles(dv_ref, dv_own[...])

    def spec(c):
        return pl.BlockSpec((ATT_BLK, D), lambda b: (jnp.minimum(b, n_blocks - 1), c))

    late = pl.BlockSpec((ATT_BLK, D), lambda b: (jnp.maximum(b - 1, 0), 0))
    bspec = pl.BlockSpec(bias.shape, lambda b: (0, 0, 0, 0, 0))
    tile_f32 = pltpu.VMEM((ATT_PAIRS, ATT_BLK, PAIR_W), F32)
    tile_bf16 = pltpu.VMEM((ATT_PAIRS, ATT_BLK, PAIR_W), BF16)
    return grid_call(
        body, (q, k, v, bias, do, dlse), name=name,
        out_shape=[jax.ShapeDtypeStruct((S, D), BF16), jax.ShapeDtypeStruct((S, D), BF16),
                   jax.ShapeDtypeStruct((S, D), BF16), jax.ShapeDtypeStruct(bias.shape, F32)],
        grid=(n_blocks + 1,),
        in_specs=[spec(qc), spec(kc), spec(vc), bspec, spec(0), spec(0)],
        out_specs=[spec(0), late, late, bspec],
        scratch_shapes=[tile_bf16, tile_bf16, tile_f32, tile_f32],
        semantics=("arbitrary",), side=side)


def regroup(a, dil, inverse=False):
    if dil == 1:
        return a
    c_dim = a.shape[1]
    shape = (dil, S // dil, c_dim) if inverse else (S // dil, dil, c_dim)
    return jnp.transpose(a.reshape(shape), (1, 0, 2)).reshape(S, c_dim)


SSD_PAIRS = SSM_HEADS // 2
PAIRS_PER_GROUP = SSD_PAIRS // SSM_GROUPS
GROUP_W = HEADS_PER_GROUP * SSM_HDIM


SSD_GROUPS_PER_STEP = 2
SSD_STEP_PAIRS = PAIRS_PER_GROUP * SSD_GROUPS_PER_STEP
SSD_STEP_W = GROUP_W * SSD_GROUPS_PER_STEP


def ssd_pairs(x, dtraw, dt_bias, a_log, dskip, bms, cms, prev):
    t, q, w = x.shape
    n = bms[0].shape[1]
    per = t // len(bms)

    def by_pair(mats):
        return jnp.concatenate([jnp.broadcast_to(m[None], (per,) + m.shape) for m in mats], axis=0)
    li = lax.broadcasted_iota(jnp.int32, (1, q, q), 1)
    si = lax.broadcasted_iota(jnp.int32, (1, q, q), 2)
    first_lane = lax.broadcasted_iota(jnp.int32, (1, 1, w), 2) < SSM_HDIM
    first_row = lax.broadcasted_iota(jnp.int32, (1, w, 1), 1) < SSM_HDIM

    def to_col(row):
        return jnp.sum(jnp.where(li == si, jnp.broadcast_to(row, (t, q, q)), 0.0), axis=2, keepdims=True)

    def lanes(a0, a1):
        return jnp.where(first_lane, a0, a1)

    dt_col, acs_col, total, lmat = [], [], [], []
    for ab in range(2):
        dt_row = _softplus(dtraw[ab] + dt_bias[ab])
        a_row = dt_row * (-jnp.exp(a_log[ab]))
        a_col = to_col(a_row)
        acs_c = jnp.sum(jnp.where(si <= li, jnp.broadcast_to(a_row, (t, q, q)), 0.0), axis=2, keepdims=True)
        acs_r = jnp.sum(jnp.where(li <= si, jnp.broadcast_to(a_col, (t, q, q)), 0.0), axis=1, keepdims=True)
        dt_col.append(to_col(dt_row))
        acs_col.append(acs_c)
        total.append(jnp.sum(a_row, axis=2, keepdims=True))
        lmat.append(jnp.exp(jnp.where(li >= si, acs_c - acs_r, -1e30)))
    cb = by_pair([mm_nt(c_, b_) for c_, b_ in zip(cms, bms, strict=True)])
    bmb, cmb = by_pair(bms), by_pair(cms)
    xdt = x * lanes(dt_col[0], dt_col[1])
    y = lanes(bmm(cb * lmat[0], xdt), bmm(cb * lmat[1], xdt))
    y = y + bmm_nt(cmb, prev) * lanes(jnp.exp(acs_col[0]), jnp.exp(acs_col[1]))
    y = y + lanes(dskip[0], dskip[1]) * x
    state = bmm_tn(xdt * lanes(jnp.exp(total[0] - acs_col[0]), jnp.exp(total[1] - acs_col[1])), bmb)
    return y, jnp.where(first_row, jnp.exp(total[0]), jnp.exp(total[1])) * prev + state


def _group_tiles(ref):
    return jnp.stack([ref[:, PAIR_W * t:PAIR_W * (t + 1)] for t in range(SSD_STEP_PAIRS)])


def _store_group_tiles(ref, val):
    for t in range(SSD_STEP_PAIRS):
        ref[:, PAIR_W * t:PAIR_W * (t + 1)] = val[t]


def _bc_groups(ref):
    return tuple(ref[:, SSM_STATE * i:SSM_STATE * (i + 1)] for i in range(SSD_GROUPS_PER_STEP))


def _by_pair(a):
    return jnp.transpose(a.reshape(SSD_PAIRS, 2, 1, -1), (1, 0, 2, 3))


def _by_head(a):
    return jnp.transpose(a, (1, 0, 2, 3)).reshape(SSM_HEADS, -1)


def _ssd2_specs(chunk_of):
    tp = SSD_STEP_PAIRS
    xspec = pl.BlockSpec((CHUNK, SSD_STEP_W), lambda g, c: (chunk_of(c), g))
    tspec = pl.BlockSpec((2, tp, 1, CHUNK), lambda g, c: (0, g, 0, chunk_of(c)))
    hp = pl.BlockSpec((2, tp, 1, 1), lambda g, c: (0, g, 0, 0))
    gspec = pl.BlockSpec((CHUNK, SSD_GROUPS_PER_STEP * SSM_STATE), lambda g, c: (chunk_of(c), g))
    sspec = pl.BlockSpec((1, tp, PAIR_W, SSM_STATE), lambda g, c: (chunk_of(c), g, 0, 0))
    return xspec, tspec, hp, gspec, sspec


def ssd2_fwd(xs, dtraw_t, dt_bias, a_log, dskip, bm, cm, side=None):
    def body(x_ref, dt_ref, dtb_ref, al_ref, dk_ref, bm_ref, cm_ref, y_ref, prev_ref, state_ref):
        @pl.when(pl.program_id(1) == 0)
        def _():
            state_ref[...] = jnp.zeros_like(state_ref)

        prev = state_ref[...]
        prev_ref[0] = prev
        y, nxt = ssd_pairs(_group_tiles(x_ref), dt_ref[...], dtb_ref[...], al_ref[...], dk_ref[...], _bc_groups(bm_ref),
                           _bc_groups(cm_ref), prev)
        _store_group_tiles(y_ref, y)
        state_ref[...] = nxt

    xspec, tspec, hp, gspec, sspec = _ssd2_specs(lambda c: c)
    return grid_call(
        body, (xs, _by_pair(dtraw_t), _by_pair(dt_bias), _by_pair(a_log), _by_pair(dskip), bm, cm), name="ssd_fwd",
        out_shape=[jax.ShapeDtypeStruct((S, SSM_INNER), F32),
                   jax.ShapeDtypeStruct((N_CHUNKS, SSD_PAIRS, PAIR_W, SSM_STATE), F32)],
        grid=(SSM_GROUPS // SSD_GROUPS_PER_STEP, N_CHUNKS), in_specs=[xspec, tspec, hp, hp, hp, gspec, gspec],
        out_specs=[xspec, sspec],
        scratch_shapes=[pltpu.VMEM((SSD_STEP_PAIRS, PAIR_W, SSM_STATE), F32)],
        semantics=("parallel", "arbitrary"), side=side)


def ssd2_bwd(xs, dtraw_t, dt_bias, a_log, dskip, bm, cm, prev_all, dy, side=None):
    def body(x_ref, dt_ref, dtb_ref, al_ref, dk_ref, bm_ref, cm_ref, prev_ref, dy_ref,
             dx_ref, ddt_ref, ddtb_ref, dal_ref, ddk_ref, dbm_ref, dcm_ref, dstate_ref):
        @pl.when(pl.program_id(1) == 0)
        def _():
            for r in (dstate_ref, ddtb_ref, dal_ref, ddk_ref):
                r[...] = jnp.zeros_like(r)

        _, vjp = jax.vjp(ssd_pairs, _group_tiles(x_ref), dt_ref[...], dtb_ref[...], al_ref[...], dk_ref[...],
                         _bc_groups(bm_ref), _bc_groups(cm_ref), prev_ref[0])
        dx, ddt, ddtb, dal, ddk, dbms, dcms, dprev = vjp((_group_tiles(dy_ref), dstate_ref[...]))
        _store_group_tiles(dx_ref, dx)
        ddt_ref[...] = ddt
        ddtb_ref[...] += ddtb
        dal_ref[...] += dal
        ddk_ref[...] += ddk
        for i in range(SSD_GROUPS_PER_STEP):
            dbm_ref[:, SSM_STATE * i:SSM_STATE * (i + 1)] = dbms[i]
            dcm_ref[:, SSM_STATE * i:SSM_STATE * (i + 1)] = dcms[i]
        dstate_ref[...] = dprev

    xspec, tspec, hp, gspec, sspec = _ssd2_specs(lambda c: N_CHUNKS - 1 - c)
    par = jax.ShapeDtypeStruct((2, SSD_PAIRS, 1, 1), F32)
    res, side_dst = grid_call(
        body, (xs, _by_pair(dtraw_t), _by_pair(dt_bias), _by_pair(a_log), _by_pair(dskip), bm, cm, prev_all, dy),
        name="ssd_bwd",
        out_shape=[jax.ShapeDtypeStruct((S, SSM_INNER), F32), jax.ShapeDtypeStruct((2, SSD_PAIRS, 1, S), F32), par, par, par,
                   jax.ShapeDtypeStruct((S, SSM_GROUPS * SSM_STATE), F32),
                   jax.ShapeDtypeStruct((S, SSM_GROUPS * SSM_STATE), F32)],
        grid=(SSM_GROUPS // SSD_GROUPS_PER_STEP, N_CHUNKS), in_specs=[xspec, tspec, hp, hp, hp, gspec, gspec, sspec, xspec],
        out_specs=[xspec, tspec, hp, hp, hp, gspec, gspec],
        scratch_shapes=[pltpu.VMEM((SSD_STEP_PAIRS, PAIR_W, SSM_STATE), F32)],
        semantics=("parallel", "arbitrary"), side=side)
    return [res[0]] + [_by_head(r) for r in res[1:5]] + list(res[5:]), side_dst


def _silu(x):
    return x * jax.nn.sigmoid(x)


def _rms(x):
    return x * lax.rsqrt(jnp.mean(x * x, -1, keepdims=True) + EPS)


def f_normmod(x, g, sc, sh):
    return (_rms(x) * g * (1.0 + sc) + sh,)


def f_resid(x, mix, gate):
    return (x + gate * mix,)


def f_resid_bias(x, mix, gate, b):
    return (x + gate * (mix + b),)


def f_swiglu(hgu):
    return (_silu(hgu[:, :FFN_HIDDEN]) * hgu[:, FFN_HIDDEN:],)


def f_silu(x):
    return (_silu(x),)


def f_silu_xbc(x):
    y = _silu(x)
    n_b = SSM_GROUPS * SSM_STATE
    return y[:, :SSM_INNER], y[:, SSM_INNER:SSM_INNER + n_b], y[:, SSM_INNER + n_b:]


def f_gated_norm(y, z, g):
    return (_rms(y * _silu(z)) * g,)


def f_glu(y, b):
    y = y + b
    return (y[:, :D] * jax.nn.sigmoid(y[:, D:]),)


def f_ln_silu(u, g, b):
    mu = jnp.mean(u, -1, keepdims=True)
    var = jnp.mean(jnp.square(u - mu), -1, keepdims=True)
    return (_silu((u - mu) * lax.rsqrt(var + EPS) * g + b),)


def f_combine(o1, o2, o3, l1, l2, l3):
    m = lax.stop_gradient(jnp.maximum(jnp.maximum(l1, l2), l3))
    e1, e2, e3 = jnp.exp(l1 - m), jnp.exp(l2 - m), jnp.exp(l3 - m)
    return ((e1 * o1 + e2 * o2 + e3 * o3) / (e1 + e2 + e3),)


def f_head(x, tgt, g):
    return (0.5 * jnp.mean(jnp.square(_rms(x) * g - tgt), -1, keepdims=True),)


def f_sum3(a, b, c):
    return (a + b + c,)


def f_add(a, b):
    return (a + b,)


def f_adamw(w, g, m, v):
    m = ADAM_B1 * m + (1.0 - ADAM_B1) * g
    v = ADAM_B2 * v + (1.0 - ADAM_B2) * jnp.square(g)
    m_hat = m / (1.0 - ADAM_B1 ** ADAM_STEP)
    v_hat = v / (1.0 - ADAM_B2 ** ADAM_STEP)
    return -ADAM_LR * (m_hat / (jnp.sqrt(v_hat) + ADAM_EPS) + ADAM_WD * w), m, v


def _rows_tile(r, cap=256):
    return _pick(r, cap, mult=8)


def adamw(w, g, m, v, *, name):
    l_dim, r_dim, c_dim = w.shape
    tr = _rows_tile(r_dim, cap=128)

    def body(w_ref, g_ref, m_ref, v_ref, d_ref, mo_ref, vo_ref):
        d_ref[...], mo_ref[...], vo_ref[...] = f_adamw(w_ref[...], g_ref[...], m_ref[...], v_ref[...])

    spec = pl.BlockSpec((1, tr, c_dim), lambda l, i: (l, i, 0))
    return pl.pallas_call(
        body, name=name, out_shape=[jax.ShapeDtypeStruct(w.shape, F32)] * 3, grid=(l_dim, r_dim // tr),
        in_specs=[spec] * 4, out_specs=[spec] * 3, compiler_params=_cparams("parallel", "parallel"),
    )(w, g, m, v)


def _t5_bucket(dist):
    max_exact = REL_BUCKETS // 2
    n = jnp.maximum(dist, 1).astype(F32)
    large = max_exact + jnp.log(n / max_exact) / math.log(REL_MAX_DIST / max_exact) * (REL_BUCKETS - max_exact)
    large = jnp.minimum(large.astype(jnp.int32), REL_BUCKETS - 1)
    return jnp.where(dist < max_exact, dist, large)


def _att_buckets(dil):
    i = jnp.arange(ATT_BLK)[:, None]
    j = jnp.arange(2 * ATT_BLK)[None, :]
    bkt = _t5_bucket(jnp.maximum(ATT_BLK + i - j, 0) * dil)
    return jnp.transpose(bkt.reshape(ATT_BLK, 2, ATT_BLK), (1, 0, 2))


def att_bias(rel_table, p, dil):
    tab = rel_table[:, p * ATT_HEADS:(p + 1) * ATT_HEADS]
    onehot = (jnp.arange(REL_BUCKETS)[:, None] == _att_buckets(dil).reshape(1, -1)).astype(F32)
    bias = lax.dot_general(tab, onehot, (((0,), (0,)), ((), ())), precision=lax.Precision.HIGHEST)
    return bias.reshape(ATT_HEADS, 2, ATT_BLK, ATT_BLK)


def att_bias_grad(dbias, dil, *, name):
    onehot = (_att_buckets(dil).reshape(-1, 1) == jnp.arange(LANES)[None, :]).astype(BF16)
    dtab = matmul(dbias.reshape(ATT_HEADS, -1), onehot, mode="nn", out_dtype=F32, name=name, tk_cap=2048)
    return dtab[:, :REL_BUCKETS].T


HY_Z, HY_XBC, HY_DT, HY_Q, HY_K, HY_V = 2048, 3072, 32, 3072, 1024, 1024
HY_IN = HY_Z + HY_XBC + HY_DT + HY_Q + HY_K + HY_V
OFF_Z, OFF_XBC, OFF_Q, OFF_KV, OFF_DT = 0, 2048, 5120, 8192, 10240
HY_CAT = OFF_DT + LANES
DT_PAD = LANES


def hy_to_cat(w):
    z, xbc, dt, qkv = w[:2048], w[2048:5120], w[5120:5152], w[5152:]
    return jnp.concatenate([z, xbc, qkv, dt, jnp.zeros((DT_PAD - HY_DT,) + w.shape[1:], w.dtype)], axis=0)


def hy_from_cat(w, axis=0):
    part = lambda a, b: lax.slice_in_dim(w, a, b, axis=axis)
    return jnp.concatenate([part(0, 5120), part(OFF_DT, OFF_DT + HY_DT), part(5120, OFF_DT)], axis=axis)


def device_step(x, tgt, mods, wts, sp, comm=None):
    g = {}
    dmods = [[None] * 6 for _ in range(2)]
    wts = dict(wts)

    def wgrad(tokens_d, tokens_n, nm):
        return matmul(transpose(tokens_d, name=nm + "_t"), tokens_n, mode="nn", out_dtype=BF16, name=nm, out_t=True,
                      tk_cap=2048)

    def w_side(i):
        return None if comm is None else GatherRows(comm["pack"], comm["full"], *W_BATCHES[i])

    def g_side(i):
        return None if comm is None else ScatterRows(comm["ga"], comm["recv"], *G_BATCHES[i])

    def normmod(xi, gain, sc, sh, nm):
        return rowmap(f_normmod, [xi], [gain, sc, sh], [BF16], name=nm)[0]

    def ffn_fwd(xi, i, gate, nm):
        h = normmod(xi, sp["norm_ffn_g"][i], mods[i][4], mods[i][3], nm + "_norm")
        hgu = matmul(h, wts["gu_t"][i], mode="nt", out_dtype=BF16, name=nm + "_gu")
        act = rowmap(f_swiglu, [hgu], [], [BF16], name=nm + "_act", tr=128)[0]
        out = matmul(act, wts["down"][i], mode="nn", out_dtype=F32, name=nm + "_down")
        xo = rowmap(f_resid, [xi, out], [gate], [F32], name=nm + "_res")[0]
        return xo, (h, hgu, act, out)

    def ffn_bwd(dres, xi, i, saved, nm):
        h, hgu, act, out = saved
        (dout,), (dgate,), _ = rowmap_bwd(f_resid, [xi, out], [mods[i][5]], [dres], name=nm + "_res_b",
                                          row_grad=[False, True], row_dtypes=[BF16])
        dmods[i][5] = dgate
        dact = matmul(dout, wts["down"][i], mode="nt", out_dtype=BF16, name=nm + "_down_dx")
        g[f"down{i}"] = wgrad(dout, act, nm + "_down_dw")
        (dhgu,), _, _ = rowmap_bwd(f_swiglu, [hgu], [], [dact], name=nm + "_act_b", row_grad=[True],
                                   row_dtypes=[BF16], tr=128)
        g[f"gu_t{i}"] = wgrad(h, dhgu, nm + "_gu_dw")
        dh = matmul(dhgu, wts["gu_t"][i], mode="nn", out_dtype=F32, name=nm + "_gu_dx")
        (dres,), (dg_, dsc, dsh), _ = rowmap_bwd(f_normmod, [xi], [sp["norm_ffn_g"][i], mods[i][4], mods[i][3]], [dh],
                                                 name=nm + "_norm_b", row_grad=[True], row_add=[dres])
        g[f"norm_ffn_g{i}"] = dg_
        dmods[i][4], dmods[i][3] = dsc, dsh
        return dres

    h0 = normmod(x, sp["norm_mix_g"][0], mods[0][1], mods[0][0], "l0_norm")
    w_in = wts["hy_in_t"]
    z = matmul(h0, w_in, mode="nt", out_dtype=BF16, name="hy_z", n=HY_Z, b_off=OFF_Z)
    xbc_raw = matmul(h0, w_in, mode="nt", out_dtype=BF16, name="hy_xbc", n=HY_XBC, b_off=OFF_XBC)
    q = matmul(h0, w_in, mode="nt", out_dtype=BF16, name="hy_q", n=HY_Q, b_off=OFF_Q)
    kv = matmul(h0, w_in, mode="nt", out_dtype=BF16, name="hy_kv", n=HY_K + HY_V, b_off=OFF_KV)
    dtr = matmul(h0, w_in, mode="nt", out_dtype=F32, name="hy_dt", n=DT_PAD, b_off=OFF_DT)
    xbc_pre = conv_fwd(xbc_raw, sp["hy_conv_w"], sp["hy_conv_b"], name="hy_conv")
    xs, bm, cm = rowmap(f_silu_xbc, [xbc_pre], [], [F32] * 3, name="hy_conv_act", tr=256)
    dtraw_t = dtr[:, :HY_DT].T
    (y, prev_all), full = ssd2_fwd(xs, dtraw_t, sp["hy_dt_bias"], sp["hy_a_log"], sp["hy_d_skip"], bm, cm, side=w_side(1))
    if comm is not None:
        comm["full"] = full
    ysn = rowmap(f_gated_norm, [y, z], [sp["hy_ssm_norm_g"]], [BF16], name="hy_gnorm", tr=128)[0]
    att_in, att_o, att_l = [], [], []
    for p, (win, dil) in enumerate(ATT_PATTERNS):
        if dil == 1:
            qa, ka, va, cols = q, kv, kv, (p, 0, 1)
        else:
            qa, ka, cols = regroup(q[:, p * D:(p + 1) * D], dil), regroup(kv, dil), (0, 0, 1)
            va = ka
        bias = pair_bias(att_bias(sp["rel_table"], p, dil))
        nb = S // dil // ATT_BLK
        (o, lse), full = att2_fwd(qa, ka, va, bias, nb, cols, name=f"att_fwd{p}", side=w_side(2 + p))
        if comm is not None:
            comm["full"] = full
        att_in.append((qa, ka, va, bias, nb, cols))
        att_o.append(regroup(o, dil, inverse=True))
        att_l.append(regroup(lse, dil, inverse=True))
    if comm is not None:
        wts.update(unpack_weights(comm["full"], skip=("hy_in_t",)))
    att = rowmap(f_combine, att_o + att_l, [], [BF16], name="att_combine", tr=256)[0]
    cat = jnp.concatenate([ysn, att], axis=-1)
    mix0 = matmul(cat, wts["hy_out"], mode="nn", out_dtype=F32, name="hy_out")
    x1 = rowmap(f_resid, [x, mix0], [mods[0][2]], [F32], name="l0_res")[0]
    x2, ffn0 = ffn_fwd(x1, 0, mods[0][5], "ffn0")

    h1 = normmod(x2, sp["norm_mix_g"][1], mods[1][1], mods[1][0], "l1_norm")
    p1 = matmul(h1, wts["pw1_t"], mode="nt", out_dtype=BF16, name="cv_pw1")
    u = rowmap(f_glu, [p1], [sp["cv_b_pw1"]], [F32], name="cv_glu")[0]
    uc = conv_fwd(u, sp["cv_w_dw"], sp["cv_b_dw"], name="cv_conv")
    ul = rowmap(f_ln_silu, [uc], [sp["cv_ln_g"], sp["cv_ln_b"]], [BF16], name="cv_ln")[0]
    mix1 = matmul(ul, wts["pw2"], mode="nn", out_dtype=F32, name="cv_pw2")
    x3 = rowmap(f_resid_bias, [x2, mix1], [mods[1][2], sp["cv_b_pw2"]], [F32], name="l1_res")[0]
    x4, ffn1 = ffn_fwd(x3, 1, mods[1][5], "ffn1")

    ones = jnp.ones((S, 1), F32)
    (dres,), (dfinal,), (loss_rows,) = rowmap_bwd(f_head, [x4, tgt], [sp["final_norm_g"]], [ones], name="head",
                                                  row_grad=[True, False], emit=(0,))
    g["final_norm_g"] = dfinal

    dres = ffn_bwd(dres, x3, 1, ffn1, "ffn1")
    (dmix1,), (dg1, db2), _ = rowmap_bwd(f_resid_bias, [x2, mix1], [mods[1][2], sp["cv_b_pw2"]], [dres], name="l1_res_b",
                                         row_grad=[False, True], row_dtypes=[BF16])
    dmods[1][2] = dg1
    g["cv_b_pw2"] = db2
    dul = matmul(dmix1, wts["pw2"], mode="nt", out_dtype=F32, name="cv_pw2_dx")
    g["pw2"] = wgrad(dmix1, ul, "cv_pw2_dw")
    (duc,), (g["cv_ln_g"], g["cv_ln_b"]), _ = rowmap_bwd(f_ln_silu, [uc], [sp["cv_ln_g"], sp["cv_ln_b"]], [dul],
                                                         name="cv_ln_b", row_grad=[True])
    du, g["cv_w_dw"], g["cv_b_dw"] = conv_bwd(u, sp["cv_w_dw"], duc, name="cv_conv_b", cb=128, chunk_rows=128)
    (dp1,), (g["cv_b_pw1"],), _ = rowmap_bwd(f_glu, [p1], [sp["cv_b_pw1"]], [du], name="cv_glu_b", row_grad=[True],
                                             row_dtypes=[BF16])
    g["pw1_t"] = wgrad(h1, dp1, "cv_pw1_dw")
    dh1 = matmul(dp1, wts["pw1_t"], mode="nn", out_dtype=F32, name="cv_pw1_dx")
    (dres,), (dg_, dsc, dsh), _ = rowmap_bwd(f_normmod, [x2], [sp["norm_mix_g"][1], mods[1][1], mods[1][0]], [dh1],
                                             name="l1_norm_b", row_grad=[True], row_add=[dres])
    g["norm_mix_g1"] = dg_
    dmods[1][1], dmods[1][0] = dsc, dsh

    dres = ffn_bwd(dres, x1, 0, ffn0, "ffn0")
    (dmix0,), (dg1,), _ = rowmap_bwd(f_resid, [x, mix0], [mods[0][2]], [dres], name="l0_res_b",
                                     row_grad=[False, True], row_dtypes=[BF16])
    dmods[0][2] = dg1
    dysn = matmul(dmix0, wts["hy_out"], mode="nt", out_dtype=F32, name="hy_out_dy", n=SSM_INNER, b_off=0)
    datt = matmul(dmix0, wts["hy_out"], mode="nt", out_dtype=F32, name="hy_out_da", n=D, b_off=SSM_INNER)
    g["hy_out"] = wgrad(dmix0, cat, "hy_out_dw")
    (dy, dz), (g["hy_ssm_norm_g"],), _ = rowmap_bwd(f_gated_norm, [y, z], [sp["hy_ssm_norm_g"]], [dysn], name="hy_gnorm_b",
                                                    row_grad=[True, True], row_dtypes=[F32, BF16], tr=128)
    if comm is not None:
        comm["ga"] = pack_grads(g, GA_LAYOUT, GA_ROWS)
        comm["recv"] = lax.empty((3, GA_ROWS, D), BF16)
    (dxs, ddtraw_t, g["hy_dt_bias"], g["hy_a_log"], g["hy_d_skip"], dbm, dcm), recv = ssd2_bwd(
        xs, dtraw_t, sp["hy_dt_bias"], sp["hy_a_log"], sp["hy_d_skip"], bm, cm, prev_all, dy, side=g_side(0))
    if comm is not None:
        comm["recv"] = recv
    (dxbc_pre,), _, _ = rowmap_bwd(f_silu_xbc, [xbc_pre], [], [dxs, dbm, dcm], name="hy_conv_act_b", row_grad=[True],
                                   tr=128)
    dxbc_raw, g["hy_conv_w"], g["hy_conv_b"] = conv_bwd(xbc_raw, sp["hy_conv_w"], dxbc_pre, name="hy_conv_b", cb=128, chunk_rows=128, dx_dtype=BF16)
    dol, _, _ = rowmap_bwd(f_combine, att_o + att_l, [], [datt], name="att_combine_b", row_grad=[True] * 6,
                           row_dtypes=[BF16] * 3 + [F32] * 3, tr=128)
    dqs, dks, dvs, dtabs = [], [], [], []
    for p, (win, dil) in enumerate(ATT_PATTERNS):
        qa, ka, va, bias, nb, cols = att_in[p]
        (dq, dkp_, dvp_, dbias), recv = att2_bwd(qa, ka, va, bias, regroup(dol[p], dil), regroup(dol[3 + p], dil), nb,
                                                 cols, name=f"att_bwd{p}", side=g_side(1 + p))
        if comm is not None:
            comm["recv"] = recv
        dqs.append(regroup(dq, dil, inverse=True))
        dks.append(regroup(dkp_, dil, inverse=True))
        dvs.append(regroup(dvp_, dil, inverse=True))
        dtabs.append(att_bias_grad(dbias.reshape(ATT_HEADS, 2, ATT_BLK, ATT_BLK), dil, name=f"att_dtab{p}"))
    g["rel_table"] = jnp.concatenate(dtabs, axis=1)
    dk = rowmap(f_sum3, dks, [], [BF16], name="att_dk_sum")[0]
    dv = rowmap(f_sum3, dvs, [], [BF16], name="att_dv_sum")[0]
    ddt = jnp.pad(ddtraw_t.T, ((0, 0), (0, DT_PAD - HY_DT)))
    dproj = jnp.concatenate([dz, dxbc_raw] + dqs + [dk, dv, ddt.astype(BF16)], axis=-1)
    g["hy_in_t"] = wgrad(h0, dproj, "hy_in_dw")
    if comm is None:
        dh0 = matmul(dproj, w_in, mode="nn", out_dtype=F32, name="hy_in_dx")
    else:
        gb = pack_grads(g, GB_LAYOUT, GB_ROWS)
        half = GB_ROWS // 2
        theirs = swap_halves(gb, name="swap_in_halves")
        ours = lax.dynamic_slice_in_dim(gb, lax.axis_index("c") * half, half, axis=1)
        comm["gb"] = rowmap(f_add, [ours.reshape(N_CHIPS * half, D), theirs.reshape(N_CHIPS * half, D)], [], [BF16],
                            name="sum_in_cores")[0].reshape(N_CHIPS, half, D)
        dh0, comm["recv_b"] = matmul(dproj, w_in, mode="nn", out_dtype=F32, name="hy_in_dx",
                                     side=ScatterRows(comm["gb"], lax.empty((3, half, D), BF16), 0, half))
    (dres,), (dg_, dsc, dsh), _ = rowmap_bwd(f_normmod, [x], [sp["norm_mix_g"][0], mods[0][1], mods[0][0]], [dh0],
                                             name="l0_norm_b", row_grad=[True], row_add=[dres])
    g["norm_mix_g0"] = dg_
    dmods[0][1], dmods[0][0] = dsc, dsh
    return loss_rows, dres, g, dmods


ANY = pl.BlockSpec(memory_space=pl.ANY)
WHOLE_VMEM = pl.BlockSpec(memory_space=pltpu.VMEM)


def _place():
    return lax.axis_index("x"), lax.axis_index("y"), lax.axis_index("c")


def _other_chips(x, y):
    return [(1 - x, y), (x, 1 - y), (1 - x, 1 - y)]


def allgather_small(v, *, name, side=None):
    m_per = v.shape[0]

    def gather(x_ref, out_ref, send_sems, recv_sems, local_sem):
        x, y, c = _place()
        me, sibling = (x, y, c), (x, y, 1 - c)
        chips = _other_chips(x, y)

        def rows(px, py, pc):
            return out_ref.at[pl.ds((4 * px + 2 * py + pc) * m_per, m_per), :]

        def copy(k, block, to, src=None):
            return pltpu.make_async_remote_copy(
                src_ref=rows(*block) if src is None else src, dst_ref=rows(*block),
                send_sem=send_sems.at[k], recv_sem=recv_sems.at[k], device_id=to, device_id_type=MESH)

        mine = pltpu.make_async_copy(x_ref, rows(*me), local_sem)
        mine.start()
        first = [copy(0, me, sibling, src=x_ref)]
        first += [copy(1 + j, me, (*chip, c), src=x_ref) for j, chip in enumerate(chips)]
        for cp in first:
            cp.start()
        passed = [copy(4 + j, (*chip, c), sibling) for j, chip in enumerate(chips)]
        for j, chip in enumerate(chips):
            copy(1 + j, (*chip, c), me).wait_recv()
            passed[j].start()
        copy(0, sibling, me).wait_recv()
        for j, chip in enumerate(chips):
            copy(4 + j, (*chip, 1 - c), me).wait_recv()
        for cp in first + passed:
            cp.wait_send()
        mine.wait()

    out = jax.ShapeDtypeStruct((N_DEV * m_per, LANES), v.dtype)
    sems = [pltpu.SemaphoreType.DMA((7,)), pltpu.SemaphoreType.DMA((7,)), pltpu.SemaphoreType.DMA]
    if side is None:
        return pl.pallas_call(gather, name=name, out_shape=out, in_specs=[WHOLE_VMEM], out_specs=WHOLE_VMEM,
                              scratch_shapes=sems)(v)

    def body(x_ref, src_ref, dst_in_ref, out_ref, dst_ref, send_sems, recv_sems, local_sem, *side_sems):
        side.start(src_ref, dst_ref, side_sems)
        gather(x_ref, out_ref, send_sems, recv_sems, local_sem)
        side.finish(src_ref, dst_ref, side_sems)

    return pl.pallas_call(
        body, name=name, out_shape=[out, jax.ShapeDtypeStruct(side.dst.shape, side.dst.dtype)],
        in_specs=[WHOLE_VMEM, ANY, ANY], out_specs=[WHOLE_VMEM, ANY], scratch_shapes=sems + side.sems(),
        input_output_aliases={2: 1},
    )(v, side.src, side.dst)


def swap_halves(gpack, *, name):
    half_rows = gpack.shape[1] // 2

    def body(g_ref, r_ref, send_sems, recv_sems):
        x, y, c = _place()
        its_half = pl.ds((1 - c) * half_rows, half_rows)
        copies = [pltpu.make_async_remote_copy(
            src_ref=g_ref.at[s, its_half], dst_ref=r_ref.at[s], send_sem=send_sems.at[s], recv_sem=recv_sems.at[s],
            device_id=(x, y, 1 - c), device_id_type=MESH) for s in range(N_CHIPS)]
        for cp in copies:
            cp.start()
        for cp in copies:
            cp.wait()

    return pl.pallas_call(
        body, name=name,
        out_shape=jax.ShapeDtypeStruct((N_CHIPS, half_rows) + gpack.shape[2:], gpack.dtype),
        in_specs=[ANY], out_specs=ANY,
        scratch_shapes=[pltpu.SemaphoreType.DMA((N_CHIPS,)), pltpu.SemaphoreType.DMA((N_CHIPS,))],
    )(gpack)


class GatherRows:
    def __init__(self, pack, full, lo, hi):
        assert (hi - lo) % 32 == 0 and lo % 16 == 0
        self.src, self.dst, self.lo, self.hi = pack, full, lo, hi

    def sems(self):
        return [pltpu.SemaphoreType.DMA((6,)), pltpu.SemaphoreType.DMA((6,)), pltpu.SemaphoreType.DMA]

    def _parts(self, pack_ref, full_ref, sems):
        send_sems, recv_sems, local_sem = sems
        x, y, c = _place()
        half = (self.hi - self.lo) // 2
        mine, its = pl.ds(self.lo + c * half, half), pl.ds(self.lo + (1 - c) * half, half)
        rows = pl.ds(self.lo, self.hi - self.lo)
        local = pltpu.make_async_copy(pack_ref.at[rows], full_ref.at[2 * x + y, rows], local_sem)
        chips = _other_chips(x, y)

        def remote(src, dst, k, to):
            return pltpu.make_async_remote_copy(src_ref=src, dst_ref=dst, send_sem=send_sems.at[k],
                                                recv_sem=recv_sems.at[k], device_id=to, device_id_type=MESH)

        sends = [remote(pack_ref.at[mine], full_ref.at[2 * x + y, mine], k, (cx, cy, c)) for k, (cx, cy) in enumerate(chips)]
        landed = [full_ref.at[2 * cx + cy, mine] for cx, cy in chips]
        arrive = [remote(pack_ref.at[mine], landed[k], k, (cx, cy, c)) for k, (cx, cy) in enumerate(chips)]
        passed = [remote(landed[k], landed[k], 3 + k, (x, y, 1 - c)) for k in range(3)]
        from_sibling = [remote(landed[k], full_ref.at[2 * cx + cy, its], 3 + k, (x, y, 1 - c))
                        for k, (cx, cy) in enumerate(chips)]
        return local, sends, arrive, passed, from_sibling

    def start(self, pack_ref, full_ref, sems):
        local, sends, _, _, _ = self._parts(pack_ref, full_ref, sems)
        local.start()
        for cp in sends:
            cp.start()

    def finish(self, pack_ref, full_ref, sems):
        local, sends, arrive, passed, from_sibling = self._parts(pack_ref, full_ref, sems)
        for k in range(3):
            arrive[k].wait_recv()
            passed[k].start()
        for cp in from_sibling:
            cp.wait_recv()
        for cp in sends + passed:
            cp.wait_send()
        local.wait()


class ScatterRows:
    def __init__(self, gpack, recv, lo, hi):
        assert lo % 16 == 0 and hi % 16 == 0
        self.src, self.dst, self.lo, self.hi = gpack, recv, lo, hi

    def sems(self):
        return [pltpu.SemaphoreType.DMA((3,)), pltpu.SemaphoreType.DMA((3,))]

    def _parts(self, g_ref, recv_ref, sems):
        send_sems, recv_sems = sems
        x, y, c = _place()
        rows = pl.ds(self.lo, self.hi - self.lo)
        return [pltpu.make_async_remote_copy(
            src_ref=g_ref.at[2 * cx + cy, rows], dst_ref=recv_ref.at[k, rows], send_sem=send_sems.at[k],
            recv_sem=recv_sems.at[k], device_id=(cx, cy, c), device_id_type=MESH)
            for k, (cx, cy) in enumerate(_other_chips(x, y))]

    def start(self, g_ref, recv_ref, sems):
        for cp in self._parts(g_ref, recv_ref, sems):
            cp.start()

    def finish(self, g_ref, recv_ref, sems):
        sends = self._parts(g_ref, recv_ref, sems)
        for cp in sends:
            cp.wait_recv()
        for cp in sends:
            cp.wait_send()


def side_call(side, *, name):
    def body(src_ref, dst_in_ref, dst_ref, *sems):
        side.start(src_ref, dst_ref, sems)
        side.finish(src_ref, dst_ref, sems)

    return pl.pallas_call(
        body, name=name, out_shape=jax.ShapeDtypeStruct(side.dst.shape, side.dst.dtype),
        in_specs=[ANY, ANY], out_specs=ANY, scratch_shapes=side.sems(), input_output_aliases={1: 0},
    )(side.src, side.dst)


def grid_call(body, args, *, name, out_shape, grid, in_specs, out_specs, scratch_shapes, semantics, side=None):
    if side is None:
        res = pl.pallas_call(body, name=name, out_shape=out_shape, grid=grid, in_specs=in_specs, out_specs=out_specs,
                             scratch_shapes=scratch_shapes, compiler_params=_cparams(*semantics))(*args)
        return res, None
    n_in, n_out, n_scr = len(args), len(out_shape), len(scratch_shapes)

    def wrapped(*refs):
        ins, (src_ref, _) = refs[:n_in], refs[n_in:n_in + 2]
        outs, dst_ref = refs[n_in + 2:n_in + 2 + n_out], refs[n_in + 2 + n_out]
        scr, sems = refs[n_in + 3 + n_out:n_in + 3 + n_out + n_scr], refs[n_in + 3 + n_out + n_scr:]
        first = functools.reduce(jnp.logical_and, [pl.program_id(i) == 0 for i in range(len(grid))])
        last = functools.reduce(jnp.logical_and, [pl.program_id(i) == n - 1 for i, n in enumerate(grid)])

        @pl.when(first)
        def _():
            side.start(src_ref, dst_ref, sems)

        body(*ins, *outs, *scr)

        @pl.when(last)
        def _():
            side.finish(src_ref, dst_ref, sems)

    res = pl.pallas_call(
        wrapped, name=name,
        out_shape=list(out_shape) + [jax.ShapeDtypeStruct(side.dst.shape, side.dst.dtype)],
        grid=grid, in_specs=list(in_specs) + [ANY, ANY], out_specs=list(out_specs) + [ANY],
        scratch_shapes=list(scratch_shapes) + side.sems(), input_output_aliases={n_in + 1: n_out},
        compiler_params=_cparams(*(["arbitrary"] * len(grid))),
    )(*args, side.src, side.dst)
    return res[:-1], res[-1]


def sibling_swap(p, *, name):
    def body(p_ref, r_ref, send_sem, recv_sem):
        x, y, c = _place()
        cp = pltpu.make_async_remote_copy(src_ref=p_ref, dst_ref=r_ref, send_sem=send_sem, recv_sem=recv_sem,
                                          device_id=(x, y, 1 - c), device_id_type=MESH)
        cp.start()
        cp.wait()

    return pl.pallas_call(
        body, name=name, out_shape=jax.ShapeDtypeStruct(p.shape, p.dtype),
        in_specs=[ANY], out_specs=ANY,
        scratch_shapes=[pltpu.SemaphoreType.DMA, pltpu.SemaphoreType.DMA],
    )(p)


def sum_slots(own, recv, *, name):
    r_dim, c_dim = own.shape
    tr = _pick(r_dim, 256, mult=16)

    def body(o_ref, r_ref, out_ref):
        acc = o_ref[...].astype(F32)
        for k in range(3):
            acc = acc + r_ref[k].astype(F32)
        out_ref[...] = acc

    return pl.pallas_call(
        body, name=name, out_shape=jax.ShapeDtypeStruct((r_dim, c_dim), F32), grid=(r_dim // tr,),
        in_specs=[pl.BlockSpec((tr, c_dim), lambda i: (i, 0)), pl.BlockSpec((3, tr, c_dim), lambda i: (0, i, 0))],
        out_specs=pl.BlockSpec((tr, c_dim), lambda i: (i, 0)),
        compiler_params=_cparams("parallel"),
    )(own, recv)


def sum_devices(v_all, *, name):
    m_per = v_all.shape[0] // N_DEV

    def body(v_ref, o_ref):
        acc = v_ref[pl.ds(0, m_per), :]
        for d in range(1, N_DEV):
            acc = acc + v_ref[pl.ds(d * m_per, m_per), :]
        o_ref[...] = acc

    return pl.pallas_call(
        body, name=name, out_shape=jax.ShapeDtypeStruct((m_per, LANES), F32),
        in_specs=[WHOLE_VMEM], out_specs=WHOLE_VMEM,
    )(v_all)


WEIGHTS = ['ada_w', 'ada_b', 'norm_mix_g', 'norm_ffn_g', 'hy_w_in', 'hy_conv_w', 'hy_conv_b', 'hy_dt_bias', 'hy_a_log',
           'hy_d_skip', 'hy_ssm_norm_g', 'hy_w_out', 'rel_table', 'cv_w_pw1', 'cv_b_pw1', 'cv_w_dw', 'cv_b_dw', 'cv_ln_g',
           'cv_ln_b', 'cv_w_pw2', 'cv_b_pw2', 'ffn_w_gate', 'ffn_w_up', 'ffn_w_down', 'final_norm_g']
BIG = ('ada_w', 'hy_w_in', 'hy_w_out', 'cv_w_pw1', 'cv_w_pw2', 'ffn_w_gate', 'ffn_w_up', 'ffn_w_down')
SMALL_SHARDED = {'hy_conv_w': (1, 4, 3072), 'cv_b_pw1': (1, 2048), 'cv_w_dw': (1, 31, 1024), 'cv_b_dw': (1, 1024),
                 'cv_ln_g': (1, 1024), 'cv_ln_b': (1, 1024), 'cv_b_pw2': (1, 1024)}
SMALL_GRADS = {'ada_b': (2, 6144), 'norm_mix_g': (2, 1024), 'norm_ffn_g': (2, 1024), 'hy_conv_w': (1, 4, 3072),
               'hy_conv_b': (1, 3072), 'hy_dt_bias': (1, 32), 'hy_a_log': (1, 32), 'hy_d_skip': (1, 32),
               'hy_ssm_norm_g': (1, 2048), 'rel_table': (32, 48), 'cv_b_pw1': (1, 2048), 'cv_w_dw': (1, 31, 1024),
               'cv_b_dw': (1, 1024), 'cv_ln_g': (1, 1024), 'cv_ln_b': (1, 1024), 'cv_b_pw2': (1, 1024),
               'final_norm_g': (1024,), 'loss': (1,)}

PACK_LAYOUT = (('hy_in_t', 2568), ('hy_out', 768), ('pw1_t', 512), ('pw2', 256),
               ('gate_t0', 704), ('up_t0', 704), ('down0', 704), ('gate_t1', 704), ('up_t1', 704), ('down1', 704))
PACK_ROWS = 8448


def _pack_offsets(layout):
    off, out = 0, {}
    for nm, r in layout:
        out[nm] = (off, r)
        off += r
    return out


PACK_OFF = _pack_offsets(PACK_LAYOUT)
W_BATCHES = ((0, 2624), (2624, 5248), (5248, 6336), (6336, 7424), (7424, 8448))
GA_LAYOUT = PACK_LAYOUT[1:]
GA_ROWS = 5888
GA_OFF = _pack_offsets(GA_LAYOUT)
G_BATCHES = ((0, 2560), (2560, 3712), (3712, 4864), (4864, 5888))
GB_LAYOUT = PACK_LAYOUT[:1]
GB_ROWS = 2816


def pack_grads(g, layout, n_rows):
    def rows_bf16(nm):
        return g[nm]

    parts = []
    for key, r in layout:
        if key == 'hy_in_t':
            a = hy_from_cat(rows_bf16('hy_in_t'))
        elif key.startswith('gate_t'):
            a = rows_bf16('gu_t' + key[-1])[:FFN_HIDDEN]
        elif key.startswith('up_t'):
            a = rows_bf16('gu_t' + key[-1])[FFN_HIDDEN:]
        else:
            a = rows_bf16(key)
        parts.append(a.reshape(N_CHIPS, r, D))
    used = sum(r for _, r in layout)
    return jnp.concatenate(parts + [jnp.zeros((N_CHIPS, n_rows - used, D), BF16)], axis=1)


def unpack_weights(full, skip=()):
    def whole(nm):
        o, r = PACK_OFF[nm]
        return full[:, o:o + r].reshape(N_CHIPS * r, D)

    out = {"hy_out": whole('hy_out'), "pw1_t": whole('pw1_t'), "pw2": whole('pw2'),
           "gu_t": [jnp.concatenate([whole(f'gate_t{i}'), whole(f'up_t{i}')], axis=0) for i in range(2)],
           "down": [whole(f'down{i}') for i in range(2)]}
    if "hy_in_t" not in skip:
        out["hy_in_t"] = hy_to_cat(whole('hy_in_t'))
    return out


def _to_lanes(flat):
    n = flat.shape[0]
    m = -(-n // (8 * LANES)) * 8
    return jnp.pad(flat, (0, m * LANES - n)).reshape(m, LANES)


def _split(flat, shapes):
    out, off = {}, 0
    for nm, shp in shapes.items():
        n = int(np.prod(shp))
        out[nm] = flat[off:off + n].reshape(shp)
        off += n
    return out


def kernel(x, c, ada_w, ada_b, norm_mix_g, norm_ffn_g, hy_w_in, hy_conv_w, hy_conv_b, hy_dt_bias, hy_a_log, hy_d_skip, hy_ssm_norm_g, hy_w_out, rel_table, cv_w_pw1, cv_b_pw1, cv_w_dw, cv_b_dw, cv_ln_g, cv_ln_b, cv_w_pw2, cv_b_pw2, ffn_w_gate, ffn_w_up, ffn_w_down, final_norm_g, loss_target, m_ada_w, m_ada_b, m_norm_mix_g, m_norm_ffn_g, m_hy_w_in, m_hy_conv_w, m_hy_conv_b, m_hy_dt_bias, m_hy_a_log, m_hy_d_skip, m_hy_ssm_norm_g, m_hy_w_out, m_rel_table, m_cv_w_pw1, m_cv_b_pw1, m_cv_w_dw, m_cv_b_dw, m_cv_ln_g, m_cv_ln_b, m_cv_w_pw2, m_cv_b_pw2, m_ffn_w_gate, m_ffn_w_up, m_ffn_w_down, m_final_norm_g, v_ada_w, v_ada_b, v_norm_mix_g, v_norm_ffn_g, v_hy_w_in, v_hy_conv_w, v_hy_conv_b, v_hy_dt_bias, v_hy_a_log, v_hy_d_skip, v_hy_ssm_norm_g, v_hy_w_out, v_rel_table, v_cv_w_pw1, v_cv_b_pw1, v_cv_w_dw, v_cv_b_dw, v_cv_ln_g, v_cv_ln_b, v_cv_w_pw2, v_cv_b_pw2, v_ffn_w_gate, v_ffn_w_up, v_ffn_w_down, v_final_norm_g):
    args = (x, c, ada_w, ada_b, norm_mix_g, norm_ffn_g, hy_w_in, hy_conv_w, hy_conv_b, hy_dt_bias, hy_a_log, hy_d_skip, hy_ssm_norm_g, hy_w_out, rel_table, cv_w_pw1, cv_b_pw1, cv_w_dw, cv_b_dw, cv_ln_g, cv_ln_b, cv_w_pw2, cv_b_pw2, ffn_w_gate, ffn_w_up, ffn_w_down, final_norm_g, loss_target, m_ada_w, m_ada_b, m_norm_mix_g, m_norm_ffn_g, m_hy_w_in, m_hy_conv_w, m_hy_conv_b, m_hy_dt_bias, m_hy_a_log, m_hy_d_skip, m_hy_ssm_norm_g, m_hy_w_out, m_rel_table, m_cv_w_pw1, m_cv_b_pw1, m_cv_w_dw, m_cv_b_dw, m_cv_ln_g, m_cv_ln_b, m_cv_w_pw2, m_cv_b_pw2, m_ffn_w_gate, m_ffn_w_up, m_ffn_w_down, m_final_norm_g, v_ada_w, v_ada_b, v_norm_mix_g, v_norm_ffn_g, v_hy_w_in, v_hy_conv_w, v_hy_conv_b, v_hy_dt_bias, v_hy_a_log, v_hy_d_skip, v_hy_ssm_norm_g, v_hy_w_out, v_rel_table, v_cv_w_pw1, v_cv_b_pw1, v_cv_w_dw, v_cv_b_dw, v_cv_ln_g, v_cv_ln_b, v_cv_w_pw2, v_cv_b_pw2, v_ffn_w_gate, v_ffn_w_up, v_ffn_w_down, v_final_norm_g)
    x_in, c_in = args[0], args[1]
    w = dict(zip(WEIGHTS, args[2:27], strict=True))
    tgt = args[27]
    m_in = dict(zip(WEIGHTS, args[28:53], strict=True))
    v_in = dict(zip(WEIGHTS, args[53:78], strict=True))
    xi, yi, ci = _place()
    chip = 2 * xi + yi
    dev = 2 * chip + ci

    cs = rowmap(f_silu, [c_in.reshape(8, LANES)], [], [F32], name="cond_silu", tr=8)[0]
    cs_all = allgather_small(cs, name="gather_cond").reshape(N_DEV, D)
    cs16 = jnp.pad(cs_all, ((0, 8), (0, 0)))
    modpart = jnp.stack([matmul(cs16, w['ada_w'][i], mode="nn", out_dtype=F32, name=f"ada_fwd{i}")[:N_DEV]
                         for i in range(2)], axis=1)
    def rows_of(nm, i=None):
        a = w[nm][0 if i is None else i]
        return (a.T if nm in ('hy_w_in', 'cv_w_pw1', 'ffn_w_gate', 'ffn_w_up') else a).astype(BF16)

    pieces = [rows_of('hy_w_in'), rows_of('hy_w_out'), rows_of('cv_w_pw1'), rows_of('cv_w_pw2')]
    for i in range(2):
        pieces += [rows_of('ffn_w_gate', i), rows_of('ffn_w_up', i), rows_of('ffn_w_down', i)]
    n_rows = sum(p.shape[0] for p in pieces)
    pack = jnp.concatenate(pieces + [jnp.zeros((PACK_ROWS - n_rows, D), BF16)], axis=0)

    shard_names = list(SMALL_SHARDED)
    payload = jnp.concatenate([modpart.reshape(-1)] + [w[nm].reshape(-1) for nm in shard_names])
    got, full = allgather_small(_to_lanes(payload), name="gather_mod",
                                side=GatherRows(pack, lax.empty((N_CHIPS, PACK_ROWS, D), BF16), *W_BATCHES[0]))
    got = got.reshape(N_DEV, -1)[0::2]
    modparts = got[:, :modpart.size].reshape(N_CHIPS, N_DEV, 2, 1536)
    mine = lax.dynamic_index_in_dim(modparts, dev, axis=1, keepdims=False)
    mod = jnp.transpose(mine, (1, 0, 2)).reshape(2, 6 * D) + w['ada_b']
    mods = [[mod[i, j * D:(j + 1) * D].reshape(1, D) for j in range(6)] for i in range(2)]
    sp = {}
    off = modpart.size
    for nm in shard_names:
        shp = w[nm].shape
        n = int(np.prod(shp))
        parts = got[:, off:off + n].reshape((N_CHIPS,) + shp)
        sp[nm] = jnp.concatenate([parts[s] for s in range(N_CHIPS)], axis=-1)
        off += n

    o_in, r_in = PACK_OFF['hy_in_t']
    wts = {"hy_in_t": hy_to_cat(full[:, o_in:o_in + r_in].reshape(N_CHIPS * r_in, D))}
    comm = {"pack": pack, "full": full}

    sp = {"norm_mix_g": [w['norm_mix_g'][i].reshape(1, D) for i in range(2)],
          "norm_ffn_g": [w['norm_ffn_g'][i].reshape(1, D) for i in range(2)],
          "hy_conv_w": sp['hy_conv_w'][0], "hy_conv_b": w['hy_conv_b'],
          "hy_dt_bias": w['hy_dt_bias'].reshape(SSM_HEADS, 1), "hy_a_log": w['hy_a_log'].reshape(SSM_HEADS, 1),
          "hy_d_skip": w['hy_d_skip'].reshape(SSM_HEADS, 1), "hy_ssm_norm_g": w['hy_ssm_norm_g'],
          "rel_table": w['rel_table'], "cv_b_pw1": sp['cv_b_pw1'], "cv_w_dw": sp['cv_w_dw'][0], "cv_b_dw": sp['cv_b_dw'],
          "cv_ln_g": sp['cv_ln_g'], "cv_ln_b": sp['cv_ln_b'], "cv_b_pw2": sp['cv_b_pw2'],
          "final_norm_g": w['final_norm_g'].reshape(1, D)}

    loss_rows, grad_x, g, dmods = device_step(x_in[0], tgt[0], mods, wts, sp, comm)

    dmod = jnp.stack([jnp.concatenate([d.reshape(-1) for d in dmods[i]]) for i in range(2)])
    small = {'ada_b': dmod, 'norm_mix_g': jnp.stack([g[f'norm_mix_g{i}'].reshape(-1) for i in range(2)]),
             'norm_ffn_g': jnp.stack([g[f'norm_ffn_g{i}'].reshape(-1) for i in range(2)]),
             'loss': jnp.sum(loss_rows).reshape(1)}
    for nm in SMALL_GRADS:
        if nm not in small:
            small[nm] = g[nm]
    vec = _to_lanes(jnp.concatenate([small[nm].reshape(-1) for nm in SMALL_GRADS]))
    vec_all = allgather_small(vec, name="gather_small_grads")
    tot = _split(sum_devices(vec_all, name="sum_small_grads").reshape(-1), SMALL_GRADS)
    dmod_all = vec_all.reshape(N_DEV, -1)[:, :2 * 6 * D].reshape(N_DEV, 2, 6 * D)

    recv = comm["recv"]
    own_a = lax.dynamic_index_in_dim(comm["ga"], chip, axis=0, keepdims=False)
    part_a = sum_slots(own_a, recv, name="sum_chip_grads")
    red_a = rowmap(f_add, [part_a, sibling_swap(part_a, name="swap_grads")], [], [F32], name="sum_core_grads")[0]
    recv_b = comm["recv_b"]
    own_b = lax.dynamic_index_in_dim(comm["gb"], chip, axis=0, keepdims=False)
    mine_half = sum_slots(own_b, recv_b, name="sum_in_chips")
    its_half = sibling_swap(mine_half, name="swap_in")
    red_b = jnp.concatenate([jnp.where(ci == 0, mine_half, its_half), jnp.where(ci == 0, its_half, mine_half)], axis=0)

    def shard_grad(nm, i=None):
        key = {'hy_w_in': 'hy_in_t', 'hy_w_out': 'hy_out', 'cv_w_pw1': 'pw1_t', 'cv_w_pw2': 'pw2'}.get(nm)
        if key is None:
            key = {'ffn_w_gate': 'gate_t', 'ffn_w_up': 'up_t', 'ffn_w_down': 'down'}[nm] + str(i)
        if key == 'hy_in_t':
            a = red_b[:PACK_OFF[key][1]]
        else:
            o, r = GA_OFF[key]
            a = red_a[o:o + r]
        return a.T if key.endswith('_t') or key[:-1].endswith('_t') else a

    grads = {}
    grads['hy_w_in'] = shard_grad('hy_w_in')[None]
    grads['hy_w_out'] = shard_grad('hy_w_out')[None]
    grads['cv_w_pw1'] = shard_grad('cv_w_pw1')[None]
    grads['cv_w_pw2'] = shard_grad('cv_w_pw2')[None]
    for nm in ('ffn_w_gate', 'ffn_w_up', 'ffn_w_down'):
        grads[nm] = jnp.stack([shard_grad(nm, i) for i in range(2)])
    cs16 = jnp.pad(cs_all, ((0, 8), (0, 0)))
    dm_mine = lax.dynamic_slice_in_dim(dmod_all, chip * 1536, 1536, axis=2)
    dm16 = jnp.pad(dm_mine, ((0, 8), (0, 0), (0, 0)))
    grads['ada_w'] = jnp.stack([matmul(cs16, dm16[:, i], mode="tn", out_dtype=F32, name=f"ada_dw{i}") for i in range(2)])
    for nm, shp in SMALL_GRADS.items():
        if nm == 'loss':
            continue
        if nm in SMALL_SHARDED:
            n = w[nm].shape[-1]
            grads[nm] = lax.dynamic_slice_in_dim(tot[nm], chip * n, n, axis=len(shp) - 1)
        else:
            grads[nm] = tot[nm].reshape(w[nm].shape)

    delta, new_m, new_v = {}, {}, {}
    for nm in BIG:
        delta[nm], new_m[nm], new_v[nm] = adamw(w[nm], grads[nm], m_in[nm], v_in[nm], name="adamw_" + nm)
    smalls = [nm for nm in WEIGHTS if nm not in BIG]
    packed = [_to_lanes(jnp.concatenate([d[nm].reshape(-1) for nm in smalls])) for d in (w, grads, m_in, v_in)]
    res = rowmap(f_adamw, packed, [], [F32] * 3, name="adamw_small", tr=_rows_tile(packed[0].shape[0]))
    for d, r in zip((delta, new_m, new_v), res, strict=True):
        d.update(_split(r.reshape(-1), {nm: w[nm].shape for nm in smalls}))

    loss = tot['loss'].reshape(())
    return (loss, grad_x[None], *[grads[nm] for nm in WEIGHTS], *[delta[nm] for nm in WEIGHTS],
            *[new_m[nm] for nm in WEIGHTS], *[new_v[nm] for nm in WEIGHTS])
```

```python
import functools
import math

import jax
import jax.numpy as jnp
import numpy as np
from jax import lax
from jax.experimental import pallas as pl
from jax.experimental.pallas import tpu as pltpu

F32 = jnp.float32
BF16 = jnp.bfloat16
MESH = pl.DeviceIdType.MESH

D = 1024
S = 4096
EPS = 1e-6
SSM_INNER = 2048
SSM_HEADS = 32
SSM_HDIM = 64
SSM_GROUPS = 4
SSM_STATE = 128
SSM_CONVK = 4
SSM_CONV_DIM = 3072
CHUNK = 128
N_CHUNKS = S // CHUNK
ATT_HEADS = 16
ATT_HDIM = 64
ATT_PATTERNS = ((128, 1), (512, 4), (2048, 16))
ATT_BLK = 128
REL_BUCKETS = 32
REL_MAX_DIST = 2048
CONV_WIDTH = 31
FFN_HIDDEN = 2816
N_CHIPS = 4
N_DEV = 8
ADAM_LR, ADAM_B1, ADAM_B2, ADAM_EPS, ADAM_WD, ADAM_STEP = 0.001, 0.9, 0.999, 1e-08, 0.01, 10

VMEM_LIMIT_BYTES = 56 * 1024 * 1024
LANES = 128


def _cparams(*sem):
    return pltpu.CompilerParams(dimension_semantics=sem, vmem_limit_bytes=VMEM_LIMIT_BYTES)


def _pick(n, cap, mult=LANES):
    best = None
    for t in range(mult, min(n, cap) + 1, mult):
        if n % t == 0:
            best = t
    return best or n


def _dot(a, b, ca, cb):
    return lax.dot_general(a.astype(BF16), b.astype(BF16), (((ca,), (cb,)), ((), ())), preferred_element_type=F32)


@jax.custom_vjp
def mm_nt(a, b):
    return _dot(a, b, 1, 1)


def _mm_nt_fwd(a, b):
    return _dot(a, b, 1, 1), (a, b)


def _mm_nt_bwd(res, g):
    a, b = res
    return _dot(g, b, 1, 0).astype(a.dtype), _dot(g, a, 0, 0).astype(b.dtype)


mm_nt.defvjp(_mm_nt_fwd, _mm_nt_bwd)


def matmul(a, b, *, mode, out_dtype, name, n=None, b_off=0, tm_cap=1024, tn_cap=512, tk_cap=3584, side=None,
           out_t=False):
    if mode == "tn":
        k_dim, m_dim = a.shape
    else:
        m_dim, k_dim = a.shape
    n_dim = n if n is not None else (b.shape[0] if mode == "nt" else b.shape[1])
    tm = m_dim if m_dim < LANES else _pick(m_dim, tm_cap)
    tn = _pick(n_dim, tn_cap)
    tk = k_dim if k_dim < LANES else _pick(k_dim, tk_cap)
    assert m_dim % tm == 0 and n_dim % tn == 0 and k_dim % tk == 0 and b_off % tn == 0
    nk = k_dim // tk
    off = b_off // tn
    if mode == "nn":
        a_spec = pl.BlockSpec((tm, tk), lambda i, j, k: (i, k))
        b_spec = pl.BlockSpec((tk, tn), lambda i, j, k: (k, j))
        ca, cb = 1, 0
    elif mode == "nt":
        a_spec = pl.BlockSpec((tm, tk), lambda i, j, k: (i, k))
        b_spec = pl.BlockSpec((tn, tk), lambda i, j, k: (j + off, k))
        ca, cb = 1, 1
    else:
        a_spec = pl.BlockSpec((tk, tm), lambda i, j, k: (k, i))
        b_spec = pl.BlockSpec((tk, tn), lambda i, j, k: (k, j))
        ca, cb = 0, 0

    def emit(o_ref, val):
        o_ref[...] = (val.T if out_t else val).astype(o_ref.dtype)

    def body(a_ref, b_ref, o_ref, acc_ref):
        part = _dot(a_ref[...], b_ref[...], ca, cb)
        if nk == 1:
            emit(o_ref, part)
        else:
            k = pl.program_id(2)

            @pl.when(k == 0)
            def _():
                acc_ref[...] = part

            @pl.when(k > 0)
            def _():
                acc_ref[...] += part

            @pl.when(k == nk - 1)
            def _():
                emit(o_ref, acc_ref[...])

    if out_t:
        out_shape, out_spec = (n_dim, m_dim), pl.BlockSpec((tn, tm), lambda i, j, k: (j, i))
    else:
        out_shape, out_spec = (m_dim, n_dim), pl.BlockSpec((tm, tn), lambda i, j, k: (i, j))
    (out,), side_dst = grid_call(
        body, (a, b), name=name,
        out_shape=[jax.ShapeDtypeStruct(out_shape, out_dtype)],
        grid=(m_dim // tm, n_dim // tn, nk),
        in_specs=[a_spec, b_spec],
        out_specs=[out_spec],
        scratch_shapes=[pltpu.VMEM((tm, tn), F32)],
        semantics=("parallel", "parallel", "arbitrary"), side=side)
    return out if side is None else (out, side_dst)


def _f32(xs):
    return [x.astype(F32) for x in xs]


def rowmap(f, rows, consts, out_dtypes, *, name, tr=256):
    r_dim = rows[0].shape[0]
    tr = _pick(r_dim, tr, mult=8)
    assert r_dim % tr == 0
    nr, nc = len(rows), len(consts)
    outs = jax.eval_shape(lambda *xs: f(*xs), *[jax.ShapeDtypeStruct((tr, x.shape[1]), F32) for x in rows],
                          *[jax.ShapeDtypeStruct(x.shape, F32) for x in consts])

    def body(*refs):
        res = f(*_f32([r[...] for r in refs[:nr + nc]]))
        for o_ref, o in zip(refs[nr + nc:], res, strict=True):
            o_ref[...] = o.astype(o_ref.dtype)

    return pl.pallas_call(
        body, name=name,
        out_shape=[jax.ShapeDtypeStruct((r_dim, o.shape[1]), dt) for o, dt in zip(outs, out_dtypes, strict=True)],
        grid=(r_dim // tr,),
        in_specs=[pl.BlockSpec((tr, x.shape[1]), lambda i: (i, 0)) for x in rows]
        + [pl.BlockSpec(x.shape, lambda i: (0, 0)) for x in consts],
        out_specs=[pl.BlockSpec((tr, o.shape[1]), lambda i: (i, 0)) for o in outs],
        compiler_params=_cparams("parallel"),
    )(*rows, *consts)


def rowmap_bwd(f, rows, consts, cts, *, name, row_grad, row_dtypes=None, tr=256, emit=(), row_add=None,
               emit_dtype=F32):
    r_dim = rows[0].shape[0]
    tr = _pick(r_dim, tr, mult=8)
    assert r_dim % tr == 0
    nr, nc, nct = len(rows), len(consts), len(cts)
    gi = [i for i, flag in enumerate(row_grad) if flag]
    row_dtypes = row_dtypes or [F32] * len(gi)
    row_add = row_add or [None] * len(gi)
    adds = [a for a in row_add if a is not None]
    outs = jax.eval_shape(lambda *xs: f(*xs), *[jax.ShapeDtypeStruct((tr, x.shape[1]), F32) for x in rows],
                          *[jax.ShapeDtypeStruct(x.shape, F32) for x in consts])

    def body(*refs):
        ins = _f32([r[...] for r in refs[:nr + nc]])
        ct = _f32([r[...] for r in refs[nr + nc:nr + nc + nct]])
        add_refs = list(refs[nr + nc + nct:nr + nc + nct + len(adds)])
        o_refs = refs[nr + nc + nct + len(adds):]
        res, vjp = jax.vjp(f, *ins)
        grads = vjp(tuple(ct))
        for o_ref, i, a in zip(o_refs[:len(gi)], gi, row_add):
            g = grads[i] if a is None else grads[i] + add_refs.pop(0)[...].astype(F32)
            o_ref[...] = g.astype(o_ref.dtype)
        first = pl.program_id(0) == 0
        for o_ref, g in zip(o_refs[len(gi):len(gi) + nc], grads[nr:]):
            @pl.when(first)
            def _(o_ref=o_ref, g=g):
                o_ref[...] = g

            @pl.when(jnp.logical_not(first))
            def _(o_ref=o_ref, g=g):
                o_ref[...] += g
        for o_ref, i in zip(o_refs[len(gi) + nc:], emit):
            o_ref[...] = res[i].astype(o_ref.dtype)

    out_shape = ([jax.ShapeDtypeStruct(rows[i].shape, dt) for i, dt in zip(gi, row_dtypes, strict=True)]
                 + [jax.ShapeDtypeStruct(x.shape, F32) for x in consts]
                 + [jax.ShapeDtypeStruct((r_dim, outs[i].shape[1]), emit_dtype) for i in emit])
    out_specs = ([pl.BlockSpec((tr, rows[i].shape[1]), lambda i_: (i_, 0)) for i in gi]
                 + [pl.BlockSpec(x.shape, lambda i_: (0, 0)) for x in consts]
                 + [pl.BlockSpec((tr, outs[i].shape[1]), lambda i_: (i_, 0)) for i in emit])
    res = pl.pallas_call(
        body, name=name,
        out_shape=out_shape,
        grid=(r_dim // tr,),
        in_specs=[pl.BlockSpec((tr, x.shape[1]), lambda i: (i, 0)) for x in rows]
        + [pl.BlockSpec(x.shape, lambda i: (0, 0)) for x in consts]
        + [pl.BlockSpec((tr, x.shape[1]), lambda i: (i, 0)) for x in list(cts) + adds],
        out_specs=out_specs,
        compiler_params=_cparams("arbitrary"),
    )(*rows, *consts, *cts, *adds)
    return res[:len(gi)], res[len(gi):len(gi) + nc], res[len(gi) + nc:]


def matmul_swiglu(hgu, w, *, name, tm=512, tk_cap=1536):
    m_dim, hid = hgu.shape[0], hgu.shape[1] // 2
    n_dim = w.shape[1]
    tk = _pick(hid, tk_cap)
    nk = hid // tk
    assert m_dim % tm == 0 and hid % tk == 0

    def body(g_ref, u_ref, w_ref, o_ref, acc_ref):
        gate, up = g_ref[...].astype(F32), u_ref[...].astype(F32)
        part = _dot(_silu(gate) * up, w_ref[...], 1, 0)
        k = pl.program_id(1)

        @pl.when(k == 0)
        def _():
            acc_ref[...] = part

        @pl.when(k > 0)
        def _():
            acc_ref[...] += part

        @pl.when(k == nk - 1)
        def _():
            o_ref[...] = acc_ref[...]

    return pl.pallas_call(
        body, name=name, out_shape=jax.ShapeDtypeStruct((m_dim, n_dim), F32), grid=(m_dim // tm, nk),
        in_specs=[pl.BlockSpec((tm, tk), lambda i, k: (i, k)), pl.BlockSpec((tm, tk), lambda i, k: (i, k + nk)),
                  pl.BlockSpec((tk, n_dim), lambda i, k: (k, 0))],
        out_specs=pl.BlockSpec((tm, n_dim), lambda i, k: (i, 0)),
        scratch_shapes=[pltpu.VMEM((tm, n_dim), F32)],
        compiler_params=_cparams("parallel", "arbitrary"),
    )(hgu, hgu, w)


def transpose(a, *, name, out_dtype=BF16, tr=512, tc=512):
    r_dim, c_dim = a.shape
    tr, tc = _pick(r_dim, tr), _pick(c_dim, tc)

    def body(a_ref, o_ref):
        o_ref[...] = a_ref[...].astype(F32).T.astype(o_ref.dtype)

    return pl.pallas_call(
        body, name=name, out_shape=jax.ShapeDtypeStruct((c_dim, r_dim), out_dtype),
        grid=(r_dim // tr, c_dim // tc),
        in_specs=[pl.BlockSpec((tr, tc), lambda i, j: (i, j))],
        out_specs=pl.BlockSpec((tc, tr), lambda i, j: (j, i)),
        compiler_params=_cparams("parallel", "parallel"),
    )(a)


CONV_HALO = 32
CONV_ROWS = 256


def conv_fwd(x, w, b, *, name, cb=256, chunk_rows=CONV_ROWS):
    s_dim, c_dim = x.shape
    taps = w.shape[0]
    assert taps - 1 <= CONV_HALO and s_dim % chunk_rows == 0 and c_dim % cb == 0
    n_chunks = s_dim // chunk_rows
    ext = chunk_rows + CONV_HALO

    def body(x_ref, w_ref, b_ref, o_ref, xp_ref):
        xp_ref[pl.ds(0, CONV_HALO), :] = jnp.zeros((CONV_HALO, cb), F32)
        xp_ref[pl.ds(CONV_HALO, s_dim), :] = x_ref[...].astype(F32)
        wv = w_ref[...].astype(F32)
        bv = b_ref[...].astype(F32)

        def chunk(t, carry):
            base = pl.multiple_of(t * chunk_rows, chunk_rows)
            xe = xp_ref[pl.ds(base, ext), :]
            acc = jnp.broadcast_to(bv, (chunk_rows, cb))
            for j in range(taps):
                sh = xe if j == 0 else pltpu.roll(xe, shift=j, axis=0)
                acc = acc + wv[taps - 1 - j:taps - j, :] * sh[CONV_HALO:, :]
            o_ref[pl.ds(base, chunk_rows), :] = acc
            return carry

        lax.fori_loop(0, n_chunks, chunk, 0)

    return pl.pallas_call(
        body, name=name,
        out_shape=jax.ShapeDtypeStruct((s_dim, c_dim), F32),
        grid=(c_dim // cb,),
        in_specs=[pl.BlockSpec((s_dim, cb), lambda i: (0, i)), pl.BlockSpec((taps, cb), lambda i: (0, i)),
                  pl.BlockSpec((1, cb), lambda i: (0, i))],
        out_specs=pl.BlockSpec((s_dim, cb), lambda i: (0, i)),
        scratch_shapes=[pltpu.VMEM((s_dim + CONV_HALO, cb), F32)],
        compiler_params=_cparams("parallel"),
    )(x, w, b)


def conv_bwd(x, w, g, *, name, cb=256, chunk_rows=CONV_ROWS, dx_dtype=F32):
    s_dim, c_dim = x.shape
    taps = w.shape[0]
    n_chunks = s_dim // chunk_rows
    ext = chunk_rows + CONV_HALO

    def rows8(a):
        return jnp.sum(a.reshape(chunk_rows // 8, 8, cb), axis=0)

    def body(x_ref, w_ref, g_ref, dx_ref, dw_ref, db_ref, xp_ref, gp_ref, acc_ref):
        xp_ref[pl.ds(0, CONV_HALO), :] = jnp.zeros((CONV_HALO, cb), F32)
        xp_ref[pl.ds(CONV_HALO, s_dim), :] = x_ref[...].astype(F32)
        gp_ref[pl.ds(0, s_dim), :] = g_ref[...].astype(F32)
        gp_ref[pl.ds(s_dim, CONV_HALO), :] = jnp.zeros((CONV_HALO, cb), F32)
        acc_ref[...] = jnp.zeros_like(acc_ref)
        wv = w_ref[...].astype(F32)

        def chunk(t, carry):
            base = pl.multiple_of(t * chunk_rows, chunk_rows)
            xe = xp_ref[pl.ds(base, ext), :]
            ge = gp_ref[pl.ds(base, ext), :]
            gc = ge[:chunk_rows, :]
            dx = jnp.zeros((chunk_rows, cb), F32)
            for j in range(taps):
                xs = xe if j == 0 else pltpu.roll(xe, shift=j, axis=0)
                gs = ge if j == 0 else pltpu.roll(ge, shift=ext - j, axis=0)
                k = taps - 1 - j
                dx = dx + wv[k:k + 1, :] * gs[:chunk_rows, :]
                acc_ref[8 * k:8 * k + 8, :] += rows8(gc * xs[CONV_HALO:, :])
            acc_ref[8 * taps:8 * taps + 8, :] += rows8(gc)
            dx_ref[pl.ds(base, chunk_rows), :] = dx.astype(dx_ref.dtype)
            return carry

        lax.fori_loop(0, n_chunks, chunk, 0)
        sums = jnp.sum(acc_ref[...].reshape(taps + 1, 8, cb), axis=1)
        dw_ref[...] = sums[0:taps, :]
        db_ref[...] = sums[taps:taps + 1, :]

    return pl.pallas_call(
        body, name=name,
        out_shape=[jax.ShapeDtypeStruct((s_dim, c_dim), dx_dtype), jax.ShapeDtypeStruct((taps, c_dim), F32),
                   jax.ShapeDtypeStruct((1, c_dim), F32)],
        grid=(c_dim // cb,),
        in_specs=[pl.BlockSpec((s_dim, cb), lambda i: (0, i)), pl.BlockSpec((taps, cb), lambda i: (0, i)),
                  pl.BlockSpec((s_dim, cb), lambda i: (0, i))],
        out_specs=[pl.BlockSpec((s_dim, cb), lambda i: (0, i)), pl.BlockSpec((taps, cb), lambda i: (0, i)),
                   pl.BlockSpec((1, cb), lambda i: (0, i))],
        scratch_shapes=[pltpu.VMEM((s_dim + CONV_HALO, cb), F32), pltpu.VMEM((s_dim + CONV_HALO, cb), F32),
                        pltpu.VMEM((8 * (taps + 1), cb), F32)],
        compiler_params=_cparams("parallel"),
    )(x, w, g)


def _softplus(x):
    return jnp.maximum(x, 0.0) + jnp.log(1.0 + jnp.exp(-jnp.abs(x)))


HEADS_PER_GROUP = SSM_HEADS // SSM_GROUPS


def _bdot(a, b, ca, cb):
    return lax.dot_general(a.astype(BF16), b.astype(BF16), (((ca,), (cb,)), ((0,), (0,))), preferred_element_type=F32)


@jax.custom_vjp
def bmm(a, b):
    return _bdot(a, b, 2, 1)


def _bmm_fwd(a, b):
    return _bdot(a, b, 2, 1), (a, b)


def _bmm_bwd(res, g):
    a, b = res
    return _bdot(g, b, 2, 2).astype(a.dtype), _bdot(a, g, 1, 1).astype(b.dtype)


bmm.defvjp(_bmm_fwd, _bmm_bwd)


@jax.custom_vjp
def bmm_nt(a, b):
    return _bdot(a, b, 2, 2)


def _bmm_nt_fwd(a, b):
    return _bdot(a, b, 2, 2), (a, b)


def _bmm_nt_bwd(res, g):
    a, b = res
    return _bdot(g, b, 2, 1).astype(a.dtype), _bdot(g, a, 1, 1).astype(b.dtype)


bmm_nt.defvjp(_bmm_nt_fwd, _bmm_nt_bwd)


@jax.custom_vjp
def bmm_tn(a, b):
    return _bdot(a, b, 1, 1)


def _bmm_tn_fwd(a, b):
    return _bdot(a, b, 1, 1), (a, b)


def _bmm_tn_bwd(res, g):
    a, b = res
    return _bdot(b, g, 2, 2).astype(a.dtype), _bdot(a, g, 2, 1).astype(b.dtype)


bmm_tn.defvjp(_bmm_tn_fwd, _bmm_tn_bwd)


ATT_PAIRS = ATT_HEADS // 2
PAIR_W = 2 * ATT_HDIM


def att_pairs(q, kp, kc, vp, vc, bias, has_prev):
    t, b, w = q.shape
    i = lax.broadcasted_iota(jnp.int32, (1, b, b), 1)
    j = lax.broadcasted_iota(jnp.int32, (1, b, b), 2)
    first = lax.broadcasted_iota(jnp.int32, (1, 1, w), 2) < ATT_HDIM
    scale = ATT_HDIM ** -0.5
    outs, lses = [], []
    for ab in range(2):
        qh = jnp.where(first if ab == 0 else jnp.logical_not(first), q, 0.0)
        sp = jnp.where(jnp.logical_and(j >= i, has_prev), bmm_nt(qh, kp) * scale + bias[:, ab, 0], -1e30)
        sc = jnp.where(j <= i, bmm_nt(qh, kc) * scale + bias[:, ab, 1], -1e30)
        m = lax.stop_gradient(jnp.maximum(jnp.max(sp, axis=2, keepdims=True), jnp.max(sc, axis=2, keepdims=True)))
        pp, pc = jnp.exp(sp - m), jnp.exp(sc - m)
        l = jnp.sum(pp, axis=2, keepdims=True) + jnp.sum(pc, axis=2, keepdims=True)
        outs.append(bmm(pp / l, vp) + bmm(pc / l, vc))
        lses.append(jnp.broadcast_to(m + jnp.log(l), (t, b, w)))
    return jnp.where(first, outs[0], outs[1]), jnp.where(first, lses[0], lses[1])


def _pair_tiles(ref):
    return jnp.stack([ref[:, PAIR_W * t:PAIR_W * (t + 1)] for t in range(ATT_PAIRS)])


def _store_pair_tiles(ref, val):
    for t in range(ATT_PAIRS):
        ref[:, PAIR_W * t:PAIR_W * (t + 1)] = val[t].astype(ref.dtype)


def pair_bias(bias):
    return bias.reshape(ATT_PAIRS, 2, 2, ATT_BLK, ATT_BLK)


def att2_fwd(q, k, v, bias, nb, cols, *, name, side=None):
    n_blocks = S // ATT_BLK
    qc, kc, vc = cols

    def body(q_ref, k_ref, v_ref, b_ref, o_ref, l_ref, kprev, vprev):
        blk = pl.program_id(0)

        @pl.when(blk == 0)
        def _():
            kprev[...] = jnp.zeros_like(kprev)
            vprev[...] = jnp.zeros_like(vprev)

        k3, v3 = _pair_tiles(k_ref), _pair_tiles(v_ref)
        o, lse = att_pairs(_pair_tiles(q_ref), kprev[...], k3, vprev[...], v3, b_ref[...], (blk % nb) != 0)
        _store_pair_tiles(o_ref, o)
        _store_pair_tiles(l_ref, lse)
        kprev[...] = k3
        vprev[...] = v3

    def spec(c):
        return pl.BlockSpec((ATT_BLK, D), lambda b: (b, c))

    return grid_call(
        body, (q, k, v, bias), name=name,
        out_shape=[jax.ShapeDtypeStruct((S, D), BF16), jax.ShapeDtypeStruct((S, D), F32)], grid=(n_blocks,),
        in_specs=[spec(qc), spec(kc), spec(vc), pl.BlockSpec(bias.shape, lambda b: (0, 0, 0, 0, 0))],
        out_specs=[spec(0), spec(0)],
        scratch_shapes=[pltpu.VMEM((ATT_PAIRS, ATT_BLK, PAIR_W), BF16), pltpu.VMEM((ATT_PAIRS, ATT_BLK, PAIR_W), BF16)],
        semantics=("arbitrary",), side=side)


def att2_bwd(q, k, v, bias, do, dlse, nb, cols, *, name, side=None):
    n_blocks = S // ATT_BLK
    qc, kc, vc = cols

    def body(q_ref, k_ref, v_ref, b_ref, do_ref, dl_ref, dq_ref, dk_ref, dv_ref, db_ref, kprev, vprev, dk_own, dv_own):
        blk = pl.program_id(0)

        @pl.when(blk == 0)
        def _():
            for r in (kprev, vprev, dk_own, dv_own, db_ref):
                r[...] = jnp.zeros_like(r)

        @pl.when(blk < n_blocks)
        def _():
            k3, v3 = _pair_tiles(k_ref), _pair_tiles(v_ref)
            ins = _f32([_pair_tiles(q_ref), kprev[...], k3, vprev[...], v3]) + [b_ref[...]]
            _, vjp = jax.vjp(functools.partial(att_pairs, has_prev=(blk % nb) != 0), *ins)
            dq, dkp, dkc, dvp, dvc, db = vjp(tuple(_f32([_pair_tiles(do_ref), _pair_tiles(dl_ref)])))
            _store_pair_tiles(dq_ref, dq)
            _store_pair_tiles(dk_ref, dk_own[...] + dkp)
            _store_pair_tiles(dv_ref, dv_own[...] + dvp)
            dk_own[...] = dkc
            dv_own[...] = dvc
            db_ref[...] += db
            kprev[...] = k3
            vprev[...] = v3

        @pl.when(blk == n_blocks)
        def _():
            _store_pair_tiles(dk_ref, dk_own[...])
            _store_pair_tiles(dv_ref, dv_own[...])

    def spec(c):
        return pl.BlockSpec((ATT_BLK, D), lambda b: (jnp.minimum(b, n_blocks - 1), c))

    late = pl.BlockSpec((ATT_BLK, D), lambda b: (jnp.maximum(b - 1, 0), 0))
    bspec = pl.BlockSpec(bias.shape, lambda b: (0, 0, 0, 0, 0))
    tile_f32 = pltpu.VMEM((ATT_PAIRS, ATT_BLK, PAIR_W), F32)
    tile_bf16 = pltpu.VMEM((ATT_PAIRS, ATT_BLK, PAIR_W), BF16)
    return grid_call(
        body, (q, k, v, bias, do, dlse), name=name,
        out_shape=[jax.ShapeDtypeStruct((S, D), BF16), jax.ShapeDtypeStruct((S, D), BF16),
                   jax.ShapeDtypeStruct((S, D), BF16), jax.ShapeDtypeStruct(bias.shape, F32)],
        grid=(n_blocks + 1,),
        in_specs=[spec(qc), spec(kc), spec(vc), bspec, spec(0), spec(0)],
        out_specs=[spec(0), late, late, bspec],
        scratch_shapes=[tile_bf16, tile_bf16, tile_f32, tile_f32],
        semantics=("arbitrary",), side=side)


def regroup(a, dil, inverse=False):
    if dil == 1:
        return a
    c_dim = a.shape[1]
    shape = (dil, S // dil, c_dim) if inverse else (S // dil, dil, c_dim)
    return jnp.transpose(a.reshape(shape), (1, 0, 2)).reshape(S, c_dim)


SSD_PAIRS = SSM_HEADS // 2
PAIRS_PER_GROUP = SSD_PAIRS // SSM_GROUPS
GROUP_W = HEADS_PER_GROUP * SSM_HDIM


SSD_GROUPS_PER_STEP = 2
SSD_STEP_PAIRS = PAIRS_PER_GROUP * SSD_GROUPS_PER_STEP
SSD_STEP_W = GROUP_W * SSD_GROUPS_PER_STEP


def ssd_pairs(x, dtraw, dt_bias, a_log, dskip, bms, cms, prev):
    t, q, w = x.shape
    n = bms[0].shape[1]
    per = t // len(bms)

    def by_pair(mats):
        return jnp.concatenate([jnp.broadcast_to(m[None], (per,) + m.shape) for m in mats], axis=0)
    li = lax.broadcasted_iota(jnp.int32, (1, q, q), 1)
    si = lax.broadcasted_iota(jnp.int32, (1, q, q), 2)
    first_lane = lax.broadcasted_iota(jnp.int32, (1, 1, w), 2) < SSM_HDIM
    first_row = lax.broadcasted_iota(jnp.int32, (1, w, 1), 1) < SSM_HDIM

    def to_col(row):
        return jnp.sum(jnp.where(li == si, jnp.broadcast_to(row, (t, q, q)), 0.0), axis=2, keepdims=True)

    def lanes(a0, a1):
        return jnp.where(first_lane, a0, a1)

    dt_col, acs_col, total, lmat = [], [], [], []
    for ab in range(2):
        dt_row = _softplus(dtraw[ab] + dt_bias[ab])
        a_row = dt_row * (-jnp.exp(a_log[ab]))
        a_col = to_col(a_row)
        acs_c = jnp.sum(jnp.where(si <= li, jnp.broadcast_to(a_row, (t, q, q)), 0.0), axis=2, keepdims=True)
        acs_r = jnp.sum(jnp.where(li <= si, jnp.broadcast_to(a_col, (t, q, q)), 0.0), axis=1, keepdims=True)
        dt_col.append(to_col(dt_row))
        acs_col.append(acs_c)
        total.append(jnp.sum(a_row, axis=2, keepdims=True))
        lmat.append(jnp.exp(jnp.where(li >= si, acs_c - acs_r, -1e30)))
    cb = by_pair([mm_nt(c_, b_) for c_, b_ in zip(cms, bms, strict=True)])
    bmb, cmb = by_pair(bms), by_pair(cms)
    xdt = x * lanes(dt_col[0], dt_col[1])
    y = lanes(bmm(cb * lmat[0], xdt), bmm(cb * lmat[1], xdt))
    y = y + bmm_nt(cmb, prev) * lanes(jnp.exp(acs_col[0]), jnp.exp(acs_col[1]))
    y = y + lanes(dskip[0], dskip[1]) * x
    state = bmm_tn(xdt * lanes(jnp.exp(total[0] - acs_col[0]), jnp.exp(total[1] - acs_col[1])), bmb)
    return y, jnp.where(first_row, jnp.exp(total[0]), jnp.exp(total[1])) * prev + state


def _group_tiles(ref):
    return jnp.stack([ref[:, PAIR_W * t:PAIR_W * (t + 1)] for t in range(SSD_STEP_PAIRS)])


def _store_group_tiles(ref, val):
    for t in range(SSD_STEP_PAIRS):
        ref[:, PAIR_W * t:PAIR_W * (t + 1)] = val[t]


def _bc_groups(ref):
    return tuple(ref[:, SSM_STATE * i:SSM_STATE * (i + 1)] for i in range(SSD_GROUPS_PER_STEP))


def _by_pair(a):
    return jnp.transpose(a.reshape(SSD_PAIRS, 2, 1, -1), (1, 0, 2, 3))


def _by_head(a):
    return jnp.transpose(a, (1, 0, 2, 3)).reshape(SSM_HEADS, -1)


def _ssd2_specs(chunk_of):
    tp = SSD_STEP_PAIRS
    xspec = pl.BlockSpec((CHUNK, SSD_STEP_W), lambda g, c: (chunk_of(c), g))
    tspec = pl.BlockSpec((2, tp, 1, CHUNK), lambda g, c: (0, g, 0, chunk_of(c)))
    hp = pl.BlockSpec((2, tp, 1, 1), lambda g, c: (0, g, 0, 0))
    gspec = pl.BlockSpec((CHUNK, SSD_GROUPS_PER_STEP * SSM_STATE), lambda g, c: (chunk_of(c), g))
    sspec = pl.BlockSpec((1, tp, PAIR_W, SSM_STATE), lambda g, c: (chunk_of(c), g, 0, 0))
    return xspec, tspec, hp, gspec, sspec


def ssd2_fwd(xs, dtraw_t, dt_bias, a_log, dskip, bm, cm, side=None):
    def body(x_ref, dt_ref, dtb_ref, al_ref, dk_ref, bm_ref, cm_ref, y_ref, prev_ref, state_ref):
        @pl.when(pl.program_id(1) == 0)
        def _():
            state_ref[...] = jnp.zeros_like(state_ref)

        prev = state_ref[...]
        prev_ref[0] = prev
        y, nxt = ssd_pairs(_group_tiles(x_ref), dt_ref[...], dtb_ref[...], al_ref[...], dk_ref[...], _bc_groups(bm_ref),
                           _bc_groups(cm_ref), prev)
        _store_group_tiles(y_ref, y)
        state_ref[...] = nxt

    xspec, tspec, hp, gspec, sspec = _ssd2_specs(lambda c: c)
    return grid_call(
        body, (xs, _by_pair(dtraw_t), _by_pair(dt_bias), _by_pair(a_log), _by_pair(dskip), bm, cm), name="ssd_fwd",
        out_shape=[jax.ShapeDtypeStruct((S, SSM_INNER), F32),
                   jax.ShapeDtypeStruct((N_CHUNKS, SSD_PAIRS, PAIR_W, SSM_STATE), F32)],
        grid=(SSM_GROUPS // SSD_GROUPS_PER_STEP, N_CHUNKS), in_specs=[xspec, tspec, hp, hp, hp, gspec, gspec],
        out_specs=[xspec, sspec],
        scratch_shapes=[pltpu.VMEM((SSD_STEP_PAIRS, PAIR_W, SSM_STATE), F32)],
        semantics=("parallel", "arbitrary"), side=side)


def ssd2_bwd(xs, dtraw_t, dt_bias, a_log, dskip, bm, cm, prev_all, dy, side=None):
    def body(x_ref, dt_ref, dtb_ref, al_ref, dk_ref, bm_ref, cm_ref, prev_ref, dy_ref,
             dx_ref, ddt_ref, ddtb_ref, dal_ref, ddk_ref, dbm_ref, dcm_ref, dstate_ref):
        @pl.when(pl.program_id(1) == 0)
        def _():
            for r in (dstate_ref, ddtb_ref, dal_ref, ddk_ref):
                r[...] = jnp.zeros_like(r)

        _, vjp = jax.vjp(ssd_pairs, _group_tiles(x_ref), dt_ref[...], dtb_ref[...], al_ref[...], dk_ref[...],
                         _bc_groups(bm_ref), _bc_groups(cm_ref), prev_ref[0])
        dx, ddt, ddtb, dal, ddk, dbms, dcms, dprev = vjp((_group_tiles(dy_ref), dstate_ref[...]))
        _store_group_tiles(dx_ref, dx)
        ddt_ref[...] = ddt
        ddtb_ref[...] += ddtb
        dal_ref[...] += dal
        ddk_ref[...] += ddk
        for i in range(SSD_GROUPS_PER_STEP):
            dbm_ref[:, SSM_STATE * i:SSM_STATE * (i + 1)] = dbms[i]
            dcm_ref[:, SSM_STATE * i:SSM_STATE * (i + 1)] = dcms[i]
        dstate_ref[...] = dprev

    xspec, tspec, hp, gspec, sspec = _ssd2_specs(lambda c: N_CHUNKS - 1 - c)
    par = jax.ShapeDtypeStruct((2, SSD_PAIRS, 1, 1), F32)
    res, side_dst = grid_call(
        body, (xs, _by_pair(dtraw_t), _by_pair(dt_bias), _by_pair(a_log), _by_pair(dskip), bm, cm, prev_all, dy),
        name="ssd_bwd",
        out_shape=[jax.ShapeDtypeStruct((S, SSM_INNER), F32), jax.ShapeDtypeStruct((2, SSD_PAIRS, 1, S), F32), par, par, par,
                   jax.ShapeDtypeStruct((S, SSM_GROUPS * SSM_STATE), F32),
                   jax.ShapeDtypeStruct((S, SSM_GROUPS * SSM_STATE), F32)],
        grid=(SSM_GROUPS // SSD_GROUPS_PER_STEP, N_CHUNKS), in_specs=[xspec, tspec, hp, hp, hp, gspec, gspec, sspec, xspec],
        out_specs=[xspec, tspec, hp, hp, hp, gspec, gspec],
        scratch_shapes=[pltpu.VMEM((SSD_STEP_PAIRS, PAIR_W, SSM_STATE), F32)],
        semantics=("parallel", "arbitrary"), side=side)
    return [res[0]] + [_by_head(r) for r in res[1:5]] + list(res[5:]), side_dst


def _silu(x):
    return x * jax.nn.sigmoid(x)


def _rms(x):
    return x * lax.rsqrt(jnp.mean(x * x, -1, keepdims=True) + EPS)


def f_normmod(x, g, sc, sh):
    return (_rms(x) * g * (1.0 + sc) + sh,)


def f_resid(x, mix, gate):
    return (x + gate * mix,)


def f_resid_bias(x, mix, gate, b):
    return (x + gate * (mix + b),)


def f_swiglu(hgu):
    return (_silu(hgu[:, :FFN_HIDDEN]) * hgu[:, FFN_HIDDEN:],)


def f_silu(x):
    return (_silu(x),)


def f_silu_xbc(x):
    y = _silu(x)
    n_b = SSM_GROUPS * SSM_STATE
    return y[:, :SSM_INNER], y[:, SSM_INNER:SSM_INNER + n_b], y[:, SSM_INNER + n_b:]


def f_gated_norm(y, z, g):
    return (_rms(y * _silu(z)) * g,)


def f_glu(y, b):
    y = y + b
    return (y[:, :D] * jax.nn.sigmoid(y[:, D:]),)


def f_ln_silu(u, g, b):
    mu = jnp.mean(u, -1, keepdims=True)
    var = jnp.mean(jnp.square(u - mu), -1, keepdims=True)
    return (_silu((u - mu) * lax.rsqrt(var + EPS) * g + b),)


def f_combine(o1, o2, o3, l1, l2, l3):
    m = lax.stop_gradient(jnp.maximum(jnp.maximum(l1, l2), l3))
    e1, e2, e3 = jnp.exp(l1 - m), jnp.exp(l2 - m), jnp.exp(l3 - m)
    return ((e1 * o1 + e2 * o2 + e3 * o3) / (e1 + e2 + e3),)


def f_head(x, tgt, g):
    return (0.5 * jnp.mean(jnp.square(_rms(x) * g - tgt), -1, keepdims=True),)


def f_sum3(a, b, c):
    return (a + b + c,)


def f_add(a, b):
    return (a + b,)


def f_adamw(w, g, m, v):
    m = ADAM_B1 * m + (1.0 - ADAM_B1) * g
    v = ADAM_B2 * v + (1.0 - ADAM_B2) * jnp.square(g)
    m_hat = m / (1.0 - ADAM_B1 ** ADAM_STEP)
    v_hat = v / (1.0 - ADAM_B2 ** ADAM_STEP)
    return -ADAM_LR * (m_hat / (jnp.sqrt(v_hat) + ADAM_EPS) + ADAM_WD * w), m, v


def _rows_tile(r, cap=256):
    return _pick(r, cap, mult=8)


def adamw(w, g, m, v, *, name):
    l_dim, r_dim, c_dim = w.shape
    tr = _rows_tile(r_dim, cap=128)

    def body(w_ref, g_ref, m_ref, v_ref, d_ref, mo_ref, vo_ref):
        d_ref[...], mo_ref[...], vo_ref[...] = f_adamw(w_ref[...], g_ref[...], m_ref[...], v_ref[...])

    spec = pl.BlockSpec((1, tr, c_dim), lambda l, i: (l, i, 0))
    return pl.pallas_call(
        body, name=name, out_shape=[jax.ShapeDtypeStruct(w.shape, F32)] * 3, grid=(l_dim, r_dim // tr),
        in_specs=[spec] * 4, out_specs=[spec] * 3, compiler_params=_cparams("parallel", "parallel"),
    )(w, g, m, v)


def _t5_bucket(dist):
    max_exact = REL_BUCKETS // 2
    n = jnp.maximum(dist, 1).astype(F32)
    large = max_exact + jnp.log(n / max_exact) / math.log(REL_MAX_DIST / max_exact) * (REL_BUCKETS - max_exact)
    large = jnp.minimum(large.astype(jnp.int32), REL_BUCKETS - 1)
    return jnp.where(dist < max_exact, dist, large)


def _att_buckets(dil):
    i = jnp.arange(ATT_BLK)[:, None]
    j = jnp.arange(2 * ATT_BLK)[None, :]
    bkt = _t5_bucket(jnp.maximum(ATT_BLK + i - j, 0) * dil)
    return jnp.transpose(bkt.reshape(ATT_BLK, 2, ATT_BLK), (1, 0, 2))


def att_bias(rel_table, p, dil):
    tab = rel_table[:, p * ATT_HEADS:(p + 1) * ATT_HEADS]
    onehot = (jnp.arange(REL_BUCKETS)[:, None] == _att_buckets(dil).reshape(1, -1)).astype(F32)
    bias = lax.dot_general(tab, onehot, (((0,), (0,)), ((), ())), precision=lax.Precision.HIGHEST)
    return bias.reshape(ATT_HEADS, 2, ATT_BLK, ATT_BLK)


def att_bias_grad(dbias, dil, *, name):
    onehot = (_att_buckets(dil).reshape(-1, 1) == jnp.arange(LANES)[None, :]).astype(BF16)
    dtab = matmul(dbias.reshape(ATT_HEADS, -1), onehot, mode="nn", out_dtype=F32, name=name, tk_cap=2048)
    return dtab[:, :REL_BUCKETS].T


HY_Z, HY_XBC, HY_DT, HY_Q, HY_K, HY_V = 2048, 3072, 32, 3072, 1024, 1024
HY_IN = HY_Z + HY_XBC + HY_DT + HY_Q + HY_K + HY_V
OFF_Z, OFF_XBC, OFF_Q, OFF_KV, OFF_DT = 0, 2048, 5120, 8192, 10240
HY_CAT = OFF_DT + LANES
DT_PAD = LANES


def hy_to_cat(w):
    z, xbc, dt, qkv = w[:2048], w[2048:5120], w[5120:5152], w[5152:]
    return jnp.concatenate([z, xbc, qkv, dt, jnp.zeros((DT_PAD - HY_DT,) + w.shape[1:], w.dtype)], axis=0)


def hy_from_cat(w, axis=0):
    part = lambda a, b: lax.slice_in_dim(w, a, b, axis=axis)
    return jnp.concatenate([part(0, 5120), part(OFF_DT, OFF_DT + HY_DT), part(5120, OFF_DT)], axis=axis)


def device_step(x, tgt, mods, wts, sp, comm=None):
    g = {}
    dmods = [[None] * 6 for _ in range(2)]
    wts = dict(wts)

    def wgrad(tokens_d, tokens_n, nm):
        return matmul(transpose(tokens_d, name=nm + "_t"), tokens_n, mode="nn", out_dtype=BF16, name=nm, out_t=True,
                      tk_cap=2048)

    def w_side(i):
        return None if comm is None else GatherRows(comm["pack"], comm["full"], *W_BATCHES[i])

    def g_side(i):
        return None if comm is None else ScatterRows(comm["ga"], comm["recv"], *G_BATCHES[i])

    def normmod(xi, gain, sc, sh, nm):
        return rowmap(f_normmod, [xi], [gain, sc, sh], [BF16], name=nm)[0]

    def ffn_fwd(xi, i, gate, nm):
        h = normmod(xi, sp["norm_ffn_g"][i], mods[i][4], mods[i][3], nm + "_norm")
        hgu = matmul(h, wts["gu_t"][i], mode="nt", out_dtype=BF16, name=nm + "_gu")
        out = matmul_swiglu(hgu, wts["down"][i], name=nm + "_down")
        xo = rowmap(f_resid, [xi, out], [gate], [F32], name=nm + "_res")[0]
        return xo, (h, hgu, out)

    def ffn_bwd(dres, xi, i, saved, nm):
        h, hgu, out = saved
        (dout,), (dgate,), _ = rowmap_bwd(f_resid, [xi, out], [mods[i][5]], [dres], name=nm + "_res_b",
                                          row_grad=[False, True], row_dtypes=[BF16])
        dmods[i][5] = dgate
        dact = matmul(dout, wts["down"][i], mode="nt", out_dtype=BF16, name=nm + "_down_dx")
        (dhgu,), _, (act,) = rowmap_bwd(f_swiglu, [hgu], [], [dact], name=nm + "_act_b", row_grad=[True],
                                        row_dtypes=[BF16], tr=128, emit=(0,), emit_dtype=BF16)
        g[f"down{i}"] = wgrad(dout, act, nm + "_down_dw")
        g[f"gu_t{i}"] = wgrad(h, dhgu, nm + "_gu_dw")
        dh = matmul(dhgu, wts["gu_t"][i], mode="nn", out_dtype=F32, name=nm + "_gu_dx")
        (dres,), (dg_, dsc, dsh), _ = rowmap_bwd(f_normmod, [xi], [sp["norm_ffn_g"][i], mods[i][4], mods[i][3]], [dh],
                                                 name=nm + "_norm_b", row_grad=[True], row_add=[dres])
        g[f"norm_ffn_g{i}"] = dg_
        dmods[i][4], dmods[i][3] = dsc, dsh
        return dres

    h0 = normmod(x, sp["norm_mix_g"][0], mods[0][1], mods[0][0], "l0_norm")
    w_in = wts["hy_in_t"]
    z = matmul(h0, w_in, mode="nt", out_dtype=BF16, name="hy_z", n=HY_Z, b_off=OFF_Z)
    xbc_raw = matmul(h0, w_in, mode="nt", out_dtype=BF16, name="hy_xbc", n=HY_XBC, b_off=OFF_XBC)
    q = matmul(h0, w_in, mode="nt", out_dtype=BF16, name="hy_q", n=HY_Q, b_off=OFF_Q)
    kv = matmul(h0, w_in, mode="nt", out_dtype=BF16, name="hy_kv", n=HY_K + HY_V, b_off=OFF_KV)
    dtr = matmul(h0, w_in, mode="nt", out_dtype=F32, name="hy_dt", n=DT_PAD, b_off=OFF_DT)
    xbc_pre = conv_fwd(xbc_raw, sp["hy_conv_w"], sp["hy_conv_b"], name="hy_conv")
    xs, bm, cm = rowmap(f_silu_xbc, [xbc_pre], [], [F32] * 3, name="hy_conv_act", tr=256)
    dtraw_t = dtr[:, :HY_DT].T
    (y, prev_all), full = ssd2_fwd(xs, dtraw_t, sp["hy_dt_bias"], sp["hy_a_log"], sp["hy_d_skip"], bm, cm, side=w_side(1))
    if comm is not None:
        comm["full"] = full
    ysn = rowmap(f_gated_norm, [y, z], [sp["hy_ssm_norm_g"]], [BF16], name="hy_gnorm", tr=128)[0]
    att_in, att_o, att_l = [], [], []
    for p, (win, dil) in enumerate(ATT_PATTERNS):
        if dil == 1:
            qa, ka, va, cols = q, kv, kv, (p, 0, 1)
        else:
            qa, ka, cols = regroup(q[:, p * D:(p + 1) * D], dil), regroup(kv, dil), (0, 0, 1)
            va = ka
        bias = pair_bias(att_bias(sp["rel_table"], p, dil))
        nb = S // dil // ATT_BLK
        (o, lse), full = att2_fwd(qa, ka, va, bias, nb, cols, name=f"att_fwd{p}", side=w_side(2 + p))
        if comm is not None:
            comm["full"] = full
        att_in.append((qa, ka, va, bias, nb, cols))
        att_o.append(regroup(o, dil, inverse=True))
        att_l.append(regroup(lse, dil, inverse=True))
    if comm is not None:
        wts.update(unpack_weights(comm["full"], skip=("hy_in_t",)))
    att = rowmap(f_combine, att_o + att_l, [], [BF16], name="att_combine", tr=256)[0]
    cat = jnp.concatenate([ysn, att], axis=-1)
    mix0 = matmul(cat, wts["hy_out"], mode="nn", out_dtype=F32, name="hy_out")
    x1 = rowmap(f_resid, [x, mix0], [mods[0][2]], [F32], name="l0_res")[0]
    x2, ffn0 = ffn_fwd(x1, 0, mods[0][5], "ffn0")

    h1 = normmod(x2, sp["norm_mix_g"][1], mods[1][1], mods[1][0], "l1_norm")
    p1 = matmul(h1, wts["pw1_t"], mode="nt", out_dtype=BF16, name="cv_pw1")
    u = rowmap(f_glu, [p1], [sp["cv_b_pw1"]], [F32], name="cv_glu")[0]
    uc = conv_fwd(u, sp["cv_w_dw"], sp["cv_b_dw"], name="cv_conv")
    ul = rowmap(f_ln_silu, [uc], [sp["cv_ln_g"], sp["cv_ln_b"]], [BF16], name="cv_ln")[0]
    mix1 = matmul(ul, wts["pw2"], mode="nn", out_dtype=F32, name="cv_pw2")
    x3 = rowmap(f_resid_bias, [x2, mix1], [mods[1][2], sp["cv_b_pw2"]], [F32], name="l1_res")[0]
    x4, ffn1 = ffn_fwd(x3, 1, mods[1][5], "ffn1")

    ones = jnp.ones((S, 1), F32)
    (dres,), (dfinal,), (loss_rows,) = rowmap_bwd(f_head, [x4, tgt], [sp["final_norm_g"]], [ones], name="head",
                                                  row_grad=[True, False], emit=(0,))
    g["final_norm_g"] = dfinal

    dres = ffn_bwd(dres, x3, 1, ffn1, "ffn1")
    (dmix1,), (dg1, db2), _ = rowmap_bwd(f_resid_bias, [x2, mix1], [mods[1][2], sp["cv_b_pw2"]], [dres], name="l1_res_b",
                                         row_grad=[False, True], row_dtypes=[BF16])
    dmods[1][2] = dg1
    g["cv_b_pw2"] = db2
    dul = matmul(dmix1, wts["pw2"], mode="nt", out_dtype=F32, name="cv_pw2_dx")
    g["pw2"] = wgrad(dmix1, ul, "cv_pw2_dw")
    (duc,), (g["cv_ln_g"], g["cv_ln_b"]), _ = rowmap_bwd(f_ln_silu, [uc], [sp["cv_ln_g"], sp["cv_ln_b"]], [dul],
                                                         name="cv_ln_b", row_grad=[True])
    du, g["cv_w_dw"], g["cv_b_dw"] = conv_bwd(u, sp["cv_w_dw"], duc, name="cv_conv_b", cb=128, chunk_rows=128)
    (dp1,), (g["cv_b_pw1"],), _ = rowmap_bwd(f_glu, [p1], [sp["cv_b_pw1"]], [du], name="cv_glu_b", row_grad=[True],
                                             row_dtypes=[BF16])
    g["pw1_t"] = wgrad(h1, dp1, "cv_pw1_dw")
    dh1 = matmul(dp1, wts["pw1_t"], mode="nn", out_dtype=F32, name="cv_pw1_dx")
    (dres,), (dg_, dsc, dsh), _ = rowmap_bwd(f_normmod, [x2], [sp["norm_mix_g"][1], mods[1][1], mods[1][0]], [dh1],
                                             name="l1_norm_b", row_grad=[True], row_add=[dres])
    g["norm_mix_g1"] = dg_
    dmods[1][1], dmods[1][0] = dsc, dsh

    dres = ffn_bwd(dres, x1, 0, ffn0, "ffn0")
    (dmix0,), (dg1,), _ = rowmap_bwd(f_resid, [x, mix0], [mods[0][2]], [dres], name="l0_res_b",
                                     row_grad=[False, True], row_dtypes=[BF16])
    dmods[0][2] = dg1
    dysn = matmul(dmix0, wts["hy_out"], mode="nt", out_dtype=F32, name="hy_out_dy", n=SSM_INNER, b_off=0)
    datt = matmul(dmix0, wts["hy_out"], mode="nt", out_dtype=F32, name="hy_out_da", n=D, b_off=SSM_INNER)
    g["hy_out"] = wgrad(dmix0, cat, "hy_out_dw")
    (dy, dz), (g["hy_ssm_norm_g"],), _ = rowmap_bwd(f_gated_norm, [y, z], [sp["hy_ssm_norm_g"]], [dysn], name="hy_gnorm_b",
                                                    row_grad=[True, True], row_dtypes=[F32, BF16], tr=128)
    if comm is not None:
        comm["ga"] = pack_grads(g, GA_LAYOUT, GA_ROWS)
        comm["recv"] = lax.empty((3, GA_ROWS, D), BF16)
    (dxs, ddtraw_t, g["hy_dt_bias"], g["hy_a_log"], g["hy_d_skip"], dbm, dcm), recv = ssd2_bwd(
        xs, dtraw_t, sp["hy_dt_bias"], sp["hy_a_log"], sp["hy_d_skip"], bm, cm, prev_all, dy, side=g_side(0))
    if comm is not None:
        comm["recv"] = recv
    (dxbc_pre,), _, _ = rowmap_bwd(f_silu_xbc, [xbc_pre], [], [dxs, dbm, dcm], name="hy_conv_act_b", row_grad=[True],
                                   tr=128)
    dxbc_raw, g["hy_conv_w"], g["hy_conv_b"] = conv_bwd(xbc_raw, sp["hy_conv_w"], dxbc_pre, name="hy_conv_b", cb=128, chunk_rows=128, dx_dtype=BF16)
    dol, _, _ = rowmap_bwd(f_combine, att_o + att_l, [], [datt], name="att_combine_b", row_grad=[True] * 6,
                           row_dtypes=[BF16] * 3 + [F32] * 3, tr=128)
    dqs, dks, dvs, dtabs = [], [], [], []
    for p, (win, dil) in enumerate(ATT_PATTERNS):
        qa, ka, va, bias, nb, cols = att_in[p]
        (dq, dkp_, dvp_, dbias), recv = att2_bwd(qa, ka, va, bias, regroup(dol[p], dil), regroup(dol[3 + p], dil), nb,
                                                 cols, name=f"att_bwd{p}", side=g_side(1 + p))
        if comm is not None:
            comm["recv"] = recv
        dqs.append(regroup(dq, dil, inverse=True))
        dks.append(regroup(dkp_, dil, inverse=True))
        dvs.append(regroup(dvp_, dil, inverse=True))
        dtabs.append(att_bias_grad(dbias.reshape(ATT_HEADS, 2, ATT_BLK, ATT_BLK), dil, name=f"att_dtab{p}"))
    g["rel_table"] = jnp.concatenate(dtabs, axis=1)
    dk = rowmap(f_sum3, dks, [], [BF16], name="att_dk_sum")[0]
    dv = rowmap(f_sum3, dvs, [], [BF16], name="att_dv_sum")[0]
    ddt = jnp.pad(ddtraw_t.T, ((0, 0), (0, DT_PAD - HY_DT)))
    dproj = jnp.concatenate([dz, dxbc_raw] + dqs + [dk, dv, ddt.astype(BF16)], axis=-1)
    g["hy_in_t"] = wgrad(h0, dproj, "hy_in_dw")
    if comm is None:
        dh0 = matmul(dproj, w_in, mode="nn", out_dtype=F32, name="hy_in_dx")
    else:
        gb = pack_grads(g, GB_LAYOUT, GB_ROWS)
        half = GB_ROWS // 2
        theirs = swap_halves(gb, name="swap_in_halves")
        ours = lax.dynamic_slice_in_dim(gb, lax.axis_index("c") * half, half, axis=1)
        comm["gb"] = rowmap(f_add, [ours.reshape(N_CHIPS * half, D), theirs.reshape(N_CHIPS * half, D)], [], [BF16],
                            name="sum_in_cores")[0].reshape(N_CHIPS, half, D)
        dh0, comm["recv_b"] = matmul(dproj, w_in, mode="nn", out_dtype=F32, name="hy_in_dx",
                                     side=ScatterRows(comm["gb"], lax.empty((3, half, D), BF16), 0, half))
    (dres,), (dg_, dsc, dsh), _ = rowmap_bwd(f_normmod, [x], [sp["norm_mix_g"][0], mods[0][1], mods[0][0]], [dh0],
                                             name="l0_norm_b", row_grad=[True], row_add=[dres])
    g["norm_mix_g0"] = dg_
    dmods[0][1], dmods[0][0] = dsc, dsh
    return loss_rows, dres, g, dmods


ANY = pl.BlockSpec(memory_space=pl.ANY)
WHOLE_VMEM = pl.BlockSpec(memory_space=pltpu.VMEM)


def _place():
    return lax.axis_index("x"), lax.axis_index("y"), lax.axis_index("c")


def _other_chips(x, y):
    return [(1 - x, y), (x, 1 - y), (1 - x, 1 - y)]


def allgather_small(v, *, name, side=None):
    m_per = v.shape[0]

    def gather(x_ref, out_ref, send_sems, recv_sems, local_sem):
        x, y, c = _place()
        me, sibling = (x, y, c), (x, y, 1 - c)
        chips = _other_chips(x, y)

        def rows(px, py, pc):
            return out_ref.at[pl.ds((4 * px + 2 * py + pc) * m_per, m_per), :]

        def copy(k, block, to, src=None):
            return pltpu.make_async_remote_copy(
                src_ref=rows(*block) if src is None else src, dst_ref=rows(*block),
                send_sem=send_sems.at[k], recv_sem=recv_sems.at[k], device_id=to, device_id_type=MESH)

        mine = pltpu.make_async_copy(x_ref, rows(*me), local_sem)
        mine.start()
        first = [copy(0, me, sibling, src=x_ref)]
        first += [copy(1 + j, me, (*chip, c), src=x_ref) for j, chip in enumerate(chips)]
        for cp in first:
            cp.start()
        passed = [copy(4 + j, (*chip, c), sibling) for j, chip in enumerate(chips)]
        for j, chip in enumerate(chips):
            copy(1 + j, (*chip, c), me).wait_recv()
            passed[j].start()
        copy(0, sibling, me).wait_recv()
        for j, chip in enumerate(chips):
            copy(4 + j, (*chip, 1 - c), me).wait_recv()
        for cp in first + passed:
            cp.wait_send()
        mine.wait()

    out = jax.ShapeDtypeStruct((N_DEV * m_per, LANES), v.dtype)
    sems = [pltpu.SemaphoreType.DMA((7,)), pltpu.SemaphoreType.DMA((7,)), pltpu.SemaphoreType.DMA]
    if side is None:
        return pl.pallas_call(gather, name=name, out_shape=out, in_specs=[WHOLE_VMEM], out_specs=WHOLE_VMEM,
                              scratch_shapes=sems)(v)

    def body(x_ref, src_ref, dst_in_ref, out_ref, dst_ref, send_sems, recv_sems, local_sem, *side_sems):
        side.start(src_ref, dst_ref, side_sems)
        gather(x_ref, out_ref, send_sems, recv_sems, local_sem)
        side.finish(src_ref, dst_ref, side_sems)

    return pl.pallas_call(
        body, name=name, out_shape=[out, jax.ShapeDtypeStruct(side.dst.shape, side.dst.dtype)],
        in_specs=[WHOLE_VMEM, ANY, ANY], out_specs=[WHOLE_VMEM, ANY], scratch_shapes=sems + side.sems(),
        input_output_aliases={2: 1},
    )(v, side.src, side.dst)


def swap_halves(gpack, *, name):
    half_rows = gpack.shape[1] // 2

    def body(g_ref, r_ref, send_sems, recv_sems):
        x, y, c = _place()
        its_half = pl.ds((1 - c) * half_rows, half_rows)
        copies = [pltpu.make_async_remote_copy(
            src_ref=g_ref.at[s, its_half], dst_ref=r_ref.at[s], send_sem=send_sems.at[s], recv_sem=recv_sems.at[s],
            device_id=(x, y, 1 - c), device_id_type=MESH) for s in range(N_CHIPS)]
        for cp in copies:
            cp.start()
        for cp in copies:
            cp.wait()

    return pl.pallas_call(
        body, name=name,
        out_shape=jax.ShapeDtypeStruct((N_CHIPS, half_rows) + gpack.shape[2:], gpack.dtype),
        in_specs=[ANY], out_specs=ANY,
        scratch_shapes=[pltpu.SemaphoreType.DMA((N_CHIPS,)), pltpu.SemaphoreType.DMA((N_CHIPS,))],
    )(gpack)


class GatherRows:
    def __init__(self, pack, full, lo, hi):
        assert (hi - lo) % 32 == 0 and lo % 16 == 0
        self.src, self.dst, self.lo, self.hi = pack, full, lo, hi

    def sems(self):
        return [pltpu.SemaphoreType.DMA((6,)), pltpu.SemaphoreType.DMA((6,)), pltpu.SemaphoreType.DMA]

    def _parts(self, pack_ref, full_ref, sems):
        send_sems, recv_sems, local_sem = sems
        x, y, c = _place()
        half = (self.hi - self.lo) // 2
        mine, its = pl.ds(self.lo + c * half, half), pl.ds(self.lo + (1 - c) * half, half)
        rows = pl.ds(self.lo, self.hi - self.lo)
        local = pltpu.make_async_copy(pack_ref.at[rows], full_ref.at[2 * x + y, rows], local_sem)
        chips = _other_chips(x, y)

        def remote(src, dst, k, to):
            return pltpu.make_async_remote_copy(src_ref=src, dst_ref=dst, send_sem=send_sems.at[k],
                                                recv_sem=recv_sems.at[k], device_id=to, device_id_type=MESH)

        sends = [remote(pack_ref.at[mine], full_ref.at[2 * x + y, mine], k, (cx, cy, c)) for k, (cx, cy) in enumerate(chips)]
        landed = [full_ref.at[2 * cx + cy, mine] for cx, cy in chips]
        arrive = [remote(pack_ref.at[mine], landed[k], k, (cx, cy, c)) for k, (cx, cy) in enumerate(chips)]
        passed = [remote(landed[k], landed[k], 3 + k, (x, y, 1 - c)) for k in range(3)]
        from_sibling = [remote(landed[k], full_ref.at[2 * cx + cy, its], 3 + k, (x, y, 1 - c))
                        for k, (cx, cy) in enumerate(chips)]
        return local, sends, arrive, passed, from_sibling

    def start(self, pack_ref, full_ref, sems):
        local, sends, _, _, _ = self._parts(pack_ref, full_ref, sems)
        local.start()
        for cp in sends:
            cp.start()

    def finish(self, pack_ref, full_ref, sems):
        local, sends, arrive, passed, from_sibling = self._parts(pack_ref, full_ref, sems)
        for k in range(3):
            arrive[k].wait_recv()
            passed[k].start()
        for cp in from_sibling:
            cp.wait_recv()
        for cp in sends + passed:
            cp.wait_send()
        local.wait()


class ScatterRows:
    def __init__(self, gpack, recv, lo, hi):
        assert lo % 16 == 0 and hi % 16 == 0
        self.src, self.dst, self.lo, self.hi = gpack, recv, lo, hi

    def sems(self):
        return [pltpu.SemaphoreType.DMA((3,)), pltpu.SemaphoreType.DMA((3,))]

    def _parts(self, g_ref, recv_ref, sems):
        send_sems, recv_sems = sems
        x, y, c = _place()
        rows = pl.ds(self.lo, self.hi - self.lo)
        return [pltpu.make_async_remote_copy(
            src_ref=g_ref.at[2 * cx + cy, rows], dst_ref=recv_ref.at[k, rows], send_sem=send_sems.at[k],
            recv_sem=recv_sems.at[k], device_id=(cx, cy, c), device_id_type=MESH)
            for k, (cx, cy) in enumerate(_other_chips(x, y))]

    def start(self, g_ref, recv_ref, sems):
        for cp in self._parts(g_ref, recv_ref, sems):
            cp.start()

    def finish(self, g_ref, recv_ref, sems):
        sends = self._parts(g_ref, recv_ref, sems)
        for cp in sends:
            cp.wait_recv()
        for cp in sends:
            cp.wait_send()


def side_call(side, *, name):
    def body(src_ref, dst_in_ref, dst_ref, *sems):
        side.start(src_ref, dst_ref, sems)
        side.finish(src_ref, dst_ref, sems)

    return pl.pallas_call(
        body, name=name, out_shape=jax.ShapeDtypeStruct(side.dst.shape, side.dst.dtype),
        in_specs=[ANY, ANY], out_specs=ANY, scratch_shapes=side.sems(), input_output_aliases={1: 0},
    )(side.src, side.dst)


def grid_call(body, args, *, name, out_shape, grid, in_specs, out_specs, scratch_shapes, semantics, side=None):
    if side is None:
        res = pl.pallas_call(body, name=name, out_shape=out_shape, grid=grid, in_specs=in_specs, out_specs=out_specs,
                             scratch_shapes=scratch_shapes, compiler_params=_cparams(*semantics))(*args)
        return res, None
    n_in, n_out, n_scr = len(args), len(out_shape), len(scratch_shapes)

    def wrapped(*refs):
        ins, (src_ref, _) = refs[:n_in], refs[n_in:n_in + 2]
        outs, dst_ref = refs[n_in + 2:n_in + 2 + n_out], refs[n_in + 2 + n_out]
        scr, sems = refs[n_in + 3 + n_out:n_in + 3 + n_out + n_scr], refs[n_in + 3 + n_out + n_scr:]
        first = functools.reduce(jnp.logical_and, [pl.program_id(i) == 0 for i in range(len(grid))])
        last = functools.reduce(jnp.logical_and, [pl.program_id(i) == n - 1 for i, n in enumerate(grid)])

        @pl.when(first)
        def _():
            side.start(src_ref, dst_ref, sems)

        body(*ins, *outs, *scr)

        @pl.when(last)
        def _():
            side.finish(src_ref, dst_ref, sems)

    res = pl.pallas_call(
        wrapped, name=name,
        out_shape=list(out_shape) + [jax.ShapeDtypeStruct(side.dst.shape, side.dst.dtype)],
        grid=grid, in_specs=list(in_specs) + [ANY, ANY], out_specs=list(out_specs) + [ANY],
        scratch_shapes=list(scratch_shapes) + side.sems(), input_output_aliases={n_in + 1: n_out},
        compiler_params=_cparams(*(["arbitrary"] * len(grid))),
    )(*args, side.src, side.dst)
    return res[:-1], res[-1]


def sibling_swap(p, *, name):
    def body(p_ref, r_ref, send_sem, recv_sem):
        x, y, c = _place()
        cp = pltpu.make_async_remote_copy(src_ref=p_ref, dst_ref=r_ref, send_sem=send_sem, recv_sem=recv_sem,
                                          device_id=(x, y, 1 - c), device_id_type=MESH)
        cp.start()
        cp.wait()

    return pl.pallas_call(
        body, name=name, out_shape=jax.ShapeDtypeStruct(p.shape, p.dtype),
        in_specs=[ANY], out_specs=ANY,
        scratch_shapes=[pltpu.SemaphoreType.DMA, pltpu.SemaphoreType.DMA],
    )(p)


def sum_slots(own, recv, *, name):
    r_dim, c_dim = own.shape
    tr = _pick(r_dim, 256, mult=16)

    def body(o_ref, r_ref, out_ref):
        acc = o_ref[...].astype(F32)
        for k in range(3):
            acc = acc + r_ref[k].astype(F32)
        out_ref[...] = acc

    return pl.pallas_call(
        body, name=name, out_shape=jax.ShapeDtypeStruct((r_dim, c_dim), F32), grid=(r_dim // tr,),
        in_specs=[pl.BlockSpec((tr, c_dim), lambda i: (i, 0)), pl.BlockSpec((3, tr, c_dim), lambda i: (0, i, 0))],
        out_specs=pl.BlockSpec((tr, c_dim), lambda i: (i, 0)),
        compiler_params=_cparams("parallel"),
    )(own, recv)


def sum_devices(v_all, *, name):
    m_per = v_all.shape[0] // N_DEV

    def body(v_ref, o_ref):
        acc = v_ref[pl.ds(0, m_per), :]
        for d in range(1, N_DEV):
            acc = acc + v_ref[pl.ds(d * m_per, m_per), :]
        o_ref[...] = acc

    return pl.pallas_call(
        body, name=name, out_shape=jax.ShapeDtypeStruct((m_per, LANES), F32),
        in_specs=[WHOLE_VMEM], out_specs=WHOLE_VMEM,
    )(v_all)


WEIGHTS = ['ada_w', 'ada_b', 'norm_mix_g', 'norm_ffn_g', 'hy_w_in', 'hy_conv_w', 'hy_conv_b', 'hy_dt_bias', 'hy_a_log',
           'hy_d_skip', 'hy_ssm_norm_g', 'hy_w_out', 'rel_table', 'cv_w_pw1', 'cv_b_pw1', 'cv_w_dw', 'cv_b_dw', 'cv_ln_g',
           'cv_ln_b', 'cv_w_pw2', 'cv_b_pw2', 'ffn_w_gate', 'ffn_w_up', 'ffn_w_down', 'final_norm_g']
BIG = ('ada_w', 'hy_w_in', 'hy_w_out', 'cv_w_pw1', 'cv_w_pw2', 'ffn_w_gate', 'ffn_w_up', 'ffn_w_down')
SMALL_SHARDED = {'hy_conv_w': (1, 4, 3072), 'cv_b_pw1': (1, 2048), 'cv_w_dw': (1, 31, 1024), 'cv_b_dw': (1, 1024),
                 'cv_ln_g': (1, 1024), 'cv_ln_b': (1, 1024), 'cv_b_pw2': (1, 1024)}
SMALL_GRADS = {'ada_b': (2, 6144), 'norm_mix_g': (2, 1024), 'norm_ffn_g': (2, 1024), 'hy_conv_w': (1, 4, 3072),
               'hy_conv_b': (1, 3072), 'hy_dt_bias': (1, 32), 'hy_a_log': (1, 32), 'hy_d_skip': (1, 32),
               'hy_ssm_norm_g': (1, 2048), 'rel_table': (32, 48), 'cv_b_pw1': (1, 2048), 'cv_w_dw': (1, 31, 1024),
               'cv_b_dw': (1, 1024), 'cv_ln_g': (1, 1024), 'cv_ln_b': (1, 1024), 'cv_b_pw2': (1, 1024),
               'final_norm_g': (1024,), 'loss': (1,)}

PACK_LAYOUT = (('hy_in_t', 2568), ('hy_out', 768), ('pw1_t', 512), ('pw2', 256),
               ('gate_t0', 704), ('up_t0', 704), ('down0', 704), ('gate_t1', 704), ('up_t1', 704), ('down1', 704))
PACK_ROWS = 8448


def _pack_offsets(layout):
    off, out = 0, {}
    for nm, r in layout:
        out[nm] = (off, r)
        off += r
    return out


PACK_OFF = _pack_offsets(PACK_LAYOUT)
W_BATCHES = ((0, 2624), (2624, 5248), (5248, 6336), (6336, 7424), (7424, 8448))
GA_LAYOUT = PACK_LAYOUT[1:]
GA_ROWS = 5888
GA_OFF = _pack_offsets(GA_LAYOUT)
G_BATCHES = ((0, 2560), (2560, 3712), (3712, 4864), (4864, 5888))
GB_LAYOUT = PACK_LAYOUT[:1]
GB_ROWS = 2816


def pack_grads(g, layout, n_rows):
    def rows_bf16(nm):
        return g[nm]

    parts = []
    for key, r in layout:
        if key == 'hy_in_t':
            a = hy_from_cat(rows_bf16('hy_in_t'))
        elif key.startswith('gate_t'):
            a = rows_bf16('gu_t' + key[-1])[:FFN_HIDDEN]
        elif key.startswith('up_t'):
            a = rows_bf16('gu_t' + key[-1])[FFN_HIDDEN:]
        else:
            a = rows_bf16(key)
        parts.append(a.reshape(N_CHIPS, r, D))
    used = sum(r for _, r in layout)
    return jnp.concatenate(parts + [jnp.zeros((N_CHIPS, n_rows - used, D), BF16)], axis=1)


def unpack_weights(full, skip=()):
    def whole(nm):
        o, r = PACK_OFF[nm]
        return full[:, o:o + r].reshape(N_CHIPS * r, D)

    out = {"hy_out": whole('hy_out'), "pw1_t": whole('pw1_t'), "pw2": whole('pw2'),
           "gu_t": [jnp.concatenate([whole(f'gate_t{i}'), whole(f'up_t{i}')], axis=0) for i in range(2)],
           "down": [whole(f'down{i}') for i in range(2)]}
    if "hy_in_t" not in skip:
        out["hy_in_t"] = hy_to_cat(whole('hy_in_t'))
    return out


def _to_lanes(flat):
    n = flat.shape[0]
    m = -(-n // (8 * LANES)) * 8
    return jnp.pad(flat, (0, m * LANES - n)).reshape(m, LANES)


def _split(flat, shapes):
    out, off = {}, 0
    for nm, shp in shapes.items():
        n = int(np.prod(shp))
        out[nm] = flat[off:off + n].reshape(shp)
        off += n
    return out


def kernel(x, c, ada_w, ada_b, norm_mix_g, norm_ffn_g, hy_w_in, hy_conv_w, hy_conv_b, hy_dt_bias, hy_a_log, hy_d_skip, hy_ssm_norm_g, hy_w_out, rel_table, cv_w_pw1, cv_b_pw1, cv_w_dw, cv_b_dw, cv_ln_g, cv_ln_b, cv_w_pw2, cv_b_pw2, ffn_w_gate, ffn_w_up, ffn_w_down, final_norm_g, loss_target, m_ada_w, m_ada_b, m_norm_mix_g, m_norm_ffn_g, m_hy_w_in, m_hy_conv_w, m_hy_conv_b, m_hy_dt_bias, m_hy_a_log, m_hy_d_skip, m_hy_ssm_norm_g, m_hy_w_out, m_rel_table, m_cv_w_pw1, m_cv_b_pw1, m_cv_w_dw, m_cv_b_dw, m_cv_ln_g, m_cv_ln_b, m_cv_w_pw2, m_cv_b_pw2, m_ffn_w_gate, m_ffn_w_up, m_ffn_w_down, m_final_norm_g, v_ada_w, v_ada_b, v_norm_mix_g, v_norm_ffn_g, v_hy_w_in, v_hy_conv_w, v_hy_conv_b, v_hy_dt_bias, v_hy_a_log, v_hy_d_skip, v_hy_ssm_norm_g, v_hy_w_out, v_rel_table, v_cv_w_pw1, v_cv_b_pw1, v_cv_w_dw, v_cv_b_dw, v_cv_ln_g, v_cv_ln_b, v_cv_w_pw2, v_cv_b_pw2, v_ffn_w_gate, v_ffn_w_up, v_ffn_w_down, v_final_norm_g):
    args = (x, c, ada_w, ada_b, norm_mix_g, norm_ffn_g, hy_w_in, hy_conv_w, hy_conv_b, hy_dt_bias, hy_a_log, hy_d_skip, hy_ssm_norm_g, hy_w_out, rel_table, cv_w_pw1, cv_b_pw1, cv_w_dw, cv_b_dw, cv_ln_g, cv_ln_b, cv_w_pw2, cv_b_pw2, ffn_w_gate, ffn_w_up, ffn_w_down, final_norm_g, loss_target, m_ada_w, m_ada_b, m_norm_mix_g, m_norm_ffn_g, m_hy_w_in, m_hy_conv_w, m_hy_conv_b, m_hy_dt_bias, m_hy_a_log, m_hy_d_skip, m_hy_ssm_norm_g, m_hy_w_out, m_rel_table, m_cv_w_pw1, m_cv_b_pw1, m_cv_w_dw, m_cv_b_dw, m_cv_ln_g, m_cv_ln_b, m_cv_w_pw2, m_cv_b_pw2, m_ffn_w_gate, m_ffn_w_up, m_ffn_w_down, m_final_norm_g, v_ada_w, v_ada_b, v_norm_mix_g, v_norm_ffn_g, v_hy_w_in, v_hy_conv_w, v_hy_conv_b, v_hy_dt_bias, v_hy_a_log, v_hy_d_skip, v_hy_ssm_norm_g, v_hy_w_out, v_rel_table, v_cv_w_pw1, v_cv_b_pw1, v_cv_w_dw, v_cv_b_dw, v_cv_ln_g, v_cv_ln_b, v_cv_w_pw2, v_cv_b_pw2, v_ffn_w_gate, v_ffn_w_up, v_ffn_w_down, v_final_norm_g)
    x_in, c_in = args[0], args[1]
    w = dict(zip(WEIGHTS, args[2:27], strict=True))
    tgt = args[27]
    m_in = dict(zip(WEIGHTS, args[28:53], strict=True))
    v_in = dict(zip(WEIGHTS, args[53:78], strict=True))
    xi, yi, ci = _place()
    chip = 2 * xi + yi
    dev = 2 * chip + ci

    cs = rowmap(f_silu, [c_in.reshape(8, LANES)], [], [F32], name="cond_silu", tr=8)[0]
    cs_all = allgather_small(cs, name="gather_cond").reshape(N_DEV, D)
    cs16 = jnp.pad(cs_all, ((0, 8), (0, 0)))
    modpart = jnp.stack([matmul(cs16, w['ada_w'][i], mode="nn", out_dtype=F32, name=f"ada_fwd{i}")[:N_DEV]
                         for i in range(2)], axis=1)
    def rows_of(nm, i=None):
        a = w[nm][0 if i is None else i]
        return (a.T if nm in ('hy_w_in', 'cv_w_pw1', 'ffn_w_gate', 'ffn_w_up') else a).astype(BF16)

    pieces = [rows_of('hy_w_in'), rows_of('hy_w_out'), rows_of('cv_w_pw1'), rows_of('cv_w_pw2')]
    for i in range(2):
        pieces += [rows_of('ffn_w_gate', i), rows_of('ffn_w_up', i), rows_of('ffn_w_down', i)]
    n_rows = sum(p.shape[0] for p in pieces)
    pack = jnp.concatenate(pieces + [jnp.zeros((PACK_ROWS - n_rows, D), BF16)], axis=0)

    shard_names = list(SMALL_SHARDED)
    payload = jnp.concatenate([modpart.reshape(-1)] + [w[nm].reshape(-1) for nm in shard_names])
    got, full = allgather_small(_to_lanes(payload), name="gather_mod",
                                side=GatherRows(pack, lax.empty((N_CHIPS, PACK_ROWS, D), BF16), *W_BATCHES[0]))
    got = got.reshape(N_DEV, -1)[0::2]
    modparts = got[:, :modpart.size].reshape(N_CHIPS, N_DEV, 2, 1536)
    mine = lax.dynamic_index_in_dim(modparts, dev, axis=1, keepdims=False)
    mod = jnp.transpose(mine, (1, 0, 2)).reshape(2, 6 * D) + w['ada_b']
    mods = [[mod[i, j * D:(j + 1) * D].reshape(1, D) for j in range(6)] for i in range(2)]
    sp = {}
    off = modpart.size
    for nm in shard_names:
        shp = w[nm].shape
        n = int(np.prod(shp))
        parts = got[:, off:off + n].reshape((N_CHIPS,) + shp)
        sp[nm] = jnp.concatenate([parts[s] for s in range(N_CHIPS)], axis=-1)
        off += n

    o_in, r_in = PACK_OFF['hy_in_t']
    wts = {"hy_in_t": hy_to_cat(full[:, o_in:o_in + r_in].reshape(N_CHIPS * r_in, D))}
    comm = {"pack": pack, "full": full}

    sp = {"norm_mix_g": [w['norm_mix_g'][i].reshape(1, D) for i in range(2)],
          "norm_ffn_g": [w['norm_ffn_g'][i].reshape(1, D) for i in range(2)],
          "hy_conv_w": sp['hy_conv_w'][0], "hy_conv_b": w['hy_conv_b'],
          "hy_dt_bias": w['hy_dt_bias'].reshape(SSM_HEADS, 1), "hy_a_log": w['hy_a_log'].reshape(SSM_HEADS, 1),
          "hy_d_skip": w['hy_d_skip'].reshape(SSM_HEADS, 1), "hy_ssm_norm_g": w['hy_ssm_norm_g'],
          "rel_table": w['rel_table'], "cv_b_pw1": sp['cv_b_pw1'], "cv_w_dw": sp['cv_w_dw'][0], "cv_b_dw": sp['cv_b_dw'],
          "cv_ln_g": sp['cv_ln_g'], "cv_ln_b": sp['cv_ln_b'], "cv_b_pw2": sp['cv_b_pw2'],
          "final_norm_g": w['final_norm_g'].reshape(1, D)}

    loss_rows, grad_x, g, dmods = device_step(x_in[0], tgt[0], mods, wts, sp, comm)

    dmod = jnp.stack([jnp.concatenate([d.reshape(-1) for d in dmods[i]]) for i in range(2)])
    small = {'ada_b': dmod, 'norm_mix_g': jnp.stack([g[f'norm_mix_g{i}'].reshape(-1) for i in range(2)]),
             'norm_ffn_g': jnp.stack([g[f'norm_ffn_g{i}'].reshape(-1) for i in range(2)]),
             'loss': jnp.sum(loss_rows).reshape(1)}
    for nm in SMALL_GRADS:
        if nm not in small:
            small[nm] = g[nm]
    vec = _to_lanes(jnp.concatenate([small[nm].reshape(-1) for nm in SMALL_GRADS]))
    vec_all = allgather_small(vec, name="gather_small_grads")
    tot = _split(sum_devices(vec_all, name="sum_small_grads").reshape(-1), SMALL_GRADS)
    dmod_all = vec_all.reshape(N_DEV, -1)[:, :2 * 6 * D].reshape(N_DEV, 2, 6 * D)

    recv = comm["recv"]
    own_a = lax.dynamic_index_in_dim(comm["ga"], chip, axis=0, keepdims=False)
    part_a = sum_slots(own_a, recv, name="sum_chip_grads")
    red_a = rowmap(f_add, [part_a, sibling_swap(part_a, name="swap_grads")], [], [F32], name="sum_core_grads")[0]
    recv_b = comm["recv_b"]
    own_b = lax.dynamic_index_in_dim(comm["gb"], chip, axis=0, keepdims=False)
    mine_half = sum_slots(own_b, recv_b, name="sum_in_chips")
    its_half = sibling_swap(mine_half, name="swap_in")
    red_b = jnp.concatenate([jnp.where(ci == 0, mine_half, its_half), jnp.where(ci == 0, its_half, mine_half)], axis=0)

    def shard_grad(nm, i=None):
        key = {'hy_w_in': 'hy_in_t', 'hy_w_out': 'hy_out', 'cv_w_pw1': 'pw1_t', 'cv_w_pw2': 'pw2'}.get(nm)
        if key is None:
            key = {'ffn_w_gate': 'gate_t', 'ffn_w_up': 'up_t', 'ffn_w_down': 'down'}[nm] + str(i)
        if key == 'hy_in_t':
            a = red_b[:PACK_OFF[key][1]]
        else:
            o, r = GA_OFF[key]
            a = red_a[o:o + r]
        return a.T if key.endswith('_t') or key[:-1].endswith('_t') else a

    grads = {}
    grads['hy_w_in'] = shard_grad('hy_w_in')[None]
    grads['hy_w_out'] = shard_grad('hy_w_out')[None]
    grads['cv_w_pw1'] = shard_grad('cv_w_pw1')[None]
    grads['cv_w_pw2'] = shard_grad('cv_w_pw2')[None]
    for nm in ('ffn_w_gate', 'ffn_w_up', 'ffn_w_down'):
        grads[nm] = jnp.stack([shard_grad(nm, i) for i in range(2)])
    cs16 = jnp.pad(cs_all, ((0, 8), (0, 0)))
    dm_mine = lax.dynamic_slice_in_dim(dmod_all, chip * 1536, 1536, axis=2)
    dm16 = jnp.pad(dm_mine, ((0, 8), (0, 0), (0, 0)))
    grads['ada_w'] = jnp.stack([matmul(cs16, dm16[:, i], mode="tn", out_dtype=F32, name=f"ada_dw{i}") for i in range(2)])
    for nm, shp in SMALL_GRADS.items():
        if nm == 'loss':
            continue
        if nm in SMALL_SHARDED:
            n = w[nm].shape[-1]
            grads[nm] = lax.dynamic_slice_in_dim(tot[nm], chip * n, n, axis=len(shp) - 1)
        else:
            grads[nm] = tot[nm].reshape(w[nm].shape)

    delta, new_m, new_v = {}, {}, {}
    for nm in BIG:
        delta[nm], new_m[nm], new_v[nm] = adamw(w[nm], grads[nm], m_in[nm], v_in[nm], name="adamw_" + nm)
    smalls = [nm for nm in WEIGHTS if nm not in BIG]
    packed = [_to_lanes(jnp.concatenate([d[nm].reshape(-1) for nm in smalls])) for d in (w, grads, m_in, v_in)]
    res = rowmap(f_adamw, packed, [], [F32] * 3, name="adamw_small", tr=_rows_tile(packed[0].shape[0]))
    for d, r in zip((delta, new_m, new_v), res, strict=True):
        d.update(_split(r.reshape(-1), {nm: w[nm].shape for nm in smalls}))

    loss = tot['loss'].reshape(())
    return (loss, grad_x[None], *[grads[nm] for nm in WEIGHTS], *[delta[nm] for nm in WEIGHTS],
            *[new_m[nm] for nm in WEIGHTS], *[new_v[nm] for nm in WEIGHTS])
```

```python
import functools
import math

import jax
import jax.numpy as jnp
import numpy as np
from jax import lax
from jax.experimental import pallas as pl
from jax.experimental.pallas import tpu as pltpu

F32 = jnp.float32
BF16 = jnp.bfloat16
MESH = pl.DeviceIdType.MESH

D = 1024
S = 4096
EPS = 1e-6
SSM_INNER = 2048
SSM_HEADS = 32
SSM_HDIM = 64
SSM_GROUPS = 4
SSM_STATE = 128
SSM_CONVK = 4
SSM_CONV_DIM = 3072
CHUNK = 128
N_CHUNKS = S // CHUNK
ATT_HEADS = 16
ATT_HDIM = 64
ATT_PATTERNS = ((128, 1), (512, 4), (2048, 16))
ATT_BLK = 128
REL_BUCKETS = 32
REL_MAX_DIST = 2048
CONV_WIDTH = 31
FFN_HIDDEN = 2816
N_CHIPS = 4
N_DEV = 8
ADAM_LR, ADAM_B1, ADAM_B2, ADAM_EPS, ADAM_WD, ADAM_STEP = 0.001, 0.9, 0.999, 1e-08, 0.01, 10

VMEM_LIMIT_BYTES = 56 * 1024 * 1024
LANES = 128


def _cparams(*sem):
    return pltpu.CompilerParams(dimension_semantics=sem, vmem_limit_bytes=VMEM_LIMIT_BYTES)


def _pick(n, cap, mult=LANES):
    best = None
    for t in range(mult, min(n, cap) + 1, mult):
        if n % t == 0:
            best = t
    return best or n


def _dot(a, b, ca, cb):
    return lax.dot_general(a.astype(BF16), b.astype(BF16), (((ca,), (cb,)), ((), ())), preferred_element_type=F32)


@jax.custom_vjp
def mm_nt(a, b):
    return _dot(a, b, 1, 1)


def _mm_nt_fwd(a, b):
    return _dot(a, b, 1, 1), (a, b)


def _mm_nt_bwd(res, g):
    a, b = res
    return _dot(g, b, 1, 0).astype(a.dtype), _dot(g, a, 0, 0).astype(b.dtype)


mm_nt.defvjp(_mm_nt_fwd, _mm_nt_bwd)


def matmul(a, b, *, mode, out_dtype, name, n=None, b_off=0, tm_cap=1024, tn_cap=512, tk_cap=3584, side=None,
           out_t=False):
    if mode == "tn":
        k_dim, m_dim = a.shape
    else:
        m_dim, k_dim = a.shape
    n_dim = n if n is not None else (b.shape[0] if mode == "nt" else b.shape[1])
    tm = m_dim if m_dim < LANES else _pick(m_dim, tm_cap)
    tn = _pick(n_dim, tn_cap)
    tk = k_dim if k_dim < LANES else _pick(k_dim, tk_cap)
    assert m_dim % tm == 0 and n_dim % tn == 0 and k_dim % tk == 0 and b_off % tn == 0
    nk = k_dim // tk
    off = b_off // tn
    if mode == "nn":
        a_spec = pl.BlockSpec((tm, tk), lambda i, j, k: (i, k))
        b_spec = pl.BlockSpec((tk, tn), lambda i, j, k: (k, j))
        ca, cb = 1, 0
    elif mode == "nt":
        a_spec = pl.BlockSpec((tm, tk), lambda i, j, k: (i, k))
        b_spec = pl.BlockSpec((tn, tk), lambda i, j, k: (j + off, k))
        ca, cb = 1, 1
    else:
        a_spec = pl.BlockSpec((tk, tm), lambda i, j, k: (k, i))
        b_spec = pl.BlockSpec((tk, tn), lambda i, j, k: (k, j))
        ca, cb = 0, 0

    def emit(o_ref, val):
        o_ref[...] = (val.T if out_t else val).astype(o_ref.dtype)

    def body(a_ref, b_ref, o_ref, acc_ref):
        part = _dot(a_ref[...], b_ref[...], ca, cb)
        if nk == 1:
            emit(o_ref, part)
        else:
            k = pl.program_id(2)

            @pl.when(k == 0)
            def _():
                acc_ref[...] = part

            @pl.when(k > 0)
            def _():
                acc_ref[...] += part

            @pl.when(k == nk - 1)
            def _():
                emit(o_ref, acc_ref[...])

    if out_t:
        out_shape, out_spec = (n_dim, m_dim), pl.BlockSpec((tn, tm), lambda i, j, k: (j, i))
    else:
        out_shape, out_spec = (m_dim, n_dim), pl.BlockSpec((tm, tn), lambda i, j, k: (i, j))
    (out,), side_dst = grid_call(
        body, (a, b), name=name,
        out_shape=[jax.ShapeDtypeStruct(out_shape, out_dtype)],
        grid=(m_dim // tm, n_dim // tn, nk),
        in_specs=[a_spec, b_spec],
        out_specs=[out_spec],
        scratch_shapes=[pltpu.VMEM((tm, tn), F32)],
        semantics=("parallel", "parallel", "arbitrary"), side=side)
    return out if side is None else (out, side_dst)


def _f32(xs):
    return [x.astype(F32) for x in xs]


def rowmap(f, rows, consts, out_dtypes, *, name, tr=256):
    r_dim = rows[0].shape[0]
    tr = _pick(r_dim, tr, mult=8)
    assert r_dim % tr == 0
    nr, nc = len(rows), len(consts)
    outs = jax.eval_shape(lambda *xs: f(*xs), *[jax.ShapeDtypeStruct((tr, x.shape[1]), F32) for x in rows],
                          *[jax.ShapeDtypeStruct(x.shape, F32) for x in consts])

    def body(*refs):
        res = f(*_f32([r[...] for r in refs[:nr + nc]]))
        for o_ref, o in zip(refs[nr + nc:], res, strict=True):
            o_ref[...] = o.astype(o_ref.dtype)

    return pl.pallas_call(
        body, name=name,
        out_shape=[jax.ShapeDtypeStruct((r_dim, o.shape[1]), dt) for o, dt in zip(outs, out_dtypes, strict=True)],
        grid=(r_dim // tr,),
        in_specs=[pl.BlockSpec((tr, x.shape[1]), lambda i: (i, 0)) for x in rows]
        + [pl.BlockSpec(x.shape, lambda i: (0, 0)) for x in consts],
        out_specs=[pl.BlockSpec((tr, o.shape[1]), lambda i: (i, 0)) for o in outs],
        compiler_params=_cparams("parallel"),
    )(*rows, *consts)


def rowmap_bwd(f, rows, consts, cts, *, name, row_grad, row_dtypes=None, tr=256, emit=(), row_add=None,
               emit_dtype=F32):
    r_dim = rows[0].shape[0]
    tr = _pick(r_dim, tr, mult=8)
    assert r_dim % tr == 0
    nr, nc, nct = len(rows), len(consts), len(cts)
    gi = [i for i, flag in enumerate(row_grad) if flag]
    row_dtypes = row_dtypes or [F32] * len(gi)
    row_add = row_add or [None] * len(gi)
    adds = [a for a in row_add if a is not None]
    outs = jax.eval_shape(lambda *xs: f(*xs), *[jax.ShapeDtypeStruct((tr, x.shape[1]), F32) for x in rows],
                          *[jax.ShapeDtypeStruct(x.shape, F32) for x in consts])

    def body(*refs):
        ins = _f32([r[...] for r in refs[:nr + nc]])
        ct = _f32([r[...] for r in refs[nr + nc:nr + nc + nct]])
        add_refs = list(refs[nr + nc + nct:nr + nc + nct + len(adds)])
        o_refs = refs[nr + nc + nct + len(adds):]
        res, vjp = jax.vjp(f, *ins)
        grads = vjp(tuple(ct))
        for o_ref, i, a in zip(o_refs[:len(gi)], gi, row_add):
            g = grads[i] if a is None else grads[i] + add_refs.pop(0)[...].astype(F32)
            o_ref[...] = g.astype(o_ref.dtype)
        first = pl.program_id(0) == 0
        for o_ref, g in zip(o_refs[len(gi):len(gi) + nc], grads[nr:]):
            @pl.when(first)
            def _(o_ref=o_ref, g=g):
                o_ref[...] = g

            @pl.when(jnp.logical_not(first))
            def _(o_ref=o_ref, g=g):
                o_ref[...] += g
        for o_ref, i in zip(o_refs[len(gi) + nc:], emit):
            o_ref[...] = res[i].astype(o_ref.dtype)

    out_shape = ([jax.ShapeDtypeStruct(rows[i].shape, dt) for i, dt in zip(gi, row_dtypes, strict=True)]
                 + [jax.ShapeDtypeStruct(x.shape, F32) for x in consts]
                 + [jax.ShapeDtypeStruct((r_dim, outs[i].shape[1]), emit_dtype) for i in emit])
    out_specs = ([pl.BlockSpec((tr, rows[i].shape[1]), lambda i_: (i_, 0)) for i in gi]
                 + [pl.BlockSpec(x.shape, lambda i_: (0, 0)) for x in consts]
                 + [pl.BlockSpec((tr, outs[i].shape[1]), lambda i_: (i_, 0)) for i in emit])
    res = pl.pallas_call(
        body, name=name,
        out_shape=out_shape,
        grid=(r_dim // tr,),
        in_specs=[pl.BlockSpec((tr, x.shape[1]), lambda i: (i, 0)) for x in rows]
        + [pl.BlockSpec(x.shape, lambda i: (0, 0)) for x in consts]
        + [pl.BlockSpec((tr, x.shape[1]), lambda i: (i, 0)) for x in list(cts) + adds],
        out_specs=out_specs,
        compiler_params=_cparams("arbitrary"),
    )(*rows, *consts, *cts, *adds)
    return res[:len(gi)], res[len(gi):len(gi) + nc], res[len(gi) + nc:]


def matmul_swiglu(hgu, w, *, name, tm=512, tk_cap=1536):
    m_dim, hid = hgu.shape[0], hgu.shape[1] // 2
    n_dim = w.shape[1]
    tk = _pick(hid, tk_cap)
    nk = hid // tk
    assert m_dim % tm == 0 and hid % tk == 0

    def body(g_ref, u_ref, w_ref, o_ref, acc_ref):
        gate, up = g_ref[...].astype(F32), u_ref[...].astype(F32)
        part = _dot(_silu(gate) * up, w_ref[...], 1, 0)
        k = pl.program_id(1)

        @pl.when(k == 0)
        def _():
            acc_ref[...] = part

        @pl.when(k > 0)
        def _():
            acc_ref[...] += part

        @pl.when(k == nk - 1)
        def _():
            o_ref[...] = acc_ref[...]

    return pl.pallas_call(
        body, name=name, out_shape=jax.ShapeDtypeStruct((m_dim, n_dim), F32), grid=(m_dim // tm, nk),
        in_specs=[pl.BlockSpec((tm, tk), lambda i, k: (i, k)), pl.BlockSpec((tm, tk), lambda i, k: (i, k + nk)),
                  pl.BlockSpec((tk, n_dim), lambda i, k: (k, 0))],
        out_specs=pl.BlockSpec((tm, n_dim), lambda i, k: (i, 0)),
        scratch_shapes=[pltpu.VMEM((tm, n_dim), F32)],
        compiler_params=_cparams("parallel", "arbitrary"),
    )(hgu, hgu, w)


def transpose(a, *, name, out_dtype=BF16, tr=512, tc=512):
    r_dim, c_dim = a.shape
    tr, tc = _pick(r_dim, tr), _pick(c_dim, tc)

    def body(a_ref, o_ref):
        o_ref[...] = a_ref[...].astype(F32).T.astype(o_ref.dtype)

    return pl.pallas_call(
        body, name=name, out_shape=jax.ShapeDtypeStruct((c_dim, r_dim), out_dtype),
        grid=(r_dim // tr, c_dim // tc),
        in_specs=[pl.BlockSpec((tr, tc), lambda i, j: (i, j))],
        out_specs=pl.BlockSpec((tc, tr), lambda i, j: (j, i)),
        compiler_params=_cparams("parallel", "parallel"),
    )(a)


CONV_HALO = 32
CONV_ROWS = 256


def conv_fwd(x, w, b, *, name, cb=256, chunk_rows=CONV_ROWS):
    s_dim, c_dim = x.shape
    taps = w.shape[0]
    assert taps - 1 <= CONV_HALO and s_dim % chunk_rows == 0 and c_dim % cb == 0
    n_chunks = s_dim // chunk_rows
    ext = chunk_rows + CONV_HALO

    def body(x_ref, w_ref, b_ref, o_ref, xp_ref):
        xp_ref[pl.ds(0, CONV_HALO), :] = jnp.zeros((CONV_HALO, cb), F32)
        xp_ref[pl.ds(CONV_HALO, s_dim), :] = x_ref[...].astype(F32)
        wv = w_ref[...].astype(F32)
        bv = b_ref[...].astype(F32)

        def chunk(t, carry):
            base = pl.multiple_of(t * chunk_rows, chunk_rows)
            xe = xp_ref[pl.ds(base, ext), :]
            acc = jnp.broadcast_to(bv, (chunk_rows, cb))
            for j in range(taps):
                sh = xe if j == 0 else pltpu.roll(xe, shift=j, axis=0)
                acc = acc + wv[taps - 1 - j:taps - j, :] * sh[CONV_HALO:, :]
            o_ref[pl.ds(base, chunk_rows), :] = acc
            return carry

        lax.fori_loop(0, n_chunks, chunk, 0)

    return pl.pallas_call(
        body, name=name,
        out_shape=jax.ShapeDtypeStruct((s_dim, c_dim), F32),
        grid=(c_dim // cb,),
        in_specs=[pl.BlockSpec((s_dim, cb), lambda i: (0, i)), pl.BlockSpec((taps, cb), lambda i: (0, i)),
                  pl.BlockSpec((1, cb), lambda i: (0, i))],
        out_specs=pl.BlockSpec((s_dim, cb), lambda i: (0, i)),
        scratch_shapes=[pltpu.VMEM((s_dim + CONV_HALO, cb), F32)],
        compiler_params=_cparams("parallel"),
    )(x, w, b)


def conv_bwd(x, w, g, *, name, cb=256, chunk_rows=CONV_ROWS, dx_dtype=F32):
    s_dim, c_dim = x.shape
    taps = w.shape[0]
    n_chunks = s_dim // chunk_rows
    ext = chunk_rows + CONV_HALO

    def rows8(a):
        return jnp.sum(a.reshape(chunk_rows // 8, 8, cb), axis=0)

    def body(x_ref, w_ref, g_ref, dx_ref, dw_ref, db_ref, xp_ref, gp_ref, acc_ref):
        xp_ref[pl.ds(0, CONV_HALO), :] = jnp.zeros((CONV_HALO, cb), F32)
        xp_ref[pl.ds(CONV_HALO, s_dim), :] = x_ref[...].astype(F32)
        gp_ref[pl.ds(0, s_dim), :] = g_ref[...].astype(F32)
        gp_ref[pl.ds(s_dim, CONV_HALO), :] = jnp.zeros((CONV_HALO, cb), F32)
        acc_ref[...] = jnp.zeros_like(acc_ref)
        wv = w_ref[...].astype(F32)

        def chunk(t, carry):
            base = pl.multiple_of(t * chunk_rows, chunk_rows)
            xe = xp_ref[pl.ds(base, ext), :]
            ge = gp_ref[pl.ds(base, ext), :]
            gc = ge[:chunk_rows, :]
            dx = jnp.zeros((chunk_rows, cb), F32)
            for j in range(taps):
                xs = xe if j == 0 else pltpu.roll(xe, shift=j, axis=0)
                gs = ge if j == 0 else pltpu.roll(ge, shift=ext - j, axis=0)
                k = taps - 1 - j
                dx = dx + wv[k:k + 1, :] * gs[:chunk_rows, :]
                acc_ref[8 * k:8 * k + 8, :] += rows8(gc * xs[CONV_HALO:, :])
            acc_ref[8 * taps:8 * taps + 8, :] += rows8(gc)
            dx_ref[pl.ds(base, chunk_rows), :] = dx.astype(dx_ref.dtype)
            return carry

        lax.fori_loop(0, n_chunks, chunk, 0)
        sums = jnp.sum(acc_ref[...].reshape(taps + 1, 8, cb), axis=1)
        dw_ref[...] = sums[0:taps, :]
        db_ref[...] = sums[taps:taps + 1, :]

    return pl.pallas_call(
        body, name=name,
        out_shape=[jax.ShapeDtypeStruct((s_dim, c_dim), dx_dtype), jax.ShapeDtypeStruct((taps, c_dim), F32),
                   jax.ShapeDtypeStruct((1, c_dim), F32)],
        grid=(c_dim // cb,),
        in_specs=[pl.BlockSpec((s_dim, cb), lambda i: (0, i)), pl.BlockSpec((taps, cb), lambda i: (0, i)),
                  pl.BlockSpec((s_dim, cb), lambda i: (0, i))],
        out_specs=[pl.BlockSpec((s_dim, cb), lambda i: (0, i)), pl.BlockSpec((taps, cb), lambda i: (0, i)),
                   pl.BlockSpec((1, cb), lambda i: (0, i))],
        scratch_shapes=[pltpu.VMEM((s_dim + CONV_HALO, cb), F32), pltpu.VMEM((s_dim + CONV_HALO, cb), F32),
                        pltpu.VMEM((8 * (taps + 1), cb), F32)],
        compiler_params=_cparams("parallel"),
    )(x, w, g)


def _softplus(x):
    return jnp.maximum(x, 0.0) + jnp.log(1.0 + jnp.exp(-jnp.abs(x)))


HEADS_PER_GROUP = SSM_HEADS // SSM_GROUPS


def _bdot(a, b, ca, cb):
    return lax.dot_general(a.astype(BF16), b.astype(BF16), (((ca,), (cb,)), ((0,), (0,))), preferred_element_type=F32)


@jax.custom_vjp
def bmm(a, b):
    return _bdot(a, b, 2, 1)


def _bmm_fwd(a, b):
    return _bdot(a, b, 2, 1), (a, b)


def _bmm_bwd(res, g):
    a, b = res
    return _bdot(g, b, 2, 2).astype(a.dtype), _bdot(a, g, 1, 1).astype(b.dtype)


bmm.defvjp(_bmm_fwd, _bmm_bwd)


@jax.custom_vjp
def bmm_nt(a, b):
    return _bdot(a, b, 2, 2)


def _bmm_nt_fwd(a, b):
    return _bdot(a, b, 2, 2), (a, b)


def _bmm_nt_bwd(res, g):
    a, b = res
    return _bdot(g, b, 2, 1).astype(a.dtype), _bdot(g, a, 1, 1).astype(b.dtype)


bmm_nt.defvjp(_bmm_nt_fwd, _bmm_nt_bwd)


@jax.custom_vjp
def bmm_tn(a, b):
    return _bdot(a, b, 1, 1)


def _bmm_tn_fwd(a, b):
    return _bdot(a, b, 1, 1), (a, b)


def _bmm_tn_bwd(res, g):
    a, b = res
    return _bdot(b, g, 2, 2).astype(a.dtype), _bdot(a, g, 2, 1).astype(b.dtype)


bmm_tn.defvjp(_bmm_tn_fwd, _bmm_tn_bwd)


ATT_PAIRS = ATT_HEADS // 2
PAIR_W = 2 * ATT_HDIM


def att_pairs(q, kp, kc, vp, vc, bias, has_prev):
    t, b, w = q.shape
    i = lax.broadcasted_iota(jnp.int32, (1, b, b), 1)
    j = lax.broadcasted_iota(jnp.int32, (1, b, b), 2)
    first = lax.broadcasted_iota(jnp.int32, (1, 1, w), 2) < ATT_HDIM
    scale = ATT_HDIM ** -0.5
    outs, lses = [], []
    for ab in range(2):
        qh = jnp.where(first if ab == 0 else jnp.logical_not(first), q, 0.0)
        sp = jnp.where(jnp.logical_and(j >= i, has_prev), bmm_nt(qh, kp) * scale + bias[:, ab, 0], -1e30)
        sc = jnp.where(j <= i, bmm_nt(qh, kc) * scale + bias[:, ab, 1], -1e30)
        m = lax.stop_gradient(jnp.maximum(jnp.max(sp, axis=2, keepdims=True), jnp.max(sc, axis=2, keepdims=True)))
        pp, pc = jnp.exp(sp - m), jnp.exp(sc - m)
        l = jnp.sum(pp, axis=2, keepdims=True) + jnp.sum(pc, axis=2, keepdims=True)
        outs.append(bmm(pp / l, vp) + bmm(pc / l, vc))
        lses.append(jnp.broadcast_to(m + jnp.log(l), (t, b, w)))
    return jnp.where(first, outs[0], outs[1]), jnp.where(first, lses[0], lses[1])


def _pair_tiles(ref):
    return jnp.stack([ref[:, PAIR_W * t:PAIR_W * (t + 1)] for t in range(ATT_PAIRS)])


def _store_pair_tiles(ref, val):
    for t in range(ATT_PAIRS):
        ref[:, PAIR_W * t:PAIR_W * (t + 1)] = val[t].astype(ref.dtype)


def pair_bias(bias):
    return bias.reshape(ATT_PAIRS, 2, 2, ATT_BLK, ATT_BLK)


def att2_fwd(q, k, v, bias, nb, cols, *, name, side=None):
    n_blocks = S // ATT_BLK
    qc, kc, vc = cols

    def body(q_ref, k_ref, v_ref, b_ref, o_ref, l_ref, kprev, vprev):
        blk = pl.program_id(0)

        @pl.when(blk == 0)
        def _():
            kprev[...] = jnp.zeros_like(kprev)
            vprev[...] = jnp.zeros_like(vprev)

        k3, v3 = _pair_tiles(k_ref), _pair_tiles(v_ref)
        o, lse = att_pairs(_pair_tiles(q_ref), kprev[...], k3, vprev[...], v3, b_ref[...], (blk % nb) != 0)
        _store_pair_tiles(o_ref, o)
        _store_pair_tiles(l_ref, lse)
        kprev[...] = k3
        vprev[...] = v3

    def spec(c):
        return pl.BlockSpec((ATT_BLK, D), lambda b: (b, c))

    return grid_call(
        body, (q, k, v, bias), name=name,
        out_shape=[jax.ShapeDtypeStruct((S, D), BF16), jax.ShapeDtypeStruct((S, D), F32)], grid=(n_blocks,),
        in_specs=[spec(qc), spec(kc), spec(vc), pl.BlockSpec(bias.shape, lambda b: (0, 0, 0, 0, 0))],
        out_specs=[spec(0), spec(0)],
        scratch_shapes=[pltpu.VMEM((ATT_PAIRS, ATT_BLK, PAIR_W), BF16), pltpu.VMEM((ATT_PAIRS, ATT_BLK, PAIR_W), BF16)],
        semantics=("arbitrary",), side=side)


def att2_bwd(q, k, v, bias, do, dlse, nb, cols, *, name, side=None):
    n_blocks = S // ATT_BLK
    qc, kc, vc = cols

    def body(q_ref, k_ref, v_ref, b_ref, do_ref, dl_ref, dq_ref, dk_ref, dv_ref, db_ref, kprev, vprev, dk_own, dv_own):
        blk = pl.program_id(0)

        @pl.when(blk == 0)
        def _():
            for r in (kprev, vprev, dk_own, dv_own, db_ref):
                r[...] = jnp.zeros_like(r)

        @pl.when(blk < n_blocks)
        def _():
            k3, v3 = _pair_tiles(k_ref), _pair_tiles(v_ref)
            ins = _f32([_pair_tiles(q_ref), kprev[...], k3, vprev[...], v3]) + [b_ref[...]]
            _, vjp = jax.vjp(functools.partial(att_pairs, has_prev=(blk % nb) != 0), *ins)
            dq, dkp, dkc, dvp, dvc, db = vjp(tuple(_f32([_pair_tiles(do_ref), _pair_tiles(dl_ref)])))
            _store_pair_tiles(dq_ref, dq)
            _store_pair_tiles(dk_ref, dk_own[...] + dkp)
            _store_pair_tiles(dv_ref, dv_own[...] + dvp)
            dk_own[...] = dkc
            dv_own[...] = dvc
            db_ref[...] += db
            kprev[...] = k3
            vprev[...] = v3

        @pl.when(blk == n_blocks)
        def _():
            _store_pair_tiles(dk_ref, dk_own[...])
            _store_pair_tiles(dv_ref, dv_own[...])

    def spec(c):
        return pl.BlockSpec((ATT_BLK, D), lambda b: (jnp.minimum(b, n_blocks - 1), c))

    late = pl.BlockSpec((ATT_BLK, D), lambda b: (jnp.maximum(b - 1, 0), 0))
    bspec = pl.BlockSpec(bias.shape, lambda b: (0, 0, 0, 0, 0))
    tile_f32 = pltpu.VMEM((ATT_PAIRS, ATT_BLK, PAIR_W), F32)
    tile_bf16 = pltpu.VMEM((ATT_PAIRS, ATT_BLK, PAIR_W), BF16)
    return grid_call(
        body, (q, k, v, bias, do, dlse), name=name,
        out_shape=[jax.ShapeDtypeStruct((S, D), BF16), jax.ShapeDtypeStruct((S, D), BF16),
                   jax.ShapeDtypeStruct((S, D), BF16), jax.ShapeDtypeStruct(bias.shape, F32)],
        grid=(n_blocks + 1,),
        in_specs=[spec(qc), spec(kc), spec(vc), bspec, spec(0), spec(0)],
        out_specs=[spec(0), late, late, bspec],
        scratch_shapes=[tile_bf16, tile_bf16, tile_f32, tile_f32],
        semantics=("arbitrary",), side=side)


def regroup(a, dil, inverse=False):
    if dil == 1:
        return a
    c_dim = a.shape[1]
    shape = (dil, S // dil, c_dim) if inverse else (S // dil, dil, c_dim)
    return jnp.transpose(a.reshape(shape), (1, 0, 2)).reshape(S, c_dim)


SSD_PAIRS = SSM_HEADS // 2
PAIRS_PER_GROUP = SSD_PAIRS // SSM_GROUPS
GROUP_W = HEADS_PER_GROUP * SSM_HDIM


SSD_GROUPS_PER_STEP = 4
SSD_STEP_PAIRS = PAIRS_PER_GROUP * SSD_GROUPS_PER_STEP
SSD_STEP_W = GROUP_W * SSD_GROUPS_PER_STEP


def ssd_pairs(x, dtraw, dt_bias, a_log, dskip, bms, cms, prev):
    t, q, w = x.shape
    n = bms[0].shape[1]
    per = t // len(bms)

    def by_pair(mats):
        return jnp.concatenate([jnp.broadcast_to(m[None], (per,) + m.shape) for m in mats], axis=0)
    li = lax.broadcasted_iota(jnp.int32, (1, q, q), 1)
    si = lax.broadcasted_iota(jnp.int32, (1, q, q), 2)
    first_lane = lax.broadcasted_iota(jnp.int32, (1, 1, w), 2) < SSM_HDIM
    first_row = lax.broadcasted_iota(jnp.int32, (1, w, 1), 1) < SSM_HDIM

    def to_col(row):
        return jnp.sum(jnp.where(li == si, jnp.broadcast_to(row, (t, q, q)), 0.0), axis=2, keepdims=True)

    def lanes(a0, a1):
        return jnp.where(first_lane, a0, a1)

    dt_col, acs_col, total, lmat = [], [], [], []
    for ab in range(2):
        dt_row = _softplus(dtraw[ab] + dt_bias[ab])
        a_row = dt_row * (-jnp.exp(a_log[ab]))
        a_col = to_col(a_row)
        acs_c = jnp.sum(jnp.where(si <= li, jnp.broadcast_to(a_row, (t, q, q)), 0.0), axis=2, keepdims=True)
        acs_r = jnp.sum(jnp.where(li <= si, jnp.broadcast_to(a_col, (t, q, q)), 0.0), axis=1, keepdims=True)
        dt_col.append(to_col(dt_row))
        acs_col.append(acs_c)
        total.append(jnp.sum(a_row, axis=2, keepdims=True))
        lmat.append(jnp.exp(jnp.where(li >= si, acs_c - acs_r, -1e30)))
    cb = by_pair([mm_nt(c_, b_) for c_, b_ in zip(cms, bms, strict=True)])
    bmb, cmb = by_pair(bms), by_pair(cms)
    xdt = x * lanes(dt_col[0], dt_col[1])
    y = lanes(bmm(cb * lmat[0], xdt), bmm(cb * lmat[1], xdt))
    y = y + bmm_nt(cmb, prev) * lanes(jnp.exp(acs_col[0]), jnp.exp(acs_col[1]))
    y = y + lanes(dskip[0], dskip[1]) * x
    state = bmm_tn(xdt * lanes(jnp.exp(total[0] - acs_col[0]), jnp.exp(total[1] - acs_col[1])), bmb)
    return y, jnp.where(first_row, jnp.exp(total[0]), jnp.exp(total[1])) * prev + state


def _group_tiles(ref):
    return jnp.stack([ref[:, PAIR_W * t:PAIR_W * (t + 1)] for t in range(SSD_STEP_PAIRS)])


def _store_group_tiles(ref, val):
    for t in range(SSD_STEP_PAIRS):
        ref[:, PAIR_W * t:PAIR_W * (t + 1)] = val[t]


def _bc_groups(ref):
    return tuple(ref[:, SSM_STATE * i:SSM_STATE * (i + 1)] for i in range(SSD_GROUPS_PER_STEP))


def _by_pair(a):
    return jnp.transpose(a.reshape(SSD_PAIRS, 2, 1, -1), (1, 0, 2, 3))


def _by_head(a):
    return jnp.transpose(a, (1, 0, 2, 3)).reshape(SSM_HEADS, -1)


def _ssd2_specs(chunk_of):
    tp = SSD_STEP_PAIRS
    xspec = pl.BlockSpec((CHUNK, SSD_STEP_W), lambda g, c: (chunk_of(c), g))
    tspec = pl.BlockSpec((2, tp, 1, CHUNK), lambda g, c: (0, g, 0, chunk_of(c)))
    hp = pl.BlockSpec((2, tp, 1, 1), lambda g, c: (0, g, 0, 0))
    gspec = pl.BlockSpec((CHUNK, SSD_GROUPS_PER_STEP * SSM_STATE), lambda g, c: (chunk_of(c), g))
    sspec = pl.BlockSpec((1, tp, PAIR_W, SSM_STATE), lambda g, c: (chunk_of(c), g, 0, 0))
    return xspec, tspec, hp, gspec, sspec


def ssd2_fwd(xs, dtraw_t, dt_bias, a_log, dskip, bm, cm, side=None):
    def body(x_ref, dt_ref, dtb_ref, al_ref, dk_ref, bm_ref, cm_ref, y_ref, prev_ref, state_ref):
        @pl.when(pl.program_id(1) == 0)
        def _():
            state_ref[...] = jnp.zeros_like(state_ref)

        prev = state_ref[...]
        prev_ref[0] = prev
        y, nxt = ssd_pairs(_group_tiles(x_ref), dt_ref[...], dtb_ref[...], al_ref[...], dk_ref[...], _bc_groups(bm_ref),
                           _bc_groups(cm_ref), prev)
        _store_group_tiles(y_ref, y)
        state_ref[...] = nxt

    xspec, tspec, hp, gspec, sspec = _ssd2_specs(lambda c: c)
    return grid_call(
        body, (xs, _by_pair(dtraw_t), _by_pair(dt_bias), _by_pair(a_log), _by_pair(dskip), bm, cm), name="ssd_fwd",
        out_shape=[jax.ShapeDtypeStruct((S, SSM_INNER), F32),
                   jax.ShapeDtypeStruct((N_CHUNKS, SSD_PAIRS, PAIR_W, SSM_STATE), F32)],
        grid=(SSM_GROUPS // SSD_GROUPS_PER_STEP, N_CHUNKS), in_specs=[xspec, tspec, hp, hp, hp, gspec, gspec],
        out_specs=[xspec, sspec],
        scratch_shapes=[pltpu.VMEM((SSD_STEP_PAIRS, PAIR_W, SSM_STATE), F32)],
        semantics=("parallel", "arbitrary"), side=side)


def ssd2_bwd(xs, dtraw_t, dt_bias, a_log, dskip, bm, cm, prev_all, dy, side=None):
    def body(x_ref, dt_ref, dtb_ref, al_ref, dk_ref, bm_ref, cm_ref, prev_ref, dy_ref,
             dx_ref, ddt_ref, ddtb_ref, dal_ref, ddk_ref, dbm_ref, dcm_ref, dstate_ref):
        @pl.when(pl.program_id(1) == 0)
        def _():
            for r in (dstate_ref, ddtb_ref, dal_ref, ddk_ref):
                r[...] = jnp.zeros_like(r)

        _, vjp = jax.vjp(ssd_pairs, _group_tiles(x_ref), dt_ref[...], dtb_ref[...], al_ref[...], dk_ref[...],
                         _bc_groups(bm_ref), _bc_groups(cm_ref), prev_ref[0])
        dx, ddt, ddtb, dal, ddk, dbms, dcms, dprev = vjp((_group_tiles(dy_ref), dstate_ref[...]))
        _store_group_tiles(dx_ref, dx)
        ddt_ref[...] = ddt
        ddtb_ref[...] += ddtb
        dal_ref[...] += dal
        ddk_ref[...] += ddk
        for i in range(SSD_GROUPS_PER_STEP):
            dbm_ref[:, SSM_STATE * i:SSM_STATE * (i + 1)] = dbms[i]
            dcm_ref[:, SSM_STATE * i:SSM_STATE * (i + 1)] = dcms[i]
        dstate_ref[...] = dprev

    xspec, tspec, hp, gspec, sspec = _ssd2_specs(lambda c: N_CHUNKS - 1 - c)
    par = jax.ShapeDtypeStruct((2, SSD_PAIRS, 1, 1), F32)
    res, side_dst = grid_call(
        body, (xs, _by_pair(dtraw_t), _by_pair(dt_bias), _by_pair(a_log), _by_pair(dskip), bm, cm, prev_all, dy),
        name="ssd_bwd",
        out_shape=[jax.ShapeDtypeStruct((S, SSM_INNER), F32), jax.ShapeDtypeStruct((2, SSD_PAIRS, 1, S), F32), par, par, par,
                   jax.ShapeDtypeStruct((S, SSM_GROUPS * SSM_STATE), F32),
                   jax.ShapeDtypeStruct((S, SSM_GROUPS * SSM_STATE), F32)],
        grid=(SSM_GROUPS // SSD_GROUPS_PER_STEP, N_CHUNKS), in_specs=[xspec, tspec, hp, hp, hp, gspec, gspec, sspec, xspec],
        out_specs=[xspec, tspec, hp, hp, hp, gspec, gspec],
        scratch_shapes=[pltpu.VMEM((SSD_STEP_PAIRS, PAIR_W, SSM_STATE), F32)],
        semantics=("parallel", "arbitrary"), side=side)
    return [res[0]] + [_by_head(r) for r in res[1:5]] + list(res[5:]), side_dst


def _silu(x):
    return x * jax.nn.sigmoid(x)


def _rms(x):
    return x * lax.rsqrt(jnp.mean(x * x, -1, keepdims=True) + EPS)


def f_normmod(x, g, sc, sh):
    return (_rms(x) * g * (1.0 + sc) + sh,)


def f_resid(x, mix, gate):
    return (x + gate * mix,)


def f_resid_bias(x, mix, gate, b):
    return (x + gate * (mix + b),)


def f_swiglu(hgu):
    return (_silu(hgu[:, :FFN_HIDDEN]) * hgu[:, FFN_HIDDEN:],)


def f_silu(x):
    return (_silu(x),)


def f_silu_xbc(x):
    y = _silu(x)
    n_b = SSM_GROUPS * SSM_STATE
    return y[:, :SSM_INNER], y[:, SSM_INNER:SSM_INNER + n_b], y[:, SSM_INNER + n_b:]


def f_gated_norm(y, z, g):
    return (_rms(y * _silu(z)) * g,)


def f_glu(y, b):
    y = y + b
    return (y[:, :D] * jax.nn.sigmoid(y[:, D:]),)


def f_ln_silu(u, g, b):
    mu = jnp.mean(u, -1, keepdims=True)
    var = jnp.mean(jnp.square(u - mu), -1, keepdims=True)
    return (_silu((u - mu) * lax.rsqrt(var + EPS) * g + b),)


def f_combine(o1, o2, o3, l1, l2, l3):
    m = lax.stop_gradient(jnp.maximum(jnp.maximum(l1, l2), l3))
    e1, e2, e3 = jnp.exp(l1 - m), jnp.exp(l2 - m), jnp.exp(l3 - m)
    return ((e1 * o1 + e2 * o2 + e3 * o3) / (e1 + e2 + e3),)


def f_head(x, tgt, g):
    return (0.5 * jnp.mean(jnp.square(_rms(x) * g - tgt), -1, keepdims=True),)


def f_sum3(a, b, c):
    return (a + b + c,)


def f_add(a, b):
    return (a + b,)


def f_adamw(w, g, m, v):
    m = ADAM_B1 * m + (1.0 - ADAM_B1) * g
    v = ADAM_B2 * v + (1.0 - ADAM_B2) * jnp.square(g)
    m_hat = m / (1.0 - ADAM_B1 ** ADAM_STEP)
    v_hat = v / (1.0 - ADAM_B2 ** ADAM_STEP)
    return -ADAM_LR * (m_hat / (jnp.sqrt(v_hat) + ADAM_EPS) + ADAM_WD * w), m, v


def _rows_tile(r, cap=256):
    return _pick(r, cap, mult=8)


def adamw(w, g, m, v, *, name):
    l_dim, r_dim, c_dim = w.shape
    tr = _rows_tile(r_dim, cap=128)

    def body(w_ref, g_ref, m_ref, v_ref, d_ref, mo_ref, vo_ref):
        d_ref[...], mo_ref[...], vo_ref[...] = f_adamw(w_ref[...], g_ref[...], m_ref[...], v_ref[...])

    spec = pl.BlockSpec((1, tr, c_dim), lambda l, i: (l, i, 0))
    return pl.pallas_call(
        body, name=name, out_shape=[jax.ShapeDtypeStruct(w.shape, F32)] * 3, grid=(l_dim, r_dim // tr),
        in_specs=[spec] * 4, out_specs=[spec] * 3, compiler_params=_cparams("parallel", "parallel"),
    )(w, g, m, v)


def _t5_bucket(dist):
    max_exact = REL_BUCKETS // 2
    n = jnp.maximum(dist, 1).astype(F32)
    large = max_exact + jnp.log(n / max_exact) / math.log(REL_MAX_DIST / max_exact) * (REL_BUCKETS - max_exact)
    large = jnp.minimum(large.astype(jnp.int32), REL_BUCKETS - 1)
    return jnp.where(dist < max_exact, dist, large)


def _att_buckets(dil):
    i = jnp.arange(ATT_BLK)[:, None]
    j = jnp.arange(2 * ATT_BLK)[None, :]
    bkt = _t5_bucket(jnp.maximum(ATT_BLK + i - j, 0) * dil)
    return jnp.transpose(bkt.reshape(ATT_BLK, 2, ATT_BLK), (1, 0, 2))


def att_bias(rel_table, p, dil):
    tab = rel_table[:, p * ATT_HEADS:(p + 1) * ATT_HEADS]
    onehot = (jnp.arange(REL_BUCKETS)[:, None] == _att_buckets(dil).reshape(1, -1)).astype(F32)
    bias = lax.dot_general(tab, onehot, (((0,), (0,)), ((), ())), precision=lax.Precision.HIGHEST)
    return bias.reshape(ATT_HEADS, 2, ATT_BLK, ATT_BLK)


def att_bias_grad(dbias, dil, *, name):
    onehot = (_att_buckets(dil).reshape(-1, 1) == jnp.arange(LANES)[None, :]).astype(BF16)
    dtab = matmul(dbias.reshape(ATT_HEADS, -1), onehot, mode="nn", out_dtype=F32, name=name, tk_cap=2048)
    return dtab[:, :REL_BUCKETS].T


HY_Z, HY_XBC, HY_DT, HY_Q, HY_K, HY_V = 2048, 3072, 32, 3072, 1024, 1024
HY_IN = HY_Z + HY_XBC + HY_DT + HY_Q + HY_K + HY_V
OFF_Z, OFF_XBC, OFF_Q, OFF_KV, OFF_DT = 0, 2048, 5120, 8192, 10240
HY_CAT = OFF_DT + LANES
DT_PAD = LANES


def hy_to_cat(w):
    z, xbc, dt, qkv = w[:2048], w[2048:5120], w[5120:5152], w[5152:]
    return jnp.concatenate([z, xbc, qkv, dt, jnp.zeros((DT_PAD - HY_DT,) + w.shape[1:], w.dtype)], axis=0)


def hy_from_cat(w, axis=0):
    part = lambda a, b: lax.slice_in_dim(w, a, b, axis=axis)
    return jnp.concatenate([part(0, 5120), part(OFF_DT, OFF_DT + HY_DT), part(5120, OFF_DT)], axis=axis)


def device_step(x, tgt, mods, wts, sp, comm=None):
    g = {}
    dmods = [[None] * 6 for _ in range(2)]
    wts = dict(wts)

    def wgrad(tokens_d, tokens_n, nm):
        return matmul(transpose(tokens_d, name=nm + "_t"), tokens_n, mode="nn", out_dtype=BF16, name=nm, out_t=True,
                      tk_cap=2048)

    def w_side(i):
        return None if comm is None else GatherRows(comm["pack"], comm["full"], *W_BATCHES[i])

    def g_side(i):
        return None if comm is None else ScatterRows(comm["ga"], comm["recv"], *G_BATCHES[i])

    def normmod(xi, gain, sc, sh, nm):
        return rowmap(f_normmod, [xi], [gain, sc, sh], [BF16], name=nm)[0]

    def ffn_fwd(xi, i, gate, nm):
        h = normmod(xi, sp["norm_ffn_g"][i], mods[i][4], mods[i][3], nm + "_norm")
        hgu = matmul(h, wts["gu_t"][i], mode="nt", out_dtype=BF16, name=nm + "_gu")
        out = matmul_swiglu(hgu, wts["down"][i], name=nm + "_down")
        xo = rowmap(f_resid, [xi, out], [gate], [F32], name=nm + "_res")[0]
        return xo, (h, hgu, out)

    def ffn_bwd(dres, xi, i, saved, nm):
        h, hgu, out = saved
        (dout,), (dgate,), _ = rowmap_bwd(f_resid, [xi, out], [mods[i][5]], [dres], name=nm + "_res_b",
                                          row_grad=[False, True], row_dtypes=[BF16])
        dmods[i][5] = dgate
        dact = matmul(dout, wts["down"][i], mode="nt", out_dtype=BF16, name=nm + "_down_dx")
        (dhgu,), _, (act,) = rowmap_bwd(f_swiglu, [hgu], [], [dact], name=nm + "_act_b", row_grad=[True],
                                        row_dtypes=[BF16], tr=128, emit=(0,), emit_dtype=BF16)
        g[f"down{i}"] = wgrad(dout, act, nm + "_down_dw")
        g[f"gu_t{i}"] = wgrad(h, dhgu, nm + "_gu_dw")
        dh = matmul(dhgu, wts["gu_t"][i], mode="nn", out_dtype=F32, name=nm + "_gu_dx")
        (dres,), (dg_, dsc, dsh), _ = rowmap_bwd(f_normmod, [xi], [sp["norm_ffn_g"][i], mods[i][4], mods[i][3]], [dh],
                                                 name=nm + "_norm_b", row_grad=[True], row_add=[dres])
        g[f"norm_ffn_g{i}"] = dg_
        dmods[i][4], dmods[i][3] = dsc, dsh
        return dres

    h0 = normmod(x, sp["norm_mix_g"][0], mods[0][1], mods[0][0], "l0_norm")
    w_in = wts["hy_in_t"]
    z = matmul(h0, w_in, mode="nt", out_dtype=BF16, name="hy_z", n=HY_Z, b_off=OFF_Z)
    xbc_raw = matmul(h0, w_in, mode="nt", out_dtype=BF16, name="hy_xbc", n=HY_XBC, b_off=OFF_XBC)
    q = matmul(h0, w_in, mode="nt", out_dtype=BF16, name="hy_q", n=HY_Q, b_off=OFF_Q)
    kv = matmul(h0, w_in, mode="nt", out_dtype=BF16, name="hy_kv", n=HY_K + HY_V, b_off=OFF_KV)
    dtr = matmul(h0, w_in, mode="nt", out_dtype=F32, name="hy_dt", n=DT_PAD, b_off=OFF_DT)
    xbc_pre = conv_fwd(xbc_raw, sp["hy_conv_w"], sp["hy_conv_b"], name="hy_conv")
    xs, bm, cm = rowmap(f_silu_xbc, [xbc_pre], [], [F32] * 3, name="hy_conv_act", tr=256)
    dtraw_t = dtr[:, :HY_DT].T
    (y, prev_all), full = ssd2_fwd(xs, dtraw_t, sp["hy_dt_bias"], sp["hy_a_log"], sp["hy_d_skip"], bm, cm, side=w_side(1))
    if comm is not None:
        comm["full"] = full
    ysn = rowmap(f_gated_norm, [y, z], [sp["hy_ssm_norm_g"]], [BF16], name="hy_gnorm", tr=128)[0]
    att_in, att_o, att_l = [], [], []
    for p, (win, dil) in enumerate(ATT_PATTERNS):
        if dil == 1:
            qa, ka, va, cols = q, kv, kv, (p, 0, 1)
        else:
            qa, ka, cols = regroup(q[:, p * D:(p + 1) * D], dil), regroup(kv, dil), (0, 0, 1)
            va = ka
        bias = pair_bias(att_bias(sp["rel_table"], p, dil))
        nb = S // dil // ATT_BLK
        (o, lse), full = att2_fwd(qa, ka, va, bias, nb, cols, name=f"att_fwd{p}", side=w_side(2 + p))
        if comm is not None:
            comm["full"] = full
        att_in.append((qa, ka, va, bias, nb, cols))
        att_o.append(regroup(o, dil, inverse=True))
        att_l.append(regroup(lse, dil, inverse=True))
    if comm is not None:
        wts.update(unpack_weights(comm["full"], skip=("hy_in_t",)))
    att = rowmap(f_combine, att_o + att_l, [], [BF16], name="att_combine", tr=256)[0]
    cat = jnp.concatenate([ysn, att], axis=-1)
    mix0 = matmul(cat, wts["hy_out"], mode="nn", out_dtype=F32, name="hy_out")
    x1 = rowmap(f_resid, [x, mix0], [mods[0][2]], [F32], name="l0_res")[0]
    x2, ffn0 = ffn_fwd(x1, 0, mods[0][5], "ffn0")

    h1 = normmod(x2, sp["norm_mix_g"][1], mods[1][1], mods[1][0], "l1_norm")
    p1 = matmul(h1, wts["pw1_t"], mode="nt", out_dtype=BF16, name="cv_pw1")
    u = rowmap(f_glu, [p1], [sp["cv_b_pw1"]], [F32], name="cv_glu")[0]
    uc = conv_fwd(u, sp["cv_w_dw"], sp["cv_b_dw"], name="cv_conv")
    ul = rowmap(f_ln_silu, [uc], [sp["cv_ln_g"], sp["cv_ln_b"]], [BF16], name="cv_ln")[0]
    mix1 = matmul(ul, wts["pw2"], mode="nn", out_dtype=F32, name="cv_pw2")
    x3 = rowmap(f_resid_bias, [x2, mix1], [mods[1][2], sp["cv_b_pw2"]], [F32], name="l1_res")[0]
    x4, ffn1 = ffn_fwd(x3, 1, mods[1][5], "ffn1")

    ones = jnp.ones((S, 1), F32)
    (dres,), (dfinal,), (loss_rows,) = rowmap_bwd(f_head, [x4, tgt], [sp["final_norm_g"]], [ones], name="head",
                                                  row_grad=[True, False], emit=(0,))
    g["final_norm_g"] = dfinal

    dres = ffn_bwd(dres, x3, 1, ffn1, "ffn1")
    (dmix1,), (dg1, db2), _ = rowmap_bwd(f_resid_bias, [x2, mix1], [mods[1][2], sp["cv_b_pw2"]], [dres], name="l1_res_b",
                                         row_grad=[False, True], row_dtypes=[BF16])
    dmods[1][2] = dg1
    g["cv_b_pw2"] = db2
    dul = matmul(dmix1, wts["pw2"], mode="nt", out_dtype=BF16, name="cv_pw2_dx")
    g["pw2"] = wgrad(dmix1, ul, "cv_pw2_dw")
    (duc,), (g["cv_ln_g"], g["cv_ln_b"]), _ = rowmap_bwd(f_ln_silu, [uc], [sp["cv_ln_g"], sp["cv_ln_b"]], [dul],
                                                         name="cv_ln_b", row_grad=[True])
    du, g["cv_w_dw"], g["cv_b_dw"] = conv_bwd(u, sp["cv_w_dw"], duc, name="cv_conv_b", cb=128, chunk_rows=128)
    (dp1,), (g["cv_b_pw1"],), _ = rowmap_bwd(f_glu, [p1], [sp["cv_b_pw1"]], [du], name="cv_glu_b", row_grad=[True],
                                             row_dtypes=[BF16])
    g["pw1_t"] = wgrad(h1, dp1, "cv_pw1_dw")
    dh1 = matmul(dp1, wts["pw1_t"], mode="nn", out_dtype=F32, name="cv_pw1_dx")
    (dres,), (dg_, dsc, dsh), _ = rowmap_bwd(f_normmod, [x2], [sp["norm_mix_g"][1], mods[1][1], mods[1][0]], [dh1],
                                             name="l1_norm_b", row_grad=[True], row_add=[dres])
    g["norm_mix_g1"] = dg_
    dmods[1][1], dmods[1][0] = dsc, dsh

    dres = ffn_bwd(dres, x1, 0, ffn0, "ffn0")
    (dmix0,), (dg1,), _ = rowmap_bwd(f_resid, [x, mix0], [mods[0][2]], [dres], name="l0_res_b",
                                     row_grad=[False, True], row_dtypes=[BF16])
    dmods[0][2] = dg1
    dysn = matmul(dmix0, wts["hy_out"], mode="nt", out_dtype=BF16, name="hy_out_dy", n=SSM_INNER, b_off=0)
    datt = matmul(dmix0, wts["hy_out"], mode="nt", out_dtype=BF16, name="hy_out_da", n=D, b_off=SSM_INNER)
    g["hy_out"] = wgrad(dmix0, cat, "hy_out_dw")
    (dy, dz), (g["hy_ssm_norm_g"],), _ = rowmap_bwd(f_gated_norm, [y, z], [sp["hy_ssm_norm_g"]], [dysn], name="hy_gnorm_b",
                                                    row_grad=[True, True], row_dtypes=[F32, BF16], tr=128)
    if comm is not None:
        comm["ga"] = pack_grads(g, GA_LAYOUT, GA_ROWS)
        comm["recv"] = lax.empty((3, GA_ROWS, D), BF16)
    (dxs, ddtraw_t, g["hy_dt_bias"], g["hy_a_log"], g["hy_d_skip"], dbm, dcm), recv = ssd2_bwd(
        xs, dtraw_t, sp["hy_dt_bias"], sp["hy_a_log"], sp["hy_d_skip"], bm, cm, prev_all, dy, side=g_side(0))
    if comm is not None:
        comm["recv"] = recv
    (dxbc_pre,), _, _ = rowmap_bwd(f_silu_xbc, [xbc_pre], [], [dxs, dbm, dcm], name="hy_conv_act_b", row_grad=[True],
                                   tr=128)
    dxbc_raw, g["hy_conv_w"], g["hy_conv_b"] = conv_bwd(xbc_raw, sp["hy_conv_w"], dxbc_pre, name="hy_conv_b", cb=128, chunk_rows=128, dx_dtype=BF16)
    dol, _, _ = rowmap_bwd(f_combine, att_o + att_l, [], [datt], name="att_combine_b", row_grad=[True] * 6,
                           row_dtypes=[BF16] * 3 + [F32] * 3, tr=128)
    dqs, dks, dvs, dtabs = [], [], [], []
    for p, (win, dil) in enumerate(ATT_PATTERNS):
        qa, ka, va, bias, nb, cols = att_in[p]
        (dq, dkp_, dvp_, dbias), recv = att2_bwd(qa, ka, va, bias, regroup(dol[p], dil), regroup(dol[3 + p], dil), nb,
                                                 cols, name=f"att_bwd{p}", side=g_side(1 + p))
        if comm is not None:
            comm["recv"] = recv
        dqs.append(regroup(dq, dil, inverse=True))
        dks.append(regroup(dkp_, dil, inverse=True))
        dvs.append(regroup(dvp_, dil, inverse=True))
        dtabs.append(att_bias_grad(dbias.reshape(ATT_HEADS, 2, ATT_BLK, ATT_BLK), dil, name=f"att_dtab{p}"))
    g["rel_table"] = jnp.concatenate(dtabs, axis=1)
    dk = rowmap(f_sum3, dks, [], [BF16], name="att_dk_sum")[0]
    dv = rowmap(f_sum3, dvs, [], [BF16], name="att_dv_sum")[0]
    ddt = jnp.pad(ddtraw_t.T, ((0, 0), (0, DT_PAD - HY_DT)))
    dproj = jnp.concatenate([dz, dxbc_raw] + dqs + [dk, dv, ddt.astype(BF16)], axis=-1)
    g["hy_in_t"] = wgrad(h0, dproj, "hy_in_dw")
    if comm is None:
        dh0 = matmul(dproj, w_in, mode="nn", out_dtype=F32, name="hy_in_dx")
    else:
        gb = pack_grads(g, GB_LAYOUT, GB_ROWS)
        half = GB_ROWS // 2
        theirs = swap_halves(gb, name="swap_in_halves")
        ours = lax.dynamic_slice_in_dim(gb, lax.axis_index("c") * half, half, axis=1)
        comm["gb"] = rowmap(f_add, [ours.reshape(N_CHIPS * half, D), theirs.reshape(N_CHIPS * half, D)], [], [BF16],
                            name="sum_in_cores")[0].reshape(N_CHIPS, half, D)
        dh0, comm["recv_b"] = matmul(dproj, w_in, mode="nn", out_dtype=F32, name="hy_in_dx",
                                     side=ScatterRows(comm["gb"], lax.empty((3, half, D), BF16), 0, half))
    (dres,), (dg_, dsc, dsh), _ = rowmap_bwd(f_normmod, [x], [sp["norm_mix_g"][0], mods[0][1], mods[0][0]], [dh0],
                                             name="l0_norm_b", row_grad=[True], row_add=[dres])
    g["norm_mix_g0"] = dg_
    dmods[0][1], dmods[0][0] = dsc, dsh
    return loss_rows, dres, g, dmods


ANY = pl.BlockSpec(memory_space=pl.ANY)
WHOLE_VMEM = pl.BlockSpec(memory_space=pltpu.VMEM)


def _place():
    return lax.axis_index("x"), lax.axis_index("y"), lax.axis_index("c")


def _other_chips(x, y):
    return [(1 - x, y), (x, 1 - y), (1 - x, 1 - y)]


def allgather_small(v, *, name, side=None):
    m_per = v.shape[0]

    def gather(x_ref, out_ref, send_sems, recv_sems, local_sem):
        x, y, c = _place()
        me, sibling = (x, y, c), (x, y, 1 - c)
        chips = _other_chips(x, y)

        def rows(px, py, pc):
            return out_ref.at[pl.ds((4 * px + 2 * py + pc) * m_per, m_per), :]

        def copy(k, block, to, src=None):
            return pltpu.make_async_remote_copy(
                src_ref=rows(*block) if src is None else src, dst_ref=rows(*block),
                send_sem=send_sems.at[k], recv_sem=recv_sems.at[k], device_id=to, device_id_type=MESH)

        mine = pltpu.make_async_copy(x_ref, rows(*me), local_sem)
        mine.start()
        first = [copy(0, me, sibling, src=x_ref)]
        first += [copy(1 + j, me, (*chip, c), src=x_ref) for j, chip in enumerate(chips)]
        for cp in first:
            cp.start()
        passed = [copy(4 + j, (*chip, c), sibling) for j, chip in enumerate(chips)]
        for j, chip in enumerate(chips):
            copy(1 + j, (*chip, c), me).wait_recv()
            passed[j].start()
        copy(0, sibling, me).wait_recv()
        for j, chip in enumerate(chips):
            copy(4 + j, (*chip, 1 - c), me).wait_recv()
        for cp in first + passed:
            cp.wait_send()
        mine.wait()

    out = jax.ShapeDtypeStruct((N_DEV * m_per, LANES), v.dtype)
    sems = [pltpu.SemaphoreType.DMA((7,)), pltpu.SemaphoreType.DMA((7,)), pltpu.SemaphoreType.DMA]
    if side is None:
        return pl.pallas_call(gather, name=name, out_shape=out, in_specs=[WHOLE_VMEM], out_specs=WHOLE_VMEM,
                              scratch_shapes=sems)(v)

    def body(x_ref, src_ref, dst_in_ref, out_ref, dst_ref, send_sems, recv_sems, local_sem, *side_sems):
        side.start(src_ref, dst_ref, side_sems)
        gather(x_ref, out_ref, send_sems, recv_sems, local_sem)
        side.finish(src_ref, dst_ref, side_sems)

    return pl.pallas_call(
        body, name=name, out_shape=[out, jax.ShapeDtypeStruct(side.dst.shape, side.dst.dtype)],
        in_specs=[WHOLE_VMEM, ANY, ANY], out_specs=[WHOLE_VMEM, ANY], scratch_shapes=sems + side.sems(),
        input_output_aliases={2: 1},
    )(v, side.src, side.dst)


def swap_halves(gpack, *, name):
    half_rows = gpack.shape[1] // 2

    def body(g_ref, r_ref, send_sems, recv_sems):
        x, y, c = _place()
        its_half = pl.ds((1 - c) * half_rows, half_rows)
        copies = [pltpu.make_async_remote_copy(
            src_ref=g_ref.at[s, its_half], dst_ref=r_ref.at[s], send_sem=send_sems.at[s], recv_sem=recv_sems.at[s],
            device_id=(x, y, 1 - c), device_id_type=MESH) for s in range(N_CHIPS)]
        for cp in copies:
            cp.start()
        for cp in copies:
            cp.wait()

    return pl.pallas_call(
        body, name=name,
        out_shape=jax.ShapeDtypeStruct((N_CHIPS, half_rows) + gpack.shape[2:], gpack.dtype),
        in_specs=[ANY], out_specs=ANY,
        scratch_shapes=[pltpu.SemaphoreType.DMA((N_CHIPS,)), pltpu.SemaphoreType.DMA((N_CHIPS,))],
    )(gpack)


class GatherRows:
    def __init__(self, pack, full, lo, hi):
        assert (hi - lo) % 32 == 0 and lo % 16 == 0
        self.src, self.dst, self.lo, self.hi = pack, full, lo, hi

    def sems(self):
        return [pltpu.SemaphoreType.DMA((6,)), pltpu.SemaphoreType.DMA((6,)), pltpu.SemaphoreType.DMA]

    def _parts(self, pack_ref, full_ref, sems):
        send_sems, recv_sems, local_sem = sems
        x, y, c = _place()
        half = (self.hi - self.lo) // 2
        mine, its = pl.ds(self.lo + c * half, half), pl.ds(self.lo + (1 - c) * half, half)
        rows = pl.ds(self.lo, self.hi - self.lo)
        local = pltpu.make_async_copy(pack_ref.at[rows], full_ref.at[2 * x + y, rows], local_sem)
        chips = _other_chips(x, y)

        def remote(src, dst, k, to):
            return pltpu.make_async_remote_copy(src_ref=src, dst_ref=dst, send_sem=send_sems.at[k],
                                                recv_sem=recv_sems.at[k], device_id=to, device_id_type=MESH)

        sends = [remote(pack_ref.at[mine], full_ref.at[2 * x + y, mine], k, (cx, cy, c)) for k, (cx, cy) in enumerate(chips)]
        landed = [full_ref.at[2 * cx + cy, mine] for cx, cy in chips]
        arrive = [remote(pack_ref.at[mine], landed[k], k, (cx, cy, c)) for k, (cx, cy) in enumerate(chips)]
        passed = [remote(landed[k], landed[k], 3 + k, (x, y, 1 - c)) for k in range(3)]
        from_sibling = [remote(landed[k], full_ref.at[2 * cx + cy, its], 3 + k, (x, y, 1 - c))
                        for k, (cx, cy) in enumerate(chips)]
        return local, sends, arrive, passed, from_sibling

    def start(self, pack_ref, full_ref, sems):
        local, sends, _, _, _ = self._parts(pack_ref, full_ref, sems)
        local.start()
        for cp in sends:
            cp.start()

    def finish(self, pack_ref, full_ref, sems):
        local, sends, arrive, passed, from_sibling = self._parts(pack_ref, full_ref, sems)
        for k in range(3):
            arrive[k].wait_recv()
            passed[k].start()
        for cp in from_sibling:
            cp.wait_recv()
        for cp in sends + passed:
            cp.wait_send()
        local.wait()


class ScatterRows:
    def __init__(self, gpack, recv, lo, hi):
        assert lo % 16 == 0 and hi % 16 == 0
        self.src, self.dst, self.lo, self.hi = gpack, recv, lo, hi

    def sems(self):
        return [pltpu.SemaphoreType.DMA((3,)), pltpu.SemaphoreType.DMA((3,))]

    def _parts(self, g_ref, recv_ref, sems):
        send_sems, recv_sems = sems
        x, y, c = _place()
        rows = pl.ds(self.lo, self.hi - self.lo)
        return [pltpu.make_async_remote_copy(
            src_ref=g_ref.at[2 * cx + cy, rows], dst_ref=recv_ref.at[k, rows], send_sem=send_sems.at[k],
            recv_sem=recv_sems.at[k], device_id=(cx, cy, c), device_id_type=MESH)
            for k, (cx, cy) in enumerate(_other_chips(x, y))]

    def start(self, g_ref, recv_ref, sems):
        for cp in self._parts(g_ref, recv_ref, sems):
            cp.start()

    def finish(self, g_ref, recv_ref, sems):
        sends = self._parts(g_ref, recv_ref, sems)
        for cp in sends:
            cp.wait_recv()
        for cp in sends:
            cp.wait_send()


def side_call(side, *, name):
    def body(src_ref, dst_in_ref, dst_ref, *sems):
        side.start(src_ref, dst_ref, sems)
        side.finish(src_ref, dst_ref, sems)

    return pl.pallas_call(
        body, name=name, out_shape=jax.ShapeDtypeStruct(side.dst.shape, side.dst.dtype),
        in_specs=[ANY, ANY], out_specs=ANY, scratch_shapes=side.sems(), input_output_aliases={1: 0},
    )(side.src, side.dst)


def grid_call(body, args, *, name, out_shape, grid, in_specs, out_specs, scratch_shapes, semantics, side=None):
    if side is None:
        res = pl.pallas_call(body, name=name, out_shape=out_shape, grid=grid, in_specs=in_specs, out_specs=out_specs,
                             scratch_shapes=scratch_shapes, compiler_params=_cparams(*semantics))(*args)
        return res, None
    n_in, n_out, n_scr = len(args), len(out_shape), len(scratch_shapes)

    def wrapped(*refs):
        ins, (src_ref, _) = refs[:n_in], refs[n_in:n_in + 2]
        outs, dst_ref = refs[n_in + 2:n_in + 2 + n_out], refs[n_in + 2 + n_out]
        scr, sems = refs[n_in + 3 + n_out:n_in + 3 + n_out + n_scr], refs[n_in + 3 + n_out + n_scr:]
        first = functools.reduce(jnp.logical_and, [pl.program_id(i) == 0 for i in range(len(grid))])
        last = functools.reduce(jnp.logical_and, [pl.program_id(i) == n - 1 for i, n in enumerate(grid)])

        @pl.when(first)
        def _():
            side.start(src_ref, dst_ref, sems)

        body(*ins, *outs, *scr)

        @pl.when(last)
        def _():
            side.finish(src_ref, dst_ref, sems)

    res = pl.pallas_call(
        wrapped, name=name,
        out_shape=list(out_shape) + [jax.ShapeDtypeStruct(side.dst.shape, side.dst.dtype)],
        grid=grid, in_specs=list(in_specs) + [ANY, ANY], out_specs=list(out_specs) + [ANY],
        scratch_shapes=list(scratch_shapes) + side.sems(), input_output_aliases={n_in + 1: n_out},
        compiler_params=_cparams(*(["arbitrary"] * len(grid))),
    )(*args, side.src, side.dst)
    return res[:-1], res[-1]


def sibling_swap(p, *, name):
    def body(p_ref, r_ref, send_sem, recv_sem):
        x, y, c = _place()
        cp = pltpu.make_async_remote_copy(src_ref=p_ref, dst_ref=r_ref, send_sem=send_sem, recv_sem=recv_sem,
                                          device_id=(x, y, 1 - c), device_id_type=MESH)
        cp.start()
        cp.wait()

    return pl.pallas_call(
        body, name=name, out_shape=jax.ShapeDtypeStruct(p.shape, p.dtype),
        in_specs=[ANY], out_specs=ANY,
        scratch_shapes=[pltpu.SemaphoreType.DMA, pltpu.SemaphoreType.DMA],
    )(p)


def sum_slots(own, recv, *, name):
    r_dim, c_dim = own.shape
    tr = _pick(r_dim, 256, mult=16)

    def body(o_ref, r_ref, out_ref):
        acc = o_ref[...].astype(F32)
        for k in range(3):
            acc = acc + r_ref[k].astype(F32)
        out_ref[...] = acc

    return pl.pallas_call(
        body, name=name, out_shape=jax.ShapeDtypeStruct((r_dim, c_dim), F32), grid=(r_dim // tr,),
        in_specs=[pl.BlockSpec((tr, c_dim), lambda i: (i, 0)), pl.BlockSpec((3, tr, c_dim), lambda i: (0, i, 0))],
        out_specs=pl.BlockSpec((tr, c_dim), lambda i: (i, 0)),
        compiler_params=_cparams("parallel"),
    )(own, recv)


def sum_devices(v_all, *, name):
    m_per = v_all.shape[0] // N_DEV

    def body(v_ref, o_ref):
        acc = v_ref[pl.ds(0, m_per), :]
        for d in range(1, N_DEV):
            acc = acc + v_ref[pl.ds(d * m_per, m_per), :]
        o_ref[...] = acc

    return pl.pallas_call(
        body, name=name, out_shape=jax.ShapeDtypeStruct((m_per, LANES), F32),
        in_specs=[WHOLE_VMEM], out_specs=WHOLE_VMEM,
    )(v_all)


WEIGHTS = ['ada_w', 'ada_b', 'norm_mix_g', 'norm_ffn_g', 'hy_w_in', 'hy_conv_w', 'hy_conv_b', 'hy_dt_bias', 'hy_a_log',
           'hy_d_skip', 'hy_ssm_norm_g', 'hy_w_out', 'rel_table', 'cv_w_pw1', 'cv_b_pw1', 'cv_w_dw', 'cv_b_dw', 'cv_ln_g',
           'cv_ln_b', 'cv_w_pw2', 'cv_b_pw2', 'ffn_w_gate', 'ffn_w_up', 'ffn_w_down', 'final_norm_g']
BIG = ('ada_w', 'hy_w_in', 'hy_w_out', 'cv_w_pw1', 'cv_w_pw2', 'ffn_w_gate', 'ffn_w_up', 'ffn_w_down')
SMALL_SHARDED = {'hy_conv_w': (1, 4, 3072), 'cv_b_pw1': (1, 2048), 'cv_w_dw': (1, 31, 1024), 'cv_b_dw': (1, 1024),
                 'cv_ln_g': (1, 1024), 'cv_ln_b': (1, 1024), 'cv_b_pw2': (1, 1024)}
SMALL_GRADS = {'ada_b': (2, 6144), 'norm_mix_g': (2, 1024), 'norm_ffn_g': (2, 1024), 'hy_conv_w': (1, 4, 3072),
               'hy_conv_b': (1, 3072), 'hy_dt_bias': (1, 32), 'hy_a_log': (1, 32), 'hy_d_skip': (1, 32),
               'hy_ssm_norm_g': (1, 2048), 'rel_table': (32, 48), 'cv_b_pw1': (1, 2048), 'cv_w_dw': (1, 31, 1024),
               'cv_b_dw': (1, 1024), 'cv_ln_g': (1, 1024), 'cv_ln_b': (1, 1024), 'cv_b_pw2': (1, 1024),
               'final_norm_g': (1024,), 'loss': (1,)}

PACK_LAYOUT = (('hy_in_t', 2568), ('hy_out', 768), ('pw1_t', 512), ('pw2', 256),
               ('gate_t0', 704), ('up_t0', 704), ('down0', 704), ('gate_t1', 704), ('up_t1', 704), ('down1', 704))
PACK_ROWS = 8448


def _pack_offsets(layout):
    off, out = 0, {}
    for nm, r in layout:
        out[nm] = (off, r)
        off += r
    return out


PACK_OFF = _pack_offsets(PACK_LAYOUT)
W_BATCHES = ((0, 2624), (2624, 4992), (4992, 6144), (6144, 7296), (7296, 8448))
GA_LAYOUT = PACK_LAYOUT[1:]
GA_ROWS = 5888
GA_OFF = _pack_offsets(GA_LAYOUT)
G_BATCHES = ((0, 2560), (2560, 3712), (3712, 4864), (4864, 5888))
GB_LAYOUT = PACK_LAYOUT[:1]
GB_ROWS = 2816


def pack_grads(g, layout, n_rows):
    def rows_bf16(nm):
        return g[nm]

    parts = []
    for key, r in layout:
        if key == 'hy_in_t':
            a = hy_from_cat(rows_bf16('hy_in_t'))
        elif key.startswith('gate_t'):
            a = rows_bf16('gu_t' + key[-1])[:FFN_HIDDEN]
        elif key.startswith('up_t'):
            a = rows_bf16('gu_t' + key[-1])[FFN_HIDDEN:]
        else:
            a = rows_bf16(key)
        parts.append(a.reshape(N_CHIPS, r, D))
    used = sum(r for _, r in layout)
    return jnp.concatenate(parts + [jnp.zeros((N_CHIPS, n_rows - used, D), BF16)], axis=1)


def unpack_weights(full, skip=()):
    def whole(nm):
        o, r = PACK_OFF[nm]
        return full[:, o:o + r].reshape(N_CHIPS * r, D)

    out = {"hy_out": whole('hy_out'), "pw1_t": whole('pw1_t'), "pw2": whole('pw2'),
           "gu_t": [jnp.concatenate([whole(f'gate_t{i}'), whole(f'up_t{i}')], axis=0) for i in range(2)],
           "down": [whole(f'down{i}') for i in range(2)]}
    if "hy_in_t" not in skip:
        out["hy_in_t"] = hy_to_cat(whole('hy_in_t'))
    return out


def _to_lanes(flat):
    n = flat.shape[0]
    m = -(-n // (8 * LANES)) * 8
    return jnp.pad(flat, (0, m * LANES - n)).reshape(m, LANES)


def _split(flat, shapes):
    out, off = {}, 0
    for nm, shp in shapes.items():
        n = int(np.prod(shp))
        out[nm] = flat[off:off + n].reshape(shp)
        off += n
    return out


def kernel(x, c, ada_w, ada_b, norm_mix_g, norm_ffn_g, hy_w_in, hy_conv_w, hy_conv_b, hy_dt_bias, hy_a_log, hy_d_skip, hy_ssm_norm_g, hy_w_out, rel_table, cv_w_pw1, cv_b_pw1, cv_w_dw, cv_b_dw, cv_ln_g, cv_ln_b, cv_w_pw2, cv_b_pw2, ffn_w_gate, ffn_w_up, ffn_w_down, final_norm_g, loss_target, m_ada_w, m_ada_b, m_norm_mix_g, m_norm_ffn_g, m_hy_w_in, m_hy_conv_w, m_hy_conv_b, m_hy_dt_bias, m_hy_a_log, m_hy_d_skip, m_hy_ssm_norm_g, m_hy_w_out, m_rel_table, m_cv_w_pw1, m_cv_b_pw1, m_cv_w_dw, m_cv_b_dw, m_cv_ln_g, m_cv_ln_b, m_cv_w_pw2, m_cv_b_pw2, m_ffn_w_gate, m_ffn_w_up, m_ffn_w_down, m_final_norm_g, v_ada_w, v_ada_b, v_norm_mix_g, v_norm_ffn_g, v_hy_w_in, v_hy_conv_w, v_hy_conv_b, v_hy_dt_bias, v_hy_a_log, v_hy_d_skip, v_hy_ssm_norm_g, v_hy_w_out, v_rel_table, v_cv_w_pw1, v_cv_b_pw1, v_cv_w_dw, v_cv_b_dw, v_cv_ln_g, v_cv_ln_b, v_cv_w_pw2, v_cv_b_pw2, v_ffn_w_gate, v_ffn_w_up, v_ffn_w_down, v_final_norm_g):
    args = (x, c, ada_w, ada_b, norm_mix_g, norm_ffn_g, hy_w_in, hy_conv_w, hy_conv_b, hy_dt_bias, hy_a_log, hy_d_skip, hy_ssm_norm_g, hy_w_out, rel_table, cv_w_pw1, cv_b_pw1, cv_w_dw, cv_b_dw, cv_ln_g, cv_ln_b, cv_w_pw2, cv_b_pw2, ffn_w_gate, ffn_w_up, ffn_w_down, final_norm_g, loss_target, m_ada_w, m_ada_b, m_norm_mix_g, m_norm_ffn_g, m_hy_w_in, m_hy_conv_w, m_hy_conv_b, m_hy_dt_bias, m_hy_a_log, m_hy_d_skip, m_hy_ssm_norm_g, m_hy_w_out, m_rel_table, m_cv_w_pw1, m_cv_b_pw1, m_cv_w_dw, m_cv_b_dw, m_cv_ln_g, m_cv_ln_b, m_cv_w_pw2, m_cv_b_pw2, m_ffn_w_gate, m_ffn_w_up, m_ffn_w_down, m_final_norm_g, v_ada_w, v_ada_b, v_norm_mix_g, v_norm_ffn_g, v_hy_w_in, v_hy_conv_w, v_hy_conv_b, v_hy_dt_bias, v_hy_a_log, v_hy_d_skip, v_hy_ssm_norm_g, v_hy_w_out, v_rel_table, v_cv_w_pw1, v_cv_b_pw1, v_cv_w_dw, v_cv_b_dw, v_cv_ln_g, v_cv_ln_b, v_cv_w_pw2, v_cv_b_pw2, v_ffn_w_gate, v_ffn_w_up, v_ffn_w_down, v_final_norm_g)
    x_in, c_in = args[0], args[1]
    w = dict(zip(WEIGHTS, args[2:27], strict=True))
    tgt = args[27]
    m_in = dict(zip(WEIGHTS, args[28:53], strict=True))
    v_in = dict(zip(WEIGHTS, args[53:78], strict=True))
    xi, yi, ci = _place()
    chip = 2 * xi + yi
    dev = 2 * chip + ci

    cs = rowmap(f_silu, [c_in.reshape(8, LANES)], [], [F32], name="cond_silu", tr=8)[0]
    cs_all = allgather_small(cs, name="gather_cond").reshape(N_DEV, D)
    cs16 = jnp.pad(cs_all, ((0, 8), (0, 0)))
    modpart = jnp.stack([matmul(cs16, w['ada_w'][i], mode="nn", out_dtype=F32, name=f"ada_fwd{i}")[:N_DEV]
                         for i in range(2)], axis=1)
    def rows_of(nm, i=None):
        a = w[nm][0 if i is None else i]
        return (a.T if nm in ('hy_w_in', 'cv_w_pw1', 'ffn_w_gate', 'ffn_w_up') else a).astype(BF16)

    pieces = [rows_of('hy_w_in'), rows_of('hy_w_out'), rows_of('cv_w_pw1'), rows_of('cv_w_pw2')]
    for i in range(2):
        pieces += [rows_of('ffn_w_gate', i), rows_of('ffn_w_up', i), rows_of('ffn_w_down', i)]
    n_rows = sum(p.shape[0] for p in pieces)
    pack = jnp.concatenate(pieces + [jnp.zeros((PACK_ROWS - n_rows, D), BF16)], axis=0)

    shard_names = list(SMALL_SHARDED)
    payload = jnp.concatenate([modpart.reshape(-1)] + [w[nm].reshape(-1) for nm in shard_names])
    got, full = allgather_small(_to_lanes(payload), name="gather_mod",
                                side=GatherRows(pack, lax.empty((N_CHIPS, PACK_ROWS, D), BF16), *W_BATCHES[0]))
    got = got.reshape(N_DEV, -1)[0::2]
    modparts = got[:, :modpart.size].reshape(N_CHIPS, N_DEV, 2, 1536)
    mine = lax.dynamic_index_in_dim(modparts, dev, axis=1, keepdims=False)
    mod = jnp.transpose(mine, (1, 0, 2)).reshape(2, 6 * D) + w['ada_b']
    mods = [[mod[i, j * D:(j + 1) * D].reshape(1, D) for j in range(6)] for i in range(2)]
    sp = {}
    off = modpart.size
    for nm in shard_names:
        shp = w[nm].shape
        n = int(np.prod(shp))
        parts = got[:, off:off + n].reshape((N_CHIPS,) + shp)
        sp[nm] = jnp.concatenate([parts[s] for s in range(N_CHIPS)], axis=-1)
        off += n

    o_in, r_in = PACK_OFF['hy_in_t']
    wts = {"hy_in_t": hy_to_cat(full[:, o_in:o_in + r_in].reshape(N_CHIPS * r_in, D))}
    comm = {"pack": pack, "full": full}

    sp = {"norm_mix_g": [w['norm_mix_g'][i].reshape(1, D) for i in range(2)],
          "norm_ffn_g": [w['norm_ffn_g'][i].reshape(1, D) for i in range(2)],
          "hy_conv_w": sp['hy_conv_w'][0], "hy_conv_b": w['hy_conv_b'],
          "hy_dt_bias": w['hy_dt_bias'].reshape(SSM_HEADS, 1), "hy_a_log": w['hy_a_log'].reshape(SSM_HEADS, 1),
          "hy_d_skip": w['hy_d_skip'].reshape(SSM_HEADS, 1), "hy_ssm_norm_g": w['hy_ssm_norm_g'],
          "rel_table": w['rel_table'], "cv_b_pw1": sp['cv_b_pw1'], "cv_w_dw": sp['cv_w_dw'][0], "cv_b_dw": sp['cv_b_dw'],
          "cv_ln_g": sp['cv_ln_g'], "cv_ln_b": sp['cv_ln_b'], "cv_b_pw2": sp['cv_b_pw2'],
          "final_norm_g": w['final_norm_g'].reshape(1, D)}

    loss_rows, grad_x, g, dmods = device_step(x_in[0], tgt[0], mods, wts, sp, comm)

    dmod = jnp.stack([jnp.concatenate([d.reshape(-1) for d in dmods[i]]) for i in range(2)])
    small = {'ada_b': dmod, 'norm_mix_g': jnp.stack([g[f'norm_mix_g{i}'].reshape(-1) for i in range(2)]),
             'norm_ffn_g': jnp.stack([g[f'norm_ffn_g{i}'].reshape(-1) for i in range(2)]),
             'loss': jnp.sum(loss_rows).reshape(1)}
    for nm in SMALL_GRADS:
        if nm not in small:
            small[nm] = g[nm]
    vec = _to_lanes(jnp.concatenate([small[nm].reshape(-1) for nm in SMALL_GRADS]))
    vec_all = allgather_small(vec, name="gather_small_grads")
    tot = _split(sum_devices(vec_all, name="sum_small_grads").reshape(-1), SMALL_GRADS)
    dmod_all = vec_all.reshape(N_DEV, -1)[:, :2 * 6 * D].reshape(N_DEV, 2, 6 * D)

    recv = comm["recv"]
    own_a = lax.dynamic_index_in_dim(comm["ga"], chip, axis=0, keepdims=False)
    part_a = sum_slots(own_a, recv, name="sum_chip_grads")
    red_a = rowmap(f_add, [part_a, sibling_swap(part_a, name="swap_grads")], [], [F32], name="sum_core_grads")[0]
    recv_b = comm["recv_b"]
    own_b = lax.dynamic_index_in_dim(comm["gb"], chip, axis=0, keepdims=False)
    mine_half = sum_slots(own_b, recv_b, name="sum_in_chips")
    its_half = sibling_swap(mine_half, name="swap_in")
    red_b = jnp.concatenate([jnp.where(ci == 0, mine_half, its_half), jnp.where(ci == 0, its_half, mine_half)], axis=0)

    def shard_grad(nm, i=None):
        key = {'hy_w_in': 'hy_in_t', 'hy_w_out': 'hy_out', 'cv_w_pw1': 'pw1_t', 'cv_w_pw2': 'pw2'}.get(nm)
        if key is None:
            key = {'ffn_w_gate': 'gate_t', 'ffn_w_up': 'up_t', 'ffn_w_down': 'down'}[nm] + str(i)
        if key == 'hy_in_t':
            a = red_b[:PACK_OFF[key][1]]
        else:
            o, r = GA_OFF[key]
            a = red_a[o:o + r]
        return a.T if key.endswith('_t') or key[:-1].endswith('_t') else a

    grads = {}
    grads['hy_w_in'] = shard_grad('hy_w_in')[None]
    grads['hy_w_out'] = shard_grad('hy_w_out')[None]
    grads['cv_w_pw1'] = shard_grad('cv_w_pw1')[None]
    grads['cv_w_pw2'] = shard_grad('cv_w_pw2')[None]
    for nm in ('ffn_w_gate', 'ffn_w_up', 'ffn_w_down'):
        grads[nm] = jnp.stack([shard_grad(nm, i) for i in range(2)])
    cs16 = jnp.pad(cs_all, ((0, 8), (0, 0)))
    dm_mine = lax.dynamic_slice_in_dim(dmod_all, chip * 1536, 1536, axis=2)
    dm16 = jnp.pad(dm_mine, ((0, 8), (0, 0), (0, 0)))
    grads['ada_w'] = jnp.stack([matmul(cs16, dm16[:, i], mode="tn", out_dtype=F32, name=f"ada_dw{i}") for i in range(2)])
    for nm, shp in SMALL_GRADS.items():
        if nm == 'loss':
            continue
        if nm in SMALL_SHARDED:
            n = w[nm].shape[-1]
            grads[nm] = lax.dynamic_slice_in_dim(tot[nm], chip * n, n, axis=len(shp) - 1)
        else:
            grads[nm] = tot[nm].reshape(w[nm].shape)

    delta, new_m, new_v = {}, {}, {}
    for nm in BIG:
        delta[nm], new_m[nm], new_v[nm] = adamw(w[nm], grads[nm], m_in[nm], v_in[nm], name="adamw_" + nm)
    smalls = [nm for nm in WEIGHTS if nm not in BIG]
    packed = [_to_lanes(jnp.concatenate([d[nm].reshape(-1) for nm in smalls])) for d in (w, grads, m_in, v_in)]
    res = rowmap(f_adamw, packed, [], [F32] * 3, name="adamw_small", tr=_rows_tile(packed[0].shape[0]))
    for d, r in zip((delta, new_m, new_v), res, strict=True):
        d.update(_split(r.reshape(-1), {nm: w[nm].shape for nm in smalls}))

    loss = tot['loss'].reshape(())
    return (loss, grad_x[None], *[grads[nm] for nm in WEIGHTS], *[delta[nm] for nm in WEIGHTS],
            *[new_m[nm] for nm in WEIGHTS], *[new_v[nm] for nm in WEIGHTS])
```

```python
import functools
import math

import jax
import jax.numpy as jnp
import numpy as np
from jax import lax
from jax.experimental import pallas as pl
from jax.experimental.pallas import tpu as pltpu

F32 = jnp.float32
BF16 = jnp.bfloat16
MESH = pl.DeviceIdType.MESH

D = 1024
S = 4096
EPS = 1e-6
SSM_INNER = 2048
SSM_HEADS = 32
SSM_HDIM = 64
SSM_GROUPS = 4
SSM_STATE = 128
SSM_CONVK = 4
SSM_CONV_DIM = 3072
CHUNK = 128
N_CHUNKS = S // CHUNK
ATT_HEADS = 16
ATT_HDIM = 64
ATT_PATTERNS = ((128, 1), (512, 4), (2048, 16))
ATT_BLK = 128
REL_BUCKETS = 32
REL_MAX_DIST = 2048
CONV_WIDTH = 31
FFN_HIDDEN = 2816
N_CHIPS = 4
N_DEV = 8
ADAM_LR, ADAM_B1, ADAM_B2, ADAM_EPS, ADAM_WD, ADAM_STEP = 0.001, 0.9, 0.999, 1e-08, 0.01, 10

VMEM_LIMIT_BYTES = 56 * 1024 * 1024
LANES = 128


def _cparams(*sem):
    return pltpu.CompilerParams(dimension_semantics=sem, vmem_limit_bytes=VMEM_LIMIT_BYTES)


def _pick(n, cap, mult=LANES):
    best = None
    for t in range(mult, min(n, cap) + 1, mult):
        if n % t == 0:
            best = t
    return best or n


def _dot(a, b, ca, cb):
    return lax.dot_general(a.astype(BF16), b.astype(BF16), (((ca,), (cb,)), ((), ())), preferred_element_type=F32)


@jax.custom_vjp
def mm_nt(a, b):
    return _dot(a, b, 1, 1)


def _mm_nt_fwd(a, b):
    return _dot(a, b, 1, 1), (a, b)


def _mm_nt_bwd(res, g):
    a, b = res
    return _dot(g, b, 1, 0).astype(a.dtype), _dot(g, a, 0, 0).astype(b.dtype)


mm_nt.defvjp(_mm_nt_fwd, _mm_nt_bwd)


def matmul(a, b, *, mode, out_dtype, name, n=None, b_off=0, tm_cap=1024, tn_cap=512, tk_cap=3584, side=None,
           out_t=False):
    if mode == "tn":
        k_dim, m_dim = a.shape
    else:
        m_dim, k_dim = a.shape
    n_dim = n if n is not None else (b.shape[0] if mode == "nt" else b.shape[1])
    tm = m_dim if m_dim < LANES else _pick(m_dim, tm_cap)
    tn = _pick(n_dim, tn_cap)
    tk = k_dim if k_dim < LANES else _pick(k_dim, tk_cap)
    assert m_dim % tm == 0 and n_dim % tn == 0 and k_dim % tk == 0 and b_off % tn == 0
    nk = k_dim // tk
    off = b_off // tn
    if mode == "nn":
        a_spec = pl.BlockSpec((tm, tk), lambda i, j, k: (i, k))
        b_spec = pl.BlockSpec((tk, tn), lambda i, j, k: (k, j))
        ca, cb = 1, 0
    elif mode == "nt":
        a_spec = pl.BlockSpec((tm, tk), lambda i, j, k: (i, k))
        b_spec = pl.BlockSpec((tn, tk), lambda i, j, k: (j + off, k))
        ca, cb = 1, 1
    else:
        a_spec = pl.BlockSpec((tk, tm), lambda i, j, k: (k, i))
        b_spec = pl.BlockSpec((tk, tn), lambda i, j, k: (k, j))
        ca, cb = 0, 0

    def emit(o_ref, val):
        o_ref[...] = (val.T if out_t else val).astype(o_ref.dtype)

    def body(a_ref, b_ref, o_ref, acc_ref):
        part = _dot(a_ref[...], b_ref[...], ca, cb)
        if nk == 1:
            emit(o_ref, part)
        else:
            k = pl.program_id(2)

            @pl.when(k == 0)
            def _():
                acc_ref[...] = part

            @pl.when(k > 0)
            def _():
                acc_ref[...] += part

            @pl.when(k == nk - 1)
            def _():
                emit(o_ref, acc_ref[...])

    if out_t:
        out_shape, out_spec = (n_dim, m_dim), pl.BlockSpec((tn, tm), lambda i, j, k: (j, i))
    else:
        out_shape, out_spec = (m_dim, n_dim), pl.BlockSpec((tm, tn), lambda i, j, k: (i, j))
    (out,), side_dst = grid_call(
        body, (a, b), name=name,
        out_shape=[jax.ShapeDtypeStruct(out_shape, out_dtype)],
        grid=(m_dim // tm, n_dim // tn, nk),
        in_specs=[a_spec, b_spec],
        out_specs=[out_spec],
        scratch_shapes=[pltpu.VMEM((tm, tn), F32)],
        semantics=("parallel", "parallel", "arbitrary"), side=side)
    return out if side is None else (out, side_dst)


def _f32(xs):
    return [x.astype(F32) for x in xs]


def rowmap(f, rows, consts, out_dtypes, *, name, tr=512):
    r_dim = rows[0].shape[0]
    tr = _pick(r_dim, tr, mult=8)
    assert r_dim % tr == 0
    nr, nc = len(rows), len(consts)
    outs = jax.eval_shape(lambda *xs: f(*xs), *[jax.ShapeDtypeStruct((tr, x.shape[1]), F32) for x in rows],
                          *[jax.ShapeDtypeStruct(x.shape, F32) for x in consts])

    def body(*refs):
        res = f(*_f32([r[...] for r in refs[:nr + nc]]))
        for o_ref, o in zip(refs[nr + nc:], res, strict=True):
            o_ref[...] = o.astype(o_ref.dtype)

    return pl.pallas_call(
        body, name=name,
        out_shape=[jax.ShapeDtypeStruct((r_dim, o.shape[1]), dt) for o, dt in zip(outs, out_dtypes, strict=True)],
        grid=(r_dim // tr,),
        in_specs=[pl.BlockSpec((tr, x.shape[1]), lambda i: (i, 0)) for x in rows]
        + [pl.BlockSpec(x.shape, lambda i: (0, 0)) for x in consts],
        out_specs=[pl.BlockSpec((tr, o.shape[1]), lambda i: (i, 0)) for o in outs],
        compiler_params=_cparams("parallel"),
    )(*rows, *consts)


def rowmap_bwd(f, rows, consts, cts, *, name, row_grad, row_dtypes=None, tr=512, emit=(), row_add=None,
               emit_dtype=F32):
    r_dim = rows[0].shape[0]
    tr = _pick(r_dim, tr, mult=8)
    assert r_dim % tr == 0
    nr, nc, nct = len(rows), len(consts), len(cts)
    gi = [i for i, flag in enumerate(row_grad) if flag]
    row_dtypes = row_dtypes or [F32] * len(gi)
    row_add = row_add or [None] * len(gi)
    adds = [a for a in row_add if a is not None]
    outs = jax.eval_shape(lambda *xs: f(*xs), *[jax.ShapeDtypeStruct((tr, x.shape[1]), F32) for x in rows],
                          *[jax.ShapeDtypeStruct(x.shape, F32) for x in consts])

    def body(*refs):
        ins = _f32([r[...] for r in refs[:nr + nc]])
        ct = _f32([r[...] for r in refs[nr + nc:nr + nc + nct]])
        add_refs = list(refs[nr + nc + nct:nr + nc + nct + len(adds)])
        o_refs = refs[nr + nc + nct + len(adds):]
        res, vjp = jax.vjp(f, *ins)
        grads = vjp(tuple(ct))
        for o_ref, i, a in zip(o_refs[:len(gi)], gi, row_add):
            g = grads[i] if a is None else grads[i] + add_refs.pop(0)[...].astype(F32)
            o_ref[...] = g.astype(o_ref.dtype)
        first = pl.program_id(0) == 0
        for o_ref, g in zip(o_refs[len(gi):len(gi) + nc], grads[nr:]):
            @pl.when(first)
            def _(o_ref=o_ref, g=g):
                o_ref[...] = g

            @pl.when(jnp.logical_not(first))
            def _(o_ref=o_ref, g=g):
                o_ref[...] += g
        for o_ref, i in zip(o_refs[len(gi) + nc:], emit):
            o_ref[...] = res[i].astype(o_ref.dtype)

    out_shape = ([jax.ShapeDtypeStruct(rows[i].shape, dt) for i, dt in zip(gi, row_dtypes, strict=True)]
                 + [jax.ShapeDtypeStruct(x.shape, F32) for x in consts]
                 + [jax.ShapeDtypeStruct((r_dim, outs[i].shape[1]), emit_dtype) for i in emit])
    out_specs = ([pl.BlockSpec((tr, rows[i].shape[1]), lambda i_: (i_, 0)) for i in gi]
                 + [pl.BlockSpec(x.shape, lambda i_: (0, 0)) for x in consts]
                 + [pl.BlockSpec((tr, outs[i].shape[1]), lambda i_: (i_, 0)) for i in emit])
    res = pl.pallas_call(
        body, name=name,
        out_shape=out_shape,
        grid=(r_dim // tr,),
        in_specs=[pl.BlockSpec((tr, x.shape[1]), lambda i: (i, 0)) for x in rows]
        + [pl.BlockSpec(x.shape, lambda i: (0, 0)) for x in consts]
        + [pl.BlockSpec((tr, x.shape[1]), lambda i: (i, 0)) for x in list(cts) + adds],
        out_specs=out_specs,
        compiler_params=_cparams("arbitrary"),
    )(*rows, *consts, *cts, *adds)
    return res[:len(gi)], res[len(gi):len(gi) + nc], res[len(gi) + nc:]


def matmul_swiglu(hgu, w, *, name, tm=512, tk_cap=1536):
    m_dim, hid = hgu.shape[0], hgu.shape[1] // 2
    n_dim = w.shape[1]
    tk = _pick(hid, tk_cap)
    nk = hid // tk
    assert m_dim % tm == 0 and hid % tk == 0

    def body(g_ref, u_ref, w_ref, o_ref, acc_ref):
        gate, up = g_ref[...].astype(F32), u_ref[...].astype(F32)
        part = _dot(_silu(gate) * up, w_ref[...], 1, 0)
        k = pl.program_id(1)

        @pl.when(k == 0)
        def _():
            acc_ref[...] = part

        @pl.when(k > 0)
        def _():
            acc_ref[...] += part

        @pl.when(k == nk - 1)
        def _():
            o_ref[...] = acc_ref[...]

    return pl.pallas_call(
        body, name=name, out_shape=jax.ShapeDtypeStruct((m_dim, n_dim), F32), grid=(m_dim // tm, nk),
        in_specs=[pl.BlockSpec((tm, tk), lambda i, k: (i, k)), pl.BlockSpec((tm, tk), lambda i, k: (i, k + nk)),
                  pl.BlockSpec((tk, n_dim), lambda i, k: (k, 0))],
        out_specs=pl.BlockSpec((tm, n_dim), lambda i, k: (i, 0)),
        scratch_shapes=[pltpu.VMEM((tm, n_dim), F32)],
        compiler_params=_cparams("parallel", "arbitrary"),
    )(hgu, hgu, w)


def transpose(a, *, name, out_dtype=BF16, tr=512, tc=512):
    r_dim, c_dim = a.shape
    tr, tc = _pick(r_dim, tr), _pick(c_dim, tc)

    def body(a_ref, o_ref):
        o_ref[...] = a_ref[...].astype(F32).T.astype(o_ref.dtype)

    return pl.pallas_call(
        body, name=name, out_shape=jax.ShapeDtypeStruct((c_dim, r_dim), out_dtype),
        grid=(r_dim // tr, c_dim // tc),
        in_specs=[pl.BlockSpec((tr, tc), lambda i, j: (i, j))],
        out_specs=pl.BlockSpec((tc, tr), lambda i, j: (j, i)),
        compiler_params=_cparams("parallel", "parallel"),
    )(a)


CONV_HALO = 32
CONV_ROWS = 256


def conv_fwd(x, w, b, *, name, cb=256, chunk_rows=CONV_ROWS):
    s_dim, c_dim = x.shape
    taps = w.shape[0]
    assert taps - 1 <= CONV_HALO and s_dim % chunk_rows == 0 and c_dim % cb == 0
    n_chunks = s_dim // chunk_rows
    ext = chunk_rows + CONV_HALO

    def body(x_ref, w_ref, b_ref, o_ref, xp_ref):
        xp_ref[pl.ds(0, CONV_HALO), :] = jnp.zeros((CONV_HALO, cb), F32)
        xp_ref[pl.ds(CONV_HALO, s_dim), :] = x_ref[...].astype(F32)
        wv = w_ref[...].astype(F32)
        bv = b_ref[...].astype(F32)

        def chunk(t, carry):
            base = pl.multiple_of(t * chunk_rows, chunk_rows)
            xe = xp_ref[pl.ds(base, ext), :]
            acc = jnp.broadcast_to(bv, (chunk_rows, cb))
            for j in range(taps):
                sh = xe if j == 0 else pltpu.roll(xe, shift=j, axis=0)
                acc = acc + wv[taps - 1 - j:taps - j, :] * sh[CONV_HALO:, :]
            o_ref[pl.ds(base, chunk_rows), :] = acc
            return carry

        lax.fori_loop(0, n_chunks, chunk, 0)

    return pl.pallas_call(
        body, name=name,
        out_shape=jax.ShapeDtypeStruct((s_dim, c_dim), F32),
        grid=(c_dim // cb,),
        in_specs=[pl.BlockSpec((s_dim, cb), lambda i: (0, i)), pl.BlockSpec((taps, cb), lambda i: (0, i)),
                  pl.BlockSpec((1, cb), lambda i: (0, i))],
        out_specs=pl.BlockSpec((s_dim, cb), lambda i: (0, i)),
        scratch_shapes=[pltpu.VMEM((s_dim + CONV_HALO, cb), F32)],
        compiler_params=_cparams("parallel"),
    )(x, w, b)


def conv_bwd(x, w, g, *, name, cb=256, chunk_rows=CONV_ROWS, dx_dtype=F32):
    s_dim, c_dim = x.shape
    taps = w.shape[0]
    n_chunks = s_dim // chunk_rows
    ext = chunk_rows + CONV_HALO

    def rows8(a):
        return jnp.sum(a.reshape(chunk_rows // 8, 8, cb), axis=0)

    def body(x_ref, w_ref, g_ref, dx_ref, dw_ref, db_ref, xp_ref, gp_ref, acc_ref):
        xp_ref[pl.ds(0, CONV_HALO), :] = jnp.zeros((CONV_HALO, cb), F32)
        xp_ref[pl.ds(CONV_HALO, s_dim), :] = x_ref[...].astype(F32)
        gp_ref[pl.ds(0, s_dim), :] = g_ref[...].astype(F32)
        gp_ref[pl.ds(s_dim, CONV_HALO), :] = jnp.zeros((CONV_HALO, cb), F32)
        acc_ref[...] = jnp.zeros_like(acc_ref)
        wv = w_ref[...].astype(F32)

        def chunk(t, carry):
            base = pl.multiple_of(t * chunk_rows, chunk_rows)
            xe = xp_ref[pl.ds(base, ext), :]
            ge = gp_ref[pl.ds(base, ext), :]
            gc = ge[:chunk_rows, :]
            dx = jnp.zeros((chunk_rows, cb), F32)
            for j in range(taps):
                xs = xe if j == 0 else pltpu.roll(xe, shift=j, axis=0)
                gs = ge if j == 0 else pltpu.roll(ge, shift=ext - j, axis=0)
                k = taps - 1 - j
                dx = dx + wv[k:k + 1, :] * gs[:chunk_rows, :]
                acc_ref[8 * k:8 * k + 8, :] += rows8(gc * xs[CONV_HALO:, :])
            acc_ref[8 * taps:8 * taps + 8, :] += rows8(gc)
            dx_ref[pl.ds(base, chunk_rows), :] = dx.astype(dx_ref.dtype)
            return carry

        lax.fori_loop(0, n_chunks, chunk, 0)
        sums = jnp.sum(acc_ref[...].reshape(taps + 1, 8, cb), axis=1)
        dw_ref[...] = sums[0:taps, :]
        db_ref[...] = sums[taps:taps + 1, :]

    return pl.pallas_call(
        body, name=name,
        out_shape=[jax.ShapeDtypeStruct((s_dim, c_dim), dx_dtype), jax.ShapeDtypeStruct((taps, c_dim), F32),
                   jax.ShapeDtypeStruct((1, c_dim), F32)],
        grid=(c_dim // cb,),
        in_specs=[pl.BlockSpec((s_dim, cb), lambda i: (0, i)), pl.BlockSpec((taps, cb), lambda i: (0, i)),
                  pl.BlockSpec((s_dim, cb), lambda i: (0, i))],
        out_specs=[pl.BlockSpec((s_dim, cb), lambda i: (0, i)), pl.BlockSpec((taps, cb), lambda i: (0, i)),
                   pl.BlockSpec((1, cb), lambda i: (0, i))],
        scratch_shapes=[pltpu.VMEM((s_dim + CONV_HALO, cb), F32), pltpu.VMEM((s_dim + CONV_HALO, cb), F32),
                        pltpu.VMEM((8 * (taps + 1), cb), F32)],
        compiler_params=_cparams("parallel"),
    )(x, w, g)


def _softplus(x):
    return jnp.maximum(x, 0.0) + jnp.log(1.0 + jnp.exp(-jnp.abs(x)))


HEADS_PER_GROUP = SSM_HEADS // SSM_GROUPS


def _bdot(a, b, ca, cb):
    return lax.dot_general(a.astype(BF16), b.astype(BF16), (((ca,), (cb,)), ((0,), (0,))), preferred_element_type=F32)


@jax.custom_vjp
def bmm(a, b):
    return _bdot(a, b, 2, 1)


def _bmm_fwd(a, b):
    return _bdot(a, b, 2, 1), (a, b)


def _bmm_bwd(res, g):
    a, b = res
    return _bdot(g, b, 2, 2).astype(a.dtype), _bdot(a, g, 1, 1).astype(b.dtype)


bmm.defvjp(_bmm_fwd, _bmm_bwd)


@jax.custom_vjp
def bmm_nt(a, b):
    return _bdot(a, b, 2, 2)


def _bmm_nt_fwd(a, b):
    return _bdot(a, b, 2, 2), (a, b)


def _bmm_nt_bwd(res, g):
    a, b = res
    return _bdot(g, b, 2, 1).astype(a.dtype), _bdot(g, a, 1, 1).astype(b.dtype)


bmm_nt.defvjp(_bmm_nt_fwd, _bmm_nt_bwd)


@jax.custom_vjp
def bmm_tn(a, b):
    return _bdot(a, b, 1, 1)


def _bmm_tn_fwd(a, b):
    return _bdot(a, b, 1, 1), (a, b)


def _bmm_tn_bwd(res, g):
    a, b = res
    return _bdot(b, g, 2, 2).astype(a.dtype), _bdot(a, g, 2, 1).astype(b.dtype)


bmm_tn.defvjp(_bmm_tn_fwd, _bmm_tn_bwd)


ATT_PAIRS = ATT_HEADS // 2
PAIR_W = 2 * ATT_HDIM


def att_pairs(q, kp, kc, vp, vc, bias, has_prev):
    t, b, w = q.shape
    i = lax.broadcasted_iota(jnp.int32, (1, b, b), 1)
    j = lax.broadcasted_iota(jnp.int32, (1, b, b), 2)
    first = lax.broadcasted_iota(jnp.int32, (1, 1, w), 2) < ATT_HDIM
    scale = ATT_HDIM ** -0.5
    outs, lses = [], []
    for ab in range(2):
        qh = jnp.where(first if ab == 0 else jnp.logical_not(first), q, 0.0)
        sp = jnp.where(jnp.logical_and(j >= i, has_prev), bmm_nt(qh, kp) * scale + bias[:, ab, 0], -1e30)
        sc = jnp.where(j <= i, bmm_nt(qh, kc) * scale + bias[:, ab, 1], -1e30)
        m = lax.stop_gradient(jnp.maximum(jnp.max(sp, axis=2, keepdims=True), jnp.max(sc, axis=2, keepdims=True)))
        pp, pc = jnp.exp(sp - m), jnp.exp(sc - m)
        l = jnp.sum(pp, axis=2, keepdims=True) + jnp.sum(pc, axis=2, keepdims=True)
        outs.append(bmm(pp / l, vp) + bmm(pc / l, vc))
        lses.append(jnp.broadcast_to(m + jnp.log(l), (t, b, w)))
    return jnp.where(first, outs[0], outs[1]), jnp.where(first, lses[0], lses[1])


def _pair_tiles(ref):
    return jnp.stack([ref[:, PAIR_W * t:PAIR_W * (t + 1)] for t in range(ATT_PAIRS)])


def _store_pair_tiles(ref, val):
    for t in range(ATT_PAIRS):
        ref[:, PAIR_W * t:PAIR_W * (t + 1)] = val[t].astype(ref.dtype)


def pair_bias(bias):
    return bias.reshape(ATT_PAIRS, 2, 2, ATT_BLK, ATT_BLK)


def att2_fwd(q, k, v, bias, nb, cols, *, name, side=None):
    n_blocks = S // ATT_BLK
    qc, kc, vc = cols

    def body(q_ref, k_ref, v_ref, b_ref, o_ref, l_ref, kprev, vprev):
        blk = pl.program_id(0)

        @pl.when(blk == 0)
        def _():
            kprev[...] = jnp.zeros_like(kprev)
            vprev[...] = jnp.zeros_like(vprev)

        k3, v3 = _pair_tiles(k_ref), _pair_tiles(v_ref)
        o, lse = att_pairs(_pair_tiles(q_ref), kprev[...], k3, vprev[...], v3, b_ref[...], (blk % nb) != 0)
        _store_pair_tiles(o_ref, o)
        _store_pair_tiles(l_ref, lse)
        kprev[...] = k3
        vprev[...] = v3

    def spec(c):
        return pl.BlockSpec((ATT_BLK, D), lambda b: (b, c))

    return grid_call(
        body, (q, k, v, bias), name=name,
        out_shape=[jax.ShapeDtypeStruct((S, D), BF16), jax.ShapeDtypeStruct((S, D), F32)], grid=(n_blocks,),
        in_specs=[spec(qc), spec(kc), spec(vc), pl.BlockSpec(bias.shape, lambda b: (0, 0, 0, 0, 0))],
        out_specs=[spec(0), spec(0)],
        scratch_shapes=[pltpu.VMEM((ATT_PAIRS, ATT_BLK, PAIR_W), BF16), pltpu.VMEM((ATT_PAIRS, ATT_BLK, PAIR_W), BF16)],
        semantics=("arbitrary",), side=side)


def att2_bwd(q, k, v, bias, do, dlse, nb, cols, *, name, side=None):
    n_blocks = S // ATT_BLK
    qc, kc, vc = cols

    def body(q_ref, k_ref, v_ref, b_ref, do_ref, dl_ref, dq_ref, dk_ref, dv_ref, db_ref, kprev, vprev, dk_own, dv_own):
        blk = pl.program_id(0)

        @pl.when(blk == 0)
        def _():
            for r in (kprev, vprev, dk_own, dv_own, db_ref):
                r[...] = jnp.zeros_like(r)

        @pl.when(blk < n_blocks)
        def _():
            k3, v3 = _pair_tiles(k_ref), _pair_tiles(v_ref)
            ins = _f32([_pair_tiles(q_ref), kprev[...], k3, vprev[...], v3]) + [b_ref[...]]
            _, vjp = jax.vjp(functools.partial(att_pairs, has_prev=(blk % nb) != 0), *ins)
            dq, dkp, dkc, dvp, dvc, db = vjp(tuple(_f32([_pair_tiles(do_ref), _pair_tiles(dl_ref)])))
            _store_pair_tiles(dq_ref, dq)
            _store_pair_tiles(dk_ref, dk_own[...] + dkp)
            _store_pair_tiles(dv_ref, dv_own[...] + dvp)
            dk_own[...] = dkc
            dv_own[...] = dvc
            db_ref[...] += db
            kprev[...] = k3
            vprev[...] = v3

        @pl.when(blk == n_blocks)
        def _():
            _store_pair_tiles(dk_ref, dk_own[...])
            _store_pair_tiles(dv_ref, dv_own[...])

    def spec(c):
        return pl.BlockSpec((ATT_BLK, D), lambda b: (jnp.minimum(b, n_blocks - 1), c))

    late = pl.BlockSpec((ATT_BLK, D), lambda b: (jnp.maximum(b - 1, 0), 0))
    bspec = pl.BlockSpec(bias.shape, lambda b: (0, 0, 0, 0, 0))
    tile_f32 = pltpu.VMEM((ATT_PAIRS, ATT_BLK, PAIR_W), F32)
    tile_bf16 = pltpu.VMEM((ATT_PAIRS, ATT_BLK, PAIR_W), BF16)
    return grid_call(
        body, (q, k, v, bias, do, dlse), name=name,
        out_shape=[jax.ShapeDtypeStruct((S, D), BF16), jax.ShapeDtypeStruct((S, D), BF16),
                   jax.ShapeDtypeStruct((S, D), BF16), jax.ShapeDtypeStruct(bias.shape, F32)],
        grid=(n_blocks + 1,),
        in_specs=[spec(qc), spec(kc), spec(vc), bspec, spec(0), spec(0)],
        out_specs=[spec(0), late, late, bspec],
        scratch_shapes=[tile_bf16, tile_bf16, tile_f32, tile_f32],
        semantics=("arbitrary",), side=side)


def regroup(a, dil, inverse=False):
    if dil == 1:
        return a
    c_dim = a.shape[1]
    shape = (dil, S // dil, c_dim) if inverse else (S // dil, dil, c_dim)
    return jnp.transpose(a.reshape(shape), (1, 0, 2)).reshape(S, c_dim)


SSD_PAIRS = SSM_HEADS // 2
PAIRS_PER_GROUP = SSD_PAIRS // SSM_GROUPS
GROUP_W = HEADS_PER_GROUP * SSM_HDIM


SSD_GROUPS_PER_STEP = 4
SSD_STEP_PAIRS = PAIRS_PER_GROUP * SSD_GROUPS_PER_STEP
SSD_STEP_W = GROUP_W * SSD_GROUPS_PER_STEP


def ssd_pairs(x, dtraw, dt_bias, a_log, dskip, bms, cms, prev):
    t, q, w = x.shape
    n = bms[0].shape[1]
    per = t // len(bms)

    def by_pair(mats):
        return jnp.concatenate([jnp.broadcast_to(m[None], (per,) + m.shape) for m in mats], axis=0)
    li = lax.broadcasted_iota(jnp.int32, (1, q, q), 1)
    si = lax.broadcasted_iota(jnp.int32, (1, q, q), 2)
    first_lane = lax.broadcasted_iota(jnp.int32, (1, 1, w), 2) < SSM_HDIM
    first_row = lax.broadcasted_iota(jnp.int32, (1, w, 1), 1) < SSM_HDIM

    def to_col(row):
        return jnp.sum(jnp.where(li == si, jnp.broadcast_to(row, (t, q, q)), 0.0), axis=2, keepdims=True)

    def lanes(a0, a1):
        return jnp.where(first_lane, a0, a1)

    dt_col, acs_col, total, lmat = [], [], [], []
    for ab in range(2):
        dt_row = _softplus(dtraw[ab] + dt_bias[ab])
        a_row = dt_row * (-jnp.exp(a_log[ab]))
        a_col = to_col(a_row)
        acs_c = jnp.sum(jnp.where(si <= li, jnp.broadcast_to(a_row, (t, q, q)), 0.0), axis=2, keepdims=True)
        acs_r = jnp.sum(jnp.where(li <= si, jnp.broadcast_to(a_col, (t, q, q)), 0.0), axis=1, keepdims=True)
        dt_col.append(to_col(dt_row))
        acs_col.append(acs_c)
        total.append(jnp.sum(a_row, axis=2, keepdims=True))
        lmat.append(jnp.exp(jnp.where(li >= si, acs_c - acs_r, -1e30)))
    cb = by_pair([mm_nt(c_, b_) for c_, b_ in zip(cms, bms, strict=True)])
    bmb, cmb = by_pair(bms), by_pair(cms)
    xdt = x * lanes(dt_col[0], dt_col[1])
    y = lanes(bmm(cb * lmat[0], xdt), bmm(cb * lmat[1], xdt))
    y = y + bmm_nt(cmb, prev) * lanes(jnp.exp(acs_col[0]), jnp.exp(acs_col[1]))
    y = y + lanes(dskip[0], dskip[1]) * x
    state = bmm_tn(xdt * lanes(jnp.exp(total[0] - acs_col[0]), jnp.exp(total[1] - acs_col[1])), bmb)
    return y, jnp.where(first_row, jnp.exp(total[0]), jnp.exp(total[1])) * prev + state


def _group_tiles(ref):
    return jnp.stack([ref[:, PAIR_W * t:PAIR_W * (t + 1)] for t in range(SSD_STEP_PAIRS)])


def _store_group_tiles(ref, val):
    for t in range(SSD_STEP_PAIRS):
        ref[:, PAIR_W * t:PAIR_W * (t + 1)] = val[t]


def _bc_groups(ref):
    return tuple(ref[:, SSM_STATE * i:SSM_STATE * (i + 1)] for i in range(SSD_GROUPS_PER_STEP))


def _by_pair(a):
    return jnp.transpose(a.reshape(SSD_PAIRS, 2, 1, -1), (1, 0, 2, 3))


def _by_head(a):
    return jnp.transpose(a, (1, 0, 2, 3)).reshape(SSM_HEADS, -1)


def _ssd2_specs(chunk_of):
    tp = SSD_STEP_PAIRS
    xspec = pl.BlockSpec((CHUNK, SSD_STEP_W), lambda g, c: (chunk_of(c), g))
    tspec = pl.BlockSpec((2, tp, 1, CHUNK), lambda g, c: (0, g, 0, chunk_of(c)))
    hp = pl.BlockSpec((2, tp, 1, 1), lambda g, c: (0, g, 0, 0))
    gspec = pl.BlockSpec((CHUNK, SSD_GROUPS_PER_STEP * SSM_STATE), lambda g, c: (chunk_of(c), g))
    sspec = pl.BlockSpec((1, tp, PAIR_W, SSM_STATE), lambda g, c: (chunk_of(c), g, 0, 0))
    return xspec, tspec, hp, gspec, sspec


def ssd2_fwd(xs, dtraw_t, dt_bias, a_log, dskip, bm, cm, side=None):
    def body(x_ref, dt_ref, dtb_ref, al_ref, dk_ref, bm_ref, cm_ref, y_ref, prev_ref, state_ref):
        @pl.when(pl.program_id(1) == 0)
        def _():
            state_ref[...] = jnp.zeros_like(state_ref)

        prev = state_ref[...]
        prev_ref[0] = prev
        y, nxt = ssd_pairs(_group_tiles(x_ref), dt_ref[...], dtb_ref[...], al_ref[...], dk_ref[...], _bc_groups(bm_ref),
                           _bc_groups(cm_ref), prev)
        _store_group_tiles(y_ref, y)
        state_ref[...] = nxt

    xspec, tspec, hp, gspec, sspec = _ssd2_specs(lambda c: c)
    return grid_call(
        body, (xs, _by_pair(dtraw_t), _by_pair(dt_bias), _by_pair(a_log), _by_pair(dskip), bm, cm), name="ssd_fwd",
        out_shape=[jax.ShapeDtypeStruct((S, SSM_INNER), F32),
                   jax.ShapeDtypeStruct((N_CHUNKS, SSD_PAIRS, PAIR_W, SSM_STATE), F32)],
        grid=(SSM_GROUPS // SSD_GROUPS_PER_STEP, N_CHUNKS), in_specs=[xspec, tspec, hp, hp, hp, gspec, gspec],
        out_specs=[xspec, sspec],
        scratch_shapes=[pltpu.VMEM((SSD_STEP_PAIRS, PAIR_W, SSM_STATE), F32)],
        semantics=("parallel", "arbitrary"), side=side)


def ssd2_bwd(xs, dtraw_t, dt_bias, a_log, dskip, bm, cm, prev_all, dy, side=None):
    def body(x_ref, dt_ref, dtb_ref, al_ref, dk_ref, bm_ref, cm_ref, prev_ref, dy_ref,
             dx_ref, ddt_ref, ddtb_ref, dal_ref, ddk_ref, dbm_ref, dcm_ref, dstate_ref):
        @pl.when(pl.program_id(1) == 0)
        def _():
            for r in (dstate_ref, ddtb_ref, dal_ref, ddk_ref):
                r[...] = jnp.zeros_like(r)

        _, vjp = jax.vjp(ssd_pairs, _group_tiles(x_ref), dt_ref[...], dtb_ref[...], al_ref[...], dk_ref[...],
                         _bc_groups(bm_ref), _bc_groups(cm_ref), prev_ref[0])
        dx, ddt, ddtb, dal, ddk, dbms, dcms, dprev = vjp((_group_tiles(dy_ref), dstate_ref[...]))
        _store_group_tiles(dx_ref, dx)
        ddt_ref[...] = ddt
        ddtb_ref[...] += ddtb
        dal_ref[...] += dal
        ddk_ref[...] += ddk
        for i in range(SSD_GROUPS_PER_STEP):
            dbm_ref[:, SSM_STATE * i:SSM_STATE * (i + 1)] = dbms[i]
            dcm_ref[:, SSM_STATE * i:SSM_STATE * (i + 1)] = dcms[i]
        dstate_ref[...] = dprev

    xspec, tspec, hp, gspec, sspec = _ssd2_specs(lambda c: N_CHUNKS - 1 - c)
    par = jax.ShapeDtypeStruct((2, SSD_PAIRS, 1, 1), F32)
    res, side_dst = grid_call(
        body, (xs, _by_pair(dtraw_t), _by_pair(dt_bias), _by_pair(a_log), _by_pair(dskip), bm, cm, prev_all, dy),
        name="ssd_bwd",
        out_shape=[jax.ShapeDtypeStruct((S, SSM_INNER), F32), jax.ShapeDtypeStruct((2, SSD_PAIRS, 1, S), F32), par, par, par,
                   jax.ShapeDtypeStruct((S, SSM_GROUPS * SSM_STATE), F32),
                   jax.ShapeDtypeStruct((S, SSM_GROUPS * SSM_STATE), F32)],
        grid=(SSM_GROUPS // SSD_GROUPS_PER_STEP, N_CHUNKS), in_specs=[xspec, tspec, hp, hp, hp, gspec, gspec, sspec, xspec],
        out_specs=[xspec, tspec, hp, hp, hp, gspec, gspec],
        scratch_shapes=[pltpu.VMEM((SSD_STEP_PAIRS, PAIR_W, SSM_STATE), F32)],
        semantics=("parallel", "arbitrary"), side=side)
    return [res[0]] + [_by_head(r) for r in res[1:5]] + list(res[5:]), side_dst


def _silu(x):
    return x * jax.nn.sigmoid(x)


def _rms(x):
    return x * lax.rsqrt(jnp.mean(x * x, -1, keepdims=True) + EPS)


def f_normmod(x, g, sc, sh):
    return (_rms(x) * g * (1.0 + sc) + sh,)


def f_resid(x, mix, gate):
    return (x + gate * mix,)


def f_resid_bias(x, mix, gate, b):
    return (x + gate * (mix + b),)


def f_swiglu(hgu):
    return (_silu(hgu[:, :FFN_HIDDEN]) * hgu[:, FFN_HIDDEN:],)


def f_silu(x):
    return (_silu(x),)


def f_silu_xbc(x):
    y = _silu(x)
    n_b = SSM_GROUPS * SSM_STATE
    return y[:, :SSM_INNER], y[:, SSM_INNER:SSM_INNER + n_b], y[:, SSM_INNER + n_b:]


def f_gated_norm(y, z, g):
    return (_rms(y * _silu(z)) * g,)


def f_glu(y, b):
    y = y + b
    return (y[:, :D] * jax.nn.sigmoid(y[:, D:]),)


def f_ln_silu(u, g, b):
    mu = jnp.mean(u, -1, keepdims=True)
    var = jnp.mean(jnp.square(u - mu), -1, keepdims=True)
    return (_silu((u - mu) * lax.rsqrt(var + EPS) * g + b),)


def f_combine(o1, o2, o3, l1, l2, l3):
    m = lax.stop_gradient(jnp.maximum(jnp.maximum(l1, l2), l3))
    e1, e2, e3 = jnp.exp(l1 - m), jnp.exp(l2 - m), jnp.exp(l3 - m)
    return ((e1 * o1 + e2 * o2 + e3 * o3) / (e1 + e2 + e3),)


def f_head(x, tgt, g):
    return (0.5 * jnp.mean(jnp.square(_rms(x) * g - tgt), -1, keepdims=True),)


def f_sum3(a, b, c):
    return (a + b + c,)


def f_add(a, b):
    return (a + b,)


def f_adamw(w, g, m, v):
    m = ADAM_B1 * m + (1.0 - ADAM_B1) * g
    v = ADAM_B2 * v + (1.0 - ADAM_B2) * jnp.square(g)
    m_hat = m / (1.0 - ADAM_B1 ** ADAM_STEP)
    v_hat = v / (1.0 - ADAM_B2 ** ADAM_STEP)
    return -ADAM_LR * (m_hat / (jnp.sqrt(v_hat) + ADAM_EPS) + ADAM_WD * w), m, v


def _rows_tile(r, cap=256):
    return _pick(r, cap, mult=8)


def adamw(w, g, m, v, *, name):
    l_dim, r_dim, c_dim = w.shape
    tr = _rows_tile(r_dim, cap=128)

    def body(w_ref, g_ref, m_ref, v_ref, d_ref, mo_ref, vo_ref):
        d_ref[...], mo_ref[...], vo_ref[...] = f_adamw(w_ref[...], g_ref[...], m_ref[...], v_ref[...])

    spec = pl.BlockSpec((1, tr, c_dim), lambda l, i: (l, i, 0))
    return pl.pallas_call(
        body, name=name, out_shape=[jax.ShapeDtypeStruct(w.shape, F32)] * 3, grid=(l_dim, r_dim // tr),
        in_specs=[spec] * 4, out_specs=[spec] * 3, compiler_params=_cparams("parallel", "parallel"),
    )(w, g, m, v)


def _t5_bucket(dist):
    max_exact = REL_BUCKETS // 2
    n = jnp.maximum(dist, 1).astype(F32)
    large = max_exact + jnp.log(n / max_exact) / math.log(REL_MAX_DIST / max_exact) * (REL_BUCKETS - max_exact)
    large = jnp.minimum(large.astype(jnp.int32), REL_BUCKETS - 1)
    return jnp.where(dist < max_exact, dist, large)


def _att_buckets(dil):
    i = jnp.arange(ATT_BLK)[:, None]
    j = jnp.arange(2 * ATT_BLK)[None, :]
    bkt = _t5_bucket(jnp.maximum(ATT_BLK + i - j, 0) * dil)
    return jnp.transpose(bkt.reshape(ATT_BLK, 2, ATT_BLK), (1, 0, 2))


def att_bias(rel_table, p, dil):
    tab = rel_table[:, p * ATT_HEADS:(p + 1) * ATT_HEADS]
    onehot = (jnp.arange(REL_BUCKETS)[:, None] == _att_buckets(dil).reshape(1, -1)).astype(F32)
    bias = lax.dot_general(tab, onehot, (((0,), (0,)), ((), ())), precision=lax.Precision.HIGHEST)
    return bias.reshape(ATT_HEADS, 2, ATT_BLK, ATT_BLK)


def att_bias_grad(dbias, dil, *, name):
    onehot = (_att_buckets(dil).reshape(-1, 1) == jnp.arange(LANES)[None, :]).astype(BF16)
    dtab = matmul(dbias.reshape(ATT_HEADS, -1), onehot, mode="nn", out_dtype=F32, name=name, tk_cap=2048)
    return dtab[:, :REL_BUCKETS].T


HY_Z, HY_XBC, HY_DT, HY_Q, HY_K, HY_V = 2048, 3072, 32, 3072, 1024, 1024
HY_IN = HY_Z + HY_XBC + HY_DT + HY_Q + HY_K + HY_V
OFF_Z, OFF_XBC, OFF_Q, OFF_KV, OFF_DT = 0, 2048, 5120, 8192, 10240
HY_CAT = OFF_DT + LANES
DT_PAD = LANES


def hy_to_cat(w):
    z, xbc, dt, qkv = w[:2048], w[2048:5120], w[5120:5152], w[5152:]
    return jnp.concatenate([z, xbc, qkv, dt, jnp.zeros((DT_PAD - HY_DT,) + w.shape[1:], w.dtype)], axis=0)


def hy_from_cat(w, axis=0):
    part = lambda a, b: lax.slice_in_dim(w, a, b, axis=axis)
    return jnp.concatenate([part(0, 5120), part(OFF_DT, OFF_DT + HY_DT), part(5120, OFF_DT)], axis=axis)


def device_step(x, tgt, mods, wts, sp, comm=None):
    g = {}
    dmods = [[None] * 6 for _ in range(2)]
    wts = dict(wts)

    def wgrad(tokens_d, tokens_n, nm):
        return matmul(transpose(tokens_d, name=nm + "_t"), tokens_n, mode="nn", out_dtype=BF16, name=nm, out_t=True,
                      tk_cap=2048)

    def w_side(i):
        return None if comm is None else GatherRows(comm["pack"], comm["full"], *W_BATCHES[i])

    def g_side(i):
        return None if comm is None else ScatterRows(comm["ga"], comm["recv"], *G_BATCHES[i])

    def normmod(xi, gain, sc, sh, nm):
        return rowmap(f_normmod, [xi], [gain, sc, sh], [BF16], name=nm)[0]

    def ffn_fwd(xi, i, gate, nm):
        h = normmod(xi, sp["norm_ffn_g"][i], mods[i][4], mods[i][3], nm + "_norm")
        hgu = matmul(h, wts["gu_t"][i], mode="nt", out_dtype=BF16, name=nm + "_gu")
        out = matmul_swiglu(hgu, wts["down"][i], name=nm + "_down")
        xo = rowmap(f_resid, [xi, out], [gate], [F32], name=nm + "_res")[0]
        return xo, (h, hgu, out)

    def ffn_bwd(dres, xi, i, saved, nm):
        h, hgu, out = saved
        (dout,), (dgate,), _ = rowmap_bwd(f_resid, [xi, out], [mods[i][5]], [dres], name=nm + "_res_b",
                                          row_grad=[False, True], row_dtypes=[BF16])
        dmods[i][5] = dgate
        dact = matmul(dout, wts["down"][i], mode="nt", out_dtype=BF16, name=nm + "_down_dx")
        (dhgu,), _, (act,) = rowmap_bwd(f_swiglu, [hgu], [], [dact], name=nm + "_act_b", row_grad=[True],
                                        row_dtypes=[BF16], tr=128, emit=(0,), emit_dtype=BF16)
        g[f"down{i}"] = wgrad(dout, act, nm + "_down_dw")
        g[f"gu_t{i}"] = wgrad(h, dhgu, nm + "_gu_dw")
        dh = matmul(dhgu, wts["gu_t"][i], mode="nn", out_dtype=F32, name=nm + "_gu_dx")
        (dres,), (dg_, dsc, dsh), _ = rowmap_bwd(f_normmod, [xi], [sp["norm_ffn_g"][i], mods[i][4], mods[i][3]], [dh],
                                                 name=nm + "_norm_b", row_grad=[True], row_add=[dres])
        g[f"norm_ffn_g{i}"] = dg_
        dmods[i][4], dmods[i][3] = dsc, dsh
        return dres

    h0 = normmod(x, sp["norm_mix_g"][0], mods[0][1], mods[0][0], "l0_norm")
    w_in = wts["hy_in_t"]
    z = matmul(h0, w_in, mode="nt", out_dtype=BF16, name="hy_z", n=HY_Z, b_off=OFF_Z)
    xbc_raw = matmul(h0, w_in, mode="nt", out_dtype=BF16, name="hy_xbc", n=HY_XBC, b_off=OFF_XBC)
    q = matmul(h0, w_in, mode="nt", out_dtype=BF16, name="hy_q", n=HY_Q, b_off=OFF_Q)
    kv = matmul(h0, w_in, mode="nt", out_dtype=BF16, name="hy_kv", n=HY_K + HY_V, b_off=OFF_KV)
    dtr = matmul(h0, w_in, mode="nt", out_dtype=F32, name="hy_dt", n=DT_PAD, b_off=OFF_DT)
    xbc_pre = conv_fwd(xbc_raw, sp["hy_conv_w"], sp["hy_conv_b"], name="hy_conv")
    xs, bm, cm = rowmap(f_silu_xbc, [xbc_pre], [], [F32] * 3, name="hy_conv_act", tr=256)
    dtraw_t = dtr[:, :HY_DT].T
    (y, prev_all), full = ssd2_fwd(xs, dtraw_t, sp["hy_dt_bias"], sp["hy_a_log"], sp["hy_d_skip"], bm, cm, side=w_side(1))
    if comm is not None:
        comm["full"] = full
    ysn = rowmap(f_gated_norm, [y, z], [sp["hy_ssm_norm_g"]], [BF16], name="hy_gnorm", tr=128)[0]
    att_in, att_o, att_l = [], [], []
    for p, (win, dil) in enumerate(ATT_PATTERNS):
        if dil == 1:
            qa, ka, va, cols = q, kv, kv, (p, 0, 1)
        else:
            qa, ka, cols = regroup(q[:, p * D:(p + 1) * D], dil), regroup(kv, dil), (0, 0, 1)
            va = ka
        bias = pair_bias(att_bias(sp["rel_table"], p, dil))
        nb = S // dil // ATT_BLK
        (o, lse), full = att2_fwd(qa, ka, va, bias, nb, cols, name=f"att_fwd{p}", side=w_side(2 + p))
        if comm is not None:
            comm["full"] = full
        att_in.append((qa, ka, va, bias, nb, cols))
        att_o.append(regroup(o, dil, inverse=True))
        att_l.append(regroup(lse, dil, inverse=True))
    if comm is not None:
        wts.update(unpack_weights(comm["full"], skip=("hy_in_t",)))
    att = rowmap(f_combine, att_o + att_l, [], [BF16], name="att_combine", tr=256)[0]
    cat = jnp.concatenate([ysn, att], axis=-1)
    mix0 = matmul(cat, wts["hy_out"], mode="nn", out_dtype=F32, name="hy_out")
    x1 = rowmap(f_resid, [x, mix0], [mods[0][2]], [F32], name="l0_res")[0]
    x2, ffn0 = ffn_fwd(x1, 0, mods[0][5], "ffn0")

    h1 = normmod(x2, sp["norm_mix_g"][1], mods[1][1], mods[1][0], "l1_norm")
    p1 = matmul(h1, wts["pw1_t"], mode="nt", out_dtype=BF16, name="cv_pw1")
    u = rowmap(f_glu, [p1], [sp["cv_b_pw1"]], [F32], name="cv_glu")[0]
    uc = conv_fwd(u, sp["cv_w_dw"], sp["cv_b_dw"], name="cv_conv")
    ul = rowmap(f_ln_silu, [uc], [sp["cv_ln_g"], sp["cv_ln_b"]], [BF16], name="cv_ln")[0]
    mix1 = matmul(ul, wts["pw2"], mode="nn", out_dtype=F32, name="cv_pw2")
    x3 = rowmap(f_resid_bias, [x2, mix1], [mods[1][2], sp["cv_b_pw2"]], [F32], name="l1_res")[0]
    x4, ffn1 = ffn_fwd(x3, 1, mods[1][5], "ffn1")

    ones = jnp.ones((S, 1), F32)
    (dres,), (dfinal,), (loss_rows,) = rowmap_bwd(f_head, [x4, tgt], [sp["final_norm_g"]], [ones], name="head",
                                                  row_grad=[True, False], emit=(0,))
    g["final_norm_g"] = dfinal

    dres = ffn_bwd(dres, x3, 1, ffn1, "ffn1")
    (dmix1,), (dg1, db2), _ = rowmap_bwd(f_resid_bias, [x2, mix1], [mods[1][2], sp["cv_b_pw2"]], [dres], name="l1_res_b",
                                         row_grad=[False, True], row_dtypes=[BF16])
    dmods[1][2] = dg1
    g["cv_b_pw2"] = db2
    dul = matmul(dmix1, wts["pw2"], mode="nt", out_dtype=BF16, name="cv_pw2_dx")
    g["pw2"] = wgrad(dmix1, ul, "cv_pw2_dw")
    (duc,), (g["cv_ln_g"], g["cv_ln_b"]), _ = rowmap_bwd(f_ln_silu, [uc], [sp["cv_ln_g"], sp["cv_ln_b"]], [dul],
                                                         name="cv_ln_b", row_grad=[True])
    du, g["cv_w_dw"], g["cv_b_dw"] = conv_bwd(u, sp["cv_w_dw"], duc, name="cv_conv_b", cb=128, chunk_rows=128)
    (dp1,), (g["cv_b_pw1"],), _ = rowmap_bwd(f_glu, [p1], [sp["cv_b_pw1"]], [du], name="cv_glu_b", row_grad=[True],
                                             row_dtypes=[BF16])
    g["pw1_t"] = wgrad(h1, dp1, "cv_pw1_dw")
    dh1 = matmul(dp1, wts["pw1_t"], mode="nn", out_dtype=F32, name="cv_pw1_dx")
    (dres,), (dg_, dsc, dsh), _ = rowmap_bwd(f_normmod, [x2], [sp["norm_mix_g"][1], mods[1][1], mods[1][0]], [dh1],
                                             name="l1_norm_b", row_grad=[True], row_add=[dres])
    g["norm_mix_g1"] = dg_
    dmods[1][1], dmods[1][0] = dsc, dsh

    dres = ffn_bwd(dres, x1, 0, ffn0, "ffn0")
    (dmix0,), (dg1,), _ = rowmap_bwd(f_resid, [x, mix0], [mods[0][2]], [dres], name="l0_res_b",
                                     row_grad=[False, True], row_dtypes=[BF16])
    dmods[0][2] = dg1
    dysn = matmul(dmix0, wts["hy_out"], mode="nt", out_dtype=BF16, name="hy_out_dy", n=SSM_INNER, b_off=0)
    datt = matmul(dmix0, wts["hy_out"], mode="nt", out_dtype=BF16, name="hy_out_da", n=D, b_off=SSM_INNER)
    g["hy_out"] = wgrad(dmix0, cat, "hy_out_dw")
    (dy, dz), (g["hy_ssm_norm_g"],), _ = rowmap_bwd(f_gated_norm, [y, z], [sp["hy_ssm_norm_g"]], [dysn], name="hy_gnorm_b",
                                                    row_grad=[True, True], row_dtypes=[F32, BF16], tr=256)
    if comm is not None:
        comm["ga"] = pack_grads(g, GA_LAYOUT, GA_ROWS)
        comm["recv"] = lax.empty((3, GA_ROWS, D), BF16)
    (dxs, ddtraw_t, g["hy_dt_bias"], g["hy_a_log"], g["hy_d_skip"], dbm, dcm), recv = ssd2_bwd(
        xs, dtraw_t, sp["hy_dt_bias"], sp["hy_a_log"], sp["hy_d_skip"], bm, cm, prev_all, dy, side=g_side(0))
    if comm is not None:
        comm["recv"] = recv
    (dxbc_pre,), _, _ = rowmap_bwd(f_silu_xbc, [xbc_pre], [], [dxs, dbm, dcm], name="hy_conv_act_b", row_grad=[True],
                                   tr=128)
    dxbc_raw, g["hy_conv_w"], g["hy_conv_b"] = conv_bwd(xbc_raw, sp["hy_conv_w"], dxbc_pre, name="hy_conv_b", cb=128, chunk_rows=128, dx_dtype=BF16)
    dol, _, _ = rowmap_bwd(f_combine, att_o + att_l, [], [datt], name="att_combine_b", row_grad=[True] * 6,
                           row_dtypes=[BF16] * 3 + [F32] * 3, tr=256)
    dqs, dks, dvs, dtabs = [], [], [], []
    for p, (win, dil) in enumerate(ATT_PATTERNS):
        qa, ka, va, bias, nb, cols = att_in[p]
        (dq, dkp_, dvp_, dbias), recv = att2_bwd(qa, ka, va, bias, regroup(dol[p], dil), regroup(dol[3 + p], dil), nb,
                                                 cols, name=f"att_bwd{p}", side=g_side(1 + p))
        if comm is not None:
            comm["recv"] = recv
        dqs.append(regroup(dq, dil, inverse=True))
        dks.append(regroup(dkp_, dil, inverse=True))
        dvs.append(regroup(dvp_, dil, inverse=True))
        dtabs.append(att_bias_grad(dbias.reshape(ATT_HEADS, 2, ATT_BLK, ATT_BLK), dil, name=f"att_dtab{p}"))
    g["rel_table"] = jnp.concatenate(dtabs, axis=1)
    dk = rowmap(f_sum3, dks, [], [BF16], name="att_dk_sum")[0]
    dv = rowmap(f_sum3, dvs, [], [BF16], name="att_dv_sum")[0]
    ddt = jnp.pad(ddtraw_t.T, ((0, 0), (0, DT_PAD - HY_DT)))
    dproj = jnp.concatenate([dz, dxbc_raw] + dqs + [dk, dv, ddt.astype(BF16)], axis=-1)
    g["hy_in_t"] = wgrad(h0, dproj, "hy_in_dw")
    if comm is None:
        dh0 = matmul(dproj, w_in, mode="nn", out_dtype=F32, name="hy_in_dx")
    else:
        gb = pack_grads(g, GB_LAYOUT, GB_ROWS)
        half = GB_ROWS // 2
        theirs = swap_halves(gb, name="swap_in_halves")
        ours = lax.dynamic_slice_in_dim(gb, lax.axis_index("c") * half, half, axis=1)
        comm["gb"] = rowmap(f_add, [ours.reshape(N_CHIPS * half, D), theirs.reshape(N_CHIPS * half, D)], [], [BF16],
                            name="sum_in_cores")[0].reshape(N_CHIPS, half, D)
        dh0, comm["recv_b"] = matmul(dproj, w_in, mode="nn", out_dtype=F32, name="hy_in_dx",
                                     side=ScatterRows(comm["gb"], lax.empty((3, half, D), BF16), 0, half))
    (dres,), (dg_, dsc, dsh), _ = rowmap_bwd(f_normmod, [x], [sp["norm_mix_g"][0], mods[0][1], mods[0][0]], [dh0],
                                             name="l0_norm_b", row_grad=[True], row_add=[dres])
    g["norm_mix_g0"] = dg_
    dmods[0][1], dmods[0][0] = dsc, dsh
    return loss_rows, dres, g, dmods


ANY = pl.BlockSpec(memory_space=pl.ANY)
WHOLE_VMEM = pl.BlockSpec(memory_space=pltpu.VMEM)


def _place():
    return lax.axis_index("x"), lax.axis_index("y"), lax.axis_index("c")


def _other_chips(x, y):
    return [(1 - x, y), (x, 1 - y), (1 - x, 1 - y)]


def allgather_small(v, *, name, side=None):
    m_per = v.shape[0]

    def gather(x_ref, out_ref, send_sems, recv_sems, local_sem):
        x, y, c = _place()
        me, sibling = (x, y, c), (x, y, 1 - c)
        chips = _other_chips(x, y)

        def rows(px, py, pc):
            return out_ref.at[pl.ds((4 * px + 2 * py + pc) * m_per, m_per), :]

        def copy(k, block, to, src=None):
            return pltpu.make_async_remote_copy(
                src_ref=rows(*block) if src is None else src, dst_ref=rows(*block),
                send_sem=send_sems.at[k], recv_sem=recv_sems.at[k], device_id=to, device_id_type=MESH)

        mine = pltpu.make_async_copy(x_ref, rows(*me), local_sem)
        mine.start()
        first = [copy(0, me, sibling, src=x_ref)]
        first += [copy(1 + j, me, (*chip, c), src=x_ref) for j, chip in enumerate(chips)]
        for cp in first:
            cp.start()
        passed = [copy(4 + j, (*chip, c), sibling) for j, chip in enumerate(chips)]
        for j, chip in enumerate(chips):
            copy(1 + j, (*chip, c), me).wait_recv()
            passed[j].start()
        copy(0, sibling, me).wait_recv()
        for j, chip in enumerate(chips):
            copy(4 + j, (*chip, 1 - c), me).wait_recv()
        for cp in first + passed:
            cp.wait_send()
        mine.wait()

    out = jax.ShapeDtypeStruct((N_DEV * m_per, LANES), v.dtype)
    sems = [pltpu.SemaphoreType.DMA((7,)), pltpu.SemaphoreType.DMA((7,)), pltpu.SemaphoreType.DMA]
    if side is None:
        return pl.pallas_call(gather, name=name, out_shape=out, in_specs=[WHOLE_VMEM], out_specs=WHOLE_VMEM,
                              scratch_shapes=sems)(v)

    def body(x_ref, src_ref, dst_in_ref, out_ref, dst_ref, send_sems, recv_sems, local_sem, *side_sems):
        side.start(src_ref, dst_ref, side_sems)
        gather(x_ref, out_ref, send_sems, recv_sems, local_sem)
        side.finish(src_ref, dst_ref, side_sems)

    return pl.pallas_call(
        body, name=name, out_shape=[out, jax.ShapeDtypeStruct(side.dst.shape, side.dst.dtype)],
        in_specs=[WHOLE_VMEM, ANY, ANY], out_specs=[WHOLE_VMEM, ANY], scratch_shapes=sems + side.sems(),
        input_output_aliases={2: 1},
    )(v, side.src, side.dst)


def swap_halves(gpack, *, name):
    half_rows = gpack.shape[1] // 2

    def body(g_ref, r_ref, send_sems, recv_sems):
        x, y, c = _place()
        its_half = pl.ds((1 - c) * half_rows, half_rows)
        copies = [pltpu.make_async_remote_copy(
            src_ref=g_ref.at[s, its_half], dst_ref=r_ref.at[s], send_sem=send_sems.at[s], recv_sem=recv_sems.at[s],
            device_id=(x, y, 1 - c), device_id_type=MESH) for s in range(N_CHIPS)]
        for cp in copies:
            cp.start()
        for cp in copies:
            cp.wait()

    return pl.pallas_call(
        body, name=name,
        out_shape=jax.ShapeDtypeStruct((N_CHIPS, half_rows) + gpack.shape[2:], gpack.dtype),
        in_specs=[ANY], out_specs=ANY,
        scratch_shapes=[pltpu.SemaphoreType.DMA((N_CHIPS,)), pltpu.SemaphoreType.DMA((N_CHIPS,))],
    )(gpack)


class GatherRows:
    def __init__(self, pack, full, lo, hi):
        assert (hi - lo) % 32 == 0 and lo % 16 == 0
        self.src, self.dst, self.lo, self.hi = pack, full, lo, hi

    def sems(self):
        return [pltpu.SemaphoreType.DMA((6,)), pltpu.SemaphoreType.DMA((6,)), pltpu.SemaphoreType.DMA]

    def _parts(self, pack_ref, full_ref, sems):
        send_sems, recv_sems, local_sem = sems
        x, y, c = _place()
        half = (self.hi - self.lo) // 2
        mine, its = pl.ds(self.lo + c * half, half), pl.ds(self.lo + (1 - c) * half, half)
        rows = pl.ds(self.lo, self.hi - self.lo)
        local = pltpu.make_async_copy(pack_ref.at[rows], full_ref.at[2 * x + y, rows], local_sem)
        chips = _other_chips(x, y)

        def remote(src, dst, k, to):
            return pltpu.make_async_remote_copy(src_ref=src, dst_ref=dst, send_sem=send_sems.at[k],
                                                recv_sem=recv_sems.at[k], device_id=to, device_id_type=MESH)

        sends = [remote(pack_ref.at[mine], full_ref.at[2 * x + y, mine], k, (cx, cy, c)) for k, (cx, cy) in enumerate(chips)]
        landed = [full_ref.at[2 * cx + cy, mine] for cx, cy in chips]
        arrive = [remote(pack_ref.at[mine], landed[k], k, (cx, cy, c)) for k, (cx, cy) in enumerate(chips)]
        passed = [remote(landed[k], landed[k], 3 + k, (x, y, 1 - c)) for k in range(3)]
        from_sibling = [remote(landed[k], full_ref.at[2 * cx + cy, its], 3 + k, (x, y, 1 - c))
                        for k, (cx, cy) in enumerate(chips)]
        return local, sends, arrive, passed, from_sibling

    def start(self, pack_ref, full_ref, sems):
        local, sends, _, _, _ = self._parts(pack_ref, full_ref, sems)
        local.start()
        for cp in sends:
            cp.start()

    def finish(self, pack_ref, full_ref, sems):
        local, sends, arrive, passed, from_sibling = self._parts(pack_ref, full_ref, sems)
        for k in range(3):
            arrive[k].wait_recv()
            passed[k].start()
        for cp in from_sibling:
            cp.wait_recv()
        for cp in sends + passed:
            cp.wait_send()
        local.wait()


class ScatterRows:
    def __init__(self, gpack, recv, lo, hi):
        assert lo % 16 == 0 and hi % 16 == 0
        self.src, self.dst, self.lo, self.hi = gpack, recv, lo, hi

    def sems(self):
        return [pltpu.SemaphoreType.DMA((3,)), pltpu.SemaphoreType.DMA((3,))]

    def _parts(self, g_ref, recv_ref, sems):
        send_sems, recv_sems = sems
        x, y, c = _place()
        rows = pl.ds(self.lo, self.hi - self.lo)
        return [pltpu.make_async_remote_copy(
            src_ref=g_ref.at[2 * cx + cy, rows], dst_ref=recv_ref.at[k, rows], send_sem=send_sems.at[k],
            recv_sem=recv_sems.at[k], device_id=(cx, cy, c), device_id_type=MESH)
            for k, (cx, cy) in enumerate(_other_chips(x, y))]

    def start(self, g_ref, recv_ref, sems):
        for cp in self._parts(g_ref, recv_ref, sems):
            cp.start()

    def finish(self, g_ref, recv_ref, sems):
        sends = self._parts(g_ref, recv_ref, sems)
        for cp in sends:
            cp.wait_recv()
        for cp in sends:
            cp.wait_send()


def side_call(side, *, name):
    def body(src_ref, dst_in_ref, dst_ref, *sems):
        side.start(src_ref, dst_ref, sems)
        side.finish(src_ref, dst_ref, sems)

    return pl.pallas_call(
        body, name=name, out_shape=jax.ShapeDtypeStruct(side.dst.shape, side.dst.dtype),
        in_specs=[ANY, ANY], out_specs=ANY, scratch_shapes=side.sems(), input_output_aliases={1: 0},
    )(side.src, side.dst)


def grid_call(body, args, *, name, out_shape, grid, in_specs, out_specs, scratch_shapes, semantics, side=None):
    if side is None:
        res = pl.pallas_call(body, name=name, out_shape=out_shape, grid=grid, in_specs=in_specs, out_specs=out_specs,
                             scratch_shapes=scratch_shapes, compiler_params=_cparams(*semantics))(*args)
        return res, None
    n_in, n_out, n_scr = len(args), len(out_shape), len(scratch_shapes)

    def wrapped(*refs):
        ins, (src_ref, _) = refs[:n_in], refs[n_in:n_in + 2]
        outs, dst_ref = refs[n_in + 2:n_in + 2 + n_out], refs[n_in + 2 + n_out]
        scr, sems = refs[n_in + 3 + n_out:n_in + 3 + n_out + n_scr], refs[n_in + 3 + n_out + n_scr:]
        first = functools.reduce(jnp.logical_and, [pl.program_id(i) == 0 for i in range(len(grid))])
        last = functools.reduce(jnp.logical_and, [pl.program_id(i) == n - 1 for i, n in enumerate(grid)])

        @pl.when(first)
        def _():
            side.start(src_ref, dst_ref, sems)

        body(*ins, *outs, *scr)

        @pl.when(last)
        def _():
            side.finish(src_ref, dst_ref, sems)

    res = pl.pallas_call(
        wrapped, name=name,
        out_shape=list(out_shape) + [jax.ShapeDtypeStruct(side.dst.shape, side.dst.dtype)],
        grid=grid, in_specs=list(in_specs) + [ANY, ANY], out_specs=list(out_specs) + [ANY],
        scratch_shapes=list(scratch_shapes) + side.sems(), input_output_aliases={n_in + 1: n_out},
        compiler_params=_cparams(*(["arbitrary"] * len(grid))),
    )(*args, side.src, side.dst)
    return res[:-1], res[-1]


def sibling_swap(p, *, name):
    def body(p_ref, r_ref, send_sem, recv_sem):
        x, y, c = _place()
        cp = pltpu.make_async_remote_copy(src_ref=p_ref, dst_ref=r_ref, send_sem=send_sem, recv_sem=recv_sem,
                                          device_id=(x, y, 1 - c), device_id_type=MESH)
        cp.start()
        cp.wait()

    return pl.pallas_call(
        body, name=name, out_shape=jax.ShapeDtypeStruct(p.shape, p.dtype),
        in_specs=[ANY], out_specs=ANY,
        scratch_shapes=[pltpu.SemaphoreType.DMA, pltpu.SemaphoreType.DMA],
    )(p)


def sum_slots(own, recv, *, name):
    r_dim, c_dim = own.shape
    tr = _pick(r_dim, 256, mult=16)

    def body(o_ref, r_ref, out_ref):
        acc = o_ref[...].astype(F32)
        for k in range(3):
            acc = acc + r_ref[k].astype(F32)
        out_ref[...] = acc

    return pl.pallas_call(
        body, name=name, out_shape=jax.ShapeDtypeStruct((r_dim, c_dim), F32), grid=(r_dim // tr,),
        in_specs=[pl.BlockSpec((tr, c_dim), lambda i: (i, 0)), pl.BlockSpec((3, tr, c_dim), lambda i: (0, i, 0))],
        out_specs=pl.BlockSpec((tr, c_dim), lambda i: (i, 0)),
        compiler_params=_cparams("parallel"),
    )(own, recv)


def sum_devices(v_all, *, name):
    m_per = v_all.shape[0] // N_DEV

    def body(v_ref, o_ref):
        acc = v_ref[pl.ds(0, m_per), :]
        for d in range(1, N_DEV):
            acc = acc + v_ref[pl.ds(d * m_per, m_per), :]
        o_ref[...] = acc

    return pl.pallas_call(
        body, name=name, out_shape=jax.ShapeDtypeStruct((m_per, LANES), F32),
        in_specs=[WHOLE_VMEM], out_specs=WHOLE_VMEM,
    )(v_all)


WEIGHTS = ['ada_w', 'ada_b', 'norm_mix_g', 'norm_ffn_g', 'hy_w_in', 'hy_conv_w', 'hy_conv_b', 'hy_dt_bias', 'hy_a_log',
           'hy_d_skip', 'hy_ssm_norm_g', 'hy_w_out', 'rel_table', 'cv_w_pw1', 'cv_b_pw1', 'cv_w_dw', 'cv_b_dw', 'cv_ln_g',
           'cv_ln_b', 'cv_w_pw2', 'cv_b_pw2', 'ffn_w_gate', 'ffn_w_up', 'ffn_w_down', 'final_norm_g']
BIG = ('ada_w', 'hy_w_in', 'hy_w_out', 'cv_w_pw1', 'cv_w_pw2', 'ffn_w_gate', 'ffn_w_up', 'ffn_w_down')
SMALL_SHARDED = {'hy_conv_w': (1, 4, 3072), 'cv_b_pw1': (1, 2048), 'cv_w_dw': (1, 31, 1024), 'cv_b_dw': (1, 1024),
                 'cv_ln_g': (1, 1024), 'cv_ln_b': (1, 1024), 'cv_b_pw2': (1, 1024)}
SMALL_GRADS = {'ada_b': (2, 6144), 'norm_mix_g': (2, 1024), 'norm_ffn_g': (2, 1024), 'hy_conv_w': (1, 4, 3072),
               'hy_conv_b': (1, 3072), 'hy_dt_bias': (1, 32), 'hy_a_log': (1, 32), 'hy_d_skip': (1, 32),
               'hy_ssm_norm_g': (1, 2048), 'rel_table': (32, 48), 'cv_b_pw1': (1, 2048), 'cv_w_dw': (1, 31, 1024),
               'cv_b_dw': (1, 1024), 'cv_ln_g': (1, 1024), 'cv_ln_b': (1, 1024), 'cv_b_pw2': (1, 1024),
               'final_norm_g': (1024,), 'loss': (1,)}

PACK_LAYOUT = (('hy_in_t', 2568), ('hy_out', 768), ('pw1_t', 512), ('pw2', 256),
               ('gate_t0', 704), ('up_t0', 704), ('down0', 704), ('gate_t1', 704), ('up_t1', 704), ('down1', 704))
PACK_ROWS = 8448


def _pack_offsets(layout):
    off, out = 0, {}
    for nm, r in layout:
        out[nm] = (off, r)
        off += r
    return out


PACK_OFF = _pack_offsets(PACK_LAYOUT)
W_BATCHES = ((0, 2624), (2624, 4992), (4992, 6144), (6144, 7296), (7296, 8448))
GA_LAYOUT = PACK_LAYOUT[1:]
GA_ROWS = 5888
GA_OFF = _pack_offsets(GA_LAYOUT)
G_BATCHES = ((0, 2560), (2560, 3712), (3712, 4864), (4864, 5888))
GB_LAYOUT = PACK_LAYOUT[:1]
GB_ROWS = 2816


def pack_grads(g, layout, n_rows):
    def rows_bf16(nm):
        return g[nm]

    parts = []
    for key, r in layout:
        if key == 'hy_in_t':
            a = hy_from_cat(rows_bf16('hy_in_t'))
        elif key.startswith('gate_t'):
            a = rows_bf16('gu_t' + key[-1])[:FFN_HIDDEN]
        elif key.startswith('up_t'):
            a = rows_bf16('gu_t' + key[-1])[FFN_HIDDEN:]
        else:
            a = rows_bf16(key)
        parts.append(a.reshape(N_CHIPS, r, D))
    used = sum(r for _, r in layout)
    return jnp.concatenate(parts + [jnp.zeros((N_CHIPS, n_rows - used, D), BF16)], axis=1)


def unpack_weights(full, skip=()):
    def whole(nm):
        o, r = PACK_OFF[nm]
        return full[:, o:o + r].reshape(N_CHIPS * r, D)

    out = {"hy_out": whole('hy_out'), "pw1_t": whole('pw1_t'), "pw2": whole('pw2'),
           "gu_t": [jnp.concatenate([whole(f'gate_t{i}'), whole(f'up_t{i}')], axis=0) for i in range(2)],
           "down": [whole(f'down{i}') for i in range(2)]}
    if "hy_in_t" not in skip:
        out["hy_in_t"] = hy_to_cat(whole('hy_in_t'))
    return out


def _to_lanes(flat):
    n = flat.shape[0]
    m = -(-n // (8 * LANES)) * 8
    return jnp.pad(flat, (0, m * LANES - n)).reshape(m, LANES)


def _split(flat, shapes):
    out, off = {}, 0
    for nm, shp in shapes.items():
        n = int(np.prod(shp))
        out[nm] = flat[off:off + n].reshape(shp)
        off += n
    return out


def kernel(x, c, ada_w, ada_b, norm_mix_g, norm_ffn_g, hy_w_in, hy_conv_w, hy_conv_b, hy_dt_bias, hy_a_log, hy_d_skip, hy_ssm_norm_g, hy_w_out, rel_table, cv_w_pw1, cv_b_pw1, cv_w_dw, cv_b_dw, cv_ln_g, cv_ln_b, cv_w_pw2, cv_b_pw2, ffn_w_gate, ffn_w_up, ffn_w_down, final_norm_g, loss_target, m_ada_w, m_ada_b, m_norm_mix_g, m_norm_ffn_g, m_hy_w_in, m_hy_conv_w, m_hy_conv_b, m_hy_dt_bias, m_hy_a_log, m_hy_d_skip, m_hy_ssm_norm_g, m_hy_w_out, m_rel_table, m_cv_w_pw1, m_cv_b_pw1, m_cv_w_dw, m_cv_b_dw, m_cv_ln_g, m_cv_ln_b, m_cv_w_pw2, m_cv_b_pw2, m_ffn_w_gate, m_ffn_w_up, m_ffn_w_down, m_final_norm_g, v_ada_w, v_ada_b, v_norm_mix_g, v_norm_ffn_g, v_hy_w_in, v_hy_conv_w, v_hy_conv_b, v_hy_dt_bias, v_hy_a_log, v_hy_d_skip, v_hy_ssm_norm_g, v_hy_w_out, v_rel_table, v_cv_w_pw1, v_cv_b_pw1, v_cv_w_dw, v_cv_b_dw, v_cv_ln_g, v_cv_ln_b, v_cv_w_pw2, v_cv_b_pw2, v_ffn_w_gate, v_ffn_w_up, v_ffn_w_down, v_final_norm_g):
    args = (x, c, ada_w, ada_b, norm_mix_g, norm_ffn_g, hy_w_in, hy_conv_w, hy_conv_b, hy_dt_bias, hy_a_log, hy_d_skip, hy_ssm_norm_g, hy_w_out, rel_table, cv_w_pw1, cv_b_pw1, cv_w_dw, cv_b_dw, cv_ln_g, cv_ln_b, cv_w_pw2, cv_b_pw2, ffn_w_gate, ffn_w_up, ffn_w_down, final_norm_g, loss_target, m_ada_w, m_ada_b, m_norm_mix_g, m_norm_ffn_g, m_hy_w_in, m_hy_conv_w, m_hy_conv_b, m_hy_dt_bias, m_hy_a_log, m_hy_d_skip, m_hy_ssm_norm_g, m_hy_w_out, m_rel_table, m_cv_w_pw1, m_cv_b_pw1, m_cv_w_dw, m_cv_b_dw, m_cv_ln_g, m_cv_ln_b, m_cv_w_pw2, m_cv_b_pw2, m_ffn_w_gate, m_ffn_w_up, m_ffn_w_down, m_final_norm_g, v_ada_w, v_ada_b, v_norm_mix_g, v_norm_ffn_g, v_hy_w_in, v_hy_conv_w, v_hy_conv_b, v_hy_dt_bias, v_hy_a_log, v_hy_d_skip, v_hy_ssm_norm_g, v_hy_w_out, v_rel_table, v_cv_w_pw1, v_cv_b_pw1, v_cv_w_dw, v_cv_b_dw, v_cv_ln_g, v_cv_ln_b, v_cv_w_pw2, v_cv_b_pw2, v_ffn_w_gate, v_ffn_w_up, v_ffn_w_down, v_final_norm_g)
    x_in, c_in = args[0], args[1]
    w = dict(zip(WEIGHTS, args[2:27], strict=True))
    tgt = args[27]
    m_in = dict(zip(WEIGHTS, args[28:53], strict=True))
    v_in = dict(zip(WEIGHTS, args[53:78], strict=True))
    xi, yi, ci = _place()
    chip = 2 * xi + yi
    dev = 2 * chip + ci

    cs = rowmap(f_silu, [c_in.reshape(8, LANES)], [], [F32], name="cond_silu", tr=8)[0]
    cs_all = allgather_small(cs, name="gather_cond").reshape(N_DEV, D)
    cs16 = jnp.pad(cs_all, ((0, 8), (0, 0)))
    modpart = jnp.stack([matmul(cs16, w['ada_w'][i], mode="nn", out_dtype=F32, name=f"ada_fwd{i}")[:N_DEV]
                         for i in range(2)], axis=1)
    def rows_of(nm, i=None):
        a = w[nm][0 if i is None else i]
        return (a.T if nm in ('hy_w_in', 'cv_w_pw1', 'ffn_w_gate', 'ffn_w_up') else a).astype(BF16)

    pieces = [rows_of('hy_w_in'), rows_of('hy_w_out'), rows_of('cv_w_pw1'), rows_of('cv_w_pw2')]
    for i in range(2):
        pieces += [rows_of('ffn_w_gate', i), rows_of('ffn_w_up', i), rows_of('ffn_w_down', i)]
    n_rows = sum(p.shape[0] for p in pieces)
    pack = jnp.concatenate(pieces + [jnp.zeros((PACK_ROWS - n_rows, D), BF16)], axis=0)

    shard_names = list(SMALL_SHARDED)
    payload = jnp.concatenate([modpart.reshape(-1)] + [w[nm].reshape(-1) for nm in shard_names])
    got, full = allgather_small(_to_lanes(payload), name="gather_mod",
                                side=GatherRows(pack, lax.empty((N_CHIPS, PACK_ROWS, D), BF16), *W_BATCHES[0]))
    got = got.reshape(N_DEV, -1)[0::2]
    modparts = got[:, :modpart.size].reshape(N_CHIPS, N_DEV, 2, 1536)
    mine = lax.dynamic_index_in_dim(modparts, dev, axis=1, keepdims=False)
    mod = jnp.transpose(mine, (1, 0, 2)).reshape(2, 6 * D) + w['ada_b']
    mods = [[mod[i, j * D:(j + 1) * D].reshape(1, D) for j in range(6)] for i in range(2)]
    sp = {}
    off = modpart.size
    for nm in shard_names:
        shp = w[nm].shape
        n = int(np.prod(shp))
        parts = got[:, off:off + n].reshape((N_CHIPS,) + shp)
        sp[nm] = jnp.concatenate([parts[s] for s in range(N_CHIPS)], axis=-1)
        off += n

    o_in, r_in = PACK_OFF['hy_in_t']
    wts = {"hy_in_t": hy_to_cat(full[:, o_in:o_in + r_in].reshape(N_CHIPS * r_in, D))}
    comm = {"pack": pack, "full": full}

    sp = {"norm_mix_g": [w['norm_mix_g'][i].reshape(1, D) for i in range(2)],
          "norm_ffn_g": [w['norm_ffn_g'][i].reshape(1, D) for i in range(2)],
          "hy_conv_w": sp['hy_conv_w'][0], "hy_conv_b": w['hy_conv_b'],
          "hy_dt_bias": w['hy_dt_bias'].reshape(SSM_HEADS, 1), "hy_a_log": w['hy_a_log'].reshape(SSM_HEADS, 1),
          "hy_d_skip": w['hy_d_skip'].reshape(SSM_HEADS, 1), "hy_ssm_norm_g": w['hy_ssm_norm_g'],
          "rel_table": w['rel_table'], "cv_b_pw1": sp['cv_b_pw1'], "cv_w_dw": sp['cv_w_dw'][0], "cv_b_dw": sp['cv_b_dw'],
          "cv_ln_g": sp['cv_ln_g'], "cv_ln_b": sp['cv_ln_b'], "cv_b_pw2": sp['cv_b_pw2'],
          "final_norm_g": w['final_norm_g'].reshape(1, D)}

    loss_rows, grad_x, g, dmods = device_step(x_in[0], tgt[0], mods, wts, sp, comm)

    dmod = jnp.stack([jnp.concatenate([d.reshape(-1) for d in dmods[i]]) for i in range(2)])
    small = {'ada_b': dmod, 'norm_mix_g': jnp.stack([g[f'norm_mix_g{i}'].reshape(-1) for i in range(2)]),
             'norm_ffn_g': jnp.stack([g[f'norm_ffn_g{i}'].reshape(-1) for i in range(2)]),
             'loss': jnp.sum(loss_rows).reshape(1)}
    for nm in SMALL_GRADS:
        if nm not in small:
            small[nm] = g[nm]
    vec = _to_lanes(jnp.concatenate([small[nm].reshape(-1) for nm in SMALL_GRADS]))
    vec_all = allgather_small(vec, name="gather_small_grads")
    tot = _split(sum_devices(vec_all, name="sum_small_grads").reshape(-1), SMALL_GRADS)
    dmod_all = vec_all.reshape(N_DEV, -1)[:, :2 * 6 * D].reshape(N_DEV, 2, 6 * D)

    recv = comm["recv"]
    own_a = lax.dynamic_index_in_dim(comm["ga"], chip, axis=0, keepdims=False)
    part_a = sum_slots(own_a, recv, name="sum_chip_grads")
    red_a = rowmap(f_add, [part_a, sibling_swap(part_a, name="swap_grads")], [], [F32], name="sum_core_grads")[0]
    recv_b = comm["recv_b"]
    own_b = lax.dynamic_index_in_dim(comm["gb"], chip, axis=0, keepdims=False)
    mine_half = sum_slots(own_b, recv_b, name="sum_in_chips")
    its_half = sibling_swap(mine_half, name="swap_in")
    red_b = jnp.concatenate([jnp.where(ci == 0, mine_half, its_half), jnp.where(ci == 0, its_half, mine_half)], axis=0)

    def shard_grad(nm, i=None):
        key = {'hy_w_in': 'hy_in_t', 'hy_w_out': 'hy_out', 'cv_w_pw1': 'pw1_t', 'cv_w_pw2': 'pw2'}.get(nm)
        if key is None:
            key = {'ffn_w_gate': 'gate_t', 'ffn_w_up': 'up_t', 'ffn_w_down': 'down'}[nm] + str(i)
        if key == 'hy_in_t':
            a = red_b[:PACK_OFF[key][1]]
        else:
            o, r = GA_OFF[key]
            a = red_a[o:o + r]
        return a.T if key.endswith('_t') or key[:-1].endswith('_t') else a

    grads = {}
    grads['hy_w_in'] = shard_grad('hy_w_in')[None]
    grads['hy_w_out'] = shard_grad('hy_w_out')[None]
    grads['cv_w_pw1'] = shard_grad('cv_w_pw1')[None]
    grads['cv_w_pw2'] = shard_grad('cv_w_pw2')[None]
    for nm in ('ffn_w_gate', 'ffn_w_up', 'ffn_w_down'):
        grads[nm] = jnp.stack([shard_grad(nm, i) for i in range(2)])
    cs16 = jnp.pad(cs_all, ((0, 8), (0, 0)))
    dm_mine = lax.dynamic_slice_in_dim(dmod_all, chip * 1536, 1536, axis=2)
    dm16 = jnp.pad(dm_mine, ((0, 8), (0, 0), (0, 0)))
    grads['ada_w'] = jnp.stack([matmul(cs16, dm16[:, i], mode="tn", out_dtype=F32, name=f"ada_dw{i}") for i in range(2)])
    for nm, shp in SMALL_GRADS.items():
        if nm == 'loss':
            continue
        if nm in SMALL_SHARDED:
            n = w[nm].shape[-1]
            grads[nm] = lax.dynamic_slice_in_dim(tot[nm], chip * n, n, axis=len(shp) - 1)
        else:
            grads[nm] = tot[nm].reshape(w[nm].shape)

    delta, new_m, new_v = {}, {}, {}
    for nm in BIG:
        delta[nm], new_m[nm], new_v[nm] = adamw(w[nm], grads[nm], m_in[nm], v_in[nm], name="adamw_" + nm)
    smalls = [nm for nm in WEIGHTS if nm not in BIG]
    packed = [_to_lanes(jnp.concatenate([d[nm].reshape(-1) for nm in smalls])) for d in (w, grads, m_in, v_in)]
    res = rowmap(f_adamw, packed, [], [F32] * 3, name="adamw_small", tr=_rows_tile(packed[0].shape[0]))
    for d, r in zip((delta, new_m, new_v), res, strict=True):
        d.update(_split(r.reshape(-1), {nm: w[nm].shape for nm in smalls}))

    loss = tot['loss'].reshape(())
    return (loss, grad_x[None], *[grads[nm] for nm in WEIGHTS], *[delta[nm] for nm in WEIGHTS],
            *[new_m[nm] for nm in WEIGHTS], *[new_v[nm] for nm in WEIGHTS])
```

```python
import functools
import math

import jax
import jax.numpy as jnp
import numpy as np
from jax import lax
from jax.experimental import pallas as pl
from jax.experimental.pallas import tpu as pltpu

F32 = jnp.float32
BF16 = jnp.bfloat16
MESH = pl.DeviceIdType.MESH

D = 1024
S = 4096
EPS = 1e-6
SSM_INNER = 2048
SSM_HEADS = 32
SSM_HDIM = 64
SSM_GROUPS = 4
SSM_STATE = 128
SSM_CONVK = 4
SSM_CONV_DIM = 3072
CHUNK = 128
N_CHUNKS = S // CHUNK
ATT_HEADS = 16
ATT_HDIM = 64
ATT_PATTERNS = ((128, 1), (512, 4), (2048, 16))
ATT_BLK = 128
REL_BUCKETS = 32
REL_MAX_DIST = 2048
CONV_WIDTH = 31
FFN_HIDDEN = 2816
N_CHIPS = 4
N_DEV = 8
ADAM_LR, ADAM_B1, ADAM_B2, ADAM_EPS, ADAM_WD, ADAM_STEP = 0.001, 0.9, 0.999, 1e-08, 0.01, 10

VMEM_LIMIT_BYTES = 56 * 1024 * 1024
LANES = 128


def _cparams(*sem):
    return pltpu.CompilerParams(dimension_semantics=sem, vmem_limit_bytes=VMEM_LIMIT_BYTES)


def _pick(n, cap, mult=LANES):
    best = None
    for t in range(mult, min(n, cap) + 1, mult):
        if n % t == 0:
            best = t
    return best or n


def _dot(a, b, ca, cb):
    return lax.dot_general(a.astype(BF16), b.astype(BF16), (((ca,), (cb,)), ((), ())), preferred_element_type=F32)


@jax.custom_vjp
def mm_nt(a, b):
    return _dot(a, b, 1, 1)


def _mm_nt_fwd(a, b):
    return _dot(a, b, 1, 1), (a, b)


def _mm_nt_bwd(res, g):
    a, b = res
    return _dot(g, b, 1, 0).astype(a.dtype), _dot(g, a, 0, 0).astype(b.dtype)


mm_nt.defvjp(_mm_nt_fwd, _mm_nt_bwd)


def matmul(a, b, *, mode, out_dtype, name, n=None, b_off=0, tm_cap=1024, tn_cap=512, tk_cap=3584, side=None,
           out_t=False):
    if mode == "tn":
        k_dim, m_dim = a.shape
    else:
        m_dim, k_dim = a.shape
    n_dim = n if n is not None else (b.shape[0] if mode == "nt" else b.shape[1])
    tm = m_dim if m_dim < LANES else _pick(m_dim, tm_cap)
    tn = _pick(n_dim, tn_cap)
    tk = k_dim if k_dim < LANES else _pick(k_dim, tk_cap)
    assert m_dim % tm == 0 and n_dim % tn == 0 and k_dim % tk == 0 and b_off % tn == 0
    nk = k_dim // tk
    off = b_off // tn
    if mode == "nn":
        a_spec = pl.BlockSpec((tm, tk), lambda i, j, k: (i, k))
        b_spec = pl.BlockSpec((tk, tn), lambda i, j, k: (k, j))
        ca, cb = 1, 0
    elif mode == "nt":
        a_spec = pl.BlockSpec((tm, tk), lambda i, j, k: (i, k))
        b_spec = pl.BlockSpec((tn, tk), lambda i, j, k: (j + off, k))
        ca, cb = 1, 1
    else:
        a_spec = pl.BlockSpec((tk, tm), lambda i, j, k: (k, i))
        b_spec = pl.BlockSpec((tk, tn), lambda i, j, k: (k, j))
        ca, cb = 0, 0

    def emit(o_ref, val):
        o_ref[...] = (val.T if out_t else val).astype(o_ref.dtype)

    def body(a_ref, b_ref, o_ref, acc_ref):
        part = _dot(a_ref[...], b_ref[...], ca, cb)
        if nk == 1:
            emit(o_ref, part)
        else:
            k = pl.program_id(2)

            @pl.when(k == 0)
            def _():
                acc_ref[...] = part

            @pl.when(k > 0)
            def _():
                acc_ref[...] += part

            @pl.when(k == nk - 1)
            def _():
                emit(o_ref, acc_ref[...])

    if out_t:
        out_shape, out_spec = (n_dim, m_dim), pl.BlockSpec((tn, tm), lambda i, j, k: (j, i))
    else:
        out_shape, out_spec = (m_dim, n_dim), pl.BlockSpec((tm, tn), lambda i, j, k: (i, j))
    (out,), side_dst = grid_call(
        body, (a, b), name=name,
        out_shape=[jax.ShapeDtypeStruct(out_shape, out_dtype)],
        grid=(m_dim // tm, n_dim // tn, nk),
        in_specs=[a_spec, b_spec],
        out_specs=[out_spec],
        scratch_shapes=[pltpu.VMEM((tm, tn), F32)],
        semantics=("parallel", "parallel", "arbitrary"), side=side)
    return out if side is None else (out, side_dst)


def _f32(xs):
    return [x.astype(F32) for x in xs]


def rowmap(f, rows, consts, out_dtypes, *, name, tr=512):
    r_dim = rows[0].shape[0]
    tr = _pick(r_dim, tr, mult=8)
    assert r_dim % tr == 0
    nr, nc = len(rows), len(consts)
    outs = jax.eval_shape(lambda *xs: f(*xs), *[jax.ShapeDtypeStruct((tr, x.shape[1]), F32) for x in rows],
                          *[jax.ShapeDtypeStruct(x.shape, F32) for x in consts])

    def body(*refs):
        res = f(*_f32([r[...] for r in refs[:nr + nc]]))
        for o_ref, o in zip(refs[nr + nc:], res, strict=True):
            o_ref[...] = o.astype(o_ref.dtype)

    return pl.pallas_call(
        body, name=name,
        out_shape=[jax.ShapeDtypeStruct((r_dim, o.shape[1]), dt) for o, dt in zip(outs, out_dtypes, strict=True)],
        grid=(r_dim // tr,),
        in_specs=[pl.BlockSpec((tr, x.shape[1]), lambda i: (i, 0)) for x in rows]
        + [pl.BlockSpec(x.shape, lambda i: (0, 0)) for x in consts],
        out_specs=[pl.BlockSpec((tr, o.shape[1]), lambda i: (i, 0)) for o in outs],
        compiler_params=_cparams("parallel"),
    )(*rows, *consts)


def rowmap_bwd(f, rows, consts, cts, *, name, row_grad, row_dtypes=None, tr=512, emit=(), row_add=None,
               emit_dtype=F32):
    r_dim = rows[0].shape[0]
    tr = _pick(r_dim, tr, mult=8)
    assert r_dim % tr == 0
    nr, nc, nct = len(rows), len(consts), len(cts)
    gi = [i for i, flag in enumerate(row_grad) if flag]
    row_dtypes = row_dtypes or [F32] * len(gi)
    row_add = row_add or [None] * len(gi)
    adds = [a for a in row_add if a is not None]
    outs = jax.eval_shape(lambda *xs: f(*xs), *[jax.ShapeDtypeStruct((tr, x.shape[1]), F32) for x in rows],
                          *[jax.ShapeDtypeStruct(x.shape, F32) for x in consts])

    def body(*refs):
        ins = _f32([r[...] for r in refs[:nr + nc]])
        ct = _f32([r[...] for r in refs[nr + nc:nr + nc + nct]])
        add_refs = list(refs[nr + nc + nct:nr + nc + nct + len(adds)])
        o_refs = refs[nr + nc + nct + len(adds):]
        res, vjp = jax.vjp(f, *ins)
        grads = vjp(tuple(ct))
        for o_ref, i, a in zip(o_refs[:len(gi)], gi, row_add):
            g = grads[i] if a is None else grads[i] + add_refs.pop(0)[...].astype(F32)
            o_ref[...] = g.astype(o_ref.dtype)
        first = pl.program_id(0) == 0
        for o_ref, g in zip(o_refs[len(gi):len(gi) + nc], grads[nr:]):
            @pl.when(first)
            def _(o_ref=o_ref, g=g):
                o_ref[...] = g

            @pl.when(jnp.logical_not(first))
            def _(o_ref=o_ref, g=g):
                o_ref[...] += g
        for o_ref, i in zip(o_refs[len(gi) + nc:], emit):
            o_ref[...] = res[i].astype(o_ref.dtype)

    out_shape = ([jax.ShapeDtypeStruct(rows[i].shape, dt) for i, dt in zip(gi, row_dtypes, strict=True)]
                 + [jax.ShapeDtypeStruct(x.shape, F32) for x in consts]
                 + [jax.ShapeDtypeStruct((r_dim, outs[i].shape[1]), emit_dtype) for i in emit])
    out_specs = ([pl.BlockSpec((tr, rows[i].shape[1]), lambda i_: (i_, 0)) for i in gi]
                 + [pl.BlockSpec(x.shape, lambda i_: (0, 0)) for x in consts]
                 + [pl.BlockSpec((tr, outs[i].shape[1]), lambda i_: (i_, 0)) for i in emit])
    res = pl.pallas_call(
        body, name=name,
        out_shape=out_shape,
        grid=(r_dim // tr,),
        in_specs=[pl.BlockSpec((tr, x.shape[1]), lambda i: (i, 0)) for x in rows]
        + [pl.BlockSpec(x.shape, lambda i: (0, 0)) for x in consts]
        + [pl.BlockSpec((tr, x.shape[1]), lambda i: (i, 0)) for x in list(cts) + adds],
        out_specs=out_specs,
        compiler_params=_cparams("arbitrary"),
    )(*rows, *consts, *cts, *adds)
    return res[:len(gi)], res[len(gi):len(gi) + nc], res[len(gi) + nc:]


def matmul_swiglu(hgu, w, *, name, tm=512, tk_cap=1536):
    m_dim, hid = hgu.shape[0], hgu.shape[1] // 2
    n_dim = w.shape[1]
    tk = _pick(hid, tk_cap)
    nk = hid // tk
    assert m_dim % tm == 0 and hid % tk == 0

    def body(g_ref, u_ref, w_ref, o_ref, acc_ref):
        gate, up = g_ref[...].astype(F32), u_ref[...].astype(F32)
        part = _dot(_silu(gate) * up, w_ref[...], 1, 0)
        k = pl.program_id(1)

        @pl.when(k == 0)
        def _():
            acc_ref[...] = part

        @pl.when(k > 0)
        def _():
            acc_ref[...] += part

        @pl.when(k == nk - 1)
        def _():
            o_ref[...] = acc_ref[...]

    return pl.pallas_call(
        body, name=name, out_shape=jax.ShapeDtypeStruct((m_dim, n_dim), F32), grid=(m_dim // tm, nk),
        in_specs=[pl.BlockSpec((tm, tk), lambda i, k: (i, k)), pl.BlockSpec((tm, tk), lambda i, k: (i, k + nk)),
                  pl.BlockSpec((tk, n_dim), lambda i, k: (k, 0))],
        out_specs=pl.BlockSpec((tm, n_dim), lambda i, k: (i, 0)),
        scratch_shapes=[pltpu.VMEM((tm, n_dim), F32)],
        compiler_params=_cparams("parallel", "arbitrary"),
    )(hgu, hgu, w)


def transpose(a, *, name, out_dtype=BF16, tr=512, tc=512):
    r_dim, c_dim = a.shape
    tr, tc = _pick(r_dim, tr), _pick(c_dim, tc)

    def body(a_ref, o_ref):
        o_ref[...] = a_ref[...].astype(F32).T.astype(o_ref.dtype)

    return pl.pallas_call(
        body, name=name, out_shape=jax.ShapeDtypeStruct((c_dim, r_dim), out_dtype),
        grid=(r_dim // tr, c_dim // tc),
        in_specs=[pl.BlockSpec((tr, tc), lambda i, j: (i, j))],
        out_specs=pl.BlockSpec((tc, tr), lambda i, j: (j, i)),
        compiler_params=_cparams("parallel", "parallel"),
    )(a)


CONV_HALO = 32
CONV_ROWS = 256


def conv_fwd(x, w, b, *, name, cb=256, chunk_rows=CONV_ROWS):
    s_dim, c_dim = x.shape
    taps = w.shape[0]
    assert taps - 1 <= CONV_HALO and s_dim % chunk_rows == 0 and c_dim % cb == 0
    n_chunks = s_dim // chunk_rows
    ext = chunk_rows + CONV_HALO

    def body(x_ref, w_ref, b_ref, o_ref, xp_ref):
        xp_ref[pl.ds(0, CONV_HALO), :] = jnp.zeros((CONV_HALO, cb), F32)
        xp_ref[pl.ds(CONV_HALO, s_dim), :] = x_ref[...].astype(F32)
        wv = w_ref[...].astype(F32)
        bv = b_ref[...].astype(F32)

        def chunk(t, carry):
            base = pl.multiple_of(t * chunk_rows, chunk_rows)
            xe = xp_ref[pl.ds(base, ext), :]
            acc = jnp.broadcast_to(bv, (chunk_rows, cb))
            for j in range(taps):
                sh = xe if j == 0 else pltpu.roll(xe, shift=j, axis=0)
                acc = acc + wv[taps - 1 - j:taps - j, :] * sh[CONV_HALO:, :]
            o_ref[pl.ds(base, chunk_rows), :] = acc
            return carry

        lax.fori_loop(0, n_chunks, chunk, 0)

    return pl.pallas_call(
        body, name=name,
        out_shape=jax.ShapeDtypeStruct((s_dim, c_dim), F32),
        grid=(c_dim // cb,),
        in_specs=[pl.BlockSpec((s_dim, cb), lambda i: (0, i)), pl.BlockSpec((taps, cb), lambda i: (0, i)),
                  pl.BlockSpec((1, cb), lambda i: (0, i))],
        out_specs=pl.BlockSpec((s_dim, cb), lambda i: (0, i)),
        scratch_shapes=[pltpu.VMEM((s_dim + CONV_HALO, cb), F32)],
        compiler_params=_cparams("parallel"),
    )(x, w, b)


def conv_bwd(x, w, g, *, name, cb=256, chunk_rows=CONV_ROWS, dx_dtype=F32):
    s_dim, c_dim = x.shape
    taps = w.shape[0]
    n_chunks = s_dim // chunk_rows
    ext = chunk_rows + CONV_HALO

    def rows8(a):
        return jnp.sum(a.reshape(chunk_rows // 8, 8, cb), axis=0)

    def body(x_ref, w_ref, g_ref, dx_ref, dw_ref, db_ref, xp_ref, gp_ref, acc_ref):
        xp_ref[pl.ds(0, CONV_HALO), :] = jnp.zeros((CONV_HALO, cb), F32)
        xp_ref[pl.ds(CONV_HALO, s_dim), :] = x_ref[...].astype(F32)
        gp_ref[pl.ds(0, s_dim), :] = g_ref[...].astype(F32)
        gp_ref[pl.ds(s_dim, CONV_HALO), :] = jnp.zeros((CONV_HALO, cb), F32)
        acc_ref[...] = jnp.zeros_like(acc_ref)
        wv = w_ref[...].astype(F32)

        def chunk(t, carry):
            base = pl.multiple_of(t * chunk_rows, chunk_rows)
            xe = xp_ref[pl.ds(base, ext), :]
            ge = gp_ref[pl.ds(base, ext), :]
            gc = ge[:chunk_rows, :]
            dx = jnp.zeros((chunk_rows, cb), F32)
            for j in range(taps):
                xs = xe if j == 0 else pltpu.roll(xe, shift=j, axis=0)
                gs = ge if j == 0 else pltpu.roll(ge, shift=ext - j, axis=0)
                k = taps - 1 - j
                dx = dx + wv[k:k + 1, :] * gs[:chunk_rows, :]
                acc_ref[8 * k:8 * k + 8, :] += rows8(gc * xs[CONV_HALO:, :])
            acc_ref[8 * taps:8 * taps + 8, :] += rows8(gc)
            dx_ref[pl.ds(base, chunk_rows), :] = dx.astype(dx_ref.dtype)
            return carry

        lax.fori_loop(0, n_chunks, chunk, 0)
        sums = jnp.sum(acc_ref[...].reshape(taps + 1, 8, cb), axis=1)
        dw_ref[...] = sums[0:taps, :]
        db_ref[...] = sums[taps:taps + 1, :]

    return pl.pallas_call(
        body, name=name,
        out_shape=[jax.ShapeDtypeStruct((s_dim, c_dim), dx_dtype), jax.ShapeDtypeStruct((taps, c_dim), F32),
                   jax.ShapeDtypeStruct((1, c_dim), F32)],
        grid=(c_dim // cb,),
        in_specs=[pl.BlockSpec((s_dim, cb), lambda i: (0, i)), pl.BlockSpec((taps, cb), lambda i: (0, i)),
                  pl.BlockSpec((s_dim, cb), lambda i: (0, i))],
        out_specs=[pl.BlockSpec((s_dim, cb), lambda i: (0, i)), pl.BlockSpec((taps, cb), lambda i: (0, i)),
                   pl.BlockSpec((1, cb), lambda i: (0, i))],
        scratch_shapes=[pltpu.VMEM((s_dim + CONV_HALO, cb), F32), pltpu.VMEM((s_dim + CONV_HALO, cb), F32),
                        pltpu.VMEM((8 * (taps + 1), cb), F32)],
        compiler_params=_cparams("parallel"),
    )(x, w, g)


def _softplus(x):
    return jnp.maximum(x, 0.0) + jnp.log(1.0 + jnp.exp(-jnp.abs(x)))


HEADS_PER_GROUP = SSM_HEADS // SSM_GROUPS


def _bdot(a, b, ca, cb):
    return lax.dot_general(a.astype(BF16), b.astype(BF16), (((ca,), (cb,)), ((0,), (0,))), preferred_element_type=F32)


@jax.custom_vjp
def bmm(a, b):
    return _bdot(a, b, 2, 1)


def _bmm_fwd(a, b):
    return _bdot(a, b, 2, 1), (a, b)


def _bmm_bwd(res, g):
    a, b = res
    return _bdot(g, b, 2, 2).astype(a.dtype), _bdot(a, g, 1, 1).astype(b.dtype)


bmm.defvjp(_bmm_fwd, _bmm_bwd)


@jax.custom_vjp
def bmm_nt(a, b):
    return _bdot(a, b, 2, 2)


def _bmm_nt_fwd(a, b):
    return _bdot(a, b, 2, 2), (a, b)


def _bmm_nt_bwd(res, g):
    a, b = res
    return _bdot(g, b, 2, 1).astype(a.dtype), _bdot(g, a, 1, 1).astype(b.dtype)


bmm_nt.defvjp(_bmm_nt_fwd, _bmm_nt_bwd)


@jax.custom_vjp
def bmm_tn(a, b):
    return _bdot(a, b, 1, 1)


def _bmm_tn_fwd(a, b):
    return _bdot(a, b, 1, 1), (a, b)


def _bmm_tn_bwd(res, g):
    a, b = res
    return _bdot(b, g, 2, 2).astype(a.dtype), _bdot(a, g, 2, 1).astype(b.dtype)


bmm_tn.defvjp(_bmm_tn_fwd, _bmm_tn_bwd)


ATT_PAIRS = ATT_HEADS // 2
PAIR_W = 2 * ATT_HDIM


def att_pairs(q, kp, kc, vp, vc, bias, has_prev):
    t, b, w = q.shape
    i = lax.broadcasted_iota(jnp.int32, (1, b, b), 1)
    j = lax.broadcasted_iota(jnp.int32, (1, b, b), 2)
    first = lax.broadcasted_iota(jnp.int32, (1, 1, w), 2) < ATT_HDIM
    scale = ATT_HDIM ** -0.5
    outs, lses = [], []
    for ab in range(2):
        qh = jnp.where(first if ab == 0 else jnp.logical_not(first), q, 0.0)
        sp = jnp.where(jnp.logical_and(j >= i, has_prev), bmm_nt(qh, kp) * scale + bias[:, ab, 0], -1e30)
        sc = jnp.where(j <= i, bmm_nt(qh, kc) * scale + bias[:, ab, 1], -1e30)
        m = lax.stop_gradient(jnp.maximum(jnp.max(sp, axis=2, keepdims=True), jnp.max(sc, axis=2, keepdims=True)))
        pp, pc = jnp.exp(sp - m), jnp.exp(sc - m)
        l = jnp.sum(pp, axis=2, keepdims=True) + jnp.sum(pc, axis=2, keepdims=True)
        outs.append(bmm(pp / l, vp) + bmm(pc / l, vc))
        lses.append(jnp.broadcast_to(m + jnp.log(l), (t, b, w)))
    return jnp.where(first, outs[0], outs[1]), jnp.where(first, lses[0], lses[1])


def _pair_tiles(ref):
    return jnp.stack([ref[:, PAIR_W * t:PAIR_W * (t + 1)] for t in range(ATT_PAIRS)])


def _store_pair_tiles(ref, val):
    for t in range(ATT_PAIRS):
        ref[:, PAIR_W * t:PAIR_W * (t + 1)] = val[t].astype(ref.dtype)


def pair_bias(bias):
    return bias.reshape(ATT_PAIRS, 2, 2, ATT_BLK, ATT_BLK)


def att2_fwd(q, k, v, bias, nb, cols, *, name, side=None):
    n_blocks = S // ATT_BLK
    qc, kc, vc = cols

    def body(q_ref, k_ref, v_ref, b_ref, o_ref, l_ref, kprev, vprev):
        blk = pl.program_id(0)

        @pl.when(blk == 0)
        def _():
            kprev[...] = jnp.zeros_like(kprev)
            vprev[...] = jnp.zeros_like(vprev)

        k3, v3 = _pair_tiles(k_ref), _pair_tiles(v_ref)
        o, lse = att_pairs(_pair_tiles(q_ref), kprev[...], k3, vprev[...], v3, b_ref[...], (blk % nb) != 0)
        _store_pair_tiles(o_ref, o)
        _store_pair_tiles(l_ref, lse)
        kprev[...] = k3
        vprev[...] = v3

    def spec(c):
        return pl.BlockSpec((ATT_BLK, D), lambda b: (b, c))

    return grid_call(
        body, (q, k, v, bias), name=name,
        out_shape=[jax.ShapeDtypeStruct((S, D), BF16), jax.ShapeDtypeStruct((S, D), F32)], grid=(n_blocks,),
        in_specs=[spec(qc), spec(kc), spec(vc), pl.BlockSpec(bias.shape, lambda b: (0, 0, 0, 0, 0))],
        out_specs=[spec(0), spec(0)],
        scratch_shapes=[pltpu.VMEM((ATT_PAIRS, ATT_BLK, PAIR_W), BF16), pltpu.VMEM((ATT_PAIRS, ATT_BLK, PAIR_W), BF16)],
        semantics=("arbitrary",), side=side)


def att2_bwd(q, k, v, bias, do, dlse, nb, cols, *, name, side=None):
    n_blocks = S // ATT_BLK
    qc, kc, vc = cols

    def body(q_ref, k_ref, v_ref, b_ref, do_ref, dl_ref, dq_ref, dk_ref, dv_ref, db_ref, kprev, vprev, dk_own, dv_own):
        blk = pl.program_id(0)

        @pl.when(blk == 0)
        def _():
            for r in (kprev, vprev, dk_own, dv_own, db_ref):
                r[...] = jnp.zeros_like(r)

        @pl.when(blk < n_blocks)
        def _():
            k3, v3 = _pair_tiles(k_ref), _pair_tiles(v_ref)
            ins = _f32([_pair_tiles(q_ref), kprev[...], k3, vprev[...], v3]) + [b_ref[...]]
            _, vjp = jax.vjp(functools.partial(att_pairs, has_prev=(blk % nb) != 0), *ins)
            dq, dkp, dkc, dvp, dvc, db = vjp(tuple(_f32([_pair_tiles(do_ref), _pair_tiles(dl_ref)])))
            _store_pair_tiles(dq_ref, dq)
            _store_pair_tiles(dk_ref, dk_own[...] + dkp)
            _store_pair_tiles(dv_ref, dv_own[...] + dvp)
            dk_own[...] = dkc
            dv_own[...] = dvc
            db_ref[...] += db
            kprev[...] = k3
            vprev[...] = v3

        @pl.when(blk == n_blocks)
        def _():
            _store_pair_tiles(dk_ref, dk_own[...])
            _store_pair_tiles(dv_ref, dv_own[...])

    def spec(c):
        return pl.BlockSpec((ATT_BLK, D), lambda b: (jnp.minimum(b, n_blocks - 1), c))

    late = pl.BlockSpec((ATT_BLK, D), lambda b: (jnp.maximum(b - 1, 0), 0))
    bspec = pl.BlockSpec(bias.shape, lambda b: (0, 0, 0, 0, 0))
    tile_f32 = pltpu.VMEM((ATT_PAIRS, ATT_BLK, PAIR_W), F32)
    tile_bf16 = pltpu.VMEM((ATT_PAIRS, ATT_BLK, PAIR_W), BF16)
    return grid_call(
        body, (q, k, v, bias, do, dlse), name=name,
        out_shape=[jax.ShapeDtypeStruct((S, D), BF16), jax.ShapeDtypeStruct((S, D), BF16),
                   jax.ShapeDtypeStruct((S, D), BF16), jax.ShapeDtypeStruct(bias.shape, F32)],
        grid=(n_blocks + 1,),
        in_specs=[spec(qc), spec(kc), spec(vc), bspec, spec(0), spec(0)],
        out_specs=[spec(0), late, late, bspec],
        scratch_shapes=[tile_bf16, tile_bf16, tile_f32, tile_f32],
        semantics=("arbitrary",), side=side)


def regroup(a, dil, inverse=False):
    if dil == 1:
        return a
    c_dim = a.shape[1]
    shape = (dil, S // dil, c_dim) if inverse else (S // dil, dil, c_dim)
    return jnp.transpose(a.reshape(shape), (1, 0, 2)).reshape(S, c_dim)


SSD_PAIRS = SSM_HEADS // 2
PAIRS_PER_GROUP = SSD_PAIRS // SSM_GROUPS
GROUP_W = HEADS_PER_GROUP * SSM_HDIM


SSD_GROUPS_PER_STEP = 4
SSD_STEP_PAIRS = PAIRS_PER_GROUP * SSD_GROUPS_PER_STEP
SSD_STEP_W = GROUP_W * SSD_GROUPS_PER_STEP


def ssd_pairs(x, dtraw, dt_bias, a_log, dskip, bms, cms, prev):
    t, q, w = x.shape
    n = bms[0].shape[1]
    per = t // len(bms)

    def by_pair(mats):
        return jnp.concatenate([jnp.broadcast_to(m[None], (per,) + m.shape) for m in mats], axis=0)
    li = lax.broadcasted_iota(jnp.int32, (1, q, q), 1)
    si = lax.broadcasted_iota(jnp.int32, (1, q, q), 2)
    first_lane = lax.broadcasted_iota(jnp.int32, (1, 1, w), 2) < SSM_HDIM
    first_row = lax.broadcasted_iota(jnp.int32, (1, w, 1), 1) < SSM_HDIM

    def to_col(row):
        return jnp.sum(jnp.where(li == si, jnp.broadcast_to(row, (t, q, q)), 0.0), axis=2, keepdims=True)

    def lanes(a0, a1):
        return jnp.where(first_lane, a0, a1)

    dt_col, acs_col, total, lmat = [], [], [], []
    for ab in range(2):
        dt_row = _softplus(dtraw[ab] + dt_bias[ab])
        a_row = dt_row * (-jnp.exp(a_log[ab]))
        a_col = to_col(a_row)
        acs_c = jnp.sum(jnp.where(si <= li, jnp.broadcast_to(a_row, (t, q, q)), 0.0), axis=2, keepdims=True)
        acs_r = jnp.sum(jnp.where(li <= si, jnp.broadcast_to(a_col, (t, q, q)), 0.0), axis=1, keepdims=True)
        dt_col.append(to_col(dt_row))
        acs_col.append(acs_c)
        total.append(jnp.sum(a_row, axis=2, keepdims=True))
        lmat.append(jnp.exp(jnp.where(li >= si, acs_c - acs_r, -1e30)))
    cb = by_pair([mm_nt(c_, b_) for c_, b_ in zip(cms, bms, strict=True)])
    bmb, cmb = by_pair(bms), by_pair(cms)
    xdt = x * lanes(dt_col[0], dt_col[1])
    y = lanes(bmm(cb * lmat[0], xdt), bmm(cb * lmat[1], xdt))
    y = y + bmm_nt(cmb, prev) * lanes(jnp.exp(acs_col[0]), jnp.exp(acs_col[1]))
    y = y + lanes(dskip[0], dskip[1]) * x
    state = bmm_tn(xdt * lanes(jnp.exp(total[0] - acs_col[0]), jnp.exp(total[1] - acs_col[1])), bmb)
    return y, jnp.where(first_row, jnp.exp(total[0]), jnp.exp(total[1])) * prev + state


def _group_tiles(ref):
    return jnp.stack([ref[:, PAIR_W * t:PAIR_W * (t + 1)] for t in range(SSD_STEP_PAIRS)])


def _store_group_tiles(ref, val):
    for t in range(SSD_STEP_PAIRS):
        ref[:, PAIR_W * t:PAIR_W * (t + 1)] = val[t]


def _bc_groups(ref):
    return tuple(ref[:, SSM_STATE * i:SSM_STATE * (i + 1)] for i in range(SSD_GROUPS_PER_STEP))


def _by_pair(a):
    return jnp.transpose(a.reshape(SSD_PAIRS, 2, 1, -1), (1, 0, 2, 3))


def _by_head(a):
    return jnp.transpose(a, (1, 0, 2, 3)).reshape(SSM_HEADS, -1)


def _ssd2_specs(chunk_of):
    tp = SSD_STEP_PAIRS
    xspec = pl.BlockSpec((CHUNK, SSD_STEP_W), lambda g, c: (chunk_of(c), g))
    tspec = pl.BlockSpec((2, tp, 1, CHUNK), lambda g, c: (0, g, 0, chunk_of(c)))
    hp = pl.BlockSpec((2, tp, 1, 1), lambda g, c: (0, g, 0, 0))
    gspec = pl.BlockSpec((CHUNK, SSD_GROUPS_PER_STEP * SSM_STATE), lambda g, c: (chunk_of(c), g))
    sspec = pl.BlockSpec((1, tp, PAIR_W, SSM_STATE), lambda g, c: (chunk_of(c), g, 0, 0))
    return xspec, tspec, hp, gspec, sspec


def ssd2_fwd(xs, dtraw_t, dt_bias, a_log, dskip, bm, cm, side=None):
    def body(x_ref, dt_ref, dtb_ref, al_ref, dk_ref, bm_ref, cm_ref, y_ref, prev_ref, state_ref):
        @pl.when(pl.program_id(1) == 0)
        def _():
            state_ref[...] = jnp.zeros_like(state_ref)

        prev = state_ref[...]
        prev_ref[0] = prev
        y, nxt = ssd_pairs(_group_tiles(x_ref), dt_ref[...], dtb_ref[...], al_ref[...], dk_ref[...], _bc_groups(bm_ref),
                           _bc_groups(cm_ref), prev)
        _store_group_tiles(y_ref, y)
        state_ref[...] = nxt

    xspec, tspec, hp, gspec, sspec = _ssd2_specs(lambda c: c)
    return grid_call(
        body, (xs, _by_pair(dtraw_t), _by_pair(dt_bias), _by_pair(a_log), _by_pair(dskip), bm, cm), name="ssd_fwd",
        out_shape=[jax.ShapeDtypeStruct((S, SSM_INNER), F32),
                   jax.ShapeDtypeStruct((N_CHUNKS, SSD_PAIRS, PAIR_W, SSM_STATE), F32)],
        grid=(SSM_GROUPS // SSD_GROUPS_PER_STEP, N_CHUNKS), in_specs=[xspec, tspec, hp, hp, hp, gspec, gspec],
        out_specs=[xspec, sspec],
        scratch_shapes=[pltpu.VMEM((SSD_STEP_PAIRS, PAIR_W, SSM_STATE), F32)],
        semantics=("parallel", "arbitrary"), side=side)


def ssd2_bwd(xs, dtraw_t, dt_bias, a_log, dskip, bm, cm, prev_all, dy, side=None):
    def body(x_ref, dt_ref, dtb_ref, al_ref, dk_ref, bm_ref, cm_ref, prev_ref, dy_ref,
             dx_ref, ddt_ref, ddtb_ref, dal_ref, ddk_ref, dbm_ref, dcm_ref, dstate_ref):
        @pl.when(pl.program_id(1) == 0)
        def _():
            for r in (dstate_ref, ddtb_ref, dal_ref, ddk_ref):
                r[...] = jnp.zeros_like(r)

        _, vjp = jax.vjp(ssd_pairs, _group_tiles(x_ref), dt_ref[...], dtb_ref[...], al_ref[...], dk_ref[...],
                         _bc_groups(bm_ref), _bc_groups(cm_ref), prev_ref[0])
        dx, ddt, ddtb, dal, ddk, dbms, dcms, dprev = vjp((_group_tiles(dy_ref), dstate_ref[...]))
        _store_group_tiles(dx_ref, dx)
        ddt_ref[...] = ddt
        ddtb_ref[...] += ddtb
        dal_ref[...] += dal
        ddk_ref[...] += ddk
        for i in range(SSD_GROUPS_PER_STEP):
            dbm_ref[:, SSM_STATE * i:SSM_STATE * (i + 1)] = dbms[i]
            dcm_ref[:, SSM_STATE * i:SSM_STATE * (i + 1)] = dcms[i]
        dstate_ref[...] = dprev

    xspec, tspec, hp, gspec, sspec = _ssd2_specs(lambda c: N_CHUNKS - 1 - c)
    par = jax.ShapeDtypeStruct((2, SSD_PAIRS, 1, 1), F32)
    res, side_dst = grid_call(
        body, (xs, _by_pair(dtraw_t), _by_pair(dt_bias), _by_pair(a_log), _by_pair(dskip), bm, cm, prev_all, dy),
        name="ssd_bwd",
        out_shape=[jax.ShapeDtypeStruct((S, SSM_INNER), F32), jax.ShapeDtypeStruct((2, SSD_PAIRS, 1, S), F32), par, par, par,
                   jax.ShapeDtypeStruct((S, SSM_GROUPS * SSM_STATE), F32),
                   jax.ShapeDtypeStruct((S, SSM_GROUPS * SSM_STATE), F32)],
        grid=(SSM_GROUPS // SSD_GROUPS_PER_STEP, N_CHUNKS), in_specs=[xspec, tspec, hp, hp, hp, gspec, gspec, sspec, xspec],
        out_specs=[xspec, tspec, hp, hp, hp, gspec, gspec],
        scratch_shapes=[pltpu.VMEM((SSD_STEP_PAIRS, PAIR_W, SSM_STATE), F32)],
        semantics=("parallel", "arbitrary"), side=side)
    return [res[0]] + [_by_head(r) for r in res[1:5]] + list(res[5:]), side_dst


def _silu(x):
    return x * jax.nn.sigmoid(x)


def _rms(x):
    return x * lax.rsqrt(jnp.mean(x * x, -1, keepdims=True) + EPS)


def f_normmod(x, g, sc, sh):
    return (_rms(x) * g * (1.0 + sc) + sh,)


def f_resid(x, mix, gate):
    return (x + gate * mix,)


def f_resid_bias(x, mix, gate, b):
    return (x + gate * (mix + b),)


def f_swiglu(hgu):
    return (_silu(hgu[:, :FFN_HIDDEN]) * hgu[:, FFN_HIDDEN:],)


def f_silu(x):
    return (_silu(x),)


def f_silu_xbc(x):
    y = _silu(x)
    n_b = SSM_GROUPS * SSM_STATE
    return y[:, :SSM_INNER], y[:, SSM_INNER:SSM_INNER + n_b], y[:, SSM_INNER + n_b:]


def f_gated_norm(y, z, g):
    return (_rms(y * _silu(z)) * g,)


def f_glu(y, b):
    y = y + b
    return (y[:, :D] * jax.nn.sigmoid(y[:, D:]),)


def f_ln_silu(u, g, b):
    mu = jnp.mean(u, -1, keepdims=True)
    var = jnp.mean(jnp.square(u - mu), -1, keepdims=True)
    return (_silu((u - mu) * lax.rsqrt(var + EPS) * g + b),)


def f_combine(o1, o2, o3, l1, l2, l3):
    m = lax.stop_gradient(jnp.maximum(jnp.maximum(l1, l2), l3))
    e1, e2, e3 = jnp.exp(l1 - m), jnp.exp(l2 - m), jnp.exp(l3 - m)
    return ((e1 * o1 + e2 * o2 + e3 * o3) / (e1 + e2 + e3),)


def f_head(x, tgt, g):
    return (0.5 * jnp.mean(jnp.square(_rms(x) * g - tgt), -1, keepdims=True),)


def f_sum3(a, b, c):
    return (a + b + c,)


def f_add(a, b):
    return (a + b,)


def f_adamw(w, g, m, v):
    m = ADAM_B1 * m + (1.0 - ADAM_B1) * g
    v = ADAM_B2 * v + (1.0 - ADAM_B2) * jnp.square(g)
    m_hat = m / (1.0 - ADAM_B1 ** ADAM_STEP)
    v_hat = v / (1.0 - ADAM_B2 ** ADAM_STEP)
    return -ADAM_LR * (m_hat / (jnp.sqrt(v_hat) + ADAM_EPS) + ADAM_WD * w), m, v


def _rows_tile(r, cap=256):
    return _pick(r, cap, mult=8)


def adamw(w, g, m, v, *, name):
    l_dim, r_dim, c_dim = w.shape
    tr = _rows_tile(r_dim, cap=256)

    def body(w_ref, g_ref, m_ref, v_ref, d_ref, mo_ref, vo_ref):
        d_ref[...], mo_ref[...], vo_ref[...] = f_adamw(w_ref[...], g_ref[...], m_ref[...], v_ref[...])

    spec = pl.BlockSpec((1, tr, c_dim), lambda l, i: (l, i, 0))
    return pl.pallas_call(
        body, name=name, out_shape=[jax.ShapeDtypeStruct(w.shape, F32)] * 3, grid=(l_dim, r_dim // tr),
        in_specs=[spec] * 4, out_specs=[spec] * 3, compiler_params=_cparams("parallel", "parallel"),
    )(w, g, m, v)


def _t5_bucket(dist):
    max_exact = REL_BUCKETS // 2
    n = jnp.maximum(dist, 1).astype(F32)
    large = max_exact + jnp.log(n / max_exact) / math.log(REL_MAX_DIST / max_exact) * (REL_BUCKETS - max_exact)
    large = jnp.minimum(large.astype(jnp.int32), REL_BUCKETS - 1)
    return jnp.where(dist < max_exact, dist, large)


def _att_buckets(dil):
    i = jnp.arange(ATT_BLK)[:, None]
    j = jnp.arange(2 * ATT_BLK)[None, :]
    bkt = _t5_bucket(jnp.maximum(ATT_BLK + i - j, 0) * dil)
    return jnp.transpose(bkt.reshape(ATT_BLK, 2, ATT_BLK), (1, 0, 2))


def att_bias(rel_table, p, dil):
    tab = rel_table[:, p * ATT_HEADS:(p + 1) * ATT_HEADS]
    onehot = (jnp.arange(REL_BUCKETS)[:, None] == _att_buckets(dil).reshape(1, -1)).astype(F32)
    bias = lax.dot_general(tab, onehot, (((0,), (0,)), ((), ())), precision=lax.Precision.HIGHEST)
    return bias.reshape(ATT_HEADS, 2, ATT_BLK, ATT_BLK)


def att_bias_grad(dbias, dil, *, name):
    onehot = (_att_buckets(dil).reshape(-1, 1) == jnp.arange(LANES)[None, :]).astype(BF16)
    dtab = matmul(dbias.reshape(ATT_HEADS, -1), onehot, mode="nn", out_dtype=F32, name=name, tk_cap=2048)
    return dtab[:, :REL_BUCKETS].T


HY_Z, HY_XBC, HY_DT, HY_Q, HY_K, HY_V = 2048, 3072, 32, 3072, 1024, 1024
HY_IN = HY_Z + HY_XBC + HY_DT + HY_Q + HY_K + HY_V
OFF_Z, OFF_XBC, OFF_Q, OFF_KV, OFF_DT = 0, 2048, 5120, 8192, 10240
HY_CAT = OFF_DT + LANES
DT_PAD = LANES


def hy_to_cat(w):
    z, xbc, dt, qkv = w[:2048], w[2048:5120], w[5120:5152], w[5152:]
    return jnp.concatenate([z, xbc, qkv, dt, jnp.zeros((DT_PAD - HY_DT,) + w.shape[1:], w.dtype)], axis=0)


def hy_from_cat(w, axis=0):
    part = lambda a, b: lax.slice_in_dim(w, a, b, axis=axis)
    return jnp.concatenate([part(0, 5120), part(OFF_DT, OFF_DT + HY_DT), part(5120, OFF_DT)], axis=axis)


def device_step(x, tgt, mods, wts, sp, comm=None):
    g = {}
    dmods = [[None] * 6 for _ in range(2)]
    wts = dict(wts)

    def wgrad(tokens_d, tokens_n, nm):
        return matmul(transpose(tokens_d, name=nm + "_t"), tokens_n, mode="nn", out_dtype=BF16, name=nm, out_t=True,
                      tk_cap=2048)

    def w_side(i):
        return None if comm is None else GatherRows(comm["pack"], comm["full"], *W_BATCHES[i])

    def g_side(i):
        return None if comm is None else ScatterRows(comm["ga"], comm["recv"], *G_BATCHES[i])

    def normmod(xi, gain, sc, sh, nm):
        return rowmap(f_normmod, [xi], [gain, sc, sh], [BF16], name=nm)[0]

    def ffn_fwd(xi, i, gate, nm):
        h = normmod(xi, sp["norm_ffn_g"][i], mods[i][4], mods[i][3], nm + "_norm")
        hgu = matmul(h, wts["gu_t"][i], mode="nt", out_dtype=BF16, name=nm + "_gu")
        out = matmul_swiglu(hgu, wts["down"][i], name=nm + "_down")
        xo = rowmap(f_resid, [xi, out], [gate], [F32], name=nm + "_res")[0]
        return xo, (h, hgu, out)

    def ffn_bwd(dres, xi, i, saved, nm):
        h, hgu, out = saved
        (dout,), (dgate,), _ = rowmap_bwd(f_resid, [xi, out], [mods[i][5]], [dres], name=nm + "_res_b",
                                          row_grad=[False, True], row_dtypes=[BF16])
        dmods[i][5] = dgate
        dact = matmul(dout, wts["down"][i], mode="nt", out_dtype=BF16, name=nm + "_down_dx")
        (dhgu,), _, (act,) = rowmap_bwd(f_swiglu, [hgu], [], [dact], name=nm + "_act_b", row_grad=[True],
                                        row_dtypes=[BF16], tr=256, emit=(0,), emit_dtype=BF16)
        g[f"down{i}"] = wgrad(dout, act, nm + "_down_dw")
        g[f"gu_t{i}"] = wgrad(h, dhgu, nm + "_gu_dw")
        dh = matmul(dhgu, wts["gu_t"][i], mode="nn", out_dtype=F32, name=nm + "_gu_dx")
        (dres,), (dg_, dsc, dsh), _ = rowmap_bwd(f_normmod, [xi], [sp["norm_ffn_g"][i], mods[i][4], mods[i][3]], [dh],
                                                 name=nm + "_norm_b", row_grad=[True], row_add=[dres])
        g[f"norm_ffn_g{i}"] = dg_
        dmods[i][4], dmods[i][3] = dsc, dsh
        return dres

    h0 = normmod(x, sp["norm_mix_g"][0], mods[0][1], mods[0][0], "l0_norm")
    w_in = wts["hy_in_t"]
    z = matmul(h0, w_in, mode="nt", out_dtype=BF16, name="hy_z", n=HY_Z, b_off=OFF_Z)
    xbc_raw = matmul(h0, w_in, mode="nt", out_dtype=BF16, name="hy_xbc", n=HY_XBC, b_off=OFF_XBC)
    q = matmul(h0, w_in, mode="nt", out_dtype=BF16, name="hy_q", n=HY_Q, b_off=OFF_Q)
    kv = matmul(h0, w_in, mode="nt", out_dtype=BF16, name="hy_kv", n=HY_K + HY_V, b_off=OFF_KV)
    dtr = matmul(h0, w_in, mode="nt", out_dtype=F32, name="hy_dt", n=DT_PAD, b_off=OFF_DT)
    xbc_pre = conv_fwd(xbc_raw, sp["hy_conv_w"], sp["hy_conv_b"], name="hy_conv")
    xs, bm, cm = rowmap(f_silu_xbc, [xbc_pre], [], [F32] * 3, name="hy_conv_act", tr=256)
    dtraw_t = dtr[:, :HY_DT].T
    (y, prev_all), full = ssd2_fwd(xs, dtraw_t, sp["hy_dt_bias"], sp["hy_a_log"], sp["hy_d_skip"], bm, cm, side=w_side(1))
    if comm is not None:
        comm["full"] = full
    ysn = rowmap(f_gated_norm, [y, z], [sp["hy_ssm_norm_g"]], [BF16], name="hy_gnorm", tr=256)[0]
    att_in, att_o, att_l = [], [], []
    for p, (win, dil) in enumerate(ATT_PATTERNS):
        if dil == 1:
            qa, ka, va, cols = q, kv, kv, (p, 0, 1)
        else:
            qa, ka, cols = regroup(q[:, p * D:(p + 1) * D], dil), regroup(kv, dil), (0, 0, 1)
            va = ka
        bias = pair_bias(att_bias(sp["rel_table"], p, dil))
        nb = S // dil // ATT_BLK
        (o, lse), full = att2_fwd(qa, ka, va, bias, nb, cols, name=f"att_fwd{p}", side=w_side(2 + p))
        if comm is not None:
            comm["full"] = full
        att_in.append((qa, ka, va, bias, nb, cols))
        att_o.append(regroup(o, dil, inverse=True))
        att_l.append(regroup(lse, dil, inverse=True))
    if comm is not None:
        wts.update(unpack_weights(comm["full"], skip=("hy_in_t",)))
    att = rowmap(f_combine, att_o + att_l, [], [BF16], name="att_combine", tr=256)[0]
    cat = jnp.concatenate([ysn, att], axis=-1)
    mix0 = matmul(cat, wts["hy_out"], mode="nn", out_dtype=F32, name="hy_out")
    x1 = rowmap(f_resid, [x, mix0], [mods[0][2]], [F32], name="l0_res")[0]
    x2, ffn0 = ffn_fwd(x1, 0, mods[0][5], "ffn0")

    h1 = normmod(x2, sp["norm_mix_g"][1], mods[1][1], mods[1][0], "l1_norm")
    p1 = matmul(h1, wts["pw1_t"], mode="nt", out_dtype=BF16, name="cv_pw1")
    u = rowmap(f_glu, [p1], [sp["cv_b_pw1"]], [F32], name="cv_glu")[0]
    uc = conv_fwd(u, sp["cv_w_dw"], sp["cv_b_dw"], name="cv_conv")
    ul = rowmap(f_ln_silu, [uc], [sp["cv_ln_g"], sp["cv_ln_b"]], [BF16], name="cv_ln")[0]
    mix1 = matmul(ul, wts["pw2"], mode="nn", out_dtype=F32, name="cv_pw2")
    x3 = rowmap(f_resid_bias, [x2, mix1], [mods[1][2], sp["cv_b_pw2"]], [F32], name="l1_res")[0]
    x4, ffn1 = ffn_fwd(x3, 1, mods[1][5], "ffn1")

    ones = jnp.ones((S, 1), F32)
    (dres,), (dfinal,), (loss_rows,) = rowmap_bwd(f_head, [x4, tgt], [sp["final_norm_g"]], [ones], name="head",
                                                  row_grad=[True, False], emit=(0,))
    g["final_norm_g"] = dfinal

    dres = ffn_bwd(dres, x3, 1, ffn1, "ffn1")
    (dmix1,), (dg1, db2), _ = rowmap_bwd(f_resid_bias, [x2, mix1], [mods[1][2], sp["cv_b_pw2"]], [dres], name="l1_res_b",
                                         row_grad=[False, True], row_dtypes=[BF16])
    dmods[1][2] = dg1
    g["cv_b_pw2"] = db2
    dul = matmul(dmix1, wts["pw2"], mode="nt", out_dtype=BF16, name="cv_pw2_dx")
    g["pw2"] = wgrad(dmix1, ul, "cv_pw2_dw")
    (duc,), (g["cv_ln_g"], g["cv_ln_b"]), _ = rowmap_bwd(f_ln_silu, [uc], [sp["cv_ln_g"], sp["cv_ln_b"]], [dul],
                                                         name="cv_ln_b", row_grad=[True])
    du, g["cv_w_dw"], g["cv_b_dw"] = conv_bwd(u, sp["cv_w_dw"], duc, name="cv_conv_b", cb=128, chunk_rows=128)
    (dp1,), (g["cv_b_pw1"],), _ = rowmap_bwd(f_glu, [p1], [sp["cv_b_pw1"]], [du], name="cv_glu_b", row_grad=[True],
                                             row_dtypes=[BF16])
    g["pw1_t"] = wgrad(h1, dp1, "cv_pw1_dw")
    dh1 = matmul(dp1, wts["pw1_t"], mode="nn", out_dtype=F32, name="cv_pw1_dx")
    (dres,), (dg_, dsc, dsh), _ = rowmap_bwd(f_normmod, [x2], [sp["norm_mix_g"][1], mods[1][1], mods[1][0]], [dh1],
                                             name="l1_norm_b", row_grad=[True], row_add=[dres])
    g["norm_mix_g1"] = dg_
    dmods[1][1], dmods[1][0] = dsc, dsh

    dres = ffn_bwd(dres, x1, 0, ffn0, "ffn0")
    (dmix0,), (dg1,), _ = rowmap_bwd(f_resid, [x, mix0], [mods[0][2]], [dres], name="l0_res_b",
                                     row_grad=[False, True], row_dtypes=[BF16])
    dmods[0][2] = dg1
    dysn = matmul(dmix0, wts["hy_out"], mode="nt", out_dtype=BF16, name="hy_out_dy", n=SSM_INNER, b_off=0)
    datt = matmul(dmix0, wts["hy_out"], mode="nt", out_dtype=BF16, name="hy_out_da", n=D, b_off=SSM_INNER)
    g["hy_out"] = wgrad(dmix0, cat, "hy_out_dw")
    (dy, dz), (g["hy_ssm_norm_g"],), _ = rowmap_bwd(f_gated_norm, [y, z], [sp["hy_ssm_norm_g"]], [dysn], name="hy_gnorm_b",
                                                    row_grad=[True, True], row_dtypes=[F32, BF16], tr=256)
    if comm is not None:
        comm["ga"] = pack_grads(g, GA_LAYOUT, GA_ROWS)
        comm["recv"] = lax.empty((3, GA_ROWS, D), BF16)
    (dxs, ddtraw_t, g["hy_dt_bias"], g["hy_a_log"], g["hy_d_skip"], dbm, dcm), recv = ssd2_bwd(
        xs, dtraw_t, sp["hy_dt_bias"], sp["hy_a_log"], sp["hy_d_skip"], bm, cm, prev_all, dy, side=g_side(0))
    if comm is not None:
        comm["recv"] = recv
    (dxbc_pre,), _, _ = rowmap_bwd(f_silu_xbc, [xbc_pre], [], [dxs, dbm, dcm], name="hy_conv_act_b", row_grad=[True],
                                   tr=256)
    dxbc_raw, g["hy_conv_w"], g["hy_conv_b"] = conv_bwd(xbc_raw, sp["hy_conv_w"], dxbc_pre, name="hy_conv_b", cb=128, chunk_rows=128, dx_dtype=BF16)
    dol, _, _ = rowmap_bwd(f_combine, att_o + att_l, [], [datt], name="att_combine_b", row_grad=[True] * 6,
                           row_dtypes=[BF16] * 3 + [F32] * 3, tr=256)
    dqs, dks, dvs, dtabs = [], [], [], []
    for p, (win, dil) in enumerate(ATT_PATTERNS):
        qa, ka, va, bias, nb, cols = att_in[p]
        (dq, dkp_, dvp_, dbias), recv = att2_bwd(qa, ka, va, bias, regroup(dol[p], dil), regroup(dol[3 + p], dil), nb,
                                                 cols, name=f"att_bwd{p}", side=g_side(1 + p))
        if comm is not None:
            comm["recv"] = recv
        dqs.append(regroup(dq, dil, inverse=True))
        dks.append(regroup(dkp_, dil, inverse=True))
        dvs.append(regroup(dvp_, dil, inverse=True))
        dtabs.append(att_bias_grad(dbias.reshape(ATT_HEADS, 2, ATT_BLK, ATT_BLK), dil, name=f"att_dtab{p}"))
    g["rel_table"] = jnp.concatenate(dtabs, axis=1)
    dk = rowmap(f_sum3, dks, [], [BF16], name="att_dk_sum")[0]
    dv = rowmap(f_sum3, dvs, [], [BF16], name="att_dv_sum")[0]
    ddt = jnp.pad(ddtraw_t.T, ((0, 0), (0, DT_PAD - HY_DT)))
    dproj = jnp.concatenate([dz, dxbc_raw] + dqs + [dk, dv, ddt.astype(BF16)], axis=-1)
    g["hy_in_t"] = wgrad(h0, dproj, "hy_in_dw")
    if comm is None:
        dh0 = matmul(dproj, w_in, mode="nn", out_dtype=F32, name="hy_in_dx")
    else:
        gb = pack_grads(g, GB_LAYOUT, GB_ROWS)
        half = GB_ROWS // 2
        theirs = swap_halves(gb, name="swap_in_halves")
        ours = lax.dynamic_slice_in_dim(gb, lax.axis_index("c") * half, half, axis=1)
        comm["gb"] = rowmap(f_add, [ours.reshape(N_CHIPS * half, D), theirs.reshape(N_CHIPS * half, D)], [], [BF16],
                            name="sum_in_cores")[0].reshape(N_CHIPS, half, D)
        dh0, comm["recv_b"] = matmul(dproj, w_in, mode="nn", out_dtype=F32, name="hy_in_dx",
                                     side=ScatterRows(comm["gb"], lax.empty((3, half, D), BF16), 0, half))
    (dres,), (dg_, dsc, dsh), _ = rowmap_bwd(f_normmod, [x], [sp["norm_mix_g"][0], mods[0][1], mods[0][0]], [dh0],
                                             name="l0_norm_b", row_grad=[True], row_add=[dres])
    g["norm_mix_g0"] = dg_
    dmods[0][1], dmods[0][0] = dsc, dsh
    return loss_rows, dres, g, dmods


ANY = pl.BlockSpec(memory_space=pl.ANY)
WHOLE_VMEM = pl.BlockSpec(memory_space=pltpu.VMEM)


def _place():
    return lax.axis_index("x"), lax.axis_index("y"), lax.axis_index("c")


def _other_chips(x, y):
    return [(1 - x, y), (x, 1 - y), (1 - x, 1 - y)]


def allgather_small(v, *, name, side=None):
    m_per = v.shape[0]

    def gather(x_ref, out_ref, send_sems, recv_sems, local_sem):
        x, y, c = _place()
        me, sibling = (x, y, c), (x, y, 1 - c)
        chips = _other_chips(x, y)

        def rows(px, py, pc):
            return out_ref.at[pl.ds((4 * px + 2 * py + pc) * m_per, m_per), :]

        def copy(k, block, to, src=None):
            return pltpu.make_async_remote_copy(
                src_ref=rows(*block) if src is None else src, dst_ref=rows(*block),
                send_sem=send_sems.at[k], recv_sem=recv_sems.at[k], device_id=to, device_id_type=MESH)

        mine = pltpu.make_async_copy(x_ref, rows(*me), local_sem)
        mine.start()
        first = [copy(0, me, sibling, src=x_ref)]
        first += [copy(1 + j, me, (*chip, c), src=x_ref) for j, chip in enumerate(chips)]
        for cp in first:
            cp.start()
        passed = [copy(4 + j, (*chip, c), sibling) for j, chip in enumerate(chips)]
        for j, chip in enumerate(chips):
            copy(1 + j, (*chip, c), me).wait_recv()
            passed[j].start()
        copy(0, sibling, me).wait_recv()
        for j, chip in enumerate(chips):
            copy(4 + j, (*chip, 1 - c), me).wait_recv()
        for cp in first + passed:
            cp.wait_send()
        mine.wait()

    out = jax.ShapeDtypeStruct((N_DEV * m_per, LANES), v.dtype)
    sems = [pltpu.SemaphoreType.DMA((7,)), pltpu.SemaphoreType.DMA((7,)), pltpu.SemaphoreType.DMA]
    if side is None:
        return pl.pallas_call(gather, name=name, out_shape=out, in_specs=[WHOLE_VMEM], out_specs=WHOLE_VMEM,
                              scratch_shapes=sems)(v)

    def body(x_ref, src_ref, dst_in_ref, out_ref, dst_ref, send_sems, recv_sems, local_sem, *side_sems):
        side.start(src_ref, dst_ref, side_sems)
        gather(x_ref, out_ref, send_sems, recv_sems, local_sem)
        side.finish(src_ref, dst_ref, side_sems)

    return pl.pallas_call(
        body, name=name, out_shape=[out, jax.ShapeDtypeStruct(side.dst.shape, side.dst.dtype)],
        in_specs=[WHOLE_VMEM, ANY, ANY], out_specs=[WHOLE_VMEM, ANY], scratch_shapes=sems + side.sems(),
        input_output_aliases={2: 1},
    )(v, side.src, side.dst)


def swap_halves(gpack, *, name):
    half_rows = gpack.shape[1] // 2

    def body(g_ref, r_ref, send_sems, recv_sems):
        x, y, c = _place()
        its_half = pl.ds((1 - c) * half_rows, half_rows)
        copies = [pltpu.make_async_remote_copy(
            src_ref=g_ref.at[s, its_half], dst_ref=r_ref.at[s], send_sem=send_sems.at[s], recv_sem=recv_sems.at[s],
            device_id=(x, y, 1 - c), device_id_type=MESH) for s in range(N_CHIPS)]
        for cp in copies:
            cp.start()
        for cp in copies:
            cp.wait()

    return pl.pallas_call(
        body, name=name,
        out_shape=jax.ShapeDtypeStruct((N_CHIPS, half_rows) + gpack.shape[2:], gpack.dtype),
        in_specs=[ANY], out_specs=ANY,
        scratch_shapes=[pltpu.SemaphoreType.DMA((N_CHIPS,)), pltpu.SemaphoreType.DMA((N_CHIPS,))],
    )(gpack)


class GatherRows:
    def __init__(self, pack, full, lo, hi):
        assert (hi - lo) % 32 == 0 and lo % 16 == 0
        self.src, self.dst, self.lo, self.hi = pack, full, lo, hi

    def sems(self):
        return [pltpu.SemaphoreType.DMA((6,)), pltpu.SemaphoreType.DMA((6,)), pltpu.SemaphoreType.DMA]

    def _parts(self, pack_ref, full_ref, sems):
        send_sems, recv_sems, local_sem = sems
        x, y, c = _place()
        half = (self.hi - self.lo) // 2
        mine, its = pl.ds(self.lo + c * half, half), pl.ds(self.lo + (1 - c) * half, half)
        rows = pl.ds(self.lo, self.hi - self.lo)
        local = pltpu.make_async_copy(pack_ref.at[rows], full_ref.at[2 * x + y, rows], local_sem)
        chips = _other_chips(x, y)

        def remote(src, dst, k, to):
            return pltpu.make_async_remote_copy(src_ref=src, dst_ref=dst, send_sem=send_sems.at[k],
                                                recv_sem=recv_sems.at[k], device_id=to, device_id_type=MESH)

        sends = [remote(pack_ref.at[mine], full_ref.at[2 * x + y, mine], k, (cx, cy, c)) for k, (cx, cy) in enumerate(chips)]
        landed = [full_ref.at[2 * cx + cy, mine] for cx, cy in chips]
        arrive = [remote(pack_ref.at[mine], landed[k], k, (cx, cy, c)) for k, (cx, cy) in enumerate(chips)]
        passed = [remote(landed[k], landed[k], 3 + k, (x, y, 1 - c)) for k in range(3)]
        from_sibling = [remote(landed[k], full_ref.at[2 * cx + cy, its], 3 + k, (x, y, 1 - c))
                        for k, (cx, cy) in enumerate(chips)]
        return local, sends, arrive, passed, from_sibling

    def start(self, pack_ref, full_ref, sems):
        local, sends, _, _, _ = self._parts(pack_ref, full_ref, sems)
        local.start()
        for cp in sends:
            cp.start()

    def finish(self, pack_ref, full_ref, sems):
        local, sends, arrive, passed, from_sibling = self._parts(pack_ref, full_ref, sems)
        for k in range(3):
            arrive[k].wait_recv()
            passed[k].start()
        for cp in from_sibling:
            cp.wait_recv()
        for cp in sends + passed:
            cp.wait_send()
        local.wait()


class ScatterRows:
    def __init__(self, gpack, recv, lo, hi):
        assert lo % 16 == 0 and hi % 16 == 0
        self.src, self.dst, self.lo, self.hi = gpack, recv, lo, hi

    def sems(self):
        return [pltpu.SemaphoreType.DMA((3,)), pltpu.SemaphoreType.DMA((3,))]

    def _parts(self, g_ref, recv_ref, sems):
        send_sems, recv_sems = sems
        x, y, c = _place()
        rows = pl.ds(self.lo, self.hi - self.lo)
        return [pltpu.make_async_remote_copy(
            src_ref=g_ref.at[2 * cx + cy, rows], dst_ref=recv_ref.at[k, rows], send_sem=send_sems.at[k],
            recv_sem=recv_sems.at[k], device_id=(cx, cy, c), device_id_type=MESH)
            for k, (cx, cy) in enumerate(_other_chips(x, y))]

    def start(self, g_ref, recv_ref, sems):
        for cp in self._parts(g_ref, recv_ref, sems):
            cp.start()

    def finish(self, g_ref, recv_ref, sems):
        sends = self._parts(g_ref, recv_ref, sems)
        for cp in sends:
            cp.wait_recv()
        for cp in sends:
            cp.wait_send()


def side_call(side, *, name):
    def body(src_ref, dst_in_ref, dst_ref, *sems):
        side.start(src_ref, dst_ref, sems)
        side.finish(src_ref, dst_ref, sems)

    return pl.pallas_call(
        body, name=name, out_shape=jax.ShapeDtypeStruct(side.dst.shape, side.dst.dtype),
        in_specs=[ANY, ANY], out_specs=ANY, scratch_shapes=side.sems(), input_output_aliases={1: 0},
    )(side.src, side.dst)


def grid_call(body, args, *, name, out_shape, grid, in_specs, out_specs, scratch_shapes, semantics, side=None):
    if side is None:
        res = pl.pallas_call(body, name=name, out_shape=out_shape, grid=grid, in_specs=in_specs, out_specs=out_specs,
                             scratch_shapes=scratch_shapes, compiler_params=_cparams(*semantics))(*args)
        return res, None
    n_in, n_out, n_scr = len(args), len(out_shape), len(scratch_shapes)

    def wrapped(*refs):
        ins, (src_ref, _) = refs[:n_in], refs[n_in:n_in + 2]
        outs, dst_ref = refs[n_in + 2:n_in + 2 + n_out], refs[n_in + 2 + n_out]
        scr, sems = refs[n_in + 3 + n_out:n_in + 3 + n_out + n_scr], refs[n_in + 3 + n_out + n_scr:]
        first = functools.reduce(jnp.logical_and, [pl.program_id(i) == 0 for i in range(len(grid))])
        last = functools.reduce(jnp.logical_and, [pl.program_id(i) == n - 1 for i, n in enumerate(grid)])

        @pl.when(first)
        def _():
            side.start(src_ref, dst_ref, sems)

        body(*ins, *outs, *scr)

        @pl.when(last)
        def _():
            side.finish(src_ref, dst_ref, sems)

    res = pl.pallas_call(
        wrapped, name=name,
        out_shape=list(out_shape) + [jax.ShapeDtypeStruct(side.dst.shape, side.dst.dtype)],
        grid=grid, in_specs=list(in_specs) + [ANY, ANY], out_specs=list(out_specs) + [ANY],
        scratch_shapes=list(scratch_shapes) + side.sems(), input_output_aliases={n_in + 1: n_out},
        compiler_params=_cparams(*(["arbitrary"] * len(grid))),
    )(*args, side.src, side.dst)
    return res[:-1], res[-1]


def sibling_swap(p, *, name):
    def body(p_ref, r_ref, send_sem, recv_sem):
        x, y, c = _place()
        cp = pltpu.make_async_remote_copy(src_ref=p_ref, dst_ref=r_ref, send_sem=send_sem, recv_sem=recv_sem,
                                          device_id=(x, y, 1 - c), device_id_type=MESH)
        cp.start()
        cp.wait()

    return pl.pallas_call(
        body, name=name, out_shape=jax.ShapeDtypeStruct(p.shape, p.dtype),
        in_specs=[ANY], out_specs=ANY,
        scratch_shapes=[pltpu.SemaphoreType.DMA, pltpu.SemaphoreType.DMA],
    )(p)


def sum_slots(own, recv, *, name):
    r_dim, c_dim = own.shape
    tr = _pick(r_dim, 512, mult=16)

    def body(o_ref, r_ref, out_ref):
        acc = o_ref[...].astype(F32)
        for k in range(3):
            acc = acc + r_ref[k].astype(F32)
        out_ref[...] = acc

    return pl.pallas_call(
        body, name=name, out_shape=jax.ShapeDtypeStruct((r_dim, c_dim), F32), grid=(r_dim // tr,),
        in_specs=[pl.BlockSpec((tr, c_dim), lambda i: (i, 0)), pl.BlockSpec((3, tr, c_dim), lambda i: (0, i, 0))],
        out_specs=pl.BlockSpec((tr, c_dim), lambda i: (i, 0)),
        compiler_params=_cparams("parallel"),
    )(own, recv)


def sum_devices(v_all, *, name):
    m_per = v_all.shape[0] // N_DEV

    def body(v_ref, o_ref):
        acc = v_ref[pl.ds(0, m_per), :]
        for d in range(1, N_DEV):
            acc = acc + v_ref[pl.ds(d * m_per, m_per), :]
        o_ref[...] = acc

    return pl.pallas_call(
        body, name=name, out_shape=jax.ShapeDtypeStruct((m_per, LANES), F32),
        in_specs=[WHOLE_VMEM], out_specs=WHOLE_VMEM,
    )(v_all)


WEIGHTS = ['ada_w', 'ada_b', 'norm_mix_g', 'norm_ffn_g', 'hy_w_in', 'hy_conv_w', 'hy_conv_b', 'hy_dt_bias', 'hy_a_log',
           'hy_d_skip', 'hy_ssm_norm_g', 'hy_w_out', 'rel_table', 'cv_w_pw1', 'cv_b_pw1', 'cv_w_dw', 'cv_b_dw', 'cv_ln_g',
           'cv_ln_b', 'cv_w_pw2', 'cv_b_pw2', 'ffn_w_gate', 'ffn_w_up', 'ffn_w_down', 'final_norm_g']
BIG = ('ada_w', 'hy_w_in', 'hy_w_out', 'cv_w_pw1', 'cv_w_pw2', 'ffn_w_gate', 'ffn_w_up', 'ffn_w_down')
SMALL_SHARDED = {'hy_conv_w': (1, 4, 3072), 'cv_b_pw1': (1, 2048), 'cv_w_dw': (1, 31, 1024), 'cv_b_dw': (1, 1024),
                 'cv_ln_g': (1, 1024), 'cv_ln_b': (1, 1024), 'cv_b_pw2': (1, 1024)}
SMALL_GRADS = {'ada_b': (2, 6144), 'norm_mix_g': (2, 1024), 'norm_ffn_g': (2, 1024), 'hy_conv_w': (1, 4, 3072),
               'hy_conv_b': (1, 3072), 'hy_dt_bias': (1, 32), 'hy_a_log': (1, 32), 'hy_d_skip': (1, 32),
               'hy_ssm_norm_g': (1, 2048), 'rel_table': (32, 48), 'cv_b_pw1': (1, 2048), 'cv_w_dw': (1, 31, 1024),
               'cv_b_dw': (1, 1024), 'cv_ln_g': (1, 1024), 'cv_ln_b': (1, 1024), 'cv_b_pw2': (1, 1024),
               'final_norm_g': (1024,), 'loss': (1,)}

PACK_LAYOUT = (('hy_in_t', 2568), ('hy_out', 768), ('pw1_t', 512), ('pw2', 256),
               ('gate_t0', 704), ('up_t0', 704), ('down0', 704), ('gate_t1', 704), ('up_t1', 704), ('down1', 704))
PACK_ROWS = 8448


def _pack_offsets(layout):
    off, out = 0, {}
    for nm, r in layout:
        out[nm] = (off, r)
        off += r
    return out


PACK_OFF = _pack_offsets(PACK_LAYOUT)
W_BATCHES = ((0, 2624), (2624, 4992), (4992, 6144), (6144, 7296), (7296, 8448))
GA_LAYOUT = PACK_LAYOUT[1:]
GA_ROWS = 5888
GA_OFF = _pack_offsets(GA_LAYOUT)
G_BATCHES = ((0, 2560), (2560, 3712), (3712, 4864), (4864, 5888))
GB_LAYOUT = PACK_LAYOUT[:1]
GB_ROWS = 2816


def pack_grads(g, layout, n_rows):
    def rows_bf16(nm):
        return g[nm]

    parts = []
    for key, r in layout:
        if key == 'hy_in_t':
            a = hy_from_cat(rows_bf16('hy_in_t'))
        elif key.startswith('gate_t'):
            a = rows_bf16('gu_t' + key[-1])[:FFN_HIDDEN]
        elif key.startswith('up_t'):
            a = rows_bf16('gu_t' + key[-1])[FFN_HIDDEN:]
        else:
            a = rows_bf16(key)
        parts.append(a.reshape(N_CHIPS, r, D))
    used = sum(r for _, r in layout)
    return jnp.concatenate(parts + [jnp.zeros((N_CHIPS, n_rows - used, D), BF16)], axis=1)


def unpack_weights(full, skip=()):
    def whole(nm):
        o, r = PACK_OFF[nm]
        return full[:, o:o + r].reshape(N_CHIPS * r, D)

    out = {"hy_out": whole('hy_out'), "pw1_t": whole('pw1_t'), "pw2": whole('pw2'),
           "gu_t": [jnp.concatenate([whole(f'gate_t{i}'), whole(f'up_t{i}')], axis=0) for i in range(2)],
           "down": [whole(f'down{i}') for i in range(2)]}
    if "hy_in_t" not in skip:
        out["hy_in_t"] = hy_to_cat(whole('hy_in_t'))
    return out


def _to_lanes(flat):
    n = flat.shape[0]
    m = -(-n // (8 * LANES)) * 8
    return jnp.pad(flat, (0, m * LANES - n)).reshape(m, LANES)


def _split(flat, shapes):
    out, off = {}, 0
    for nm, shp in shapes.items():
        n = int(np.prod(shp))
        out[nm] = flat[off:off + n].reshape(shp)
        off += n
    return out


def kernel(x, c, ada_w, ada_b, norm_mix_g, norm_ffn_g, hy_w_in, hy_conv_w, hy_conv_b, hy_dt_bias, hy_a_log, hy_d_skip, hy_ssm_norm_g, hy_w_out, rel_table, cv_w_pw1, cv_b_pw1, cv_w_dw, cv_b_dw, cv_ln_g, cv_ln_b, cv_w_pw2, cv_b_pw2, ffn_w_gate, ffn_w_up, ffn_w_down, final_norm_g, loss_target, m_ada_w, m_ada_b, m_norm_mix_g, m_norm_ffn_g, m_hy_w_in, m_hy_conv_w, m_hy_conv_b, m_hy_dt_bias, m_hy_a_log, m_hy_d_skip, m_hy_ssm_norm_g, m_hy_w_out, m_rel_table, m_cv_w_pw1, m_cv_b_pw1, m_cv_w_dw, m_cv_b_dw, m_cv_ln_g, m_cv_ln_b, m_cv_w_pw2, m_cv_b_pw2, m_ffn_w_gate, m_ffn_w_up, m_ffn_w_down, m_final_norm_g, v_ada_w, v_ada_b, v_norm_mix_g, v_norm_ffn_g, v_hy_w_in, v_hy_conv_w, v_hy_conv_b, v_hy_dt_bias, v_hy_a_log, v_hy_d_skip, v_hy_ssm_norm_g, v_hy_w_out, v_rel_table, v_cv_w_pw1, v_cv_b_pw1, v_cv_w_dw, v_cv_b_dw, v_cv_ln_g, v_cv_ln_b, v_cv_w_pw2, v_cv_b_pw2, v_ffn_w_gate, v_ffn_w_up, v_ffn_w_down, v_final_norm_g):
    args = (x, c, ada_w, ada_b, norm_mix_g, norm_ffn_g, hy_w_in, hy_conv_w, hy_conv_b, hy_dt_bias, hy_a_log, hy_d_skip, hy_ssm_norm_g, hy_w_out, rel_table, cv_w_pw1, cv_b_pw1, cv_w_dw, cv_b_dw, cv_ln_g, cv_ln_b, cv_w_pw2, cv_b_pw2, ffn_w_gate, ffn_w_up, ffn_w_down, final_norm_g, loss_target, m_ada_w, m_ada_b, m_norm_mix_g, m_norm_ffn_g, m_hy_w_in, m_hy_conv_w, m_hy_conv_b, m_hy_dt_bias, m_hy_a_log, m_hy_d_skip, m_hy_ssm_norm_g, m_hy_w_out, m_rel_table, m_cv_w_pw1, m_cv_b_pw1, m_cv_w_dw, m_cv_b_dw, m_cv_ln_g, m_cv_ln_b, m_cv_w_pw2, m_cv_b_pw2, m_ffn_w_gate, m_ffn_w_up, m_ffn_w_down, m_final_norm_g, v_ada_w, v_ada_b, v_norm_mix_g, v_norm_ffn_g, v_hy_w_in, v_hy_conv_w, v_hy_conv_b, v_hy_dt_bias, v_hy_a_log, v_hy_d_skip, v_hy_ssm_norm_g, v_hy_w_out, v_rel_table, v_cv_w_pw1, v_cv_b_pw1, v_cv_w_dw, v_cv_b_dw, v_cv_ln_g, v_cv_ln_b, v_cv_w_pw2, v_cv_b_pw2, v_ffn_w_gate, v_ffn_w_up, v_ffn_w_down, v_final_norm_g)
    x_in, c_in = args[0], args[1]
    w = dict(zip(WEIGHTS, args[2:27], strict=True))
    tgt = args[27]
    m_in = dict(zip(WEIGHTS, args[28:53], strict=True))
    v_in = dict(zip(WEIGHTS, args[53:78], strict=True))
    xi, yi, ci = _place()
    chip = 2 * xi + yi
    dev = 2 * chip + ci

    cs = rowmap(f_silu, [c_in.reshape(8, LANES)], [], [F32], name="cond_silu", tr=8)[0]
    cs_all = allgather_small(cs, name="gather_cond").reshape(N_DEV, D)
    cs16 = jnp.pad(cs_all, ((0, 8), (0, 0)))
    modpart = jnp.stack([matmul(cs16, w['ada_w'][i], mode="nn", out_dtype=F32, name=f"ada_fwd{i}")[:N_DEV]
                         for i in range(2)], axis=1)
    def rows_of(nm, i=None):
        a = w[nm][0 if i is None else i]
        return (a.T if nm in ('hy_w_in', 'cv_w_pw1', 'ffn_w_gate', 'ffn_w_up') else a).astype(BF16)

    pieces = [rows_of('hy_w_in'), rows_of('hy_w_out'), rows_of('cv_w_pw1'), rows_of('cv_w_pw2')]
    for i in range(2):
        pieces += [rows_of('ffn_w_gate', i), rows_of('ffn_w_up', i), rows_of('ffn_w_down', i)]
    n_rows = sum(p.shape[0] for p in pieces)
    pack = jnp.concatenate(pieces + [jnp.zeros((PACK_ROWS - n_rows, D), BF16)], axis=0)

    shard_names = list(SMALL_SHARDED)
    payload = jnp.concatenate([modpart.reshape(-1)] + [w[nm].reshape(-1) for nm in shard_names])
    got, full = allgather_small(_to_lanes(payload), name="gather_mod",
                                side=GatherRows(pack, lax.empty((N_CHIPS, PACK_ROWS, D), BF16), *W_BATCHES[0]))
    got = got.reshape(N_DEV, -1)[0::2]
    modparts = got[:, :modpart.size].reshape(N_CHIPS, N_DEV, 2, 1536)
    mine = lax.dynamic_index_in_dim(modparts, dev, axis=1, keepdims=False)
    mod = jnp.transpose(mine, (1, 0, 2)).reshape(2, 6 * D) + w['ada_b']
    mods = [[mod[i, j * D:(j + 1) * D].reshape(1, D) for j in range(6)] for i in range(2)]
    sp = {}
    off = modpart.size
    for nm in shard_names:
        shp = w[nm].shape
        n = int(np.prod(shp))
        parts = got[:, off:off + n].reshape((N_CHIPS,) + shp)
        sp[nm] = jnp.concatenate([parts[s] for s in range(N_CHIPS)], axis=-1)
        off += n

    o_in, r_in = PACK_OFF['hy_in_t']
    wts = {"hy_in_t": hy_to_cat(full[:, o_in:o_in + r_in].reshape(N_CHIPS * r_in, D))}
    comm = {"pack": pack, "full": full}

    sp = {"norm_mix_g": [w['norm_mix_g'][i].reshape(1, D) for i in range(2)],
          "norm_ffn_g": [w['norm_ffn_g'][i].reshape(1, D) for i in range(2)],
          "hy_conv_w": sp['hy_conv_w'][0], "hy_conv_b": w['hy_conv_b'],
          "hy_dt_bias": w['hy_dt_bias'].reshape(SSM_HEADS, 1), "hy_a_log": w['hy_a_log'].reshape(SSM_HEADS, 1),
          "hy_d_skip": w['hy_d_skip'].reshape(SSM_HEADS, 1), "hy_ssm_norm_g": w['hy_ssm_norm_g'],
          "rel_table": w['rel_table'], "cv_b_pw1": sp['cv_b_pw1'], "cv_w_dw": sp['cv_w_dw'][0], "cv_b_dw": sp['cv_b_dw'],
          "cv_ln_g": sp['cv_ln_g'], "cv_ln_b": sp['cv_ln_b'], "cv_b_pw2": sp['cv_b_pw2'],
          "final_norm_g": w['final_norm_g'].reshape(1, D)}

    loss_rows, grad_x, g, dmods = device_step(x_in[0], tgt[0], mods, wts, sp, comm)

    dmod = jnp.stack([jnp.concatenate([d.reshape(-1) for d in dmods[i]]) for i in range(2)])
    small = {'ada_b': dmod, 'norm_mix_g': jnp.stack([g[f'norm_mix_g{i}'].reshape(-1) for i in range(2)]),
             'norm_ffn_g': jnp.stack([g[f'norm_ffn_g{i}'].reshape(-1) for i in range(2)]),
             'loss': jnp.sum(loss_rows).reshape(1)}
    for nm in SMALL_GRADS:
        if nm not in small:
            small[nm] = g[nm]
    vec = _to_lanes(jnp.concatenate([small[nm].reshape(-1) for nm in SMALL_GRADS]))
    vec_all = allgather_small(vec, name="gather_small_grads")
    tot = _split(sum_devices(vec_all, name="sum_small_grads").reshape(-1), SMALL_GRADS)
    dmod_all = vec_all.reshape(N_DEV, -1)[:, :2 * 6 * D].reshape(N_DEV, 2, 6 * D)

    recv = comm["recv"]
    own_a = lax.dynamic_index_in_dim(comm["ga"], chip, axis=0, keepdims=False)
    part_a = sum_slots(own_a, recv, name="sum_chip_grads")
    red_a = rowmap(f_add, [part_a, sibling_swap(part_a, name="swap_grads")], [], [F32], name="sum_core_grads")[0]
    recv_b = comm["recv_b"]
    own_b = lax.dynamic_index_in_dim(comm["gb"], chip, axis=0, keepdims=False)
    mine_half = sum_slots(own_b, recv_b, name="sum_in_chips")
    its_half = sibling_swap(mine_half, name="swap_in")
    red_b = jnp.concatenate([jnp.where(ci == 0, mine_half, its_half), jnp.where(ci == 0, its_half, mine_half)], axis=0)

    def shard_grad(nm, i=None):
        key = {'hy_w_in': 'hy_in_t', 'hy_w_out': 'hy_out', 'cv_w_pw1': 'pw1_t', 'cv_w_pw2': 'pw2'}.get(nm)
        if key is None:
            key = {'ffn_w_gate': 'gate_t', 'ffn_w_up': 'up_t', 'ffn_w_down': 'down'}[nm] + str(i)
        if key == 'hy_in_t':
            a = red_b[:PACK_OFF[key][1]]
        else:
            o, r = GA_OFF[key]
            a = red_a[o:o + r]
        return a.T if key.endswith('_t') or key[:-1].endswith('_t') else a

    grads = {}
    grads['hy_w_in'] = shard_grad('hy_w_in')[None]
    grads['hy_w_out'] = shard_grad('hy_w_out')[None]
    grads['cv_w_pw1'] = shard_grad('cv_w_pw1')[None]
    grads['cv_w_pw2'] = shard_grad('cv_w_pw2')[None]
    for nm in ('ffn_w_gate', 'ffn_w_up', 'ffn_w_down'):
        grads[nm] = jnp.stack([shard_grad(nm, i) for i in range(2)])
    cs16 = jnp.pad(cs_all, ((0, 8), (0, 0)))
    dm_mine = lax.dynamic_slice_in_dim(dmod_all, chip * 1536, 1536, axis=2)
    dm16 = jnp.pad(dm_mine, ((0, 8), (0, 0), (0, 0)))
    grads['ada_w'] = jnp.stack([matmul(cs16, dm16[:, i], mode="tn", out_dtype=F32, name=f"ada_dw{i}") for i in range(2)])
    for nm, shp in SMALL_GRADS.items():
        if nm == 'loss':
            continue
        if nm in SMALL_SHARDED:
            n = w[nm].shape[-1]
            grads[nm] = lax.dynamic_slice_in_dim(tot[nm], chip * n, n, axis=len(shp) - 1)
        else:
            grads[nm] = tot[nm].reshape(w[nm].shape)

    delta, new_m, new_v = {}, {}, {}
    for nm in BIG:
        delta[nm], new_m[nm], new_v[nm] = adamw(w[nm], grads[nm], m_in[nm], v_in[nm], name="adamw_" + nm)
    smalls = [nm for nm in WEIGHTS if nm not in BIG]
    packed = [_to_lanes(jnp.concatenate([d[nm].reshape(-1) for nm in smalls])) for d in (w, grads, m_in, v_in)]
    res = rowmap(f_adamw, packed, [], [F32] * 3, name="adamw_small", tr=_rows_tile(packed[0].shape[0]))
    for d, r in zip((delta, new_m, new_v), res, strict=True):
        d.update(_split(r.reshape(-1), {nm: w[nm].shape for nm in smalls}))

    loss = tot['loss'].reshape(())
    return (loss, grad_x[None], *[grads[nm] for nm in WEIGHTS], *[delta[nm] for nm in WEIGHTS],
            *[new_m[nm] for nm in WEIGHTS], *[new_v[nm] for nm in WEIGHTS])
```

```python
import functools
import math

import jax
import jax.numpy as jnp
import numpy as np
from jax import lax
from jax.experimental import pallas as pl
from jax.experimental.pallas import tpu as pltpu

F32 = jnp.float32
BF16 = jnp.bfloat16
MESH = pl.DeviceIdType.MESH

D = 1024
S = 4096
EPS = 1e-6
SSM_INNER = 2048
SSM_HEADS = 32
SSM_HDIM = 64
SSM_GROUPS = 4
SSM_STATE = 128
SSM_CONVK = 4
SSM_CONV_DIM = 3072
CHUNK = 128
N_CHUNKS = S // CHUNK
ATT_HEADS = 16
ATT_HDIM = 64
ATT_PATTERNS = ((128, 1), (512, 4), (2048, 16))
ATT_BLK = 128
REL_BUCKETS = 32
REL_MAX_DIST = 2048
CONV_WIDTH = 31
FFN_HIDDEN = 2816
N_CHIPS = 4
N_DEV = 8
ADAM_LR, ADAM_B1, ADAM_B2, ADAM_EPS, ADAM_WD, ADAM_STEP = 0.001, 0.9, 0.999, 1e-08, 0.01, 10

VMEM_LIMIT_BYTES = 56 * 1024 * 1024
LANES = 128


def _cparams(*sem):
    return pltpu.CompilerParams(dimension_semantics=sem, vmem_limit_bytes=VMEM_LIMIT_BYTES)


def _pick(n, cap, mult=LANES):
    best = None
    for t in range(mult, min(n, cap) + 1, mult):
        if n % t == 0:
            best = t
    return best or n


def _dot(a, b, ca, cb):
    return lax.dot_general(a.astype(BF16), b.astype(BF16), (((ca,), (cb,)), ((), ())), preferred_element_type=F32)


@jax.custom_vjp
def mm_nt(a, b):
    return _dot(a, b, 1, 1)


def _mm_nt_fwd(a, b):
    return _dot(a, b, 1, 1), (a, b)


def _mm_nt_bwd(res, g):
    a, b = res
    return _dot(g, b, 1, 0).astype(a.dtype), _dot(g, a, 0, 0).astype(b.dtype)


mm_nt.defvjp(_mm_nt_fwd, _mm_nt_bwd)


def matmul(a, b, *, mode, out_dtype, name, n=None, b_off=0, tm_cap=1024, tn_cap=512, tk_cap=3584, side=None,
           out_t=False):
    if mode == "tn":
        k_dim, m_dim = a.shape
    else:
        m_dim, k_dim = a.shape
    n_dim = n if n is not None else (b.shape[0] if mode == "nt" else b.shape[1])
    tm = m_dim if m_dim < LANES else _pick(m_dim, tm_cap)
    tn = _pick(n_dim, tn_cap)
    tk = k_dim if k_dim < LANES else _pick(k_dim, tk_cap)
    assert m_dim % tm == 0 and n_dim % tn == 0 and k_dim % tk == 0 and b_off % tn == 0
    nk = k_dim // tk
    off = b_off // tn
    if mode == "nn":
        a_spec = pl.BlockSpec((tm, tk), lambda i, j, k: (i, k))
        b_spec = pl.BlockSpec((tk, tn), lambda i, j, k: (k, j))
        ca, cb = 1, 0
    elif mode == "nt":
        a_spec = pl.BlockSpec((tm, tk), lambda i, j, k: (i, k))
        b_spec = pl.BlockSpec((tn, tk), lambda i, j, k: (j + off, k))
        ca, cb = 1, 1
    else:
        a_spec = pl.BlockSpec((tk, tm), lambda i, j, k: (k, i))
        b_spec = pl.BlockSpec((tk, tn), lambda i, j, k: (k, j))
        ca, cb = 0, 0

    def emit(o_ref, val):
        o_ref[...] = (val.T if out_t else val).astype(o_ref.dtype)

    def body(a_ref, b_ref, o_ref, acc_ref):
        part = _dot(a_ref[...], b_ref[...], ca, cb)
        if nk == 1:
            emit(o_ref, part)
        else:
            k = pl.program_id(2)

            @pl.when(k == 0)
            def _():
                acc_ref[...] = part

            @pl.when(k > 0)
            def _():
                acc_ref[...] += part

            @pl.when(k == nk - 1)
            def _():
                emit(o_ref, acc_ref[...])

    if out_t:
        out_shape, out_spec = (n_dim, m_dim), pl.BlockSpec((tn, tm), lambda i, j, k: (j, i))
    else:
        out_shape, out_spec = (m_dim, n_dim), pl.BlockSpec((tm, tn), lambda i, j, k: (i, j))
    (out,), side_dst = grid_call(
        body, (a, b), name=name,
        out_shape=[jax.ShapeDtypeStruct(out_shape, out_dtype)],
        grid=(m_dim // tm, n_dim // tn, nk),
        in_specs=[a_spec, b_spec],
        out_specs=[out_spec],
        scratch_shapes=[pltpu.VMEM((tm, tn), F32)],
        semantics=("parallel", "parallel", "arbitrary"), side=side)
    return out if side is None else (out, side_dst)


def _f32(xs):
    return [x.astype(F32) for x in xs]


def rowmap(f, rows, consts, out_dtypes, *, name, tr=1024):
    r_dim = rows[0].shape[0]
    tr = _pick(r_dim, tr, mult=8)
    assert r_dim % tr == 0
    nr, nc = len(rows), len(consts)
    outs = jax.eval_shape(lambda *xs: f(*xs), *[jax.ShapeDtypeStruct((tr, x.shape[1]), F32) for x in rows],
                          *[jax.ShapeDtypeStruct(x.shape, F32) for x in consts])

    def body(*refs):
        res = f(*_f32([r[...] for r in refs[:nr + nc]]))
        for o_ref, o in zip(refs[nr + nc:], res, strict=True):
            o_ref[...] = o.astype(o_ref.dtype)

    return pl.pallas_call(
        body, name=name,
        out_shape=[jax.ShapeDtypeStruct((r_dim, o.shape[1]), dt) for o, dt in zip(outs, out_dtypes, strict=True)],
        grid=(r_dim // tr,),
        in_specs=[pl.BlockSpec((tr, x.shape[1]), lambda i: (i, 0)) for x in rows]
        + [pl.BlockSpec(x.shape, lambda i: (0, 0)) for x in consts],
        out_specs=[pl.BlockSpec((tr, o.shape[1]), lambda i: (i, 0)) for o in outs],
        compiler_params=_cparams("parallel"),
    )(*rows, *consts)


def rowmap_bwd(f, rows, consts, cts, *, name, row_grad, row_dtypes=None, tr=512, emit=(), row_add=None,
               emit_dtype=F32):
    r_dim = rows[0].shape[0]
    tr = _pick(r_dim, tr, mult=8)
    assert r_dim % tr == 0
    nr, nc, nct = len(rows), len(consts), len(cts)
    gi = [i for i, flag in enumerate(row_grad) if flag]
    row_dtypes = row_dtypes or [F32] * len(gi)
    row_add = row_add or [None] * len(gi)
    adds = [a for a in row_add if a is not None]
    outs = jax.eval_shape(lambda *xs: f(*xs), *[jax.ShapeDtypeStruct((tr, x.shape[1]), F32) for x in rows],
                          *[jax.ShapeDtypeStruct(x.shape, F32) for x in consts])

    def body(*refs):
        ins = _f32([r[...] for r in refs[:nr + nc]])
        ct = _f32([r[...] for r in refs[nr + nc:nr + nc + nct]])
        add_refs = list(refs[nr + nc + nct:nr + nc + nct + len(adds)])
        o_refs = refs[nr + nc + nct + len(adds):]
        res, vjp = jax.vjp(f, *ins)
        grads = vjp(tuple(ct))
        for o_ref, i, a in zip(o_refs[:len(gi)], gi, row_add):
            g = grads[i] if a is None else grads[i] + add_refs.pop(0)[...].astype(F32)
            o_ref[...] = g.astype(o_ref.dtype)
        first = pl.program_id(0) == 0
        for o_ref, g in zip(o_refs[len(gi):len(gi) + nc], grads[nr:]):
            @pl.when(first)
            def _(o_ref=o_ref, g=g):
                o_ref[...] = g

            @pl.when(jnp.logical_not(first))
            def _(o_ref=o_ref, g=g):
                o_ref[...] += g
        for o_ref, i in zip(o_refs[len(gi) + nc:], emit):
            o_ref[...] = res[i].astype(o_ref.dtype)

    out_shape = ([jax.ShapeDtypeStruct(rows[i].shape, dt) for i, dt in zip(gi, row_dtypes, strict=True)]
                 + [jax.ShapeDtypeStruct(x.shape, F32) for x in consts]
                 + [jax.ShapeDtypeStruct((r_dim, outs[i].shape[1]), emit_dtype) for i in emit])
    out_specs = ([pl.BlockSpec((tr, rows[i].shape[1]), lambda i_: (i_, 0)) for i in gi]
                 + [pl.BlockSpec(x.shape, lambda i_: (0, 0)) for x in consts]
                 + [pl.BlockSpec((tr, outs[i].shape[1]), lambda i_: (i_, 0)) for i in emit])
    res = pl.pallas_call(
        body, name=name,
        out_shape=out_shape,
        grid=(r_dim // tr,),
        in_specs=[pl.BlockSpec((tr, x.shape[1]), lambda i: (i, 0)) for x in rows]
        + [pl.BlockSpec(x.shape, lambda i: (0, 0)) for x in consts]
        + [pl.BlockSpec((tr, x.shape[1]), lambda i: (i, 0)) for x in list(cts) + adds],
        out_specs=out_specs,
        compiler_params=_cparams("arbitrary"),
    )(*rows, *consts, *cts, *adds)
    return res[:len(gi)], res[len(gi):len(gi) + nc], res[len(gi) + nc:]


def matmul_swiglu(hgu, w, *, name, tm=512, tk_cap=1536):
    m_dim, hid = hgu.shape[0], hgu.shape[1] // 2
    n_dim = w.shape[1]
    tk = _pick(hid, tk_cap)
    nk = hid // tk
    assert m_dim % tm == 0 and hid % tk == 0

    def body(g_ref, u_ref, w_ref, o_ref, acc_ref):
        gate, up = g_ref[...].astype(F32), u_ref[...].astype(F32)
        part = _dot(_silu(gate) * up, w_ref[...], 1, 0)
        k = pl.program_id(1)

        @pl.when(k == 0)
        def _():
            acc_ref[...] = part

        @pl.when(k > 0)
        def _():
            acc_ref[...] += part

        @pl.when(k == nk - 1)
        def _():
            o_ref[...] = acc_ref[...]

    return pl.pallas_call(
        body, name=name, out_shape=jax.ShapeDtypeStruct((m_dim, n_dim), F32), grid=(m_dim // tm, nk),
        in_specs=[pl.BlockSpec((tm, tk), lambda i, k: (i, k)), pl.BlockSpec((tm, tk), lambda i, k: (i, k + nk)),
                  pl.BlockSpec((tk, n_dim), lambda i, k: (k, 0))],
        out_specs=pl.BlockSpec((tm, n_dim), lambda i, k: (i, 0)),
        scratch_shapes=[pltpu.VMEM((tm, n_dim), F32)],
        compiler_params=_cparams("parallel", "arbitrary"),
    )(hgu, hgu, w)


def transpose(a, *, name, out_dtype=BF16, tr=512, tc=512):
    r_dim, c_dim = a.shape
    tr, tc = _pick(r_dim, tr), _pick(c_dim, tc)

    def body(a_ref, o_ref):
        o_ref[...] = a_ref[...].astype(F32).T.astype(o_ref.dtype)

    return pl.pallas_call(
        body, name=name, out_shape=jax.ShapeDtypeStruct((c_dim, r_dim), out_dtype),
        grid=(r_dim // tr, c_dim // tc),
        in_specs=[pl.BlockSpec((tr, tc), lambda i, j: (i, j))],
        out_specs=pl.BlockSpec((tc, tr), lambda i, j: (j, i)),
        compiler_params=_cparams("parallel", "parallel"),
    )(a)


CONV_HALO = 32
CONV_ROWS = 256


def conv_fwd(x, w, b, *, name, cb=256, chunk_rows=CONV_ROWS):
    s_dim, c_dim = x.shape
    taps = w.shape[0]
    assert taps - 1 <= CONV_HALO and s_dim % chunk_rows == 0 and c_dim % cb == 0
    n_chunks = s_dim // chunk_rows
    ext = chunk_rows + CONV_HALO

    def body(x_ref, w_ref, b_ref, o_ref, xp_ref):
        xp_ref[pl.ds(0, CONV_HALO), :] = jnp.zeros((CONV_HALO, cb), F32)
        xp_ref[pl.ds(CONV_HALO, s_dim), :] = x_ref[...].astype(F32)
        wv = w_ref[...].astype(F32)
        bv = b_ref[...].astype(F32)

        def chunk(t, carry):
            base = pl.multiple_of(t * chunk_rows, chunk_rows)
            xe = xp_ref[pl.ds(base, ext), :]
            acc = jnp.broadcast_to(bv, (chunk_rows, cb))
            for j in range(taps):
                sh = xe if j == 0 else pltpu.roll(xe, shift=j, axis=0)
                acc = acc + wv[taps - 1 - j:taps - j, :] * sh[CONV_HALO:, :]
            o_ref[pl.ds(base, chunk_rows), :] = acc
            return carry

        lax.fori_loop(0, n_chunks, chunk, 0)

    return pl.pallas_call(
        body, name=name,
        out_shape=jax.ShapeDtypeStruct((s_dim, c_dim), F32),
        grid=(c_dim // cb,),
        in_specs=[pl.BlockSpec((s_dim, cb), lambda i: (0, i)), pl.BlockSpec((taps, cb), lambda i: (0, i)),
                  pl.BlockSpec((1, cb), lambda i: (0, i))],
        out_specs=pl.BlockSpec((s_dim, cb), lambda i: (0, i)),
        scratch_shapes=[pltpu.VMEM((s_dim + CONV_HALO, cb), F32)],
        compiler_params=_cparams("parallel"),
    )(x, w, b)


def conv_bwd(x, w, g, *, name, cb=256, chunk_rows=CONV_ROWS, dx_dtype=F32):
    s_dim, c_dim = x.shape
    taps = w.shape[0]
    n_chunks = s_dim // chunk_rows
    ext = chunk_rows + CONV_HALO

    def rows8(a):
        return jnp.sum(a.reshape(chunk_rows // 8, 8, cb), axis=0)

    def body(x_ref, w_ref, g_ref, dx_ref, dw_ref, db_ref, xp_ref, gp_ref, acc_ref):
        xp_ref[pl.ds(0, CONV_HALO), :] = jnp.zeros((CONV_HALO, cb), F32)
        xp_ref[pl.ds(CONV_HALO, s_dim), :] = x_ref[...].astype(F32)
        gp_ref[pl.ds(0, s_dim), :] = g_ref[...].astype(F32)
        gp_ref[pl.ds(s_dim, CONV_HALO), :] = jnp.zeros((CONV_HALO, cb), F32)
        acc_ref[...] = jnp.zeros_like(acc_ref)
        wv = w_ref[...].astype(F32)

        def chunk(t, carry):
            base = pl.multiple_of(t * chunk_rows, chunk_rows)
            xe = xp_ref[pl.ds(base, ext), :]
            ge = gp_ref[pl.ds(base, ext), :]
            gc = ge[:chunk_rows, :]
            dx = jnp.zeros((chunk_rows, cb), F32)
            for j in range(taps):
                xs = xe if j == 0 else pltpu.roll(xe, shift=j, axis=0)
                gs = ge if j == 0 else pltpu.roll(ge, shift=ext - j, axis=0)
                k = taps - 1 - j
                dx = dx + wv[k:k + 1, :] * gs[:chunk_rows, :]
                acc_ref[8 * k:8 * k + 8, :] += rows8(gc * xs[CONV_HALO:, :])
            acc_ref[8 * taps:8 * taps + 8, :] += rows8(gc)
            dx_ref[pl.ds(base, chunk_rows), :] = dx.astype(dx_ref.dtype)
            return carry

        lax.fori_loop(0, n_chunks, chunk, 0)
        sums = jnp.sum(acc_ref[...].reshape(taps + 1, 8, cb), axis=1)
        dw_ref[...] = sums[0:taps, :]
        db_ref[...] = sums[taps:taps + 1, :]

    return pl.pallas_call(
        body, name=name,
        out_shape=[jax.ShapeDtypeStruct((s_dim, c_dim), dx_dtype), jax.ShapeDtypeStruct((taps, c_dim), F32),
                   jax.ShapeDtypeStruct((1, c_dim), F32)],
        grid=(c_dim // cb,),
        in_specs=[pl.BlockSpec((s_dim, cb), lambda i: (0, i)), pl.BlockSpec((taps, cb), lambda i: (0, i)),
                  pl.BlockSpec((s_dim, cb), lambda i: (0, i))],
        out_specs=[pl.BlockSpec((s_dim, cb), lambda i: (0, i)), pl.BlockSpec((taps, cb), lambda i: (0, i)),
                   pl.BlockSpec((1, cb), lambda i: (0, i))],
        scratch_shapes=[pltpu.VMEM((s_dim + CONV_HALO, cb), F32), pltpu.VMEM((s_dim + CONV_HALO, cb), F32),
                        pltpu.VMEM((8 * (taps + 1), cb), F32)],
        compiler_params=_cparams("parallel"),
    )(x, w, g)


def _softplus(x):
    return jnp.maximum(x, 0.0) + jnp.log(1.0 + jnp.exp(-jnp.abs(x)))


HEADS_PER_GROUP = SSM_HEADS // SSM_GROUPS


def _bdot(a, b, ca, cb):
    return lax.dot_general(a.astype(BF16), b.astype(BF16), (((ca,), (cb,)), ((0,), (0,))), preferred_element_type=F32)


@jax.custom_vjp
def bmm(a, b):
    return _bdot(a, b, 2, 1)


def _bmm_fwd(a, b):
    return _bdot(a, b, 2, 1), (a, b)


def _bmm_bwd(res, g):
    a, b = res
    return _bdot(g, b, 2, 2).astype(a.dtype), _bdot(a, g, 1, 1).astype(b.dtype)


bmm.defvjp(_bmm_fwd, _bmm_bwd)


@jax.custom_vjp
def bmm_nt(a, b):
    return _bdot(a, b, 2, 2)


def _bmm_nt_fwd(a, b):
    return _bdot(a, b, 2, 2), (a, b)


def _bmm_nt_bwd(res, g):
    a, b = res
    return _bdot(g, b, 2, 1).astype(a.dtype), _bdot(g, a, 1, 1).astype(b.dtype)


bmm_nt.defvjp(_bmm_nt_fwd, _bmm_nt_bwd)


@jax.custom_vjp
def bmm_tn(a, b):
    return _bdot(a, b, 1, 1)


def _bmm_tn_fwd(a, b):
    return _bdot(a, b, 1, 1), (a, b)


def _bmm_tn_bwd(res, g):
    a, b = res
    return _bdot(b, g, 2, 2).astype(a.dtype), _bdot(a, g, 2, 1).astype(b.dtype)


bmm_tn.defvjp(_bmm_tn_fwd, _bmm_tn_bwd)


ATT_PAIRS = ATT_HEADS // 2
PAIR_W = 2 * ATT_HDIM


def att_pairs(q, kp, kc, vp, vc, bias, has_prev):
    t, b, w = q.shape
    i = lax.broadcasted_iota(jnp.int32, (1, b, b), 1)
    j = lax.broadcasted_iota(jnp.int32, (1, b, b), 2)
    first = lax.broadcasted_iota(jnp.int32, (1, 1, w), 2) < ATT_HDIM
    scale = ATT_HDIM ** -0.5
    outs, lses = [], []
    for ab in range(2):
        qh = jnp.where(first if ab == 0 else jnp.logical_not(first), q, 0.0)
        sp = jnp.where(jnp.logical_and(j >= i, has_prev), bmm_nt(qh, kp) * scale + bias[:, ab, 0], -1e30)
        sc = jnp.where(j <= i, bmm_nt(qh, kc) * scale + bias[:, ab, 1], -1e30)
        m = lax.stop_gradient(jnp.maximum(jnp.max(sp, axis=2, keepdims=True), jnp.max(sc, axis=2, keepdims=True)))
        pp, pc = jnp.exp(sp - m), jnp.exp(sc - m)
        l = jnp.sum(pp, axis=2, keepdims=True) + jnp.sum(pc, axis=2, keepdims=True)
        outs.append(bmm(pp / l, vp) + bmm(pc / l, vc))
        lses.append(jnp.broadcast_to(m + jnp.log(l), (t, b, w)))
    return jnp.where(first, outs[0], outs[1]), jnp.where(first, lses[0], lses[1])


def _pair_tiles(ref):
    return jnp.stack([ref[:, PAIR_W * t:PAIR_W * (t + 1)] for t in range(ATT_PAIRS)])


def _store_pair_tiles(ref, val):
    for t in range(ATT_PAIRS):
        ref[:, PAIR_W * t:PAIR_W * (t + 1)] = val[t].astype(ref.dtype)


def pair_bias(bias):
    return bias.reshape(ATT_PAIRS, 2, 2, ATT_BLK, ATT_BLK)


def att2_fwd(q, k, v, bias, nb, cols, *, name, side=None):
    n_blocks = S // ATT_BLK
    qc, kc, vc = cols

    def body(q_ref, k_ref, v_ref, b_ref, o_ref, l_ref, kprev, vprev):
        blk = pl.program_id(0)

        @pl.when(blk == 0)
        def _():
            kprev[...] = jnp.zeros_like(kprev)
            vprev[...] = jnp.zeros_like(vprev)

        k3, v3 = _pair_tiles(k_ref), _pair_tiles(v_ref)
        o, lse = att_pairs(_pair_tiles(q_ref), kprev[...], k3, vprev[...], v3, b_ref[...], (blk % nb) != 0)
        _store_pair_tiles(o_ref, o)
        _store_pair_tiles(l_ref, lse)
        kprev[...] = k3
        vprev[...] = v3

    def spec(c):
        return pl.BlockSpec((ATT_BLK, D), lambda b: (b, c))

    return grid_call(
        body, (q, k, v, bias), name=name,
        out_shape=[jax.ShapeDtypeStruct((S, D), BF16), jax.ShapeDtypeStruct((S, D), F32)], grid=(n_blocks,),
        in_specs=[spec(qc), spec(kc), spec(vc), pl.BlockSpec(bias.shape, lambda b: (0, 0, 0, 0, 0))],
        out_specs=[spec(0), spec(0)],
        scratch_shapes=[pltpu.VMEM((ATT_PAIRS, ATT_BLK, PAIR_W), BF16), pltpu.VMEM((ATT_PAIRS, ATT_BLK, PAIR_W), BF16)],
        semantics=("arbitrary",), side=side)


def att2_bwd(q, k, v, bias, do, dlse, nb, cols, *, name, side=None):
    n_blocks = S // ATT_BLK
    qc, kc, vc = cols

    def body(q_ref, k_ref, v_ref, b_ref, do_ref, dl_ref, dq_ref, dk_ref, dv_ref, db_ref, kprev, vprev, dk_own, dv_own):
        blk = pl.program_id(0)

        @pl.when(blk == 0)
        def _():
            for r in (kprev, vprev, dk_own, dv_own, db_ref):
                r[...] = jnp.zeros_like(r)

        @pl.when(blk < n_blocks)
        def _():
            k3, v3 = _pair_tiles(k_ref), _pair_tiles(v_ref)
            ins = _f32([_pair_tiles(q_ref), kprev[...], k3, vprev[...], v3]) + [b_ref[...]]
            _, vjp = jax.vjp(functools.partial(att_pairs, has_prev=(blk % nb) != 0), *ins)
            dq, dkp, dkc, dvp, dvc, db = vjp(tuple(_f32([_pair_tiles(do_ref), _pair_tiles(dl_ref)])))
            _store_pair_tiles(dq_ref, dq)
            _store_pair_tiles(dk_ref, dk_own[...] + dkp)
            _store_pair_tiles(dv_ref, dv_own[...] + dvp)
            dk_own[...] = dkc
            dv_own[...] = dvc
            db_ref[...] += db
            kprev[...] = k3
            vprev[...] = v3

        @pl.when(blk == n_blocks)
        def _():
            _store_pair_tiles(dk_ref, dk_own[...])
            _store_pair_tiles(dv_ref, dv_own[...])

    def spec(c):
        return pl.BlockSpec((ATT_BLK, D), lambda b: (jnp.minimum(b, n_blocks - 1), c))

    late = pl.BlockSpec((ATT_BLK, D), lambda b: (jnp.maximum(b - 1, 0), 0))
    bspec = pl.BlockSpec(bias.shape, lambda b: (0, 0, 0, 0, 0))
    tile_f32 = pltpu.VMEM((ATT_PAIRS, ATT_BLK, PAIR_W), F32)
    tile_bf16 = pltpu.VMEM((ATT_PAIRS, ATT_BLK, PAIR_W), BF16)
    return grid_call(
        body, (q, k, v, bias, do, dlse), name=name,
        out_shape=[jax.ShapeDtypeStruct((S, D), BF16), jax.ShapeDtypeStruct((S, D), BF16),
                   jax.ShapeDtypeStruct((S, D), BF16), jax.ShapeDtypeStruct(bias.shape, F32)],
        grid=(n_blocks + 1,),
        in_specs=[spec(qc), spec(kc), spec(vc), bspec, spec(0), spec(0)],
        out_specs=[spec(0), late, late, bspec],
        scratch_shapes=[tile_bf16, tile_bf16, tile_f32, tile_f32],
        semantics=("arbitrary",), side=side)


def regroup(a, dil, inverse=False):
    if dil == 1:
        return a
    c_dim = a.shape[1]
    shape = (dil, S // dil, c_dim) if inverse else (S // dil, dil, c_dim)
    return jnp.transpose(a.reshape(shape), (1, 0, 2)).reshape(S, c_dim)


SSD_PAIRS = SSM_HEADS // 2
PAIRS_PER_GROUP = SSD_PAIRS // SSM_GROUPS
GROUP_W = HEADS_PER_GROUP * SSM_HDIM


SSD_GROUPS_PER_STEP = 4
SSD_STEP_PAIRS = PAIRS_PER_GROUP * SSD_GROUPS_PER_STEP
SSD_STEP_W = GROUP_W * SSD_GROUPS_PER_STEP


def ssd_pairs(x, dtraw, dt_bias, a_log, dskip, bms, cms, prev):
    t, q, w = x.shape
    n = bms[0].shape[1]
    per = t // len(bms)

    def by_pair(mats):
        return jnp.concatenate([jnp.broadcast_to(m[None], (per,) + m.shape) for m in mats], axis=0)
    li = lax.broadcasted_iota(jnp.int32, (1, q, q), 1)
    si = lax.broadcasted_iota(jnp.int32, (1, q, q), 2)
    first_lane = lax.broadcasted_iota(jnp.int32, (1, 1, w), 2) < SSM_HDIM
    first_row = lax.broadcasted_iota(jnp.int32, (1, w, 1), 1) < SSM_HDIM

    def to_col(row):
        return jnp.sum(jnp.where(li == si, jnp.broadcast_to(row, (t, q, q)), 0.0), axis=2, keepdims=True)

    def lanes(a0, a1):
        return jnp.where(first_lane, a0, a1)

    dt_col, acs_col, total, lmat = [], [], [], []
    for ab in range(2):
        dt_row = _softplus(dtraw[ab] + dt_bias[ab])
        a_row = dt_row * (-jnp.exp(a_log[ab]))
        a_col = to_col(a_row)
        acs_c = jnp.sum(jnp.where(si <= li, jnp.broadcast_to(a_row, (t, q, q)), 0.0), axis=2, keepdims=True)
        acs_r = jnp.sum(jnp.where(li <= si, jnp.broadcast_to(a_col, (t, q, q)), 0.0), axis=1, keepdims=True)
        dt_col.append(to_col(dt_row))
        acs_col.append(acs_c)
        total.append(jnp.sum(a_row, axis=2, keepdims=True))
        lmat.append(jnp.exp(jnp.where(li >= si, acs_c - acs_r, -1e30)))
    cb = by_pair([mm_nt(c_, b_) for c_, b_ in zip(cms, bms, strict=True)])
    bmb, cmb = by_pair(bms), by_pair(cms)
    xdt = x * lanes(dt_col[0], dt_col[1])
    y = lanes(bmm(cb * lmat[0], xdt), bmm(cb * lmat[1], xdt))
    y = y + bmm_nt(cmb, prev) * lanes(jnp.exp(acs_col[0]), jnp.exp(acs_col[1]))
    y = y + lanes(dskip[0], dskip[1]) * x
    state = bmm_tn(xdt * lanes(jnp.exp(total[0] - acs_col[0]), jnp.exp(total[1] - acs_col[1])), bmb)
    return y, jnp.where(first_row, jnp.exp(total[0]), jnp.exp(total[1])) * prev + state


def _group_tiles(ref):
    return jnp.stack([ref[:, PAIR_W * t:PAIR_W * (t + 1)] for t in range(SSD_STEP_PAIRS)])


def _store_group_tiles(ref, val):
    for t in range(SSD_STEP_PAIRS):
        ref[:, PAIR_W * t:PAIR_W * (t + 1)] = val[t]


def _bc_groups(ref):
    return tuple(ref[:, SSM_STATE * i:SSM_STATE * (i + 1)] for i in range(SSD_GROUPS_PER_STEP))


def _by_pair(a):
    return jnp.transpose(a.reshape(SSD_PAIRS, 2, 1, -1), (1, 0, 2, 3))


def _by_head(a):
    return jnp.transpose(a, (1, 0, 2, 3)).reshape(SSM_HEADS, -1)


def _ssd2_specs(chunk_of):
    tp = SSD_STEP_PAIRS
    xspec = pl.BlockSpec((CHUNK, SSD_STEP_W), lambda g, c: (chunk_of(c), g))
    tspec = pl.BlockSpec((2, tp, 1, CHUNK), lambda g, c: (0, g, 0, chunk_of(c)))
    hp = pl.BlockSpec((2, tp, 1, 1), lambda g, c: (0, g, 0, 0))
    gspec = pl.BlockSpec((CHUNK, SSD_GROUPS_PER_STEP * SSM_STATE), lambda g, c: (chunk_of(c), g))
    sspec = pl.BlockSpec((1, tp, PAIR_W, SSM_STATE), lambda g, c: (chunk_of(c), g, 0, 0))
    return xspec, tspec, hp, gspec, sspec


def ssd2_fwd(xs, dtraw_t, dt_bias, a_log, dskip, bm, cm, side=None):
    def body(x_ref, dt_ref, dtb_ref, al_ref, dk_ref, bm_ref, cm_ref, y_ref, prev_ref, state_ref):
        @pl.when(pl.program_id(1) == 0)
        def _():
            state_ref[...] = jnp.zeros_like(state_ref)

        prev = state_ref[...]
        prev_ref[0] = prev
        y, nxt = ssd_pairs(_group_tiles(x_ref), dt_ref[...], dtb_ref[...], al_ref[...], dk_ref[...], _bc_groups(bm_ref),
                           _bc_groups(cm_ref), prev)
        _store_group_tiles(y_ref, y)
        state_ref[...] = nxt

    xspec, tspec, hp, gspec, sspec = _ssd2_specs(lambda c: c)
    return grid_call(
        body, (xs, _by_pair(dtraw_t), _by_pair(dt_bias), _by_pair(a_log), _by_pair(dskip), bm, cm), name="ssd_fwd",
        out_shape=[jax.ShapeDtypeStruct((S, SSM_INNER), F32),
                   jax.ShapeDtypeStruct((N_CHUNKS, SSD_PAIRS, PAIR_W, SSM_STATE), F32)],
        grid=(SSM_GROUPS // SSD_GROUPS_PER_STEP, N_CHUNKS), in_specs=[xspec, tspec, hp, hp, hp, gspec, gspec],
        out_specs=[xspec, sspec],
        scratch_shapes=[pltpu.VMEM((SSD_STEP_PAIRS, PAIR_W, SSM_STATE), F32)],
        semantics=("parallel", "arbitrary"), side=side)


def ssd2_bwd(xs, dtraw_t, dt_bias, a_log, dskip, bm, cm, prev_all, dy, side=None):
    def body(x_ref, dt_ref, dtb_ref, al_ref, dk_ref, bm_ref, cm_ref, prev_ref, dy_ref,
             dx_ref, ddt_ref, ddtb_ref, dal_ref, ddk_ref, dbm_ref, dcm_ref, dstate_ref):
        @pl.when(pl.program_id(1) == 0)
        def _():
            for r in (dstate_ref, ddtb_ref, dal_ref, ddk_ref):
                r[...] = jnp.zeros_like(r)

        _, vjp = jax.vjp(ssd_pairs, _group_tiles(x_ref), dt_ref[...], dtb_ref[...], al_ref[...], dk_ref[...],
                         _bc_groups(bm_ref), _bc_groups(cm_ref), prev_ref[0])
        dx, ddt, ddtb, dal, ddk, dbms, dcms, dprev = vjp((_group_tiles(dy_ref), dstate_ref[...]))
        _store_group_tiles(dx_ref, dx)
        ddt_ref[...] = ddt
        ddtb_ref[...] += ddtb
        dal_ref[...] += dal
        ddk_ref[...] += ddk
        for i in range(SSD_GROUPS_PER_STEP):
            dbm_ref[:, SSM_STATE * i:SSM_STATE * (i + 1)] = dbms[i]
            dcm_ref[:, SSM_STATE * i:SSM_STATE * (i + 1)] = dcms[i]
        dstate_ref[...] = dprev

    xspec, tspec, hp, gspec, sspec = _ssd2_specs(lambda c: N_CHUNKS - 1 - c)
    par = jax.ShapeDtypeStruct((2, SSD_PAIRS, 1, 1), F32)
    res, side_dst = grid_call(
        body, (xs, _by_pair(dtraw_t), _by_pair(dt_bias), _by_pair(a_log), _by_pair(dskip), bm, cm, prev_all, dy),
        name="ssd_bwd",
        out_shape=[jax.ShapeDtypeStruct((S, SSM_INNER), F32), jax.ShapeDtypeStruct((2, SSD_PAIRS, 1, S), F32), par, par, par,
                   jax.ShapeDtypeStruct((S, SSM_GROUPS * SSM_STATE), F32),
                   jax.ShapeDtypeStruct((S, SSM_GROUPS * SSM_STATE), F32)],
        grid=(SSM_GROUPS // SSD_GROUPS_PER_STEP, N_CHUNKS), in_specs=[xspec, tspec, hp, hp, hp, gspec, gspec, sspec, xspec],
        out_specs=[xspec, tspec, hp, hp, hp, gspec, gspec],
        scratch_shapes=[pltpu.VMEM((SSD_STEP_PAIRS, PAIR_W, SSM_STATE), F32)],
        semantics=("parallel", "arbitrary"), side=side)
    return [res[0]] + [_by_head(r) for r in res[1:5]] + list(res[5:]), side_dst


def _silu(x):
    return x * jax.nn.sigmoid(x)


def _rms(x):
    return x * lax.rsqrt(jnp.mean(x * x, -1, keepdims=True) + EPS)


def f_normmod(x, g, sc, sh):
    return (_rms(x) * g * (1.0 + sc) + sh,)


def f_resid(x, mix, gate):
    return (x + gate * mix,)


def f_resid_bias(x, mix, gate, b):
    return (x + gate * (mix + b),)


def f_swiglu(hgu):
    return (_silu(hgu[:, :FFN_HIDDEN]) * hgu[:, FFN_HIDDEN:],)


def f_silu(x):
    return (_silu(x),)


def f_silu_xbc(x):
    y = _silu(x)
    n_b = SSM_GROUPS * SSM_STATE
    return y[:, :SSM_INNER], y[:, SSM_INNER:SSM_INNER + n_b], y[:, SSM_INNER + n_b:]


def f_gated_norm(y, z, g):
    return (_rms(y * _silu(z)) * g,)


def f_glu(y, b):
    y = y + b
    return (y[:, :D] * jax.nn.sigmoid(y[:, D:]),)


def f_ln_silu(u, g, b):
    mu = jnp.mean(u, -1, keepdims=True)
    var = jnp.mean(jnp.square(u - mu), -1, keepdims=True)
    return (_silu((u - mu) * lax.rsqrt(var + EPS) * g + b),)


def f_combine(o1, o2, o3, l1, l2, l3):
    m = lax.stop_gradient(jnp.maximum(jnp.maximum(l1, l2), l3))
    e1, e2, e3 = jnp.exp(l1 - m), jnp.exp(l2 - m), jnp.exp(l3 - m)
    return ((e1 * o1 + e2 * o2 + e3 * o3) / (e1 + e2 + e3),)


def f_head(x, tgt, g):
    return (0.5 * jnp.mean(jnp.square(_rms(x) * g - tgt), -1, keepdims=True),)


def f_sum3(a, b, c):
    return (a + b + c,)


def f_add(a, b):
    return (a + b,)


def f_adamw(w, g, m, v):
    m = ADAM_B1 * m + (1.0 - ADAM_B1) * g
    v = ADAM_B2 * v + (1.0 - ADAM_B2) * jnp.square(g)
    m_hat = m / (1.0 - ADAM_B1 ** ADAM_STEP)
    v_hat = v / (1.0 - ADAM_B2 ** ADAM_STEP)
    return -ADAM_LR * (m_hat / (jnp.sqrt(v_hat) + ADAM_EPS) + ADAM_WD * w), m, v


def _rows_tile(r, cap=256):
    return _pick(r, cap, mult=8)


def adamw(w, g, m, v, *, name):
    l_dim, r_dim, c_dim = w.shape
    tr = _rows_tile(r_dim, cap=256)

    def body(w_ref, g_ref, m_ref, v_ref, d_ref, mo_ref, vo_ref):
        d_ref[...], mo_ref[...], vo_ref[...] = f_adamw(w_ref[...], g_ref[...], m_ref[...], v_ref[...])

    spec = pl.BlockSpec((1, tr, c_dim), lambda l, i: (l, i, 0))
    return pl.pallas_call(
        body, name=name, out_shape=[jax.ShapeDtypeStruct(w.shape, F32)] * 3, grid=(l_dim, r_dim // tr),
        in_specs=[spec] * 4, out_specs=[spec] * 3, compiler_params=_cparams("parallel", "parallel"),
    )(w, g, m, v)


def _t5_bucket(dist):
    max_exact = REL_BUCKETS // 2
    n = jnp.maximum(dist, 1).astype(F32)
    large = max_exact + jnp.log(n / max_exact) / math.log(REL_MAX_DIST / max_exact) * (REL_BUCKETS - max_exact)
    large = jnp.minimum(large.astype(jnp.int32), REL_BUCKETS - 1)
    return jnp.where(dist < max_exact, dist, large)


def _att_buckets(dil):
    i = jnp.arange(ATT_BLK)[:, None]
    j = jnp.arange(2 * ATT_BLK)[None, :]
    bkt = _t5_bucket(jnp.maximum(ATT_BLK + i - j, 0) * dil)
    return jnp.transpose(bkt.reshape(ATT_BLK, 2, ATT_BLK), (1, 0, 2))


def att_bias(rel_table, p, dil):
    tab = rel_table[:, p * ATT_HEADS:(p + 1) * ATT_HEADS]
    onehot = (jnp.arange(REL_BUCKETS)[:, None] == _att_buckets(dil).reshape(1, -1)).astype(F32)
    bias = lax.dot_general(tab, onehot, (((0,), (0,)), ((), ())), precision=lax.Precision.HIGHEST)
    return bias.reshape(ATT_HEADS, 2, ATT_BLK, ATT_BLK)


def att_bias_grad(dbias, dil, *, name):
    onehot = (_att_buckets(dil).reshape(-1, 1) == jnp.arange(LANES)[None, :]).astype(BF16)
    dtab = matmul(dbias.reshape(ATT_HEADS, -1), onehot, mode="nn", out_dtype=F32, name=name, tk_cap=2048)
    return dtab[:, :REL_BUCKETS].T


HY_Z, HY_XBC, HY_DT, HY_Q, HY_K, HY_V = 2048, 3072, 32, 3072, 1024, 1024
HY_IN = HY_Z + HY_XBC + HY_DT + HY_Q + HY_K + HY_V
OFF_Z, OFF_XBC, OFF_Q, OFF_KV, OFF_DT = 0, 2048, 5120, 8192, 10240
HY_CAT = OFF_DT + LANES
DT_PAD = LANES


def hy_to_cat(w):
    z, xbc, dt, qkv = w[:2048], w[2048:5120], w[5120:5152], w[5152:]
    return jnp.concatenate([z, xbc, qkv, dt, jnp.zeros((DT_PAD - HY_DT,) + w.shape[1:], w.dtype)], axis=0)


def hy_from_cat(w, axis=0):
    part = lambda a, b: lax.slice_in_dim(w, a, b, axis=axis)
    return jnp.concatenate([part(0, 5120), part(OFF_DT, OFF_DT + HY_DT), part(5120, OFF_DT)], axis=axis)


def device_step(x, tgt, mods, wts, sp, comm=None):
    g = {}
    dmods = [[None] * 6 for _ in range(2)]
    wts = dict(wts)

    def wgrad(tokens_d, tokens_n, nm):
        return matmul(transpose(tokens_d, name=nm + "_t"), tokens_n, mode="nn", out_dtype=BF16, name=nm, out_t=True,
                      tk_cap=2048)

    def w_side(i):
        return None if comm is None else GatherRows(comm["pack"], comm["full"], *W_BATCHES[i])

    def g_side(i):
        return None if comm is None else ScatterRows(comm["ga"], comm["recv"], *G_BATCHES[i])

    def normmod(xi, gain, sc, sh, nm):
        return rowmap(f_normmod, [xi], [gain, sc, sh], [BF16], name=nm)[0]

    def ffn_fwd(xi, i, gate, nm):
        h = normmod(xi, sp["norm_ffn_g"][i], mods[i][4], mods[i][3], nm + "_norm")
        hgu = matmul(h, wts["gu_t"][i], mode="nt", out_dtype=BF16, name=nm + "_gu")
        out = matmul_swiglu(hgu, wts["down"][i], name=nm + "_down")
        xo = rowmap(f_resid, [xi, out], [gate], [F32], name=nm + "_res")[0]
        return xo, (h, hgu, out)

    def ffn_bwd(dres, xi, i, saved, nm):
        h, hgu, out = saved
        (dout,), (dgate,), _ = rowmap_bwd(f_resid, [xi, out], [mods[i][5]], [dres], name=nm + "_res_b",
                                          row_grad=[False, True], row_dtypes=[BF16])
        dmods[i][5] = dgate
        dact = matmul(dout, wts["down"][i], mode="nt", out_dtype=BF16, name=nm + "_down_dx")
        (dhgu,), _, (act,) = rowmap_bwd(f_swiglu, [hgu], [], [dact], name=nm + "_act_b", row_grad=[True],
                                        row_dtypes=[BF16], tr=256, emit=(0,), emit_dtype=BF16)
        g[f"down{i}"] = wgrad(dout, act, nm + "_down_dw")
        g[f"gu_t{i}"] = wgrad(h, dhgu, nm + "_gu_dw")
        dh = matmul(dhgu, wts["gu_t"][i], mode="nn", out_dtype=F32, name=nm + "_gu_dx")
        (dres,), (dg_, dsc, dsh), _ = rowmap_bwd(f_normmod, [xi], [sp["norm_ffn_g"][i], mods[i][4], mods[i][3]], [dh],
                                                 name=nm + "_norm_b", row_grad=[True], row_add=[dres])
        g[f"norm_ffn_g{i}"] = dg_
        dmods[i][4], dmods[i][3] = dsc, dsh
        return dres

    h0 = normmod(x, sp["norm_mix_g"][0], mods[0][1], mods[0][0], "l0_norm")
    w_in = wts["hy_in_t"]
    z = matmul(h0, w_in, mode="nt", out_dtype=BF16, name="hy_z", n=HY_Z, b_off=OFF_Z)
    xbc_raw = matmul(h0, w_in, mode="nt", out_dtype=BF16, name="hy_xbc", n=HY_XBC, b_off=OFF_XBC)
    q = matmul(h0, w_in, mode="nt", out_dtype=BF16, name="hy_q", n=HY_Q, b_off=OFF_Q)
    kv = matmul(h0, w_in, mode="nt", out_dtype=BF16, name="hy_kv", n=HY_K + HY_V, b_off=OFF_KV)
    dtr = matmul(h0, w_in, mode="nt", out_dtype=F32, name="hy_dt", n=DT_PAD, b_off=OFF_DT)
    xbc_pre = conv_fwd(xbc_raw, sp["hy_conv_w"], sp["hy_conv_b"], name="hy_conv")
    xs, bm, cm = rowmap(f_silu_xbc, [xbc_pre], [], [F32] * 3, name="hy_conv_act", tr=256)
    dtraw_t = dtr[:, :HY_DT].T
    (y, prev_all), full = ssd2_fwd(xs, dtraw_t, sp["hy_dt_bias"], sp["hy_a_log"], sp["hy_d_skip"], bm, cm, side=w_side(1))
    if comm is not None:
        comm["full"] = full
    ysn = rowmap(f_gated_norm, [y, z], [sp["hy_ssm_norm_g"]], [BF16], name="hy_gnorm", tr=256)[0]
    att_in, att_o, att_l = [], [], []
    for p, (win, dil) in enumerate(ATT_PATTERNS):
        if dil == 1:
            qa, ka, va, cols = q, kv, kv, (p, 0, 1)
        else:
            qa, ka, cols = regroup(q[:, p * D:(p + 1) * D], dil), regroup(kv, dil), (0, 0, 1)
            va = ka
        bias = pair_bias(att_bias(sp["rel_table"], p, dil))
        nb = S // dil // ATT_BLK
        (o, lse), full = att2_fwd(qa, ka, va, bias, nb, cols, name=f"att_fwd{p}", side=w_side(2 + p))
        if comm is not None:
            comm["full"] = full
        att_in.append((qa, ka, va, bias, nb, cols))
        att_o.append(regroup(o, dil, inverse=True))
        att_l.append(regroup(lse, dil, inverse=True))
    if comm is not None:
        wts.update(unpack_weights(comm["full"], skip=("hy_in_t",)))
    att = rowmap(f_combine, att_o + att_l, [], [BF16], name="att_combine", tr=256)[0]
    cat = jnp.concatenate([ysn, att], axis=-1)
    mix0 = matmul(cat, wts["hy_out"], mode="nn", out_dtype=F32, name="hy_out")
    x1 = rowmap(f_resid, [x, mix0], [mods[0][2]], [F32], name="l0_res")[0]
    x2, ffn0 = ffn_fwd(x1, 0, mods[0][5], "ffn0")

    h1 = normmod(x2, sp["norm_mix_g"][1], mods[1][1], mods[1][0], "l1_norm")
    p1 = matmul(h1, wts["pw1_t"], mode="nt", out_dtype=BF16, name="cv_pw1")
    u = rowmap(f_glu, [p1], [sp["cv_b_pw1"]], [F32], name="cv_glu")[0]
    uc = conv_fwd(u, sp["cv_w_dw"], sp["cv_b_dw"], name="cv_conv")
    ul = rowmap(f_ln_silu, [uc], [sp["cv_ln_g"], sp["cv_ln_b"]], [BF16], name="cv_ln")[0]
    mix1 = matmul(ul, wts["pw2"], mode="nn", out_dtype=F32, name="cv_pw2")
    x3 = rowmap(f_resid_bias, [x2, mix1], [mods[1][2], sp["cv_b_pw2"]], [F32], name="l1_res")[0]
    x4, ffn1 = ffn_fwd(x3, 1, mods[1][5], "ffn1")

    ones = jnp.ones((S, 1), F32)
    (dres,), (dfinal,), (loss_rows,) = rowmap_bwd(f_head, [x4, tgt], [sp["final_norm_g"]], [ones], name="head",
                                                  row_grad=[True, False], emit=(0,))
    g["final_norm_g"] = dfinal

    dres = ffn_bwd(dres, x3, 1, ffn1, "ffn1")
    (dmix1,), (dg1, db2), _ = rowmap_bwd(f_resid_bias, [x2, mix1], [mods[1][2], sp["cv_b_pw2"]], [dres], name="l1_res_b",
                                         row_grad=[False, True], row_dtypes=[BF16])
    dmods[1][2] = dg1
    g["cv_b_pw2"] = db2
    dul = matmul(dmix1, wts["pw2"], mode="nt", out_dtype=BF16, name="cv_pw2_dx")
    g["pw2"] = wgrad(dmix1, ul, "cv_pw2_dw")
    (duc,), (g["cv_ln_g"], g["cv_ln_b"]), _ = rowmap_bwd(f_ln_silu, [uc], [sp["cv_ln_g"], sp["cv_ln_b"]], [dul],
                                                         name="cv_ln_b", row_grad=[True])
    du, g["cv_w_dw"], g["cv_b_dw"] = conv_bwd(u, sp["cv_w_dw"], duc, name="cv_conv_b", cb=128, chunk_rows=128)
    (dp1,), (g["cv_b_pw1"],), _ = rowmap_bwd(f_glu, [p1], [sp["cv_b_pw1"]], [du], name="cv_glu_b", row_grad=[True],
                                             row_dtypes=[BF16])
    g["pw1_t"] = wgrad(h1, dp1, "cv_pw1_dw")
    dh1 = matmul(dp1, wts["pw1_t"], mode="nn", out_dtype=F32, name="cv_pw1_dx")
    (dres,), (dg_, dsc, dsh), _ = rowmap_bwd(f_normmod, [x2], [sp["norm_mix_g"][1], mods[1][1], mods[1][0]], [dh1],
                                             name="l1_norm_b", row_grad=[True], row_add=[dres])
    g["norm_mix_g1"] = dg_
    dmods[1][1], dmods[1][0] = dsc, dsh

    dres = ffn_bwd(dres, x1, 0, ffn0, "ffn0")
    (dmix0,), (dg1,), _ = rowmap_bwd(f_resid, [x, mix0], [mods[0][2]], [dres], name="l0_res_b",
                                     row_grad=[False, True], row_dtypes=[BF16])
    dmods[0][2] = dg1
    dysn = matmul(dmix0, wts["hy_out"], mode="nt", out_dtype=BF16, name="hy_out_dy", n=SSM_INNER, b_off=0)
    datt = matmul(dmix0, wts["hy_out"], mode="nt", out_dtype=BF16, name="hy_out_da", n=D, b_off=SSM_INNER)
    g["hy_out"] = wgrad(dmix0, cat, "hy_out_dw")
    (dy, dz), (g["hy_ssm_norm_g"],), _ = rowmap_bwd(f_gated_norm, [y, z], [sp["hy_ssm_norm_g"]], [dysn], name="hy_gnorm_b",
                                                    row_grad=[True, True], row_dtypes=[F32, BF16], tr=256)
    if comm is not None:
        comm["ga"] = pack_grads(g, GA_LAYOUT, GA_ROWS)
        comm["recv"] = lax.empty((3, GA_ROWS, D), BF16)
    (dxs, ddtraw_t, g["hy_dt_bias"], g["hy_a_log"], g["hy_d_skip"], dbm, dcm), recv = ssd2_bwd(
        xs, dtraw_t, sp["hy_dt_bias"], sp["hy_a_log"], sp["hy_d_skip"], bm, cm, prev_all, dy, side=g_side(0))
    if comm is not None:
        comm["recv"] = recv
    (dxbc_pre,), _, _ = rowmap_bwd(f_silu_xbc, [xbc_pre], [], [dxs, dbm, dcm], name="hy_conv_act_b", row_grad=[True],
                                   tr=256)
    dxbc_raw, g["hy_conv_w"], g["hy_conv_b"] = conv_bwd(xbc_raw, sp["hy_conv_w"], dxbc_pre, name="hy_conv_b", cb=128, chunk_rows=128, dx_dtype=BF16)
    dol, _, _ = rowmap_bwd(f_combine, att_o + att_l, [], [datt], name="att_combine_b", row_grad=[True] * 6,
                           row_dtypes=[BF16] * 3 + [F32] * 3, tr=256)
    dqs, dks, dvs, dtabs = [], [], [], []
    for p, (win, dil) in enumerate(ATT_PATTERNS):
        qa, ka, va, bias, nb, cols = att_in[p]
        (dq, dkp_, dvp_, dbias), recv = att2_bwd(qa, ka, va, bias, regroup(dol[p], dil), regroup(dol[3 + p], dil), nb,
                                                 cols, name=f"att_bwd{p}", side=g_side(1 + p))
        if comm is not None:
            comm["recv"] = recv
        dqs.append(regroup(dq, dil, inverse=True))
        dks.append(regroup(dkp_, dil, inverse=True))
        dvs.append(regroup(dvp_, dil, inverse=True))
        dtabs.append(att_bias_grad(dbias.reshape(ATT_HEADS, 2, ATT_BLK, ATT_BLK), dil, name=f"att_dtab{p}"))
    g["rel_table"] = jnp.concatenate(dtabs, axis=1)
    dk = rowmap(f_sum3, dks, [], [BF16], name="att_dk_sum")[0]
    dv = rowmap(f_sum3, dvs, [], [BF16], name="att_dv_sum")[0]
    ddt = jnp.pad(ddtraw_t.T, ((0, 0), (0, DT_PAD - HY_DT)))
    dproj = jnp.concatenate([dz, dxbc_raw] + dqs + [dk, dv, ddt.astype(BF16)], axis=-1)
    g["hy_in_t"] = wgrad(h0, dproj, "hy_in_dw")
    if comm is None:
        dh0 = matmul(dproj, w_in, mode="nn", out_dtype=F32, name="hy_in_dx")
    else:
        gb = pack_grads(g, GB_LAYOUT, GB_ROWS)
        half = GB_ROWS // 2
        theirs = swap_halves(gb, name="swap_in_halves")
        ours = lax.dynamic_slice_in_dim(gb, lax.axis_index("c") * half, half, axis=1)
        comm["gb"] = rowmap(f_add, [ours.reshape(N_CHIPS * half, D), theirs.reshape(N_CHIPS * half, D)], [], [BF16],
                            name="sum_in_cores")[0].reshape(N_CHIPS, half, D)
        dh0, comm["recv_b"] = matmul(dproj, w_in, mode="nn", out_dtype=F32, name="hy_in_dx",
                                     side=ScatterRows(comm["gb"], lax.empty((3, half, D), BF16), 0, half))
    (dres,), (dg_, dsc, dsh), _ = rowmap_bwd(f_normmod, [x], [sp["norm_mix_g"][0], mods[0][1], mods[0][0]], [dh0],
                                             name="l0_norm_b", row_grad=[True], row_add=[dres])
    g["norm_mix_g0"] = dg_
    dmods[0][1], dmods[0][0] = dsc, dsh
    return loss_rows, dres, g, dmods


ANY = pl.BlockSpec(memory_space=pl.ANY)
WHOLE_VMEM = pl.BlockSpec(memory_space=pltpu.VMEM)


def _place():
    return lax.axis_index("x"), lax.axis_index("y"), lax.axis_index("c")


def _other_chips(x, y):
    return [(1 - x, y), (x, 1 - y), (1 - x, 1 - y)]


def allgather_small(v, *, name, side=None):
    m_per = v.shape[0]

    def gather(x_ref, out_ref, send_sems, recv_sems, local_sem):
        x, y, c = _place()
        me, sibling = (x, y, c), (x, y, 1 - c)
        chips = _other_chips(x, y)

        def rows(px, py, pc):
            return out_ref.at[pl.ds((4 * px + 2 * py + pc) * m_per, m_per), :]

        def copy(k, block, to, src=None):
            return pltpu.make_async_remote_copy(
                src_ref=rows(*block) if src is None else src, dst_ref=rows(*block),
                send_sem=send_sems.at[k], recv_sem=recv_sems.at[k], device_id=to, device_id_type=MESH)

        mine = pltpu.make_async_copy(x_ref, rows(*me), local_sem)
        mine.start()
        first = [copy(0, me, sibling, src=x_ref)]
        first += [copy(1 + j, me, (*chip, c), src=x_ref) for j, chip in enumerate(chips)]
        for cp in first:
            cp.start()
        passed = [copy(4 + j, (*chip, c), sibling) for j, chip in enumerate(chips)]
        for j, chip in enumerate(chips):
            copy(1 + j, (*chip, c), me).wait_recv()
            passed[j].start()
        copy(0, sibling, me).wait_recv()
        for j, chip in enumerate(chips):
            copy(4 + j, (*chip, 1 - c), me).wait_recv()
        for cp in first + passed:
            cp.wait_send()
        mine.wait()

    out = jax.ShapeDtypeStruct((N_DEV * m_per, LANES), v.dtype)
    sems = [pltpu.SemaphoreType.DMA((7,)), pltpu.SemaphoreType.DMA((7,)), pltpu.SemaphoreType.DMA]
    if side is None:
        return pl.pallas_call(gather, name=name, out_shape=out, in_specs=[WHOLE_VMEM], out_specs=WHOLE_VMEM,
                              scratch_shapes=sems)(v)

    def body(x_ref, src_ref, dst_in_ref, out_ref, dst_ref, send_sems, recv_sems, local_sem, *side_sems):
        side.start(src_ref, dst_ref, side_sems)
        gather(x_ref, out_ref, send_sems, recv_sems, local_sem)
        side.finish(src_ref, dst_ref, side_sems)

    return pl.pallas_call(
        body, name=name, out_shape=[out, jax.ShapeDtypeStruct(side.dst.shape, side.dst.dtype)],
        in_specs=[WHOLE_VMEM, ANY, ANY], out_specs=[WHOLE_VMEM, ANY], scratch_shapes=sems + side.sems(),
        input_output_aliases={2: 1},
    )(v, side.src, side.dst)


def swap_halves(gpack, *, name):
    half_rows = gpack.shape[1] // 2

    def body(g_ref, r_ref, send_sems, recv_sems):
        x, y, c = _place()
        its_half = pl.ds((1 - c) * half_rows, half_rows)
        copies = [pltpu.make_async_remote_copy(
            src_ref=g_ref.at[s, its_half], dst_ref=r_ref.at[s], send_sem=send_sems.at[s], recv_sem=recv_sems.at[s],
            device_id=(x, y, 1 - c), device_id_type=MESH) for s in range(N_CHIPS)]
        for cp in copies:
            cp.start()
        for cp in copies:
            cp.wait()

    return pl.pallas_call(
        body, name=name,
        out_shape=jax.ShapeDtypeStruct((N_CHIPS, half_rows) + gpack.shape[2:], gpack.dtype),
        in_specs=[ANY], out_specs=ANY,
        scratch_shapes=[pltpu.SemaphoreType.DMA((N_CHIPS,)), pltpu.SemaphoreType.DMA((N_CHIPS,))],
    )(gpack)


class GatherRows:
    def __init__(self, pack, full, lo, hi):
        assert (hi - lo) % 32 == 0 and lo % 16 == 0
        self.src, self.dst, self.lo, self.hi = pack, full, lo, hi

    def sems(self):
        return [pltpu.SemaphoreType.DMA((6,)), pltpu.SemaphoreType.DMA((6,)), pltpu.SemaphoreType.DMA]

    def _parts(self, pack_ref, full_ref, sems):
        send_sems, recv_sems, local_sem = sems
        x, y, c = _place()
        half = (self.hi - self.lo) // 2
        mine, its = pl.ds(self.lo + c * half, half), pl.ds(self.lo + (1 - c) * half, half)
        rows = pl.ds(self.lo, self.hi - self.lo)
        local = pltpu.make_async_copy(pack_ref.at[rows], full_ref.at[2 * x + y, rows], local_sem)
        chips = _other_chips(x, y)

        def remote(src, dst, k, to):
            return pltpu.make_async_remote_copy(src_ref=src, dst_ref=dst, send_sem=send_sems.at[k],
                                                recv_sem=recv_sems.at[k], device_id=to, device_id_type=MESH)

        sends = [remote(pack_ref.at[mine], full_ref.at[2 * x + y, mine], k, (cx, cy, c)) for k, (cx, cy) in enumerate(chips)]
        landed = [full_ref.at[2 * cx + cy, mine] for cx, cy in chips]
        arrive = [remote(pack_ref.at[mine], landed[k], k, (cx, cy, c)) for k, (cx, cy) in enumerate(chips)]
        passed = [remote(landed[k], landed[k], 3 + k, (x, y, 1 - c)) for k in range(3)]
        from_sibling = [remote(landed[k], full_ref.at[2 * cx + cy, its], 3 + k, (x, y, 1 - c))
                        for k, (cx, cy) in enumerate(chips)]
        return local, sends, arrive, passed, from_sibling

    def start(self, pack_ref, full_ref, sems):
        local, sends, _, _, _ = self._parts(pack_ref, full_ref, sems)
        local.start()
        for cp in sends:
            cp.start()

    def finish(self, pack_ref, full_ref, sems):
        local, sends, arrive, passed, from_sibling = self._parts(pack_ref, full_ref, sems)
        for k in range(3):
            arrive[k].wait_recv()
            passed[k].start()
        for cp in from_sibling:
            cp.wait_recv()
        for cp in sends + passed:
            cp.wait_send()
        local.wait()


class ScatterRows:
    def __init__(self, gpack, recv, lo, hi):
        assert lo % 16 == 0 and hi % 16 == 0
        self.src, self.dst, self.lo, self.hi = gpack, recv, lo, hi

    def sems(self):
        return [pltpu.SemaphoreType.DMA((3,)), pltpu.SemaphoreType.DMA((3,))]

    def _parts(self, g_ref, recv_ref, sems):
        send_sems, recv_sems = sems
        x, y, c = _place()
        rows = pl.ds(self.lo, self.hi - self.lo)
        return [pltpu.make_async_remote_copy(
            src_ref=g_ref.at[2 * cx + cy, rows], dst_ref=recv_ref.at[k, rows], send_sem=send_sems.at[k],
            recv_sem=recv_sems.at[k], device_id=(cx, cy, c), device_id_type=MESH)
            for k, (cx, cy) in enumerate(_other_chips(x, y))]

    def start(self, g_ref, recv_ref, sems):
        for cp in self._parts(g_ref, recv_ref, sems):
            cp.start()

    def finish(self, g_ref, recv_ref, sems):
        sends = self._parts(g_ref, recv_ref, sems)
        for cp in sends:
            cp.wait_recv()
        for cp in sends:
            cp.wait_send()


def side_call(side, *, name):
    def body(src_ref, dst_in_ref, dst_ref, *sems):
        side.start(src_ref, dst_ref, sems)
        side.finish(src_ref, dst_ref, sems)

    return pl.pallas_call(
        body, name=name, out_shape=jax.ShapeDtypeStruct(side.dst.shape, side.dst.dtype),
        in_specs=[ANY, ANY], out_specs=ANY, scratch_shapes=side.sems(), input_output_aliases={1: 0},
    )(side.src, side.dst)


def grid_call(body, args, *, name, out_shape, grid, in_specs, out_specs, scratch_shapes, semantics, side=None):
    if side is None:
        res = pl.pallas_call(body, name=name, out_shape=out_shape, grid=grid, in_specs=in_specs, out_specs=out_specs,
                             scratch_shapes=scratch_shapes, compiler_params=_cparams(*semantics))(*args)
        return res, None
    n_in, n_out, n_scr = len(args), len(out_shape), len(scratch_shapes)

    def wrapped(*refs):
        ins, (src_ref, _) = refs[:n_in], refs[n_in:n_in + 2]
        outs, dst_ref = refs[n_in + 2:n_in + 2 + n_out], refs[n_in + 2 + n_out]
        scr, sems = refs[n_in + 3 + n_out:n_in + 3 + n_out + n_scr], refs[n_in + 3 + n_out + n_scr:]
        first = functools.reduce(jnp.logical_and, [pl.program_id(i) == 0 for i in range(len(grid))])
        last = functools.reduce(jnp.logical_and, [pl.program_id(i) == n - 1 for i, n in enumerate(grid)])

        @pl.when(first)
        def _():
            side.start(src_ref, dst_ref, sems)

        body(*ins, *outs, *scr)

        @pl.when(last)
        def _():
            side.finish(src_ref, dst_ref, sems)

    res = pl.pallas_call(
        wrapped, name=name,
        out_shape=list(out_shape) + [jax.ShapeDtypeStruct(side.dst.shape, side.dst.dtype)],
        grid=grid, in_specs=list(in_specs) + [ANY, ANY], out_specs=list(out_specs) + [ANY],
        scratch_shapes=list(scratch_shapes) + side.sems(), input_output_aliases={n_in + 1: n_out},
        compiler_params=_cparams(*(["arbitrary"] * len(grid))),
    )(*args, side.src, side.dst)
    return res[:-1], res[-1]


def sibling_swap(p, *, name):
    def body(p_ref, r_ref, send_sem, recv_sem):
        x, y, c = _place()
        cp = pltpu.make_async_remote_copy(src_ref=p_ref, dst_ref=r_ref, send_sem=send_sem, recv_sem=recv_sem,
                                          device_id=(x, y, 1 - c), device_id_type=MESH)
        cp.start()
        cp.wait()

    return pl.pallas_call(
        body, name=name, out_shape=jax.ShapeDtypeStruct(p.shape, p.dtype),
        in_specs=[ANY], out_specs=ANY,
        scratch_shapes=[pltpu.SemaphoreType.DMA, pltpu.SemaphoreType.DMA],
    )(p)


def sum_slots(own, recv, *, name):
    r_dim, c_dim = own.shape
    tr = _pick(r_dim, 512, mult=16)

    def body(o_ref, r_ref, out_ref):
        acc = o_ref[...].astype(F32)
        for k in range(3):
            acc = acc + r_ref[k].astype(F32)
        out_ref[...] = acc

    return pl.pallas_call(
        body, name=name, out_shape=jax.ShapeDtypeStruct((r_dim, c_dim), F32), grid=(r_dim // tr,),
        in_specs=[pl.BlockSpec((tr, c_dim), lambda i: (i, 0)), pl.BlockSpec((3, tr, c_dim), lambda i: (0, i, 0))],
        out_specs=pl.BlockSpec((tr, c_dim), lambda i: (i, 0)),
        compiler_params=_cparams("parallel"),
    )(own, recv)


def sum_devices(v_all, *, name):
    m_per = v_all.shape[0] // N_DEV

    def body(v_ref, o_ref):
        acc = v_ref[pl.ds(0, m_per), :]
        for d in range(1, N_DEV):
            acc = acc + v_ref[pl.ds(d * m_per, m_per), :]
        o_ref[...] = acc

    return pl.pallas_call(
        body, name=name, out_shape=jax.ShapeDtypeStruct((m_per, LANES), F32),
        in_specs=[WHOLE_VMEM], out_specs=WHOLE_VMEM,
    )(v_all)


WEIGHTS = ['ada_w', 'ada_b', 'norm_mix_g', 'norm_ffn_g', 'hy_w_in', 'hy_conv_w', 'hy_conv_b', 'hy_dt_bias', 'hy_a_log',
           'hy_d_skip', 'hy_ssm_norm_g', 'hy_w_out', 'rel_table', 'cv_w_pw1', 'cv_b_pw1', 'cv_w_dw', 'cv_b_dw', 'cv_ln_g',
           'cv_ln_b', 'cv_w_pw2', 'cv_b_pw2', 'ffn_w_gate', 'ffn_w_up', 'ffn_w_down', 'final_norm_g']
BIG = ('ada_w', 'hy_w_in', 'hy_w_out', 'cv_w_pw1', 'cv_w_pw2', 'ffn_w_gate', 'ffn_w_up', 'ffn_w_down')
SMALL_SHARDED = {'hy_conv_w': (1, 4, 3072), 'cv_b_pw1': (1, 2048), 'cv_w_dw': (1, 31, 1024), 'cv_b_dw': (1, 1024),
                 'cv_ln_g': (1, 1024), 'cv_ln_b': (1, 1024), 'cv_b_pw2': (1, 1024)}
SMALL_GRADS = {'ada_b': (2, 6144), 'norm_mix_g': (2, 1024), 'norm_ffn_g': (2, 1024), 'hy_conv_w': (1, 4, 3072),
               'hy_conv_b': (1, 3072), 'hy_dt_bias': (1, 32), 'hy_a_log': (1, 32), 'hy_d_skip': (1, 32),
               'hy_ssm_norm_g': (1, 2048), 'rel_table': (32, 48), 'cv_b_pw1': (1, 2048), 'cv_w_dw': (1, 31, 1024),
               'cv_b_dw': (1, 1024), 'cv_ln_g': (1, 1024), 'cv_ln_b': (1, 1024), 'cv_b_pw2': (1, 1024),
               'final_norm_g': (1024,), 'loss': (1,)}

PACK_LAYOUT = (('hy_in_t', 2568), ('hy_out', 768), ('pw1_t', 512), ('pw2', 256),
               ('gate_t0', 704), ('up_t0', 704), ('down0', 704), ('gate_t1', 704), ('up_t1', 704), ('down1', 704))
PACK_ROWS = 8448


def _pack_offsets(layout):
    off, out = 0, {}
    for nm, r in layout:
        out[nm] = (off, r)
        off += r
    return out


PACK_OFF = _pack_offsets(PACK_LAYOUT)
W_BATCHES = ((0, 2624), (2624, 4992), (4992, 6144), (6144, 7296), (7296, 8448))
GA_LAYOUT = PACK_LAYOUT[1:]
GA_ROWS = 5888
GA_OFF = _pack_offsets(GA_LAYOUT)
G_BATCHES = ((0, 2560), (2560, 3712), (3712, 4864), (4864, 5888))
GB_LAYOUT = PACK_LAYOUT[:1]
GB_ROWS = 2816


def pack_grads(g, layout, n_rows):
    def rows_bf16(nm):
        return g[nm]

    parts = []
    for key, r in layout:
        if key == 'hy_in_t':
            a = hy_from_cat(rows_bf16('hy_in_t'))
        elif key.startswith('gate_t'):
            a = rows_bf16('gu_t' + key[-1])[:FFN_HIDDEN]
        elif key.startswith('up_t'):
            a = rows_bf16('gu_t' + key[-1])[FFN_HIDDEN:]
        else:
            a = rows_bf16(key)
        parts.append(a.reshape(N_CHIPS, r, D))
    used = sum(r for _, r in layout)
    return jnp.concatenate(parts + [jnp.zeros((N_CHIPS, n_rows - used, D), BF16)], axis=1)


def unpack_weights(full, skip=()):
    def whole(nm):
        o, r = PACK_OFF[nm]
        return full[:, o:o + r].reshape(N_CHIPS * r, D)

    out = {"hy_out": whole('hy_out'), "pw1_t": whole('pw1_t'), "pw2": whole('pw2'),
           "gu_t": [jnp.concatenate([whole(f'gate_t{i}'), whole(f'up_t{i}')], axis=0) for i in range(2)],
           "down": [whole(f'down{i}') for i in range(2)]}
    if "hy_in_t" not in skip:
        out["hy_in_t"] = hy_to_cat(whole('hy_in_t'))
    return out


def _to_lanes(flat):
    n = flat.shape[0]
    m = -(-n // (8 * LANES)) * 8
    return jnp.pad(flat, (0, m * LANES - n)).reshape(m, LANES)


def _split(flat, shapes):
    out, off = {}, 0
    for nm, shp in shapes.items():
        n = int(np.prod(shp))
        out[nm] = flat[off:off + n].reshape(shp)
        off += n
    return out


def kernel(x, c, ada_w, ada_b, norm_mix_g, norm_ffn_g, hy_w_in, hy_conv_w, hy_conv_b, hy_dt_bias, hy_a_log, hy_d_skip, hy_ssm_norm_g, hy_w_out, rel_table, cv_w_pw1, cv_b_pw1, cv_w_dw, cv_b_dw, cv_ln_g, cv_ln_b, cv_w_pw2, cv_b_pw2, ffn_w_gate, ffn_w_up, ffn_w_down, final_norm_g, loss_target, m_ada_w, m_ada_b, m_norm_mix_g, m_norm_ffn_g, m_hy_w_in, m_hy_conv_w, m_hy_conv_b, m_hy_dt_bias, m_hy_a_log, m_hy_d_skip, m_hy_ssm_norm_g, m_hy_w_out, m_rel_table, m_cv_w_pw1, m_cv_b_pw1, m_cv_w_dw, m_cv_b_dw, m_cv_ln_g, m_cv_ln_b, m_cv_w_pw2, m_cv_b_pw2, m_ffn_w_gate, m_ffn_w_up, m_ffn_w_down, m_final_norm_g, v_ada_w, v_ada_b, v_norm_mix_g, v_norm_ffn_g, v_hy_w_in, v_hy_conv_w, v_hy_conv_b, v_hy_dt_bias, v_hy_a_log, v_hy_d_skip, v_hy_ssm_norm_g, v_hy_w_out, v_rel_table, v_cv_w_pw1, v_cv_b_pw1, v_cv_w_dw, v_cv_b_dw, v_cv_ln_g, v_cv_ln_b, v_cv_w_pw2, v_cv_b_pw2, v_ffn_w_gate, v_ffn_w_up, v_ffn_w_down, v_final_norm_g):
    args = (x, c, ada_w, ada_b, norm_mix_g, norm_ffn_g, hy_w_in, hy_conv_w, hy_conv_b, hy_dt_bias, hy_a_log, hy_d_skip, hy_ssm_norm_g, hy_w_out, rel_table, cv_w_pw1, cv_b_pw1, cv_w_dw, cv_b_dw, cv_ln_g, cv_ln_b, cv_w_pw2, cv_b_pw2, ffn_w_gate, ffn_w_up, ffn_w_down, final_norm_g, loss_target, m_ada_w, m_ada_b, m_norm_mix_g, m_norm_ffn_g, m_hy_w_in, m_hy_conv_w, m_hy_conv_b, m_hy_dt_bias, m_hy_a_log, m_hy_d_skip, m_hy_ssm_norm_g, m_hy_w_out, m_rel_table, m_cv_w_pw1, m_cv_b_pw1, m_cv_w_dw, m_cv_b_dw, m_cv_ln_g, m_cv_ln_b, m_cv_w_pw2, m_cv_b_pw2, m_ffn_w_gate, m_ffn_w_up, m_ffn_w_down, m_final_norm_g, v_ada_w, v_ada_b, v_norm_mix_g, v_norm_ffn_g, v_hy_w_in, v_hy_conv_w, v_hy_conv_b, v_hy_dt_bias, v_hy_a_log, v_hy_d_skip, v_hy_ssm_norm_g, v_hy_w_out, v_rel_table, v_cv_w_pw1, v_cv_b_pw1, v_cv_w_dw, v_cv_b_dw, v_cv_ln_g, v_cv_ln_b, v_cv_w_pw2, v_cv_b_pw2, v_ffn_w_gate, v_ffn_w_up, v_ffn_w_down, v_final_norm_g)
    x_in, c_in = args[0], args[1]
    w = dict(zip(WEIGHTS, args[2:27], strict=True))
    tgt = args[27]
    m_in = dict(zip(WEIGHTS, args[28:53], strict=True))
    v_in = dict(zip(WEIGHTS, args[53:78], strict=True))
    xi, yi, ci = _place()
    chip = 2 * xi + yi
    dev = 2 * chip + ci

    cs = rowmap(f_silu, [c_in.reshape(8, LANES)], [], [F32], name="cond_silu", tr=8)[0]
    cs_all = allgather_small(cs, name="gather_cond").reshape(N_DEV, D)
    cs16 = jnp.pad(cs_all, ((0, 8), (0, 0)))
    modpart = jnp.stack([matmul(cs16, w['ada_w'][i], mode="nn", out_dtype=F32, name=f"ada_fwd{i}")[:N_DEV]
                         for i in range(2)], axis=1)
    def rows_of(nm, i=None):
        a = w[nm][0 if i is None else i]
        return (a.T if nm in ('hy_w_in', 'cv_w_pw1', 'ffn_w_gate', 'ffn_w_up') else a).astype(BF16)

    pieces = [rows_of('hy_w_in'), rows_of('hy_w_out'), rows_of('cv_w_pw1'), rows_of('cv_w_pw2')]
    for i in range(2):
        pieces += [rows_of('ffn_w_gate', i), rows_of('ffn_w_up', i), rows_of('ffn_w_down', i)]
    n_rows = sum(p.shape[0] for p in pieces)
    pack = jnp.concatenate(pieces + [jnp.zeros((PACK_ROWS - n_rows, D), BF16)], axis=0)

    shard_names = list(SMALL_SHARDED)
    payload = jnp.concatenate([modpart.reshape(-1)] + [w[nm].reshape(-1) for nm in shard_names])
    got, full = allgather_small(_to_lanes(payload), name="gather_mod",
                                side=GatherRows(pack, lax.empty((N_CHIPS, PACK_ROWS, D), BF16), *W_BATCHES[0]))
    got = got.reshape(N_DEV, -1)[0::2]
    modparts = got[:, :modpart.size].reshape(N_CHIPS, N_DEV, 2, 1536)
    mine = lax.dynamic_index_in_dim(modparts, dev, axis=1, keepdims=False)
    mod = jnp.transpose(mine, (1, 0, 2)).reshape(2, 6 * D) + w['ada_b']
    mods = [[mod[i, j * D:(j + 1) * D].reshape(1, D) for j in range(6)] for i in range(2)]
    sp = {}
    off = modpart.size
    for nm in shard_names:
        shp = w[nm].shape
        n = int(np.prod(shp))
        parts = got[:, off:off + n].reshape((N_CHIPS,) + shp)
        sp[nm] = jnp.concatenate([parts[s] for s in range(N_CHIPS)], axis=-1)
        off += n

    o_in, r_in = PACK_OFF['hy_in_t']
    wts = {"hy_in_t": hy_to_cat(full[:, o_in:o_in + r_in].reshape(N_CHIPS * r_in, D))}
    comm = {"pack": pack, "full": full}

    sp = {"norm_mix_g": [w['norm_mix_g'][i].reshape(1, D) for i in range(2)],
          "norm_ffn_g": [w['norm_ffn_g'][i].reshape(1, D) for i in range(2)],
          "hy_conv_w": sp['hy_conv_w'][0], "hy_conv_b": w['hy_conv_b'],
          "hy_dt_bias": w['hy_dt_bias'].reshape(SSM_HEADS, 1), "hy_a_log": w['hy_a_log'].reshape(SSM_HEADS, 1),
          "hy_d_skip": w['hy_d_skip'].reshape(SSM_HEADS, 1), "hy_ssm_norm_g": w['hy_ssm_norm_g'],
          "rel_table": w['rel_table'], "cv_b_pw1": sp['cv_b_pw1'], "cv_w_dw": sp['cv_w_dw'][0], "cv_b_dw": sp['cv_b_dw'],
          "cv_ln_g": sp['cv_ln_g'], "cv_ln_b": sp['cv_ln_b'], "cv_b_pw2": sp['cv_b_pw2'],
          "final_norm_g": w['final_norm_g'].reshape(1, D)}

    loss_rows, grad_x, g, dmods = device_step(x_in[0], tgt[0], mods, wts, sp, comm)

    dmod = jnp.stack([jnp.concatenate([d.reshape(-1) for d in dmods[i]]) for i in range(2)])
    small = {'ada_b': dmod, 'norm_mix_g': jnp.stack([g[f'norm_mix_g{i}'].reshape(-1) for i in range(2)]),
             'norm_ffn_g': jnp.stack([g[f'norm_ffn_g{i}'].reshape(-1) for i in range(2)]),
             'loss': jnp.sum(loss_rows).reshape(1)}
    for nm in SMALL_GRADS:
        if nm not in small:
            small[nm] = g[nm]
    vec = _to_lanes(jnp.concatenate([small[nm].reshape(-1) for nm in SMALL_GRADS]))
    vec_all = allgather_small(vec, name="gather_small_grads")
    tot = _split(sum_devices(vec_all, name="sum_small_grads").reshape(-1), SMALL_GRADS)
    dmod_all = vec_all.reshape(N_DEV, -1)[:, :2 * 6 * D].reshape(N_DEV, 2, 6 * D)

    recv = comm["recv"]
    own_a = lax.dynamic_index_in_dim(comm["ga"], chip, axis=0, keepdims=False)
    part_a = sum_slots(own_a, recv, name="sum_chip_grads")
    red_a = rowmap(f_add, [part_a, sibling_swap(part_a, name="swap_grads")], [], [F32], name="sum_core_grads")[0]
    recv_b = comm["recv_b"]
    own_b = lax.dynamic_index_in_dim(comm["gb"], chip, axis=0, keepdims=False)
    mine_half = sum_slots(own_b, recv_b, name="sum_in_chips")
    its_half = sibling_swap(mine_half, name="swap_in")
    red_b = jnp.concatenate([jnp.where(ci == 0, mine_half, its_half), jnp.where(ci == 0, its_half, mine_half)], axis=0)

    def shard_grad(nm, i=None):
        key = {'hy_w_in': 'hy_in_t', 'hy_w_out': 'hy_out', 'cv_w_pw1': 'pw1_t', 'cv_w_pw2': 'pw2'}.get(nm)
        if key is None:
            key = {'ffn_w_gate': 'gate_t', 'ffn_w_up': 'up_t', 'ffn_w_down': 'down'}[nm] + str(i)
        if key == 'hy_in_t':
            a = red_b[:PACK_OFF[key][1]]
        else:
            o, r = GA_OFF[key]
            a = red_a[o:o + r]
        return a.T if key.endswith('_t') or key[:-1].endswith('_t') else a

    grads = {}
    grads['hy_w_in'] = shard_grad('hy_w_in')[None]
    grads['hy_w_out'] = shard_grad('hy_w_out')[None]
    grads['cv_w_pw1'] = shard_grad('cv_w_pw1')[None]
    grads['cv_w_pw2'] = shard_grad('cv_w_pw2')[None]
    for nm in ('ffn_w_gate', 'ffn_w_up', 'ffn_w_down'):
        grads[nm] = jnp.stack([shard_grad(nm, i) for i in range(2)])
    cs16 = jnp.pad(cs_all, ((0, 8), (0, 0)))
    dm_mine = lax.dynamic_slice_in_dim(dmod_all, chip * 1536, 1536, axis=2)
    dm16 = jnp.pad(dm_mine, ((0, 8), (0, 0), (0, 0)))
    grads['ada_w'] = jnp.stack([matmul(cs16, dm16[:, i], mode="tn", out_dtype=F32, name=f"ada_dw{i}") for i in range(2)])
    for nm, shp in SMALL_GRADS.items():
        if nm == 'loss':
            continue
        if nm in SMALL_SHARDED:
            n = w[nm].shape[-1]
            grads[nm] = lax.dynamic_slice_in_dim(tot[nm], chip * n, n, axis=len(shp) - 1)
        else:
            grads[nm] = tot[nm].reshape(w[nm].shape)

    delta, new_m, new_v = {}, {}, {}
    for nm in BIG:
        delta[nm], new_m[nm], new_v[nm] = adamw(w[nm], grads[nm], m_in[nm], v_in[nm], name="adamw_" + nm)
    smalls = [nm for nm in WEIGHTS if nm not in BIG]
    packed = [_to_lanes(jnp.concatenate([d[nm].reshape(-1) for nm in smalls])) for d in (w, grads, m_in, v_in)]
    res = rowmap(f_adamw, packed, [], [F32] * 3, name="adamw_small", tr=_rows_tile(packed[0].shape[0]))
    for d, r in zip((delta, new_m, new_v), res, strict=True):
        d.update(_split(r.reshape(-1), {nm: w[nm].shape for nm in smalls}))

    loss = tot['loss'].reshape(())
    return (loss, grad_x[None], *[grads[nm] for nm in WEIGHTS], *[delta[nm] for nm in WEIGHTS],
            *[new_m[nm] for nm in WEIGHTS], *[new_v[nm] for nm in WEIGHTS])
```
